```python
import math
import jax
import jax.numpy as jnp
from jax import lax
import numpy as np

D_MODEL = 1024
BATCH = 4
SEQ = 4096
DEPTH = 2

GRID_W = 64
CTX_LEN = 256
EPS = 1e-6
NEG_INF = -1e30
CONV_W = 4
F32 = jnp.float32

LRU_WIDTH = D_MODEL // 2
LRU_BLOCKS = 8
LRU_BLOCK = LRU_WIDTH // LRU_BLOCKS
LRU_C = 8.0
NA_HEADS = 8
NA_HEAD_DIM = (D_MODEL // 2) // NA_HEADS
NA_WIDTH = NA_HEADS * NA_HEAD_DIM
NA_WIN_R = 8
NA_WIN_C = 16
EVEN_SPLITS = (LRU_WIDTH, LRU_WIDTH, NA_WIDTH, NA_WIDTH, NA_WIDTH)
EVEN_IN = sum(EVEN_SPLITS)
EVEN_MIX = LRU_WIDTH + NA_WIDTH

SSD_HEADS = 8
SSD_HEAD_DIM = 64
SSD_WIDTH = SSD_HEADS * SSD_HEAD_DIM
SSD_GROUPS = 2
SSD_STATE = 128
SSD_CHUNK = 128
SSD_XBC = SSD_WIDTH + 2 * SSD_GROUPS * SSD_STATE
ML_HEADS = 4
ML_HEAD_DIM = 128
ML_WIDTH = ML_HEADS * ML_HEAD_DIM
ML_CHUNK = 128
ODD_SPLITS = (SSD_WIDTH, SSD_XBC, 2 * SSD_HEADS, ML_WIDTH, ML_WIDTH, ML_WIDTH, ML_WIDTH, 4 * ML_HEADS)
ODD_IN = sum(ODD_SPLITS)
ODD_MIX = SSD_WIDTH + ML_WIDTH

MOE_GROUPS = 4
MOE_EXPERTS_PER_GROUP = 4
MOE_EXPERTS = MOE_GROUPS * MOE_EXPERTS_PER_GROUP
MOE_TOP_K = 2
MOE_FF = 512

N_EVEN = (DEPTH + 1) // 2
N_ODD = DEPTH // 2

kernel_name = 'hybrid_prefix_diffusion_trunk'


def rms_norm(x, g):
    xf = x.astype(F32)
    y = xf * lax.rsqrt(jnp.mean(xf * xf, axis=-1, keepdims=True) + EPS)
    return (y * g.astype(F32)).astype(x.dtype)


def _split(t, sizes):
    return jnp.split(t, np.cumsum(sizes)[:-1].tolist(), axis=-1)


def _flip(t):
    return jnp.flip(t, axis=1)


def dw_conv(x, w, b):
    K = w.shape[0]
    y = lax.conv_general_dilated(x, w[:, None, :], window_strides=(1,), padding=[(K // 2, K - 1 - K // 2)],
                                 dimension_numbers=('NWC', 'WIO', 'NWC'), feature_group_count=x.shape[-1])
    return y + b


def block_diag_linear(x, w, b):
    bsz, L, _ = x.shape
    nb, bs, _ = w.shape
    y = jnp.einsum('blnd,nde->blne', x.reshape(bsz, L, nb, bs), w)
    return y.reshape(bsz, L, nb * bs) + b


def linear_scan(a, u, h0):
    def combine(left, right):
        al, ul = left
        ar, ur = right
        return al * ar, ar * ul + ur
    a_cum, h = lax.associative_scan(combine, (a, u), axis=1)
    return a_cum * h0[:, None] + h


def rglru_scan(xs, wa, ba, wx, bx, lam, h0):
    r = jax.nn.sigmoid(block_diag_linear(xs, wa, ba).astype(F32))
    i = jax.nn.sigmoid(block_diag_linear(xs, wx, bx).astype(F32))
    log_a = -LRU_C * r * jax.nn.softplus(-lam.astype(F32))
    u = jnp.sqrt(-jnp.expm1(2.0 * log_a)) * (i * xs.astype(F32))
    return linear_scan(jnp.exp(log_a), u, h0)


def prefix_bidirectional(run, ctx_in, lat_in, init, need_ctx):
    yc_f, sc_f = run(ctx_in, 0, init)
    yl_f, _ = run(lat_in, 0, sc_f)
    yc_b, sc_b = run(tuple(_flip(t) for t in ctx_in), 1, init)
    yl_b, _ = run(tuple(_flip(t) for t in lat_in), 1, sc_b)
    y_ctx = yc_f + _flip(yc_b) if need_ctx else None
    return yl_f + _flip(yl_b), y_ctx


def neighbourhood_attention(q, k, v, qc, kc, vc, q_g, k_g, rpb, need_ctx):
    bsz, S, H, hd = q.shape
    rows = S // GRID_W
    kr = min(NA_WIN_R, rows)
    scale = hd ** -0.5
    q, k, kc = rms_norm(q, q_g), rms_norm(k, k_g), rms_norm(kc, k_g)
    r = jnp.arange(rows)
    row_idx = jnp.clip(r - kr // 2, 0, rows - kr)[:, None] + jnp.arange(kr)[None, :]
    col = jnp.arange(GRID_W)
    col_start = jnp.clip(col - NA_WIN_C // 2, 0, GRID_W - NA_WIN_C)
    col_in = (col[None, :] >= col_start[:, None]) & (col[None, :] < col_start[:, None] + NA_WIN_C)
    d_row = row_idx - r[:, None] + (NA_WIN_R - 1)
    d_col = jnp.clip(col[None, :] - col[:, None] + (NA_WIN_C - 1), 0, 2 * NA_WIN_C - 2)
    bias = rpb.astype(F32)[:, d_row[:, None, :, None], d_col[None, :, None, :]]
    bias = jnp.where(col_in[None, None, :, None, :], bias, NEG_INF)

    def grid(t):
        return t.reshape(bsz, rows, GRID_W, H, hd).transpose(0, 3, 1, 2, 4)
    qg = grid(q)
    kg = grid(k)[:, :, row_idx]
    vg = grid(v)[:, :, row_idx]
    kch = kc.transpose(0, 2, 1, 3)
    vch = vc.transpose(0, 2, 1, 3)
    s_win = jnp.einsum('bhrqd,bhrjkd->bhrqjk', qg, kg, preferred_element_type=F32) * scale + bias[None]
    s_ctx = jnp.einsum('bhrqd,bhcd->bhrqc', qg, kch, preferred_element_type=F32) * scale
    n_win = kr * GRID_W
    p = jax.nn.softmax(jnp.concatenate([s_win.reshape(bsz, H, rows, GRID_W, n_win), s_ctx], axis=-1), axis=-1)
    p = p.astype(v.dtype)
    o = (jnp.einsum('bhrqjk,bhrjkd->bhrqd', p[..., :n_win].reshape(s_win.shape), vg)
         + jnp.einsum('bhrqc,bhcd->bhrqd', p[..., n_win:], vch))
    y = o.transpose(0, 2, 3, 1, 4).reshape(bsz, S, H * hd)
    if not need_ctx:
        return y, None
    qch = rms_norm(qc, q_g).transpose(0, 2, 1, 3)
    pc = jax.nn.softmax(jnp.einsum('bhqd,bhkd->bhqk', qch, kch, preferred_element_type=F32) * scale, axis=-1)
    yc = jnp.einsum('bhqk,bhkd->bhqd', pc.astype(v.dtype), vch).transpose(0, 2, 1, 3)
    return y, yc.reshape(bsz, -1, H * hd)


def ssd_chunked(xh, dt, A, Bm, Cm, h0):
    bsz, L, H, P = xh.shape
    G, N = Bm.shape[2], Bm.shape[3]
    R = H // G
    Q = SSD_CHUNK
    nc = L // Q
    x = xh.astype(F32).reshape(bsz, nc, Q, G, R, P)
    dtc = dt.reshape(bsz, nc, Q, G, R)
    Bc = Bm.astype(F32).reshape(bsz, nc, Q, G, N)
    Cc = Cm.astype(F32).reshape(bsz, nc, Q, G, N)
    dtx = x * dtc[..., None]
    cum = jnp.cumsum(dtc * A.reshape(G, R), axis=2)
    causal = jnp.tril(jnp.ones((Q, Q), bool))
    seg = cum[:, :, :, None] - cum[:, :, None, :]
    decay = jnp.exp(jnp.where(causal[None, None, :, :, None, None], seg, NEG_INF))
    cb = jnp.einsum('bcign,bcjgn->bcijg', Cc, Bc)
    y_diag = jnp.einsum('bcijg,bcijgr,bcjgrp->bcigrp', cb, decay, dtx)
    to_end = jnp.exp(cum[:, :, -1:] - cum)
    states = jnp.einsum('bcjgn,bcjgr,bcjgrp->bcgrpn', Bc, to_end, dtx)
    chunk_decay = jnp.exp(cum[:, :, -1])

    def step(h, inp):
        s_c, a_c = inp
        return a_c[..., None, None] * h + s_c, h
    h_last, h_start = lax.scan(step, h0, (states.swapaxes(0, 1), chunk_decay.swapaxes(0, 1)))
    h_start = h_start.swapaxes(0, 1)
    y_off = jnp.einsum('bcign,bcigr,bcgrpn->bcigrp', Cc, jnp.exp(cum), h_start)
    return (y_diag + y_off).reshape(bsz, L, H, P), h_last


def mlstm_chunked(q, k, v, log_i, log_f, state):
    bsz, L, H, dh = q.shape
    Q = ML_CHUNK
    nc = L // Q
    causal = jnp.tril(jnp.ones((Q, Q), bool))

    def chunks(t):
        return t.astype(F32).reshape(bsz, nc, Q, *t.shape[2:]).swapaxes(0, 1)

    def step(carry, inp):
        C, n, m = carry
        qc, kc, vc, ic, fc = inp
        b = jnp.cumsum(fc, axis=1)
        dlog = jnp.where(causal[None, :, :, None], b[:, :, None] - b[:, None] + ic[:, None], NEG_INF)
        inter = b + m[:, None]
        m_t = jnp.maximum(jnp.max(dlog, axis=2), inter)
        w = jnp.exp(dlog - m_t[:, :, None]) * jnp.einsum('bthd,bshd->btsh', qc, kc)
        g_in = jnp.exp(inter - m_t)
        num = jnp.einsum('btsh,bshd->bthd', w, vc) + g_in[..., None] * jnp.einsum('bhed,bthd->bthe', C, qc)
        den = jnp.sum(w, axis=2) + g_in * jnp.einsum('bhd,bthd->bth', n, qc)
        h = num / jnp.maximum(jnp.abs(den), jnp.exp(-m_t))[..., None]
        b_end = b[:, -1]
        g_s = b_end[:, None] - b + ic
        m_new = jnp.maximum(jnp.max(g_s, axis=1), b_end + m)
        w_s = jnp.exp(g_s - m_new[:, None])
        keep = jnp.exp(b_end + m - m_new)
        C_new = keep[..., None, None] * C + jnp.einsum('bsh,bshe,bshd->bhed', w_s, vc, kc)
        n_new = keep[..., None] * n + jnp.einsum('bsh,bshd->bhd', w_s, kc)
        return (C_new, n_new, m_new), h
    final, hs = lax.scan(step, state, tuple(chunks(t) for t in (q, k, v, log_i, log_f)))
    return hs.swapaxes(0, 1).reshape(bsz, L, H, dh), final


def even_mixer(h, hc, w_in, w_out, conv_w, conv_b, wa, ba, wx, bx, lam, q_g, k_g, rpb, need_ctx):
    bsz = h.shape[0]
    ux, ug, q, k, v = _split(h @ w_in, EVEN_SPLITS)
    uxc, ugc, qc, kc, vc = _split(hc @ w_in, EVEN_SPLITS)
    xl = dw_conv(ux, conv_w, conv_b)
    xcv = dw_conv(uxc, conv_w, conv_b)

    def run(inp, d, state):
        hs = rglru_scan(inp[0], wa[d], ba[d], wx[d], bx[d], lam[d], state)
        return hs, hs[:, -1]
    h0 = jnp.zeros((bsz, LRU_WIDTH), F32)
    r_lat, r_ctx = prefix_bidirectional(run, (xcv,), (xl,), h0, need_ctx)

    def heads(t):
        return t.reshape(t.shape[0], t.shape[1], NA_HEADS, NA_HEAD_DIM)
    a_lat, a_ctx = neighbourhood_attention(heads(q), heads(k), heads(v), heads(qc), heads(kc), heads(vc),
                                           q_g, k_g, rpb, need_ctx)
    y = jnp.concatenate([(r_lat * jax.nn.gelu(ug.astype(F32))).astype(h.dtype), a_lat], axis=-1) @ w_out
    if not need_ctx:
        return y, None
    yc = jnp.concatenate([(r_ctx * jax.nn.gelu(ugc.astype(F32))).astype(h.dtype), a_ctx], axis=-1) @ w_out
    return y, yc


def odd_mixer(h, hc, w_in, w_out, sconv_w, sconv_b, dt_bias, a_log, d_skip, snorm_g,
              mconv_w, mconv_b, gate_b, mnorm_g, need_ctx):
    bsz = h.shape[0]

    def prep(t):
        L = t.shape[1]
        z, xbc, dt_raw, mq, mk, mv, mo, mg = _split(t @ w_in, ODD_SPLITS)
        xbc = jax.nn.silu(dw_conv(xbc, sconv_w, sconv_b))
        xs, Bm, Cm = _split(xbc, (SSD_WIDTH, SSD_GROUPS * SSD_STATE, SSD_GROUPS * SSD_STATE))
        qk = jax.nn.silu(dw_conv(jnp.concatenate([mq, mk], axis=-1), mconv_w, mconv_b))
        mq, mk = _split(qk, (ML_WIDTH, ML_WIDTH))
        ssd_in = (xs.reshape(bsz, L, SSD_HEADS, SSD_HEAD_DIM), dt_raw.reshape(bsz, L, 2, SSD_HEADS),
                  Bm.reshape(bsz, L, SSD_GROUPS, SSD_STATE), Cm.reshape(bsz, L, SSD_GROUPS, SSD_STATE))
        ml_in = (mq.reshape(bsz, L, ML_HEADS, ML_HEAD_DIM),
                 mk.reshape(bsz, L, ML_HEADS, ML_HEAD_DIM) * (ML_HEAD_DIM ** -0.5),
                 mv.reshape(bsz, L, ML_HEADS, ML_HEAD_DIM),
                 mg.reshape(bsz, L, 2, 2, ML_HEADS) + gate_b)
        return z, mo, ssd_in, ml_in
    z, mo, s_in, m_in = prep(h)
    zc, moc, s_inc, m_inc = prep(hc)

    def ssd_run(inp, d, state):
        xh, dt_raw, Bm, Cm = inp
        dt = jax.nn.softplus(dt_raw[:, :, d].astype(F32) + dt_bias[d].astype(F32))
        return ssd_chunked(xh, dt, -jnp.exp(a_log[d].astype(F32)), Bm, Cm, state)
    s0 = jnp.zeros((bsz, SSD_GROUPS, SSD_HEADS // SSD_GROUPS, SSD_HEAD_DIM, SSD_STATE), F32)
    s_lat, s_ctx = prefix_bidirectional(ssd_run, s_inc, s_in, s0, need_ctx)

    def ml_run(inp, d, state):
        q, k, v, g = inp
        return mlstm_chunked(q, k, v, g[:, :, d, 0], jax.nn.log_sigmoid(g[:, :, d, 1].astype(F32)), state)
    m0 = (jnp.zeros((bsz, ML_HEADS, ML_HEAD_DIM, ML_HEAD_DIM), F32), jnp.zeros((bsz, ML_HEADS, ML_HEAD_DIM), F32),
          jnp.full((bsz, ML_HEADS), NEG_INF, F32))
    m_lat, m_ctx = prefix_bidirectional(ml_run, m_inc, m_in, m0, need_ctx)

    def merge(ys, xh, zz, hm, o):
        L = ys.shape[1]
        ys = ys + d_skip.astype(F32)[:, None] * xh.astype(F32)
        ys = ys.reshape(bsz, L, SSD_WIDTH) * jax.nn.silu(zz.astype(F32))
        ys = rms_norm(ys.reshape(bsz, L, SSD_GROUPS, -1), snorm_g.reshape(SSD_GROUPS, -1)).reshape(bsz, L, SSD_WIDTH)
        hm = rms_norm(hm, mnorm_g.reshape(ML_HEADS, ML_HEAD_DIM)).reshape(bsz, L, ML_WIDTH)
        hm = hm * jax.nn.sigmoid(o.astype(F32))
        return jnp.concatenate([ys, hm], axis=-1).astype(h.dtype) @ w_out
    y = merge(s_lat, s_in[0], z, m_lat, mo)
    if not need_ctx:
        return y, None
    return y, merge(s_ctx, s_inc[0], zc, m_ctx, moc)


def hierarchical_moe(t, router_g, router_e, w1, w3, w2):
    T = t.shape[0]
    tf = t.astype(F32)
    g_logits = tf @ router_g.astype(F32)
    g_sel = jnp.argmax(g_logits, axis=-1)
    g_w = jnp.max(jax.nn.softmax(g_logits, axis=-1), axis=-1, keepdims=True)
    e_logits = (tf @ router_e.astype(F32)).reshape(T, MOE_GROUPS, MOE_EXPERTS_PER_GROUP)
    e_logits = e_logits[jnp.arange(T), g_sel]
    top_v, top_i = lax.top_k(e_logits, MOE_TOP_K)
    top_w = jax.nn.softmax(top_v, axis=-1) * g_w
    expert_id = g_sel[:, None] * MOE_EXPERTS_PER_GROUP + top_i
    gate = jnp.einsum('tk,tke->te', top_w, jax.nn.one_hot(expert_id, MOE_EXPERTS, dtype=F32))
    out = jnp.zeros(t.shape, F32)
    for e in range(MOE_EXPERTS):
        hid = jax.nn.silu(t @ w1[e]) * (t @ w3[e])
        out = out + gate[:, e:e + 1] * (hid @ w2[e])
    return out.astype(t.dtype)


def setup_inputs(seed: int = 0) -> dict:
    key = jax.random.key(seed)
    ks = iter(jax.random.split(key, 64))
    D = D_MODEL

    def nrm(shape, s):
        return s * jax.random.normal(next(ks), shape, F32)
    lam_u = jax.random.uniform(next(ks), (N_EVEN, 2, LRU_WIDTH), F32, 0.9, 0.999) ** (1.0 / LRU_C)
    lru_lam = jnp.log(lam_u) - jnp.log1p(-lam_u)
    dt0 = jnp.exp(jax.random.uniform(next(ks), (N_ODD, 2, SSD_HEADS), F32, math.log(1e-3), math.log(1e-1)))
    ssd_dt_bias = dt0 + jnp.log(-jnp.expm1(-dt0))
    ssd_a_log = jnp.log(jax.random.uniform(next(ks), (N_ODD, 2, SSD_HEADS), F32, 1.0, 16.0))
    ig_b = nrm((N_ODD, 2, 1, ML_HEADS), 0.1)
    fg_b = jnp.broadcast_to(jnp.linspace(3.0, 6.0, ML_HEADS), (N_ODD, 2, 1, ML_HEADS)) + nrm((N_ODD, 2, 1, ML_HEADS), 0.1)
    ml_gate_b = jnp.concatenate([ig_b, fg_b], axis=2)
    return {
        'x': nrm((BATCH, SEQ, D), 1.0),
        'c': nrm((BATCH, D), 1.0),
        'ctx': nrm((BATCH, CTX_LEN, D), 1.0),
        'c_ctx': nrm((D,), 1.0),
        'mod_w': nrm((DEPTH, D, 6 * D), 0.5 * D ** -0.5),
        'mod_b': nrm((DEPTH, 6 * D), 0.02),
        'norm1_g': 1.0 + nrm((DEPTH, D), 0.1),
        'norm2_g': 1.0 + nrm((DEPTH, D), 0.1),
        'even_w_in': nrm((N_EVEN, D, EVEN_IN), D ** -0.5),
        'even_w_out': nrm((N_EVEN, EVEN_MIX, D), EVEN_MIX ** -0.5),
        'lru_conv_w': nrm((N_EVEN, CONV_W, LRU_WIDTH), CONV_W ** -0.5),
        'lru_conv_b': nrm((N_EVEN, LRU_WIDTH), 0.02),
        'lru_wa': nrm((N_EVEN, 2, LRU_BLOCKS, LRU_BLOCK, LRU_BLOCK), LRU_BLOCK ** -0.5),
        'lru_ba': nrm((N_EVEN, 2, LRU_WIDTH), 0.02),
        'lru_wx': nrm((N_EVEN, 2, LRU_BLOCKS, LRU_BLOCK, LRU_BLOCK), LRU_BLOCK ** -0.5),
        'lru_bx': nrm((N_EVEN, 2, LRU_WIDTH), 0.02),
        'lru_lam': lru_lam,
        'na_q_g': 1.0 + nrm((N_EVEN, NA_HEAD_DIM), 0.1),
        'na_k_g': 1.0 + nrm((N_EVEN, NA_HEAD_DIM), 0.1),
        'na_rpb': nrm((N_EVEN, NA_HEADS, 2 * NA_WIN_R - 1, 2 * NA_WIN_C - 1), 0.1),
        'odd_w_in': nrm((N_ODD, D, ODD_IN), D ** -0.5),
        'odd_w_out': nrm((N_ODD, ODD_MIX, D), ODD_MIX ** -0.5),
        'ssd_conv_w': nrm((N_ODD, CONV_W, SSD_XBC), CONV_W ** -0.5),
        'ssd_conv_b': nrm((N_ODD, SSD_XBC), 0.02),
        'ssd_dt_bias': ssd_dt_bias,
        'ssd_a_log': ssd_a_log,
        'ssd_d': 1.0 + nrm((N_ODD, SSD_HEADS), 0.1),
        'ssd_norm_g': 1.0 + nrm((N_ODD, SSD_WIDTH), 0.1),
        'ml_conv_w': nrm((N_ODD, CONV_W, 2 * ML_WIDTH), CONV_W ** -0.5),
        'ml_conv_b': nrm((N_ODD, 2 * ML_WIDTH), 0.02),
        'ml_gate_b': ml_gate_b,
        'ml_norm_g': 1.0 + nrm((N_ODD, ML_WIDTH), 0.1),
        'moe_router_g': nrm((DEPTH, D, MOE_GROUPS), D ** -0.5),
        'moe_router_e': nrm((DEPTH, D, MOE_EXPERTS), D ** -0.5),
        'moe_w1': nrm((DEPTH, MOE_EXPERTS, D, MOE_FF), D ** -0.5),
        'moe_w3': nrm((DEPTH, MOE_EXPERTS, D, MOE_FF), D ** -0.5),
        'moe_w2': nrm((DEPTH, MOE_EXPERTS, MOE_FF, D), MOE_FF ** -0.5),
    }


def reference(x, c, ctx, c_ctx, mod_w, mod_b, norm1_g, norm2_g,
              even_w_in, even_w_out, lru_conv_w, lru_conv_b, lru_wa, lru_ba, lru_wx, lru_bx, lru_lam,
              na_q_g, na_k_g, na_rpb,
              odd_w_in, odd_w_out, ssd_conv_w, ssd_conv_b, ssd_dt_bias, ssd_a_log, ssd_d, ssd_norm_g,
              ml_conv_w, ml_conv_b, ml_gate_b, ml_norm_g,
              moe_router_g, moe_router_e, moe_w1, moe_w3, moe_w2):
    bsz, S, D = x.shape
    xc = ctx
    for l in range(DEPTH):
        need_ctx = l < DEPTH - 1
        j = l // 2
        m = (jax.nn.silu(c) @ mod_w[l] + mod_b[l]).reshape(bsz, 1, 6, D)
        mc = (jax.nn.silu(c_ctx) @ mod_w[l] + mod_b[l]).reshape(1, 1, 6, D)
        h = rms_norm(x, norm1_g[l]) * (1 + m[:, :, 1]) + m[:, :, 0]
        hc = rms_norm(xc, norm1_g[l]) * (1 + mc[:, :, 1]) + mc[:, :, 0]
        if l % 2 == 0:
            y, yc = even_mixer(h, hc, even_w_in[j], even_w_out[j], lru_conv_w[j], lru_conv_b[j], lru_wa[j], lru_ba[j],
                               lru_wx[j], lru_bx[j], lru_lam[j], na_q_g[j], na_k_g[j], na_rpb[j], need_ctx)
        else:
            y, yc = odd_mixer(h, hc, odd_w_in[j], odd_w_out[j], ssd_conv_w[j], ssd_conv_b[j], ssd_dt_bias[j],
                              ssd_a_log[j], ssd_d[j], ssd_norm_g[j], ml_conv_w[j], ml_conv_b[j], ml_gate_b[j],
                              ml_norm_g[j], need_ctx)
        x = x + m[:, :, 2] * y
        h = rms_norm(x, norm2_g[l]) * (1 + m[:, :, 4]) + m[:, :, 3]
        if need_ctx:
            xc = xc + mc[:, :, 2] * yc
            hc = rms_norm(xc, norm2_g[l]) * (1 + mc[:, :, 4]) + mc[:, :, 3]
            tokens = jnp.concatenate([h.reshape(-1, D), hc.reshape(-1, D)], axis=0)
            out = hierarchical_moe(tokens, moe_router_g[l], moe_router_e[l], moe_w1[l], moe_w3[l], moe_w2[l])
            x = x + m[:, :, 5] * out[:bsz * S].reshape(bsz, S, D)
            xc = xc + mc[:, :, 5] * out[bsz * S:].reshape(xc.shape)
        else:
            out = hierarchical_moe(h.reshape(-1, D), moe_router_g[l], moe_router_e[l], moe_w1[l], moe_w3[l], moe_w2[l])
            x = x + m[:, :, 5] * out.reshape(bsz, S, D)
    return x
```

```python
import functools
import math

import jax
import jax.numpy as jnp
import numpy as np
from jax import lax
from jax.experimental import pallas as pl
from jax.experimental.pallas import tpu as pltpu

F32 = jnp.float32
BF16 = jnp.bfloat16
HI = lax.Precision.HIGHEST

EPS = 1e-6
NEG = -1e30
GRID_W = 64
CONV_W = 4
LRU_C = 8.0
TM = 256
CHUNK = 128
NA_RQ = 4
NA_RK = 12
N_EXPERTS = 16
EXPERT_LANE0 = 4
VMEM_LIMIT = 56 * 1024 * 1024


def _cp(sem, vmem=VMEM_LIMIT):
    return pltpu.CompilerParams(dimension_semantics=sem, vmem_limit_bytes=vmem)


def _sigmoid(x):
    return jax.nn.sigmoid(x)


def _silu(x):
    return x * jax.nn.sigmoid(x)


def _softplus(x):
    return jnp.maximum(x, 0.0) + jnp.log(1.0 + jnp.exp(-jnp.abs(x)))


def _gelu_tanh(x):
    return 0.5 * x * (1.0 + jnp.tanh(math.sqrt(2.0 / math.pi) * (x + 0.044715 * (x * x * x))))


def _rms(x, axis=-1):
    return x * lax.rsqrt(jnp.mean(x * x, axis=axis, keepdims=True) + EPS)


def _dot(a, b):
    return jnp.dot(a.astype(BF16), b.astype(BF16), preferred_element_type=F32)


def _dot_hi(a, b):
    return jnp.dot(a, b, precision=HI, preferred_element_type=F32)


def _dot_nt(a, b):
    return lax.dot_general(a.astype(BF16), b.astype(BF16), (((1,), (1,)), ((), ())),
                           preferred_element_type=F32)


def _mod_kernel(c_ref, w_ref, b_ref, o_ref):
    c = c_ref[...]
    o_ref[0] = _dot_hi(_silu(c), w_ref[0]) + b_ref[0]


def _modulation(cc, mod_w, mod_b):
    depth, d, n = mod_w.shape
    tn = 1536
    return pl.pallas_call(
        _mod_kernel,
        grid=(depth, n // tn),
        in_specs=[pl.BlockSpec((8, d), lambda l, j: (0, 0)),
                  pl.BlockSpec((1, d, tn), lambda l, j: (l, 0, j)),
                  pl.BlockSpec((1, 1, tn), lambda l, j: (l, 0, j))],
        out_specs=pl.BlockSpec((1, 8, tn), lambda l, j: (l, 0, j)),
        out_shape=jax.ShapeDtypeStruct((depth, 8, n), F32),
        compiler_params=_cp(("arbitrary", "arbitrary")),
        name="adaln_mod",
    )(cc, mod_w, mod_b.reshape(depth, 1, n))


def _inproj_kernel(*refs, fuse_prev):
    if fuse_prev:
        x_ref, mo_ref, pmod_ref, mod_ref, g_ref, w_ref, xo_ref, p_ref = refs
        x = x_ref[0] + pmod_ref[0, 0][5:6, :] * mo_ref[0]
        xo_ref[0] = x
    else:
        x_ref, mod_ref, g_ref, w_ref, p_ref = refs
        x = x_ref[0]
    mod = mod_ref[0, 0]
    h = _rms(x) * g_ref[...] * (1.0 + mod[1:2, :]) + mod[0:1, :]
    p_ref[0] = _dot(h, w_ref[...])


def _seg_map(b, i):
    return (b, jnp.minimum(i, 1), 0, 0)


def _inproj(x, mod, g, w, prev=None):
    bsz, L, d = x.shape
    n = w.shape[1]
    nt = L // TM
    tok = pl.BlockSpec((1, TM, d), lambda b, i: (b, i, 0))
    modspec = pl.BlockSpec((1, 1, 6, d), _seg_map)
    tail = [modspec, pl.BlockSpec((1, d), lambda b, i: (0, 0)), pl.BlockSpec((d, n), lambda b, i: (0, 0))]
    pspec = pl.BlockSpec((1, TM, n), lambda b, i: (b, i, 0))
    pshape = jax.ShapeDtypeStruct((bsz, L, n), F32)
    if prev is None:
        return pl.pallas_call(
            functools.partial(_inproj_kernel, fuse_prev=False),
            grid=(bsz, nt), in_specs=[tok] + tail, out_specs=pspec, out_shape=pshape,
            compiler_params=_cp(("parallel", "parallel")), name="inproj",
        )(x, mod, g.reshape(1, d), w)
    moe_out, pmod = prev
    return pl.pallas_call(
        functools.partial(_inproj_kernel, fuse_prev=True),
        grid=(bsz, nt), in_specs=[tok, tok, modspec] + tail,
        out_specs=[tok, pspec], out_shape=[jax.ShapeDtypeStruct(x.shape, F32), pshape],
        compiler_params=_cp(("parallel", "parallel")), name="inproj_res",
    )(x, moe_out, pmod, mod, g.reshape(1, d), w)


def _conv_tile(ref, i, nt, cw, cb, width=TM):
    L = nt * width
    t0 = pl.multiple_of(i * width, width)
    cur = ref[0, pl.ds(t0, width), :]
    prev = ref[0, pl.ds(pl.multiple_of(jnp.maximum(t0 - 8, 0), 8), 8), :]
    nxt = ref[0, pl.ds(pl.multiple_of(jnp.minimum(t0 + width, L - 8), 8), 8), :]
    return _conv_vals(cur, prev, nxt, i, nt, cw, cb, width, first_lat=TM // width)


def _conv_vals(cur, prev, nxt, i, nt, cw, cb, width, first_lat):
    prev = jnp.where((i != 0) & (i != first_lat), prev, 0.0)
    nxt = jnp.where((i != first_lat - 1) & (i != nt - 1), nxt, 0.0)
    cat = jnp.concatenate([prev, cur, nxt], axis=0)
    return (cw[0:1] * cat[6:6 + width] + cw[1:2] * cat[7:7 + width] + cw[2:3] * cur
            + cw[3:4] * cat[9:9 + width] + cb)


def _lru_kernel(ux_ref, cw_ref, cb_ref, gw_ref, gb_ref, lam_ref, o_ref, *, nt):
    cw = cw_ref[...]
    cb = cb_ref[...]
    lam = lam_ref[0]
    sp = _softplus(-lam)
    row = lax.broadcasted_iota(jnp.int32, (TM, 128), 0) & 7

    def gates(i, d):
        xl = _conv_tile(ux_ref, i, nt, cw, cb)
        g = _dot(xl, gw_ref[0, d]) + gb_ref[0, d]
        r = _sigmoid(g[:, :128])
        ig = _sigmoid(g[:, 128:])
        log_a = -LRU_C * r * sp[d:d + 1]
        a = jnp.exp(log_a)
        u = jnp.sqrt(1.0 - a * a) * (ig * xl)
        return a, u

    def scan_tile(i, d, carry, accumulate):
        a, u = gates(i, d)
        rev = d == 1
        for k in (1, 2, 4):
            sh = TM - k if rev else k
            ok = (row < 8 - k) if rev else (row >= k)
            ash = pltpu.roll(a, sh, 0)
            ush = pltpu.roll(u, sh, 0)
            u = jnp.where(ok, u + a * ush, u)
            a = jnp.where(ok, a * ash, a)
        t0 = i * TM
        groups = range(TM // 8)
        for s in (reversed(groups) if rev else groups):
            h = u[s * 8:(s + 1) * 8] + a[s * 8:(s + 1) * 8] * carry
            carry = h[0:1] if rev else h[7:8]
            idx = pl.ds(pl.multiple_of(t0 + s * 8, 8), 8)
            if accumulate:
                o_ref[0, idx, :] += h
            else:
                o_ref[0, idx, :] = h
        return carry

    zero = jnp.zeros((1, 128), F32)
    lax.fori_loop(0, nt, lambda i, c: scan_tile(i, 0, c, False), zero)
    lax.fori_loop(0, nt, lambda j, c: scan_tile(jnp.where(j == 0, 0, nt - j), 1, c, True), zero)


def _lru(proj, conv_w, conv_b, wa, ba, wx, bx, lam):
    bsz, L, _ = proj.shape
    nt = L // TM
    width = conv_w.shape[1]
    ng = width // 128

    def blockdiag(w):
        w = w.reshape(2, ng, 2, 64, 64)
        z = jnp.zeros_like(w[:, :, 0])
        top = jnp.concatenate([w[:, :, 0], z], axis=-1)
        bot = jnp.concatenate([z, w[:, :, 1]], axis=-1)
        return jnp.concatenate([top, bot], axis=-2)
    gw = jnp.concatenate([blockdiag(wa), blockdiag(wx)], axis=-1).transpose(1, 0, 2, 3).astype(BF16)
    gb = jnp.concatenate([ba.reshape(2, ng, 1, 128), bx.reshape(2, ng, 1, 128)], axis=-1).transpose(1, 0, 2, 3)
    lam_g = lam.reshape(2, ng, 128).transpose(1, 0, 2)
    return pl.pallas_call(
        functools.partial(_lru_kernel, nt=nt),
        grid=(bsz, ng),
        in_specs=[pl.BlockSpec((1, L, 128), lambda b, c: (b, 0, c)),
                  pl.BlockSpec((CONV_W, 128), lambda b, c: (0, c)),
                  pl.BlockSpec((1, 128), lambda b, c: (0, c)),
                  pl.BlockSpec((1, 2, 128, 256), lambda b, c: (c, 0, 0, 0)),
                  pl.BlockSpec((1, 2, 1, 256), lambda b, c: (c, 0, 0, 0)),
                  pl.BlockSpec((1, 2, 128), lambda b, c: (c, 0, 0))],
        out_specs=pl.BlockSpec((1, L, 128), lambda b, c: (b, 0, c)),
        out_shape=jax.ShapeDtypeStruct((bsz, L, width), F32),
        compiler_params=_cp(("parallel", "parallel")), name="rglru",
    )(proj, conv_w, conv_b.reshape(1, width), gw, gb, lam_g)


def _na_bias_table(rpb, rows):
    nh = rpb.shape[0]
    win_r = (rpb.shape[1] + 1) // 2
    win_c = (rpb.shape[2] + 1) // 2
    qr = np.arange(NA_RQ)[:, None, None, None]
    qc = np.arange(GRID_W)[None, :, None, None]
    kr = np.arange(NA_RK)[None, None, :, None]
    kc = np.arange(GRID_W)[None, None, None, :]
    cstart = np.clip(qc - win_c // 2, 0, GRID_W - win_c)
    col_ok = (kc >= cstart) & (kc < cstart + win_c)
    dcol = np.clip(kc - qc + (win_c - 1), 0, 2 * win_c - 2)
    tabs = []
    for r0, w0 in ((0, 0), (2 * NA_RQ, NA_RQ), (rows - NA_RQ, rows - NA_RK)):
        r = r0 + qr
        kabs = w0 + kr
        rstart = np.clip(r - win_r // 2, 0, rows - win_r)
        ok = (kabs >= rstart) & (kabs < rstart + win_r) & col_ok
        drow = np.clip(kabs - r + (win_r - 1), 0, 2 * win_r - 2)
        drow_b, dcol_b = np.broadcast_arrays(drow, dcol)
        b = rpb.astype(F32)[:, drow_b, dcol_b]
        b = jnp.where(jnp.asarray(np.broadcast_to(ok, drow_b.shape))[None], b, NEG)
        tabs.append(b.reshape(nh, NA_RQ * GRID_W, NA_RK * GRID_W))
    return jnp.stack(tabs)


def _na_kernel(q_ref, k_ref, v_ref, bias_ref, qg_ref, kg_ref, o_ref, *, rows, hd):
    i = pl.program_id(2)
    scale = hd ** -0.5
    qg = qg_ref[...] * scale
    kg = kg_ref[...]
    q2 = q_ref[0]
    nkeys = NA_RK * GRID_W

    def head_out(hh, with_window, kwin, vwin, kctx, vctx):
        sl = slice(hh * hd, (hh + 1) * hd)
        qn = _rms(q2[:, sl]) * qg
        kc = _rms(kctx[:, sl]) * kg
        s_c = _dot_nt(qn, kc)
        m = jnp.max(s_c, axis=-1, keepdims=True)
        if with_window:
            kw = _rms(kwin[:, sl]) * kg
            s_w = _dot_nt(qn, kw) + bias_ref[0, hh]
            m = jnp.maximum(m, jnp.max(s_w, axis=-1, keepdims=True))
            p_w = jnp.exp(s_w - m)
        p_c = jnp.exp(s_c - m)
        den = jnp.sum(p_c, axis=-1, keepdims=True)
        num = _dot(p_c, vctx[:, sl])
        if with_window:
            den = den + jnp.sum(p_w, axis=-1, keepdims=True)
            num = num + _dot(p_w, vwin[:, sl])
        return num / den

    kctx = k_ref[0, 0:TM, :]
    vctx = v_ref[0, 0:TM, :]

    @pl.when(i == 0)
    def _():
        o_ref[0] = jnp.concatenate([head_out(hh, False, None, None, kctx, vctx) for hh in range(2)], axis=-1)

    @pl.when(i > 0)
    def _():
        r0 = (i - 1) * NA_RQ
        w0 = jnp.clip(r0 - NA_RQ, 0, rows - NA_RK)
        start = pl.multiple_of(TM + w0 * GRID_W, GRID_W)
        kwin = k_ref[0, pl.ds(start, nkeys), :]
        vwin = v_ref[0, pl.ds(start, nkeys), :]
        o_ref[0] = jnp.concatenate([head_out(hh, True, kwin, vwin, kctx, vctx) for hh in range(2)], axis=-1)


def _na(proj, q_g, k_g, rpb, col0):
    bsz, L, _ = proj.shape
    nh = rpb.shape[0]
    hd = q_g.shape[0]
    width = nh * hd
    rows = (L - TM) // GRID_W
    nb = L // TM
    bias = _na_bias_table(rpb, rows)
    qb, kb, vb = col0 // 128, (col0 + width) // 128, (col0 + 2 * width) // 128

    def pat(i):
        return jnp.where(i <= 1, 0, jnp.where(i == nb - 1, 2, 1))
    return pl.pallas_call(
        functools.partial(_na_kernel, rows=rows, hd=hd),
        grid=(bsz, width // 128, nb),
        in_specs=[pl.BlockSpec((1, TM, 128), lambda b, h, i: (b, i, qb + h)),
                  pl.BlockSpec((1, L, 128), lambda b, h, i: (b, 0, kb + h)),
                  pl.BlockSpec((1, L, 128), lambda b, h, i: (b, 0, vb + h)),
                  pl.BlockSpec((1, 2, TM, NA_RK * GRID_W), lambda b, h, i: (pat(i), h, 0, 0)),
                  pl.BlockSpec((1, hd), lambda b, h, i: (0, 0)),
                  pl.BlockSpec((1, hd), lambda b, h, i: (0, 0))],
        out_specs=pl.BlockSpec((1, TM, 128), lambda b, h, i: (b, i, h)),
        out_shape=jax.ShapeDtypeStruct((bsz, L, width), F32),
        compiler_params=_cp(("parallel", "parallel", "arbitrary")), name="nbr_attn",
    )(proj, proj, proj, bias, q_g.reshape(1, hd), k_g.reshape(1, hd))


def _route(lg):
    lane = lax.broadcasted_iota(jnp.int32, lg.shape, 1)
    lane_f = lane.astype(F32)
    is_g = lane < EXPERT_LANE0
    gl = jnp.where(is_g, lg, NEG)
    gmax = jnp.max(gl, axis=-1, keepdims=True)
    gsel = jnp.min(jnp.where(is_g & (gl == gmax), lane_f, 1e9), axis=-1, keepdims=True)
    g_w = 1.0 / jnp.sum(jnp.where(is_g, jnp.exp(gl - gmax), 0.0), axis=-1, keepdims=True)
    grp = ((lane - EXPERT_LANE0) >> 2).astype(F32)
    in_g = (lane >= EXPERT_LANE0) & (lane < EXPERT_LANE0 + N_EXPERTS) & (grp == gsel)
    el = jnp.where(in_g, lg, NEG)
    v1 = jnp.max(el, axis=-1, keepdims=True)
    i1 = jnp.min(jnp.where(in_g & (el == v1), lane_f, 1e9), axis=-1, keepdims=True)
    el2 = jnp.where(lane_f == i1, NEG, el)
    v2 = jnp.max(el2, axis=-1, keepdims=True)
    i2 = jnp.min(jnp.where(in_g & (lane_f != i1) & (el2 == v2), lane_f, 1e9), axis=-1, keepdims=True)
    t = jnp.exp(v2 - v1)
    w1 = g_w / (1.0 + t)
    w2 = g_w * t / (1.0 + t)
    return jnp.where(lane_f == i1, w1, 0.0) + jnp.where(lane_f == i2, w2, 0.0)


def _outproj_kernel(*refs, even):
    if even:
        (r_ref, ug_ref, a_ref, x_ref, w_ref, mod_ref, g2_ref, rw_ref, x1_ref, h2_ref, gate_ref) = refs
        y_in = jnp.concatenate([r_ref[0] * _gelu_tanh(ug_ref[0]), a_ref[0]], axis=-1)
    else:
        (ys_ref, z_ref, hm_ref, mo_ref, sg_ref, mg_ref, x_ref, w_ref, mod_ref, g2_ref, rw_ref,
         x1_ref, h2_ref, gate_ref) = refs
        ys = ys_ref[0] * _silu(z_ref[0])
        sg = sg_ref[...]
        mg = mg_ref[...]
        hm = hm_ref[0]
        sig_o = _sigmoid(mo_ref[0])
        gw = ys.shape[-1] // 2
        parts = [_rms(ys[:, g * gw:(g + 1) * gw]) * sg[:, g * gw:(g + 1) * gw] for g in range(2)]
        hw = 128
        parts += [_rms(hm[:, h * hw:(h + 1) * hw]) * mg[:, h * hw:(h + 1) * hw] * sig_o[:, h * hw:(h + 1) * hw]
                  for h in range(hm.shape[-1] // hw)]
        y_in = jnp.concatenate(parts, axis=-1)
    mod = mod_ref[0, 0]
    x1 = x_ref[0] + mod[2:3, :] * _dot(y_in, w_ref[...])
    x1_ref[0] = x1
    h2 = _rms(x1) * g2_ref[...] * (1.0 + mod[4:5, :]) + mod[3:4, :]
    h2_ref[0] = h2.astype(BF16)
    gate_ref[0] = _route(_dot_hi(h2, rw_ref[...]))


def _outproj(mix_inputs, x, w, mod, g2, rw, even, tile0, ntiles):
    bsz, _, d = x.shape
    specs, args = [], []
    for arr, cb, wdt in mix_inputs:
        if arr.ndim == 3:
            specs.append(pl.BlockSpec((1, TM, wdt), lambda b, i, cb=cb: (b, i + tile0, cb)))
        else:
            specs.append(pl.BlockSpec((1, wdt), lambda b, i: (0, 0)))
        args.append(arr)
    specs += [pl.BlockSpec((1, TM, d), lambda b, i: (b, i + tile0, 0)),
              pl.BlockSpec(w.shape, lambda b, i: (0, 0)),
              pl.BlockSpec((1, 1, 6, d), lambda b, i: (b, jnp.minimum(i + tile0, 1), 0, 0)),
              pl.BlockSpec((1, d), lambda b, i: (0, 0)),
              pl.BlockSpec(rw.shape, lambda b, i: (0, 0))]
    args += [x, w, mod, g2.reshape(1, d), rw]
    lo = ntiles * TM
    return pl.pallas_call(
        functools.partial(_outproj_kernel, even=even),
        grid=(bsz, ntiles), in_specs=specs,
        out_specs=[pl.BlockSpec((1, TM, d), lambda b, i: (b, i, 0)),
                   pl.BlockSpec((1, TM, d), lambda b, i: (b, i, 0)),
                   pl.BlockSpec((1, TM, 128), lambda b, i: (b, i, 0))],
        out_shape=[jax.ShapeDtypeStruct((bsz, lo, d), F32), jax.ShapeDtypeStruct((bsz, lo, d), BF16),
                   jax.ShapeDtypeStruct((bsz, lo, 128), F32)],
        compiler_params=_cp(("parallel", "parallel")), name="outproj_even" if even else "outproj_odd",
    )(*args)


def _moe_dense_kernel(h_ref, g_ref, w1_ref, w3_ref, w2_ref, o_ref):
    e = pl.program_id(1)
    h = h_ref[...]
    a = jnp.dot(h, w1_ref[0], preferred_element_type=F32)
    b = jnp.dot(h, w3_ref[0], preferred_element_type=F32)
    y = _dot(_silu(a) * b, w2_ref[0])
    g = g_ref[...]
    lane = lax.broadcasted_iota(jnp.int32, g.shape, 1)
    ge = jnp.sum(jnp.where(lane == e + EXPERT_LANE0, g, 0.0), axis=-1, keepdims=True)

    @pl.when(e == 0)
    def _():
        o_ref[...] = ge * y

    @pl.when(e > 0)
    def _():
        o_ref[...] += ge * y


def _moe_dense(h2, gate, w1, w3, w2):
    t, d = h2.shape
    ne, _, ff = w1.shape
    tm = math.gcd(t, 1024)
    return pl.pallas_call(
        _moe_dense_kernel,
        grid=(t // tm, ne),
        in_specs=[pl.BlockSpec((tm, d), lambda i, e: (i, 0)),
                  pl.BlockSpec((tm, 128), lambda i, e: (i, 0)),
                  pl.BlockSpec((1, d, ff), lambda i, e: (e, 0, 0)),
                  pl.BlockSpec((1, d, ff), lambda i, e: (e, 0, 0)),
                  pl.BlockSpec((1, ff, d), lambda i, e: (e, 0, 0))],
        out_specs=pl.BlockSpec((tm, d), lambda i, e: (i, 0)),
        out_shape=jax.ShapeDtypeStruct((t, d), F32),
        compiler_params=_cp(("parallel", "arbitrary")), name="moe_dense",
    )(h2, gate, w1, w3, w2)


def _tri(q, rev):
    r = lax.broadcasted_iota(jnp.int32, (q, q), 0)
    c = lax.broadcasted_iota(jnp.int32, (q, q), 1)
    return (c >= r) if rev else (c <= r)


def _ssd_kernel(xs_ref, b_ref, c_ref, dt_ref, cwx_ref, cbx_ref, cwb_ref, cbb_ref, cwc_ref, cbc_ref,
                dtb_ref, alog_ref, dsk_ref, sel_ref, o_ref, xc_s, bc_s, cc_s, dt_s, cumt_s, h_s, *, nc):
    g = pl.program_id(1)
    q = CHUNK
    first_lat = TM // q

    dtb = dtb_ref[...]

    def prep(i, _):
        t0 = pl.multiple_of(i * q, q)
        idx = pl.ds(t0, q)
        for src, cw, cb, dst in ((xs_ref, cwx_ref, cbx_ref, xc_s), (b_ref, cwb_ref, cbb_ref, bc_s),
                                 (c_ref, cwc_ref, cbc_ref, cc_s)):
            L = nc * q
            cur = src[0, idx, :]
            prev = src[0, pl.ds(pl.multiple_of(jnp.maximum(t0 - 8, 0), 8), 8), :]
            nxt = src[0, pl.ds(pl.multiple_of(jnp.minimum(t0 + q, L - 8), 8), 8), :]
            dst[idx, :] = _silu(_conv_vals(cur, prev, nxt, i, nc, cw[...], cb[...], q, first_lat))
        dt_s[idx, :] = _softplus(dt_ref[0, idx, :] + dtb)
        o_ref[0, idx, :] = dsk_ref[...] * xc_s[idx, :]
        return 0
    lax.fori_loop(0, nc, prep, 0)

    a_lane = -jnp.exp(alog_ref[...])
    lane256 = lax.broadcasted_iota(jnp.int32, (q, 256), 1) // 64

    for d in (0, 1):
        rev = d == 1
        causal = _tri(q, rev)
        tri = causal.astype(F32)
        sel = sel_ref[d, 0]
        last = 0 if rev else q - 1
        h_s[...] = jnp.zeros_like(h_s)

        def chunk(j, _):
            if rev:
                ci = jnp.where(j < first_lat, first_lat - 1 - j, nc - 1 - (j - first_lat))
            else:
                ci = j
            idx = pl.ds(pl.multiple_of(ci * q, q), q)
            dt_all = dt_s[idx, :]
            cum = _dot_hi(tri, dt_all * a_lane)
            cumt_s[...] = cum.T
            dt_e = _dot_hi(dt_all, sel)
            cum_e = _dot_hi(cum, sel)
            tot_e = cum_e[last:last + 1, :]
            xc = xc_s[idx, :]
            bc = bc_s[idx, :]
            cc = cc_s[idx, :]
            dtx = xc * dt_e
            cb = _dot_nt(cc, bc)
            y = jnp.exp(cum_e) * _dot(cc, h_s[...])
            for r in range(4):
                lr = d * 8 + g * 4 + r
                seg = cum_e[:, r * 64:r * 64 + 1] - cumt_s[pl.ds(lr, 1), :]
                m = cb * jnp.exp(jnp.where(causal, seg, NEG))
                y = y + _dot(m, jnp.where(lane256 == r, dtx, 0.0))
            o_ref[0, idx, :] += y
            states = _dot(bc.T, dtx * jnp.exp(tot_e - cum_e))
            h_s[...] = jnp.exp(tot_e) * h_s[...] + states
            return 0
        lax.fori_loop(0, nc, chunk, 0)


def _ssd(proj, conv_w, conv_b, dt_bias, a_log, d_skip, cols):
    bsz, L, _ = proj.shape
    nc = L // CHUNK
    cx, cbm, ccm, cdt = cols
    nh = d_skip.shape[0]
    dtb = jnp.zeros((1, 128), F32).at[0, :2 * nh].set(dt_bias.reshape(-1))
    alog = jnp.zeros((1, 128), F32).at[0, :2 * nh].set(a_log.reshape(-1))
    dsk = jnp.repeat(d_skip.astype(F32), 64).reshape(1, nh * 64)
    sel = np.zeros((2, 2, 128, 256), np.float32)
    for d in range(2):
        for g in range(2):
            for r in range(4):
                sel[d, g, d * nh + g * 4 + r, r * 64:(r + 1) * 64] = 1.0
    cw_specs = []
    for width, off in ((256, 0), (128, 512), (128, 768)):
        cw_specs += [pl.BlockSpec((CONV_W, width), lambda b, g, off=off, width=width: (0, off // width + g)),
                     pl.BlockSpec((1, width), lambda b, g, off=off, width=width: (0, off // width + g))]
    return pl.pallas_call(
        functools.partial(_ssd_kernel, nc=nc),
        grid=(bsz, 2),
        in_specs=[pl.BlockSpec((1, L, 256), lambda b, g: (b, 0, cx // 256 + g)),
                  pl.BlockSpec((1, L, 128), lambda b, g: (b, 0, cbm // 128 + g)),
                  pl.BlockSpec((1, L, 128), lambda b, g: (b, 0, ccm // 128 + g)),
                  pl.BlockSpec((1, L, 128), lambda b, g: (b, 0, cdt // 128))] + cw_specs + [
                  pl.BlockSpec((1, 128), lambda b, g: (0, 0)),
                  pl.BlockSpec((1, 128), lambda b, g: (0, 0)),
                  pl.BlockSpec((1, 256), lambda b, g: (0, g)),
                  pl.BlockSpec((2, 1, 128, 256), lambda b, g: (0, g, 0, 0))],
        out_specs=pl.BlockSpec((1, L, 256), lambda b, g: (b, 0, g)),
        out_shape=jax.ShapeDtypeStruct((bsz, L, nh * 64), F32),
        scratch_shapes=[pltpu.VMEM((L, 256), F32), pltpu.VMEM((L, 128), F32), pltpu.VMEM((L, 128), F32),
                        pltpu.VMEM((L, 128), F32), pltpu.VMEM((128, CHUNK), F32), pltpu.VMEM((128, 256), F32)],
        compiler_params=_cp(("parallel", "arbitrary")), name="ssd",
    )(proj, proj, proj, proj, conv_w, conv_b.reshape(1, -1), conv_w, conv_b.reshape(1, -1),
      conv_w, conv_b.reshape(1, -1), dtb, alog, dsk, jnp.asarray(sel))


def _mlstm_kernel(q_ref, k_ref, v_ref, g_ref, cwq_ref, cbq_ref, cwk_ref, cbk_ref, gb_ref, sel_ref, o_ref,
                  qc_s, kc_s, c_s, n_s, m_s, *, nc, dh):
    q = CHUNK
    first_lat = TM // q

    def prep(i, _):
        t0 = pl.multiple_of(i * q, q)
        idx = pl.ds(t0, q)
        L = nc * q
        for src, cw, cb, dst, mul in ((q_ref, cwq_ref, cbq_ref, qc_s, 1.0), (k_ref, cwk_ref, cbk_ref, kc_s, dh ** -0.5)):
            cur = src[0, idx, :]
            prev = src[0, pl.ds(pl.multiple_of(jnp.maximum(t0 - 8, 0), 8), 8), :]
            nxt = src[0, pl.ds(pl.multiple_of(jnp.minimum(t0 + q, L - 8), 8), 8), :]
            dst[idx, :] = _silu(_conv_vals(cur, prev, nxt, i, nc, cw[...], cb[...], q, first_lat)) * mul
        return 0
    lax.fori_loop(0, nc, prep, 0)

    gb = gb_ref[...]
    lane = lax.broadcasted_iota(jnp.int32, (q, 128), 1)
    for d in (0, 1):
        rev = d == 1
        causal = _tri(q, rev)
        tri = causal.astype(F32)
        sel = sel_ref[d, 0]
        last = 0 if rev else q - 1
        c_s[...] = jnp.zeros_like(c_s)
        n_s[...] = jnp.zeros_like(n_s)
        m_s[...] = jnp.full_like(m_s, NEG)

        def chunk(j, _):
            if rev:
                ci = jnp.where(j < first_lat, first_lat - 1 - j, nc - 1 - (j - first_lat))
            else:
                ci = j
            idx = pl.ds(pl.multiple_of(ci * q, q), q)
            gsel = _dot_hi(g_ref[0, idx, :] + gb, sel)
            logf = jnp.minimum(gsel, 0.0) - jnp.log(1.0 + jnp.exp(-jnp.abs(gsel)))
            gi = jnp.where(lane == 0, gsel, jnp.where(lane == 1, logf, 0.0))
            cum = _dot_hi(tri, gi)
            git = gi.T
            cumt = cum.T
            i_col, i_row = gi[:, 0:1], git[0:1, :]
            b_col, b_row = cum[:, 1:2], cumt[1:2, :]
            m_prev = m_s[...]
            dlog = jnp.where(causal, b_col - b_row + i_row, NEG)
            inter = b_col + m_prev
            m_t = jnp.maximum(jnp.max(dlog, axis=-1, keepdims=True), inter)
            qc = qc_s[idx, :]
            kc = kc_s[idx, :]
            vc = v_ref[0, idx, :]
            kct = kc.T
            w = jnp.exp(dlog - m_t) * _dot(qc, kct)
            g_in = jnp.exp(inter - m_t)
            num = _dot(w, vc) + g_in * _dot(qc, c_s[...])
            den = jnp.sum(w, axis=-1, keepdims=True) + g_in * jnp.sum(qc * n_s[...], axis=-1, keepdims=True)
            o_ref[0, idx, :] += num / jnp.maximum(jnp.abs(den), jnp.exp(-m_t))
            b_end = b_col[last:last + 1, :]
            g_s = b_end - b_col + i_col
            m_new = jnp.maximum(jnp.max(g_s, axis=0, keepdims=True), b_end + m_prev)
            w_s = jnp.exp(g_s - m_new)
            keep = jnp.exp(b_end + m_prev - m_new)
            c_s[...] = keep * c_s[...] + _dot(kct, w_s * vc)
            n_s[...] = keep * n_s[...] + jnp.sum(w_s * kc, axis=0, keepdims=True)
            m_s[...] = m_new
            return 0

        if d == 0:
            o_ref[...] = jnp.zeros_like(o_ref)
        lax.fori_loop(0, nc, chunk, 0)


def _mlstm(proj, conv_w, conv_b, gate_b, cols):
    bsz, L, _ = proj.shape
    nc = L // CHUNK
    cq, ck, cv, cg = cols
    nh = gate_b.shape[-1]
    dh = conv_w.shape[1] // (2 * nh)
    gbl = jnp.zeros((1, 128), F32).at[0, 16:16 + 4 * nh].set(gate_b.reshape(-1))
    sel = np.zeros((2, nh, 128, 128), np.float32)
    for d in range(2):
        for h in range(nh):
            sel[d, h, 16 + d * 2 * nh + h, 0] = 1.0
            sel[d, h, 16 + d * 2 * nh + nh + h, 1] = 1.0
    return pl.pallas_call(
        functools.partial(_mlstm_kernel, nc=nc, dh=dh),
        grid=(bsz, nh),
        in_specs=[pl.BlockSpec((1, L, dh), lambda b, h: (b, 0, cq // dh + h)),
                  pl.BlockSpec((1, L, dh), lambda b, h: (b, 0, ck // dh + h)),
                  pl.BlockSpec((1, L, dh), lambda b, h: (b, 0, cv // dh + h)),
                  pl.BlockSpec((1, L, 128), lambda b, h: (b, 0, cg // 128)),
                  pl.BlockSpec((CONV_W, dh), lambda b, h: (0, h)),
                  pl.BlockSpec((1, dh), lambda b, h: (0, h)),
                  pl.BlockSpec((CONV_W, dh), lambda b, h: (0, nh + h)),
                  pl.BlockSpec((1, dh), lambda b, h: (0, nh + h)),
                  pl.BlockSpec((1, 128), lambda b, h: (0, 0)),
                  pl.BlockSpec((2, 1, 128, 128), lambda b, h: (0, h, 0, 0))],
        out_specs=pl.BlockSpec((1, L, dh), lambda b, h: (b, 0, h)),
        out_shape=jax.ShapeDtypeStruct((bsz, L, nh * dh), F32),
        scratch_shapes=[pltpu.VMEM((L, dh), F32), pltpu.VMEM((L, dh), F32), pltpu.VMEM((dh, dh), F32),
                        pltpu.VMEM((1, dh), F32), pltpu.VMEM((1, 1), F32)],
        compiler_params=_cp(("parallel", "arbitrary")), name="mlstm",
    )(proj, proj, proj, proj, conv_w, conv_b.reshape(1, -1), conv_w, conv_b.reshape(1, -1), gbl, jnp.asarray(sel))


def _final_kernel(x_ref, o_ref, mod_ref, y_ref):
    y_ref[0] = x_ref[0] + mod_ref[0, 0][5:6, :] * o_ref[0]


def _final(x1, moe_out, mod):
    bsz, S, d = x1.shape
    tok = pl.BlockSpec((1, TM, d), lambda b, i: (b, i, 0))
    return pl.pallas_call(
        _final_kernel, grid=(bsz, S // TM),
        in_specs=[tok, tok, pl.BlockSpec((1, 1, 6, d), lambda b, i: (b, 1, 0, 0))],
        out_specs=tok, out_shape=jax.ShapeDtypeStruct(x1.shape, F32),
        compiler_params=_cp(("parallel", "parallel")), name="final_residual",
    )(x1, moe_out, mod)


def kernel(x, c, ctx, c_ctx, mod_w, mod_b, norm1_g, norm2_g, even_w_in, even_w_out, lru_conv_w, lru_conv_b, lru_wa, lru_ba, lru_wx, lru_bx, lru_lam, na_q_g, na_k_g, na_rpb, odd_w_in, odd_w_out, ssd_conv_w, ssd_conv_b, ssd_dt_bias, ssd_a_log, ssd_d, ssd_norm_g, ml_conv_w, ml_conv_b, ml_gate_b, ml_norm_g, moe_router_g, moe_router_e, moe_w1, moe_w3, moe_w2):
    bsz, S, d = x.shape
    lc = ctx.shape[1]
    assert lc == TM and S % TM == 0 and mod_w.shape[0] == 2
    nt = (lc + S) // TM
    L = lc + S

    cc = jnp.zeros((8, d), F32).at[0].set(c_ctx).at[1:1 + bsz].set(c)
    mod_all = _modulation(cc, mod_w, mod_b)

    def mod_for(l):
        m = mod_all[l].reshape(8, 6, d)
        return jnp.stack([jnp.broadcast_to(m[0], (bsz, 6, d)), m[1:1 + bsz]], axis=1)

    def router_w(l):
        return jnp.zeros((d, 128), F32).at[:, :EXPERT_LANE0].set(moe_router_g[l]) \
            .at[:, EXPERT_LANE0:EXPERT_LANE0 + N_EXPERTS].set(moe_router_e[l])

    xx = jnp.concatenate([ctx, x], axis=1)

    mod0 = mod_for(0)
    proj = _inproj(xx, mod0, norm1_g[0], even_w_in[0].astype(BF16))
    lw = lru_conv_w.shape[-1]
    r = _lru(proj, lru_conv_w[0], lru_conv_b[0], lru_wa[0], lru_ba[0], lru_wx[0], lru_bx[0], lru_lam[0])
    a = _na(proj, na_q_g[0], na_k_g[0], na_rpb[0], col0=2 * lw)
    x1, h2, gate = _outproj([(r, 0, lw), (proj, 1, lw), (a, 0, lw)], xx, even_w_out[0].astype(BF16), mod0,
                            norm2_g[0], router_w(0), even=True, tile0=0, ntiles=nt)
    moe0 = _moe_dense(h2.reshape(bsz * L, d), gate.reshape(bsz * L, 128), moe_w1[0].astype(BF16),
                      moe_w3[0].astype(BF16), moe_w2[0].astype(BF16)).reshape(bsz, L, d)

    mod1 = mod_for(1)
    sw = ssd_d.shape[-1] * 64
    xbc = ssd_conv_w.shape[-1]
    mw = ml_norm_g.shape[-1]
    w = odd_w_in[0]
    o = np.cumsum([0, sw, xbc, 2 * ssd_d.shape[-1], mw, mw, mw, mw])
    small = jnp.concatenate([w[:, o[2]:o[3]], w[:, o[7]:], jnp.zeros((d, 128 - 2 * ssd_d.shape[-1] - (w.shape[1] - o[7])), F32)], axis=1)
    w_odd = jnp.concatenate([w[:, :o[2]], w[:, o[3]:o[7]], small], axis=1).astype(BF16)
    cz, cxs = 0, sw
    cB, cC = cxs + sw, cxs + sw + (xbc - sw) // 2
    cq = sw + xbc
    ck, cv, co, csm = cq + mw, cq + 2 * mw, cq + 3 * mw, cq + 4 * mw
    x0, proj1 = _inproj(x1, mod1, norm1_g[1], w_odd, prev=(moe0, mod0))
    ys = _ssd(proj1, ssd_conv_w[0], ssd_conv_b[0], ssd_dt_bias[0], ssd_a_log[0], ssd_d[0], (cxs, cB, cC, csm))
    hm = _mlstm(proj1, ml_conv_w[0], ml_conv_b[0], ml_gate_b[0], (cq, ck, cv, csm))
    x2, h2b, gate1 = _outproj([(ys, 0, sw), (proj1, cz // sw, sw), (hm, 0, mw), (proj1, co // mw, mw),
                               (ssd_norm_g[0].reshape(1, sw), 0, sw), (ml_norm_g[0].reshape(1, mw), 0, mw)],
                              x0, odd_w_out[0].astype(BF16), mod1, norm2_g[1], router_w(1),
                              even=False, tile0=1, ntiles=nt - 1)
    moe1 = _moe_dense(h2b.reshape(bsz * S, d), gate1.reshape(bsz * S, 128), moe_w1[1].astype(BF16),
                      moe_w3[1].astype(BF16), moe_w2[1].astype(BF16)).reshape(bsz, S, d)
    return _final(x2, moe1, mod1)
```

```python
import functools
import math

import jax
import jax.numpy as jnp
import numpy as np
from jax import lax
from jax.experimental import pallas as pl
from jax.experimental.pallas import tpu as pltpu

F32 = jnp.float32
BF16 = jnp.bfloat16
HI = lax.Precision.HIGHEST

EPS = 1e-6
NEG = -1e30
GRID_W = 64
CONV_W = 4
LRU_C = 8.0
TM = 256
CHUNK = 128
NA_RQ = 4
NA_RK = 12
N_EXPERTS = 16
EXPERT_LANE0 = 4
VMEM_LIMIT = 56 * 1024 * 1024


def _cp(sem, vmem=VMEM_LIMIT):
    return pltpu.CompilerParams(dimension_semantics=sem, vmem_limit_bytes=vmem)


def _sigmoid(x):
    return jax.nn.sigmoid(x)


def _silu(x):
    return x * jax.nn.sigmoid(x)


def _softplus(x):
    return jnp.maximum(x, 0.0) + jnp.log(1.0 + jnp.exp(-jnp.abs(x)))


def _gelu_tanh(x):
    return 0.5 * x * (1.0 + jnp.tanh(math.sqrt(2.0 / math.pi) * (x + 0.044715 * (x * x * x))))


def _rms(x, axis=-1):
    return x * lax.rsqrt(jnp.mean(x * x, axis=axis, keepdims=True) + EPS)


def _dot(a, b):
    return jnp.dot(a.astype(BF16), b.astype(BF16), preferred_element_type=F32)


def _dot_hi(a, b):
    return jnp.dot(a, b, precision=HI, preferred_element_type=F32)


def _dot_nt(a, b):
    return lax.dot_general(a.astype(BF16), b.astype(BF16), (((1,), (1,)), ((), ())),
                           preferred_element_type=F32)


def _mod_kernel(c_ref, w_ref, b_ref, o_ref):
    c = c_ref[...]
    o_ref[0] = _dot_hi(_silu(c), w_ref[0]) + b_ref[0]


def _modulation(cc, mod_w, mod_b):
    depth, d, n = mod_w.shape
    tn = 1536
    return pl.pallas_call(
        _mod_kernel,
        grid=(depth, n // tn),
        in_specs=[pl.BlockSpec((8, d), lambda l, j: (0, 0)),
                  pl.BlockSpec((1, d, tn), lambda l, j: (l, 0, j)),
                  pl.BlockSpec((1, 1, tn), lambda l, j: (l, 0, j))],
        out_specs=pl.BlockSpec((1, 8, tn), lambda l, j: (l, 0, j)),
        out_shape=jax.ShapeDtypeStruct((depth, 8, n), F32),
        compiler_params=_cp(("arbitrary", "arbitrary")),
        name="adaln_mod",
    )(cc, mod_w, mod_b.reshape(depth, 1, n))


def _inproj_kernel(*refs, fuse_prev):
    if fuse_prev:
        x_ref, mo_ref, pmod_ref, mod_ref, g_ref, w_ref, xo_ref, p_ref = refs
        x = x_ref[0] + pmod_ref[0, 0][5:6, :] * mo_ref[0]
        xo_ref[0] = x
    else:
        x_ref, mod_ref, g_ref, w_ref, p_ref = refs
        x = x_ref[0]
    mod = mod_ref[0, 0]
    h = _rms(x) * g_ref[...] * (1.0 + mod[1:2, :]) + mod[0:1, :]
    p_ref[0] = _dot(h, w_ref[...])


def _seg_map(b, i):
    return (b, jnp.minimum(i, 1), 0, 0)


def _inproj(x, mod, g, w, prev=None):
    bsz, L, d = x.shape
    n = w.shape[1]
    nt = L // TM
    tok = pl.BlockSpec((1, TM, d), lambda b, i: (b, i, 0))
    modspec = pl.BlockSpec((1, 1, 6, d), _seg_map)
    tail = [modspec, pl.BlockSpec((1, d), lambda b, i: (0, 0)), pl.BlockSpec((d, n), lambda b, i: (0, 0))]
    pspec = pl.BlockSpec((1, TM, n), lambda b, i: (b, i, 0))
    pshape = jax.ShapeDtypeStruct((bsz, L, n), F32)
    if prev is None:
        return pl.pallas_call(
            functools.partial(_inproj_kernel, fuse_prev=False),
            grid=(bsz, nt), in_specs=[tok] + tail, out_specs=pspec, out_shape=pshape,
            compiler_params=_cp(("parallel", "parallel")), name="inproj",
        )(x, mod, g.reshape(1, d), w)
    moe_out, pmod = prev
    return pl.pallas_call(
        functools.partial(_inproj_kernel, fuse_prev=True),
        grid=(bsz, nt), in_specs=[tok, tok, modspec] + tail,
        out_specs=[tok, pspec], out_shape=[jax.ShapeDtypeStruct(x.shape, F32), pshape],
        compiler_params=_cp(("parallel", "parallel")), name="inproj_res",
    )(x, moe_out, pmod, mod, g.reshape(1, d), w)


def _conv_tile(ref, i, nt, cw, cb, width=TM):
    L = nt * width
    t0 = pl.multiple_of(i * width, width)
    cur = ref[0, pl.ds(t0, width), :]
    prev = ref[0, pl.ds(pl.multiple_of(jnp.maximum(t0 - 8, 0), 8), 8), :]
    nxt = ref[0, pl.ds(pl.multiple_of(jnp.minimum(t0 + width, L - 8), 8), 8), :]
    return _conv_vals(cur, prev, nxt, i, nt, cw, cb, width, first_lat=TM // width)


def _conv_vals(cur, prev, nxt, i, nt, cw, cb, width, first_lat):
    prev = jnp.where((i != 0) & (i != first_lat), prev, 0.0)
    nxt = jnp.where((i != first_lat - 1) & (i != nt - 1), nxt, 0.0)
    cat = jnp.concatenate([prev, cur, nxt], axis=0)
    return (cw[0:1] * cat[6:6 + width] + cw[1:2] * cat[7:7 + width] + cw[2:3] * cur
            + cw[3:4] * cat[9:9 + width] + cb)


def _lru_kernel(ux_ref, cw_ref, cb_ref, gw_ref, gb_ref, lam_ref, o_ref, *, nt):
    cw = cw_ref[...]
    cb = cb_ref[...]
    lam = lam_ref[0]
    sp = _softplus(-lam)
    row = lax.broadcasted_iota(jnp.int32, (TM, 128), 0) & 7

    def gates(i, d):
        xl = _conv_tile(ux_ref, i, nt, cw, cb)
        g = _dot(xl, gw_ref[0, d]) + gb_ref[0, d]
        r = _sigmoid(g[:, :128])
        ig = _sigmoid(g[:, 128:])
        log_a = -LRU_C * r * sp[d:d + 1]
        a = jnp.exp(log_a)
        u = jnp.sqrt(1.0 - a * a) * (ig * xl)
        return a, u

    def scan_tile(i, d, carry, accumulate):
        a, u = gates(i, d)
        rev = d == 1
        for k in (1, 2, 4):
            sh = TM - k if rev else k
            ok = (row < 8 - k) if rev else (row >= k)
            ash = pltpu.roll(a, sh, 0)
            ush = pltpu.roll(u, sh, 0)
            u = jnp.where(ok, u + a * ush, u)
            a = jnp.where(ok, a * ash, a)
        t0 = i * TM
        groups = range(TM // 8)
        for s in (reversed(groups) if rev else groups):
            h = u[s * 8:(s + 1) * 8] + a[s * 8:(s + 1) * 8] * carry
            carry = h[0:1] if rev else h[7:8]
            idx = pl.ds(pl.multiple_of(t0 + s * 8, 8), 8)
            if accumulate:
                o_ref[0, idx, :] += h
            else:
                o_ref[0, idx, :] = h
        return carry

    zero = jnp.zeros((1, 128), F32)
    lax.fori_loop(0, nt, lambda i, c: scan_tile(i, 0, c, False), zero)
    lax.fori_loop(0, nt, lambda j, c: scan_tile(jnp.where(j == 0, 0, nt - j), 1, c, True), zero)


def _lru(proj, conv_w, conv_b, wa, ba, wx, bx, lam):
    bsz, L, _ = proj.shape
    nt = L // TM
    width = conv_w.shape[1]
    ng = width // 128

    def blockdiag(w):
        w = w.reshape(2, ng, 2, 64, 64)
        z = jnp.zeros_like(w[:, :, 0])
        top = jnp.concatenate([w[:, :, 0], z], axis=-1)
        bot = jnp.concatenate([z, w[:, :, 1]], axis=-1)
        return jnp.concatenate([top, bot], axis=-2)
    gw = jnp.concatenate([blockdiag(wa), blockdiag(wx)], axis=-1).transpose(1, 0, 2, 3).astype(BF16)
    gb = jnp.concatenate([ba.reshape(2, ng, 1, 128), bx.reshape(2, ng, 1, 128)], axis=-1).transpose(1, 0, 2, 3)
    lam_g = lam.reshape(2, ng, 128).transpose(1, 0, 2)
    return pl.pallas_call(
        functools.partial(_lru_kernel, nt=nt),
        grid=(bsz, ng),
        in_specs=[pl.BlockSpec((1, L, 128), lambda b, c: (b, 0, c)),
                  pl.BlockSpec((CONV_W, 128), lambda b, c: (0, c)),
                  pl.BlockSpec((1, 128), lambda b, c: (0, c)),
                  pl.BlockSpec((1, 2, 128, 256), lambda b, c: (c, 0, 0, 0)),
                  pl.BlockSpec((1, 2, 1, 256), lambda b, c: (c, 0, 0, 0)),
                  pl.BlockSpec((1, 2, 128), lambda b, c: (c, 0, 0))],
        out_specs=pl.BlockSpec((1, L, 128), lambda b, c: (b, 0, c)),
        out_shape=jax.ShapeDtypeStruct((bsz, L, width), F32),
        compiler_params=_cp(("parallel", "parallel")), name="rglru",
    )(proj, conv_w, conv_b.reshape(1, width), gw, gb, lam_g)


def _na_bias_table(rpb, rows):
    nh = rpb.shape[0]
    win_r = (rpb.shape[1] + 1) // 2
    win_c = (rpb.shape[2] + 1) // 2
    qr = np.arange(NA_RQ)[:, None, None, None]
    qc = np.arange(GRID_W)[None, :, None, None]
    kr = np.arange(NA_RK)[None, None, :, None]
    kc = np.arange(GRID_W)[None, None, None, :]
    cstart = np.clip(qc - win_c // 2, 0, GRID_W - win_c)
    col_ok = (kc >= cstart) & (kc < cstart + win_c)
    dcol = np.clip(kc - qc + (win_c - 1), 0, 2 * win_c - 2)
    oc = (np.arange(2 * win_c - 1)[:, None, None] == dcol[0, :, 0, :][None]).astype(np.float32)
    o_rows, oks = [], []
    for r0, w0 in ((0, 0), (2 * NA_RQ, NA_RQ), (rows - NA_RQ, rows - NA_RK)):
        r = r0 + qr
        kabs = w0 + kr
        rstart = np.clip(r - win_r // 2, 0, rows - win_r)
        oks.append(np.broadcast_to((kabs >= rstart) & (kabs < rstart + win_r) & col_ok,
                                   (NA_RQ, GRID_W, NA_RK, GRID_W)))
        drow = np.clip(kabs - r + (win_r - 1), 0, 2 * win_r - 2)[:, 0, :, 0]
        o_rows.append((np.arange(2 * win_r - 1)[:, None, None] == drow[None]).astype(np.float32))
    t1 = jnp.einsum('hrc,prab->phabc', rpb.astype(F32), jnp.asarray(np.stack(o_rows)), precision=HI)
    b = jnp.einsum('phabc,cqk->phaqbk', t1, jnp.asarray(oc), precision=HI)
    b = jnp.where(jnp.asarray(np.stack(oks))[:, None], b, NEG)
    return b.reshape(3, nh, NA_RQ * GRID_W, NA_RK * GRID_W)


def _na_kernel(q_ref, k_ref, v_ref, bias_ref, qg_ref, kg_ref, o_ref, *, rows, hd):
    i = pl.program_id(2)
    scale = hd ** -0.5
    qg = qg_ref[...] * scale
    kg = kg_ref[...]
    q2 = q_ref[0]
    nkeys = NA_RK * GRID_W

    def head_out(hh, with_window, kwin, vwin, kctx, vctx):
        sl = slice(hh * hd, (hh + 1) * hd)
        qn = _rms(q2[:, sl]) * qg
        kc = _rms(kctx[:, sl]) * kg
        s_c = _dot_nt(qn, kc)
        m = jnp.max(s_c, axis=-1, keepdims=True)
        if with_window:
            kw = _rms(kwin[:, sl]) * kg
            s_w = _dot_nt(qn, kw) + bias_ref[0, hh]
            m = jnp.maximum(m, jnp.max(s_w, axis=-1, keepdims=True))
            p_w = jnp.exp(s_w - m)
        p_c = jnp.exp(s_c - m)
        den = jnp.sum(p_c, axis=-1, keepdims=True)
        num = _dot(p_c, vctx[:, sl])
        if with_window:
            den = den + jnp.sum(p_w, axis=-1, keepdims=True)
            num = num + _dot(p_w, vwin[:, sl])
        return num / den

    kctx = k_ref[0, 0:TM, :]
    vctx = v_ref[0, 0:TM, :]

    @pl.when(i == 0)
    def _():
        o_ref[0] = jnp.concatenate([head_out(hh, False, None, None, kctx, vctx) for hh in range(2)], axis=-1)

    @pl.when(i > 0)
    def _():
        r0 = (i - 1) * NA_RQ
        w0 = jnp.clip(r0 - NA_RQ, 0, rows - NA_RK)
        start = pl.multiple_of(TM + w0 * GRID_W, GRID_W)
        kwin = k_ref[0, pl.ds(start, nkeys), :]
        vwin = v_ref[0, pl.ds(start, nkeys), :]
        o_ref[0] = jnp.concatenate([head_out(hh, True, kwin, vwin, kctx, vctx) for hh in range(2)], axis=-1)


def _na(proj, q_g, k_g, rpb, col0):
    bsz, L, _ = proj.shape
    nh = rpb.shape[0]
    hd = q_g.shape[0]
    width = nh * hd
    rows = (L - TM) // GRID_W
    nb = L // TM
    bias = _na_bias_table(rpb, rows)
    qb, kb, vb = col0 // 128, (col0 + width) // 128, (col0 + 2 * width) // 128

    def pat(i):
        return jnp.where(i <= 1, 0, jnp.where(i == nb - 1, 2, 1))
    return pl.pallas_call(
        functools.partial(_na_kernel, rows=rows, hd=hd),
        grid=(bsz, width // 128, nb),
        in_specs=[pl.BlockSpec((1, TM, 128), lambda b, h, i: (b, i, qb + h)),
                  pl.BlockSpec((1, L, 128), lambda b, h, i: (b, 0, kb + h)),
                  pl.BlockSpec((1, L, 128), lambda b, h, i: (b, 0, vb + h)),
                  pl.BlockSpec((1, 2, TM, NA_RK * GRID_W), lambda b, h, i: (pat(i), h, 0, 0)),
                  pl.BlockSpec((1, hd), lambda b, h, i: (0, 0)),
                  pl.BlockSpec((1, hd), lambda b, h, i: (0, 0))],
        out_specs=pl.BlockSpec((1, TM, 128), lambda b, h, i: (b, i, h)),
        out_shape=jax.ShapeDtypeStruct((bsz, L, width), F32),
        compiler_params=_cp(("parallel", "parallel", "arbitrary")), name="nbr_attn",
    )(proj, proj, proj, bias, q_g.reshape(1, hd), k_g.reshape(1, hd))


def _route(lg):
    lane = lax.broadcasted_iota(jnp.int32, lg.shape, 1)
    lane_f = lane.astype(F32)
    is_g = lane < EXPERT_LANE0
    gl = jnp.where(is_g, lg, NEG)
    gmax = jnp.max(gl, axis=-1, keepdims=True)
    gsel = jnp.min(jnp.where(is_g & (gl == gmax), lane_f, 1e9), axis=-1, keepdims=True)
    g_w = 1.0 / jnp.sum(jnp.where(is_g, jnp.exp(gl - gmax), 0.0), axis=-1, keepdims=True)
    grp = ((lane - EXPERT_LANE0) >> 2).astype(F32)
    in_g = (lane >= EXPERT_LANE0) & (lane < EXPERT_LANE0 + N_EXPERTS) & (grp == gsel)
    el = jnp.where(in_g, lg, NEG)
    v1 = jnp.max(el, axis=-1, keepdims=True)
    i1 = jnp.min(jnp.where(in_g & (el == v1), lane_f, 1e9), axis=-1, keepdims=True)
    el2 = jnp.where(lane_f == i1, NEG, el)
    v2 = jnp.max(el2, axis=-1, keepdims=True)
    i2 = jnp.min(jnp.where(in_g & (lane_f != i1) & (el2 == v2), lane_f, 1e9), axis=-1, keepdims=True)
    t = jnp.exp(v2 - v1)
    w1 = g_w / (1.0 + t)
    w2 = g_w * t / (1.0 + t)
    return jnp.where(lane_f == i1, w1, 0.0) + jnp.where(lane_f == i2, w2, 0.0)


def _outproj_kernel(*refs, even):
    if even:
        (r_ref, ug_ref, a_ref, x_ref, w_ref, mod_ref, g2_ref, rw_ref, x1_ref, h2_ref, gate_ref) = refs
        y_in = jnp.concatenate([r_ref[0] * _gelu_tanh(ug_ref[0]), a_ref[0]], axis=-1)
    else:
        (ys_ref, z_ref, hm_ref, mo_ref, sg_ref, mg_ref, x_ref, w_ref, mod_ref, g2_ref, rw_ref,
         x1_ref, h2_ref, gate_ref) = refs
        ys = ys_ref[0] * _silu(z_ref[0])
        sg = sg_ref[...]
        mg = mg_ref[...]
        hm = hm_ref[0]
        sig_o = _sigmoid(mo_ref[0])
        gw = ys.shape[-1] // 2
        parts = [_rms(ys[:, g * gw:(g + 1) * gw]) * sg[:, g * gw:(g + 1) * gw] for g in range(2)]
        hw = 128
        parts += [_rms(hm[:, h * hw:(h + 1) * hw]) * mg[:, h * hw:(h + 1) * hw] * sig_o[:, h * hw:(h + 1) * hw]
                  for h in range(hm.shape[-1] // hw)]
        y_in = jnp.concatenate(parts, axis=-1)
    mod = mod_ref[0, 0]
    x1 = x_ref[0] + mod[2:3, :] * _dot(y_in, w_ref[...])
    x1_ref[0] = x1
    h2 = _rms(x1) * g2_ref[...] * (1.0 + mod[4:5, :]) + mod[3:4, :]
    h2_ref[0] = h2.astype(BF16)
    gate_ref[0] = _route(_dot_hi(h2, rw_ref[...]))


def _outproj(mix_inputs, x, w, mod, g2, rw, even, tile0, ntiles):
    bsz, _, d = x.shape
    specs, args = [], []
    for arr, cb, wdt in mix_inputs:
        if arr.ndim == 3:
            specs.append(pl.BlockSpec((1, TM, wdt), lambda b, i, cb=cb: (b, i + tile0, cb)))
        else:
            specs.append(pl.BlockSpec((1, wdt), lambda b, i: (0, 0)))
        args.append(arr)
    specs += [pl.BlockSpec((1, TM, d), lambda b, i: (b, i + tile0, 0)),
              pl.BlockSpec(w.shape, lambda b, i: (0, 0)),
              pl.BlockSpec((1, 1, 6, d), lambda b, i: (b, jnp.minimum(i + tile0, 1), 0, 0)),
              pl.BlockSpec((1, d), lambda b, i: (0, 0)),
              pl.BlockSpec(rw.shape, lambda b, i: (0, 0))]
    args += [x, w, mod, g2.reshape(1, d), rw]
    lo = ntiles * TM
    return pl.pallas_call(
        functools.partial(_outproj_kernel, even=even),
        grid=(bsz, ntiles), in_specs=specs,
        out_specs=[pl.BlockSpec((1, TM, d), lambda b, i: (b, i, 0)),
                   pl.BlockSpec((1, TM, d), lambda b, i: (b, i, 0)),
                   pl.BlockSpec((1, TM, 128), lambda b, i: (b, i, 0))],
        out_shape=[jax.ShapeDtypeStruct((bsz, lo, d), F32), jax.ShapeDtypeStruct((bsz, lo, d), BF16),
                   jax.ShapeDtypeStruct((bsz, lo, 128), F32)],
        compiler_params=_cp(("parallel", "parallel")), name="outproj_even" if even else "outproj_odd",
    )(*args)


def _moe_dense_kernel(h_ref, g_ref, w1_ref, w3_ref, w2_ref, o_ref):
    e = pl.program_id(1)
    h = h_ref[...]
    a = jnp.dot(h, w1_ref[0], preferred_element_type=F32)
    b = jnp.dot(h, w3_ref[0], preferred_element_type=F32)
    y = _dot(_silu(a) * b, w2_ref[0])
    g = g_ref[...]
    lane = lax.broadcasted_iota(jnp.int32, g.shape, 1)
    ge = jnp.sum(jnp.where(lane == e + EXPERT_LANE0, g, 0.0), axis=-1, keepdims=True)

    @pl.when(e == 0)
    def _():
        o_ref[...] = ge * y

    @pl.when(e > 0)
    def _():
        o_ref[...] += ge * y


def _moe_dense(h2, gate, w1, w3, w2):
    t, d = h2.shape
    ne, _, ff = w1.shape
    tm = math.gcd(t, 1024)
    return pl.pallas_call(
        _moe_dense_kernel,
        grid=(t // tm, ne),
        in_specs=[pl.BlockSpec((tm, d), lambda i, e: (i, 0)),
                  pl.BlockSpec((tm, 128), lambda i, e: (i, 0)),
                  pl.BlockSpec((1, d, ff), lambda i, e: (e, 0, 0)),
                  pl.BlockSpec((1, d, ff), lambda i, e: (e, 0, 0)),
                  pl.BlockSpec((1, ff, d), lambda i, e: (e, 0, 0))],
        out_specs=pl.BlockSpec((tm, d), lambda i, e: (i, 0)),
        out_shape=jax.ShapeDtypeStruct((t, d), F32),
        compiler_params=_cp(("parallel", "arbitrary")), name="moe_dense",
    )(h2, gate, w1, w3, w2)


def _tri(q, rev):
    r = lax.broadcasted_iota(jnp.int32, (q, q), 0)
    c = lax.broadcasted_iota(jnp.int32, (q, q), 1)
    return (c >= r) if rev else (c <= r)


def _ssd_kernel(xs_ref, b_ref, c_ref, dt_ref, cwx_ref, cbx_ref, cwb_ref, cbb_ref, cwc_ref, cbc_ref,
                dtb_ref, alog_ref, dsk_ref, sel_ref, o_ref, xc_s, bc_s, cc_s, dt_s, cumt_s, h_s, *, nc):
    g = pl.program_id(1)
    q = CHUNK
    first_lat = TM // q

    dtb = dtb_ref[...]

    def prep(i, _):
        t0 = pl.multiple_of(i * q, q)
        idx = pl.ds(t0, q)
        for src, cw, cb, dst in ((xs_ref, cwx_ref, cbx_ref, xc_s), (b_ref, cwb_ref, cbb_ref, bc_s),
                                 (c_ref, cwc_ref, cbc_ref, cc_s)):
            L = nc * q
            cur = src[0, idx, :]
            prev = src[0, pl.ds(pl.multiple_of(jnp.maximum(t0 - 8, 0), 8), 8), :]
            nxt = src[0, pl.ds(pl.multiple_of(jnp.minimum(t0 + q, L - 8), 8), 8), :]
            dst[idx, :] = _silu(_conv_vals(cur, prev, nxt, i, nc, cw[...], cb[...], q, first_lat))
        dt_s[idx, :] = _softplus(dt_ref[0, idx, :] + dtb)
        o_ref[0, idx, :] = dsk_ref[...] * xc_s[idx, :]
        return 0
    lax.fori_loop(0, nc, prep, 0)

    a_lane = -jnp.exp(alog_ref[...])
    lane256 = lax.broadcasted_iota(jnp.int32, (q, 256), 1) // 64

    for d in (0, 1):
        rev = d == 1
        causal = _tri(q, rev)
        tri = causal.astype(F32)
        sel = sel_ref[d, 0]
        last = 0 if rev else q - 1
        h_s[...] = jnp.zeros_like(h_s)

        def chunk(j, _):
            if rev:
                ci = jnp.where(j < first_lat, first_lat - 1 - j, nc - 1 - (j - first_lat))
            else:
                ci = j
            idx = pl.ds(pl.multiple_of(ci * q, q), q)
            dt_all = dt_s[idx, :]
            cum = _dot_hi(tri, dt_all * a_lane)
            cumt_s[...] = cum.T
            dt_e = _dot_hi(dt_all, sel)
            cum_e = _dot_hi(cum, sel)
            tot_e = cum_e[last:last + 1, :]
            xc = xc_s[idx, :]
            bc = bc_s[idx, :]
            cc = cc_s[idx, :]
            dtx = xc * dt_e
            cb = _dot_nt(cc, bc)
            y = jnp.exp(cum_e) * _dot(cc, h_s[...])
            for r in range(4):
                lr = d * 8 + g * 4 + r
                seg = cum_e[:, r * 64:r * 64 + 1] - cumt_s[pl.ds(lr, 1), :]
                m = cb * jnp.exp(jnp.where(causal, seg, NEG))
                y = y + _dot(m, jnp.where(lane256 == r, dtx, 0.0))
            o_ref[0, idx, :] += y
            states = _dot(bc.T, dtx * jnp.exp(tot_e - cum_e))
            h_s[...] = jnp.exp(tot_e) * h_s[...] + states
            return 0
        lax.fori_loop(0, nc, chunk, 0)


def _ssd(proj, conv_w, conv_b, dt_bias, a_log, d_skip, cols):
    bsz, L, _ = proj.shape
    nc = L // CHUNK
    cx, cbm, ccm, cdt = cols
    nh = d_skip.shape[0]
    dtb = jnp.zeros((1, 128), F32).at[0, :2 * nh].set(dt_bias.reshape(-1))
    alog = jnp.zeros((1, 128), F32).at[0, :2 * nh].set(a_log.reshape(-1))
    dsk = jnp.repeat(d_skip.astype(F32), 64).reshape(1, nh * 64)
    sel = np.zeros((2, 2, 128, 256), np.float32)
    for d in range(2):
        for g in range(2):
            for r in range(4):
                sel[d, g, d * nh + g * 4 + r, r * 64:(r + 1) * 64] = 1.0
    cw_specs = []
    for width, off in ((256, 0), (128, 512), (128, 768)):
        cw_specs += [pl.BlockSpec((CONV_W, width), lambda b, g, off=off, width=width: (0, off // width + g)),
                     pl.BlockSpec((1, width), lambda b, g, off=off, width=width: (0, off // width + g))]
    return pl.pallas_call(
        functools.partial(_ssd_kernel, nc=nc),
        grid=(bsz, 2),
        in_specs=[pl.BlockSpec((1, L, 256), lambda b, g: (b, 0, cx // 256 + g)),
                  pl.BlockSpec((1, L, 128), lambda b, g: (b, 0, cbm // 128 + g)),
                  pl.BlockSpec((1, L, 128), lambda b, g: (b, 0, ccm // 128 + g)),
                  pl.BlockSpec((1, L, 128), lambda b, g: (b, 0, cdt // 128))] + cw_specs + [
                  pl.BlockSpec((1, 128), lambda b, g: (0, 0)),
                  pl.BlockSpec((1, 128), lambda b, g: (0, 0)),
                  pl.BlockSpec((1, 256), lambda b, g: (0, g)),
                  pl.BlockSpec((2, 1, 128, 256), lambda b, g: (0, g, 0, 0))],
        out_specs=pl.BlockSpec((1, L, 256), lambda b, g: (b, 0, g)),
        out_shape=jax.ShapeDtypeStruct((bsz, L, nh * 64), F32),
        scratch_shapes=[pltpu.VMEM((L, 256), F32), pltpu.VMEM((L, 128), F32), pltpu.VMEM((L, 128), F32),
                        pltpu.VMEM((L, 128), F32), pltpu.VMEM((128, CHUNK), F32), pltpu.VMEM((128, 256), F32)],
        compiler_params=_cp(("parallel", "arbitrary")), name="ssd",
    )(proj, proj, proj, proj, conv_w, conv_b.reshape(1, -1), conv_w, conv_b.reshape(1, -1),
      conv_w, conv_b.reshape(1, -1), dtb, alog, dsk, jnp.asarray(sel))


def _mlstm_kernel(q_ref, k_ref, v_ref, g_ref, cwq_ref, cbq_ref, cwk_ref, cbk_ref, gb_ref, sel_ref, o_ref,
                  qc_s, kc_s, c_s, n_s, m_s, *, nc, dh):
    q = CHUNK
    first_lat = TM // q

    def prep(i, _):
        t0 = pl.multiple_of(i * q, q)
        idx = pl.ds(t0, q)
        L = nc * q
        for src, cw, cb, dst, mul in ((q_ref, cwq_ref, cbq_ref, qc_s, 1.0), (k_ref, cwk_ref, cbk_ref, kc_s, dh ** -0.5)):
            cur = src[0, idx, :]
            prev = src[0, pl.ds(pl.multiple_of(jnp.maximum(t0 - 8, 0), 8), 8), :]
            nxt = src[0, pl.ds(pl.multiple_of(jnp.minimum(t0 + q, L - 8), 8), 8), :]
            dst[idx, :] = _silu(_conv_vals(cur, prev, nxt, i, nc, cw[...], cb[...], q, first_lat)) * mul
        return 0
    lax.fori_loop(0, nc, prep, 0)

    gb = gb_ref[...]
    lane = lax.broadcasted_iota(jnp.int32, (q, 128), 1)
    for d in (0, 1):
        rev = d == 1
        causal = _tri(q, rev)
        tri = causal.astype(F32)
        sel = sel_ref[d, 0]
        last = 0 if rev else q - 1
        c_s[...] = jnp.zeros_like(c_s)
        n_s[...] = jnp.zeros_like(n_s)
        m_s[...] = jnp.full_like(m_s, NEG)

        def chunk(j, _):
            if rev:
                ci = jnp.where(j < first_lat, first_lat - 1 - j, nc - 1 - (j - first_lat))
            else:
                ci = j
            idx = pl.ds(pl.multiple_of(ci * q, q), q)
            gsel = _dot_hi(g_ref[0, idx, :] + gb, sel)
            logf = jnp.minimum(gsel, 0.0) - jnp.log(1.0 + jnp.exp(-jnp.abs(gsel)))
            gi = jnp.where(lane == 0, gsel, jnp.where(lane == 1, logf, 0.0))
            cum = _dot_hi(tri, gi)
            git = gi.T
            cumt = cum.T
            i_col, i_row = gi[:, 0:1], git[0:1, :]
            b_col, b_row = cum[:, 1:2], cumt[1:2, :]
            m_prev = m_s[...]
            dlog = jnp.where(causal, b_col - b_row + i_row, NEG)
            inter = b_col + m_prev
            m_t = jnp.maximum(jnp.max(dlog, axis=-1, keepdims=True), inter)
            qc = qc_s[idx, :]
            kc = kc_s[idx, :]
            vc = v_ref[0, idx, :]
            kct = kc.T
            w = jnp.exp(dlog - m_t) * _dot(qc, kct)
            g_in = jnp.exp(inter - m_t)
            num = _dot(w, vc) + g_in * _dot(qc, c_s[...])
            den = jnp.sum(w, axis=-1, keepdims=True) + g_in * jnp.sum(qc * n_s[...], axis=-1, keepdims=True)
            o_ref[0, idx, :] += num / jnp.maximum(jnp.abs(den), jnp.exp(-m_t))
            b_end = b_col[last:last + 1, :]
            g_s = b_end - b_col + i_col
            m_new = jnp.maximum(jnp.max(g_s, axis=0, keepdims=True), b_end + m_prev)
            w_s = jnp.exp(g_s - m_new)
            keep = jnp.exp(b_end + m_prev - m_new)
            c_s[...] = keep * c_s[...] + _dot(kct, w_s * vc)
            n_s[...] = keep * n_s[...] + jnp.sum(w_s * kc, axis=0, keepdims=True)
            m_s[...] = m_new
            return 0

        if d == 0:
            o_ref[...] = jnp.zeros_like(o_ref)
        lax.fori_loop(0, nc, chunk, 0)


def _mlstm(proj, conv_w, conv_b, gate_b, cols):
    bsz, L, _ = proj.shape
    nc = L // CHUNK
    cq, ck, cv, cg = cols
    nh = gate_b.shape[-1]
    dh = conv_w.shape[1] // (2 * nh)
    gbl = jnp.zeros((1, 128), F32).at[0, 16:16 + 4 * nh].set(gate_b.reshape(-1))
    sel = np.zeros((2, nh, 128, 128), np.float32)
    for d in range(2):
        for h in range(nh):
            sel[d, h, 16 + d * 2 * nh + h, 0] = 1.0
            sel[d, h, 16 + d * 2 * nh + nh + h, 1] = 1.0
    return pl.pallas_call(
        functools.partial(_mlstm_kernel, nc=nc, dh=dh),
        grid=(bsz, nh),
        in_specs=[pl.BlockSpec((1, L, dh), lambda b, h: (b, 0, cq // dh + h)),
                  pl.BlockSpec((1, L, dh), lambda b, h: (b, 0, ck // dh + h)),
                  pl.BlockSpec((1, L, dh), lambda b, h: (b, 0, cv // dh + h)),
                  pl.BlockSpec((1, L, 128), lambda b, h: (b, 0, cg // 128)),
                  pl.BlockSpec((CONV_W, dh), lambda b, h: (0, h)),
                  pl.BlockSpec((1, dh), lambda b, h: (0, h)),
                  pl.BlockSpec((CONV_W, dh), lambda b, h: (0, nh + h)),
                  pl.BlockSpec((1, dh), lambda b, h: (0, nh + h)),
                  pl.BlockSpec((1, 128), lambda b, h: (0, 0)),
                  pl.BlockSpec((2, 1, 128, 128), lambda b, h: (0, h, 0, 0))],
        out_specs=pl.BlockSpec((1, L, dh), lambda b, h: (b, 0, h)),
        out_shape=jax.ShapeDtypeStruct((bsz, L, nh * dh), F32),
        scratch_shapes=[pltpu.VMEM((L, dh), F32), pltpu.VMEM((L, dh), F32), pltpu.VMEM((dh, dh), F32),
                        pltpu.VMEM((1, dh), F32), pltpu.VMEM((1, 1), F32)],
        compiler_params=_cp(("parallel", "arbitrary")), name="mlstm",
    )(proj, proj, proj, proj, conv_w, conv_b.reshape(1, -1), conv_w, conv_b.reshape(1, -1), gbl, jnp.asarray(sel))


def _final_kernel(x_ref, o_ref, mod_ref, y_ref):
    y_ref[0] = x_ref[0] + mod_ref[0, 0][5:6, :] * o_ref[0]


def _final(x1, moe_out, mod):
    bsz, S, d = x1.shape
    tok = pl.BlockSpec((1, TM, d), lambda b, i: (b, i, 0))
    return pl.pallas_call(
        _final_kernel, grid=(bsz, S // TM),
        in_specs=[tok, tok, pl.BlockSpec((1, 1, 6, d), lambda b, i: (b, 1, 0, 0))],
        out_specs=tok, out_shape=jax.ShapeDtypeStruct(x1.shape, F32),
        compiler_params=_cp(("parallel", "parallel")), name="final_residual",
    )(x1, moe_out, mod)


def kernel(x, c, ctx, c_ctx, mod_w, mod_b, norm1_g, norm2_g, even_w_in, even_w_out, lru_conv_w, lru_conv_b, lru_wa, lru_ba, lru_wx, lru_bx, lru_lam, na_q_g, na_k_g, na_rpb, odd_w_in, odd_w_out, ssd_conv_w, ssd_conv_b, ssd_dt_bias, ssd_a_log, ssd_d, ssd_norm_g, ml_conv_w, ml_conv_b, ml_gate_b, ml_norm_g, moe_router_g, moe_router_e, moe_w1, moe_w3, moe_w2):
    bsz, S, d = x.shape
    lc = ctx.shape[1]
    assert lc == TM and S % TM == 0 and mod_w.shape[0] == 2
    nt = (lc + S) // TM
    L = lc + S

    cc = jnp.zeros((8, d), F32).at[0].set(c_ctx).at[1:1 + bsz].set(c)
    mod_all = _modulation(cc, mod_w, mod_b)

    def mod_for(l):
        m = mod_all[l].reshape(8, 6, d)
        return jnp.stack([jnp.broadcast_to(m[0], (bsz, 6, d)), m[1:1 + bsz]], axis=1)

    def router_w(l):
        return jnp.zeros((d, 128), F32).at[:, :EXPERT_LANE0].set(moe_router_g[l]) \
            .at[:, EXPERT_LANE0:EXPERT_LANE0 + N_EXPERTS].set(moe_router_e[l])

    xx = jnp.concatenate([ctx, x], axis=1)

    mod0 = mod_for(0)
    proj = _inproj(xx, mod0, norm1_g[0], even_w_in[0].astype(BF16))
    lw = lru_conv_w.shape[-1]
    r = _lru(proj, lru_conv_w[0], lru_conv_b[0], lru_wa[0], lru_ba[0], lru_wx[0], lru_bx[0], lru_lam[0])
    a = _na(proj, na_q_g[0], na_k_g[0], na_rpb[0], col0=2 * lw)
    x1, h2, gate = _outproj([(r, 0, lw), (proj, 1, lw), (a, 0, lw)], xx, even_w_out[0].astype(BF16), mod0,
                            norm2_g[0], router_w(0), even=True, tile0=0, ntiles=nt)
    moe0 = _moe_dense(h2.reshape(bsz * L, d), gate.reshape(bsz * L, 128), moe_w1[0].astype(BF16),
                      moe_w3[0].astype(BF16), moe_w2[0].astype(BF16)).reshape(bsz, L, d)

    mod1 = mod_for(1)
    sw = ssd_d.shape[-1] * 64
    xbc = ssd_conv_w.shape[-1]
    mw = ml_norm_g.shape[-1]
    w = odd_w_in[0]
    o = np.cumsum([0, sw, xbc, 2 * ssd_d.shape[-1], mw, mw, mw, mw])
    small = jnp.concatenate([w[:, o[2]:o[3]], w[:, o[7]:], jnp.zeros((d, 128 - 2 * ssd_d.shape[-1] - (w.shape[1] - o[7])), F32)], axis=1)
    w_odd = jnp.concatenate([w[:, :o[2]], w[:, o[3]:o[7]], small], axis=1).astype(BF16)
    cz, cxs = 0, sw
    cB, cC = cxs + sw, cxs + sw + (xbc - sw) // 2
    cq = sw + xbc
    ck, cv, co, csm = cq + mw, cq + 2 * mw, cq + 3 * mw, cq + 4 * mw
    x0, proj1 = _inproj(x1, mod1, norm1_g[1], w_odd, prev=(moe0, mod0))
    ys = _ssd(proj1, ssd_conv_w[0], ssd_conv_b[0], ssd_dt_bias[0], ssd_a_log[0], ssd_d[0], (cxs, cB, cC, csm))
    hm = _mlstm(proj1, ml_conv_w[0], ml_conv_b[0], ml_gate_b[0], (cq, ck, cv, csm))
    x2, h2b, gate1 = _outproj([(ys, 0, sw), (proj1, cz // sw, sw), (hm, 0, mw), (proj1, co // mw, mw),
                               (ssd_norm_g[0].reshape(1, sw), 0, sw), (ml_norm_g[0].reshape(1, mw), 0, mw)],
                              x0, odd_w_out[0].astype(BF16), mod1, norm2_g[1], router_w(1),
                              even=False, tile0=1, ntiles=nt - 1)
    moe1 = _moe_dense(h2b.reshape(bsz * S, d), gate1.reshape(bsz * S, 128), moe_w1[1].astype(BF16),
                      moe_w3[1].astype(BF16), moe_w2[1].astype(BF16)).reshape(bsz, S, d)
    return _final(x2, moe1, mod1)
```

```python
import functools
import math

import jax
import jax.numpy as jnp
import numpy as np
from jax import lax
from jax.experimental import pallas as pl
from jax.experimental.pallas import tpu as pltpu

F32 = jnp.float32
BF16 = jnp.bfloat16
HI = lax.Precision.HIGHEST

EPS = 1e-6
NEG = -1e30
GRID_W = 64
CONV_W = 4
LRU_C = 8.0
TM = 256
CHUNK = 128
NA_RQ = 4
NA_RK = 12
N_EXPERTS = 16
EXPERT_LANE0 = 4
VMEM_LIMIT = 56 * 1024 * 1024


def _cp(sem, vmem=VMEM_LIMIT):
    return pltpu.CompilerParams(dimension_semantics=sem, vmem_limit_bytes=vmem)


def _sigmoid(x):
    return jax.nn.sigmoid(x)


def _silu(x):
    return x * jax.nn.sigmoid(x)


def _softplus(x):
    return jnp.maximum(x, 0.0) + jnp.log(1.0 + jnp.exp(-jnp.abs(x)))


def _gelu_tanh(x):
    return 0.5 * x * (1.0 + jnp.tanh(math.sqrt(2.0 / math.pi) * (x + 0.044715 * (x * x * x))))


def _rms(x, axis=-1):
    return x * lax.rsqrt(jnp.mean(x * x, axis=axis, keepdims=True) + EPS)


def _dot(a, b):
    return jnp.dot(a.astype(BF16), b.astype(BF16), preferred_element_type=F32)


def _dot_hi(a, b):
    return jnp.dot(a, b, precision=HI, preferred_element_type=F32)


def _dot_nt(a, b):
    return lax.dot_general(a.astype(BF16), b.astype(BF16), (((1,), (1,)), ((), ())),
                           preferred_element_type=F32)


def _mod_kernel(c_ref, w_ref, b_ref, o_ref):
    c = c_ref[...]
    o_ref[0] = _dot_hi(_silu(c), w_ref[0]) + b_ref[0]


def _modulation(cc, mod_w, mod_b):
    depth, d, n = mod_w.shape
    tn = 1536
    return pl.pallas_call(
        _mod_kernel,
        grid=(depth, n // tn),
        in_specs=[pl.BlockSpec((8, d), lambda l, j: (0, 0)),
                  pl.BlockSpec((1, d, tn), lambda l, j: (l, 0, j)),
                  pl.BlockSpec((1, 1, tn), lambda l, j: (l, 0, j))],
        out_specs=pl.BlockSpec((1, 8, tn), lambda l, j: (l, 0, j)),
        out_shape=jax.ShapeDtypeStruct((depth, 8, n), F32),
        compiler_params=_cp(("arbitrary", "arbitrary")),
        name="adaln_mod",
    )(cc, mod_w, mod_b.reshape(depth, 1, n))


def _inproj_kernel(*refs, fuse_prev):
    if fuse_prev:
        x_ref, mo_ref, pmod_ref, mod_ref, g_ref, w_ref, xo_ref, p_ref = refs
        x = x_ref[0] + pmod_ref[0, 0][5:6, :] * mo_ref[0]
        xo_ref[0] = x
    else:
        x_ref, mod_ref, g_ref, w_ref, p_ref = refs
        x = x_ref[0]
    mod = mod_ref[0, 0]
    h = _rms(x) * g_ref[...] * (1.0 + mod[1:2, :]) + mod[0:1, :]
    p_ref[0] = _dot(h, w_ref[...])


def _seg_map(b, i):
    return (b, jnp.minimum(i, 1), 0, 0)


def _inproj(x, mod, g, w, prev=None):
    bsz, L, d = x.shape
    n = w.shape[1]
    nt = L // TM
    tok = pl.BlockSpec((1, TM, d), lambda b, i: (b, i, 0))
    modspec = pl.BlockSpec((1, 1, 6, d), _seg_map)
    tail = [modspec, pl.BlockSpec((1, d), lambda b, i: (0, 0)), pl.BlockSpec((d, n), lambda b, i: (0, 0))]
    pspec = pl.BlockSpec((1, TM, n), lambda b, i: (b, i, 0))
    pshape = jax.ShapeDtypeStruct((bsz, L, n), F32)
    if prev is None:
        return pl.pallas_call(
            functools.partial(_inproj_kernel, fuse_prev=False),
            grid=(bsz, nt), in_specs=[tok] + tail, out_specs=pspec, out_shape=pshape,
            compiler_params=_cp(("parallel", "parallel")), name="inproj",
        )(x, mod, g.reshape(1, d), w)
    moe_out, pmod = prev
    return pl.pallas_call(
        functools.partial(_inproj_kernel, fuse_prev=True),
        grid=(bsz, nt), in_specs=[tok, tok, modspec] + tail,
        out_specs=[tok, pspec], out_shape=[jax.ShapeDtypeStruct(x.shape, F32), pshape],
        compiler_params=_cp(("parallel", "parallel")), name="inproj_res",
    )(x, moe_out, pmod, mod, g.reshape(1, d), w)


def _conv_tile(ref, i, nt, cw, cb, width=TM):
    L = nt * width
    t0 = pl.multiple_of(i * width, width)
    cur = ref[0, pl.ds(t0, width), :]
    prev = ref[0, pl.ds(pl.multiple_of(jnp.maximum(t0 - 8, 0), 8), 8), :]
    nxt = ref[0, pl.ds(pl.multiple_of(jnp.minimum(t0 + width, L - 8), 8), 8), :]
    return _conv_vals(cur, prev, nxt, i, nt, cw, cb, width, first_lat=TM // width)


def _conv_vals(cur, prev, nxt, i, nt, cw, cb, width, first_lat):
    prev = jnp.where((i != 0) & (i != first_lat), prev, 0.0)
    nxt = jnp.where((i != first_lat - 1) & (i != nt - 1), nxt, 0.0)
    cat = jnp.concatenate([prev, cur, nxt], axis=0)
    return (cw[0:1] * cat[6:6 + width] + cw[1:2] * cat[7:7 + width] + cw[2:3] * cur
            + cw[3:4] * cat[9:9 + width] + cb)


def _lru_kernel(ux_ref, cw_ref, cb_ref, gw_ref, gb_ref, lam_ref, o_ref, *, nt):
    cw = cw_ref[...]
    cb = cb_ref[...]
    lam = lam_ref[0]
    sp = _softplus(-lam)
    row = lax.broadcasted_iota(jnp.int32, (TM, 128), 0) & 7

    def gates(i, d):
        xl = _conv_tile(ux_ref, i, nt, cw, cb)
        g = _dot(xl, gw_ref[0, d]) + gb_ref[0, d]
        r = _sigmoid(g[:, :128])
        ig = _sigmoid(g[:, 128:])
        log_a = -LRU_C * r * sp[d:d + 1]
        a = jnp.exp(log_a)
        u = jnp.sqrt(1.0 - a * a) * (ig * xl)
        return a, u

    def scan_tile(i, d, carry, accumulate):
        a, u = gates(i, d)
        rev = d == 1
        for k in (1, 2, 4):
            sh = TM - k if rev else k
            ok = (row < 8 - k) if rev else (row >= k)
            ash = pltpu.roll(a, sh, 0)
            ush = pltpu.roll(u, sh, 0)
            u = jnp.where(ok, u + a * ush, u)
            a = jnp.where(ok, a * ash, a)
        t0 = i * TM
        groups = range(TM // 8)
        for s in (reversed(groups) if rev else groups):
            h = u[s * 8:(s + 1) * 8] + a[s * 8:(s + 1) * 8] * carry
            carry = h[0:1] if rev else h[7:8]
            idx = pl.ds(pl.multiple_of(t0 + s * 8, 8), 8)
            if accumulate:
                o_ref[0, idx, :] += h
            else:
                o_ref[0, idx, :] = h
        return carry

    zero = jnp.zeros((1, 128), F32)
    lax.fori_loop(0, nt, lambda i, c: scan_tile(i, 0, c, False), zero)
    lax.fori_loop(0, nt, lambda j, c: scan_tile(jnp.where(j == 0, 0, nt - j), 1, c, True), zero)


def _lru(proj, conv_w, conv_b, wa, ba, wx, bx, lam):
    bsz, L, _ = proj.shape
    nt = L // TM
    width = conv_w.shape[1]
    ng = width // 128

    def blockdiag(w):
        w = w.reshape(2, ng, 2, 64, 64)
        z = jnp.zeros_like(w[:, :, 0])
        top = jnp.concatenate([w[:, :, 0], z], axis=-1)
        bot = jnp.concatenate([z, w[:, :, 1]], axis=-1)
        return jnp.concatenate([top, bot], axis=-2)
    gw = jnp.concatenate([blockdiag(wa), blockdiag(wx)], axis=-1).transpose(1, 0, 2, 3).astype(BF16)
    gb = jnp.concatenate([ba.reshape(2, ng, 1, 128), bx.reshape(2, ng, 1, 128)], axis=-1).transpose(1, 0, 2, 3)
    lam_g = lam.reshape(2, ng, 128).transpose(1, 0, 2)
    return pl.pallas_call(
        functools.partial(_lru_kernel, nt=nt),
        grid=(bsz, ng),
        in_specs=[pl.BlockSpec((1, L, 128), lambda b, c: (b, 0, c)),
                  pl.BlockSpec((CONV_W, 128), lambda b, c: (0, c)),
                  pl.BlockSpec((1, 128), lambda b, c: (0, c)),
                  pl.BlockSpec((1, 2, 128, 256), lambda b, c: (c, 0, 0, 0)),
                  pl.BlockSpec((1, 2, 1, 256), lambda b, c: (c, 0, 0, 0)),
                  pl.BlockSpec((1, 2, 128), lambda b, c: (c, 0, 0))],
        out_specs=pl.BlockSpec((1, L, 128), lambda b, c: (b, 0, c)),
        out_shape=jax.ShapeDtypeStruct((bsz, L, width), F32),
        compiler_params=_cp(("parallel", "parallel")), name="rglru",
    )(proj, conv_w, conv_b.reshape(1, width), gw, gb, lam_g)


def _na_bias_table(rpb, rows):
    nh = rpb.shape[0]
    win_r = (rpb.shape[1] + 1) // 2
    win_c = (rpb.shape[2] + 1) // 2
    qr = np.arange(NA_RQ)[:, None, None, None]
    qc = np.arange(GRID_W)[None, :, None, None]
    kr = np.arange(NA_RK)[None, None, :, None]
    kc = np.arange(GRID_W)[None, None, None, :]
    cstart = np.clip(qc - win_c // 2, 0, GRID_W - win_c)
    col_ok = (kc >= cstart) & (kc < cstart + win_c)
    dcol = np.clip(kc - qc + (win_c - 1), 0, 2 * win_c - 2)
    oc = (np.arange(2 * win_c - 1)[:, None, None] == dcol[0, :, 0, :][None]).astype(np.float32)
    o_rows, oks = [], []
    for r0, w0 in ((0, 0), (2 * NA_RQ, NA_RQ), (rows - NA_RQ, rows - NA_RK)):
        r = r0 + qr
        kabs = w0 + kr
        rstart = np.clip(r - win_r // 2, 0, rows - win_r)
        oks.append(np.broadcast_to((kabs >= rstart) & (kabs < rstart + win_r) & col_ok,
                                   (NA_RQ, GRID_W, NA_RK, GRID_W)))
        drow = np.clip(kabs - r + (win_r - 1), 0, 2 * win_r - 2)[:, 0, :, 0]
        o_rows.append((np.arange(2 * win_r - 1)[:, None, None] == drow[None]).astype(np.float32))
    t1 = jnp.einsum('hrc,prab->phabc', rpb.astype(F32), jnp.asarray(np.stack(o_rows)), precision=HI)
    b = jnp.einsum('phabc,cqk->phaqbk', t1, jnp.asarray(oc), precision=HI)
    b = jnp.where(jnp.asarray(np.stack(oks))[:, None], b, NEG)
    return b.reshape(3, nh, NA_RQ * GRID_W, NA_RK * GRID_W)


def _na_kernel(q_ref, k_ref, v_ref, bias_ref, qg_ref, kg_ref, o_ref, *, rows, hd):
    i = pl.program_id(2)
    scale = hd ** -0.5
    qg = qg_ref[...] * scale
    kg = kg_ref[...]
    q2 = q_ref[0]
    nkeys = NA_RK * GRID_W

    def head_out(hh, with_window, kwin, vwin, kctx, vctx):
        sl = slice(hh * hd, (hh + 1) * hd)
        qn = _rms(q2[:, sl]) * qg
        kc = _rms(kctx[:, sl]) * kg
        s_c = _dot_nt(qn, kc)
        m = jnp.max(s_c, axis=-1, keepdims=True)
        if with_window:
            kw = _rms(kwin[:, sl]) * kg
            s_w = _dot_nt(qn, kw) + bias_ref[0, hh]
            m = jnp.maximum(m, jnp.max(s_w, axis=-1, keepdims=True))
            p_w = jnp.exp(s_w - m)
        p_c = jnp.exp(s_c - m)
        den = jnp.sum(p_c, axis=-1, keepdims=True)
        num = _dot(p_c, vctx[:, sl])
        if with_window:
            den = den + jnp.sum(p_w, axis=-1, keepdims=True)
            num = num + _dot(p_w, vwin[:, sl])
        return num / den

    kctx = k_ref[0, 0:TM, :]
    vctx = v_ref[0, 0:TM, :]

    @pl.when(i == 0)
    def _():
        o_ref[0] = jnp.concatenate([head_out(hh, False, None, None, kctx, vctx) for hh in range(2)], axis=-1)

    @pl.when(i > 0)
    def _():
        r0 = (i - 1) * NA_RQ
        w0 = jnp.clip(r0 - NA_RQ, 0, rows - NA_RK)
        start = pl.multiple_of(TM + w0 * GRID_W, GRID_W)
        kwin = k_ref[0, pl.ds(start, nkeys), :]
        vwin = v_ref[0, pl.ds(start, nkeys), :]
        o_ref[0] = jnp.concatenate([head_out(hh, True, kwin, vwin, kctx, vctx) for hh in range(2)], axis=-1)


def _na(proj, q_g, k_g, rpb, col0):
    bsz, L, _ = proj.shape
    nh = rpb.shape[0]
    hd = q_g.shape[0]
    width = nh * hd
    rows = (L - TM) // GRID_W
    nb = L // TM
    bias = _na_bias_table(rpb, rows)
    qb, kb, vb = col0 // 128, (col0 + width) // 128, (col0 + 2 * width) // 128

    def pat(i):
        return jnp.where(i <= 1, 0, jnp.where(i == nb - 1, 2, 1))
    return pl.pallas_call(
        functools.partial(_na_kernel, rows=rows, hd=hd),
        grid=(bsz, width // 128, nb),
        in_specs=[pl.BlockSpec((1, TM, 128), lambda b, h, i: (b, i, qb + h)),
                  pl.BlockSpec((1, L, 128), lambda b, h, i: (b, 0, kb + h)),
                  pl.BlockSpec((1, L, 128), lambda b, h, i: (b, 0, vb + h)),
                  pl.BlockSpec((1, 2, TM, NA_RK * GRID_W), lambda b, h, i: (pat(i), h, 0, 0)),
                  pl.BlockSpec((1, hd), lambda b, h, i: (0, 0)),
                  pl.BlockSpec((1, hd), lambda b, h, i: (0, 0))],
        out_specs=pl.BlockSpec((1, TM, 128), lambda b, h, i: (b, i, h)),
        out_shape=jax.ShapeDtypeStruct((bsz, L, width), F32),
        compiler_params=_cp(("parallel", "parallel", "arbitrary")), name="nbr_attn",
    )(proj, proj, proj, bias, q_g.reshape(1, hd), k_g.reshape(1, hd))


def _route(lg):
    lane = lax.broadcasted_iota(jnp.int32, lg.shape, 1)
    lane_f = lane.astype(F32)
    is_g = lane < EXPERT_LANE0
    gl = jnp.where(is_g, lg, NEG)
    gmax = jnp.max(gl, axis=-1, keepdims=True)
    gsel = jnp.min(jnp.where(is_g & (gl == gmax), lane_f, 1e9), axis=-1, keepdims=True)
    g_w = 1.0 / jnp.sum(jnp.where(is_g, jnp.exp(gl - gmax), 0.0), axis=-1, keepdims=True)
    grp = ((lane - EXPERT_LANE0) >> 2).astype(F32)
    in_g = (lane >= EXPERT_LANE0) & (lane < EXPERT_LANE0 + N_EXPERTS) & (grp == gsel)
    el = jnp.where(in_g, lg, NEG)
    v1 = jnp.max(el, axis=-1, keepdims=True)
    i1 = jnp.min(jnp.where(in_g & (el == v1), lane_f, 1e9), axis=-1, keepdims=True)
    el2 = jnp.where(lane_f == i1, NEG, el)
    v2 = jnp.max(el2, axis=-1, keepdims=True)
    i2 = jnp.min(jnp.where(in_g & (lane_f != i1) & (el2 == v2), lane_f, 1e9), axis=-1, keepdims=True)
    t = jnp.exp(v2 - v1)
    w1 = g_w / (1.0 + t)
    w2 = g_w * t / (1.0 + t)
    return (jnp.where(lane_f == i1, w1, 0.0) + jnp.where(lane_f == i2, w2, 0.0)
            + jnp.where(lane_f == gsel, 1.0, 0.0))


def _outproj_kernel(*refs, even):
    if even:
        (r_ref, ug_ref, a_ref, x_ref, w_ref, mod_ref, g2_ref, rw_ref, x1_ref, h2_ref, gate_ref) = refs
        y_in = jnp.concatenate([r_ref[0] * _gelu_tanh(ug_ref[0]), a_ref[0]], axis=-1)
    else:
        (ys_ref, z_ref, hm_ref, mo_ref, sg_ref, mg_ref, x_ref, w_ref, mod_ref, g2_ref, rw_ref,
         x1_ref, h2_ref, gate_ref) = refs
        ys = ys_ref[0] * _silu(z_ref[0])
        sg = sg_ref[...]
        mg = mg_ref[...]
        hm = hm_ref[0]
        sig_o = _sigmoid(mo_ref[0])
        gw = ys.shape[-1] // 2
        parts = [_rms(ys[:, g * gw:(g + 1) * gw]) * sg[:, g * gw:(g + 1) * gw] for g in range(2)]
        hw = 128
        parts += [_rms(hm[:, h * hw:(h + 1) * hw]) * mg[:, h * hw:(h + 1) * hw] * sig_o[:, h * hw:(h + 1) * hw]
                  for h in range(hm.shape[-1] // hw)]
        y_in = jnp.concatenate(parts, axis=-1)
    mod = mod_ref[0, 0]
    x1 = x_ref[0] + mod[2:3, :] * _dot(y_in, w_ref[...])
    x1_ref[0] = x1
    h2 = _rms(x1) * g2_ref[...] * (1.0 + mod[4:5, :]) + mod[3:4, :]
    h2_ref[0] = h2.astype(BF16)
    gate_ref[0] = _route(_dot_hi(h2, rw_ref[...]))


def _outproj(mix_inputs, x, w, mod, g2, rw, even, tile0, ntiles):
    bsz, _, d = x.shape
    specs, args = [], []
    for arr, cb, wdt in mix_inputs:
        if arr.ndim == 3:
            specs.append(pl.BlockSpec((1, TM, wdt), lambda b, i, cb=cb: (b, i + tile0, cb)))
        else:
            specs.append(pl.BlockSpec((1, wdt), lambda b, i: (0, 0)))
        args.append(arr)
    specs += [pl.BlockSpec((1, TM, d), lambda b, i: (b, i + tile0, 0)),
              pl.BlockSpec(w.shape, lambda b, i: (0, 0)),
              pl.BlockSpec((1, 1, 6, d), lambda b, i: (b, jnp.minimum(i + tile0, 1), 0, 0)),
              pl.BlockSpec((1, d), lambda b, i: (0, 0)),
              pl.BlockSpec(rw.shape, lambda b, i: (0, 0))]
    args += [x, w, mod, g2.reshape(1, d), rw]
    lo = ntiles * TM
    return pl.pallas_call(
        functools.partial(_outproj_kernel, even=even),
        grid=(bsz, ntiles), in_specs=specs,
        out_specs=[pl.BlockSpec((1, TM, d), lambda b, i: (b, i, 0)),
                   pl.BlockSpec((1, TM, d), lambda b, i: (b, i, 0)),
                   pl.BlockSpec((1, TM, 128), lambda b, i: (b, i, 0))],
        out_shape=[jax.ShapeDtypeStruct((bsz, lo, d), F32), jax.ShapeDtypeStruct((bsz, lo, d), BF16),
                   jax.ShapeDtypeStruct((bsz, lo, 128), F32)],
        compiler_params=_cp(("parallel", "parallel")), name="outproj_even" if even else "outproj_odd",
    )(*args)


MOE_CH = 128
GROUP_SIZE = 4
N_GROUPS = N_EXPERTS // GROUP_SIZE


def _moe_kernel(x_ref, g_ref, tri_ref, w1_ref, w3_ref, w2_ref, o_ref, xs_s, ys_s, gs_s, pt_s, plan_s):
    e = pl.program_id(1)
    tb = x_ref.shape[0]
    nch = xs_s.shape[0] // MOE_CH

    @pl.when(e == 0)
    def _plan():
        g = g_ref[...]
        lane = lax.broadcasted_iota(jnp.int32, g.shape, 1)
        oh = jnp.where(lane < N_GROUPS, g, 0.0)
        rank = jnp.dot(tri_ref[...], oh.astype(BF16), preferred_element_type=F32)
        cnt = jnp.sum(oh, axis=0, keepdims=True)
        lane1 = lax.broadcasted_iota(jnp.int32, (1, 128), 1)
        off = jnp.int32(0)
        offv = jnp.zeros((1, 128), F32)
        for gi in range(N_GROUPS):
            n = jnp.sum(jnp.where(lane1 == gi, cnt, 0.0)).astype(jnp.int32)
            nchunks = (n + (MOE_CH - 1)) // MOE_CH
            plan_s[gi] = off // MOE_CH
            plan_s[N_GROUPS + gi] = nchunks
            offv = offv + jnp.where(lane1 == gi, off.astype(F32), 0.0)
            off = off + nchunks * MOE_CH
        pos_col = jnp.sum(oh * (rank + offv), axis=1, keepdims=True)
        posb = jnp.broadcast_to(pos_col, (tb, 128))
        pos_row = jnp.concatenate([posb[i * 128:(i + 1) * 128, :].T[0:1, :] for i in range(tb // 128)], axis=1)
        x = x_ref[...]
        lane_c = lax.broadcasted_iota(jnp.int32, (tb, MOE_CH), 1).astype(F32)
        row_c = lax.broadcasted_iota(jnp.int32, (MOE_CH, tb), 0).astype(F32)
        for c in range(nch):
            sl = slice(c * MOE_CH, (c + 1) * MOE_CH)
            pt_s[:, sl] = jnp.where(pos_col == lane_c + float(c * MOE_CH), 1.0, 0.0).astype(BF16)
            p = jnp.where(row_c + float(c * MOE_CH) == pos_row, 1.0, 0.0)
            xs_s[sl, :] = jnp.dot(p.astype(BF16), x, preferred_element_type=F32).astype(BF16)
            gs_s[sl, :] = _dot_hi(p, g)
        ys_s[...] = jnp.zeros_like(ys_s)

    grp = e // GROUP_SIZE
    c0 = plan_s[grp]
    lane_g = lax.broadcasted_iota(jnp.int32, (MOE_CH, 128), 1)

    def chunk(c, _):
        rows = pl.ds(pl.multiple_of(c * MOE_CH, MOE_CH), MOE_CH)
        xs = xs_s[rows, :]
        a = jnp.dot(xs, w1_ref[0], preferred_element_type=F32)
        b = jnp.dot(xs, w3_ref[0], preferred_element_type=F32)
        y = _dot(_silu(a) * b, w2_ref[0])
        ge = jnp.sum(jnp.where(lane_g == e + EXPERT_LANE0, gs_s[rows, :], 0.0), axis=-1, keepdims=True)
        ys_s[rows, :] += ge * y
        return 0
    lax.fori_loop(c0, c0 + plan_s[N_GROUPS + grp], chunk, 0)

    @pl.when(e == N_EXPERTS - 1)
    def _combine():
        o_ref[...] = jnp.dot(pt_s[...], ys_s[...].astype(BF16), preferred_element_type=F32)


def _moe(h2, gate, w1, w3, w2):
    t, d = h2.shape
    ne, _, ff = w1.shape
    tb = math.gcd(t, 1024)
    npad = tb + N_GROUPS * MOE_CH
    tri = jnp.asarray(np.tril(np.ones((tb, tb), np.float32), -1), BF16)
    return pl.pallas_call(
        _moe_kernel,
        grid=(t // tb, ne),
        in_specs=[pl.BlockSpec((tb, d), lambda i, e: (i, 0)),
                  pl.BlockSpec((tb, 128), lambda i, e: (i, 0)),
                  pl.BlockSpec((tb, tb), lambda i, e: (0, 0)),
                  pl.BlockSpec((1, d, ff), lambda i, e: (e, 0, 0)),
                  pl.BlockSpec((1, d, ff), lambda i, e: (e, 0, 0)),
                  pl.BlockSpec((1, ff, d), lambda i, e: (e, 0, 0))],
        out_specs=pl.BlockSpec((tb, d), lambda i, e: (i, 0)),
        out_shape=jax.ShapeDtypeStruct((t, d), F32),
        scratch_shapes=[pltpu.VMEM((npad, d), BF16), pltpu.VMEM((npad, d), F32), pltpu.VMEM((npad, 128), F32),
                        pltpu.VMEM((tb, npad), BF16), pltpu.SMEM((2 * N_GROUPS,), jnp.int32)],
        compiler_params=_cp(("parallel", "arbitrary")), name="moe",
    )(h2, gate, tri, w1, w3, w2)


def _tri(q, rev):
    r = lax.broadcasted_iota(jnp.int32, (q, q), 0)
    c = lax.broadcasted_iota(jnp.int32, (q, q), 1)
    return (c >= r) if rev else (c <= r)


def _ssd_kernel(xs_ref, b_ref, c_ref, dt_ref, cwx_ref, cbx_ref, cwb_ref, cbb_ref, cwc_ref, cbc_ref,
                dtb_ref, alog_ref, dsk_ref, sel_ref, o_ref, xc_s, bc_s, cc_s, dt_s, cumt_s, h_s, *, nc):
    g = pl.program_id(1)
    q = CHUNK
    first_lat = TM // q

    dtb = dtb_ref[...]

    def prep(i, _):
        t0 = pl.multiple_of(i * q, q)
        idx = pl.ds(t0, q)
        for src, cw, cb, dst in ((xs_ref, cwx_ref, cbx_ref, xc_s), (b_ref, cwb_ref, cbb_ref, bc_s),
                                 (c_ref, cwc_ref, cbc_ref, cc_s)):
            L = nc * q
            cur = src[0, idx, :]
            prev = src[0, pl.ds(pl.multiple_of(jnp.maximum(t0 - 8, 0), 8), 8), :]
            nxt = src[0, pl.ds(pl.multiple_of(jnp.minimum(t0 + q, L - 8), 8), 8), :]
            dst[idx, :] = _silu(_conv_vals(cur, prev, nxt, i, nc, cw[...], cb[...], q, first_lat))
        dt_s[idx, :] = _softplus(dt_ref[0, idx, :] + dtb)
        o_ref[0, idx, :] = dsk_ref[...] * xc_s[idx, :]
        return 0
    lax.fori_loop(0, nc, prep, 0)

    a_lane = -jnp.exp(alog_ref[...])
    lane256 = lax.broadcasted_iota(jnp.int32, (q, 256), 1) // 64

    for d in (0, 1):
        rev = d == 1
        causal = _tri(q, rev)
        tri = causal.astype(F32)
        sel = sel_ref[d, 0]
        last = 0 if rev else q - 1
        h_s[...] = jnp.zeros_like(h_s)

        def chunk(j, _):
            if rev:
                ci = jnp.where(j < first_lat, first_lat - 1 - j, nc - 1 - (j - first_lat))
            else:
                ci = j
            idx = pl.ds(pl.multiple_of(ci * q, q), q)
            dt_all = dt_s[idx, :]
            cum = _dot_hi(tri, dt_all * a_lane)
            cumt_s[...] = cum.T
            dt_e = _dot_hi(dt_all, sel)
            cum_e = _dot_hi(cum, sel)
            tot_e = cum_e[last:last + 1, :]
            xc = xc_s[idx, :]
            bc = bc_s[idx, :]
            cc = cc_s[idx, :]
            dtx = xc * dt_e
            cb = _dot_nt(cc, bc)
            y = jnp.exp(cum_e) * _dot(cc, h_s[...])
            for r in range(4):
                lr = d * 8 + g * 4 + r
                seg = cum_e[:, r * 64:r * 64 + 1] - cumt_s[pl.ds(lr, 1), :]
                m = cb * jnp.exp(jnp.where(causal, seg, NEG))
                y = y + _dot(m, jnp.where(lane256 == r, dtx, 0.0))
            o_ref[0, idx, :] += y
            states = _dot(bc.T, dtx * jnp.exp(tot_e - cum_e))
            h_s[...] = jnp.exp(tot_e) * h_s[...] + states
            return 0
        lax.fori_loop(0, nc, chunk, 0)


def _ssd(proj, conv_w, conv_b, dt_bias, a_log, d_skip, cols):
    bsz, L, _ = proj.shape
    nc = L // CHUNK
    cx, cbm, ccm, cdt = cols
    nh = d_skip.shape[0]
    dtb = jnp.zeros((1, 128), F32).at[0, :2 * nh].set(dt_bias.reshape(-1))
    alog = jnp.zeros((1, 128), F32).at[0, :2 * nh].set(a_log.reshape(-1))
    dsk = jnp.repeat(d_skip.astype(F32), 64).reshape(1, nh * 64)
    sel = np.zeros((2, 2, 128, 256), np.float32)
    for d in range(2):
        for g in range(2):
            for r in range(4):
                sel[d, g, d * nh + g * 4 + r, r * 64:(r + 1) * 64] = 1.0
    cw_specs = []
    for width, off in ((256, 0), (128, 512), (128, 768)):
        cw_specs += [pl.BlockSpec((CONV_W, width), lambda b, g, off=off, width=width: (0, off // width + g)),
                     pl.BlockSpec((1, width), lambda b, g, off=off, width=width: (0, off // width + g))]
    return pl.pallas_call(
        functools.partial(_ssd_kernel, nc=nc),
        grid=(bsz, 2),
        in_specs=[pl.BlockSpec((1, L, 256), lambda b, g: (b, 0, cx // 256 + g)),
                  pl.BlockSpec((1, L, 128), lambda b, g: (b, 0, cbm // 128 + g)),
                  pl.BlockSpec((1, L, 128), lambda b, g: (b, 0, ccm // 128 + g)),
                  pl.BlockSpec((1, L, 128), lambda b, g: (b, 0, cdt // 128))] + cw_specs + [
                  pl.BlockSpec((1, 128), lambda b, g: (0, 0)),
                  pl.BlockSpec((1, 128), lambda b, g: (0, 0)),
                  pl.BlockSpec((1, 256), lambda b, g: (0, g)),
                  pl.BlockSpec((2, 1, 128, 256), lambda b, g: (0, g, 0, 0))],
        out_specs=pl.BlockSpec((1, L, 256), lambda b, g: (b, 0, g)),
        out_shape=jax.ShapeDtypeStruct((bsz, L, nh * 64), F32),
        scratch_shapes=[pltpu.VMEM((L, 256), F32), pltpu.VMEM((L, 128), F32), pltpu.VMEM((L, 128), F32),
                        pltpu.VMEM((L, 128), F32), pltpu.VMEM((128, CHUNK), F32), pltpu.VMEM((128, 256), F32)],
        compiler_params=_cp(("parallel", "arbitrary")), name="ssd",
    )(proj, proj, proj, proj, conv_w, conv_b.reshape(1, -1), conv_w, conv_b.reshape(1, -1),
      conv_w, conv_b.reshape(1, -1), dtb, alog, dsk, jnp.asarray(sel))


def _mlstm_kernel(q_ref, k_ref, v_ref, g_ref, cwq_ref, cbq_ref, cwk_ref, cbk_ref, gb_ref, sel_ref, o_ref,
                  qc_s, kc_s, c_s, n_s, m_s, *, nc, dh):
    q = CHUNK
    first_lat = TM // q

    def prep(i, _):
        t0 = pl.multiple_of(i * q, q)
        idx = pl.ds(t0, q)
        L = nc * q
        for src, cw, cb, dst, mul in ((q_ref, cwq_ref, cbq_ref, qc_s, 1.0), (k_ref, cwk_ref, cbk_ref, kc_s, dh ** -0.5)):
            cur = src[0, idx, :]
            prev = src[0, pl.ds(pl.multiple_of(jnp.maximum(t0 - 8, 0), 8), 8), :]
            nxt = src[0, pl.ds(pl.multiple_of(jnp.minimum(t0 + q, L - 8), 8), 8), :]
            dst[idx, :] = _silu(_conv_vals(cur, prev, nxt, i, nc, cw[...], cb[...], q, first_lat)) * mul
        return 0
    lax.fori_loop(0, nc, prep, 0)

    gb = gb_ref[...]
    lane = lax.broadcasted_iota(jnp.int32, (q, 128), 1)
    for d in (0, 1):
        rev = d == 1
        causal = _tri(q, rev)
        tri = causal.astype(F32)
        sel = sel_ref[d, 0]
        last = 0 if rev else q - 1
        c_s[...] = jnp.zeros_like(c_s)
        n_s[...] = jnp.zeros_like(n_s)
        m_s[...] = jnp.full_like(m_s, NEG)

        def chunk(j, _):
            if rev:
                ci = jnp.where(j < first_lat, first_lat - 1 - j, nc - 1 - (j - first_lat))
            else:
                ci = j
            idx = pl.ds(pl.multiple_of(ci * q, q), q)
            gsel = _dot_hi(g_ref[0, idx, :] + gb, sel)
            logf = jnp.minimum(gsel, 0.0) - jnp.log(1.0 + jnp.exp(-jnp.abs(gsel)))
            gi = jnp.where(lane == 0, gsel, jnp.where(lane == 1, logf, 0.0))
            cum = _dot_hi(tri, gi)
            git = gi.T
            cumt = cum.T
            i_col, i_row = gi[:, 0:1], git[0:1, :]
            b_col, b_row = cum[:, 1:2], cumt[1:2, :]
            m_prev = m_s[...]
            dlog = jnp.where(causal, b_col - b_row + i_row, NEG)
            inter = b_col + m_prev
            m_t = jnp.maximum(jnp.max(dlog, axis=-1, keepdims=True), inter)
            qc = qc_s[idx, :]
            kc = kc_s[idx, :]
            vc = v_ref[0, idx, :]
            kct = kc.T
            w = jnp.exp(dlog - m_t) * _dot(qc, kct)
            g_in = jnp.exp(inter - m_t)
            num = _dot(w, vc) + g_in * _dot(qc, c_s[...])
            den = jnp.sum(w, axis=-1, keepdims=True) + g_in * jnp.sum(qc * n_s[...], axis=-1, keepdims=True)
            o_ref[0, idx, :] += num / jnp.maximum(jnp.abs(den), jnp.exp(-m_t))
            b_end = b_col[last:last + 1, :]
            g_s = b_end - b_col + i_col
            m_new = jnp.maximum(jnp.max(g_s, axis=0, keepdims=True), b_end + m_prev)
            w_s = jnp.exp(g_s - m_new)
            keep = jnp.exp(b_end + m_prev - m_new)
            c_s[...] = keep * c_s[...] + _dot(kct, w_s * vc)
            n_s[...] = keep * n_s[...] + jnp.sum(w_s * kc, axis=0, keepdims=True)
            m_s[...] = m_new
            return 0

        if d == 0:
            o_ref[...] = jnp.zeros_like(o_ref)
        lax.fori_loop(0, nc, chunk, 0)


def _mlstm(proj, conv_w, conv_b, gate_b, cols):
    bsz, L, _ = proj.shape
    nc = L // CHUNK
    cq, ck, cv, cg = cols
    nh = gate_b.shape[-1]
    dh = conv_w.shape[1] // (2 * nh)
    gbl = jnp.zeros((1, 128), F32).at[0, 16:16 + 4 * nh].set(gate_b.reshape(-1))
    sel = np.zeros((2, nh, 128, 128), np.float32)
    for d in range(2):
        for h in range(nh):
            sel[d, h, 16 + d * 2 * nh + h, 0] = 1.0
            sel[d, h, 16 + d * 2 * nh + nh + h, 1] = 1.0
    return pl.pallas_call(
        functools.partial(_mlstm_kernel, nc=nc, dh=dh),
        grid=(bsz, nh),
        in_specs=[pl.BlockSpec((1, L, dh), lambda b, h: (b, 0, cq // dh + h)),
                  pl.BlockSpec((1, L, dh), lambda b, h: (b, 0, ck // dh + h)),
                  pl.BlockSpec((1, L, dh), lambda b, h: (b, 0, cv // dh + h)),
                  pl.BlockSpec((1, L, 128), lambda b, h: (b, 0, cg // 128)),
                  pl.BlockSpec((CONV_W, dh), lambda b, h: (0, h)),
                  pl.BlockSpec((1, dh), lambda b, h: (0, h)),
                  pl.BlockSpec((CONV_W, dh), lambda b, h: (0, nh + h)),
                  pl.BlockSpec((1, dh), lambda b, h: (0, nh + h)),
                  pl.BlockSpec((1, 128), lambda b, h: (0, 0)),
                  pl.BlockSpec((2, 1, 128, 128), lambda b, h: (0, h, 0, 0))],
        out_specs=pl.BlockSpec((1, L, dh), lambda b, h: (b, 0, h)),
        out_shape=jax.ShapeDtypeStruct((bsz, L, nh * dh), F32),
        scratch_shapes=[pltpu.VMEM((L, dh), F32), pltpu.VMEM((L, dh), F32), pltpu.VMEM((dh, dh), F32),
                        pltpu.VMEM((1, dh), F32), pltpu.VMEM((1, 1), F32)],
        compiler_params=_cp(("parallel", "arbitrary")), name="mlstm",
    )(proj, proj, proj, proj, conv_w, conv_b.reshape(1, -1), conv_w, conv_b.reshape(1, -1), gbl, jnp.asarray(sel))


def _final_kernel(x_ref, o_ref, mod_ref, y_ref):
    y_ref[0] = x_ref[0] + mod_ref[0, 0][5:6, :] * o_ref[0]


def _final(x1, moe_out, mod):
    bsz, S, d = x1.shape
    tok = pl.BlockSpec((1, TM, d), lambda b, i: (b, i, 0))
    return pl.pallas_call(
        _final_kernel, grid=(bsz, S // TM),
        in_specs=[tok, tok, pl.BlockSpec((1, 1, 6, d), lambda b, i: (b, 1, 0, 0))],
        out_specs=tok, out_shape=jax.ShapeDtypeStruct(x1.shape, F32),
        compiler_params=_cp(("parallel", "parallel")), name="final_residual",
    )(x1, moe_out, mod)


def kernel(x, c, ctx, c_ctx, mod_w, mod_b, norm1_g, norm2_g, even_w_in, even_w_out, lru_conv_w, lru_conv_b, lru_wa, lru_ba, lru_wx, lru_bx, lru_lam, na_q_g, na_k_g, na_rpb, odd_w_in, odd_w_out, ssd_conv_w, ssd_conv_b, ssd_dt_bias, ssd_a_log, ssd_d, ssd_norm_g, ml_conv_w, ml_conv_b, ml_gate_b, ml_norm_g, moe_router_g, moe_router_e, moe_w1, moe_w3, moe_w2):
    bsz, S, d = x.shape
    lc = ctx.shape[1]
    assert lc == TM and S % TM == 0 and mod_w.shape[0] == 2
    nt = (lc + S) // TM
    L = lc + S

    cc = jnp.zeros((8, d), F32).at[0].set(c_ctx).at[1:1 + bsz].set(c)
    mod_all = _modulation(cc, mod_w, mod_b)

    def mod_for(l):
        m = mod_all[l].reshape(8, 6, d)
        return jnp.stack([jnp.broadcast_to(m[0], (bsz, 6, d)), m[1:1 + bsz]], axis=1)

    def router_w(l):
        return jnp.zeros((d, 128), F32).at[:, :EXPERT_LANE0].set(moe_router_g[l]) \
            .at[:, EXPERT_LANE0:EXPERT_LANE0 + N_EXPERTS].set(moe_router_e[l])

    xx = jnp.concatenate([ctx, x], axis=1)

    mod0 = mod_for(0)
    proj = _inproj(xx, mod0, norm1_g[0], even_w_in[0].astype(BF16))
    lw = lru_conv_w.shape[-1]
    r = _lru(proj, lru_conv_w[0], lru_conv_b[0], lru_wa[0], lru_ba[0], lru_wx[0], lru_bx[0], lru_lam[0])
    a = _na(proj, na_q_g[0], na_k_g[0], na_rpb[0], col0=2 * lw)
    x1, h2, gate = _outproj([(r, 0, lw), (proj, 1, lw), (a, 0, lw)], xx, even_w_out[0].astype(BF16), mod0,
                            norm2_g[0], router_w(0), even=True, tile0=0, ntiles=nt)
    moe0 = _moe(h2.reshape(bsz * L, d), gate.reshape(bsz * L, 128), moe_w1[0].astype(BF16),
                      moe_w3[0].astype(BF16), moe_w2[0].astype(BF16)).reshape(bsz, L, d)

    mod1 = mod_for(1)
    sw = ssd_d.shape[-1] * 64
    xbc = ssd_conv_w.shape[-1]
    mw = ml_norm_g.shape[-1]
    w = odd_w_in[0]
    o = np.cumsum([0, sw, xbc, 2 * ssd_d.shape[-1], mw, mw, mw, mw])
    small = jnp.concatenate([w[:, o[2]:o[3]], w[:, o[7]:], jnp.zeros((d, 128 - 2 * ssd_d.shape[-1] - (w.shape[1] - o[7])), F32)], axis=1)
    w_odd = jnp.concatenate([w[:, :o[2]], w[:, o[3]:o[7]], small], axis=1).astype(BF16)
    cz, cxs = 0, sw
    cB, cC = cxs + sw, cxs + sw + (xbc - sw) // 2
    cq = sw + xbc
    ck, cv, co, csm = cq + mw, cq + 2 * mw, cq + 3 * mw, cq + 4 * mw
    x0, proj1 = _inproj(x1, mod1, norm1_g[1], w_odd, prev=(moe0, mod0))
    ys = _ssd(proj1, ssd_conv_w[0], ssd_conv_b[0], ssd_dt_bias[0], ssd_a_log[0], ssd_d[0], (cxs, cB, cC, csm))
    hm = _mlstm(proj1, ml_conv_w[0], ml_conv_b[0], ml_gate_b[0], (cq, ck, cv, csm))
    x2, h2b, gate1 = _outproj([(ys, 0, sw), (proj1, cz // sw, sw), (hm, 0, mw), (proj1, co // mw, mw),
                               (ssd_norm_g[0].reshape(1, sw), 0, sw), (ml_norm_g[0].reshape(1, mw), 0, mw)],
                              x0, odd_w_out[0].astype(BF16), mod1, norm2_g[1], router_w(1),
                              even=False, tile0=1, ntiles=nt - 1)
    moe1 = _moe(h2b.reshape(bsz * S, d), gate1.reshape(bsz * S, 128), moe_w1[1].astype(BF16),
                      moe_w3[1].astype(BF16), moe_w2[1].astype(BF16)).reshape(bsz, S, d)
    return _final(x2, moe1, mod1)
```

```python
import functools
import math

import jax
import jax.numpy as jnp
import numpy as np
from jax import lax
from jax.experimental import pallas as pl
from jax.experimental.pallas import tpu as pltpu

F32 = jnp.float32
BF16 = jnp.bfloat16
HI = lax.Precision.HIGHEST

EPS = 1e-6
NEG = -1e30
GRID_W = 64
CONV_W = 4
LRU_C = 8.0
TM = 256
CHUNK = 128
NA_RQ = 4
NA_RK = 12
N_EXPERTS = 16
EXPERT_LANE0 = 4
VMEM_LIMIT = 56 * 1024 * 1024


def _cp(sem, vmem=VMEM_LIMIT):
    return pltpu.CompilerParams(dimension_semantics=sem, vmem_limit_bytes=vmem)


def _sigmoid(x):
    return jax.nn.sigmoid(x)


def _silu(x):
    return x * jax.nn.sigmoid(x)


def _softplus(x):
    return jnp.maximum(x, 0.0) + jnp.log(1.0 + jnp.exp(-jnp.abs(x)))


def _gelu_tanh(x):
    return 0.5 * x * (1.0 + jnp.tanh(math.sqrt(2.0 / math.pi) * (x + 0.044715 * (x * x * x))))


def _rms(x, axis=-1):
    return x * lax.rsqrt(jnp.mean(x * x, axis=axis, keepdims=True) + EPS)


def _dot(a, b):
    return jnp.dot(a.astype(BF16), b.astype(BF16), preferred_element_type=F32)


def _dot_hi(a, b):
    return jnp.dot(a, b, precision=HI, preferred_element_type=F32)


def _interleave(gens):
    results = [None] * len(gens)
    live = list(range(len(gens)))
    while live:
        for i in list(live):
            try:
                next(gens[i])
            except StopIteration as stop:
                results[i] = stop.value
                live.remove(i)
    return results


def _dot_nt(a, b):
    return lax.dot_general(a.astype(BF16), b.astype(BF16), (((1,), (1,)), ((), ())),
                           preferred_element_type=F32)


def _mod_kernel(c_ref, w_ref, b_ref, o_ref):
    c = c_ref[...]
    o_ref[0] = _dot_hi(_silu(c), w_ref[0]) + b_ref[0]


def _modulation(cc, mod_w, mod_b):
    depth, d, n = mod_w.shape
    tn = 1536
    return pl.pallas_call(
        _mod_kernel,
        grid=(depth, n // tn),
        in_specs=[pl.BlockSpec((8, d), lambda l, j: (0, 0)),
                  pl.BlockSpec((1, d, tn), lambda l, j: (l, 0, j)),
                  pl.BlockSpec((1, 1, tn), lambda l, j: (l, 0, j))],
        out_specs=pl.BlockSpec((1, 8, tn), lambda l, j: (l, 0, j)),
        out_shape=jax.ShapeDtypeStruct((depth, 8, n), F32),
        compiler_params=_cp(("arbitrary", "arbitrary")),
        name="adaln_mod",
    )(cc, mod_w, mod_b.reshape(depth, 1, n))


def _inproj_kernel(*refs, fuse_prev):
    if fuse_prev:
        x_ref, mo_ref, pmod_ref, mod_ref, g_ref, w_ref, xo_ref, p_ref = refs
        x = x_ref[0] + pmod_ref[0, 0][5:6, :] * mo_ref[0]
        xo_ref[0] = x
    else:
        x_ref, mod_ref, g_ref, w_ref, p_ref = refs
        x = x_ref[0]
    mod = mod_ref[0, 0]
    h = _rms(x) * g_ref[...] * (1.0 + mod[1:2, :]) + mod[0:1, :]
    p_ref[0] = _dot(h, w_ref[...])


def _seg_map(b, i):
    return (b, jnp.minimum(i, 1), 0, 0)


def _inproj(x, mod, g, w, prev=None):
    bsz, L, d = x.shape
    n = w.shape[1]
    nt = L // TM
    tok = pl.BlockSpec((1, TM, d), lambda b, i: (b, i, 0))
    modspec = pl.BlockSpec((1, 1, 6, d), _seg_map)
    tail = [modspec, pl.BlockSpec((1, d), lambda b, i: (0, 0)), pl.BlockSpec((d, n), lambda b, i: (0, 0))]
    pspec = pl.BlockSpec((1, TM, n), lambda b, i: (b, i, 0))
    pshape = jax.ShapeDtypeStruct((bsz, L, n), F32)
    if prev is None:
        return pl.pallas_call(
            functools.partial(_inproj_kernel, fuse_prev=False),
            grid=(bsz, nt), in_specs=[tok] + tail, out_specs=pspec, out_shape=pshape,
            compiler_params=_cp(("parallel", "parallel")), name="inproj",
        )(x, mod, g.reshape(1, d), w)
    moe_out, pmod = prev
    return pl.pallas_call(
        functools.partial(_inproj_kernel, fuse_prev=True),
        grid=(bsz, nt), in_specs=[tok, tok, modspec] + tail,
        out_specs=[tok, pspec], out_shape=[jax.ShapeDtypeStruct(x.shape, F32), pshape],
        compiler_params=_cp(("parallel", "parallel")), name="inproj_res",
    )(x, moe_out, pmod, mod, g.reshape(1, d), w)


def _conv_tile(ref, i, nt, cw, cb, width=TM):
    L = nt * width
    t0 = pl.multiple_of(i * width, width)
    cur = ref[0, pl.ds(t0, width), :]
    prev = ref[0, pl.ds(pl.multiple_of(jnp.maximum(t0 - 8, 0), 8), 8), :]
    nxt = ref[0, pl.ds(pl.multiple_of(jnp.minimum(t0 + width, L - 8), 8), 8), :]
    return _conv_vals(cur, prev, nxt, i, nt, cw, cb, width, first_lat=TM // width)


def _conv_vals(cur, prev, nxt, i, nt, cw, cb, width, first_lat):
    prev = jnp.where((i != 0) & (i != first_lat), prev, 0.0)
    nxt = jnp.where((i != first_lat - 1) & (i != nt - 1), nxt, 0.0)
    cat = jnp.concatenate([prev, cur, nxt], axis=0)
    return (cw[0:1] * cat[6:6 + width] + cw[1:2] * cat[7:7 + width] + cw[2:3] * cur
            + cw[3:4] * cat[9:9 + width] + cb)


def _lru_kernel(ux_ref, cw_ref, cb_ref, gw_ref, gb_ref, lam_ref, o_ref, *, nt):
    cw = cw_ref[...]
    cb = cb_ref[...]
    lam = lam_ref[0]
    sp = _softplus(-lam)
    row = lax.broadcasted_iota(jnp.int32, (TM, 128), 0) & 7

    def gates(i, d):
        xl = _conv_tile(ux_ref, i, nt, cw, cb)
        g = _dot(xl, gw_ref[0, d]) + gb_ref[0, d]
        r = _sigmoid(g[:, :128])
        ig = _sigmoid(g[:, 128:])
        log_a = -LRU_C * r * sp[d:d + 1]
        a = jnp.exp(log_a)
        u = jnp.sqrt(1.0 - a * a) * (ig * xl)
        return a, u

    def scan_tile(i, d, carry, accumulate):
        a, u = gates(i, d)
        rev = d == 1
        for k in (1, 2, 4):
            sh = TM - k if rev else k
            ok = (row < 8 - k) if rev else (row >= k)
            ash = pltpu.roll(a, sh, 0)
            ush = pltpu.roll(u, sh, 0)
            u = jnp.where(ok, u + a * ush, u)
            a = jnp.where(ok, a * ash, a)
        t0 = i * TM
        groups = range(TM // 8)
        for s in (reversed(groups) if rev else groups):
            h = u[s * 8:(s + 1) * 8] + a[s * 8:(s + 1) * 8] * carry
            carry = h[0:1] if rev else h[7:8]
            idx = pl.ds(pl.multiple_of(t0 + s * 8, 8), 8)
            if accumulate:
                o_ref[0, idx, :] += h
            else:
                o_ref[0, idx, :] = h
        return carry

    zero = jnp.zeros((1, 128), F32)
    lax.fori_loop(0, nt, lambda i, c: scan_tile(i, 0, c, False), zero)
    lax.fori_loop(0, nt, lambda j, c: scan_tile(jnp.where(j == 0, 0, nt - j), 1, c, True), zero)


def _lru(proj, conv_w, conv_b, wa, ba, wx, bx, lam):
    bsz, L, _ = proj.shape
    nt = L // TM
    width = conv_w.shape[1]
    ng = width // 128

    def blockdiag(w):
        w = w.reshape(2, ng, 2, 64, 64)
        z = jnp.zeros_like(w[:, :, 0])
        top = jnp.concatenate([w[:, :, 0], z], axis=-1)
        bot = jnp.concatenate([z, w[:, :, 1]], axis=-1)
        return jnp.concatenate([top, bot], axis=-2)
    gw = jnp.concatenate([blockdiag(wa), blockdiag(wx)], axis=-1).transpose(1, 0, 2, 3).astype(BF16)
    gb = jnp.concatenate([ba.reshape(2, ng, 1, 128), bx.reshape(2, ng, 1, 128)], axis=-1).transpose(1, 0, 2, 3)
    lam_g = lam.reshape(2, ng, 128).transpose(1, 0, 2)
    return pl.pallas_call(
        functools.partial(_lru_kernel, nt=nt),
        grid=(bsz, ng),
        in_specs=[pl.BlockSpec((1, L, 128), lambda b, c: (b, 0, c)),
                  pl.BlockSpec((CONV_W, 128), lambda b, c: (0, c)),
                  pl.BlockSpec((1, 128), lambda b, c: (0, c)),
                  pl.BlockSpec((1, 2, 128, 256), lambda b, c: (c, 0, 0, 0)),
                  pl.BlockSpec((1, 2, 1, 256), lambda b, c: (c, 0, 0, 0)),
                  pl.BlockSpec((1, 2, 128), lambda b, c: (c, 0, 0))],
        out_specs=pl.BlockSpec((1, L, 128), lambda b, c: (b, 0, c)),
        out_shape=jax.ShapeDtypeStruct((bsz, L, width), F32),
        compiler_params=_cp(("parallel", "parallel")), name="rglru",
    )(proj, conv_w, conv_b.reshape(1, width), gw, gb, lam_g)


def _na_bias_table(rpb, rows):
    nh = rpb.shape[0]
    win_r = (rpb.shape[1] + 1) // 2
    win_c = (rpb.shape[2] + 1) // 2
    qr = np.arange(NA_RQ)[:, None, None, None]
    qc = np.arange(GRID_W)[None, :, None, None]
    kr = np.arange(NA_RK)[None, None, :, None]
    kc = np.arange(GRID_W)[None, None, None, :]
    cstart = np.clip(qc - win_c // 2, 0, GRID_W - win_c)
    col_ok = (kc >= cstart) & (kc < cstart + win_c)
    dcol = np.clip(kc - qc + (win_c - 1), 0, 2 * win_c - 2)
    oc = (np.arange(2 * win_c - 1)[:, None, None] == dcol[0, :, 0, :][None]).astype(np.float32)
    o_rows, oks = [], []
    for r0, w0 in ((0, 0), (2 * NA_RQ, NA_RQ), (rows - NA_RQ, rows - NA_RK)):
        r = r0 + qr
        kabs = w0 + kr
        rstart = np.clip(r - win_r // 2, 0, rows - win_r)
        oks.append(np.broadcast_to((kabs >= rstart) & (kabs < rstart + win_r) & col_ok,
                                   (NA_RQ, GRID_W, NA_RK, GRID_W)))
        drow = np.clip(kabs - r + (win_r - 1), 0, 2 * win_r - 2)[:, 0, :, 0]
        o_rows.append((np.arange(2 * win_r - 1)[:, None, None] == drow[None]).astype(np.float32))
    t1 = jnp.einsum('hrc,prab->phabc', rpb.astype(F32), jnp.asarray(np.stack(o_rows)), precision=HI)
    b = jnp.einsum('phabc,cqk->phaqbk', t1, jnp.asarray(oc), precision=HI)
    b = jnp.where(jnp.asarray(np.stack(oks))[:, None], b, NEG)
    return b.reshape(3, nh, NA_RQ * GRID_W, NA_RK * GRID_W)


def _na_kernel(q_ref, k_ref, v_ref, bias_ref, qg_ref, kg_ref, o_ref, *, rows, hd):
    i = pl.program_id(2)
    scale = hd ** -0.5
    qg = qg_ref[...] * scale
    kg = kg_ref[...]
    q2 = q_ref[0]
    nkeys = NA_RK * GRID_W

    def head_out(hh, with_window, kwin, vwin, kctx, vctx):
        sl = slice(hh * hd, (hh + 1) * hd)
        qn = _rms(q2[:, sl]) * qg
        kc = _rms(kctx[:, sl]) * kg
        s_c = _dot_nt(qn, kc)
        m = jnp.max(s_c, axis=-1, keepdims=True)
        if with_window:
            kw = _rms(kwin[:, sl]) * kg
            s_w = _dot_nt(qn, kw) + bias_ref[0, hh]
            m = jnp.maximum(m, jnp.max(s_w, axis=-1, keepdims=True))
            p_w = jnp.exp(s_w - m)
        p_c = jnp.exp(s_c - m)
        den = jnp.sum(p_c, axis=-1, keepdims=True)
        num = _dot(p_c, vctx[:, sl])
        if with_window:
            den = den + jnp.sum(p_w, axis=-1, keepdims=True)
            num = num + _dot(p_w, vwin[:, sl])
        return num / den

    kctx = k_ref[0, 0:TM, :]
    vctx = v_ref[0, 0:TM, :]

    @pl.when(i == 0)
    def _():
        o_ref[0] = jnp.concatenate([head_out(hh, False, None, None, kctx, vctx) for hh in range(2)], axis=-1)

    @pl.when(i > 0)
    def _():
        r0 = (i - 1) * NA_RQ
        w0 = jnp.clip(r0 - NA_RQ, 0, rows - NA_RK)
        start = pl.multiple_of(TM + w0 * GRID_W, GRID_W)
        kwin = k_ref[0, pl.ds(start, nkeys), :]
        vwin = v_ref[0, pl.ds(start, nkeys), :]
        o_ref[0] = jnp.concatenate([head_out(hh, True, kwin, vwin, kctx, vctx) for hh in range(2)], axis=-1)


def _na(proj, q_g, k_g, rpb, col0):
    bsz, L, _ = proj.shape
    nh = rpb.shape[0]
    hd = q_g.shape[0]
    width = nh * hd
    rows = (L - TM) // GRID_W
    nb = L // TM
    bias = _na_bias_table(rpb, rows)
    qb, kb, vb = col0 // 128, (col0 + width) // 128, (col0 + 2 * width) // 128

    def pat(i):
        return jnp.where(i <= 1, 0, jnp.where(i == nb - 1, 2, 1))
    return pl.pallas_call(
        functools.partial(_na_kernel, rows=rows, hd=hd),
        grid=(bsz, width // 128, nb),
        in_specs=[pl.BlockSpec((1, TM, 128), lambda b, h, i: (b, i, qb + h)),
                  pl.BlockSpec((1, L, 128), lambda b, h, i: (b, 0, kb + h)),
                  pl.BlockSpec((1, L, 128), lambda b, h, i: (b, 0, vb + h)),
                  pl.BlockSpec((1, 2, TM, NA_RK * GRID_W), lambda b, h, i: (pat(i), h, 0, 0)),
                  pl.BlockSpec((1, hd), lambda b, h, i: (0, 0)),
                  pl.BlockSpec((1, hd), lambda b, h, i: (0, 0))],
        out_specs=pl.BlockSpec((1, TM, 128), lambda b, h, i: (b, i, h)),
        out_shape=jax.ShapeDtypeStruct((bsz, L, width), F32),
        compiler_params=_cp(("parallel", "parallel", "arbitrary")), name="nbr_attn",
    )(proj, proj, proj, bias, q_g.reshape(1, hd), k_g.reshape(1, hd))


def _route(lg):
    lane = lax.broadcasted_iota(jnp.int32, lg.shape, 1)
    lane_f = lane.astype(F32)
    is_g = lane < EXPERT_LANE0
    gl = jnp.where(is_g, lg, NEG)
    gmax = jnp.max(gl, axis=-1, keepdims=True)
    gsel = jnp.min(jnp.where(is_g & (gl == gmax), lane_f, 1e9), axis=-1, keepdims=True)
    g_w = 1.0 / jnp.sum(jnp.where(is_g, jnp.exp(gl - gmax), 0.0), axis=-1, keepdims=True)
    grp = ((lane - EXPERT_LANE0) >> 2).astype(F32)
    in_g = (lane >= EXPERT_LANE0) & (lane < EXPERT_LANE0 + N_EXPERTS) & (grp == gsel)
    el = jnp.where(in_g, lg, NEG)
    v1 = jnp.max(el, axis=-1, keepdims=True)
    i1 = jnp.min(jnp.where(in_g & (el == v1), lane_f, 1e9), axis=-1, keepdims=True)
    el2 = jnp.where(lane_f == i1, NEG, el)
    v2 = jnp.max(el2, axis=-1, keepdims=True)
    i2 = jnp.min(jnp.where(in_g & (lane_f != i1) & (el2 == v2), lane_f, 1e9), axis=-1, keepdims=True)
    t = jnp.exp(v2 - v1)
    w1 = g_w / (1.0 + t)
    w2 = g_w * t / (1.0 + t)
    return (jnp.where(lane_f == i1, w1, 0.0) + jnp.where(lane_f == i2, w2, 0.0)
            + jnp.where(lane_f == gsel, 1.0, 0.0))


def _outproj_kernel(*refs, even):
    if even:
        (r_ref, ug_ref, a_ref, x_ref, w_ref, mod_ref, g2_ref, rw_ref, x1_ref, h2_ref, gate_ref) = refs
        y_in = jnp.concatenate([r_ref[0] * _gelu_tanh(ug_ref[0]), a_ref[0]], axis=-1)
    else:
        (ys_ref, z_ref, hm_ref, mo_ref, sg_ref, mg_ref, x_ref, w_ref, mod_ref, g2_ref, rw_ref,
         x1_ref, h2_ref, gate_ref) = refs
        ys = ys_ref[0] * _silu(z_ref[0])
        sg = sg_ref[...]
        mg = mg_ref[...]
        hm = hm_ref[0]
        sig_o = _sigmoid(mo_ref[0])
        gw = ys.shape[-1] // 2
        parts = [_rms(ys[:, g * gw:(g + 1) * gw]) * sg[:, g * gw:(g + 1) * gw] for g in range(2)]
        hw = 128
        parts += [_rms(hm[:, h * hw:(h + 1) * hw]) * mg[:, h * hw:(h + 1) * hw] * sig_o[:, h * hw:(h + 1) * hw]
                  for h in range(hm.shape[-1] // hw)]
        y_in = jnp.concatenate(parts, axis=-1)
    mod = mod_ref[0, 0]
    x1 = x_ref[0] + mod[2:3, :] * _dot(y_in, w_ref[...])
    x1_ref[0] = x1
    h2 = _rms(x1) * g2_ref[...] * (1.0 + mod[4:5, :]) + mod[3:4, :]
    h2_ref[0] = h2.astype(BF16)
    gate_ref[0] = _route(_dot_hi(h2, rw_ref[...]))


def _outproj(mix_inputs, x, w, mod, g2, rw, even, tile0, ntiles):
    bsz, _, d = x.shape
    specs, args = [], []
    for arr, cb, wdt in mix_inputs:
        if arr.ndim == 3:
            specs.append(pl.BlockSpec((1, TM, wdt), lambda b, i, cb=cb: (b, i + tile0, cb)))
        else:
            specs.append(pl.BlockSpec((1, wdt), lambda b, i: (0, 0)))
        args.append(arr)
    specs += [pl.BlockSpec((1, TM, d), lambda b, i: (b, i + tile0, 0)),
              pl.BlockSpec(w.shape, lambda b, i: (0, 0)),
              pl.BlockSpec((1, 1, 6, d), lambda b, i: (b, jnp.minimum(i + tile0, 1), 0, 0)),
              pl.BlockSpec((1, d), lambda b, i: (0, 0)),
              pl.BlockSpec(rw.shape, lambda b, i: (0, 0))]
    args += [x, w, mod, g2.reshape(1, d), rw]
    lo = ntiles * TM
    return pl.pallas_call(
        functools.partial(_outproj_kernel, even=even),
        grid=(bsz, ntiles), in_specs=specs,
        out_specs=[pl.BlockSpec((1, TM, d), lambda b, i: (b, i, 0)),
                   pl.BlockSpec((1, TM, d), lambda b, i: (b, i, 0)),
                   pl.BlockSpec((1, TM, 128), lambda b, i: (b, i, 0))],
        out_shape=[jax.ShapeDtypeStruct((bsz, lo, d), F32), jax.ShapeDtypeStruct((bsz, lo, d), BF16),
                   jax.ShapeDtypeStruct((bsz, lo, 128), F32)],
        compiler_params=_cp(("parallel", "parallel")), name="outproj_even" if even else "outproj_odd",
    )(*args)


MOE_CH = 128
GROUP_SIZE = 4
N_GROUPS = N_EXPERTS // GROUP_SIZE


def _moe_kernel(*refs, residual):
    if residual:
        x_ref, g_ref, tri_ref, w1_ref, w3_ref, w2_ref, x1_ref, mod_ref, o_ref, xs_s, ys_s, gs_s, pt_s, plan_s = refs
    else:
        x_ref, g_ref, tri_ref, w1_ref, w3_ref, w2_ref, o_ref, xs_s, ys_s, gs_s, pt_s, plan_s = refs
    e = pl.program_id(1)
    tb = x_ref.shape[0]
    nch = xs_s.shape[0] // MOE_CH

    @pl.when(e == 0)
    def _plan():
        g = g_ref[...]
        lane = lax.broadcasted_iota(jnp.int32, g.shape, 1)
        oh = jnp.where(lane < N_GROUPS, g, 0.0)
        rank = jnp.dot(tri_ref[...], oh.astype(BF16), preferred_element_type=F32)
        cnt = jnp.sum(oh, axis=0, keepdims=True)
        lane1 = lax.broadcasted_iota(jnp.int32, (1, 128), 1)
        off = jnp.int32(0)
        offv = jnp.zeros((1, 128), F32)
        for gi in range(N_GROUPS):
            n = jnp.sum(jnp.where(lane1 == gi, cnt, 0.0)).astype(jnp.int32)
            nchunks = (n + (MOE_CH - 1)) // MOE_CH
            plan_s[gi] = off // MOE_CH
            plan_s[N_GROUPS + gi] = nchunks
            offv = offv + jnp.where(lane1 == gi, off.astype(F32), 0.0)
            off = off + nchunks * MOE_CH
        pos_col = jnp.sum(oh * (rank + offv), axis=1, keepdims=True)
        posb = jnp.broadcast_to(pos_col, (tb, 128))
        pos_row = jnp.concatenate([posb[i * 128:(i + 1) * 128, :].T[0:1, :] for i in range(tb // 128)], axis=1)
        x = x_ref[...]
        g_hi = g.astype(BF16)
        g_lo = (g - g_hi.astype(F32)).astype(BF16)
        lane_c = lax.broadcasted_iota(jnp.int32, (tb, MOE_CH), 1).astype(F32)
        row_c = lax.broadcasted_iota(jnp.int32, (MOE_CH, tb), 0).astype(F32)
        for c in range(nch):
            sl = slice(c * MOE_CH, (c + 1) * MOE_CH)
            pt_s[:, sl] = jnp.where(pos_col == lane_c + float(c * MOE_CH), 1.0, 0.0).astype(BF16)
            p = jnp.where(row_c + float(c * MOE_CH) == pos_row, 1.0, 0.0).astype(BF16)
            xs_s[sl, :] = jnp.dot(p, x, preferred_element_type=F32).astype(BF16)
            gs_s[sl, :] = (jnp.dot(p, g_hi, preferred_element_type=F32)
                           + jnp.dot(p, g_lo, preferred_element_type=F32))
        ys_s[...] = jnp.zeros_like(ys_s)

    grp = e // GROUP_SIZE
    c0 = plan_s[grp]
    lane_g = lax.broadcasted_iota(jnp.int32, (MOE_CH, 128), 1)

    def chunk(c, _):
        rows = pl.ds(pl.multiple_of(c * MOE_CH, MOE_CH), MOE_CH)
        xs = xs_s[rows, :]
        a = jnp.dot(xs, w1_ref[0], preferred_element_type=F32)
        b = jnp.dot(xs, w3_ref[0], preferred_element_type=F32)
        y = _dot(_silu(a) * b, w2_ref[0])
        ge = jnp.sum(jnp.where(lane_g == e + EXPERT_LANE0, gs_s[rows, :], 0.0), axis=-1, keepdims=True)
        ys_s[rows, :] += ge * y
        return 0
    lax.fori_loop(c0, c0 + plan_s[N_GROUPS + grp], chunk, 0)

    @pl.when(e == N_EXPERTS - 1)
    def _combine():
        out = jnp.dot(pt_s[...], ys_s[...].astype(BF16), preferred_element_type=F32)
        if residual:
            out = x1_ref[...] + mod_ref[0, 0][5:6, :] * out
        o_ref[...] = out


def _moe(h2, gate, w1, w3, w2, resid=None):
    t, d = h2.shape
    ne, _, ff = w1.shape
    tb = math.gcd(t, 1024)
    npad = tb + N_GROUPS * MOE_CH
    tri = jnp.asarray(np.tril(np.ones((tb, tb), np.float32), -1), BF16)
    extra_specs, extra_args = [], []
    if resid is not None:
        x1, mod = resid
        per_batch = t // mod.shape[0] // tb
        extra_specs = [pl.BlockSpec((tb, d), lambda i, e: (i, 0)),
                       pl.BlockSpec((1, 1, 6, d), lambda i, e: (i // per_batch, 1, 0, 0))]
        extra_args = [x1, mod]
    return pl.pallas_call(
        functools.partial(_moe_kernel, residual=resid is not None),
        grid=(t // tb, ne),
        in_specs=[pl.BlockSpec((tb, d), lambda i, e: (i, 0)),
                  pl.BlockSpec((tb, 128), lambda i, e: (i, 0)),
                  pl.BlockSpec((tb, tb), lambda i, e: (0, 0)),
                  pl.BlockSpec((1, d, ff), lambda i, e: (e, 0, 0)),
                  pl.BlockSpec((1, d, ff), lambda i, e: (e, 0, 0)),
                  pl.BlockSpec((1, ff, d), lambda i, e: (e, 0, 0))] + extra_specs,
        out_specs=pl.BlockSpec((tb, d), lambda i, e: (i, 0)),
        out_shape=jax.ShapeDtypeStruct((t, d), F32),
        scratch_shapes=[pltpu.VMEM((npad, d), BF16), pltpu.VMEM((npad, d), F32), pltpu.VMEM((npad, 128), F32),
                        pltpu.VMEM((tb, npad), BF16), pltpu.SMEM((2 * N_GROUPS,), jnp.int32)],
        compiler_params=_cp(("parallel", "arbitrary")), name="moe",
    )(h2, gate, tri, w1, w3, w2, *extra_args)


def _tri(q, rev):
    r = lax.broadcasted_iota(jnp.int32, (q, q), 0)
    c = lax.broadcasted_iota(jnp.int32, (q, q), 1)
    return (c >= r) if rev else (c <= r)


def _ssd_kernel(xs_ref, b_ref, c_ref, dt_ref, cwx_ref, cbx_ref, cwb_ref, cbb_ref, cwc_ref, cbc_ref,
                dtb_ref, alog_ref, dsk_ref, sel_ref, o_ref, xc_s, bc_s, cc_s, dt_s, *, nc):
    q = CHUNK
    first_lat = TM // q

    dtb = dtb_ref[...]

    def prep(i, _):
        t0 = pl.multiple_of(i * q, q)
        idx = pl.ds(t0, q)
        for src, cw, cb, dst in ((xs_ref, cwx_ref, cbx_ref, xc_s), (b_ref, cwb_ref, cbb_ref, bc_s),
                                 (c_ref, cwc_ref, cbc_ref, cc_s)):
            L = nc * q
            cur = src[0, idx, :]
            prev = src[0, pl.ds(pl.multiple_of(jnp.maximum(t0 - 8, 0), 8), 8), :]
            nxt = src[0, pl.ds(pl.multiple_of(jnp.minimum(t0 + q, L - 8), 8), 8), :]
            dst[idx, :] = _silu(_conv_vals(cur, prev, nxt, i, nc, cw[...], cb[...], q, first_lat))
        dt_s[idx, :] = _softplus(dt_ref[0, idx, :] + dtb)
        o_ref[0, idx, :] = dsk_ref[...] * xc_s[idx, :]
        return 0
    lax.fori_loop(0, nc, prep, 0)

    a_lane = jnp.broadcast_to(-jnp.exp(alog_ref[...]), (8, 128))
    a_e = [_dot_hi(a_lane, sel_ref[d, 0])[0:1, :] for d in (0, 1)]
    lane256 = lax.broadcasted_iota(jnp.int32, (q, 256), 1) // 64
    r8 = lax.broadcasted_iota(jnp.int32, (8, 256), 0)
    l8 = lax.broadcasted_iota(jnp.int32, (8, 256), 1)
    head_rows = (l8 == r8 * 64).astype(F32)

    def chain(d, j, h):
        rev = d == 1
        causal = _tri(q, rev)
        tri = causal.astype(F32)
        sel = sel_ref[d, 0]
        last = 0 if rev else q - 1
        if rev:
            ci = jnp.where(j < first_lat, first_lat - 1 - j, nc - 1 - (j - first_lat))
        else:
            ci = j
        idx = pl.ds(pl.multiple_of(ci * q, q), q)
        dt_e = _dot_hi(dt_s[idx, :], sel)
        xc = xc_s[idx, :]
        bc = bc_s[idx, :]
        cc = cc_s[idx, :]
        cb = _dot_nt(cc, bc)
        y_off = _dot(cc, h)
        yield
        cum_e = _dot_hi(tri, dt_e * a_e[d])
        dtx = xc * dt_e
        yield
        cum_rows = lax.dot_general(head_rows, cum_e, (((1,), (1,)), ((), ())), precision=HI,
                                   preferred_element_type=F32)
        tot_e = cum_e[last:last + 1, :]
        states = _dot(bc.T, dtx * jnp.exp(tot_e - cum_e))
        yield
        y = jnp.exp(cum_e) * y_off
        for r in range(4):
            seg = cum_e[:, r * 64:r * 64 + 1] - cum_rows[r:r + 1, :]
            m = cb * jnp.exp(jnp.where(causal, seg, NEG))
            y = y + _dot(m, jnp.where(lane256 == r, dtx, 0.0))
            yield
        o_ref[0, idx, :] += y
        return jnp.exp(tot_e) * h + states

    def chunk(j, hs):
        return tuple(_interleave([chain(0, j, hs[0]), chain(1, j, hs[1])]))
    h0 = jnp.zeros((128, 256), F32)
    lax.fori_loop(0, nc, chunk, (h0, h0))


def _ssd(proj, conv_w, conv_b, dt_bias, a_log, d_skip, cols):
    bsz, L, _ = proj.shape
    nc = L // CHUNK
    cx, cbm, ccm, cdt = cols
    nh = d_skip.shape[0]
    dtb = jnp.zeros((1, 128), F32).at[0, :2 * nh].set(dt_bias.reshape(-1))
    alog = jnp.zeros((1, 128), F32).at[0, :2 * nh].set(a_log.reshape(-1))
    dsk = jnp.repeat(d_skip.astype(F32), 64).reshape(1, nh * 64)
    sel = np.zeros((2, 2, 128, 256), np.float32)
    for d in range(2):
        for g in range(2):
            for r in range(4):
                sel[d, g, d * nh + g * 4 + r, r * 64:(r + 1) * 64] = 1.0
    cw_specs = []
    for width, off in ((256, 0), (128, 512), (128, 768)):
        cw_specs += [pl.BlockSpec((CONV_W, width), lambda b, g, off=off, width=width: (0, off // width + g)),
                     pl.BlockSpec((1, width), lambda b, g, off=off, width=width: (0, off // width + g))]
    return pl.pallas_call(
        functools.partial(_ssd_kernel, nc=nc),
        grid=(bsz, 2),
        in_specs=[pl.BlockSpec((1, L, 256), lambda b, g: (b, 0, cx // 256 + g)),
                  pl.BlockSpec((1, L, 128), lambda b, g: (b, 0, cbm // 128 + g)),
                  pl.BlockSpec((1, L, 128), lambda b, g: (b, 0, ccm // 128 + g)),
                  pl.BlockSpec((1, L, 128), lambda b, g: (b, 0, cdt // 128))] + cw_specs + [
                  pl.BlockSpec((1, 128), lambda b, g: (0, 0)),
                  pl.BlockSpec((1, 128), lambda b, g: (0, 0)),
                  pl.BlockSpec((1, 256), lambda b, g: (0, g)),
                  pl.BlockSpec((2, 1, 128, 256), lambda b, g: (0, g, 0, 0))],
        out_specs=pl.BlockSpec((1, L, 256), lambda b, g: (b, 0, g)),
        out_shape=jax.ShapeDtypeStruct((bsz, L, nh * 64), F32),
        scratch_shapes=[pltpu.VMEM((L, 256), F32), pltpu.VMEM((L, 128), F32), pltpu.VMEM((L, 128), F32),
                        pltpu.VMEM((L, 128), F32)],
        compiler_params=_cp(("parallel", "parallel")), name="ssd",
    )(proj, proj, proj, proj, conv_w, conv_b.reshape(1, -1), conv_w, conv_b.reshape(1, -1),
      conv_w, conv_b.reshape(1, -1), dtb, alog, dsk, jnp.asarray(sel))


ML_HP = 2


def _mlstm_kernel(q_ref, k_ref, v_ref, g_ref, cwq_ref, cbq_ref, cwk_ref, cbk_ref, gb_ref, sel_ref, o_ref,
                  qc_s, kc_s, *, nc, dh):
    q = CHUNK
    first_lat = TM // q

    def prep(i, _):
        t0 = pl.multiple_of(i * q, q)
        idx = pl.ds(t0, q)
        L = nc * q
        for src, cw, cb, dst, mul in ((q_ref, cwq_ref, cbq_ref, qc_s, 1.0), (k_ref, cwk_ref, cbk_ref, kc_s, dh ** -0.5)):
            cur = src[0, idx, :]
            prev = src[0, pl.ds(pl.multiple_of(jnp.maximum(t0 - 8, 0), 8), 8), :]
            nxt = src[0, pl.ds(pl.multiple_of(jnp.minimum(t0 + q, L - 8), 8), 8), :]
            dst[idx, :] = _silu(_conv_vals(cur, prev, nxt, i, nc, cw[...], cb[...], q, first_lat)) * mul
        return 0
    lax.fori_loop(0, nc, prep, 0)

    gb = gb_ref[...]
    lane = lax.broadcasted_iota(jnp.int32, (q, 128), 1)
    eye8 = (lax.broadcasted_iota(jnp.int32, (8, 128), 0) == lax.broadcasted_iota(jnp.int32, (8, 128), 1)).astype(F32)
    o_ref[...] = jnp.zeros_like(o_ref)

    def chain(hh, d, j, state):
        c_prev, n_prev, m_prev = state
        rev = d == 1
        causal = _tri(q, rev)
        tri = causal.astype(F32)
        sel = sel_ref[d, hh]
        last = 0 if rev else q - 1
        hs = slice(hh * dh, (hh + 1) * dh)
        if rev:
            ci = jnp.where(j < first_lat, first_lat - 1 - j, nc - 1 - (j - first_lat))
        else:
            ci = j
        idx = pl.ds(pl.multiple_of(ci * q, q), q)
        gsel = _dot_hi(g_ref[0, idx, :] + gb, sel)
        yield
        logf = jnp.minimum(gsel, 0.0) - jnp.log(1.0 + jnp.exp(-jnp.abs(gsel)))
        gi = jnp.where(lane == 0, gsel, jnp.where(lane == 1, logf, 0.0))
        cum = _dot_hi(tri, gi)
        yield
        rows = lax.dot_general(eye8, jnp.where(lane == 0, gi, cum), (((1,), (1,)), ((), ())), precision=HI,
                               preferred_element_type=F32)
        qc = qc_s[idx, hs]
        kc = kc_s[idx, hs]
        vc = v_ref[0, idx, hs]
        kct = kc.T
        qk = _dot(qc, kct)
        inter_c = _dot(qc, c_prev)
        yield
        i_col, i_row = gi[:, 0:1], rows[0:1, :]
        b_col, b_row = cum[:, 1:2], rows[1:2, :]
        dlog = jnp.where(causal, b_col - b_row + i_row, NEG)
        inter = b_col + m_prev
        m_t = jnp.maximum(jnp.max(dlog, axis=-1, keepdims=True), inter)
        w = jnp.exp(dlog - m_t) * qk
        g_in = jnp.exp(inter - m_t)
        b_end = b_col[last:last + 1, :]
        g_s = b_end - b_col + i_col
        m_new = jnp.maximum(jnp.max(g_s, axis=0, keepdims=True), b_end + m_prev)
        w_s = jnp.exp(g_s - m_new)
        keep = jnp.exp(b_end + m_prev - m_new)
        wv = _dot(w, vc)
        upd = _dot(kct, w_s * vc)
        yield
        num = wv + g_in * inter_c
        den = jnp.sum(w, axis=-1, keepdims=True) + g_in * jnp.sum(qc * n_prev, axis=-1, keepdims=True)
        o_ref[0, idx, hs] += num / jnp.maximum(jnp.abs(den), jnp.exp(-m_t))
        return (keep * c_prev + upd, keep * n_prev + jnp.sum(w_s * kc, axis=0, keepdims=True), m_new)

    chains = [(hh, d) for hh in range(ML_HP) for d in (0, 1)]

    def chunk(j, states):
        return tuple(_interleave([chain(hh, d, j, st) for (hh, d), st in zip(chains, states)]))
    init = (jnp.zeros((dh, dh), F32), jnp.zeros((1, dh), F32), jnp.full((1, 1), NEG, F32))
    lax.fori_loop(0, nc, chunk, tuple(init for _ in chains))


def _mlstm(proj, conv_w, conv_b, gate_b, cols):
    bsz, L, _ = proj.shape
    nc = L // CHUNK
    cq, ck, cv, cg = cols
    nh = gate_b.shape[-1]
    dh = conv_w.shape[1] // (2 * nh)
    bw = ML_HP * dh
    gbl = jnp.zeros((1, 128), F32).at[0, 16:16 + 4 * nh].set(gate_b.reshape(-1))
    sel = np.zeros((2, nh, 128, 128), np.float32)
    for d in range(2):
        for h in range(nh):
            sel[d, h, 16 + d * 2 * nh + h, 0] = 1.0
            sel[d, h, 16 + d * 2 * nh + nh + h, 1] = 1.0
    return pl.pallas_call(
        functools.partial(_mlstm_kernel, nc=nc, dh=dh),
        grid=(bsz, nh // ML_HP),
        in_specs=[pl.BlockSpec((1, L, bw), lambda b, h: (b, 0, cq // bw + h)),
                  pl.BlockSpec((1, L, bw), lambda b, h: (b, 0, ck // bw + h)),
                  pl.BlockSpec((1, L, bw), lambda b, h: (b, 0, cv // bw + h)),
                  pl.BlockSpec((1, L, 128), lambda b, h: (b, 0, cg // 128)),
                  pl.BlockSpec((CONV_W, bw), lambda b, h: (0, h)),
                  pl.BlockSpec((1, bw), lambda b, h: (0, h)),
                  pl.BlockSpec((CONV_W, bw), lambda b, h: (0, nh // ML_HP + h)),
                  pl.BlockSpec((1, bw), lambda b, h: (0, nh // ML_HP + h)),
                  pl.BlockSpec((1, 128), lambda b, h: (0, 0)),
                  pl.BlockSpec((2, ML_HP, 128, 128), lambda b, h: (0, h, 0, 0))],
        out_specs=pl.BlockSpec((1, L, bw), lambda b, h: (b, 0, h)),
        out_shape=jax.ShapeDtypeStruct((bsz, L, nh * dh), F32),
        scratch_shapes=[pltpu.VMEM((L, bw), F32), pltpu.VMEM((L, bw), F32)],
        compiler_params=_cp(("parallel", "parallel")), name="mlstm",
    )(proj, proj, proj, proj, conv_w, conv_b.reshape(1, -1), conv_w, conv_b.reshape(1, -1), gbl, jnp.asarray(sel))


def kernel(x, c, ctx, c_ctx, mod_w, mod_b, norm1_g, norm2_g, even_w_in, even_w_out, lru_conv_w, lru_conv_b, lru_wa, lru_ba, lru_wx, lru_bx, lru_lam, na_q_g, na_k_g, na_rpb, odd_w_in, odd_w_out, ssd_conv_w, ssd_conv_b, ssd_dt_bias, ssd_a_log, ssd_d, ssd_norm_g, ml_conv_w, ml_conv_b, ml_gate_b, ml_norm_g, moe_router_g, moe_router_e, moe_w1, moe_w3, moe_w2):
    bsz, S, d = x.shape
    lc = ctx.shape[1]
    assert lc == TM and S % TM == 0 and mod_w.shape[0] == 2
    nt = (lc + S) // TM
    L = lc + S

    cc = jnp.zeros((8, d), F32).at[0].set(c_ctx).at[1:1 + bsz].set(c)
    mod_all = _modulation(cc, mod_w, mod_b)

    def mod_for(l):
        m = mod_all[l].reshape(8, 6, d)
        return jnp.stack([jnp.broadcast_to(m[0], (bsz, 6, d)), m[1:1 + bsz]], axis=1)

    def router_w(l):
        return jnp.zeros((d, 128), F32).at[:, :EXPERT_LANE0].set(moe_router_g[l]) \
            .at[:, EXPERT_LANE0:EXPERT_LANE0 + N_EXPERTS].set(moe_router_e[l])

    xx = jnp.concatenate([ctx, x], axis=1)

    mod0 = mod_for(0)
    proj = _inproj(xx, mod0, norm1_g[0], even_w_in[0].astype(BF16))
    lw = lru_conv_w.shape[-1]
    r = _lru(proj, lru_conv_w[0], lru_conv_b[0], lru_wa[0], lru_ba[0], lru_wx[0], lru_bx[0], lru_lam[0])
    a = _na(proj, na_q_g[0], na_k_g[0], na_rpb[0], col0=2 * lw)
    x1, h2, gate = _outproj([(r, 0, lw), (proj, 1, lw), (a, 0, lw)], xx, even_w_out[0].astype(BF16), mod0,
                            norm2_g[0], router_w(0), even=True, tile0=0, ntiles=nt)
    moe0 = _moe(h2.reshape(bsz * L, d), gate.reshape(bsz * L, 128), moe_w1[0].astype(BF16),
                      moe_w3[0].astype(BF16), moe_w2[0].astype(BF16)).reshape(bsz, L, d)

    mod1 = mod_for(1)
    sw = ssd_d.shape[-1] * 64
    xbc = ssd_conv_w.shape[-1]
    mw = ml_norm_g.shape[-1]
    w = odd_w_in[0]
    o = np.cumsum([0, sw, xbc, 2 * ssd_d.shape[-1], mw, mw, mw, mw])
    small = jnp.concatenate([w[:, o[2]:o[3]], w[:, o[7]:], jnp.zeros((d, 128 - 2 * ssd_d.shape[-1] - (w.shape[1] - o[7])), F32)], axis=1)
    w_odd = jnp.concatenate([w[:, :o[2]], w[:, o[3]:o[7]], small], axis=1).astype(BF16)
    cz, cxs = 0, sw
    cB, cC = cxs + sw, cxs + sw + (xbc - sw) // 2
    cq = sw + xbc
    ck, cv, co, csm = cq + mw, cq + 2 * mw, cq + 3 * mw, cq + 4 * mw
    x0, proj1 = _inproj(x1, mod1, norm1_g[1], w_odd, prev=(moe0, mod0))
    ys = _ssd(proj1, ssd_conv_w[0], ssd_conv_b[0], ssd_dt_bias[0], ssd_a_log[0], ssd_d[0], (cxs, cB, cC, csm))
    hm = _mlstm(proj1, ml_conv_w[0], ml_conv_b[0], ml_gate_b[0], (cq, ck, cv, csm))
    x2, h2b, gate1 = _outproj([(ys, 0, sw), (proj1, cz // sw, sw), (hm, 0, mw), (proj1, co // mw, mw),
                               (ssd_norm_g[0].reshape(1, sw), 0, sw), (ml_norm_g[0].reshape(1, mw), 0, mw)],
                              x0, odd_w_out[0].astype(BF16), mod1, norm2_g[1], router_w(1),
                              even=False, tile0=1, ntiles=nt - 1)
    out = _moe(h2b.reshape(bsz * S, d), gate1.reshape(bsz * S, 128), moe_w1[1].astype(BF16),
               moe_w3[1].astype(BF16), moe_w2[1].astype(BF16), resid=(x2.reshape(bsz * S, d), mod1))
    return out.reshape(bsz, S, d)
```

```python
import functools
import math

import jax
import jax.numpy as jnp
import numpy as np
from jax import lax
from jax.experimental import pallas as pl
from jax.experimental.pallas import tpu as pltpu

F32 = jnp.float32
BF16 = jnp.bfloat16
HI = lax.Precision.HIGHEST

EPS = 1e-6
NEG = -1e30
GRID_W = 64
CONV_W = 4
LRU_C = 8.0
TM = 256
CHUNK = 128
NA_RQ = 4
NA_RK = 12
N_EXPERTS = 16
EXPERT_LANE0 = 4
VMEM_LIMIT = 56 * 1024 * 1024


def _cp(sem, vmem=VMEM_LIMIT):
    return pltpu.CompilerParams(dimension_semantics=sem, vmem_limit_bytes=vmem)


def _sigmoid(x):
    return jax.nn.sigmoid(x)


def _silu(x):
    return x * jax.nn.sigmoid(x)


def _softplus(x):
    return jnp.maximum(x, 0.0) + jnp.log(1.0 + jnp.exp(-jnp.abs(x)))


def _gelu_tanh(x):
    return 0.5 * x * (1.0 + jnp.tanh(math.sqrt(2.0 / math.pi) * (x + 0.044715 * (x * x * x))))


def _rms(x, axis=-1):
    return x * lax.rsqrt(jnp.mean(x * x, axis=axis, keepdims=True) + EPS)


def _dot(a, b):
    return jnp.dot(a.astype(BF16), b.astype(BF16), preferred_element_type=F32)


def _dot_hi(a, b):
    return jnp.dot(a, b, precision=HI, preferred_element_type=F32)


def _interleave(gens):
    results = [None] * len(gens)
    live = list(range(len(gens)))
    while live:
        for i in list(live):
            try:
                next(gens[i])
            except StopIteration as stop:
                results[i] = stop.value
                live.remove(i)
    return results


def _dot_nt(a, b):
    return lax.dot_general(a.astype(BF16), b.astype(BF16), (((1,), (1,)), ((), ())),
                           preferred_element_type=F32)


def _mod_kernel(c_ref, w_ref, b_ref, o_ref):
    c = c_ref[...]
    o_ref[0] = _dot_hi(_silu(c), w_ref[0]) + b_ref[0]


def _modulation(cc, mod_w, mod_b):
    depth, d, n = mod_w.shape
    tn = 1536
    return pl.pallas_call(
        _mod_kernel,
        grid=(depth, n // tn),
        in_specs=[pl.BlockSpec((8, d), lambda l, j: (0, 0)),
                  pl.BlockSpec((1, d, tn), lambda l, j: (l, 0, j)),
                  pl.BlockSpec((1, 1, tn), lambda l, j: (l, 0, j))],
        out_specs=pl.BlockSpec((1, 8, tn), lambda l, j: (l, 0, j)),
        out_shape=jax.ShapeDtypeStruct((depth, 8, n), F32),
        compiler_params=_cp(("arbitrary", "arbitrary")),
        name="adaln_mod",
    )(cc, mod_w, mod_b.reshape(depth, 1, n))


def _inproj_kernel(*refs, fuse_prev):
    if fuse_prev:
        x_ref, mo_ref, pmod_ref, mod_ref, g_ref, w_ref, xo_ref, p_ref = refs
        x = x_ref[0] + pmod_ref[0, 0][5:6, :] * mo_ref[0]
        xo_ref[0] = x
    else:
        x_ref, mod_ref, g_ref, w_ref, p_ref = refs
        x = x_ref[0]
    mod = mod_ref[0, 0]
    h = _rms(x) * g_ref[...] * (1.0 + mod[1:2, :]) + mod[0:1, :]
    p_ref[0] = _dot(h, w_ref[...])


def _seg_map(b, i):
    return (b, jnp.minimum(i, 1), 0, 0)


def _inproj(x, mod, g, w, prev=None):
    bsz, L, d = x.shape
    n = w.shape[1]
    nt = L // TM
    tok = pl.BlockSpec((1, TM, d), lambda b, i: (b, i, 0))
    modspec = pl.BlockSpec((1, 1, 6, d), _seg_map)
    tail = [modspec, pl.BlockSpec((1, d), lambda b, i: (0, 0)), pl.BlockSpec((d, n), lambda b, i: (0, 0))]
    pspec = pl.BlockSpec((1, TM, n), lambda b, i: (b, i, 0))
    pshape = jax.ShapeDtypeStruct((bsz, L, n), F32)
    if prev is None:
        return pl.pallas_call(
            functools.partial(_inproj_kernel, fuse_prev=False),
            grid=(bsz, nt), in_specs=[tok] + tail, out_specs=pspec, out_shape=pshape,
            compiler_params=_cp(("parallel", "parallel")), name="inproj",
        )(x, mod, g.reshape(1, d), w)
    moe_out, pmod = prev
    return pl.pallas_call(
        functools.partial(_inproj_kernel, fuse_prev=True),
        grid=(bsz, nt), in_specs=[tok, tok, modspec] + tail,
        out_specs=[tok, pspec], out_shape=[jax.ShapeDtypeStruct(x.shape, F32), pshape],
        compiler_params=_cp(("parallel", "parallel")), name="inproj_res",
    )(x, moe_out, pmod, mod, g.reshape(1, d), w)


def _conv_tile(ref, i, nt, cw, cb, width=TM):
    L = nt * width
    t0 = pl.multiple_of(i * width, width)
    cur = ref[0, pl.ds(t0, width), :]
    prev = ref[0, pl.ds(pl.multiple_of(jnp.maximum(t0 - 8, 0), 8), 8), :]
    nxt = ref[0, pl.ds(pl.multiple_of(jnp.minimum(t0 + width, L - 8), 8), 8), :]
    return _conv_vals(cur, prev, nxt, i, nt, cw, cb, width, first_lat=TM // width)


def _conv_vals(cur, prev, nxt, i, nt, cw, cb, width, first_lat):
    prev = jnp.where((i != 0) & (i != first_lat), prev, 0.0)
    nxt = jnp.where((i != first_lat - 1) & (i != nt - 1), nxt, 0.0)
    cat = jnp.concatenate([prev, cur, nxt], axis=0)
    return (cw[0:1] * cat[6:6 + width] + cw[1:2] * cat[7:7 + width] + cw[2:3] * cur
            + cw[3:4] * cat[9:9 + width] + cb)


def _lru_kernel(ux_ref, cw_ref, cb_ref, gw_ref, gb_ref, lam_ref, o_ref, *, nt):
    cw = cw_ref[...]
    cb = cb_ref[...]
    lam = lam_ref[0]
    sp = _softplus(-lam)
    row = lax.broadcasted_iota(jnp.int32, (TM, 128), 0) & 7

    def gates(i, d):
        xl = _conv_tile(ux_ref, i, nt, cw, cb)
        g = _dot(xl, gw_ref[0, d]) + gb_ref[0, d]
        r = _sigmoid(g[:, :128])
        ig = _sigmoid(g[:, 128:])
        log_a = -LRU_C * r * sp[d:d + 1]
        a = jnp.exp(log_a)
        u = jnp.sqrt(1.0 - a * a) * (ig * xl)
        return a, u

    def scan_tile(i, d, carry, accumulate):
        a, u = gates(i, d)
        rev = d == 1
        for k in (1, 2, 4):
            sh = TM - k if rev else k
            ok = (row < 8 - k) if rev else (row >= k)
            ash = pltpu.roll(a, sh, 0)
            ush = pltpu.roll(u, sh, 0)
            u = jnp.where(ok, u + a * ush, u)
            a = jnp.where(ok, a * ash, a)
        t0 = i * TM
        groups = range(TM // 8)
        for s in (reversed(groups) if rev else groups):
            h = u[s * 8:(s + 1) * 8] + a[s * 8:(s + 1) * 8] * carry
            carry = h[0:1] if rev else h[7:8]
            idx = pl.ds(pl.multiple_of(t0 + s * 8, 8), 8)
            if accumulate:
                o_ref[0, idx, :] += h
            else:
                o_ref[0, idx, :] = h
        return carry

    zero = jnp.zeros((1, 128), F32)
    lax.fori_loop(0, nt, lambda i, c: scan_tile(i, 0, c, False), zero)
    lax.fori_loop(0, nt, lambda j, c: scan_tile(jnp.where(j == 0, 0, nt - j), 1, c, True), zero)


def _lru(proj, conv_w, conv_b, wa, ba, wx, bx, lam):
    bsz, L, _ = proj.shape
    nt = L // TM
    width = conv_w.shape[1]
    ng = width // 128

    def blockdiag(w):
        w = w.reshape(2, ng, 2, 64, 64)
        z = jnp.zeros_like(w[:, :, 0])
        top = jnp.concatenate([w[:, :, 0], z], axis=-1)
        bot = jnp.concatenate([z, w[:, :, 1]], axis=-1)
        return jnp.concatenate([top, bot], axis=-2)
    gw = jnp.concatenate([blockdiag(wa), blockdiag(wx)], axis=-1).transpose(1, 0, 2, 3).astype(BF16)
    gb = jnp.concatenate([ba.reshape(2, ng, 1, 128), bx.reshape(2, ng, 1, 128)], axis=-1).transpose(1, 0, 2, 3)
    lam_g = lam.reshape(2, ng, 128).transpose(1, 0, 2)
    return pl.pallas_call(
        functools.partial(_lru_kernel, nt=nt),
        grid=(bsz, ng),
        in_specs=[pl.BlockSpec((1, L, 128), lambda b, c: (b, 0, c)),
                  pl.BlockSpec((CONV_W, 128), lambda b, c: (0, c)),
                  pl.BlockSpec((1, 128), lambda b, c: (0, c)),
                  pl.BlockSpec((1, 2, 128, 256), lambda b, c: (c, 0, 0, 0)),
                  pl.BlockSpec((1, 2, 1, 256), lambda b, c: (c, 0, 0, 0)),
                  pl.BlockSpec((1, 2, 128), lambda b, c: (c, 0, 0))],
        out_specs=pl.BlockSpec((1, L, 128), lambda b, c: (b, 0, c)),
        out_shape=jax.ShapeDtypeStruct((bsz, L, width), F32),
        compiler_params=_cp(("parallel", "parallel")), name="rglru",
    )(proj, conv_w, conv_b.reshape(1, width), gw, gb, lam_g)


def _na_bias_table(rpb, rows):
    nh = rpb.shape[0]
    win_r = (rpb.shape[1] + 1) // 2
    win_c = (rpb.shape[2] + 1) // 2
    qr = np.arange(NA_RQ)[:, None, None, None]
    qc = np.arange(GRID_W)[None, :, None, None]
    kr = np.arange(NA_RK)[None, None, :, None]
    kc = np.arange(GRID_W)[None, None, None, :]
    cstart = np.clip(qc - win_c // 2, 0, GRID_W - win_c)
    col_ok = (kc >= cstart) & (kc < cstart + win_c)
    dcol = np.clip(kc - qc + (win_c - 1), 0, 2 * win_c - 2)
    oc = (np.arange(2 * win_c - 1)[:, None, None] == dcol[0, :, 0, :][None]).astype(np.float32)
    o_rows, oks = [], []
    for r0, w0 in ((0, 0), (2 * NA_RQ, NA_RQ), (rows - NA_RQ, rows - NA_RK)):
        r = r0 + qr
        kabs = w0 + kr
        rstart = np.clip(r - win_r // 2, 0, rows - win_r)
        oks.append(np.broadcast_to((kabs >= rstart) & (kabs < rstart + win_r) & col_ok,
                                   (NA_RQ, GRID_W, NA_RK, GRID_W)))
        drow = np.clip(kabs - r + (win_r - 1), 0, 2 * win_r - 2)[:, 0, :, 0]
        o_rows.append((np.arange(2 * win_r - 1)[:, None, None] == drow[None]).astype(np.float32))
    t1 = jnp.einsum('hrc,prab->phabc', rpb.astype(F32), jnp.asarray(np.stack(o_rows)), precision=HI)
    b = jnp.einsum('phabc,cqk->phaqbk', t1, jnp.asarray(oc), precision=HI)
    b = jnp.where(jnp.asarray(np.stack(oks))[:, None], b, NEG)
    return b.reshape(3, nh, NA_RQ * GRID_W, NA_RK * GRID_W)


def _pair_rms(x, lo):
    x2 = x * x
    s0 = jnp.sum(jnp.where(lo, x2, 0.0), axis=-1, keepdims=True)
    s1 = jnp.sum(jnp.where(lo, 0.0, x2), axis=-1, keepdims=True)
    return x * lax.rsqrt(jnp.where(lo, s0, s1) * (2.0 / x.shape[-1]) + EPS)


def _na_kernel(q_ref, k_ref, v_ref, bias_ref, qg_ref, kg_ref, o_ref, kn_s, vb_s, *, rows, hd):
    i = pl.program_id(2)
    L = k_ref.shape[1]
    nkeys = NA_RK * GRID_W
    lo = lax.broadcasted_iota(jnp.int32, (1, 2 * hd), 1) < hd

    @pl.when(i == 0)
    def _prep():
        def body(t, _):
            idx = pl.ds(pl.multiple_of(t * TM, TM), TM)
            kn_s[idx, :] = (_pair_rms(k_ref[0, idx, :], lo) * kg_ref[...]).astype(BF16)
            vb_s[idx, :] = v_ref[0, idx, :].astype(BF16)
            return 0
        lax.fori_loop(0, L // TM, body, 0)

    qn = _pair_rms(q_ref[0], lo) * (qg_ref[...] * hd ** -0.5)
    q_h = [jnp.where(lo, qn, 0.0).astype(BF16), jnp.where(lo, 0.0, qn).astype(BF16)]
    kctx = kn_s[0:TM, :]
    vctx = vb_s[0:TM, :]

    def head(hh, kwin, vwin):
        s_c = _dot_nt(q_h[hh], kctx)
        if kwin is not None:
            s_w = _dot_nt(q_h[hh], kwin) + bias_ref[0, hh]
        yield
        m = jnp.max(s_c, axis=-1, keepdims=True)
        if kwin is not None:
            m = jnp.maximum(m, jnp.max(s_w, axis=-1, keepdims=True))
            p_w = jnp.exp(s_w - m)
        p_c = jnp.exp(s_c - m)
        den = jnp.sum(p_c, axis=-1, keepdims=True)
        num = _dot(p_c, vctx)
        if kwin is not None:
            den = den + jnp.sum(p_w, axis=-1, keepdims=True)
            num = num + _dot(p_w, vwin)
        yield
        return num / den

    def both(kwin, vwin):
        o0, o1 = _interleave([head(0, kwin, vwin), head(1, kwin, vwin)])
        o_ref[0] = jnp.where(lo, o0, o1)

    @pl.when(i == 0)
    def _():
        both(None, None)

    @pl.when(i > 0)
    def _():
        r0 = (i - 1) * NA_RQ
        w0 = jnp.clip(r0 - NA_RQ, 0, rows - NA_RK)
        start = pl.multiple_of(TM + w0 * GRID_W, GRID_W)
        both(kn_s[pl.ds(start, nkeys), :], vb_s[pl.ds(start, nkeys), :])


def _na(proj, q_g, k_g, rpb, col0):
    bsz, L, _ = proj.shape
    nh = rpb.shape[0]
    hd = q_g.shape[0]
    width = nh * hd
    rows = (L - TM) // GRID_W
    nb = L // TM
    bias = _na_bias_table(rpb, rows)
    qb, kb, vb = col0 // 128, (col0 + width) // 128, (col0 + 2 * width) // 128

    def pat(i):
        return jnp.where(i <= 1, 0, jnp.where(i == nb - 1, 2, 1))
    return pl.pallas_call(
        functools.partial(_na_kernel, rows=rows, hd=hd),
        grid=(bsz, width // 128, nb),
        in_specs=[pl.BlockSpec((1, TM, 128), lambda b, h, i: (b, i, qb + h)),
                  pl.BlockSpec((1, L, 128), lambda b, h, i: (b, 0, kb + h)),
                  pl.BlockSpec((1, L, 128), lambda b, h, i: (b, 0, vb + h)),
                  pl.BlockSpec((1, 2, TM, NA_RK * GRID_W), lambda b, h, i: (pat(i), h, 0, 0)),
                  pl.BlockSpec((1, 2 * hd), lambda b, h, i: (0, 0)),
                  pl.BlockSpec((1, 2 * hd), lambda b, h, i: (0, 0))],
        out_specs=pl.BlockSpec((1, TM, 128), lambda b, h, i: (b, i, h)),
        out_shape=jax.ShapeDtypeStruct((bsz, L, width), F32),
        scratch_shapes=[pltpu.VMEM((L, 128), BF16), pltpu.VMEM((L, 128), BF16)],
        compiler_params=_cp(("parallel", "parallel", "arbitrary")), name="nbr_attn",
    )(proj, proj, proj, bias, jnp.tile(q_g, 2).reshape(1, 2 * hd), jnp.tile(k_g, 2).reshape(1, 2 * hd))


def _route(lg):
    lane = lax.broadcasted_iota(jnp.int32, lg.shape, 1)
    lane_f = lane.astype(F32)
    is_g = lane < EXPERT_LANE0
    gl = jnp.where(is_g, lg, NEG)
    gmax = jnp.max(gl, axis=-1, keepdims=True)
    gsel = jnp.min(jnp.where(is_g & (gl == gmax), lane_f, 1e9), axis=-1, keepdims=True)
    g_w = 1.0 / jnp.sum(jnp.where(is_g, jnp.exp(gl - gmax), 0.0), axis=-1, keepdims=True)
    grp = ((lane - EXPERT_LANE0) >> 2).astype(F32)
    in_g = (lane >= EXPERT_LANE0) & (lane < EXPERT_LANE0 + N_EXPERTS) & (grp == gsel)
    el = jnp.where(in_g, lg, NEG)
    v1 = jnp.max(el, axis=-1, keepdims=True)
    i1 = jnp.min(jnp.where(in_g & (el == v1), lane_f, 1e9), axis=-1, keepdims=True)
    el2 = jnp.where(lane_f == i1, NEG, el)
    v2 = jnp.max(el2, axis=-1, keepdims=True)
    i2 = jnp.min(jnp.where(in_g & (lane_f != i1) & (el2 == v2), lane_f, 1e9), axis=-1, keepdims=True)
    t = jnp.exp(v2 - v1)
    w1 = g_w / (1.0 + t)
    w2 = g_w * t / (1.0 + t)
    return (jnp.where(lane_f == i1, w1, 0.0) + jnp.where(lane_f == i2, w2, 0.0)
            + jnp.where(lane_f == gsel, 1.0, 0.0))


def _outproj_kernel(*refs, even):
    if even:
        (r_ref, ug_ref, a_ref, x_ref, w_ref, mod_ref, g2_ref, rw_ref, x1_ref, h2_ref, gate_ref) = refs
    else:
        (ys_ref, z_ref, hm_ref, mo_ref, sg_ref, mg_ref, x_ref, w_ref, mod_ref, g2_ref, rw_ref,
         x1_ref, h2_ref, gate_ref) = refs
    mod = mod_ref[0, 0]

    def part(rs):
        if even:
            y_in = jnp.concatenate([r_ref[0, rs, :] * _gelu_tanh(ug_ref[0, rs, :]), a_ref[0, rs, :]], axis=-1)
        else:
            ys = ys_ref[0, rs, :] * _silu(z_ref[0, rs, :])
            sg = sg_ref[...]
            mg = mg_ref[...]
            hm = hm_ref[0, rs, :]
            sig_o = _sigmoid(mo_ref[0, rs, :])
            gw = ys.shape[-1] // 2
            parts = [_rms(ys[:, g * gw:(g + 1) * gw]) * sg[:, g * gw:(g + 1) * gw] for g in range(2)]
            hw = 128
            parts += [_rms(hm[:, h * hw:(h + 1) * hw]) * mg[:, h * hw:(h + 1) * hw] * sig_o[:, h * hw:(h + 1) * hw]
                      for h in range(hm.shape[-1] // hw)]
            y_in = jnp.concatenate(parts, axis=-1)
        y = _dot(y_in, w_ref[...])
        yield
        x1 = x_ref[0, rs, :] + mod[2:3, :] * y
        x1_ref[0, rs, :] = x1
        h2 = _rms(x1) * g2_ref[...] * (1.0 + mod[4:5, :]) + mod[3:4, :]
        h2_ref[0, rs, :] = h2.astype(BF16)
        hi = h2.astype(BF16)
        lo = (h2 - hi.astype(F32)).astype(BF16)
        lg2 = jnp.dot(hi, rw_ref[...], preferred_element_type=F32)
        lg1 = jnp.dot(lo, rw_ref[:, 0:128], preferred_element_type=F32)
        yield
        gate_ref[0, rs, :] = _route(lg2[:, 0:128] + lg2[:, 128:256] + lg1)

    nparts = 2
    rows = TM // nparts
    _interleave([part(pl.ds(p * rows, rows)) for p in range(nparts)])


def _outproj(mix_inputs, x, w, mod, g2, rw, even, tile0, ntiles):
    bsz, _, d = x.shape
    specs, args = [], []
    for arr, cb, wdt in mix_inputs:
        if arr.ndim == 3:
            specs.append(pl.BlockSpec((1, TM, wdt), lambda b, i, cb=cb: (b, i + tile0, cb)))
        else:
            specs.append(pl.BlockSpec((1, wdt), lambda b, i: (0, 0)))
        args.append(arr)
    specs += [pl.BlockSpec((1, TM, d), lambda b, i: (b, i + tile0, 0)),
              pl.BlockSpec(w.shape, lambda b, i: (0, 0)),
              pl.BlockSpec((1, 1, 6, d), lambda b, i: (b, jnp.minimum(i + tile0, 1), 0, 0)),
              pl.BlockSpec((1, d), lambda b, i: (0, 0)),
              pl.BlockSpec(rw.shape, lambda b, i: (0, 0))]
    args += [x, w, mod, g2.reshape(1, d), rw]
    lo = ntiles * TM
    return pl.pallas_call(
        functools.partial(_outproj_kernel, even=even),
        grid=(bsz, ntiles), in_specs=specs,
        out_specs=[pl.BlockSpec((1, TM, d), lambda b, i: (b, i, 0)),
                   pl.BlockSpec((1, TM, d), lambda b, i: (b, i, 0)),
                   pl.BlockSpec((1, TM, 128), lambda b, i: (b, i, 0))],
        out_shape=[jax.ShapeDtypeStruct((bsz, lo, d), F32), jax.ShapeDtypeStruct((bsz, lo, d), BF16),
                   jax.ShapeDtypeStruct((bsz, lo, 128), F32)],
        compiler_params=_cp(("parallel", "parallel")), name="outproj_even" if even else "outproj_odd",
    )(*args)


MOE_CH = 128
GROUP_SIZE = 4
N_GROUPS = N_EXPERTS // GROUP_SIZE


def _moe_kernel(*refs, residual):
    if residual:
        x_ref, g_ref, tri_ref, w1_ref, w3_ref, w2_ref, x1_ref, mod_ref, o_ref, xs_s, ys_s, gs_s, pt_s, plan_s = refs
    else:
        x_ref, g_ref, tri_ref, w1_ref, w3_ref, w2_ref, o_ref, xs_s, ys_s, gs_s, pt_s, plan_s = refs
    e = pl.program_id(1)
    tb = x_ref.shape[0]
    nch = xs_s.shape[0] // MOE_CH

    @pl.when(e == 0)
    def _plan():
        g = g_ref[...]
        lane = lax.broadcasted_iota(jnp.int32, g.shape, 1)
        oh = jnp.where(lane < N_GROUPS, g, 0.0)
        rank = jnp.dot(tri_ref[...], oh.astype(BF16), preferred_element_type=F32)
        cnt = jnp.sum(oh, axis=0, keepdims=True)
        lane1 = lax.broadcasted_iota(jnp.int32, (1, 128), 1)
        off = jnp.int32(0)
        offv = jnp.zeros((1, 128), F32)
        for gi in range(N_GROUPS):
            n = jnp.sum(jnp.where(lane1 == gi, cnt, 0.0)).astype(jnp.int32)
            nchunks = (n + (MOE_CH - 1)) // MOE_CH
            plan_s[gi] = off // MOE_CH
            plan_s[N_GROUPS + gi] = nchunks
            offv = offv + jnp.where(lane1 == gi, off.astype(F32), 0.0)
            off = off + nchunks * MOE_CH
        pos_col = jnp.sum(oh * (rank + offv), axis=1, keepdims=True)
        posb = jnp.broadcast_to(pos_col, (tb, 128))
        pos_row = jnp.concatenate([posb[i * 128:(i + 1) * 128, :].T[0:1, :] for i in range(tb // 128)], axis=1)
        x = x_ref[...]
        g_hi = g.astype(BF16)
        g_lo = (g - g_hi.astype(F32)).astype(BF16)
        lane_c = lax.broadcasted_iota(jnp.int32, (tb, MOE_CH), 1).astype(F32)
        row_c = lax.broadcasted_iota(jnp.int32, (MOE_CH, tb), 0).astype(F32)
        for c in range(nch):
            sl = slice(c * MOE_CH, (c + 1) * MOE_CH)
            pt_s[:, sl] = jnp.where(pos_col == lane_c + float(c * MOE_CH), 1.0, 0.0).astype(BF16)
            p = jnp.where(row_c + float(c * MOE_CH) == pos_row, 1.0, 0.0).astype(BF16)
            xs_s[sl, :] = jnp.dot(p, x, preferred_element_type=F32).astype(BF16)
            gs_s[sl, :] = (jnp.dot(p, g_hi, preferred_element_type=F32)
                           + jnp.dot(p, g_lo, preferred_element_type=F32))
        ys_s[...] = jnp.zeros_like(ys_s)

    grp = e // GROUP_SIZE
    c0 = plan_s[grp]
    lane_g = lax.broadcasted_iota(jnp.int32, (MOE_CH, 128), 1)

    def chunk(c, _):
        rows = pl.ds(pl.multiple_of(c * MOE_CH, MOE_CH), MOE_CH)
        xs = xs_s[rows, :]
        a = jnp.dot(xs, w1_ref[0], preferred_element_type=F32)
        b = jnp.dot(xs, w3_ref[0], preferred_element_type=F32)
        y = _dot(_silu(a) * b, w2_ref[0])
        ge = jnp.sum(jnp.where(lane_g == e + EXPERT_LANE0, gs_s[rows, :], 0.0), axis=-1, keepdims=True)
        ys_s[rows, :] += ge * y
        return 0
    lax.fori_loop(c0, c0 + plan_s[N_GROUPS + grp], chunk, 0)

    @pl.when(e == N_EXPERTS - 1)
    def _combine():
        out = jnp.dot(pt_s[...], ys_s[...].astype(BF16), preferred_element_type=F32)
        if residual:
            out = x1_ref[...] + mod_ref[0, 0][5:6, :] * out
        o_ref[...] = out


def _moe(h2, gate, w1, w3, w2, resid=None):
    t, d = h2.shape
    ne, _, ff = w1.shape
    tb = math.gcd(t, 1024)
    npad = tb + N_GROUPS * MOE_CH
    tri = jnp.asarray(np.tril(np.ones((tb, tb), np.float32), -1), BF16)
    extra_specs, extra_args = [], []
    if resid is not None:
        x1, mod = resid
        per_batch = t // mod.shape[0] // tb
        extra_specs = [pl.BlockSpec((tb, d), lambda i, e: (i, 0)),
                       pl.BlockSpec((1, 1, 6, d), lambda i, e: (i // per_batch, 1, 0, 0))]
        extra_args = [x1, mod]
    return pl.pallas_call(
        functools.partial(_moe_kernel, residual=resid is not None),
        grid=(t // tb, ne),
        in_specs=[pl.BlockSpec((tb, d), lambda i, e: (i, 0)),
                  pl.BlockSpec((tb, 128), lambda i, e: (i, 0)),
                  pl.BlockSpec((tb, tb), lambda i, e: (0, 0)),
                  pl.BlockSpec((1, d, ff), lambda i, e: (e, 0, 0)),
                  pl.BlockSpec((1, d, ff), lambda i, e: (e, 0, 0)),
                  pl.BlockSpec((1, ff, d), lambda i, e: (e, 0, 0))] + extra_specs,
        out_specs=pl.BlockSpec((tb, d), lambda i, e: (i, 0)),
        out_shape=jax.ShapeDtypeStruct((t, d), F32),
        scratch_shapes=[pltpu.VMEM((npad, d), BF16), pltpu.VMEM((npad, d), F32), pltpu.VMEM((npad, 128), F32),
                        pltpu.VMEM((tb, npad), BF16), pltpu.SMEM((2 * N_GROUPS,), jnp.int32)],
        compiler_params=_cp(("parallel", "arbitrary")), name="moe",
    )(h2, gate, tri, w1, w3, w2, *extra_args)


def _tri(q, rev):
    r = lax.broadcasted_iota(jnp.int32, (q, q), 0)
    c = lax.broadcasted_iota(jnp.int32, (q, q), 1)
    return (c >= r) if rev else (c <= r)


def _ssd_kernel(xs_ref, b_ref, c_ref, dt_ref, cwx_ref, cbx_ref, cwb_ref, cbb_ref, cwc_ref, cbc_ref,
                dtb_ref, alog_ref, dsk_ref, sel_ref, o_ref, xc_s, bc_s, cc_s, dt_s, *, nc):
    q = CHUNK
    first_lat = TM // q

    dtb = dtb_ref[...]

    def prep(i, _):
        t0 = pl.multiple_of(i * q, q)
        idx = pl.ds(t0, q)
        for src, cw, cb, dst in ((xs_ref, cwx_ref, cbx_ref, xc_s), (b_ref, cwb_ref, cbb_ref, bc_s),
                                 (c_ref, cwc_ref, cbc_ref, cc_s)):
            L = nc * q
            cur = src[0, idx, :]
            prev = src[0, pl.ds(pl.multiple_of(jnp.maximum(t0 - 8, 0), 8), 8), :]
            nxt = src[0, pl.ds(pl.multiple_of(jnp.minimum(t0 + q, L - 8), 8), 8), :]
            dst[idx, :] = _silu(_conv_vals(cur, prev, nxt, i, nc, cw[...], cb[...], q, first_lat))
        dt_s[idx, :] = _softplus(dt_ref[0, idx, :] + dtb)
        o_ref[0, idx, :] = dsk_ref[...] * xc_s[idx, :]
        return 0
    lax.fori_loop(0, nc, prep, 0)

    a_lane = jnp.broadcast_to(-jnp.exp(alog_ref[...]), (8, 128))
    a_e = [_dot_hi(a_lane, sel_ref[d, 0])[0:1, :] for d in (0, 1)]
    lane256 = lax.broadcasted_iota(jnp.int32, (q, 256), 1) // 64
    r8 = lax.broadcasted_iota(jnp.int32, (8, 256), 0)
    l8 = lax.broadcasted_iota(jnp.int32, (8, 256), 1)
    head_rows = (l8 == r8 * 64).astype(F32)

    def chain(d, j, h):
        rev = d == 1
        causal = _tri(q, rev)
        tri = causal.astype(F32)
        sel = sel_ref[d, 0]
        last = 0 if rev else q - 1
        if rev:
            ci = jnp.where(j < first_lat, first_lat - 1 - j, nc - 1 - (j - first_lat))
        else:
            ci = j
        idx = pl.ds(pl.multiple_of(ci * q, q), q)
        dt_e = _dot_hi(dt_s[idx, :], sel)
        xc = xc_s[idx, :]
        bc = bc_s[idx, :]
        cc = cc_s[idx, :]
        cb = _dot_nt(cc, bc)
        y_off = _dot(cc, h)
        yield
        cum_e = _dot_hi(tri, dt_e * a_e[d])
        dtx = xc * dt_e
        yield
        cum_rows = lax.dot_general(head_rows, cum_e, (((1,), (1,)), ((), ())), precision=HI,
                                   preferred_element_type=F32)
        tot_e = cum_e[last:last + 1, :]
        states = _dot(bc.T, dtx * jnp.exp(tot_e - cum_e))
        yield
        y = jnp.exp(cum_e) * y_off
        for r in range(4):
            seg = cum_e[:, r * 64:r * 64 + 1] - cum_rows[r:r + 1, :]
            m = cb * jnp.exp(jnp.where(causal, seg, NEG))
            y = y + _dot(m, jnp.where(lane256 == r, dtx, 0.0))
            yield
        o_ref[0, idx, :] += y
        return jnp.exp(tot_e) * h + states

    def chunk(j, hs):
        return tuple(_interleave([chain(0, j, hs[0]), chain(1, j, hs[1])]))
    h0 = jnp.zeros((128, 256), F32)
    lax.fori_loop(0, nc, chunk, (h0, h0))


def _ssd(proj, conv_w, conv_b, dt_bias, a_log, d_skip, cols):
    bsz, L, _ = proj.shape
    nc = L // CHUNK
    cx, cbm, ccm, cdt = cols
    nh = d_skip.shape[0]
    dtb = jnp.zeros((1, 128), F32).at[0, :2 * nh].set(dt_bias.reshape(-1))
    alog = jnp.zeros((1, 128), F32).at[0, :2 * nh].set(a_log.reshape(-1))
    dsk = jnp.repeat(d_skip.astype(F32), 64).reshape(1, nh * 64)
    sel = np.zeros((2, 2, 128, 256), np.float32)
    for d in range(2):
        for g in range(2):
            for r in range(4):
                sel[d, g, d * nh + g * 4 + r, r * 64:(r + 1) * 64] = 1.0
    cw_specs = []
    for width, off in ((256, 0), (128, 512), (128, 768)):
        cw_specs += [pl.BlockSpec((CONV_W, width), lambda b, g, off=off, width=width: (0, off // width + g)),
                     pl.BlockSpec((1, width), lambda b, g, off=off, width=width: (0, off // width + g))]
    return pl.pallas_call(
        functools.partial(_ssd_kernel, nc=nc),
        grid=(bsz, 2),
        in_specs=[pl.BlockSpec((1, L, 256), lambda b, g: (b, 0, cx // 256 + g)),
                  pl.BlockSpec((1, L, 128), lambda b, g: (b, 0, cbm // 128 + g)),
                  pl.BlockSpec((1, L, 128), lambda b, g: (b, 0, ccm // 128 + g)),
                  pl.BlockSpec((1, L, 128), lambda b, g: (b, 0, cdt // 128))] + cw_specs + [
                  pl.BlockSpec((1, 128), lambda b, g: (0, 0)),
                  pl.BlockSpec((1, 128), lambda b, g: (0, 0)),
                  pl.BlockSpec((1, 256), lambda b, g: (0, g)),
                  pl.BlockSpec((2, 1, 128, 256), lambda b, g: (0, g, 0, 0))],
        out_specs=pl.BlockSpec((1, L, 256), lambda b, g: (b, 0, g)),
        out_shape=jax.ShapeDtypeStruct((bsz, L, nh * 64), F32),
        scratch_shapes=[pltpu.VMEM((L, 256), F32), pltpu.VMEM((L, 128), F32), pltpu.VMEM((L, 128), F32),
                        pltpu.VMEM((L, 128), F32)],
        compiler_params=_cp(("parallel", "parallel")), name="ssd",
    )(proj, proj, proj, proj, conv_w, conv_b.reshape(1, -1), conv_w, conv_b.reshape(1, -1),
      conv_w, conv_b.reshape(1, -1), dtb, alog, dsk, jnp.asarray(sel))


ML_HP = 2


def _mlstm_kernel(q_ref, k_ref, v_ref, g_ref, cwq_ref, cbq_ref, cwk_ref, cbk_ref, gb_ref, sel_ref, o_ref,
                  qc_s, kc_s, *, nc, dh):
    q = CHUNK
    first_lat = TM // q

    def prep(i, _):
        t0 = pl.multiple_of(i * q, q)
        idx = pl.ds(t0, q)
        L = nc * q
        for src, cw, cb, dst, mul in ((q_ref, cwq_ref, cbq_ref, qc_s, 1.0), (k_ref, cwk_ref, cbk_ref, kc_s, dh ** -0.5)):
            cur = src[0, idx, :]
            prev = src[0, pl.ds(pl.multiple_of(jnp.maximum(t0 - 8, 0), 8), 8), :]
            nxt = src[0, pl.ds(pl.multiple_of(jnp.minimum(t0 + q, L - 8), 8), 8), :]
            dst[idx, :] = _silu(_conv_vals(cur, prev, nxt, i, nc, cw[...], cb[...], q, first_lat)) * mul
        return 0
    lax.fori_loop(0, nc, prep, 0)

    gb = gb_ref[...]
    lane = lax.broadcasted_iota(jnp.int32, (q, 128), 1)
    eye8 = (lax.broadcasted_iota(jnp.int32, (8, 128), 0) == lax.broadcasted_iota(jnp.int32, (8, 128), 1)).astype(F32)
    o_ref[...] = jnp.zeros_like(o_ref)

    def chain(hh, d, j, state):
        c_prev, n_prev, m_prev = state
        rev = d == 1
        causal = _tri(q, rev)
        tri = causal.astype(F32)
        sel = sel_ref[d, hh]
        last = 0 if rev else q - 1
        hs = slice(hh * dh, (hh + 1) * dh)
        if rev:
            ci = jnp.where(j < first_lat, first_lat - 1 - j, nc - 1 - (j - first_lat))
        else:
            ci = j
        idx = pl.ds(pl.multiple_of(ci * q, q), q)
        gsel = _dot_hi(g_ref[0, idx, :] + gb, sel)
        yield
        logf = jnp.minimum(gsel, 0.0) - jnp.log(1.0 + jnp.exp(-jnp.abs(gsel)))
        gi = jnp.where(lane == 0, gsel, jnp.where(lane == 1, logf, 0.0))
        cum = _dot_hi(tri, gi)
        yield
        rows = lax.dot_general(eye8, jnp.where(lane == 0, gi, cum), (((1,), (1,)), ((), ())), precision=HI,
                               preferred_element_type=F32)
        qc = qc_s[idx, hs]
        kc = kc_s[idx, hs]
        vc = v_ref[0, idx, hs]
        kct = kc.T
        qk = _dot(qc, kct)
        inter_c = _dot(qc, c_prev)
        yield
        i_col, i_row = gi[:, 0:1], rows[0:1, :]
        b_col, b_row = cum[:, 1:2], rows[1:2, :]
        dlog = jnp.where(causal, b_col - b_row + i_row, NEG)
        inter = b_col + m_prev
        m_t = jnp.maximum(jnp.max(dlog, axis=-1, keepdims=True), inter)
        w = jnp.exp(dlog - m_t) * qk
        g_in = jnp.exp(inter - m_t)
        b_end = b_col[last:last + 1, :]
        g_s = b_end - b_col + i_col
        m_new = jnp.maximum(jnp.max(g_s, axis=0, keepdims=True), b_end + m_prev)
        w_s = jnp.exp(g_s - m_new)
        keep = jnp.exp(b_end + m_prev - m_new)
        wv = _dot(w, vc)
        upd = _dot(kct, w_s * vc)
        yield
        num = wv + g_in * inter_c
        den = jnp.sum(w, axis=-1, keepdims=True) + g_in * jnp.sum(qc * n_prev, axis=-1, keepdims=True)
        o_ref[0, idx, hs] += num / jnp.maximum(jnp.abs(den), jnp.exp(-m_t))
        return (keep * c_prev + upd, keep * n_prev + jnp.sum(w_s * kc, axis=0, keepdims=True), m_new)

    chains = [(hh, d) for hh in range(ML_HP) for d in (0, 1)]

    def chunk(j, states):
        return tuple(_interleave([chain(hh, d, j, st) for (hh, d), st in zip(chains, states)]))
    init = (jnp.zeros((dh, dh), F32), jnp.zeros((1, dh), F32), jnp.full((1, 1), NEG, F32))
    lax.fori_loop(0, nc, chunk, tuple(init for _ in chains))


def _mlstm(proj, conv_w, conv_b, gate_b, cols):
    bsz, L, _ = proj.shape
    nc = L // CHUNK
    cq, ck, cv, cg = cols
    nh = gate_b.shape[-1]
    dh = conv_w.shape[1] // (2 * nh)
    bw = ML_HP * dh
    gbl = jnp.zeros((1, 128), F32).at[0, 16:16 + 4 * nh].set(gate_b.reshape(-1))
    sel = np.zeros((2, nh, 128, 128), np.float32)
    for d in range(2):
        for h in range(nh):
            sel[d, h, 16 + d * 2 * nh + h, 0] = 1.0
            sel[d, h, 16 + d * 2 * nh + nh + h, 1] = 1.0
    return pl.pallas_call(
        functools.partial(_mlstm_kernel, nc=nc, dh=dh),
        grid=(bsz, nh // ML_HP),
        in_specs=[pl.BlockSpec((1, L, bw), lambda b, h: (b, 0, cq // bw + h)),
                  pl.BlockSpec((1, L, bw), lambda b, h: (b, 0, ck // bw + h)),
                  pl.BlockSpec((1, L, bw), lambda b, h: (b, 0, cv // bw + h)),
                  pl.BlockSpec((1, L, 128), lambda b, h: (b, 0, cg // 128)),
                  pl.BlockSpec((CONV_W, bw), lambda b, h: (0, h)),
                  pl.BlockSpec((1, bw), lambda b, h: (0, h)),
                  pl.BlockSpec((CONV_W, bw), lambda b, h: (0, nh // ML_HP + h)),
                  pl.BlockSpec((1, bw), lambda b, h: (0, nh // ML_HP + h)),
                  pl.BlockSpec((1, 128), lambda b, h: (0, 0)),
                  pl.BlockSpec((2, ML_HP, 128, 128), lambda b, h: (0, h, 0, 0))],
        out_specs=pl.BlockSpec((1, L, bw), lambda b, h: (b, 0, h)),
        out_shape=jax.ShapeDtypeStruct((bsz, L, nh * dh), F32),
        scratch_shapes=[pltpu.VMEM((L, bw), F32), pltpu.VMEM((L, bw), F32)],
        compiler_params=_cp(("parallel", "parallel")), name="mlstm",
    )(proj, proj, proj, proj, conv_w, conv_b.reshape(1, -1), conv_w, conv_b.reshape(1, -1), gbl, jnp.asarray(sel))


def kernel(x, c, ctx, c_ctx, mod_w, mod_b, norm1_g, norm2_g, even_w_in, even_w_out, lru_conv_w, lru_conv_b, lru_wa, lru_ba, lru_wx, lru_bx, lru_lam, na_q_g, na_k_g, na_rpb, odd_w_in, odd_w_out, ssd_conv_w, ssd_conv_b, ssd_dt_bias, ssd_a_log, ssd_d, ssd_norm_g, ml_conv_w, ml_conv_b, ml_gate_b, ml_norm_g, moe_router_g, moe_router_e, moe_w1, moe_w3, moe_w2):
    bsz, S, d = x.shape
    lc = ctx.shape[1]
    assert lc == TM and S % TM == 0 and mod_w.shape[0] == 2
    nt = (lc + S) // TM
    L = lc + S

    cc = jnp.zeros((8, d), F32).at[0].set(c_ctx).at[1:1 + bsz].set(c)
    mod_all = _modulation(cc, mod_w, mod_b)

    def mod_for(l):
        m = mod_all[l].reshape(8, 6, d)
        return jnp.stack([jnp.broadcast_to(m[0], (bsz, 6, d)), m[1:1 + bsz]], axis=1)

    def router_w(l):
        w = jnp.zeros((d, 128), F32).at[:, :EXPERT_LANE0].set(moe_router_g[l]) \
            .at[:, EXPERT_LANE0:EXPERT_LANE0 + N_EXPERTS].set(moe_router_e[l])
        hi = w.astype(BF16)
        return jnp.concatenate([hi, (w - hi.astype(F32)).astype(BF16)], axis=1)

    xx = jnp.concatenate([ctx, x], axis=1)

    mod0 = mod_for(0)
    proj = _inproj(xx, mod0, norm1_g[0], even_w_in[0].astype(BF16))
    lw = lru_conv_w.shape[-1]
    r = _lru(proj, lru_conv_w[0], lru_conv_b[0], lru_wa[0], lru_ba[0], lru_wx[0], lru_bx[0], lru_lam[0])
    a = _na(proj, na_q_g[0], na_k_g[0], na_rpb[0], col0=2 * lw)
    x1, h2, gate = _outproj([(r, 0, lw), (proj, 1, lw), (a, 0, lw)], xx, even_w_out[0].astype(BF16), mod0,
                            norm2_g[0], router_w(0), even=True, tile0=0, ntiles=nt)
    moe0 = _moe(h2.reshape(bsz * L, d), gate.reshape(bsz * L, 128), moe_w1[0].astype(BF16),
                      moe_w3[0].astype(BF16), moe_w2[0].astype(BF16)).reshape(bsz, L, d)

    mod1 = mod_for(1)
    sw = ssd_d.shape[-1] * 64
    xbc = ssd_conv_w.shape[-1]
    mw = ml_norm_g.shape[-1]
    w = odd_w_in[0]
    o = np.cumsum([0, sw, xbc, 2 * ssd_d.shape[-1], mw, mw, mw, mw])
    small = jnp.concatenate([w[:, o[2]:o[3]], w[:, o[7]:], jnp.zeros((d, 128 - 2 * ssd_d.shape[-1] - (w.shape[1] - o[7])), F32)], axis=1)
    w_odd = jnp.concatenate([w[:, :o[2]], w[:, o[3]:o[7]], small], axis=1).astype(BF16)
    cz, cxs = 0, sw
    cB, cC = cxs + sw, cxs + sw + (xbc - sw) // 2
    cq = sw + xbc
    ck, cv, co, csm = cq + mw, cq + 2 * mw, cq + 3 * mw, cq + 4 * mw
    x0, proj1 = _inproj(x1, mod1, norm1_g[1], w_odd, prev=(moe0, mod0))
    ys = _ssd(proj1, ssd_conv_w[0], ssd_conv_b[0], ssd_dt_bias[0], ssd_a_log[0], ssd_d[0], (cxs, cB, cC, csm))
    hm = _mlstm(proj1, ml_conv_w[0], ml_conv_b[0], ml_gate_b[0], (cq, ck, cv, csm))
    x2, h2b, gate1 = _outproj([(ys, 0, sw), (proj1, cz // sw, sw), (hm, 0, mw), (proj1, co // mw, mw),
                               (ssd_norm_g[0].reshape(1, sw), 0, sw), (ml_norm_g[0].reshape(1, mw), 0, mw)],
                              x0, odd_w_out[0].astype(BF16), mod1, norm2_g[1], router_w(1),
                              even=False, tile0=1, ntiles=nt - 1)
    out = _moe(h2b.reshape(bsz * S, d), gate1.reshape(bsz * S, 128), moe_w1[1].astype(BF16),
               moe_w3[1].astype(BF16), moe_w2[1].astype(BF16), resid=(x2.reshape(bsz * S, d), mod1))
    return out.reshape(bsz, S, d)
```

```python
import functools
import math

import jax
import jax.numpy as jnp
import numpy as np
from jax import lax
from jax.experimental import pallas as pl
from jax.experimental.pallas import tpu as pltpu

F32 = jnp.float32
BF16 = jnp.bfloat16
HI = lax.Precision.HIGHEST

EPS = 1e-6
NEG = -1e30
GRID_W = 64
CONV_W = 4
LRU_C = 8.0
TM = 256
CHUNK = 128
NA_RQ = 4
NA_RK = 12
N_EXPERTS = 16
EXPERT_LANE0 = 4
VMEM_LIMIT = 56 * 1024 * 1024


def _cp(sem, vmem=VMEM_LIMIT):
    return pltpu.CompilerParams(dimension_semantics=sem, vmem_limit_bytes=vmem)


def _sigmoid(x):
    return jax.nn.sigmoid(x)


def _silu(x):
    return x * jax.nn.sigmoid(x)


def _softplus(x):
    return jnp.maximum(x, 0.0) + jnp.log(1.0 + jnp.exp(-jnp.abs(x)))


def _gelu_tanh(x):
    return 0.5 * x * (1.0 + jnp.tanh(math.sqrt(2.0 / math.pi) * (x + 0.044715 * (x * x * x))))


def _rms(x, axis=-1):
    return x * lax.rsqrt(jnp.mean(x * x, axis=axis, keepdims=True) + EPS)


def _dot(a, b):
    return jnp.dot(a.astype(BF16), b.astype(BF16), preferred_element_type=F32)


def _dot_hi(a, b):
    return jnp.dot(a, b, precision=HI, preferred_element_type=F32)


def _split3(x):
    x1 = x.astype(BF16)
    r = x - x1.astype(F32)
    x2 = r.astype(BF16)
    x3 = (r - x2.astype(F32)).astype(BF16)
    return x1, x2, x3


def _dot_sel(sel, x, dims=(((1,), (0,)), ((), ()))):
    s = sel.astype(BF16)
    x1, x2, x3 = _split3(x)
    d = lambda xi: lax.dot_general(s, xi, dims, preferred_element_type=F32)
    return (d(x3) + d(x2)) + d(x1)


def _dot_sel_r(x, sel):
    s = sel.astype(BF16)
    x1, x2, x3 = _split3(x)
    d = lambda xi: jnp.dot(xi, s, preferred_element_type=F32)
    return (d(x3) + d(x2)) + d(x1)


_NT = (((1,), (1,)), ((), ()))


def _interleave(gens):
    results = [None] * len(gens)
    live = list(range(len(gens)))
    while live:
        for i in list(live):
            try:
                next(gens[i])
            except StopIteration as stop:
                results[i] = stop.value
                live.remove(i)
    return results


def _dot_nt(a, b):
    return lax.dot_general(a.astype(BF16), b.astype(BF16), (((1,), (1,)), ((), ())),
                           preferred_element_type=F32)


def _mod_kernel(c_ref, w_ref, b_ref, o_ref):
    c = c_ref[...]
    o_ref[0] = _dot_hi(_silu(c), w_ref[0]) + b_ref[0]


def _modulation(cc, mod_w, mod_b):
    depth, d, n = mod_w.shape
    tn = 1536
    return pl.pallas_call(
        _mod_kernel,
        grid=(depth, n // tn),
        in_specs=[pl.BlockSpec((8, d), lambda l, j: (0, 0)),
                  pl.BlockSpec((1, d, tn), lambda l, j: (l, 0, j)),
                  pl.BlockSpec((1, 1, tn), lambda l, j: (l, 0, j))],
        out_specs=pl.BlockSpec((1, 8, tn), lambda l, j: (l, 0, j)),
        out_shape=jax.ShapeDtypeStruct((depth, 8, n), F32),
        compiler_params=_cp(("arbitrary", "arbitrary")),
        name="adaln_mod",
    )(cc, mod_w, mod_b.reshape(depth, 1, n))


def _inproj_kernel(*refs, fuse_prev):
    if fuse_prev:
        x_ref, mo_ref, pmod_ref, mod_ref, g_ref, w_ref, xo_ref, p_ref = refs
        x = x_ref[0] + pmod_ref[0, 0][5:6, :] * mo_ref[0]
        xo_ref[0] = x
    else:
        x_ref, mod_ref, g_ref, w_ref, p_ref = refs
        x = x_ref[0]
    mod = mod_ref[0, 0]
    h = _rms(x) * g_ref[...] * (1.0 + mod[1:2, :]) + mod[0:1, :]
    p_ref[0] = _dot(h, w_ref[...])


def _seg_map(b, i):
    return (b, jnp.minimum(i, 1), 0, 0)


def _inproj(x, mod, g, w, prev=None):
    bsz, L, d = x.shape
    n = w.shape[1]
    nt = L // TM
    tok = pl.BlockSpec((1, TM, d), lambda b, i: (b, i, 0))
    modspec = pl.BlockSpec((1, 1, 6, d), _seg_map)
    tail = [modspec, pl.BlockSpec((1, d), lambda b, i: (0, 0)), pl.BlockSpec((d, n), lambda b, i: (0, 0))]
    pspec = pl.BlockSpec((1, TM, n), lambda b, i: (b, i, 0))
    pshape = jax.ShapeDtypeStruct((bsz, L, n), F32)
    if prev is None:
        return pl.pallas_call(
            functools.partial(_inproj_kernel, fuse_prev=False),
            grid=(bsz, nt), in_specs=[tok] + tail, out_specs=pspec, out_shape=pshape,
            compiler_params=_cp(("parallel", "parallel")), name="inproj",
        )(x, mod, g.reshape(1, d), w)
    moe_out, pmod = prev
    return pl.pallas_call(
        functools.partial(_inproj_kernel, fuse_prev=True),
        grid=(bsz, nt), in_specs=[tok, tok, modspec] + tail,
        out_specs=[tok, pspec], out_shape=[jax.ShapeDtypeStruct(x.shape, F32), pshape],
        compiler_params=_cp(("parallel", "parallel")), name="inproj_res",
    )(x, moe_out, pmod, mod, g.reshape(1, d), w)


def _conv_tile(ref, i, nt, cw, cb, width=TM):
    L = nt * width
    t0 = pl.multiple_of(i * width, width)
    cur = ref[0, pl.ds(t0, width), :]
    prev = ref[0, pl.ds(pl.multiple_of(jnp.maximum(t0 - 8, 0), 8), 8), :]
    nxt = ref[0, pl.ds(pl.multiple_of(jnp.minimum(t0 + width, L - 8), 8), 8), :]
    return _conv_vals(cur, prev, nxt, i, nt, cw, cb, width, first_lat=TM // width)


def _conv_vals(cur, prev, nxt, i, nt, cw, cb, width, first_lat):
    prev = jnp.where((i != 0) & (i != first_lat), prev, 0.0)
    nxt = jnp.where((i != first_lat - 1) & (i != nt - 1), nxt, 0.0)
    cat = jnp.concatenate([prev, cur, nxt], axis=0)
    return (cw[0:1] * cat[6:6 + width] + cw[1:2] * cat[7:7 + width] + cw[2:3] * cur
            + cw[3:4] * cat[9:9 + width] + cb)


def _lru_kernel(ux_ref, cw_ref, cb_ref, gw_ref, gb_ref, lam_ref, o_ref, *, nt):
    cw = cw_ref[...]
    cb = cb_ref[...]
    lam = lam_ref[0]
    sp = _softplus(-lam)
    row = lax.broadcasted_iota(jnp.int32, (TM, 128), 0) & 7

    def gates(i, d):
        xl = _conv_tile(ux_ref, i, nt, cw, cb)
        g = _dot(xl, gw_ref[0, d]) + gb_ref[0, d]
        r = _sigmoid(g[:, :128])
        ig = _sigmoid(g[:, 128:])
        log_a = -LRU_C * r * sp[d:d + 1]
        a = jnp.exp(log_a)
        u = jnp.sqrt(1.0 - a * a) * (ig * xl)
        return a, u

    def scan_tile(i, d, carry, accumulate):
        a, u = gates(i, d)
        rev = d == 1
        for k in (1, 2, 4):
            sh = TM - k if rev else k
            ok = (row < 8 - k) if rev else (row >= k)
            ash = pltpu.roll(a, sh, 0)
            ush = pltpu.roll(u, sh, 0)
            u = jnp.where(ok, u + a * ush, u)
            a = jnp.where(ok, a * ash, a)
        t0 = i * TM
        groups = range(TM // 8)
        for s in (reversed(groups) if rev else groups):
            h = u[s * 8:(s + 1) * 8] + a[s * 8:(s + 1) * 8] * carry
            carry = h[0:1] if rev else h[7:8]
            idx = pl.ds(pl.multiple_of(t0 + s * 8, 8), 8)
            if accumulate:
                o_ref[0, idx, :] += h
            else:
                o_ref[0, idx, :] = h
        return carry

    zero = jnp.zeros((1, 128), F32)
    lax.fori_loop(0, nt, lambda i, c: scan_tile(i, 0, c, False), zero)
    lax.fori_loop(0, nt, lambda j, c: scan_tile(jnp.where(j == 0, 0, nt - j), 1, c, True), zero)


def _lru(proj, conv_w, conv_b, wa, ba, wx, bx, lam):
    bsz, L, _ = proj.shape
    nt = L // TM
    width = conv_w.shape[1]
    ng = width // 128

    def blockdiag(w):
        w = w.reshape(2, ng, 2, 64, 64)
        z = jnp.zeros_like(w[:, :, 0])
        top = jnp.concatenate([w[:, :, 0], z], axis=-1)
        bot = jnp.concatenate([z, w[:, :, 1]], axis=-1)
        return jnp.concatenate([top, bot], axis=-2)
    gw = jnp.concatenate([blockdiag(wa), blockdiag(wx)], axis=-1).transpose(1, 0, 2, 3).astype(BF16)
    gb = jnp.concatenate([ba.reshape(2, ng, 1, 128), bx.reshape(2, ng, 1, 128)], axis=-1).transpose(1, 0, 2, 3)
    lam_g = lam.reshape(2, ng, 128).transpose(1, 0, 2)
    return pl.pallas_call(
        functools.partial(_lru_kernel, nt=nt),
        grid=(bsz, ng),
        in_specs=[pl.BlockSpec((1, L, 128), lambda b, c: (b, 0, c)),
                  pl.BlockSpec((CONV_W, 128), lambda b, c: (0, c)),
                  pl.BlockSpec((1, 128), lambda b, c: (0, c)),
                  pl.BlockSpec((1, 2, 128, 256), lambda b, c: (c, 0, 0, 0)),
                  pl.BlockSpec((1, 2, 1, 256), lambda b, c: (c, 0, 0, 0)),
                  pl.BlockSpec((1, 2, 128), lambda b, c: (c, 0, 0))],
        out_specs=pl.BlockSpec((1, L, 128), lambda b, c: (b, 0, c)),
        out_shape=jax.ShapeDtypeStruct((bsz, L, width), F32),
        compiler_params=_cp(("parallel", "parallel")), name="rglru",
    )(proj, conv_w, conv_b.reshape(1, width), gw, gb, lam_g)


def _na_bias_table(rpb, rows):
    nh = rpb.shape[0]
    win_r = (rpb.shape[1] + 1) // 2
    win_c = (rpb.shape[2] + 1) // 2
    qr = np.arange(NA_RQ)[:, None, None, None]
    qc = np.arange(GRID_W)[None, :, None, None]
    kr = np.arange(NA_RK)[None, None, :, None]
    kc = np.arange(GRID_W)[None, None, None, :]
    cstart = np.clip(qc - win_c // 2, 0, GRID_W - win_c)
    col_ok = (kc >= cstart) & (kc < cstart + win_c)
    dcol = np.clip(kc - qc + (win_c - 1), 0, 2 * win_c - 2)
    oc = (np.arange(2 * win_c - 1)[:, None, None] == dcol[0, :, 0, :][None]).astype(np.float32)
    o_rows, oks = [], []
    for r0, w0 in ((0, 0), (2 * NA_RQ, NA_RQ), (rows - NA_RQ, rows - NA_RK)):
        r = r0 + qr
        kabs = w0 + kr
        rstart = np.clip(r - win_r // 2, 0, rows - win_r)
        oks.append(np.broadcast_to((kabs >= rstart) & (kabs < rstart + win_r) & col_ok,
                                   (NA_RQ, GRID_W, NA_RK, GRID_W)))
        drow = np.clip(kabs - r + (win_r - 1), 0, 2 * win_r - 2)[:, 0, :, 0]
        o_rows.append((np.arange(2 * win_r - 1)[:, None, None] == drow[None]).astype(np.float32))
    t1 = jnp.einsum('hrc,prab->phabc', rpb.astype(F32), jnp.asarray(np.stack(o_rows)), precision=HI)
    b = jnp.einsum('phabc,cqk->phaqbk', t1, jnp.asarray(oc), precision=HI)
    b = jnp.where(jnp.asarray(np.stack(oks))[:, None], b, NEG)
    return b.reshape(3, nh, NA_RQ * GRID_W, NA_RK * GRID_W)


def _pair_rms(x, lo):
    x2 = x * x
    s0 = jnp.sum(jnp.where(lo, x2, 0.0), axis=-1, keepdims=True)
    s1 = jnp.sum(jnp.where(lo, 0.0, x2), axis=-1, keepdims=True)
    return x * lax.rsqrt(jnp.where(lo, s0, s1) * (2.0 / x.shape[-1]) + EPS)


def _na_kernel(q_ref, k_ref, v_ref, bias_ref, qg_ref, kg_ref, o_ref, kn_s, vb_s, *, rows, hd):
    i = pl.program_id(2)
    L = k_ref.shape[1]
    nkeys = NA_RK * GRID_W
    lo = lax.broadcasted_iota(jnp.int32, (1, 2 * hd), 1) < hd

    @pl.when(i == 0)
    def _prep():
        def body(t, _):
            idx = pl.ds(pl.multiple_of(t * TM, TM), TM)
            kn_s[idx, :] = (_pair_rms(k_ref[0, idx, :], lo) * kg_ref[...]).astype(BF16)
            vb_s[idx, :] = v_ref[0, idx, :].astype(BF16)
            return 0
        lax.fori_loop(0, L // TM, body, 0)

    qn = _pair_rms(q_ref[0], lo) * (qg_ref[...] * hd ** -0.5)
    q_h = [jnp.where(lo, qn, 0.0).astype(BF16), jnp.where(lo, 0.0, qn).astype(BF16)]
    kctx = kn_s[0:TM, :]
    vctx = vb_s[0:TM, :]

    def head(hh, kwin, vwin):
        s_c = _dot_nt(q_h[hh], kctx)
        if kwin is not None:
            s_w = _dot_nt(q_h[hh], kwin) + bias_ref[0, hh]
        yield
        m = jnp.max(s_c, axis=-1, keepdims=True)
        if kwin is not None:
            m = jnp.maximum(m, jnp.max(s_w, axis=-1, keepdims=True))
            p_w = jnp.exp(s_w - m)
        p_c = jnp.exp(s_c - m)
        den = jnp.sum(p_c, axis=-1, keepdims=True)
        num = _dot(p_c, vctx)
        if kwin is not None:
            den = den + jnp.sum(p_w, axis=-1, keepdims=True)
            num = num + _dot(p_w, vwin)
        yield
        return num / den

    def both(kwin, vwin):
        o0, o1 = _interleave([head(0, kwin, vwin), head(1, kwin, vwin)])
        o_ref[0] = jnp.where(lo, o0, o1)

    @pl.when(i == 0)
    def _():
        both(None, None)

    @pl.when(i > 0)
    def _():
        r0 = (i - 1) * NA_RQ
        w0 = jnp.clip(r0 - NA_RQ, 0, rows - NA_RK)
        start = pl.multiple_of(TM + w0 * GRID_W, GRID_W)
        both(kn_s[pl.ds(start, nkeys), :], vb_s[pl.ds(start, nkeys), :])


def _na(proj, q_g, k_g, rpb, col0):
    bsz, L, _ = proj.shape
    nh = rpb.shape[0]
    hd = q_g.shape[0]
    width = nh * hd
    rows = (L - TM) // GRID_W
    nb = L // TM
    bias = _na_bias_table(rpb, rows)
    qb, kb, vb = col0 // 128, (col0 + width) // 128, (col0 + 2 * width) // 128

    def pat(i):
        return jnp.where(i <= 1, 0, jnp.where(i == nb - 1, 2, 1))
    return pl.pallas_call(
        functools.partial(_na_kernel, rows=rows, hd=hd),
        grid=(bsz, width // 128, nb),
        in_specs=[pl.BlockSpec((1, TM, 128), lambda b, h, i: (b, i, qb + h)),
                  pl.BlockSpec((1, L, 128), lambda b, h, i: (b, 0, kb + h)),
                  pl.BlockSpec((1, L, 128), lambda b, h, i: (b, 0, vb + h)),
                  pl.BlockSpec((1, 2, TM, NA_RK * GRID_W), lambda b, h, i: (pat(i), h, 0, 0)),
                  pl.BlockSpec((1, 2 * hd), lambda b, h, i: (0, 0)),
                  pl.BlockSpec((1, 2 * hd), lambda b, h, i: (0, 0))],
        out_specs=pl.BlockSpec((1, TM, 128), lambda b, h, i: (b, i, h)),
        out_shape=jax.ShapeDtypeStruct((bsz, L, width), F32),
        scratch_shapes=[pltpu.VMEM((L, 128), BF16), pltpu.VMEM((L, 128), BF16)],
        compiler_params=_cp(("parallel", "parallel", "arbitrary")), name="nbr_attn",
    )(proj, proj, proj, bias, jnp.tile(q_g, 2).reshape(1, 2 * hd), jnp.tile(k_g, 2).reshape(1, 2 * hd))


def _route(lg):
    lane = lax.broadcasted_iota(jnp.int32, lg.shape, 1)
    lane_f = lane.astype(F32)
    is_g = lane < EXPERT_LANE0
    gl = jnp.where(is_g, lg, NEG)
    gmax = jnp.max(gl, axis=-1, keepdims=True)
    gsel = jnp.min(jnp.where(is_g & (gl == gmax), lane_f, 1e9), axis=-1, keepdims=True)
    g_w = 1.0 / jnp.sum(jnp.where(is_g, jnp.exp(gl - gmax), 0.0), axis=-1, keepdims=True)
    grp = ((lane - EXPERT_LANE0) >> 2).astype(F32)
    in_g = (lane >= EXPERT_LANE0) & (lane < EXPERT_LANE0 + N_EXPERTS) & (grp == gsel)
    el = jnp.where(in_g, lg, NEG)
    v1 = jnp.max(el, axis=-1, keepdims=True)
    i1 = jnp.min(jnp.where(in_g & (el == v1), lane_f, 1e9), axis=-1, keepdims=True)
    el2 = jnp.where(lane_f == i1, NEG, el)
    v2 = jnp.max(el2, axis=-1, keepdims=True)
    i2 = jnp.min(jnp.where(in_g & (lane_f != i1) & (el2 == v2), lane_f, 1e9), axis=-1, keepdims=True)
    t = jnp.exp(v2 - v1)
    w1 = g_w / (1.0 + t)
    w2 = g_w * t / (1.0 + t)
    return (jnp.where(lane_f == i1, w1, 0.0) + jnp.where(lane_f == i2, w2, 0.0)
            + jnp.where(lane_f == gsel, 1.0, 0.0))


def _outproj_kernel(*refs, even):
    if even:
        (r_ref, ug_ref, a_ref, x_ref, w_ref, mod_ref, g2_ref, rw_ref, x1_ref, h2_ref, gate_ref) = refs
    else:
        (ys_ref, z_ref, hm_ref, mo_ref, sg_ref, mg_ref, x_ref, w_ref, mod_ref, g2_ref, rw_ref,
         x1_ref, h2_ref, gate_ref) = refs
    mod = mod_ref[0, 0]

    def part(rs):
        if even:
            y_in = jnp.concatenate([r_ref[0, rs, :] * _gelu_tanh(ug_ref[0, rs, :]), a_ref[0, rs, :]], axis=-1)
        else:
            ys = ys_ref[0, rs, :] * _silu(z_ref[0, rs, :])
            sg = sg_ref[...]
            mg = mg_ref[...]
            hm = hm_ref[0, rs, :]
            sig_o = _sigmoid(mo_ref[0, rs, :])
            gw = ys.shape[-1] // 2
            parts = [_rms(ys[:, g * gw:(g + 1) * gw]) * sg[:, g * gw:(g + 1) * gw] for g in range(2)]
            hw = 128
            parts += [_rms(hm[:, h * hw:(h + 1) * hw]) * mg[:, h * hw:(h + 1) * hw] * sig_o[:, h * hw:(h + 1) * hw]
                      for h in range(hm.shape[-1] // hw)]
            y_in = jnp.concatenate(parts, axis=-1)
        y = _dot(y_in, w_ref[...])
        yield
        x1 = x_ref[0, rs, :] + mod[2:3, :] * y
        x1_ref[0, rs, :] = x1
        h2 = _rms(x1) * g2_ref[...] * (1.0 + mod[4:5, :]) + mod[3:4, :]
        h2_ref[0, rs, :] = h2.astype(BF16)
        hi = h2.astype(BF16)
        lo = (h2 - hi.astype(F32)).astype(BF16)
        lg2 = jnp.dot(hi, rw_ref[...], preferred_element_type=F32)
        lg1 = jnp.dot(lo, rw_ref[:, 0:128], preferred_element_type=F32)
        yield
        gate_ref[0, rs, :] = _route(lg2[:, 0:128] + lg2[:, 128:256] + lg1)

    nparts = 2
    rows = TM // nparts
    _interleave([part(pl.ds(p * rows, rows)) for p in range(nparts)])


def _outproj(mix_inputs, x, w, mod, g2, rw, even, tile0, ntiles):
    bsz, _, d = x.shape
    specs, args = [], []
    for arr, cb, wdt in mix_inputs:
        if arr.ndim == 3:
            specs.append(pl.BlockSpec((1, TM, wdt), lambda b, i, cb=cb: (b, i + tile0, cb)))
        else:
            specs.append(pl.BlockSpec((1, wdt), lambda b, i: (0, 0)))
        args.append(arr)
    specs += [pl.BlockSpec((1, TM, d), lambda b, i: (b, i + tile0, 0)),
              pl.BlockSpec(w.shape, lambda b, i: (0, 0)),
              pl.BlockSpec((1, 1, 6, d), lambda b, i: (b, jnp.minimum(i + tile0, 1), 0, 0)),
              pl.BlockSpec((1, d), lambda b, i: (0, 0)),
              pl.BlockSpec(rw.shape, lambda b, i: (0, 0))]
    args += [x, w, mod, g2.reshape(1, d), rw]
    lo = ntiles * TM
    return pl.pallas_call(
        functools.partial(_outproj_kernel, even=even),
        grid=(bsz, ntiles), in_specs=specs,
        out_specs=[pl.BlockSpec((1, TM, d), lambda b, i: (b, i, 0)),
                   pl.BlockSpec((1, TM, d), lambda b, i: (b, i, 0)),
                   pl.BlockSpec((1, TM, 128), lambda b, i: (b, i, 0))],
        out_shape=[jax.ShapeDtypeStruct((bsz, lo, d), F32), jax.ShapeDtypeStruct((bsz, lo, d), BF16),
                   jax.ShapeDtypeStruct((bsz, lo, 128), F32)],
        compiler_params=_cp(("parallel", "parallel")), name="outproj_even" if even else "outproj_odd",
    )(*args)


MOE_CH = 128
GROUP_SIZE = 4
N_GROUPS = N_EXPERTS // GROUP_SIZE


def _moe_kernel(*refs, residual):
    if residual:
        x_ref, g_ref, tri_ref, w1_ref, w3_ref, w2_ref, x1_ref, mod_ref, o_ref, xs_s, ys_s, gs_s, pt_s, plan_s = refs
    else:
        x_ref, g_ref, tri_ref, w1_ref, w3_ref, w2_ref, o_ref, xs_s, ys_s, gs_s, pt_s, plan_s = refs
    e = pl.program_id(1)
    tb = x_ref.shape[0]
    nch = xs_s.shape[0] // MOE_CH

    @pl.when(e == 0)
    def _plan():
        g = g_ref[...]
        lane = lax.broadcasted_iota(jnp.int32, g.shape, 1)
        oh = jnp.where(lane < N_GROUPS, g, 0.0)
        rank = jnp.dot(tri_ref[...], oh.astype(BF16), preferred_element_type=F32)
        cnt = jnp.sum(oh, axis=0, keepdims=True)
        lane1 = lax.broadcasted_iota(jnp.int32, (1, 128), 1)
        off = jnp.int32(0)
        offv = jnp.zeros((1, 128), F32)
        for gi in range(N_GROUPS):
            n = jnp.sum(jnp.where(lane1 == gi, cnt, 0.0)).astype(jnp.int32)
            nchunks = (n + (MOE_CH - 1)) // MOE_CH
            plan_s[gi] = off // MOE_CH
            plan_s[N_GROUPS + gi] = nchunks
            offv = offv + jnp.where(lane1 == gi, off.astype(F32), 0.0)
            off = off + nchunks * MOE_CH
        pos_col = jnp.sum(oh * (rank + offv), axis=1, keepdims=True)
        posb = jnp.broadcast_to(pos_col, (tb, 128))
        pos_row = jnp.concatenate([posb[i * 128:(i + 1) * 128, :].T[0:1, :] for i in range(tb // 128)], axis=1)
        x = x_ref[...]
        g_hi = g.astype(BF16)
        g_lo = (g - g_hi.astype(F32)).astype(BF16)
        lane_c = lax.broadcasted_iota(jnp.int32, (tb, MOE_CH), 1).astype(F32)
        row_c = lax.broadcasted_iota(jnp.int32, (MOE_CH, tb), 0).astype(F32)
        for c in range(nch):
            sl = slice(c * MOE_CH, (c + 1) * MOE_CH)
            pt_s[:, sl] = jnp.where(pos_col == lane_c + float(c * MOE_CH), 1.0, 0.0).astype(BF16)
            p = jnp.where(row_c + float(c * MOE_CH) == pos_row, 1.0, 0.0).astype(BF16)
            xs_s[sl, :] = jnp.dot(p, x, preferred_element_type=F32).astype(BF16)
            gs_s[sl, :] = (jnp.dot(p, g_hi, preferred_element_type=F32)
                           + jnp.dot(p, g_lo, preferred_element_type=F32))
        ys_s[...] = jnp.zeros_like(ys_s)

    grp = e // GROUP_SIZE
    c0 = plan_s[grp]
    nchunks = plan_s[N_GROUPS + grp]

    def ffn(chunk0, nrows):
        rows = pl.ds(pl.multiple_of(chunk0 * MOE_CH, MOE_CH), nrows)
        xs = xs_s[rows, :]
        a = jnp.dot(xs, w1_ref[0], preferred_element_type=F32)
        b = jnp.dot(xs, w3_ref[0], preferred_element_type=F32)
        y = _dot(_silu(a) * b, w2_ref[0])
        lane_g = lax.broadcasted_iota(jnp.int32, (nrows, 128), 1)
        ge = jnp.sum(jnp.where(lane_g == e + EXPERT_LANE0, gs_s[rows, :], 0.0), axis=-1, keepdims=True)
        ys_s[rows, :] += ge * y

    def pair(p, _):
        ffn(c0 + 2 * p, 2 * MOE_CH)
        return 0
    lax.fori_loop(0, nchunks // 2, pair, 0)

    @pl.when(nchunks % 2 == 1)
    def _():
        ffn(c0 + nchunks - 1, MOE_CH)

    @pl.when(e == N_EXPERTS - 1)
    def _combine():
        out = jnp.dot(pt_s[...], ys_s[...].astype(BF16), preferred_element_type=F32)
        if residual:
            out = x1_ref[...] + mod_ref[0, 0][5:6, :] * out
        o_ref[...] = out


def _moe(h2, gate, w1, w3, w2, resid=None):
    t, d = h2.shape
    ne, _, ff = w1.shape
    tb = math.gcd(t, 1024)
    npad = tb + N_GROUPS * MOE_CH
    tri = jnp.asarray(np.tril(np.ones((tb, tb), np.float32), -1), BF16)
    extra_specs, extra_args = [], []
    if resid is not None:
        x1, mod = resid
        per_batch = t // mod.shape[0] // tb
        extra_specs = [pl.BlockSpec((tb, d), lambda i, e: (i, 0)),
                       pl.BlockSpec((1, 1, 6, d), lambda i, e: (i // per_batch, 1, 0, 0))]
        extra_args = [x1, mod]
    return pl.pallas_call(
        functools.partial(_moe_kernel, residual=resid is not None),
        grid=(t // tb, ne),
        in_specs=[pl.BlockSpec((tb, d), lambda i, e: (i, 0)),
                  pl.BlockSpec((tb, 128), lambda i, e: (i, 0)),
                  pl.BlockSpec((tb, tb), lambda i, e: (0, 0)),
                  pl.BlockSpec((1, d, ff), lambda i, e: (e, 0, 0)),
                  pl.BlockSpec((1, d, ff), lambda i, e: (e, 0, 0)),
                  pl.BlockSpec((1, ff, d), lambda i, e: (e, 0, 0))] + extra_specs,
        out_specs=pl.BlockSpec((tb, d), lambda i, e: (i, 0)),
        out_shape=jax.ShapeDtypeStruct((t, d), F32),
        scratch_shapes=[pltpu.VMEM((npad, d), BF16), pltpu.VMEM((npad, d), F32), pltpu.VMEM((npad, 128), F32),
                        pltpu.VMEM((tb, npad), BF16), pltpu.SMEM((2 * N_GROUPS,), jnp.int32)],
        compiler_params=_cp(("parallel", "arbitrary")), name="moe",
    )(h2, gate, tri, w1, w3, w2, *extra_args)


def _tri(q, rev):
    r = lax.broadcasted_iota(jnp.int32, (q, q), 0)
    c = lax.broadcasted_iota(jnp.int32, (q, q), 1)
    return (c >= r) if rev else (c <= r)


def _ssd_kernel(xs_ref, b_ref, c_ref, dt_ref, cwx_ref, cbx_ref, cwb_ref, cbb_ref, cwc_ref, cbc_ref,
                dtb_ref, alog_ref, dsk_ref, sel_ref, o_ref, xc_s, bc_s, cc_s, dt_s, bct_s, *, nc):
    q = CHUNK
    first_lat = TM // q

    dtb = dtb_ref[...]

    def prep(i, _):
        t0 = pl.multiple_of(i * q, q)
        idx = pl.ds(t0, q)
        for src, cw, cb, dst in ((xs_ref, cwx_ref, cbx_ref, xc_s), (b_ref, cwb_ref, cbb_ref, bc_s),
                                 (c_ref, cwc_ref, cbc_ref, cc_s)):
            L = nc * q
            cur = src[0, idx, :]
            prev = src[0, pl.ds(pl.multiple_of(jnp.maximum(t0 - 8, 0), 8), 8), :]
            nxt = src[0, pl.ds(pl.multiple_of(jnp.minimum(t0 + q, L - 8), 8), 8), :]
            dst[idx, :] = _silu(_conv_vals(cur, prev, nxt, i, nc, cw[...], cb[...], q, first_lat))
        dt_s[idx, :] = _softplus(dt_ref[0, idx, :] + dtb)
        o_ref[0, idx, :] = dsk_ref[...] * xc_s[idx, :]
        bct_s[i] = bc_s[idx, :].T.astype(BF16)
        return 0
    lax.fori_loop(0, nc, prep, 0)

    a_lane = jnp.broadcast_to(-jnp.exp(alog_ref[...]), (8, 128))
    a_e = [_dot_hi(a_lane, sel_ref[d, 0])[0:1, :] for d in (0, 1)]
    lane256 = lax.broadcasted_iota(jnp.int32, (q, 256), 1) // 64
    r8 = lax.broadcasted_iota(jnp.int32, (8, 256), 0)
    l8 = lax.broadcasted_iota(jnp.int32, (8, 256), 1)
    head_rows = (l8 == r8 * 64).astype(F32)

    def chain(d, j, h):
        rev = d == 1
        causal = _tri(q, rev)
        tri = causal.astype(F32)
        sel = sel_ref[d, 0]
        last = 0 if rev else q - 1
        if rev:
            ci = jnp.where(j < first_lat, first_lat - 1 - j, nc - 1 - (j - first_lat))
        else:
            ci = j
        idx = pl.ds(pl.multiple_of(ci * q, q), q)
        dt_e = _dot_sel_r(dt_s[idx, :], sel)
        xc = xc_s[idx, :]
        bc = bc_s[idx, :]
        cc = cc_s[idx, :]
        cb = _dot_nt(cc, bc)
        y_off = _dot(cc, h)
        yield
        cum_e = _dot_sel(tri, dt_e * a_e[d])
        dtx = xc * dt_e
        yield
        cum_rows = _dot_sel(head_rows, cum_e, _NT)
        tot_e = cum_e[last:last + 1, :]
        states = _dot(bct_s[ci], dtx * jnp.exp(tot_e - cum_e))
        yield
        y = jnp.exp(cum_e) * y_off
        for r in range(4):
            seg = cum_e[:, r * 64:r * 64 + 1] - cum_rows[r:r + 1, :]
            m = cb * jnp.exp(jnp.where(causal, seg, NEG))
            y = y + _dot(m, jnp.where(lane256 == r, dtx, 0.0))
            yield
        o_ref[0, idx, :] += y
        return jnp.exp(tot_e) * h + states

    def chunk(j, hs):
        return tuple(_interleave([chain(0, j, hs[0]), chain(1, j, hs[1])]))
    h0 = jnp.zeros((128, 256), F32)
    lax.fori_loop(0, nc, chunk, (h0, h0))


def _ssd(proj, conv_w, conv_b, dt_bias, a_log, d_skip, cols):
    bsz, L, _ = proj.shape
    nc = L // CHUNK
    cx, cbm, ccm, cdt = cols
    nh = d_skip.shape[0]
    dtb = jnp.zeros((1, 128), F32).at[0, :2 * nh].set(dt_bias.reshape(-1))
    alog = jnp.zeros((1, 128), F32).at[0, :2 * nh].set(a_log.reshape(-1))
    dsk = jnp.repeat(d_skip.astype(F32), 64).reshape(1, nh * 64)
    sel = np.zeros((2, 2, 128, 256), np.float32)
    for d in range(2):
        for g in range(2):
            for r in range(4):
                sel[d, g, d * nh + g * 4 + r, r * 64:(r + 1) * 64] = 1.0
    cw_specs = []
    for width, off in ((256, 0), (128, 512), (128, 768)):
        cw_specs += [pl.BlockSpec((CONV_W, width), lambda b, g, off=off, width=width: (0, off // width + g)),
                     pl.BlockSpec((1, width), lambda b, g, off=off, width=width: (0, off // width + g))]
    return pl.pallas_call(
        functools.partial(_ssd_kernel, nc=nc),
        grid=(bsz, 2),
        in_specs=[pl.BlockSpec((1, L, 256), lambda b, g: (b, 0, cx // 256 + g)),
                  pl.BlockSpec((1, L, 128), lambda b, g: (b, 0, cbm // 128 + g)),
                  pl.BlockSpec((1, L, 128), lambda b, g: (b, 0, ccm // 128 + g)),
                  pl.BlockSpec((1, L, 128), lambda b, g: (b, 0, cdt // 128))] + cw_specs + [
                  pl.BlockSpec((1, 128), lambda b, g: (0, 0)),
                  pl.BlockSpec((1, 128), lambda b, g: (0, 0)),
                  pl.BlockSpec((1, 256), lambda b, g: (0, g)),
                  pl.BlockSpec((2, 1, 128, 256), lambda b, g: (0, g, 0, 0))],
        out_specs=pl.BlockSpec((1, L, 256), lambda b, g: (b, 0, g)),
        out_shape=jax.ShapeDtypeStruct((bsz, L, nh * 64), F32),
        scratch_shapes=[pltpu.VMEM((L, 256), F32), pltpu.VMEM((L, 128), F32), pltpu.VMEM((L, 128), F32),
                        pltpu.VMEM((L, 128), F32), pltpu.VMEM((nc, 128, CHUNK), BF16)],
        compiler_params=_cp(("parallel", "parallel")), name="ssd",
    )(proj, proj, proj, proj, conv_w, conv_b.reshape(1, -1), conv_w, conv_b.reshape(1, -1),
      conv_w, conv_b.reshape(1, -1), dtb, alog, dsk, jnp.asarray(sel))


ML_HP = 2


def _ml_gate_lanes(hh, d):
    li = (hh * 2 + d) * 2
    return li, li + 1


def _ml_gate_columns(nh):
    cols = []
    for hp in range(nh // ML_HP):
        blk = [None] * (4 * ML_HP)
        for hh in range(ML_HP):
            for d in range(2):
                for t, lane in enumerate(_ml_gate_lanes(hh, d)):
                    blk[lane] = (d * 2 + t) * nh + hp * ML_HP + hh
        cols.append(blk)
    return cols


def _mlstm_kernel(q_ref, k_ref, v_ref, g_ref, cwq_ref, cbq_ref, cwk_ref, cbk_ref, gb_ref, o_ref,
                  qc_s, kc_s, kct_s, *, nc, dh):
    q = CHUNK
    first_lat = TM // q

    def prep(i, _):
        t0 = pl.multiple_of(i * q, q)
        idx = pl.ds(t0, q)
        L = nc * q
        for src, cw, cb, dst, mul in ((q_ref, cwq_ref, cbq_ref, qc_s, 1.0), (k_ref, cwk_ref, cbk_ref, kc_s, dh ** -0.5)):
            cur = src[0, idx, :]
            prev = src[0, pl.ds(pl.multiple_of(jnp.maximum(t0 - 8, 0), 8), 8), :]
            nxt = src[0, pl.ds(pl.multiple_of(jnp.minimum(t0 + q, L - 8), 8), 8), :]
            dst[idx, :] = _silu(_conv_vals(cur, prev, nxt, i, nc, cw[...], cb[...], q, first_lat)) * mul
        for hh in range(ML_HP):
            kct_s[i, hh * dh:(hh + 1) * dh, :] = kc_s[idx, hh * dh:(hh + 1) * dh].T.astype(BF16)
        return 0
    lax.fori_loop(0, nc, prep, 0)

    gb = gb_ref[0]
    lane = lax.broadcasted_iota(jnp.int32, (q, 128), 1)
    eye8 = (lax.broadcasted_iota(jnp.int32, (8, 128), 0) == lax.broadcasted_iota(jnp.int32, (8, 128), 1)).astype(F32)
    o_ref[...] = jnp.zeros_like(o_ref)
    heads = range(ML_HP)

    def direction(d, j, states):
        rev = d == 1
        causal = _tri(q, rev)
        last = 0 if rev else q - 1
        if rev:
            ci = jnp.where(j < first_lat, first_lat - 1 - j, nc - 1 - (j - first_lat))
        else:
            ci = j
        idx = pl.ds(pl.multiple_of(ci * q, q), q)
        g = g_ref[0, idx, :] + gb
        logf = jnp.minimum(g, 0.0) - jnp.log(1.0 + jnp.exp(-jnp.abs(g)))
        is_f = functools.reduce(jnp.logical_or, [lane == _ml_gate_lanes(hh, d)[1] for hh in heads])
        cum = _dot_sel(jnp.where(causal, 1.0, 0.0), jnp.where(is_f, logf, 0.0))
        qc = [qc_s[idx, hh * dh:(hh + 1) * dh] for hh in heads]
        kc = [kc_s[idx, hh * dh:(hh + 1) * dh] for hh in heads]
        vc = [v_ref[0, idx, hh * dh:(hh + 1) * dh] for hh in heads]
        kct = [kct_s[ci, hh * dh:(hh + 1) * dh, :] for hh in heads]
        qk = [_dot(qc[hh], kct[hh]) for hh in heads]
        inter_c = [_dot(qc[hh], states[hh][0]) for hh in heads]
        yield
        rows = _dot_sel(eye8, jnp.where(is_f, cum, g), _NT)
        yield
        w, w_s, g_in, m_t, keep, m_new, wv, upd = [], [], [], [], [], [], [], []
        for hh in heads:
            li, lf = _ml_gate_lanes(hh, d)
            m_prev = states[hh][2]
            i_col, i_row = g[:, li:li + 1], rows[li:li + 1, :]
            b_col, b_row = cum[:, lf:lf + 1], rows[lf:lf + 1, :]
            dlog = jnp.where(causal, b_col - b_row + i_row, NEG)
            inter = b_col + m_prev
            m_t.append(jnp.maximum(jnp.max(dlog, axis=-1, keepdims=True), inter))
            w.append(jnp.exp(dlog - m_t[hh]) * qk[hh])
            g_in.append(jnp.exp(inter - m_t[hh]))
            b_end = b_col[last:last + 1, :]
            g_s = b_end - b_col + i_col
            m_new.append(jnp.maximum(jnp.max(g_s, axis=0, keepdims=True), b_end + m_prev))
            w_s.append(jnp.exp(g_s - m_new[hh]))
            keep.append(jnp.exp(b_end + m_prev - m_new[hh]))
            wv.append(_dot(w[hh], vc[hh]))
            upd.append(_dot(kct[hh], w_s[hh] * vc[hh]))
        yield
        new_states = []
        for hh in heads:
            c_prev, n_prev, _ = states[hh]
            num = wv[hh] + g_in[hh] * inter_c[hh]
            den = (jnp.sum(w[hh], axis=-1, keepdims=True)
                   + g_in[hh] * jnp.sum(qc[hh] * n_prev, axis=-1, keepdims=True))
            o_ref[0, idx, hh * dh:(hh + 1) * dh] += num / jnp.maximum(jnp.abs(den), jnp.exp(-m_t[hh]))
            new_states.append((keep[hh] * c_prev + upd[hh],
                               keep[hh] * n_prev + jnp.sum(w_s[hh] * kc[hh], axis=0, keepdims=True), m_new[hh]))
        return tuple(new_states)

    def chunk(j, states):
        return tuple(_interleave([direction(d, j, states[d]) for d in (0, 1)]))
    init = (jnp.zeros((dh, dh), F32), jnp.zeros((1, dh), F32), jnp.full((1, 1), NEG, F32))
    lax.fori_loop(0, nc, chunk, tuple(tuple(init for _ in heads) for _ in (0, 1)))


def _mlstm(proj, conv_w, conv_b, gate_b, cols):
    bsz, L, _ = proj.shape
    nc = L // CHUNK
    cq, ck, cv, cg = cols
    nh = gate_b.shape[-1]
    dh = conv_w.shape[1] // (2 * nh)
    bw = ML_HP * dh
    gcols = np.asarray(_ml_gate_columns(nh))
    gbl = jnp.zeros((nh // ML_HP, 1, 128), F32).at[:, 0, :gcols.shape[1]].set(gate_b.reshape(-1)[gcols])
    return pl.pallas_call(
        functools.partial(_mlstm_kernel, nc=nc, dh=dh),
        grid=(bsz, nh // ML_HP),
        in_specs=[pl.BlockSpec((1, L, bw), lambda b, h: (b, 0, cq // bw + h)),
                  pl.BlockSpec((1, L, bw), lambda b, h: (b, 0, ck // bw + h)),
                  pl.BlockSpec((1, L, bw), lambda b, h: (b, 0, cv // bw + h)),
                  pl.BlockSpec((1, L, 128), lambda b, h: (b, 0, cg // 128 + h)),
                  pl.BlockSpec((CONV_W, bw), lambda b, h: (0, h)),
                  pl.BlockSpec((1, bw), lambda b, h: (0, h)),
                  pl.BlockSpec((CONV_W, bw), lambda b, h: (0, nh // ML_HP + h)),
                  pl.BlockSpec((1, bw), lambda b, h: (0, nh // ML_HP + h)),
                  pl.BlockSpec((1, 1, 128), lambda b, h: (h, 0, 0))],
        out_specs=pl.BlockSpec((1, L, bw), lambda b, h: (b, 0, h)),
        out_shape=jax.ShapeDtypeStruct((bsz, L, nh * dh), F32),
        scratch_shapes=[pltpu.VMEM((L, bw), F32), pltpu.VMEM((L, bw), F32), pltpu.VMEM((nc, bw, CHUNK), BF16)],
        compiler_params=_cp(("parallel", "parallel")), name="mlstm",
    )(proj, proj, proj, proj, conv_w, conv_b.reshape(1, -1), conv_w, conv_b.reshape(1, -1), gbl)


def kernel(x, c, ctx, c_ctx, mod_w, mod_b, norm1_g, norm2_g, even_w_in, even_w_out, lru_conv_w, lru_conv_b, lru_wa, lru_ba, lru_wx, lru_bx, lru_lam, na_q_g, na_k_g, na_rpb, odd_w_in, odd_w_out, ssd_conv_w, ssd_conv_b, ssd_dt_bias, ssd_a_log, ssd_d, ssd_norm_g, ml_conv_w, ml_conv_b, ml_gate_b, ml_norm_g, moe_router_g, moe_router_e, moe_w1, moe_w3, moe_w2):
    bsz, S, d = x.shape
    lc = ctx.shape[1]
    assert lc == TM and S % TM == 0 and mod_w.shape[0] == 2
    nt = (lc + S) // TM
    L = lc + S

    cc = jnp.zeros((8, d), F32).at[0].set(c_ctx).at[1:1 + bsz].set(c)
    mod_all = _modulation(cc, mod_w, mod_b)

    def mod_for(l):
        m = mod_all[l].reshape(8, 6, d)
        return jnp.stack([jnp.broadcast_to(m[0], (bsz, 6, d)), m[1:1 + bsz]], axis=1)

    def router_w(l):
        w = jnp.zeros((d, 128), F32).at[:, :EXPERT_LANE0].set(moe_router_g[l]) \
            .at[:, EXPERT_LANE0:EXPERT_LANE0 + N_EXPERTS].set(moe_router_e[l])
        hi = w.astype(BF16)
        return jnp.concatenate([hi, (w - hi.astype(F32)).astype(BF16)], axis=1)

    xx = jnp.concatenate([ctx, x], axis=1)

    mod0 = mod_for(0)
    proj = _inproj(xx, mod0, norm1_g[0], even_w_in[0].astype(BF16))
    lw = lru_conv_w.shape[-1]
    r = _lru(proj, lru_conv_w[0], lru_conv_b[0], lru_wa[0], lru_ba[0], lru_wx[0], lru_bx[0], lru_lam[0])
    a = _na(proj, na_q_g[0], na_k_g[0], na_rpb[0], col0=2 * lw)
    x1, h2, gate = _outproj([(r, 0, lw), (proj, 1, lw), (a, 0, lw)], xx, even_w_out[0].astype(BF16), mod0,
                            norm2_g[0], router_w(0), even=True, tile0=0, ntiles=nt)
    moe0 = _moe(h2.reshape(bsz * L, d), gate.reshape(bsz * L, 128), moe_w1[0].astype(BF16),
                      moe_w3[0].astype(BF16), moe_w2[0].astype(BF16)).reshape(bsz, L, d)

    mod1 = mod_for(1)
    sw = ssd_d.shape[-1] * 64
    xbc = ssd_conv_w.shape[-1]
    mw = ml_norm_g.shape[-1]
    w = odd_w_in[0]
    o = np.cumsum([0, sw, xbc, 2 * ssd_d.shape[-1], mw, mw, mw, mw])
    ndt = 2 * ssd_d.shape[-1]
    dt_blk = jnp.concatenate([w[:, o[2]:o[3]], jnp.zeros((d, 128 - ndt), F32)], axis=1)
    gate_blks = [jnp.concatenate([w[:, o[7] + np.asarray(blk)], jnp.zeros((d, 128 - len(blk)), F32)], axis=1)
                 for blk in _ml_gate_columns(ml_gate_b.shape[-1])]
    w_odd = jnp.concatenate([w[:, :o[2]], w[:, o[3]:o[7]], dt_blk] + gate_blks, axis=1).astype(BF16)
    cz, cxs = 0, sw
    cB, cC = cxs + sw, cxs + sw + (xbc - sw) // 2
    cq = sw + xbc
    ck, cv, co, csm = cq + mw, cq + 2 * mw, cq + 3 * mw, cq + 4 * mw
    x0, proj1 = _inproj(x1, mod1, norm1_g[1], w_odd, prev=(moe0, mod0))
    ys = _ssd(proj1, ssd_conv_w[0], ssd_conv_b[0], ssd_dt_bias[0], ssd_a_log[0], ssd_d[0], (cxs, cB, cC, csm))
    hm = _mlstm(proj1, ml_conv_w[0], ml_conv_b[0], ml_gate_b[0], (cq, ck, cv, csm + 128))
    x2, h2b, gate1 = _outproj([(ys, 0, sw), (proj1, cz // sw, sw), (hm, 0, mw), (proj1, co // mw, mw),
                               (ssd_norm_g[0].reshape(1, sw), 0, sw), (ml_norm_g[0].reshape(1, mw), 0, mw)],
                              x0, odd_w_out[0].astype(BF16), mod1, norm2_g[1], router_w(1),
                              even=False, tile0=1, ntiles=nt - 1)
    out = _moe(h2b.reshape(bsz * S, d), gate1.reshape(bsz * S, 128), moe_w1[1].astype(BF16),
               moe_w3[1].astype(BF16), moe_w2[1].astype(BF16), resid=(x2.reshape(bsz * S, d), mod1))
    return out.reshape(bsz, S, d)
```

```python
import functools
import math

import jax
import jax.numpy as jnp
import numpy as np
from jax import lax
from jax.experimental import pallas as pl
from jax.experimental.pallas import tpu as pltpu

F32 = jnp.float32
BF16 = jnp.bfloat16
HI = lax.Precision.HIGHEST

EPS = 1e-6
NEG = -1e30
GRID_W = 64
CONV_W = 4
LRU_C = 8.0
TM = 256
CHUNK = 128
NA_RQ = 4
NA_RK = 12
N_EXPERTS = 16
EXPERT_LANE0 = 4
VMEM_LIMIT = 56 * 1024 * 1024


def _cp(sem, vmem=VMEM_LIMIT):
    return pltpu.CompilerParams(dimension_semantics=sem, vmem_limit_bytes=vmem)


def _sigmoid(x):
    return jax.nn.sigmoid(x)


def _silu(x):
    return x * jax.nn.sigmoid(x)


def _softplus(x):
    return jnp.maximum(x, 0.0) + jnp.log(1.0 + jnp.exp(-jnp.abs(x)))


def _gelu_tanh(x):
    return 0.5 * x * (1.0 + jnp.tanh(math.sqrt(2.0 / math.pi) * (x + 0.044715 * (x * x * x))))


def _rms(x, axis=-1):
    return x * lax.rsqrt(jnp.mean(x * x, axis=axis, keepdims=True) + EPS)


def _dot(a, b):
    return jnp.dot(a.astype(BF16), b.astype(BF16), preferred_element_type=F32)


def _dot_hi(a, b):
    return jnp.dot(a, b, precision=HI, preferred_element_type=F32)


def _split3(x):
    x1 = x.astype(BF16)
    r = x - x1.astype(F32)
    x2 = r.astype(BF16)
    x3 = (r - x2.astype(F32)).astype(BF16)
    return x1, x2, x3


def _dot_sel(sel, x, dims=(((1,), (0,)), ((), ()))):
    s = sel.astype(BF16)
    x1, x2, x3 = _split3(x)
    d = lambda xi: lax.dot_general(s, xi, dims, preferred_element_type=F32)
    return (d(x3) + d(x2)) + d(x1)


def _dot_sel_r(x, sel):
    s = sel.astype(BF16)
    x1, x2, x3 = _split3(x)
    d = lambda xi: jnp.dot(xi, s, preferred_element_type=F32)
    return (d(x3) + d(x2)) + d(x1)


_NT = (((1,), (1,)), ((), ()))


def _interleave(gens):
    results = [None] * len(gens)
    live = list(range(len(gens)))
    while live:
        for i in list(live):
            try:
                next(gens[i])
            except StopIteration as stop:
                results[i] = stop.value
                live.remove(i)
    return results


def _dot_nt(a, b):
    return lax.dot_general(a.astype(BF16), b.astype(BF16), (((1,), (1,)), ((), ())),
                           preferred_element_type=F32)


def _mod_kernel(c_ref, w_ref, b_ref, o_ref):
    c = c_ref[...]
    o_ref[0] = _dot_hi(_silu(c), w_ref[0]) + b_ref[0]


def _modulation(cc, mod_w, mod_b):
    depth, d, n = mod_w.shape
    tn = 1536
    return pl.pallas_call(
        _mod_kernel,
        grid=(depth, n // tn),
        in_specs=[pl.BlockSpec((8, d), lambda l, j: (0, 0)),
                  pl.BlockSpec((1, d, tn), lambda l, j: (l, 0, j)),
                  pl.BlockSpec((1, 1, tn), lambda l, j: (l, 0, j))],
        out_specs=pl.BlockSpec((1, 8, tn), lambda l, j: (l, 0, j)),
        out_shape=jax.ShapeDtypeStruct((depth, 8, n), F32),
        compiler_params=_cp(("arbitrary", "arbitrary")),
        name="adaln_mod",
    )(cc, mod_w, mod_b.reshape(depth, 1, n))


def _inproj_kernel(*refs, fuse_prev):
    if fuse_prev:
        x_ref, mo_ref, pmod_ref, mod_ref, g_ref, w_ref, xo_ref, p_ref = refs
        x = x_ref[0] + pmod_ref[0, 0][5:6, :] * mo_ref[0]
        xo_ref[0] = x
    else:
        c_ref, x_ref, mod_ref, g_ref, w_ref, p_ref = refs
        x = _pick_segment(c_ref, x_ref)
    mod = mod_ref[0, 0]
    h = _rms(x) * g_ref[...] * (1.0 + mod[1:2, :]) + mod[0:1, :]
    p_ref[0] = _dot(h, w_ref[...])


def _seg_map(b, i):
    return (b, jnp.minimum(i, 1), 0, 0)


def _segment_specs(d):
    return [pl.BlockSpec((1, TM, d), lambda b, i: (b, 0, 0)),
            pl.BlockSpec((1, TM, d), lambda b, i: (b, jnp.maximum(i - 1, 0), 0))]


def _pick_segment(c_ref, x_ref):
    return jnp.where(pl.program_id(1) == 0, c_ref[0], x_ref[0])


def _inproj(x, mod, g, w, prev=None):
    if prev is None:
        ctx, lat = x
        bsz, S, d = lat.shape
        L = TM + S
    else:
        bsz, L, d = x.shape
    n = w.shape[1]
    nt = L // TM
    tok = pl.BlockSpec((1, TM, d), lambda b, i: (b, i, 0))
    modspec = pl.BlockSpec((1, 1, 6, d), _seg_map)
    tail = [modspec, pl.BlockSpec((1, d), lambda b, i: (0, 0)), pl.BlockSpec((d, n), lambda b, i: (0, 0))]
    pspec = pl.BlockSpec((1, TM, n), lambda b, i: (b, i, 0))
    pshape = jax.ShapeDtypeStruct((bsz, L, n), F32)
    if prev is None:
        return pl.pallas_call(
            functools.partial(_inproj_kernel, fuse_prev=False),
            grid=(bsz, nt), in_specs=_segment_specs(d) + tail, out_specs=pspec, out_shape=pshape,
            compiler_params=_cp(("parallel", "arbitrary")), name="inproj",
        )(ctx, lat, mod, g.reshape(1, d), w)
    moe_out, pmod = prev
    return pl.pallas_call(
        functools.partial(_inproj_kernel, fuse_prev=True),
        grid=(bsz, nt), in_specs=[tok, tok, modspec] + tail,
        out_specs=[tok, pspec], out_shape=[jax.ShapeDtypeStruct(x.shape, F32), pshape],
        compiler_params=_cp(("parallel", "parallel")), name="inproj_res",
    )(x, moe_out, pmod, mod, g.reshape(1, d), w)


def _conv_tile(ref, i, nt, cw, cb, width=TM):
    L = nt * width
    t0 = pl.multiple_of(i * width, width)
    cur = ref[0, pl.ds(t0, width), :]
    prev = ref[0, pl.ds(pl.multiple_of(jnp.maximum(t0 - 8, 0), 8), 8), :]
    nxt = ref[0, pl.ds(pl.multiple_of(jnp.minimum(t0 + width, L - 8), 8), 8), :]
    return _conv_vals(cur, prev, nxt, i, nt, cw, cb, width, first_lat=TM // width)


def _conv_vals(cur, prev, nxt, i, nt, cw, cb, width, first_lat):
    prev = jnp.where((i != 0) & (i != first_lat), prev, 0.0)
    nxt = jnp.where((i != first_lat - 1) & (i != nt - 1), nxt, 0.0)
    cat = jnp.concatenate([prev, cur, nxt], axis=0)
    return (cw[0:1] * cat[6:6 + width] + cw[1:2] * cat[7:7 + width] + cw[2:3] * cur
            + cw[3:4] * cat[9:9 + width] + cb)


def _lru_kernel(ux_ref, cw_ref, cb_ref, gw_ref, gb_ref, lam_ref, o_ref, *, nt):
    cw = cw_ref[...]
    cb = cb_ref[...]
    lam = lam_ref[0]
    sp = _softplus(-lam)
    row = lax.broadcasted_iota(jnp.int32, (TM, 128), 0) & 7

    def gates(i, d):
        xl = _conv_tile(ux_ref, i, nt, cw, cb)
        g = _dot(xl, gw_ref[0, d]) + gb_ref[0, d]
        r = _sigmoid(g[:, :128])
        ig = _sigmoid(g[:, 128:])
        log_a = -LRU_C * r * sp[d:d + 1]
        a = jnp.exp(log_a)
        u = jnp.sqrt(1.0 - a * a) * (ig * xl)
        return a, u

    def scan_tile(i, d, carry, accumulate):
        a, u = gates(i, d)
        rev = d == 1
        for k in (1, 2, 4):
            sh = TM - k if rev else k
            ok = (row < 8 - k) if rev else (row >= k)
            ash = pltpu.roll(a, sh, 0)
            ush = pltpu.roll(u, sh, 0)
            u = jnp.where(ok, u + a * ush, u)
            a = jnp.where(ok, a * ash, a)
        t0 = i * TM
        groups = range(TM // 8)
        for s in (reversed(groups) if rev else groups):
            h = u[s * 8:(s + 1) * 8] + a[s * 8:(s + 1) * 8] * carry
            carry = h[0:1] if rev else h[7:8]
            idx = pl.ds(pl.multiple_of(t0 + s * 8, 8), 8)
            if accumulate:
                o_ref[0, idx, :] += h
            else:
                o_ref[0, idx, :] = h
        return carry

    zero = jnp.zeros((1, 128), F32)
    lax.fori_loop(0, nt, lambda i, c: scan_tile(i, 0, c, False), zero)
    lax.fori_loop(0, nt, lambda j, c: scan_tile(jnp.where(j == 0, 0, nt - j), 1, c, True), zero)


def _lru(proj, conv_w, conv_b, wa, ba, wx, bx, lam):
    bsz, L, _ = proj.shape
    nt = L // TM
    width = conv_w.shape[1]
    ng = width // 128

    def blockdiag(w):
        w = w.reshape(2, ng, 2, 64, 64)
        z = jnp.zeros_like(w[:, :, 0])
        top = jnp.concatenate([w[:, :, 0], z], axis=-1)
        bot = jnp.concatenate([z, w[:, :, 1]], axis=-1)
        return jnp.concatenate([top, bot], axis=-2)
    gw = jnp.concatenate([blockdiag(wa), blockdiag(wx)], axis=-1).transpose(1, 0, 2, 3).astype(BF16)
    gb = jnp.concatenate([ba.reshape(2, ng, 1, 128), bx.reshape(2, ng, 1, 128)], axis=-1).transpose(1, 0, 2, 3)
    lam_g = lam.reshape(2, ng, 128).transpose(1, 0, 2)
    return pl.pallas_call(
        functools.partial(_lru_kernel, nt=nt),
        grid=(bsz, ng),
        in_specs=[pl.BlockSpec((1, L, 128), lambda b, c: (b, 0, c)),
                  pl.BlockSpec((CONV_W, 128), lambda b, c: (0, c)),
                  pl.BlockSpec((1, 128), lambda b, c: (0, c)),
                  pl.BlockSpec((1, 2, 128, 256), lambda b, c: (c, 0, 0, 0)),
                  pl.BlockSpec((1, 2, 1, 256), lambda b, c: (c, 0, 0, 0)),
                  pl.BlockSpec((1, 2, 128), lambda b, c: (c, 0, 0))],
        out_specs=pl.BlockSpec((1, L, 128), lambda b, c: (b, 0, c)),
        out_shape=jax.ShapeDtypeStruct((bsz, L, width), F32),
        compiler_params=_cp(("parallel", "parallel")), name="rglru",
    )(proj, conv_w, conv_b.reshape(1, width), gw, gb, lam_g)


def _na_bias_table(rpb, rows):
    nh = rpb.shape[0]
    win_r = (rpb.shape[1] + 1) // 2
    win_c = (rpb.shape[2] + 1) // 2
    qr = np.arange(NA_RQ)[:, None, None, None]
    qc = np.arange(GRID_W)[None, :, None, None]
    kr = np.arange(NA_RK)[None, None, :, None]
    kc = np.arange(GRID_W)[None, None, None, :]
    cstart = np.clip(qc - win_c // 2, 0, GRID_W - win_c)
    col_ok = (kc >= cstart) & (kc < cstart + win_c)
    dcol = np.clip(kc - qc + (win_c - 1), 0, 2 * win_c - 2)
    oc = (np.arange(2 * win_c - 1)[:, None, None] == dcol[0, :, 0, :][None]).astype(np.float32)
    o_rows, oks = [], []
    for r0, w0 in ((0, 0), (2 * NA_RQ, NA_RQ), (rows - NA_RQ, rows - NA_RK)):
        r = r0 + qr
        kabs = w0 + kr
        rstart = np.clip(r - win_r // 2, 0, rows - win_r)
        oks.append(np.broadcast_to((kabs >= rstart) & (kabs < rstart + win_r) & col_ok,
                                   (NA_RQ, GRID_W, NA_RK, GRID_W)))
        drow = np.clip(kabs - r + (win_r - 1), 0, 2 * win_r - 2)[:, 0, :, 0]
        o_rows.append((np.arange(2 * win_r - 1)[:, None, None] == drow[None]).astype(np.float32))
    t1 = jnp.einsum('hrc,prab->phabc', rpb.astype(F32), jnp.asarray(np.stack(o_rows)), precision=HI)
    b = jnp.einsum('phabc,cqk->phaqbk', t1, jnp.asarray(oc), precision=HI)
    b = jnp.where(jnp.asarray(np.stack(oks))[:, None], b, NEG)
    return b.reshape(3, nh, NA_RQ * GRID_W, NA_RK * GRID_W)


def _pair_rms(x, lo):
    x2 = x * x
    s0 = jnp.sum(jnp.where(lo, x2, 0.0), axis=-1, keepdims=True)
    s1 = jnp.sum(jnp.where(lo, 0.0, x2), axis=-1, keepdims=True)
    return x * lax.rsqrt(jnp.where(lo, s0, s1) * (2.0 / x.shape[-1]) + EPS)


def _na_kernel(q_ref, k_ref, v_ref, bias_ref, qg_ref, kg_ref, o_ref, kn_s, vb_s, *, rows, hd):
    i = pl.program_id(2)
    L = k_ref.shape[1]
    nkeys = NA_RK * GRID_W
    lo = lax.broadcasted_iota(jnp.int32, (1, 2 * hd), 1) < hd

    @pl.when(i == 0)
    def _prep():
        def body(t, _):
            idx = pl.ds(pl.multiple_of(t * TM, TM), TM)
            kn_s[idx, :] = (_pair_rms(k_ref[0, idx, :], lo) * kg_ref[...]).astype(BF16)
            vb_s[idx, :] = v_ref[0, idx, :].astype(BF16)
            return 0
        lax.fori_loop(0, L // TM, body, 0)

    qn = _pair_rms(q_ref[0], lo) * (qg_ref[...] * hd ** -0.5)
    q_h = [jnp.where(lo, qn, 0.0).astype(BF16), jnp.where(lo, 0.0, qn).astype(BF16)]
    kctx = kn_s[0:TM, :]
    vctx = vb_s[0:TM, :]

    def head(hh, kwin, vwin):
        s_c = _dot_nt(q_h[hh], kctx)
        if kwin is not None:
            s_w = _dot_nt(q_h[hh], kwin) + bias_ref[0, hh]
        yield
        m = jnp.max(s_c, axis=-1, keepdims=True)
        if kwin is not None:
            m = jnp.maximum(m, jnp.max(s_w, axis=-1, keepdims=True))
            p_w = jnp.exp(s_w - m)
        p_c = jnp.exp(s_c - m)
        den = jnp.sum(p_c, axis=-1, keepdims=True)
        num = _dot(p_c, vctx)
        if kwin is not None:
            den = den + jnp.sum(p_w, axis=-1, keepdims=True)
            num = num + _dot(p_w, vwin)
        yield
        return num / den

    def both(kwin, vwin):
        o0, o1 = _interleave([head(0, kwin, vwin), head(1, kwin, vwin)])
        o_ref[0] = jnp.where(lo, o0, o1)

    @pl.when(i == 0)
    def _():
        both(None, None)

    @pl.when(i > 0)
    def _():
        r0 = (i - 1) * NA_RQ
        w0 = jnp.clip(r0 - NA_RQ, 0, rows - NA_RK)
        start = pl.multiple_of(TM + w0 * GRID_W, GRID_W)
        both(kn_s[pl.ds(start, nkeys), :], vb_s[pl.ds(start, nkeys), :])


def _na(proj, q_g, k_g, rpb, col0):
    bsz, L, _ = proj.shape
    nh = rpb.shape[0]
    hd = q_g.shape[0]
    width = nh * hd
    rows = (L - TM) // GRID_W
    nb = L // TM
    bias = _na_bias_table(rpb, rows)
    qb, kb, vb = col0 // 128, (col0 + width) // 128, (col0 + 2 * width) // 128

    def pat(i):
        return jnp.where(i <= 1, 0, jnp.where(i == nb - 1, 2, 1))
    return pl.pallas_call(
        functools.partial(_na_kernel, rows=rows, hd=hd),
        grid=(bsz, width // 128, nb),
        in_specs=[pl.BlockSpec((1, TM, 128), lambda b, h, i: (b, i, qb + h)),
                  pl.BlockSpec((1, L, 128), lambda b, h, i: (b, 0, kb + h)),
                  pl.BlockSpec((1, L, 128), lambda b, h, i: (b, 0, vb + h)),
                  pl.BlockSpec((1, 2, TM, NA_RK * GRID_W), lambda b, h, i: (pat(i), h, 0, 0)),
                  pl.BlockSpec((1, 2 * hd), lambda b, h, i: (0, 0)),
                  pl.BlockSpec((1, 2 * hd), lambda b, h, i: (0, 0))],
        out_specs=pl.BlockSpec((1, TM, 128), lambda b, h, i: (b, i, h)),
        out_shape=jax.ShapeDtypeStruct((bsz, L, width), F32),
        scratch_shapes=[pltpu.VMEM((L, 128), BF16), pltpu.VMEM((L, 128), BF16)],
        compiler_params=_cp(("parallel", "parallel", "arbitrary")), name="nbr_attn",
    )(proj, proj, proj, bias, jnp.tile(q_g, 2).reshape(1, 2 * hd), jnp.tile(k_g, 2).reshape(1, 2 * hd))


def _route(lg):
    lane = lax.broadcasted_iota(jnp.int32, lg.shape, 1)
    lane_f = lane.astype(F32)
    is_g = lane < EXPERT_LANE0
    gl = jnp.where(is_g, lg, NEG)
    gmax = jnp.max(gl, axis=-1, keepdims=True)
    gsel = jnp.min(jnp.where(is_g & (gl == gmax), lane_f, 1e9), axis=-1, keepdims=True)
    g_w = 1.0 / jnp.sum(jnp.where(is_g, jnp.exp(gl - gmax), 0.0), axis=-1, keepdims=True)
    grp = ((lane - EXPERT_LANE0) >> 2).astype(F32)
    in_g = (lane >= EXPERT_LANE0) & (lane < EXPERT_LANE0 + N_EXPERTS) & (grp == gsel)
    el = jnp.where(in_g, lg, NEG)
    v1 = jnp.max(el, axis=-1, keepdims=True)
    i1 = jnp.min(jnp.where(in_g & (el == v1), lane_f, 1e9), axis=-1, keepdims=True)
    el2 = jnp.where(lane_f == i1, NEG, el)
    v2 = jnp.max(el2, axis=-1, keepdims=True)
    i2 = jnp.min(jnp.where(in_g & (lane_f != i1) & (el2 == v2), lane_f, 1e9), axis=-1, keepdims=True)
    t = jnp.exp(v2 - v1)
    w1 = g_w / (1.0 + t)
    w2 = g_w * t / (1.0 + t)
    return (jnp.where(lane_f == i1, w1, 0.0) + jnp.where(lane_f == i2, w2, 0.0)
            + jnp.where(lane_f == gsel, 1.0, 0.0))


def _outproj_kernel(*refs, even):
    if even:
        (r_ref, ug_ref, a_ref, c_ref, x_ref, w_ref, mod_ref, g2_ref, rw_ref, x1_ref, h2_ref, gate_ref) = refs
        is_ctx = pl.program_id(1) == 0
        x_rows = lambda rs: jnp.where(is_ctx, c_ref[0, rs, :], x_ref[0, rs, :])
    else:
        (ys_ref, z_ref, hm_ref, mo_ref, sg_ref, mg_ref, x_ref, w_ref, mod_ref, g2_ref, rw_ref,
         x1_ref, h2_ref, gate_ref) = refs
        x_rows = lambda rs: x_ref[0, rs, :]
    mod = mod_ref[0, 0]

    def part(rs):
        if even:
            y_in = jnp.concatenate([r_ref[0, rs, :] * _gelu_tanh(ug_ref[0, rs, :]), a_ref[0, rs, :]], axis=-1)
        else:
            ys = ys_ref[0, rs, :] * _silu(z_ref[0, rs, :])
            sg = sg_ref[...]
            mg = mg_ref[...]
            hm = hm_ref[0, rs, :]
            sig_o = _sigmoid(mo_ref[0, rs, :])
            gw = ys.shape[-1] // 2
            parts = [_rms(ys[:, g * gw:(g + 1) * gw]) * sg[:, g * gw:(g + 1) * gw] for g in range(2)]
            hw = 128
            parts += [_rms(hm[:, h * hw:(h + 1) * hw]) * mg[:, h * hw:(h + 1) * hw] * sig_o[:, h * hw:(h + 1) * hw]
                      for h in range(hm.shape[-1] // hw)]
            y_in = jnp.concatenate(parts, axis=-1)
        y = _dot(y_in, w_ref[...])
        yield
        x1 = x_rows(rs) + mod[2:3, :] * y
        x1_ref[0, rs, :] = x1
        h2 = _rms(x1) * g2_ref[...] * (1.0 + mod[4:5, :]) + mod[3:4, :]
        h2_ref[0, rs, :] = h2.astype(BF16)
        hi = h2.astype(BF16)
        lo = (h2 - hi.astype(F32)).astype(BF16)
        lg2 = jnp.dot(hi, rw_ref[...], preferred_element_type=F32)
        lg1 = jnp.dot(lo, rw_ref[:, 0:128], preferred_element_type=F32)
        yield
        gate_ref[0, rs, :] = _route(lg2[:, 0:128] + lg2[:, 128:256] + lg1)

    nparts = 2
    rows = TM // nparts
    _interleave([part(pl.ds(p * rows, rows)) for p in range(nparts)])


def _outproj(mix_inputs, x, w, mod, g2, rw, even, tile0, ntiles):
    xs = list(x) if even else [x]
    bsz, _, d = xs[-1].shape
    specs, args = [], []
    for arr, cb, wdt in mix_inputs:
        if arr.ndim == 3:
            specs.append(pl.BlockSpec((1, TM, wdt), lambda b, i, cb=cb: (b, i + tile0, cb)))
        else:
            specs.append(pl.BlockSpec((1, wdt), lambda b, i: (0, 0)))
        args.append(arr)
    specs += _segment_specs(d) if even else [pl.BlockSpec((1, TM, d), lambda b, i: (b, i + tile0, 0))]
    specs += [pl.BlockSpec(w.shape, lambda b, i: (0, 0)),
              pl.BlockSpec((1, 1, 6, d), lambda b, i: (b, jnp.minimum(i + tile0, 1), 0, 0)),
              pl.BlockSpec((1, d), lambda b, i: (0, 0)),
              pl.BlockSpec(rw.shape, lambda b, i: (0, 0))]
    args += xs + [w, mod, g2.reshape(1, d), rw]
    lo = ntiles * TM
    return pl.pallas_call(
        functools.partial(_outproj_kernel, even=even),
        grid=(bsz, ntiles), in_specs=specs,
        out_specs=[pl.BlockSpec((1, TM, d), lambda b, i: (b, i, 0)),
                   pl.BlockSpec((1, TM, d), lambda b, i: (b, i, 0)),
                   pl.BlockSpec((1, TM, 128), lambda b, i: (b, i, 0))],
        out_shape=[jax.ShapeDtypeStruct((bsz, lo, d), F32), jax.ShapeDtypeStruct((bsz, lo, d), BF16),
                   jax.ShapeDtypeStruct((bsz, lo, 128), F32)],
        compiler_params=_cp(("parallel", "parallel")), name="outproj_even" if even else "outproj_odd",
    )(*args)


MOE_CH = 128
GROUP_SIZE = 4
N_GROUPS = N_EXPERTS // GROUP_SIZE


def _moe_kernel(*refs, residual):
    if residual:
        x_ref, g_ref, tri_ref, w1_ref, w3_ref, w2_ref, x1_ref, mod_ref, o_ref, xs_s, ys_s, gs_s, pt_s, plan_s = refs
    else:
        x_ref, g_ref, tri_ref, w1_ref, w3_ref, w2_ref, o_ref, xs_s, ys_s, gs_s, pt_s, plan_s = refs
    e = pl.program_id(1)
    tb = x_ref.shape[0]
    nch = xs_s.shape[0] // MOE_CH

    @pl.when(e == 0)
    def _plan():
        g = g_ref[...]
        lane = lax.broadcasted_iota(jnp.int32, g.shape, 1)
        oh = jnp.where(lane < N_GROUPS, g, 0.0)
        rank = jnp.dot(tri_ref[...], oh.astype(BF16), preferred_element_type=F32)
        cnt = jnp.sum(oh, axis=0, keepdims=True)
        lane1 = lax.broadcasted_iota(jnp.int32, (1, 128), 1)
        off = jnp.int32(0)
        offv = jnp.zeros((1, 128), F32)
        for gi in range(N_GROUPS):
            n = jnp.sum(jnp.where(lane1 == gi, cnt, 0.0)).astype(jnp.int32)
            nchunks = (n + (MOE_CH - 1)) // MOE_CH
            plan_s[gi] = off // MOE_CH
            plan_s[N_GROUPS + gi] = nchunks
            offv = offv + jnp.where(lane1 == gi, off.astype(F32), 0.0)
            off = off + nchunks * MOE_CH
        pos_col = jnp.sum(oh * (rank + offv), axis=1, keepdims=True)
        posb = jnp.broadcast_to(pos_col, (tb, 128))
        pos_row = jnp.concatenate([posb[i * 128:(i + 1) * 128, :].T[0:1, :] for i in range(tb // 128)], axis=1)
        x = x_ref[...]
        g_hi = g.astype(BF16)
        g_lo = (g - g_hi.astype(F32)).astype(BF16)
        lane_c = lax.broadcasted_iota(jnp.int32, (tb, MOE_CH), 1).astype(F32)
        row_c = lax.broadcasted_iota(jnp.int32, (MOE_CH, tb), 0).astype(F32)
        for c in range(nch):
            sl = slice(c * MOE_CH, (c + 1) * MOE_CH)
            pt_s[:, sl] = jnp.where(pos_col == lane_c + float(c * MOE_CH), 1.0, 0.0).astype(BF16)
            p = jnp.where(row_c + float(c * MOE_CH) == pos_row, 1.0, 0.0).astype(BF16)
            xs_s[sl, :] = jnp.dot(p, x, preferred_element_type=F32).astype(BF16)
            gs_s[sl, :] = (jnp.dot(p, g_hi, preferred_element_type=F32)
                           + jnp.dot(p, g_lo, preferred_element_type=F32))
        ys_s[...] = jnp.zeros_like(ys_s)

    grp = e // GROUP_SIZE
    c0 = plan_s[grp]
    nchunks = plan_s[N_GROUPS + grp]

    def ffn(chunk0, nrows):
        rows = pl.ds(pl.multiple_of(chunk0 * MOE_CH, MOE_CH), nrows)
        xs = xs_s[rows, :]
        a = jnp.dot(xs, w1_ref[0, 0], preferred_element_type=F32)
        b = jnp.dot(xs, w3_ref[0, 0], preferred_element_type=F32)
        y = _dot(_silu(a) * b, w2_ref[0, 0])
        lane_g = lax.broadcasted_iota(jnp.int32, (nrows, 128), 1)
        ge = jnp.sum(jnp.where(lane_g == e + EXPERT_LANE0, gs_s[rows, :], 0.0), axis=-1, keepdims=True)
        ys_s[rows, :] += ge * y

    def pair(p, _):
        ffn(c0 + 2 * p, 2 * MOE_CH)
        return 0
    lax.fori_loop(0, nchunks // 2, pair, 0)

    @pl.when(nchunks % 2 == 1)
    def _():
        ffn(c0 + nchunks - 1, MOE_CH)

    @pl.when(e == N_EXPERTS - 1)
    def _combine():
        out = jnp.dot(pt_s[...], ys_s[...].astype(BF16), preferred_element_type=F32)
        if residual:
            out = x1_ref[...] + mod_ref[0, 0][5:6, :] * out
        o_ref[...] = out


def _moe(h2, gate, layer, w1, w3, w2, resid=None):
    t, d = h2.shape
    _, ne, _, ff = w1.shape
    tb = math.gcd(t, 1024)
    npad = tb + N_GROUPS * MOE_CH
    tri = jnp.asarray(np.tril(np.ones((tb, tb), np.float32), -1), BF16)
    extra_specs, extra_args = [], []
    if resid is not None:
        x1, mod = resid
        per_batch = t // mod.shape[0] // tb
        extra_specs = [pl.BlockSpec((tb, d), lambda i, e: (i, 0)),
                       pl.BlockSpec((1, 1, 6, d), lambda i, e: (i // per_batch, 1, 0, 0))]
        extra_args = [x1, mod]
    return pl.pallas_call(
        functools.partial(_moe_kernel, residual=resid is not None),
        grid=(t // tb, ne),
        in_specs=[pl.BlockSpec((tb, d), lambda i, e: (i, 0)),
                  pl.BlockSpec((tb, 128), lambda i, e: (i, 0)),
                  pl.BlockSpec((tb, tb), lambda i, e: (0, 0)),
                  pl.BlockSpec((1, 1, d, ff), lambda i, e: (layer, e, 0, 0)),
                  pl.BlockSpec((1, 1, d, ff), lambda i, e: (layer, e, 0, 0)),
                  pl.BlockSpec((1, 1, ff, d), lambda i, e: (layer, e, 0, 0))] + extra_specs,
        out_specs=pl.BlockSpec((tb, d), lambda i, e: (i, 0)),
        out_shape=jax.ShapeDtypeStruct((t, d), F32),
        scratch_shapes=[pltpu.VMEM((npad, d), BF16), pltpu.VMEM((npad, d), F32), pltpu.VMEM((npad, 128), F32),
                        pltpu.VMEM((tb, npad), BF16), pltpu.SMEM((2 * N_GROUPS,), jnp.int32)],
        compiler_params=_cp(("parallel", "arbitrary")), name="moe",
    )(h2, gate, tri, w1, w3, w2, *extra_args)


def _tri(q, rev):
    r = lax.broadcasted_iota(jnp.int32, (q, q), 0)
    c = lax.broadcasted_iota(jnp.int32, (q, q), 1)
    return (c >= r) if rev else (c <= r)


def _ssd_kernel(xs_ref, b_ref, c_ref, dt_ref, cwx_ref, cbx_ref, cwb_ref, cbb_ref, cwc_ref, cbc_ref,
                dtb_ref, alog_ref, dsk_ref, sel_ref, o_ref, xc_s, bc_s, cc_s, dt_s, bct_s, *, nc):
    q = CHUNK
    first_lat = TM // q

    dtb = dtb_ref[...]

    def prep(i, _):
        t0 = pl.multiple_of(i * q, q)
        idx = pl.ds(t0, q)
        for src, cw, cb, dst in ((xs_ref, cwx_ref, cbx_ref, xc_s), (b_ref, cwb_ref, cbb_ref, bc_s),
                                 (c_ref, cwc_ref, cbc_ref, cc_s)):
            L = nc * q
            cur = src[0, idx, :]
            prev = src[0, pl.ds(pl.multiple_of(jnp.maximum(t0 - 8, 0), 8), 8), :]
            nxt = src[0, pl.ds(pl.multiple_of(jnp.minimum(t0 + q, L - 8), 8), 8), :]
            dst[idx, :] = _silu(_conv_vals(cur, prev, nxt, i, nc, cw[...], cb[...], q, first_lat))
        dt_s[idx, :] = _softplus(dt_ref[0, idx, :] + dtb)
        o_ref[0, idx, :] = dsk_ref[...] * xc_s[idx, :]
        bct_s[i] = bc_s[idx, :].T.astype(BF16)
        return 0
    lax.fori_loop(0, nc, prep, 0)

    a_lane = jnp.broadcast_to(-jnp.exp(alog_ref[...]), (8, 128))
    a_e = [_dot_hi(a_lane, sel_ref[d, 0])[0:1, :] for d in (0, 1)]
    lane256 = lax.broadcasted_iota(jnp.int32, (q, 256), 1) // 64
    r8 = lax.broadcasted_iota(jnp.int32, (8, 256), 0)
    l8 = lax.broadcasted_iota(jnp.int32, (8, 256), 1)
    head_rows = (l8 == r8 * 64).astype(F32)

    def chain(d, j, h):
        rev = d == 1
        causal = _tri(q, rev)
        tri = causal.astype(F32)
        sel = sel_ref[d, 0]
        last = 0 if rev else q - 1
        if rev:
            ci = jnp.where(j < first_lat, first_lat - 1 - j, nc - 1 - (j - first_lat))
        else:
            ci = j
        idx = pl.ds(pl.multiple_of(ci * q, q), q)
        dt_e = _dot_sel_r(dt_s[idx, :], sel)
        xc = xc_s[idx, :]
        bc = bc_s[idx, :]
        cc = cc_s[idx, :]
        cb = _dot_nt(cc, bc)
        y_off = _dot(cc, h)
        yield
        cum_e = _dot_sel(tri, dt_e * a_e[d])
        dtx = xc * dt_e
        yield
        cum_rows = _dot_sel(head_rows, cum_e, _NT)
        tot_e = cum_e[last:last + 1, :]
        states = _dot(bct_s[ci], dtx * jnp.exp(tot_e - cum_e))
        yield
        y = jnp.exp(cum_e) * y_off
        for r in range(4):
            seg = cum_e[:, r * 64:r * 64 + 1] - cum_rows[r:r + 1, :]
            m = cb * jnp.exp(jnp.where(causal, seg, NEG))
            y = y + _dot(m, jnp.where(lane256 == r, dtx, 0.0))
            yield
        o_ref[0, idx, :] += y
        return jnp.exp(tot_e) * h + states

    def chunk(j, hs):
        return tuple(_interleave([chain(0, j, hs[0]), chain(1, j, hs[1])]))
    h0 = jnp.zeros((128, 256), F32)
    lax.fori_loop(0, nc, chunk, (h0, h0))


def _ssd(proj, conv_w, conv_b, dt_bias, a_log, d_skip, cols):
    bsz, L, _ = proj.shape
    nc = L // CHUNK
    cx, cbm, ccm, cdt = cols
    nh = d_skip.shape[0]
    dtb = jnp.zeros((1, 128), F32).at[0, :2 * nh].set(dt_bias.reshape(-1))
    alog = jnp.zeros((1, 128), F32).at[0, :2 * nh].set(a_log.reshape(-1))
    dsk = jnp.repeat(d_skip.astype(F32), 64).reshape(1, nh * 64)
    sel = np.zeros((2, 2, 128, 256), np.float32)
    for d in range(2):
        for g in range(2):
            for r in range(4):
                sel[d, g, d * nh + g * 4 + r, r * 64:(r + 1) * 64] = 1.0
    cw_specs = []
    for width, off in ((256, 0), (128, 512), (128, 768)):
        cw_specs += [pl.BlockSpec((CONV_W, width), lambda b, g, off=off, width=width: (0, off // width + g)),
                     pl.BlockSpec((1, width), lambda b, g, off=off, width=width: (0, off // width + g))]
    return pl.pallas_call(
        functools.partial(_ssd_kernel, nc=nc),
        grid=(bsz, 2),
        in_specs=[pl.BlockSpec((1, L, 256), lambda b, g: (b, 0, cx // 256 + g)),
                  pl.BlockSpec((1, L, 128), lambda b, g: (b, 0, cbm // 128 + g)),
                  pl.BlockSpec((1, L, 128), lambda b, g: (b, 0, ccm // 128 + g)),
                  pl.BlockSpec((1, L, 128), lambda b, g: (b, 0, cdt // 128))] + cw_specs + [
                  pl.BlockSpec((1, 128), lambda b, g: (0, 0)),
                  pl.BlockSpec((1, 128), lambda b, g: (0, 0)),
                  pl.BlockSpec((1, 256), lambda b, g: (0, g)),
                  pl.BlockSpec((2, 1, 128, 256), lambda b, g: (0, g, 0, 0))],
        out_specs=pl.BlockSpec((1, L, 256), lambda b, g: (b, 0, g)),
        out_shape=jax.ShapeDtypeStruct((bsz, L, nh * 64), F32),
        scratch_shapes=[pltpu.VMEM((L, 256), F32), pltpu.VMEM((L, 128), F32), pltpu.VMEM((L, 128), F32),
                        pltpu.VMEM((L, 128), F32), pltpu.VMEM((nc, 128, CHUNK), BF16)],
        compiler_params=_cp(("parallel", "parallel")), name="ssd",
    )(proj, proj, proj, proj, conv_w, conv_b.reshape(1, -1), conv_w, conv_b.reshape(1, -1),
      conv_w, conv_b.reshape(1, -1), dtb, alog, dsk, jnp.asarray(sel))


ML_HP = 2


def _ml_gate_lanes(hh, d):
    li = (hh * 2 + d) * 2
    return li, li + 1


def _ml_gate_columns(nh):
    cols = []
    for hp in range(nh // ML_HP):
        blk = [None] * (4 * ML_HP)
        for hh in range(ML_HP):
            for d in range(2):
                for t, lane in enumerate(_ml_gate_lanes(hh, d)):
                    blk[lane] = (d * 2 + t) * nh + hp * ML_HP + hh
        cols.append(blk)
    return cols


def _mlstm_kernel(q_ref, k_ref, v_ref, g_ref, cwq_ref, cbq_ref, cwk_ref, cbk_ref, gb_ref, o_ref,
                  qc_s, kc_s, kct_s, *, nc, dh):
    q = CHUNK
    first_lat = TM // q

    def prep(i, _):
        t0 = pl.multiple_of(i * q, q)
        idx = pl.ds(t0, q)
        L = nc * q
        for src, cw, cb, dst, mul in ((q_ref, cwq_ref, cbq_ref, qc_s, 1.0), (k_ref, cwk_ref, cbk_ref, kc_s, dh ** -0.5)):
            cur = src[0, idx, :]
            prev = src[0, pl.ds(pl.multiple_of(jnp.maximum(t0 - 8, 0), 8), 8), :]
            nxt = src[0, pl.ds(pl.multiple_of(jnp.minimum(t0 + q, L - 8), 8), 8), :]
            dst[idx, :] = _silu(_conv_vals(cur, prev, nxt, i, nc, cw[...], cb[...], q, first_lat)) * mul
        for hh in range(ML_HP):
            kct_s[i, hh * dh:(hh + 1) * dh, :] = kc_s[idx, hh * dh:(hh + 1) * dh].T.astype(BF16)
        return 0
    lax.fori_loop(0, nc, prep, 0)

    gb = gb_ref[0]
    lane = lax.broadcasted_iota(jnp.int32, (q, 128), 1)
    eye8 = (lax.broadcasted_iota(jnp.int32, (8, 128), 0) == lax.broadcasted_iota(jnp.int32, (8, 128), 1)).astype(F32)
    o_ref[...] = jnp.zeros_like(o_ref)
    heads = range(ML_HP)

    def direction(d, j, states):
        rev = d == 1
        causal = _tri(q, rev)
        last = 0 if rev else q - 1
        if rev:
            ci = jnp.where(j < first_lat, first_lat - 1 - j, nc - 1 - (j - first_lat))
        else:
            ci = j
        idx = pl.ds(pl.multiple_of(ci * q, q), q)
        g = g_ref[0, idx, :] + gb
        logf = jnp.minimum(g, 0.0) - jnp.log(1.0 + jnp.exp(-jnp.abs(g)))
        is_f = functools.reduce(jnp.logical_or, [lane == _ml_gate_lanes(hh, d)[1] for hh in heads])
        cum = _dot_sel(jnp.where(causal, 1.0, 0.0), jnp.where(is_f, logf, 0.0))
        qc = [qc_s[idx, hh * dh:(hh + 1) * dh] for hh in heads]
        kc = [kc_s[idx, hh * dh:(hh + 1) * dh] for hh in heads]
        vc = [v_ref[0, idx, hh * dh:(hh + 1) * dh] for hh in heads]
        kct = [kct_s[ci, hh * dh:(hh + 1) * dh, :] for hh in heads]
        qk = [_dot(qc[hh], kct[hh]) for hh in heads]
        inter_c = [_dot(qc[hh], states[hh][0]) for hh in heads]
        yield
        rows = _dot_sel(eye8, jnp.where(is_f, cum, g), _NT)
        yield
        w, w_s, g_in, m_t, keep, m_new, wv, upd = [], [], [], [], [], [], [], []
        for hh in heads:
            li, lf = _ml_gate_lanes(hh, d)
            m_prev = states[hh][2]
            i_col, i_row = g[:, li:li + 1], rows[li:li + 1, :]
            b_col, b_row = cum[:, lf:lf + 1], rows[lf:lf + 1, :]
            dlog = jnp.where(causal, b_col - b_row + i_row, NEG)
            inter = b_col + m_prev
            m_t.append(jnp.maximum(jnp.max(dlog, axis=-1, keepdims=True), inter))
            w.append(jnp.exp(dlog - m_t[hh]) * qk[hh])
            g_in.append(jnp.exp(inter - m_t[hh]))
            b_end = b_col[last:last + 1, :]
            g_s = b_end - b_col + i_col
            m_new.append(jnp.maximum(jnp.max(g_s, axis=0, keepdims=True), b_end + m_prev))
            w_s.append(jnp.exp(g_s - m_new[hh]))
            keep.append(jnp.exp(b_end + m_prev - m_new[hh]))
            wv.append(_dot(w[hh], vc[hh]))
            upd.append(_dot(kct[hh], w_s[hh] * vc[hh]))
        yield
        new_states = []
        for hh in heads:
            c_prev, n_prev, _ = states[hh]
            num = wv[hh] + g_in[hh] * inter_c[hh]
            den = (jnp.sum(w[hh], axis=-1, keepdims=True)
                   + g_in[hh] * jnp.sum(qc[hh] * n_prev, axis=-1, keepdims=True))
            o_ref[0, idx, hh * dh:(hh + 1) * dh] += num / jnp.maximum(jnp.abs(den), jnp.exp(-m_t[hh]))
            new_states.append((keep[hh] * c_prev + upd[hh],
                               keep[hh] * n_prev + jnp.sum(w_s[hh] * kc[hh], axis=0, keepdims=True), m_new[hh]))
        return tuple(new_states)

    def chunk(j, states):
        return tuple(_interleave([direction(d, j, states[d]) for d in (0, 1)]))
    init = (jnp.zeros((dh, dh), F32), jnp.zeros((1, dh), F32), jnp.full((1, 1), NEG, F32))
    lax.fori_loop(0, nc, chunk, tuple(tuple(init for _ in heads) for _ in (0, 1)))


def _mlstm(proj, conv_w, conv_b, gate_b, cols):
    bsz, L, _ = proj.shape
    nc = L // CHUNK
    cq, ck, cv, cg = cols
    nh = gate_b.shape[-1]
    dh = conv_w.shape[1] // (2 * nh)
    bw = ML_HP * dh
    gcols = np.asarray(_ml_gate_columns(nh))
    gbl = jnp.zeros((nh // ML_HP, 1, 128), F32).at[:, 0, :gcols.shape[1]].set(gate_b.reshape(-1)[gcols])
    return pl.pallas_call(
        functools.partial(_mlstm_kernel, nc=nc, dh=dh),
        grid=(bsz, nh // ML_HP),
        in_specs=[pl.BlockSpec((1, L, bw), lambda b, h: (b, 0, cq // bw + h)),
                  pl.BlockSpec((1, L, bw), lambda b, h: (b, 0, ck // bw + h)),
                  pl.BlockSpec((1, L, bw), lambda b, h: (b, 0, cv // bw + h)),
                  pl.BlockSpec((1, L, 128), lambda b, h: (b, 0, cg // 128 + h)),
                  pl.BlockSpec((CONV_W, bw), lambda b, h: (0, h)),
                  pl.BlockSpec((1, bw), lambda b, h: (0, h)),
                  pl.BlockSpec((CONV_W, bw), lambda b, h: (0, nh // ML_HP + h)),
                  pl.BlockSpec((1, bw), lambda b, h: (0, nh // ML_HP + h)),
                  pl.BlockSpec((1, 1, 128), lambda b, h: (h, 0, 0))],
        out_specs=pl.BlockSpec((1, L, bw), lambda b, h: (b, 0, h)),
        out_shape=jax.ShapeDtypeStruct((bsz, L, nh * dh), F32),
        scratch_shapes=[pltpu.VMEM((L, bw), F32), pltpu.VMEM((L, bw), F32), pltpu.VMEM((nc, bw, CHUNK), BF16)],
        compiler_params=_cp(("parallel", "parallel")), name="mlstm",
    )(proj, proj, proj, proj, conv_w, conv_b.reshape(1, -1), conv_w, conv_b.reshape(1, -1), gbl)


def kernel(x, c, ctx, c_ctx, mod_w, mod_b, norm1_g, norm2_g, even_w_in, even_w_out, lru_conv_w, lru_conv_b, lru_wa, lru_ba, lru_wx, lru_bx, lru_lam, na_q_g, na_k_g, na_rpb, odd_w_in, odd_w_out, ssd_conv_w, ssd_conv_b, ssd_dt_bias, ssd_a_log, ssd_d, ssd_norm_g, ml_conv_w, ml_conv_b, ml_gate_b, ml_norm_g, moe_router_g, moe_router_e, moe_w1, moe_w3, moe_w2):
    bsz, S, d = x.shape
    lc = ctx.shape[1]
    assert lc == TM and S % TM == 0 and mod_w.shape[0] == 2
    nt = (lc + S) // TM
    L = lc + S

    cc = jnp.zeros((8, d), F32).at[0].set(c_ctx).at[1:1 + bsz].set(c)
    mod_all = _modulation(cc, mod_w, mod_b)

    def mod_for(l):
        m = mod_all[l].reshape(8, 6, d)
        return jnp.stack([jnp.broadcast_to(m[0], (bsz, 6, d)), m[1:1 + bsz]], axis=1)

    def router_w(l):
        w = jnp.zeros((d, 128), F32).at[:, :EXPERT_LANE0].set(moe_router_g[l]) \
            .at[:, EXPERT_LANE0:EXPERT_LANE0 + N_EXPERTS].set(moe_router_e[l])
        hi = w.astype(BF16)
        return jnp.concatenate([hi, (w - hi.astype(F32)).astype(BF16)], axis=1)

    xx = (ctx, x)

    mod0 = mod_for(0)
    proj = _inproj(xx, mod0, norm1_g[0], even_w_in[0].astype(BF16))
    lw = lru_conv_w.shape[-1]
    r = _lru(proj, lru_conv_w[0], lru_conv_b[0], lru_wa[0], lru_ba[0], lru_wx[0], lru_bx[0], lru_lam[0])
    a = _na(proj, na_q_g[0], na_k_g[0], na_rpb[0], col0=2 * lw)
    x1, h2, gate = _outproj([(r, 0, lw), (proj, 1, lw), (a, 0, lw)], xx, even_w_out[0].astype(BF16), mod0,
                            norm2_g[0], router_w(0), even=True, tile0=0, ntiles=nt)
    ew = (moe_w1.astype(BF16), moe_w3.astype(BF16), moe_w2.astype(BF16))
    moe0 = _moe(h2.reshape(bsz * L, d), gate.reshape(bsz * L, 128), 0, *ew).reshape(bsz, L, d)

    mod1 = mod_for(1)
    sw = ssd_d.shape[-1] * 64
    xbc = ssd_conv_w.shape[-1]
    mw = ml_norm_g.shape[-1]
    w = odd_w_in[0]
    o = np.cumsum([0, sw, xbc, 2 * ssd_d.shape[-1], mw, mw, mw, mw])
    ndt = 2 * ssd_d.shape[-1]
    dt_blk = jnp.concatenate([w[:, o[2]:o[3]], jnp.zeros((d, 128 - ndt), F32)], axis=1)
    gate_blks = [jnp.concatenate([w[:, o[7] + np.asarray(blk)], jnp.zeros((d, 128 - len(blk)), F32)], axis=1)
                 for blk in _ml_gate_columns(ml_gate_b.shape[-1])]
    w_odd = jnp.concatenate([w[:, :o[2]], w[:, o[3]:o[7]], dt_blk] + gate_blks, axis=1).astype(BF16)
    cz, cxs = 0, sw
    cB, cC = cxs + sw, cxs + sw + (xbc - sw) // 2
    cq = sw + xbc
    ck, cv, co, csm = cq + mw, cq + 2 * mw, cq + 3 * mw, cq + 4 * mw
    x0, proj1 = _inproj(x1, mod1, norm1_g[1], w_odd, prev=(moe0, mod0))
    ys = _ssd(proj1, ssd_conv_w[0], ssd_conv_b[0], ssd_dt_bias[0], ssd_a_log[0], ssd_d[0], (cxs, cB, cC, csm))
    hm = _mlstm(proj1, ml_conv_w[0], ml_conv_b[0], ml_gate_b[0], (cq, ck, cv, csm + 128))
    x2, h2b, gate1 = _outproj([(ys, 0, sw), (proj1, cz // sw, sw), (hm, 0, mw), (proj1, co // mw, mw),
                               (ssd_norm_g[0].reshape(1, sw), 0, sw), (ml_norm_g[0].reshape(1, mw), 0, mw)],
                              x0, odd_w_out[0].astype(BF16), mod1, norm2_g[1], router_w(1),
                              even=False, tile0=1, ntiles=nt - 1)
    out = _moe(h2b.reshape(bsz * S, d), gate1.reshape(bsz * S, 128), 1, *ew, resid=(x2.reshape(bsz * S, d), mod1))
    return out.reshape(bsz, S, d)
```

```python
import functools
import math

import jax
import jax.numpy as jnp
import numpy as np
from jax import lax
from jax.experimental import pallas as pl
from jax.experimental.pallas import tpu as pltpu

F32 = jnp.float32
BF16 = jnp.bfloat16
HI = lax.Precision.HIGHEST

EPS = 1e-6
NEG = -1e30
GRID_W = 64
CONV_W = 4
LRU_C = 8.0
TM = 256
CHUNK = 128
NA_RQ = 4
NA_RK = 12
N_EXPERTS = 16
EXPERT_LANE0 = 4
VMEM_LIMIT = 56 * 1024 * 1024


def _cp(sem, vmem=VMEM_LIMIT):
    return pltpu.CompilerParams(dimension_semantics=sem, vmem_limit_bytes=vmem)


def _sigmoid(x):
    return jax.nn.sigmoid(x)


def _silu(x):
    return x * jax.nn.sigmoid(x)


def _softplus(x):
    return jnp.maximum(x, 0.0) + jnp.log(1.0 + jnp.exp(-jnp.abs(x)))


def _gelu_tanh(x):
    return 0.5 * x * (1.0 + jnp.tanh(math.sqrt(2.0 / math.pi) * (x + 0.044715 * (x * x * x))))


def _rms(x, axis=-1):
    return x * lax.rsqrt(jnp.mean(x * x, axis=axis, keepdims=True) + EPS)


def _dot(a, b):
    return jnp.dot(a.astype(BF16), b.astype(BF16), preferred_element_type=F32)


def _dot_hi(a, b):
    return jnp.dot(a, b, precision=HI, preferred_element_type=F32)


def _split3(x):
    x1 = x.astype(BF16)
    r = x - x1.astype(F32)
    x2 = r.astype(BF16)
    x3 = (r - x2.astype(F32)).astype(BF16)
    return x1, x2, x3


def _dot_sel(sel, x, dims=(((1,), (0,)), ((), ()))):
    s = sel.astype(BF16)
    x1, x2, x3 = _split3(x)
    d = lambda xi: lax.dot_general(s, xi, dims, preferred_element_type=F32)
    return (d(x3) + d(x2)) + d(x1)


def _dot_sel_r(x, sel):
    s = sel.astype(BF16)
    x1, x2, x3 = _split3(x)
    d = lambda xi: jnp.dot(xi, s, preferred_element_type=F32)
    return (d(x3) + d(x2)) + d(x1)


_NT = (((1,), (1,)), ((), ()))


def _interleave(gens):
    results = [None] * len(gens)
    live = list(range(len(gens)))
    while live:
        for i in list(live):
            try:
                next(gens[i])
            except StopIteration as stop:
                results[i] = stop.value
                live.remove(i)
    return results


def _dot_nt(a, b):
    return lax.dot_general(a.astype(BF16), b.astype(BF16), (((1,), (1,)), ((), ())),
                           preferred_element_type=F32)


def _mod_kernel(c_ref, w_ref, b_ref, o_ref):
    c = c_ref[...]
    o_ref[0] = _dot_hi(_silu(c), w_ref[0]) + b_ref[0]


def _modulation(cc, mod_w, mod_b):
    depth, d, n = mod_w.shape
    tn = 1536
    return pl.pallas_call(
        _mod_kernel,
        grid=(depth, n // tn),
        in_specs=[pl.BlockSpec((8, d), lambda l, j: (0, 0)),
                  pl.BlockSpec((1, d, tn), lambda l, j: (l, 0, j)),
                  pl.BlockSpec((1, 1, tn), lambda l, j: (l, 0, j))],
        out_specs=pl.BlockSpec((1, 8, tn), lambda l, j: (l, 0, j)),
        out_shape=jax.ShapeDtypeStruct((depth, 8, n), F32),
        compiler_params=_cp(("arbitrary", "arbitrary")),
        name="adaln_mod",
    )(cc, mod_w, mod_b.reshape(depth, 1, n))


def _inproj_kernel(*refs, fuse_prev):
    if fuse_prev:
        x_ref, mo_ref, pmod_ref, mod_ref, g_ref, w_ref, xo_ref, p_ref = refs
        x = x_ref[0] + pmod_ref[0, 0][5:6, :] * mo_ref[0]
        xo_ref[0] = x
    else:
        c_ref, x_ref, mod_ref, g_ref, w_ref, p_ref = refs
        x = _pick_segment(c_ref, x_ref)
    mod = mod_ref[0, 0]
    h = _rms(x) * g_ref[...] * (1.0 + mod[1:2, :]) + mod[0:1, :]
    p_ref[0] = _dot(h, w_ref[...])


def _seg_map(b, i):
    return (b, jnp.minimum(i, 1), 0, 0)


def _segment_specs(d):
    return [pl.BlockSpec((1, TM, d), lambda b, i: (b, 0, 0)),
            pl.BlockSpec((1, TM, d), lambda b, i: (b, jnp.maximum(i - 1, 0), 0))]


def _pick_segment(c_ref, x_ref):
    return jnp.where(pl.program_id(1) == 0, c_ref[0], x_ref[0])


def _inproj(x, mod, g, w, prev=None):
    if prev is None:
        ctx, lat = x
        bsz, S, d = lat.shape
        L = TM + S
    else:
        bsz, L, d = x.shape
    n = w.shape[1]
    nt = L // TM
    tok = pl.BlockSpec((1, TM, d), lambda b, i: (b, i, 0))
    modspec = pl.BlockSpec((1, 1, 6, d), _seg_map)
    tail = [modspec, pl.BlockSpec((1, d), lambda b, i: (0, 0)), pl.BlockSpec((d, n), lambda b, i: (0, 0))]
    pspec = pl.BlockSpec((1, TM, n), lambda b, i: (b, i, 0))
    pshape = jax.ShapeDtypeStruct((bsz, L, n), F32)
    if prev is None:
        return pl.pallas_call(
            functools.partial(_inproj_kernel, fuse_prev=False),
            grid=(bsz, nt), in_specs=_segment_specs(d) + tail, out_specs=pspec, out_shape=pshape,
            compiler_params=_cp(("parallel", "arbitrary")), name="inproj",
        )(ctx, lat, mod, g.reshape(1, d), w)
    moe_out, pmod = prev
    return pl.pallas_call(
        functools.partial(_inproj_kernel, fuse_prev=True),
        grid=(bsz, nt), in_specs=[tok, tok, modspec] + tail,
        out_specs=[tok, pspec], out_shape=[jax.ShapeDtypeStruct(x.shape, F32), pshape],
        compiler_params=_cp(("parallel", "parallel")), name="inproj_res",
    )(x, moe_out, pmod, mod, g.reshape(1, d), w)


def _conv_tile(ref, i, nt, cw, cb, width=TM):
    L = nt * width
    t0 = pl.multiple_of(i * width, width)
    cur = ref[0, pl.ds(t0, width), :]
    prev = ref[0, pl.ds(pl.multiple_of(jnp.maximum(t0 - 8, 0), 8), 8), :]
    nxt = ref[0, pl.ds(pl.multiple_of(jnp.minimum(t0 + width, L - 8), 8), 8), :]
    return _conv_vals(cur, prev, nxt, i, nt, cw, cb, width, first_lat=TM // width)


def _conv_vals(cur, prev, nxt, i, nt, cw, cb, width, first_lat):
    prev = jnp.where((i != 0) & (i != first_lat), prev, 0.0)
    nxt = jnp.where((i != first_lat - 1) & (i != nt - 1), nxt, 0.0)
    cat = jnp.concatenate([prev, cur, nxt], axis=0)
    return (cw[0:1] * cat[6:6 + width] + cw[1:2] * cat[7:7 + width] + cw[2:3] * cur
            + cw[3:4] * cat[9:9 + width] + cb)


def _lru_kernel(ux_ref, cw_ref, cb_ref, gw_ref, gb_ref, lam_ref, o_ref, *, nt):
    cw = cw_ref[...]
    cb = cb_ref[...]
    lam = lam_ref[0]
    sp = _softplus(-lam)
    row = lax.broadcasted_iota(jnp.int32, (TM, 128), 0) & 7

    def gates(i, d):
        xl = _conv_tile(ux_ref, i, nt, cw, cb)
        g = _dot(xl, gw_ref[0, d]) + gb_ref[0, d]
        r = _sigmoid(g[:, :128])
        ig = _sigmoid(g[:, 128:])
        log_a = -LRU_C * r * sp[d:d + 1]
        a = jnp.exp(log_a)
        u = jnp.sqrt(1.0 - a * a) * (ig * xl)
        return a, u

    def scan_tile(i, d, carry, accumulate):
        a, u = gates(i, d)
        rev = d == 1
        for k in (1, 2, 4):
            sh = TM - k if rev else k
            ok = (row < 8 - k) if rev else (row >= k)
            ash = pltpu.roll(a, sh, 0)
            ush = pltpu.roll(u, sh, 0)
            u = jnp.where(ok, u + a * ush, u)
            a = jnp.where(ok, a * ash, a)
        t0 = i * TM
        groups = range(TM // 8)
        for s in (reversed(groups) if rev else groups):
            h = u[s * 8:(s + 1) * 8] + a[s * 8:(s + 1) * 8] * carry
            carry = h[0:1] if rev else h[7:8]
            idx = pl.ds(pl.multiple_of(t0 + s * 8, 8), 8)
            if accumulate:
                o_ref[0, idx, :] += h
            else:
                o_ref[0, idx, :] = h
        return carry

    zero = jnp.zeros((1, 128), F32)
    lax.fori_loop(0, nt, lambda i, c: scan_tile(i, 0, c, False), zero)
    lax.fori_loop(0, nt, lambda j, c: scan_tile(jnp.where(j == 0, 0, nt - j), 1, c, True), zero)


def _lru(proj, conv_w, conv_b, wa, ba, wx, bx, lam):
    bsz, L, _ = proj.shape
    nt = L // TM
    width = conv_w.shape[1]
    ng = width // 128

    def blockdiag(w):
        w = w.reshape(2, ng, 2, 64, 64)
        z = jnp.zeros_like(w[:, :, 0])
        top = jnp.concatenate([w[:, :, 0], z], axis=-1)
        bot = jnp.concatenate([z, w[:, :, 1]], axis=-1)
        return jnp.concatenate([top, bot], axis=-2)
    gw = jnp.concatenate([blockdiag(wa), blockdiag(wx)], axis=-1).transpose(1, 0, 2, 3).astype(BF16)
    gb = jnp.concatenate([ba.reshape(2, ng, 1, 128), bx.reshape(2, ng, 1, 128)], axis=-1).transpose(1, 0, 2, 3)
    lam_g = lam.reshape(2, ng, 128).transpose(1, 0, 2)
    return pl.pallas_call(
        functools.partial(_lru_kernel, nt=nt),
        grid=(bsz, ng),
        in_specs=[pl.BlockSpec((1, L, 128), lambda b, c: (b, 0, c)),
                  pl.BlockSpec((CONV_W, 128), lambda b, c: (0, c)),
                  pl.BlockSpec((1, 128), lambda b, c: (0, c)),
                  pl.BlockSpec((1, 2, 128, 256), lambda b, c: (c, 0, 0, 0)),
                  pl.BlockSpec((1, 2, 1, 256), lambda b, c: (c, 0, 0, 0)),
                  pl.BlockSpec((1, 2, 128), lambda b, c: (c, 0, 0))],
        out_specs=pl.BlockSpec((1, L, 128), lambda b, c: (b, 0, c)),
        out_shape=jax.ShapeDtypeStruct((bsz, L, width), F32),
        compiler_params=_cp(("parallel", "parallel")), name="rglru",
    )(proj, conv_w, conv_b.reshape(1, width), gw, gb, lam_g)


def _na_bias_table(rpb, rows):
    nh = rpb.shape[0]
    win_r = (rpb.shape[1] + 1) // 2
    win_c = (rpb.shape[2] + 1) // 2
    qr = np.arange(NA_RQ)[:, None, None, None]
    qc = np.arange(GRID_W)[None, :, None, None]
    kr = np.arange(NA_RK)[None, None, :, None]
    kc = np.arange(GRID_W)[None, None, None, :]
    cstart = np.clip(qc - win_c // 2, 0, GRID_W - win_c)
    col_ok = (kc >= cstart) & (kc < cstart + win_c)
    dcol = np.clip(kc - qc + (win_c - 1), 0, 2 * win_c - 2)
    oc = (np.arange(2 * win_c - 1)[:, None, None] == dcol[0, :, 0, :][None]).astype(np.float32)
    o_rows, oks = [], []
    for r0, w0 in ((0, 0), (2 * NA_RQ, NA_RQ), (rows - NA_RQ, rows - NA_RK)):
        r = r0 + qr
        kabs = w0 + kr
        rstart = np.clip(r - win_r // 2, 0, rows - win_r)
        oks.append(np.broadcast_to((kabs >= rstart) & (kabs < rstart + win_r) & col_ok,
                                   (NA_RQ, GRID_W, NA_RK, GRID_W)))
        drow = np.clip(kabs - r + (win_r - 1), 0, 2 * win_r - 2)[:, 0, :, 0]
        o_rows.append((np.arange(2 * win_r - 1)[:, None, None] == drow[None]).astype(np.float32))
    t1 = jnp.einsum('hrc,prab->phabc', rpb.astype(F32), jnp.asarray(np.stack(o_rows)), precision=HI)
    b = jnp.einsum('phabc,cqk->phaqbk', t1, jnp.asarray(oc), precision=HI)
    b = jnp.where(jnp.asarray(np.stack(oks))[:, None], b, NEG)
    return b.reshape(3, nh, NA_RQ * GRID_W, NA_RK * GRID_W)


def _pair_rms(x, lo):
    x2 = x * x
    s0 = jnp.sum(jnp.where(lo, x2, 0.0), axis=-1, keepdims=True)
    s1 = jnp.sum(jnp.where(lo, 0.0, x2), axis=-1, keepdims=True)
    return x * lax.rsqrt(jnp.where(lo, s0, s1) * (2.0 / x.shape[-1]) + EPS)


def _na_kernel(q_ref, k_ref, v_ref, bias_ref, qg_ref, kg_ref, o_ref, kn_s, vb_s, *, rows, hd):
    i = pl.program_id(2)
    L = k_ref.shape[1]
    nkeys = NA_RK * GRID_W
    lo = lax.broadcasted_iota(jnp.int32, (1, 2 * hd), 1) < hd

    @pl.when(i == 0)
    def _prep():
        def body(t, _):
            idx = pl.ds(pl.multiple_of(t * TM, TM), TM)
            kn_s[idx, :] = (_pair_rms(k_ref[0, idx, :], lo) * kg_ref[...]).astype(BF16)
            vb_s[idx, :] = v_ref[0, idx, :].astype(BF16)
            return 0
        lax.fori_loop(0, L // TM, body, 0)

    qn = _pair_rms(q_ref[0], lo) * (qg_ref[...] * hd ** -0.5)
    q_h = [jnp.where(lo, qn, 0.0).astype(BF16), jnp.where(lo, 0.0, qn).astype(BF16)]
    kctx = kn_s[0:TM, :]
    vctx = vb_s[0:TM, :]

    def head(hh, kwin, vwin):
        s_c = _dot_nt(q_h[hh], kctx)
        if kwin is not None:
            s_w = _dot_nt(q_h[hh], kwin) + bias_ref[0, hh]
        yield
        m = jnp.max(s_c, axis=-1, keepdims=True)
        if kwin is not None:
            m = jnp.maximum(m, jnp.max(s_w, axis=-1, keepdims=True))
            p_w = jnp.exp(s_w - m)
        p_c = jnp.exp(s_c - m)
        den = jnp.sum(p_c, axis=-1, keepdims=True)
        num = _dot(p_c, vctx)
        if kwin is not None:
            den = den + jnp.sum(p_w, axis=-1, keepdims=True)
            num = num + _dot(p_w, vwin)
        yield
        return num / den

    def both(kwin, vwin):
        o0, o1 = _interleave([head(0, kwin, vwin), head(1, kwin, vwin)])
        o_ref[0] = jnp.where(lo, o0, o1)

    @pl.when(i == 0)
    def _():
        both(None, None)

    @pl.when(i > 0)
    def _():
        r0 = (i - 1) * NA_RQ
        w0 = jnp.clip(r0 - NA_RQ, 0, rows - NA_RK)
        start = pl.multiple_of(TM + w0 * GRID_W, GRID_W)
        both(kn_s[pl.ds(start, nkeys), :], vb_s[pl.ds(start, nkeys), :])


def _na(proj, q_g, k_g, rpb, col0):
    bsz, L, _ = proj.shape
    nh = rpb.shape[0]
    hd = q_g.shape[0]
    width = nh * hd
    rows = (L - TM) // GRID_W
    nb = L // TM
    bias = _na_bias_table(rpb, rows)
    qb, kb, vb = col0 // 128, (col0 + width) // 128, (col0 + 2 * width) // 128

    def pat(i):
        return jnp.where(i <= 1, 0, jnp.where(i == nb - 1, 2, 1))
    return pl.pallas_call(
        functools.partial(_na_kernel, rows=rows, hd=hd),
        grid=(bsz, width // 128, nb),
        in_specs=[pl.BlockSpec((1, TM, 128), lambda b, h, i: (b, i, qb + h)),
                  pl.BlockSpec((1, L, 128), lambda b, h, i: (b, 0, kb + h)),
                  pl.BlockSpec((1, L, 128), lambda b, h, i: (b, 0, vb + h)),
                  pl.BlockSpec((1, 2, TM, NA_RK * GRID_W), lambda b, h, i: (pat(i), h, 0, 0)),
                  pl.BlockSpec((1, 2 * hd), lambda b, h, i: (0, 0)),
                  pl.BlockSpec((1, 2 * hd), lambda b, h, i: (0, 0))],
        out_specs=pl.BlockSpec((1, TM, 128), lambda b, h, i: (b, i, h)),
        out_shape=jax.ShapeDtypeStruct((bsz, L, width), F32),
        scratch_shapes=[pltpu.VMEM((L, 128), BF16), pltpu.VMEM((L, 128), BF16)],
        compiler_params=_cp(("parallel", "parallel", "arbitrary")), name="nbr_attn",
    )(proj, proj, proj, bias, jnp.tile(q_g, 2).reshape(1, 2 * hd), jnp.tile(k_g, 2).reshape(1, 2 * hd))


def _route(lg):
    lane = lax.broadcasted_iota(jnp.int32, lg.shape, 1)
    lane_f = lane.astype(F32)
    is_g = lane < EXPERT_LANE0
    gl = jnp.where(is_g, lg, NEG)
    gmax = jnp.max(gl, axis=-1, keepdims=True)
    gsel = jnp.min(jnp.where(is_g & (gl == gmax), lane_f, 1e9), axis=-1, keepdims=True)
    g_w = 1.0 / jnp.sum(jnp.where(is_g, jnp.exp(gl - gmax), 0.0), axis=-1, keepdims=True)
    grp = ((lane - EXPERT_LANE0) >> 2).astype(F32)
    in_g = (lane >= EXPERT_LANE0) & (lane < EXPERT_LANE0 + N_EXPERTS) & (grp == gsel)
    el = jnp.where(in_g, lg, NEG)
    v1 = jnp.max(el, axis=-1, keepdims=True)
    i1 = jnp.min(jnp.where(in_g & (el == v1), lane_f, 1e9), axis=-1, keepdims=True)
    el2 = jnp.where(lane_f == i1, NEG, el)
    v2 = jnp.max(el2, axis=-1, keepdims=True)
    i2 = jnp.min(jnp.where(in_g & (lane_f != i1) & (el2 == v2), lane_f, 1e9), axis=-1, keepdims=True)
    t = jnp.exp(v2 - v1)
    w1 = g_w / (1.0 + t)
    w2 = g_w * t / (1.0 + t)
    return (jnp.where(lane_f == i1, w1, 0.0) + jnp.where(lane_f == i2, w2, 0.0)
            + jnp.where(lane_f == gsel, 1.0, 0.0))


def _outproj_kernel(*refs, even):
    if even:
        (r_ref, ug_ref, a_ref, c_ref, x_ref, w_ref, mod_ref, g2_ref, rw_ref, x1_ref, h2_ref, gate_ref) = refs
        is_ctx = pl.program_id(1) == 0
        x_rows = lambda rs: jnp.where(is_ctx, c_ref[0, rs, :], x_ref[0, rs, :])
    else:
        (ys_ref, z_ref, hm_ref, mo_ref, sg_ref, mg_ref, x_ref, w_ref, mod_ref, g2_ref, rw_ref,
         x1_ref, h2_ref, gate_ref) = refs
        x_rows = lambda rs: x_ref[0, rs, :]
    mod = mod_ref[0, 0]

    def part(rs):
        if even:
            y_in = jnp.concatenate([r_ref[0, rs, :] * _gelu_tanh(ug_ref[0, rs, :]), a_ref[0, rs, :]], axis=-1)
        else:
            ys = ys_ref[0, rs, :] * _silu(z_ref[0, rs, :])
            sg = sg_ref[...]
            mg = mg_ref[...]
            hm = hm_ref[0, rs, :]
            sig_o = _sigmoid(mo_ref[0, rs, :])
            gw = ys.shape[-1] // 2
            parts = [_rms(ys[:, g * gw:(g + 1) * gw]) * sg[:, g * gw:(g + 1) * gw] for g in range(2)]
            hw = 128
            parts += [_rms(hm[:, h * hw:(h + 1) * hw]) * mg[:, h * hw:(h + 1) * hw] * sig_o[:, h * hw:(h + 1) * hw]
                      for h in range(hm.shape[-1] // hw)]
            y_in = jnp.concatenate(parts, axis=-1)
        y = _dot(y_in, w_ref[...])
        yield
        x1 = x_rows(rs) + mod[2:3, :] * y
        x1_ref[0, rs, :] = x1
        h2 = _rms(x1) * g2_ref[...] * (1.0 + mod[4:5, :]) + mod[3:4, :]
        h2_ref[0, rs, :] = h2.astype(BF16)
        hi = h2.astype(BF16)
        lo = (h2 - hi.astype(F32)).astype(BF16)
        lg2 = jnp.dot(hi, rw_ref[...], preferred_element_type=F32)
        lg1 = jnp.dot(lo, rw_ref[:, 0:128], preferred_element_type=F32)
        yield
        gate_ref[0, rs, :] = _route(lg2[:, 0:128] + lg2[:, 128:256] + lg1)

    nparts = 2
    rows = TM // nparts
    _interleave([part(pl.ds(p * rows, rows)) for p in range(nparts)])


def _outproj(mix_inputs, x, w, mod, g2, rw, even, tile0, ntiles):
    xs = list(x) if even else [x]
    bsz, _, d = xs[-1].shape
    specs, args = [], []
    for arr, cb, wdt in mix_inputs:
        if arr.ndim == 3:
            specs.append(pl.BlockSpec((1, TM, wdt), lambda b, i, cb=cb: (b, i + tile0, cb)))
        else:
            specs.append(pl.BlockSpec((1, wdt), lambda b, i: (0, 0)))
        args.append(arr)
    specs += _segment_specs(d) if even else [pl.BlockSpec((1, TM, d), lambda b, i: (b, i + tile0, 0))]
    specs += [pl.BlockSpec(w.shape, lambda b, i: (0, 0)),
              pl.BlockSpec((1, 1, 6, d), lambda b, i: (b, jnp.minimum(i + tile0, 1), 0, 0)),
              pl.BlockSpec((1, d), lambda b, i: (0, 0)),
              pl.BlockSpec(rw.shape, lambda b, i: (0, 0))]
    args += xs + [w, mod, g2.reshape(1, d), rw]
    lo = ntiles * TM
    return pl.pallas_call(
        functools.partial(_outproj_kernel, even=even),
        grid=(bsz, ntiles), in_specs=specs,
        out_specs=[pl.BlockSpec((1, TM, d), lambda b, i: (b, i, 0)),
                   pl.BlockSpec((1, TM, d), lambda b, i: (b, i, 0)),
                   pl.BlockSpec((1, TM, 128), lambda b, i: (b, i, 0))],
        out_shape=[jax.ShapeDtypeStruct((bsz, lo, d), F32), jax.ShapeDtypeStruct((bsz, lo, d), BF16),
                   jax.ShapeDtypeStruct((bsz, lo, 128), F32)],
        compiler_params=_cp(("parallel", "parallel")), name="outproj_even" if even else "outproj_odd",
    )(*args)


MOE_CH = 128
MOE_EPS = 2
GROUP_SIZE = 4
N_GROUPS = N_EXPERTS // GROUP_SIZE


def _moe_kernel(*refs, residual):
    if residual:
        x_ref, g_ref, tri_ref, w1_ref, w3_ref, w2_ref, x1_ref, mod_ref, o_ref, xs_s, ys_s, gs_s, pt_s, plan_s = refs
    else:
        x_ref, g_ref, tri_ref, w1_ref, w3_ref, w2_ref, o_ref, xs_s, ys_s, gs_s, pt_s, plan_s = refs
    e = pl.program_id(1)
    tb = x_ref.shape[0]
    nch = xs_s.shape[0] // MOE_CH

    @pl.when(e == 0)
    def _plan():
        g = g_ref[...]
        lane = lax.broadcasted_iota(jnp.int32, g.shape, 1)
        oh = jnp.where(lane < N_GROUPS, g, 0.0)
        rank = jnp.dot(tri_ref[...], oh.astype(BF16), preferred_element_type=F32)
        cnt = jnp.sum(oh, axis=0, keepdims=True)
        lane1 = lax.broadcasted_iota(jnp.int32, (1, 128), 1)
        off = jnp.int32(0)
        offv = jnp.zeros((1, 128), F32)
        for gi in range(N_GROUPS):
            n = jnp.sum(jnp.where(lane1 == gi, cnt, 0.0)).astype(jnp.int32)
            nchunks = (n + (MOE_CH - 1)) // MOE_CH
            plan_s[gi] = off // MOE_CH
            plan_s[N_GROUPS + gi] = nchunks
            offv = offv + jnp.where(lane1 == gi, off.astype(F32), 0.0)
            off = off + nchunks * MOE_CH
        pos_col = jnp.sum(oh * (rank + offv), axis=1, keepdims=True)
        posb = jnp.broadcast_to(pos_col, (tb, 128))
        pos_row = jnp.concatenate([posb[i * 128:(i + 1) * 128, :].T[0:1, :] for i in range(tb // 128)], axis=1)
        x = x_ref[...]
        g_hi = g.astype(BF16)
        g_lo = (g - g_hi.astype(F32)).astype(BF16)
        lane_c = lax.broadcasted_iota(jnp.int32, (tb, MOE_CH), 1).astype(F32)
        row_c = lax.broadcasted_iota(jnp.int32, (MOE_CH, tb), 0).astype(F32)
        for c in range(nch):
            sl = slice(c * MOE_CH, (c + 1) * MOE_CH)
            pt_s[:, sl] = jnp.where(pos_col == lane_c + float(c * MOE_CH), 1.0, 0.0).astype(BF16)
            p = jnp.where(row_c + float(c * MOE_CH) == pos_row, 1.0, 0.0).astype(BF16)
            xs_s[sl, :] = jnp.dot(p, x, preferred_element_type=F32).astype(BF16)
            gs_s[sl, :] = (jnp.dot(p, g_hi, preferred_element_type=F32)
                           + jnp.dot(p, g_lo, preferred_element_type=F32))
        ys_s[...] = jnp.zeros_like(ys_s)

    grp = (e * MOE_EPS) // GROUP_SIZE
    c0 = plan_s[grp]
    nchunks = plan_s[N_GROUPS + grp]

    def ffn(chunk0, nrows):
        rows = pl.ds(pl.multiple_of(chunk0 * MOE_CH, MOE_CH), nrows)
        xs = xs_s[rows, :]
        gs = gs_s[rows, :]
        lane_g = lax.broadcasted_iota(jnp.int32, (nrows, 128), 1)
        acc = None
        for k in range(MOE_EPS):
            a = jnp.dot(xs, w1_ref[0, k], preferred_element_type=F32)
            b = jnp.dot(xs, w3_ref[0, k], preferred_element_type=F32)
            y = _dot(_silu(a) * b, w2_ref[0, k])
            ge = jnp.sum(jnp.where(lane_g == e * MOE_EPS + k + EXPERT_LANE0, gs, 0.0), axis=-1, keepdims=True)
            acc = ge * y if acc is None else acc + ge * y
        ys_s[rows, :] += acc

    def pair(p, _):
        ffn(c0 + 2 * p, 2 * MOE_CH)
        return 0
    lax.fori_loop(0, nchunks // 2, pair, 0)

    @pl.when(nchunks % 2 == 1)
    def _():
        ffn(c0 + nchunks - 1, MOE_CH)

    @pl.when(e == N_EXPERTS // MOE_EPS - 1)
    def _combine():
        out = jnp.dot(pt_s[...], ys_s[...].astype(BF16), preferred_element_type=F32)
        if residual:
            out = x1_ref[...] + mod_ref[0, 0][5:6, :] * out
        o_ref[...] = out


def _moe(h2, gate, layer, w1, w3, w2, resid=None):
    t, d = h2.shape
    _, ne, _, ff = w1.shape
    tb = math.gcd(t, 1024)
    npad = tb + N_GROUPS * MOE_CH
    tri = jnp.asarray(np.tril(np.ones((tb, tb), np.float32), -1), BF16)
    extra_specs, extra_args = [], []
    if resid is not None:
        x1, mod = resid
        per_batch = t // mod.shape[0] // tb
        extra_specs = [pl.BlockSpec((tb, d), lambda i, e: (i, 0)),
                       pl.BlockSpec((1, 1, 6, d), lambda i, e: (i // per_batch, 1, 0, 0))]
        extra_args = [x1, mod]
    return pl.pallas_call(
        functools.partial(_moe_kernel, residual=resid is not None),
        grid=(t // tb, ne // MOE_EPS),
        in_specs=[pl.BlockSpec((tb, d), lambda i, e: (i, 0)),
                  pl.BlockSpec((tb, 128), lambda i, e: (i, 0)),
                  pl.BlockSpec((tb, tb), lambda i, e: (0, 0)),
                  pl.BlockSpec((1, MOE_EPS, d, ff), lambda i, e: (layer, e, 0, 0)),
                  pl.BlockSpec((1, MOE_EPS, d, ff), lambda i, e: (layer, e, 0, 0)),
                  pl.BlockSpec((1, MOE_EPS, ff, d), lambda i, e: (layer, e, 0, 0))] + extra_specs,
        out_specs=pl.BlockSpec((tb, d), lambda i, e: (i, 0)),
        out_shape=jax.ShapeDtypeStruct((t, d), F32),
        scratch_shapes=[pltpu.VMEM((npad, d), BF16), pltpu.VMEM((npad, d), F32), pltpu.VMEM((npad, 128), F32),
                        pltpu.VMEM((tb, npad), BF16), pltpu.SMEM((2 * N_GROUPS,), jnp.int32)],
        compiler_params=_cp(("parallel", "arbitrary")), name="moe",
    )(h2, gate, tri, w1, w3, w2, *extra_args)


def _tri(q, rev):
    r = lax.broadcasted_iota(jnp.int32, (q, q), 0)
    c = lax.broadcasted_iota(jnp.int32, (q, q), 1)
    return (c >= r) if rev else (c <= r)


def _ssd_kernel(xs_ref, b_ref, c_ref, dt_ref, cwx_ref, cbx_ref, cwb_ref, cbb_ref, cwc_ref, cbc_ref,
                dtb_ref, alog_ref, dsk_ref, sel_ref, o_ref, xc_s, bc_s, cc_s, dt_s, bct_s, *, nc):
    q = CHUNK
    first_lat = TM // q

    dtb = dtb_ref[...]

    def prep(i, _):
        t0 = pl.multiple_of(i * q, q)
        idx = pl.ds(t0, q)
        for src, cw, cb, dst in ((xs_ref, cwx_ref, cbx_ref, xc_s), (b_ref, cwb_ref, cbb_ref, bc_s),
                                 (c_ref, cwc_ref, cbc_ref, cc_s)):
            L = nc * q
            cur = src[0, idx, :]
            prev = src[0, pl.ds(pl.multiple_of(jnp.maximum(t0 - 8, 0), 8), 8), :]
            nxt = src[0, pl.ds(pl.multiple_of(jnp.minimum(t0 + q, L - 8), 8), 8), :]
            dst[idx, :] = _silu(_conv_vals(cur, prev, nxt, i, nc, cw[...], cb[...], q, first_lat))
        dt_s[idx, :] = _softplus(dt_ref[0, idx, :] + dtb)
        o_ref[0, idx, :] = dsk_ref[...] * xc_s[idx, :]
        bct_s[i] = bc_s[idx, :].T.astype(BF16)
        return 0
    lax.fori_loop(0, nc, prep, 0)

    a_lane = jnp.broadcast_to(-jnp.exp(alog_ref[...]), (8, 128))
    a_e = [_dot_hi(a_lane, sel_ref[d, 0])[0:1, :] for d in (0, 1)]
    lane256 = lax.broadcasted_iota(jnp.int32, (q, 256), 1) // 64
    r8 = lax.broadcasted_iota(jnp.int32, (8, 256), 0)
    l8 = lax.broadcasted_iota(jnp.int32, (8, 256), 1)
    head_rows = (l8 == r8 * 64).astype(F32)

    def chain(d, j, h):
        rev = d == 1
        causal = _tri(q, rev)
        tri = causal.astype(F32)
        sel = sel_ref[d, 0]
        last = 0 if rev else q - 1
        if rev:
            ci = jnp.where(j < first_lat, first_lat - 1 - j, nc - 1 - (j - first_lat))
        else:
            ci = j
        idx = pl.ds(pl.multiple_of(ci * q, q), q)
        dt_e = _dot_sel_r(dt_s[idx, :], sel)
        xc = xc_s[idx, :]
        bc = bc_s[idx, :]
        cc = cc_s[idx, :]
        cb = _dot_nt(cc, bc)
        y_off = _dot(cc, h)
        yield
        cum_e = _dot_sel(tri, dt_e * a_e[d])
        dtx = xc * dt_e
        yield
        cum_rows = _dot_sel(head_rows, cum_e, _NT)
        tot_e = cum_e[last:last + 1, :]
        states = _dot(bct_s[ci], dtx * jnp.exp(tot_e - cum_e))
        yield
        y = jnp.exp(cum_e) * y_off
        for r in range(4):
            seg = cum_e[:, r * 64:r * 64 + 1] - cum_rows[r:r + 1, :]
            m = cb * jnp.exp(jnp.where(causal, seg, NEG))
            y = y + _dot(m, jnp.where(lane256 == r, dtx, 0.0))
            yield
        o_ref[0, idx, :] += y
        return jnp.exp(tot_e) * h + states

    def chunk(j, hs):
        return tuple(_interleave([chain(0, j, hs[0]), chain(1, j, hs[1])]))
    h0 = jnp.zeros((128, 256), F32)
    lax.fori_loop(0, nc, chunk, (h0, h0))


def _ssd(proj, conv_w, conv_b, dt_bias, a_log, d_skip, cols):
    bsz, L, _ = proj.shape
    nc = L // CHUNK
    cx, cbm, ccm, cdt = cols
    nh = d_skip.shape[0]
    dtb = jnp.zeros((1, 128), F32).at[0, :2 * nh].set(dt_bias.reshape(-1))
    alog = jnp.zeros((1, 128), F32).at[0, :2 * nh].set(a_log.reshape(-1))
    dsk = jnp.repeat(d_skip.astype(F32), 64).reshape(1, nh * 64)
    sel = np.zeros((2, 2, 128, 256), np.float32)
    for d in range(2):
        for g in range(2):
            for r in range(4):
                sel[d, g, d * nh + g * 4 + r, r * 64:(r + 1) * 64] = 1.0
    cw_specs = []
    for width, off in ((256, 0), (128, 512), (128, 768)):
        cw_specs += [pl.BlockSpec((CONV_W, width), lambda b, g, off=off, width=width: (0, off // width + g)),
                     pl.BlockSpec((1, width), lambda b, g, off=off, width=width: (0, off // width + g))]
    return pl.pallas_call(
        functools.partial(_ssd_kernel, nc=nc),
        grid=(bsz, 2),
        in_specs=[pl.BlockSpec((1, L, 256), lambda b, g: (b, 0, cx // 256 + g)),
                  pl.BlockSpec((1, L, 128), lambda b, g: (b, 0, cbm // 128 + g)),
                  pl.BlockSpec((1, L, 128), lambda b, g: (b, 0, ccm // 128 + g)),
                  pl.BlockSpec((1, L, 128), lambda b, g: (b, 0, cdt // 128))] + cw_specs + [
                  pl.BlockSpec((1, 128), lambda b, g: (0, 0)),
                  pl.BlockSpec((1, 128), lambda b, g: (0, 0)),
                  pl.BlockSpec((1, 256), lambda b, g: (0, g)),
                  pl.BlockSpec((2, 1, 128, 256), lambda b, g: (0, g, 0, 0))],
        out_specs=pl.BlockSpec((1, L, 256), lambda b, g: (b, 0, g)),
        out_shape=jax.ShapeDtypeStruct((bsz, L, nh * 64), F32),
        scratch_shapes=[pltpu.VMEM((L, 256), F32), pltpu.VMEM((L, 128), F32), pltpu.VMEM((L, 128), F32),
                        pltpu.VMEM((L, 128), F32), pltpu.VMEM((nc, 128, CHUNK), BF16)],
        compiler_params=_cp(("parallel", "parallel")), name="ssd",
    )(proj, proj, proj, proj, conv_w, conv_b.reshape(1, -1), conv_w, conv_b.reshape(1, -1),
      conv_w, conv_b.reshape(1, -1), dtb, alog, dsk, jnp.asarray(sel))


ML_HP = 2


def _ml_gate_lanes(hh, d):
    li = (hh * 2 + d) * 2
    return li, li + 1


def _ml_gate_columns(nh):
    cols = []
    for hp in range(nh // ML_HP):
        blk = [None] * (4 * ML_HP)
        for hh in range(ML_HP):
            for d in range(2):
                for t, lane in enumerate(_ml_gate_lanes(hh, d)):
                    blk[lane] = (d * 2 + t) * nh + hp * ML_HP + hh
        cols.append(blk)
    return cols


def _mlstm_kernel(q_ref, k_ref, v_ref, g_ref, cwq_ref, cbq_ref, cwk_ref, cbk_ref, gb_ref, o_ref,
                  qc_s, kc_s, kct_s, *, nc, dh):
    q = CHUNK
    first_lat = TM // q

    def prep(i, _):
        t0 = pl.multiple_of(i * q, q)
        idx = pl.ds(t0, q)
        L = nc * q
        for src, cw, cb, dst, mul in ((q_ref, cwq_ref, cbq_ref, qc_s, 1.0), (k_ref, cwk_ref, cbk_ref, kc_s, dh ** -0.5)):
            cur = src[0, idx, :]
            prev = src[0, pl.ds(pl.multiple_of(jnp.maximum(t0 - 8, 0), 8), 8), :]
            nxt = src[0, pl.ds(pl.multiple_of(jnp.minimum(t0 + q, L - 8), 8), 8), :]
            dst[idx, :] = _silu(_conv_vals(cur, prev, nxt, i, nc, cw[...], cb[...], q, first_lat)) * mul
        for hh in range(ML_HP):
            kct_s[i, hh * dh:(hh + 1) * dh, :] = kc_s[idx, hh * dh:(hh + 1) * dh].T.astype(BF16)
        return 0
    lax.fori_loop(0, nc, prep, 0)

    gb = gb_ref[0]
    lane = lax.broadcasted_iota(jnp.int32, (q, 128), 1)
    eye8 = (lax.broadcasted_iota(jnp.int32, (8, 128), 0) == lax.broadcasted_iota(jnp.int32, (8, 128), 1)).astype(F32)
    o_ref[...] = jnp.zeros_like(o_ref)
    heads = range(ML_HP)

    def direction(d, j, states):
        rev = d == 1
        causal = _tri(q, rev)
        last = 0 if rev else q - 1
        if rev:
            ci = jnp.where(j < first_lat, first_lat - 1 - j, nc - 1 - (j - first_lat))
        else:
            ci = j
        idx = pl.ds(pl.multiple_of(ci * q, q), q)
        g = g_ref[0, idx, :] + gb
        logf = jnp.minimum(g, 0.0) - jnp.log(1.0 + jnp.exp(-jnp.abs(g)))
        is_f = functools.reduce(jnp.logical_or, [lane == _ml_gate_lanes(hh, d)[1] for hh in heads])
        cum = _dot_sel(jnp.where(causal, 1.0, 0.0), jnp.where(is_f, logf, 0.0))
        qc = [qc_s[idx, hh * dh:(hh + 1) * dh] for hh in heads]
        kc = [kc_s[idx, hh * dh:(hh + 1) * dh] for hh in heads]
        vc = [v_ref[0, idx, hh * dh:(hh + 1) * dh] for hh in heads]
        kct = [kct_s[ci, hh * dh:(hh + 1) * dh, :] for hh in heads]
        qk = [_dot(qc[hh], kct[hh]) for hh in heads]
        inter_c = [_dot(qc[hh], states[hh][0]) for hh in heads]
        yield
        rows = _dot_sel(eye8, jnp.where(is_f, cum, g), _NT)
        yield
        w, w_s, g_in, m_t, keep, m_new, wv, upd = [], [], [], [], [], [], [], []
        for hh in heads:
            li, lf = _ml_gate_lanes(hh, d)
            m_prev = states[hh][2]
            i_col, i_row = g[:, li:li + 1], rows[li:li + 1, :]
            b_col, b_row = cum[:, lf:lf + 1], rows[lf:lf + 1, :]
            dlog = jnp.where(causal, b_col - b_row + i_row, NEG)
            inter = b_col + m_prev
            m_t.append(jnp.maximum(jnp.max(dlog, axis=-1, keepdims=True), inter))
            w.append(jnp.exp(dlog - m_t[hh]) * qk[hh])
            g_in.append(jnp.exp(inter - m_t[hh]))
            b_end = b_col[last:last + 1, :]
            g_s = b_end - b_col + i_col
            m_new.append(jnp.maximum(jnp.max(g_s, axis=0, keepdims=True), b_end + m_prev))
            w_s.append(jnp.exp(g_s - m_new[hh]))
            keep.append(jnp.exp(b_end + m_prev - m_new[hh]))
            wv.append(_dot(w[hh], vc[hh]))
            upd.append(_dot(kct[hh], w_s[hh] * vc[hh]))
        yield
        new_states = []
        for hh in heads:
            c_prev, n_prev, _ = states[hh]
            num = wv[hh] + g_in[hh] * inter_c[hh]
            den = (jnp.sum(w[hh], axis=-1, keepdims=True)
                   + g_in[hh] * jnp.sum(qc[hh] * n_prev, axis=-1, keepdims=True))
            o_ref[0, idx, hh * dh:(hh + 1) * dh] += num / jnp.maximum(jnp.abs(den), jnp.exp(-m_t[hh]))
            new_states.append((keep[hh] * c_prev + upd[hh],
                               keep[hh] * n_prev + jnp.sum(w_s[hh] * kc[hh], axis=0, keepdims=True), m_new[hh]))
        return tuple(new_states)

    def chunk(j, states):
        return tuple(_interleave([direction(d, j, states[d]) for d in (0, 1)]))
    init = (jnp.zeros((dh, dh), F32), jnp.zeros((1, dh), F32), jnp.full((1, 1), NEG, F32))
    lax.fori_loop(0, nc, chunk, tuple(tuple(init for _ in heads) for _ in (0, 1)))


def _mlstm(proj, conv_w, conv_b, gate_b, cols):
    bsz, L, _ = proj.shape
    nc = L // CHUNK
    cq, ck, cv, cg = cols
    nh = gate_b.shape[-1]
    dh = conv_w.shape[1] // (2 * nh)
    bw = ML_HP * dh
    gcols = np.asarray(_ml_gate_columns(nh))
    gbl = jnp.zeros((nh // ML_HP, 1, 128), F32).at[:, 0, :gcols.shape[1]].set(gate_b.reshape(-1)[gcols])
    return pl.pallas_call(
        functools.partial(_mlstm_kernel, nc=nc, dh=dh),
        grid=(bsz, nh // ML_HP),
        in_specs=[pl.BlockSpec((1, L, bw), lambda b, h: (b, 0, cq // bw + h)),
                  pl.BlockSpec((1, L, bw), lambda b, h: (b, 0, ck // bw + h)),
                  pl.BlockSpec((1, L, bw), lambda b, h: (b, 0, cv // bw + h)),
                  pl.BlockSpec((1, L, 128), lambda b, h: (b, 0, cg // 128 + h)),
                  pl.BlockSpec((CONV_W, bw), lambda b, h: (0, h)),
                  pl.BlockSpec((1, bw), lambda b, h: (0, h)),
                  pl.BlockSpec((CONV_W, bw), lambda b, h: (0, nh // ML_HP + h)),
                  pl.BlockSpec((1, bw), lambda b, h: (0, nh // ML_HP + h)),
                  pl.BlockSpec((1, 1, 128), lambda b, h: (h, 0, 0))],
        out_specs=pl.BlockSpec((1, L, bw), lambda b, h: (b, 0, h)),
        out_shape=jax.ShapeDtypeStruct((bsz, L, nh * dh), F32),
        scratch_shapes=[pltpu.VMEM((L, bw), F32), pltpu.VMEM((L, bw), F32), pltpu.VMEM((nc, bw, CHUNK), BF16)],
        compiler_params=_cp(("parallel", "parallel")), name="mlstm",
    )(proj, proj, proj, proj, conv_w, conv_b.reshape(1, -1), conv_w, conv_b.reshape(1, -1), gbl)


def kernel(x, c, ctx, c_ctx, mod_w, mod_b, norm1_g, norm2_g, even_w_in, even_w_out, lru_conv_w, lru_conv_b, lru_wa, lru_ba, lru_wx, lru_bx, lru_lam, na_q_g, na_k_g, na_rpb, odd_w_in, odd_w_out, ssd_conv_w, ssd_conv_b, ssd_dt_bias, ssd_a_log, ssd_d, ssd_norm_g, ml_conv_w, ml_conv_b, ml_gate_b, ml_norm_g, moe_router_g, moe_router_e, moe_w1, moe_w3, moe_w2):
    bsz, S, d = x.shape
    lc = ctx.shape[1]
    assert lc == TM and S % TM == 0 and mod_w.shape[0] == 2
    nt = (lc + S) // TM
    L = lc + S

    cc = jnp.zeros((8, d), F32).at[0].set(c_ctx).at[1:1 + bsz].set(c)
    mod_all = _modulation(cc, mod_w, mod_b)

    def mod_for(l):
        m = mod_all[l].reshape(8, 6, d)
        return jnp.stack([jnp.broadcast_to(m[0], (bsz, 6, d)), m[1:1 + bsz]], axis=1)

    def router_w(l):
        w = jnp.zeros((d, 128), F32).at[:, :EXPERT_LANE0].set(moe_router_g[l]) \
            .at[:, EXPERT_LANE0:EXPERT_LANE0 + N_EXPERTS].set(moe_router_e[l])
        hi = w.astype(BF16)
        return jnp.concatenate([hi, (w - hi.astype(F32)).astype(BF16)], axis=1)

    xx = (ctx, x)

    mod0 = mod_for(0)
    proj = _inproj(xx, mod0, norm1_g[0], even_w_in[0].astype(BF16))
    lw = lru_conv_w.shape[-1]
    r = _lru(proj, lru_conv_w[0], lru_conv_b[0], lru_wa[0], lru_ba[0], lru_wx[0], lru_bx[0], lru_lam[0])
    a = _na(proj, na_q_g[0], na_k_g[0], na_rpb[0], col0=2 * lw)
    x1, h2, gate = _outproj([(r, 0, lw), (proj, 1, lw), (a, 0, lw)], xx, even_w_out[0].astype(BF16), mod0,
                            norm2_g[0], router_w(0), even=True, tile0=0, ntiles=nt)
    ew = (moe_w1.astype(BF16), moe_w3.astype(BF16), moe_w2.astype(BF16))
    moe0 = _moe(h2.reshape(bsz * L, d), gate.reshape(bsz * L, 128), 0, *ew).reshape(bsz, L, d)

    mod1 = mod_for(1)
    sw = ssd_d.shape[-1] * 64
    xbc = ssd_conv_w.shape[-1]
    mw = ml_norm_g.shape[-1]
    w = odd_w_in[0]
    o = np.cumsum([0, sw, xbc, 2 * ssd_d.shape[-1], mw, mw, mw, mw])
    ndt = 2 * ssd_d.shape[-1]
    dt_blk = jnp.concatenate([w[:, o[2]:o[3]], jnp.zeros((d, 128 - ndt), F32)], axis=1)
    gate_blks = [jnp.concatenate([w[:, o[7] + np.asarray(blk)], jnp.zeros((d, 128 - len(blk)), F32)], axis=1)
                 for blk in _ml_gate_columns(ml_gate_b.shape[-1])]
    w_odd = jnp.concatenate([w[:, :o[2]], w[:, o[3]:o[7]], dt_blk] + gate_blks, axis=1).astype(BF16)
    cz, cxs = 0, sw
    cB, cC = cxs + sw, cxs + sw + (xbc - sw) // 2
    cq = sw + xbc
    ck, cv, co, csm = cq + mw, cq + 2 * mw, cq + 3 * mw, cq + 4 * mw
    x0, proj1 = _inproj(x1, mod1, norm1_g[1], w_odd, prev=(moe0, mod0))
    ys = _ssd(proj1, ssd_conv_w[0], ssd_conv_b[0], ssd_dt_bias[0], ssd_a_log[0], ssd_d[0], (cxs, cB, cC, csm))
    hm = _mlstm(proj1, ml_conv_w[0], ml_conv_b[0], ml_gate_b[0], (cq, ck, cv, csm + 128))
    x2, h2b, gate1 = _outproj([(ys, 0, sw), (proj1, cz // sw, sw), (hm, 0, mw), (proj1, co // mw, mw),
                               (ssd_norm_g[0].reshape(1, sw), 0, sw), (ml_norm_g[0].reshape(1, mw), 0, mw)],
                              x0, odd_w_out[0].astype(BF16), mod1, norm2_g[1], router_w(1),
                              even=False, tile0=1, ntiles=nt - 1)
    out = _moe(h2b.reshape(bsz * S, d), gate1.reshape(bsz * S, 128), 1, *ew, resid=(x2.reshape(bsz * S, d), mod1))
    return out.reshape(bsz, S, d)
```

```python
import functools
import math

import jax
import jax.numpy as jnp
import numpy as np
from jax import lax
from jax.experimental import pallas as pl
from jax.experimental.pallas import tpu as pltpu

F32 = jnp.float32
BF16 = jnp.bfloat16
HI = lax.Precision.HIGHEST

EPS = 1e-6
NEG = -1e30
GRID_W = 64
CONV_W = 4
LRU_C = 8.0
TM = 256
CHUNK = 128
NA_RQ = 4
NA_RK = 12
N_EXPERTS = 16
EXPERT_LANE0 = 4
VMEM_LIMIT = 56 * 1024 * 1024


def _cp(sem, vmem=VMEM_LIMIT):
    return pltpu.CompilerParams(dimension_semantics=sem, vmem_limit_bytes=vmem)


def _sigmoid(x):
    return jax.nn.sigmoid(x)


def _silu(x):
    return x * jax.nn.sigmoid(x)


def _softplus(x):
    return jnp.maximum(x, 0.0) + jnp.log(1.0 + jnp.exp(-jnp.abs(x)))


def _gelu_tanh(x):
    return 0.5 * x * (1.0 + jnp.tanh(math.sqrt(2.0 / math.pi) * (x + 0.044715 * (x * x * x))))


def _rms(x, axis=-1):
    return x * lax.rsqrt(jnp.mean(x * x, axis=axis, keepdims=True) + EPS)


def _dot(a, b):
    return jnp.dot(a.astype(BF16), b.astype(BF16), preferred_element_type=F32)


def _dot_hi(a, b):
    return jnp.dot(a, b, precision=HI, preferred_element_type=F32)


def _split3(x):
    x1 = x.astype(BF16)
    r = x - x1.astype(F32)
    x2 = r.astype(BF16)
    x3 = (r - x2.astype(F32)).astype(BF16)
    return x1, x2, x3


def _dot_sel(sel, x, dims=(((1,), (0,)), ((), ()))):
    s = sel.astype(BF16)
    x1, x2, x3 = _split3(x)
    d = lambda xi: lax.dot_general(s, xi, dims, preferred_element_type=F32)
    return (d(x3) + d(x2)) + d(x1)


def _dot_sel_r(x, sel):
    s = sel.astype(BF16)
    x1, x2, x3 = _split3(x)
    d = lambda xi: jnp.dot(xi, s, preferred_element_type=F32)
    return (d(x3) + d(x2)) + d(x1)


_NT = (((1,), (1,)), ((), ()))


def _interleave(gens):
    results = [None] * len(gens)
    live = list(range(len(gens)))
    while live:
        for i in list(live):
            try:
                next(gens[i])
            except StopIteration as stop:
                results[i] = stop.value
                live.remove(i)
    return results


def _dot_nt(a, b):
    return lax.dot_general(a.astype(BF16), b.astype(BF16), (((1,), (1,)), ((), ())),
                           preferred_element_type=F32)


def _mod_kernel(c_ref, w_ref, b_ref, o_ref):
    c = c_ref[...]
    o_ref[0] = _dot_hi(_silu(c), w_ref[0]) + b_ref[0]


def _modulation(cc, mod_w, mod_b):
    depth, d, n = mod_w.shape
    tn = 1536
    return pl.pallas_call(
        _mod_kernel,
        grid=(depth, n // tn),
        in_specs=[pl.BlockSpec((8, d), lambda l, j: (0, 0)),
                  pl.BlockSpec((1, d, tn), lambda l, j: (l, 0, j)),
                  pl.BlockSpec((1, 1, tn), lambda l, j: (l, 0, j))],
        out_specs=pl.BlockSpec((1, 8, tn), lambda l, j: (l, 0, j)),
        out_shape=jax.ShapeDtypeStruct((depth, 8, n), F32),
        compiler_params=_cp(("arbitrary", "arbitrary")),
        name="adaln_mod",
    )(cc, mod_w, mod_b.reshape(depth, 1, n))


def _inproj_kernel(*refs, fuse_prev):
    if fuse_prev:
        x_ref, mo_ref, pmod_ref, mod_ref, g_ref, w_ref, xo_ref, p_ref = refs
        x = x_ref[0] + pmod_ref[0, 0][5:6, :] * mo_ref[0]
        xo_ref[0] = x
    else:
        c_ref, x_ref, mod_ref, g_ref, w_ref, p_ref = refs
        x = _pick_segment(c_ref, x_ref)
    mod = mod_ref[0, 0]
    h = _rms(x) * g_ref[...] * (1.0 + mod[1:2, :]) + mod[0:1, :]
    p_ref[0] = _dot(h, w_ref[...])


def _seg_map(b, i):
    return (b, jnp.minimum(i, 1), 0, 0)


def _segment_specs(d):
    return [pl.BlockSpec((1, TM, d), lambda b, i: (b, 0, 0)),
            pl.BlockSpec((1, TM, d), lambda b, i: (b, jnp.maximum(i - 1, 0), 0))]


def _pick_segment(c_ref, x_ref):
    return jnp.where(pl.program_id(1) == 0, c_ref[0], x_ref[0])


def _inproj(x, mod, g, w, prev=None):
    if prev is None:
        ctx, lat = x
        bsz, S, d = lat.shape
        L = TM + S
    else:
        bsz, L, d = x.shape
    n = w.shape[1]
    nt = L // TM
    tok = pl.BlockSpec((1, TM, d), lambda b, i: (b, i, 0))
    modspec = pl.BlockSpec((1, 1, 6, d), _seg_map)
    tail = [modspec, pl.BlockSpec((1, d), lambda b, i: (0, 0)), pl.BlockSpec((d, n), lambda b, i: (0, 0))]
    pspec = pl.BlockSpec((1, TM, n), lambda b, i: (b, i, 0))
    pshape = jax.ShapeDtypeStruct((bsz, L, n), F32)
    if prev is None:
        return pl.pallas_call(
            functools.partial(_inproj_kernel, fuse_prev=False),
            grid=(bsz, nt), in_specs=_segment_specs(d) + tail, out_specs=pspec, out_shape=pshape,
            compiler_params=_cp(("parallel", "arbitrary")), name="inproj",
        )(ctx, lat, mod, g.reshape(1, d), w)
    moe_out, pmod = prev
    return pl.pallas_call(
        functools.partial(_inproj_kernel, fuse_prev=True),
        grid=(bsz, nt), in_specs=[tok, tok, modspec] + tail,
        out_specs=[tok, pspec], out_shape=[jax.ShapeDtypeStruct(x.shape, F32), pshape],
        compiler_params=_cp(("parallel", "parallel")), name="inproj_res",
    )(x, moe_out, pmod, mod, g.reshape(1, d), w)


def _conv_tile(ref, i, nt, cw, cb, width=TM):
    L = nt * width
    t0 = pl.multiple_of(i * width, width)
    cur = ref[0, pl.ds(t0, width), :]
    prev = ref[0, pl.ds(pl.multiple_of(jnp.maximum(t0 - 8, 0), 8), 8), :]
    nxt = ref[0, pl.ds(pl.multiple_of(jnp.minimum(t0 + width, L - 8), 8), 8), :]
    return _conv_vals(cur, prev, nxt, i, nt, cw, cb, width, first_lat=TM // width)


def _conv_vals(cur, prev, nxt, i, nt, cw, cb, width, first_lat):
    prev = jnp.where((i != 0) & (i != first_lat), prev, 0.0)
    nxt = jnp.where((i != first_lat - 1) & (i != nt - 1), nxt, 0.0)
    cat = jnp.concatenate([prev, cur, nxt], axis=0)
    return (cw[0:1] * cat[6:6 + width] + cw[1:2] * cat[7:7 + width] + cw[2:3] * cur
            + cw[3:4] * cat[9:9 + width] + cb)


def _lru_kernel(ux_ref, cw_ref, cb_ref, gw_ref, gb_ref, lam_ref, o_ref, *, nt):
    cw = cw_ref[...]
    cb = cb_ref[...]
    lam = lam_ref[0]
    sp = _softplus(-lam)
    row = lax.broadcasted_iota(jnp.int32, (TM, 128), 0) & 7

    def gates(i, d):
        xl = _conv_tile(ux_ref, i, nt, cw, cb)
        g = _dot(xl, gw_ref[0, d]) + gb_ref[0, d]
        r = _sigmoid(g[:, :128])
        ig = _sigmoid(g[:, 128:])
        log_a = -LRU_C * r * sp[d:d + 1]
        a = jnp.exp(log_a)
        u = jnp.sqrt(1.0 - a * a) * (ig * xl)
        return a, u

    def scan_tile(i, d, carry, accumulate):
        a, u = gates(i, d)
        rev = d == 1
        for k in (1, 2, 4):
            sh = TM - k if rev else k
            ok = (row < 8 - k) if rev else (row >= k)
            ash = pltpu.roll(a, sh, 0)
            ush = pltpu.roll(u, sh, 0)
            u = jnp.where(ok, u + a * ush, u)
            a = jnp.where(ok, a * ash, a)
        t0 = i * TM
        groups = range(TM // 8)
        for s in (reversed(groups) if rev else groups):
            h = u[s * 8:(s + 1) * 8] + a[s * 8:(s + 1) * 8] * carry
            carry = h[0:1] if rev else h[7:8]
            idx = pl.ds(pl.multiple_of(t0 + s * 8, 8), 8)
            if accumulate:
                o_ref[0, idx, :] += h
            else:
                o_ref[0, idx, :] = h
        return carry

    zero = jnp.zeros((1, 128), F32)
    lax.fori_loop(0, nt, lambda i, c: scan_tile(i, 0, c, False), zero)
    lax.fori_loop(0, nt, lambda j, c: scan_tile(jnp.where(j == 0, 0, nt - j), 1, c, True), zero)


def _lru(proj, conv_w, conv_b, wa, ba, wx, bx, lam):
    bsz, L, _ = proj.shape
    nt = L // TM
    width = conv_w.shape[1]
    ng = width // 128

    def blockdiag(w):
        w = w.reshape(2, ng, 2, 64, 64)
        z = jnp.zeros_like(w[:, :, 0])
        top = jnp.concatenate([w[:, :, 0], z], axis=-1)
        bot = jnp.concatenate([z, w[:, :, 1]], axis=-1)
        return jnp.concatenate([top, bot], axis=-2)
    gw = jnp.concatenate([blockdiag(wa), blockdiag(wx)], axis=-1).transpose(1, 0, 2, 3).astype(BF16)
    gb = jnp.concatenate([ba.reshape(2, ng, 1, 128), bx.reshape(2, ng, 1, 128)], axis=-1).transpose(1, 0, 2, 3)
    lam_g = lam.reshape(2, ng, 128).transpose(1, 0, 2)
    return pl.pallas_call(
        functools.partial(_lru_kernel, nt=nt),
        grid=(bsz, ng),
        in_specs=[pl.BlockSpec((1, L, 128), lambda b, c: (b, 0, c)),
                  pl.BlockSpec((CONV_W, 128), lambda b, c: (0, c)),
                  pl.BlockSpec((1, 128), lambda b, c: (0, c)),
                  pl.BlockSpec((1, 2, 128, 256), lambda b, c: (c, 0, 0, 0)),
                  pl.BlockSpec((1, 2, 1, 256), lambda b, c: (c, 0, 0, 0)),
                  pl.BlockSpec((1, 2, 128), lambda b, c: (c, 0, 0))],
        out_specs=pl.BlockSpec((1, L, 128), lambda b, c: (b, 0, c)),
        out_shape=jax.ShapeDtypeStruct((bsz, L, width), F32),
        compiler_params=_cp(("parallel", "parallel")), name="rglru",
    )(proj, conv_w, conv_b.reshape(1, width), gw, gb, lam_g)


def _na_bias_table(rpb, rows):
    nh = rpb.shape[0]
    win_r = (rpb.shape[1] + 1) // 2
    win_c = (rpb.shape[2] + 1) // 2
    qr = np.arange(NA_RQ)[:, None, None, None]
    qc = np.arange(GRID_W)[None, :, None, None]
    kr = np.arange(NA_RK)[None, None, :, None]
    kc = np.arange(GRID_W)[None, None, None, :]
    cstart = np.clip(qc - win_c // 2, 0, GRID_W - win_c)
    col_ok = (kc >= cstart) & (kc < cstart + win_c)
    dcol = np.clip(kc - qc + (win_c - 1), 0, 2 * win_c - 2)
    oc = (np.arange(2 * win_c - 1)[:, None, None] == dcol[0, :, 0, :][None]).astype(np.float32)
    o_rows, oks = [], []
    for r0, w0 in ((0, 0), (2 * NA_RQ, NA_RQ), (rows - NA_RQ, rows - NA_RK)):
        r = r0 + qr
        kabs = w0 + kr
        rstart = np.clip(r - win_r // 2, 0, rows - win_r)
        oks.append(np.broadcast_to((kabs >= rstart) & (kabs < rstart + win_r) & col_ok,
                                   (NA_RQ, GRID_W, NA_RK, GRID_W)))
        drow = np.clip(kabs - r + (win_r - 1), 0, 2 * win_r - 2)[:, 0, :, 0]
        o_rows.append((np.arange(2 * win_r - 1)[:, None, None] == drow[None]).astype(np.float32))
    t1 = jnp.einsum('hrc,prab->phabc', rpb.astype(F32), jnp.asarray(np.stack(o_rows)), precision=HI)
    b = jnp.einsum('phabc,cqk->phaqbk', t1, jnp.asarray(oc), precision=HI)
    b = jnp.where(jnp.asarray(np.stack(oks))[:, None], b, NEG)
    return b.reshape(3, nh, NA_RQ * GRID_W, NA_RK * GRID_W)


def _pair_rms(x, lo):
    x2 = x * x
    s0 = jnp.sum(jnp.where(lo, x2, 0.0), axis=-1, keepdims=True)
    s1 = jnp.sum(jnp.where(lo, 0.0, x2), axis=-1, keepdims=True)
    return x * lax.rsqrt(jnp.where(lo, s0, s1) * (2.0 / x.shape[-1]) + EPS)


def _na_kernel(q_ref, k_ref, v_ref, bias_ref, qg_ref, kg_ref, o_ref, kn_s, vb_s, *, rows, hd):
    i = pl.program_id(2)
    L = k_ref.shape[1]
    nkeys = NA_RK * GRID_W
    lo = lax.broadcasted_iota(jnp.int32, (1, 2 * hd), 1) < hd

    @pl.when(i == 0)
    def _prep():
        def body(t, _):
            idx = pl.ds(pl.multiple_of(t * TM, TM), TM)
            kn_s[idx, :] = (_pair_rms(k_ref[0, idx, :], lo) * kg_ref[...]).astype(BF16)
            vb_s[idx, :] = v_ref[0, idx, :].astype(BF16)
            return 0
        lax.fori_loop(0, L // TM, body, 0)

    qn = _pair_rms(q_ref[0], lo) * (qg_ref[...] * hd ** -0.5)
    q_h = [jnp.where(lo, qn, 0.0).astype(BF16), jnp.where(lo, 0.0, qn).astype(BF16)]
    kctx = kn_s[0:TM, :]
    vctx = vb_s[0:TM, :]

    def head(hh, kwin, vwin):
        s_c = _dot_nt(q_h[hh], kctx)
        if kwin is not None:
            s_w = _dot_nt(q_h[hh], kwin) + bias_ref[0, hh]
        yield
        m = jnp.max(s_c, axis=-1, keepdims=True)
        if kwin is not None:
            m = jnp.maximum(m, jnp.max(s_w, axis=-1, keepdims=True))
            p_w = jnp.exp(s_w - m)
        p_c = jnp.exp(s_c - m)
        den = jnp.sum(p_c, axis=-1, keepdims=True)
        num = _dot(p_c, vctx)
        if kwin is not None:
            den = den + jnp.sum(p_w, axis=-1, keepdims=True)
            num = num + _dot(p_w, vwin)
        yield
        return num / den

    def both(kwin, vwin):
        o0, o1 = _interleave([head(0, kwin, vwin), head(1, kwin, vwin)])
        o_ref[0] = jnp.where(lo, o0, o1)

    @pl.when(i == 0)
    def _():
        both(None, None)

    @pl.when(i > 0)
    def _():
        r0 = (i - 1) * NA_RQ
        w0 = jnp.clip(r0 - NA_RQ, 0, rows - NA_RK)
        start = pl.multiple_of(TM + w0 * GRID_W, GRID_W)
        both(kn_s[pl.ds(start, nkeys), :], vb_s[pl.ds(start, nkeys), :])


def _na(proj, q_g, k_g, rpb, col0):
    bsz, L, _ = proj.shape
    nh = rpb.shape[0]
    hd = q_g.shape[0]
    width = nh * hd
    rows = (L - TM) // GRID_W
    nb = L // TM
    bias = _na_bias_table(rpb, rows)
    qb, kb, vb = col0 // 128, (col0 + width) // 128, (col0 + 2 * width) // 128

    def pat(i):
        return jnp.where(i <= 1, 0, jnp.where(i == nb - 1, 2, 1))
    return pl.pallas_call(
        functools.partial(_na_kernel, rows=rows, hd=hd),
        grid=(bsz, width // 128, nb),
        in_specs=[pl.BlockSpec((1, TM, 128), lambda b, h, i: (b, i, qb + h)),
                  pl.BlockSpec((1, L, 128), lambda b, h, i: (b, 0, kb + h)),
                  pl.BlockSpec((1, L, 128), lambda b, h, i: (b, 0, vb + h)),
                  pl.BlockSpec((1, 2, TM, NA_RK * GRID_W), lambda b, h, i: (pat(i), h, 0, 0)),
                  pl.BlockSpec((1, 2 * hd), lambda b, h, i: (0, 0)),
                  pl.BlockSpec((1, 2 * hd), lambda b, h, i: (0, 0))],
        out_specs=pl.BlockSpec((1, TM, 128), lambda b, h, i: (b, i, h)),
        out_shape=jax.ShapeDtypeStruct((bsz, L, width), F32),
        scratch_shapes=[pltpu.VMEM((L, 128), BF16), pltpu.VMEM((L, 128), BF16)],
        compiler_params=_cp(("parallel", "parallel", "arbitrary")), name="nbr_attn",
    )(proj, proj, proj, bias, jnp.tile(q_g, 2).reshape(1, 2 * hd), jnp.tile(k_g, 2).reshape(1, 2 * hd))


def _route(lg):
    lane = lax.broadcasted_iota(jnp.int32, lg.shape, 1)
    lane_f = lane.astype(F32)
    is_g = lane < EXPERT_LANE0
    gl = jnp.where(is_g, lg, NEG)
    gmax = jnp.max(gl, axis=-1, keepdims=True)
    gsel = jnp.min(jnp.where(is_g & (gl == gmax), lane_f, 1e9), axis=-1, keepdims=True)
    g_w = 1.0 / jnp.sum(jnp.where(is_g, jnp.exp(gl - gmax), 0.0), axis=-1, keepdims=True)
    grp = ((lane - EXPERT_LANE0) >> 2).astype(F32)
    in_g = (lane >= EXPERT_LANE0) & (lane < EXPERT_LANE0 + N_EXPERTS) & (grp == gsel)
    el = jnp.where(in_g, lg, NEG)
    v1 = jnp.max(el, axis=-1, keepdims=True)
    i1 = jnp.min(jnp.where(in_g & (el == v1), lane_f, 1e9), axis=-1, keepdims=True)
    el2 = jnp.where(lane_f == i1, NEG, el)
    v2 = jnp.max(el2, axis=-1, keepdims=True)
    i2 = jnp.min(jnp.where(in_g & (lane_f != i1) & (el2 == v2), lane_f, 1e9), axis=-1, keepdims=True)
    t = jnp.exp(v2 - v1)
    w1 = g_w / (1.0 + t)
    w2 = g_w * t / (1.0 + t)
    return (jnp.where(lane_f == i1, w1, 0.0) + jnp.where(lane_f == i2, w2, 0.0)
            + jnp.where(lane_f == gsel, 1.0, 0.0))


def _outproj_kernel(*refs, even):
    if even:
        (r_ref, ug_ref, a_ref, c_ref, x_ref, w_ref, mod_ref, g2_ref, rw_ref, x1_ref, h2_ref, gate_ref) = refs
        is_ctx = pl.program_id(1) == 0
        x_rows = lambda rs: jnp.where(is_ctx, c_ref[0, rs, :], x_ref[0, rs, :])
    else:
        (ys_ref, z_ref, hm_ref, mo_ref, sg_ref, mg_ref, x_ref, w_ref, mod_ref, g2_ref, rw_ref,
         x1_ref, h2_ref, gate_ref) = refs
        x_rows = lambda rs: x_ref[0, rs, :]
    mod = mod_ref[0, 0]

    def part(rs):
        if even:
            y_in = jnp.concatenate([r_ref[0, rs, :] * _gelu_tanh(ug_ref[0, rs, :]), a_ref[0, rs, :]], axis=-1)
        else:
            ys = ys_ref[0, rs, :] * _silu(z_ref[0, rs, :])
            sg = sg_ref[...]
            mg = mg_ref[...]
            hm = hm_ref[0, rs, :]
            sig_o = _sigmoid(mo_ref[0, rs, :])
            gw = ys.shape[-1] // 2
            parts = [_rms(ys[:, g * gw:(g + 1) * gw]) * sg[:, g * gw:(g + 1) * gw] for g in range(2)]
            hw = 128
            parts += [_rms(hm[:, h * hw:(h + 1) * hw]) * mg[:, h * hw:(h + 1) * hw] * sig_o[:, h * hw:(h + 1) * hw]
                      for h in range(hm.shape[-1] // hw)]
            y_in = jnp.concatenate(parts, axis=-1)
        y = _dot(y_in, w_ref[...])
        yield
        x1 = x_rows(rs) + mod[2:3, :] * y
        x1_ref[0, rs, :] = x1
        h2 = _rms(x1) * g2_ref[...] * (1.0 + mod[4:5, :]) + mod[3:4, :]
        h2_ref[0, rs, :] = h2.astype(BF16)
        hi = h2.astype(BF16)
        lo = (h2 - hi.astype(F32)).astype(BF16)
        lg2 = jnp.dot(hi, rw_ref[...], preferred_element_type=F32)
        lg1 = jnp.dot(lo, rw_ref[:, 0:128], preferred_element_type=F32)
        yield
        gate_ref[0, rs, :] = _route(lg2[:, 0:128] + lg2[:, 128:256] + lg1)

    nparts = 2
    rows = TM // nparts
    _interleave([part(pl.ds(p * rows, rows)) for p in range(nparts)])


def _outproj(mix_inputs, x, w, mod, g2, rw, even, tile0, ntiles):
    xs = list(x) if even else [x]
    bsz, _, d = xs[-1].shape
    specs, args = [], []
    for arr, cb, wdt in mix_inputs:
        if arr.ndim == 3:
            specs.append(pl.BlockSpec((1, TM, wdt), lambda b, i, cb=cb: (b, i + tile0, cb)))
        else:
            specs.append(pl.BlockSpec((1, wdt), lambda b, i: (0, 0)))
        args.append(arr)
    specs += _segment_specs(d) if even else [pl.BlockSpec((1, TM, d), lambda b, i: (b, i + tile0, 0))]
    specs += [pl.BlockSpec(w.shape, lambda b, i: (0, 0)),
              pl.BlockSpec((1, 1, 6, d), lambda b, i: (b, jnp.minimum(i + tile0, 1), 0, 0)),
              pl.BlockSpec((1, d), lambda b, i: (0, 0)),
              pl.BlockSpec(rw.shape, lambda b, i: (0, 0))]
    args += xs + [w, mod, g2.reshape(1, d), rw]
    lo = ntiles * TM
    return pl.pallas_call(
        functools.partial(_outproj_kernel, even=even),
        grid=(bsz, ntiles), in_specs=specs,
        out_specs=[pl.BlockSpec((1, TM, d), lambda b, i: (b, i, 0)),
                   pl.BlockSpec((1, TM, d), lambda b, i: (b, i, 0)),
                   pl.BlockSpec((1, TM, 128), lambda b, i: (b, i, 0))],
        out_shape=[jax.ShapeDtypeStruct((bsz, lo, d), F32), jax.ShapeDtypeStruct((bsz, lo, d), BF16),
                   jax.ShapeDtypeStruct((bsz, lo, 128), F32)],
        compiler_params=_cp(("parallel", "parallel")), name="outproj_even" if even else "outproj_odd",
    )(*args)


MOE_CH = 128
GROUP_SIZE = 4
MOE_VMEM_LIMIT = 62 * 1024 * 1024
N_GROUPS = N_EXPERTS // GROUP_SIZE


def _moe_kernel(*refs, residual):
    if residual:
        x_ref, g_ref, w1_ref, w3_ref, w2_ref, x1_ref, mod_ref, o_ref, xs_s, ys_s, gs_s, pt_s, plan_s = refs
    else:
        x_ref, g_ref, w1_ref, w3_ref, w2_ref, o_ref, xs_s, ys_s, gs_s, pt_s, plan_s = refs
    grp = pl.program_id(1)
    tb = x_ref.shape[0]
    nch = xs_s.shape[0] // MOE_CH

    @pl.when(grp == 0)
    def _plan():
        g = g_ref[...]
        lane = lax.broadcasted_iota(jnp.int32, g.shape, 1)
        oh = jnp.where(lane < N_GROUPS, g, 0.0)
        earlier = (lax.broadcasted_iota(jnp.int32, (tb, tb), 0) > lax.broadcasted_iota(jnp.int32, (tb, tb), 1))
        rank = jnp.dot(jnp.where(earlier, 1.0, 0.0).astype(BF16), oh.astype(BF16),
                       preferred_element_type=F32)
        cnt = jnp.sum(oh, axis=0, keepdims=True)
        lane1 = lax.broadcasted_iota(jnp.int32, (1, 128), 1)
        off = jnp.int32(0)
        offv = jnp.zeros((1, 128), F32)
        for gi in range(N_GROUPS):
            n = jnp.sum(jnp.where(lane1 == gi, cnt, 0.0)).astype(jnp.int32)
            nchunks = (n + (MOE_CH - 1)) // MOE_CH
            plan_s[gi] = off // MOE_CH
            plan_s[N_GROUPS + gi] = nchunks
            offv = offv + jnp.where(lane1 == gi, off.astype(F32), 0.0)
            off = off + nchunks * MOE_CH
        pos_col = jnp.sum(oh * (rank + offv), axis=1, keepdims=True)
        posb = jnp.broadcast_to(pos_col, (tb, 128))
        pos_row = jnp.concatenate([posb[i * 128:(i + 1) * 128, :].T[0:1, :] for i in range(tb // 128)], axis=1)
        x = x_ref[...]
        g_hi = g.astype(BF16)
        g_lo = (g - g_hi.astype(F32)).astype(BF16)
        lane_c = lax.broadcasted_iota(jnp.int32, (tb, MOE_CH), 1).astype(F32)
        row_c = lax.broadcasted_iota(jnp.int32, (MOE_CH, tb), 0).astype(F32)
        for c in range(nch):
            sl = slice(c * MOE_CH, (c + 1) * MOE_CH)
            pt_s[:, sl] = jnp.where(pos_col == lane_c + float(c * MOE_CH), 1.0, 0.0).astype(BF16)
            p = jnp.where(row_c + float(c * MOE_CH) == pos_row, 1.0, 0.0).astype(BF16)
            xs_s[sl, :] = jnp.dot(p, x, preferred_element_type=F32).astype(BF16)
            gs_s[sl, :] = (jnp.dot(p, g_hi, preferred_element_type=F32)
                           + jnp.dot(p, g_lo, preferred_element_type=F32))
        ys_s[...] = jnp.zeros_like(ys_s)

    c0 = plan_s[grp]
    nchunks = plan_s[N_GROUPS + grp]

    def ffn(chunk0, nrows):
        rows = pl.ds(pl.multiple_of(chunk0 * MOE_CH, MOE_CH), nrows)
        xs = xs_s[rows, :]
        gs = gs_s[rows, :]
        lane_g = lax.broadcasted_iota(jnp.int32, (nrows, 128), 1)
        acc = None
        for k in range(GROUP_SIZE):
            a = jnp.dot(xs, w1_ref[0, k], preferred_element_type=F32)
            b = jnp.dot(xs, w3_ref[0, k], preferred_element_type=F32)
            y = _dot(_silu(a) * b, w2_ref[0, k])
            ge = jnp.sum(jnp.where(lane_g == grp * GROUP_SIZE + k + EXPERT_LANE0, gs, 0.0), axis=-1, keepdims=True)
            acc = ge * y if acc is None else acc + ge * y
        ys_s[rows, :] = acc.astype(BF16)

    def pair(p, _):
        ffn(c0 + 2 * p, 2 * MOE_CH)
        return 0
    lax.fori_loop(0, nchunks // 2, pair, 0)

    @pl.when(nchunks % 2 == 1)
    def _():
        ffn(c0 + nchunks - 1, MOE_CH)

    @pl.when(grp == N_GROUPS - 1)
    def _combine():
        out = jnp.dot(pt_s[...], ys_s[...], preferred_element_type=F32)
        if residual:
            out = x1_ref[...] + mod_ref[0, 0][5:6, :] * out
        o_ref[...] = out


def _moe(h2, gate, layer, w1, w3, w2, resid=None):
    t, d = h2.shape
    _, ne, _, ff = w1.shape
    tb = math.gcd(t, 1024)
    npad = tb + N_GROUPS * MOE_CH
    assert ne == N_EXPERTS
    extra_specs, extra_args = [], []
    if resid is not None:
        x1, mod = resid
        per_batch = t // mod.shape[0] // tb
        extra_specs = [pl.BlockSpec((tb, d), lambda i, e: (i, 0)),
                       pl.BlockSpec((1, 1, 6, d), lambda i, e: (i // per_batch, 1, 0, 0))]
        extra_args = [x1, mod]
    return pl.pallas_call(
        functools.partial(_moe_kernel, residual=resid is not None),
        grid=(t // tb, N_GROUPS),
        in_specs=[pl.BlockSpec((tb, d), lambda i, e: (i, 0)),
                  pl.BlockSpec((tb, 128), lambda i, e: (i, 0)),
                  pl.BlockSpec((1, GROUP_SIZE, d, ff), lambda i, e: (layer, e, 0, 0)),
                  pl.BlockSpec((1, GROUP_SIZE, d, ff), lambda i, e: (layer, e, 0, 0)),
                  pl.BlockSpec((1, GROUP_SIZE, ff, d), lambda i, e: (layer, e, 0, 0))] + extra_specs,
        out_specs=pl.BlockSpec((tb, d), lambda i, e: (i, 0)),
        out_shape=jax.ShapeDtypeStruct((t, d), F32),
        scratch_shapes=[pltpu.VMEM((npad, d), BF16), pltpu.VMEM((npad, d), BF16), pltpu.VMEM((npad, 128), F32),
                        pltpu.VMEM((tb, npad), BF16), pltpu.SMEM((2 * N_GROUPS,), jnp.int32)],
        compiler_params=_cp(("parallel", "arbitrary"), MOE_VMEM_LIMIT), name="moe",
    )(h2, gate, w1, w3, w2, *extra_args)


def _tri(q, rev):
    r = lax.broadcasted_iota(jnp.int32, (q, q), 0)
    c = lax.broadcasted_iota(jnp.int32, (q, q), 1)
    return (c >= r) if rev else (c <= r)


def _ssd_kernel(xs_ref, b_ref, c_ref, dt_ref, cwx_ref, cbx_ref, cwb_ref, cbb_ref, cwc_ref, cbc_ref,
                dtb_ref, alog_ref, dsk_ref, sel_ref, o_ref, xc_s, bc_s, cc_s, dt_s, bct_s, *, nc):
    q = CHUNK
    first_lat = TM // q

    dtb = dtb_ref[...]

    def prep(i, _):
        t0 = pl.multiple_of(i * q, q)
        idx = pl.ds(t0, q)
        for src, cw, cb, dst in ((xs_ref, cwx_ref, cbx_ref, xc_s), (b_ref, cwb_ref, cbb_ref, bc_s),
                                 (c_ref, cwc_ref, cbc_ref, cc_s)):
            L = nc * q
            cur = src[0, idx, :]
            prev = src[0, pl.ds(pl.multiple_of(jnp.maximum(t0 - 8, 0), 8), 8), :]
            nxt = src[0, pl.ds(pl.multiple_of(jnp.minimum(t0 + q, L - 8), 8), 8), :]
            dst[idx, :] = _silu(_conv_vals(cur, prev, nxt, i, nc, cw[...], cb[...], q, first_lat))
        dt_s[idx, :] = _softplus(dt_ref[0, idx, :] + dtb)
        o_ref[0, idx, :] = dsk_ref[...] * xc_s[idx, :]
        bct_s[i] = bc_s[idx, :].T.astype(BF16)
        return 0
    lax.fori_loop(0, nc, prep, 0)

    a_lane = jnp.broadcast_to(-jnp.exp(alog_ref[...]), (8, 128))
    a_e = [_dot_hi(a_lane, sel_ref[d, 0])[0:1, :] for d in (0, 1)]
    lane256 = lax.broadcasted_iota(jnp.int32, (q, 256), 1) // 64
    r8 = lax.broadcasted_iota(jnp.int32, (8, 256), 0)
    l8 = lax.broadcasted_iota(jnp.int32, (8, 256), 1)
    head_rows = (l8 == r8 * 64).astype(F32)

    def chain(d, j, h):
        rev = d == 1
        causal = _tri(q, rev)
        tri = causal.astype(F32)
        sel = sel_ref[d, 0]
        last = 0 if rev else q - 1
        if rev:
            ci = jnp.where(j < first_lat, first_lat - 1 - j, nc - 1 - (j - first_lat))
        else:
            ci = j
        idx = pl.ds(pl.multiple_of(ci * q, q), q)
        dt_e = _dot_sel_r(dt_s[idx, :], sel)
        xc = xc_s[idx, :]
        bc = bc_s[idx, :]
        cc = cc_s[idx, :]
        cb = _dot_nt(cc, bc)
        y_off = _dot(cc, h)
        yield
        cum_e = _dot_sel(tri, dt_e * a_e[d])
        dtx = xc * dt_e
        yield
        cum_rows = _dot_sel(head_rows, cum_e, _NT)
        tot_e = cum_e[last:last + 1, :]
        states = _dot(bct_s[ci], dtx * jnp.exp(tot_e - cum_e))
        yield
        y = jnp.exp(cum_e) * y_off
        for r in range(4):
            seg = cum_e[:, r * 64:r * 64 + 1] - cum_rows[r:r + 1, :]
            m = cb * jnp.exp(jnp.where(causal, seg, NEG))
            y = y + _dot(m, jnp.where(lane256 == r, dtx, 0.0))
            yield
        o_ref[0, idx, :] += y
        return jnp.exp(tot_e) * h + states

    def chunk(j, hs):
        return tuple(_interleave([chain(0, j, hs[0]), chain(1, j, hs[1])]))
    h0 = jnp.zeros((128, 256), F32)
    lax.fori_loop(0, nc, chunk, (h0, h0))


def _ssd(proj, conv_w, conv_b, dt_bias, a_log, d_skip, cols):
    bsz, L, _ = proj.shape
    nc = L // CHUNK
    cx, cbm, ccm, cdt = cols
    nh = d_skip.shape[0]
    dtb = jnp.zeros((1, 128), F32).at[0, :2 * nh].set(dt_bias.reshape(-1))
    alog = jnp.zeros((1, 128), F32).at[0, :2 * nh].set(a_log.reshape(-1))
    dsk = jnp.repeat(d_skip.astype(F32), 64).reshape(1, nh * 64)
    sel = np.zeros((2, 2, 128, 256), np.float32)
    for d in range(2):
        for g in range(2):
            for r in range(4):
                sel[d, g, d * nh + g * 4 + r, r * 64:(r + 1) * 64] = 1.0
    cw_specs = []
    for width, off in ((256, 0), (128, 512), (128, 768)):
        cw_specs += [pl.BlockSpec((CONV_W, width), lambda b, g, off=off, width=width: (0, off // width + g)),
                     pl.BlockSpec((1, width), lambda b, g, off=off, width=width: (0, off // width + g))]
    return pl.pallas_call(
        functools.partial(_ssd_kernel, nc=nc),
        grid=(bsz, 2),
        in_specs=[pl.BlockSpec((1, L, 256), lambda b, g: (b, 0, cx // 256 + g)),
                  pl.BlockSpec((1, L, 128), lambda b, g: (b, 0, cbm // 128 + g)),
                  pl.BlockSpec((1, L, 128), lambda b, g: (b, 0, ccm // 128 + g)),
                  pl.BlockSpec((1, L, 128), lambda b, g: (b, 0, cdt // 128))] + cw_specs + [
                  pl.BlockSpec((1, 128), lambda b, g: (0, 0)),
                  pl.BlockSpec((1, 128), lambda b, g: (0, 0)),
                  pl.BlockSpec((1, 256), lambda b, g: (0, g)),
                  pl.BlockSpec((2, 1, 128, 256), lambda b, g: (0, g, 0, 0))],
        out_specs=pl.BlockSpec((1, L, 256), lambda b, g: (b, 0, g)),
        out_shape=jax.ShapeDtypeStruct((bsz, L, nh * 64), F32),
        scratch_shapes=[pltpu.VMEM((L, 256), F32), pltpu.VMEM((L, 128), F32), pltpu.VMEM((L, 128), F32),
                        pltpu.VMEM((L, 128), F32), pltpu.VMEM((nc, 128, CHUNK), BF16)],
        compiler_params=_cp(("parallel", "parallel")), name="ssd",
    )(proj, proj, proj, proj, conv_w, conv_b.reshape(1, -1), conv_w, conv_b.reshape(1, -1),
      conv_w, conv_b.reshape(1, -1), dtb, alog, dsk, jnp.asarray(sel))


ML_HP = 2


def _ml_gate_lanes(hh, d):
    li = (hh * 2 + d) * 2
    return li, li + 1


def _ml_gate_columns(nh):
    cols = []
    for hp in range(nh // ML_HP):
        blk = [None] * (4 * ML_HP)
        for hh in range(ML_HP):
            for d in range(2):
                for t, lane in enumerate(_ml_gate_lanes(hh, d)):
                    blk[lane] = (d * 2 + t) * nh + hp * ML_HP + hh
        cols.append(blk)
    return cols


def _mlstm_kernel(q_ref, k_ref, v_ref, g_ref, cwq_ref, cbq_ref, cwk_ref, cbk_ref, gb_ref, o_ref,
                  qc_s, kc_s, kct_s, *, nc, dh):
    q = CHUNK
    first_lat = TM // q

    def prep(i, _):
        t0 = pl.multiple_of(i * q, q)
        idx = pl.ds(t0, q)
        L = nc * q
        for src, cw, cb, dst, mul in ((q_ref, cwq_ref, cbq_ref, qc_s, 1.0), (k_ref, cwk_ref, cbk_ref, kc_s, dh ** -0.5)):
            cur = src[0, idx, :]
            prev = src[0, pl.ds(pl.multiple_of(jnp.maximum(t0 - 8, 0), 8), 8), :]
            nxt = src[0, pl.ds(pl.multiple_of(jnp.minimum(t0 + q, L - 8), 8), 8), :]
            dst[idx, :] = _silu(_conv_vals(cur, prev, nxt, i, nc, cw[...], cb[...], q, first_lat)) * mul
        for hh in range(ML_HP):
            kct_s[i, hh * dh:(hh + 1) * dh, :] = kc_s[idx, hh * dh:(hh + 1) * dh].T.astype(BF16)
        return 0
    lax.fori_loop(0, nc, prep, 0)

    gb = gb_ref[0]
    lane = lax.broadcasted_iota(jnp.int32, (q, 128), 1)
    eye8 = (lax.broadcasted_iota(jnp.int32, (8, 128), 0) == lax.broadcasted_iota(jnp.int32, (8, 128), 1)).astype(F32)
    o_ref[...] = jnp.zeros_like(o_ref)
    heads = range(ML_HP)

    def direction(d, j, states):
        rev = d == 1
        causal = _tri(q, rev)
        last = 0 if rev else q - 1
        if rev:
            ci = jnp.where(j < first_lat, first_lat - 1 - j, nc - 1 - (j - first_lat))
        else:
            ci = j
        idx = pl.ds(pl.multiple_of(ci * q, q), q)
        g = g_ref[0, idx, :] + gb
        logf = jnp.minimum(g, 0.0) - jnp.log(1.0 + jnp.exp(-jnp.abs(g)))
        is_f = functools.reduce(jnp.logical_or, [lane == _ml_gate_lanes(hh, d)[1] for hh in heads])
        cum = _dot_sel(jnp.where(causal, 1.0, 0.0), jnp.where(is_f, logf, 0.0))
        qc = [qc_s[idx, hh * dh:(hh + 1) * dh] for hh in heads]
        kc = [kc_s[idx, hh * dh:(hh + 1) * dh] for hh in heads]
        vc = [v_ref[0, idx, hh * dh:(hh + 1) * dh] for hh in heads]
        kct = [kct_s[ci, hh * dh:(hh + 1) * dh, :] for hh in heads]
        qk = [_dot(qc[hh], kct[hh]) for hh in heads]
        inter_c = [_dot(qc[hh], states[hh][0]) for hh in heads]
        yield
        rows = _dot_sel(eye8, jnp.where(is_f, cum, g), _NT)
        yield
        w, w_s, g_in, m_t, keep, m_new, wv, upd = [], [], [], [], [], [], [], []
        for hh in heads:
            li, lf = _ml_gate_lanes(hh, d)
            m_prev = states[hh][2]
            i_col, i_row = g[:, li:li + 1], rows[li:li + 1, :]
            b_col, b_row = cum[:, lf:lf + 1], rows[lf:lf + 1, :]
            dlog = jnp.where(causal, b_col - b_row + i_row, NEG)
            inter = b_col + m_prev
            m_t.append(jnp.maximum(jnp.max(dlog, axis=-1, keepdims=True), inter))
            w.append(jnp.exp(dlog - m_t[hh]) * qk[hh])
            g_in.append(jnp.exp(inter - m_t[hh]))
            b_end = b_col[last:last + 1, :]
            g_s = b_end - b_col + i_col
            m_new.append(jnp.maximum(jnp.max(g_s, axis=0, keepdims=True), b_end + m_prev))
            w_s.append(jnp.exp(g_s - m_new[hh]))
            keep.append(jnp.exp(b_end + m_prev - m_new[hh]))
            wv.append(_dot(w[hh], vc[hh]))
            upd.append(_dot(kct[hh], w_s[hh] * vc[hh]))
        yield
        new_states = []
        for hh in heads:
            c_prev, n_prev, _ = states[hh]
            num = wv[hh] + g_in[hh] * inter_c[hh]
            den = (jnp.sum(w[hh], axis=-1, keepdims=True)
                   + g_in[hh] * jnp.sum(qc[hh] * n_prev, axis=-1, keepdims=True))
            o_ref[0, idx, hh * dh:(hh + 1) * dh] += num / jnp.maximum(jnp.abs(den), jnp.exp(-m_t[hh]))
            new_states.append((keep[hh] * c_prev + upd[hh],
                               keep[hh] * n_prev + jnp.sum(w_s[hh] * kc[hh], axis=0, keepdims=True), m_new[hh]))
        return tuple(new_states)

    def chunk(j, states):
        return tuple(_interleave([direction(d, j, states[d]) for d in (0, 1)]))
    init = (jnp.zeros((dh, dh), F32), jnp.zeros((1, dh), F32), jnp.full((1, 1), NEG, F32))
    lax.fori_loop(0, nc, chunk, tuple(tuple(init for _ in heads) for _ in (0, 1)))


def _mlstm(proj, conv_w, conv_b, gate_b, cols):
    bsz, L, _ = proj.shape
    nc = L // CHUNK
    cq, ck, cv, cg = cols
    nh = gate_b.shape[-1]
    dh = conv_w.shape[1] // (2 * nh)
    bw = ML_HP * dh
    gcols = np.asarray(_ml_gate_columns(nh))
    gbl = jnp.zeros((nh // ML_HP, 1, 128), F32).at[:, 0, :gcols.shape[1]].set(gate_b.reshape(-1)[gcols])
    return pl.pallas_call(
        functools.partial(_mlstm_kernel, nc=nc, dh=dh),
        grid=(bsz, nh // ML_HP),
        in_specs=[pl.BlockSpec((1, L, bw), lambda b, h: (b, 0, cq // bw + h)),
                  pl.BlockSpec((1, L, bw), lambda b, h: (b, 0, ck // bw + h)),
                  pl.BlockSpec((1, L, bw), lambda b, h: (b, 0, cv // bw + h)),
                  pl.BlockSpec((1, L, 128), lambda b, h: (b, 0, cg // 128 + h)),
                  pl.BlockSpec((CONV_W, bw), lambda b, h: (0, h)),
                  pl.BlockSpec((1, bw), lambda b, h: (0, h)),
                  pl.BlockSpec((CONV_W, bw), lambda b, h: (0, nh // ML_HP + h)),
                  pl.BlockSpec((1, bw), lambda b, h: (0, nh // ML_HP + h)),
                  pl.BlockSpec((1, 1, 128), lambda b, h: (h, 0, 0))],
        out_specs=pl.BlockSpec((1, L, bw), lambda b, h: (b, 0, h)),
        out_shape=jax.ShapeDtypeStruct((bsz, L, nh * dh), F32),
        scratch_shapes=[pltpu.VMEM((L, bw), F32), pltpu.VMEM((L, bw), F32), pltpu.VMEM((nc, bw, CHUNK), BF16)],
        compiler_params=_cp(("parallel", "parallel")), name="mlstm",
    )(proj, proj, proj, proj, conv_w, conv_b.reshape(1, -1), conv_w, conv_b.reshape(1, -1), gbl)


def kernel(x, c, ctx, c_ctx, mod_w, mod_b, norm1_g, norm2_g, even_w_in, even_w_out, lru_conv_w, lru_conv_b, lru_wa, lru_ba, lru_wx, lru_bx, lru_lam, na_q_g, na_k_g, na_rpb, odd_w_in, odd_w_out, ssd_conv_w, ssd_conv_b, ssd_dt_bias, ssd_a_log, ssd_d, ssd_norm_g, ml_conv_w, ml_conv_b, ml_gate_b, ml_norm_g, moe_router_g, moe_router_e, moe_w1, moe_w3, moe_w2):
    bsz, S, d = x.shape
    lc = ctx.shape[1]
    assert lc == TM and S % TM == 0 and mod_w.shape[0] == 2
    nt = (lc + S) // TM
    L = lc + S

    cc = jnp.zeros((8, d), F32).at[0].set(c_ctx).at[1:1 + bsz].set(c)
    mod_all = _modulation(cc, mod_w, mod_b)

    def mod_for(l):
        m = mod_all[l].reshape(8, 6, d)
        return jnp.stack([jnp.broadcast_to(m[0], (bsz, 6, d)), m[1:1 + bsz]], axis=1)

    def router_w(l):
        w = jnp.zeros((d, 128), F32).at[:, :EXPERT_LANE0].set(moe_router_g[l]) \
            .at[:, EXPERT_LANE0:EXPERT_LANE0 + N_EXPERTS].set(moe_router_e[l])
        hi = w.astype(BF16)
        return jnp.concatenate([hi, (w - hi.astype(F32)).astype(BF16)], axis=1)

    xx = (ctx, x)

    mod0 = mod_for(0)
    proj = _inproj(xx, mod0, norm1_g[0], even_w_in[0].astype(BF16))
    lw = lru_conv_w.shape[-1]
    r = _lru(proj, lru_conv_w[0], lru_conv_b[0], lru_wa[0], lru_ba[0], lru_wx[0], lru_bx[0], lru_lam[0])
    a = _na(proj, na_q_g[0], na_k_g[0], na_rpb[0], col0=2 * lw)
    x1, h2, gate = _outproj([(r, 0, lw), (proj, 1, lw), (a, 0, lw)], xx, even_w_out[0].astype(BF16), mod0,
                            norm2_g[0], router_w(0), even=True, tile0=0, ntiles=nt)
    ew = (moe_w1.astype(BF16), moe_w3.astype(BF16), moe_w2.astype(BF16))
    moe0 = _moe(h2.reshape(bsz * L, d), gate.reshape(bsz * L, 128), 0, *ew).reshape(bsz, L, d)

    mod1 = mod_for(1)
    sw = ssd_d.shape[-1] * 64
    xbc = ssd_conv_w.shape[-1]
    mw = ml_norm_g.shape[-1]
    w = odd_w_in[0]
    o = np.cumsum([0, sw, xbc, 2 * ssd_d.shape[-1], mw, mw, mw, mw])
    ndt = 2 * ssd_d.shape[-1]
    dt_blk = jnp.concatenate([w[:, o[2]:o[3]], jnp.zeros((d, 128 - ndt), F32)], axis=1)
    gate_blks = [jnp.concatenate([w[:, o[7] + np.asarray(blk)], jnp.zeros((d, 128 - len(blk)), F32)], axis=1)
                 for blk in _ml_gate_columns(ml_gate_b.shape[-1])]
    w_odd = jnp.concatenate([w[:, :o[2]], w[:, o[3]:o[7]], dt_blk] + gate_blks, axis=1).astype(BF16)
    cz, cxs = 0, sw
    cB, cC = cxs + sw, cxs + sw + (xbc - sw) // 2
    cq = sw + xbc
    ck, cv, co, csm = cq + mw, cq + 2 * mw, cq + 3 * mw, cq + 4 * mw
    x0, proj1 = _inproj(x1, mod1, norm1_g[1], w_odd, prev=(moe0, mod0))
    ys = _ssd(proj1, ssd_conv_w[0], ssd_conv_b[0], ssd_dt_bias[0], ssd_a_log[0], ssd_d[0], (cxs, cB, cC, csm))
    hm = _mlstm(proj1, ml_conv_w[0], ml_conv_b[0], ml_gate_b[0], (cq, ck, cv, csm + 128))
    x2, h2b, gate1 = _outproj([(ys, 0, sw), (proj1, cz // sw, sw), (hm, 0, mw), (proj1, co // mw, mw),
                               (ssd_norm_g[0].reshape(1, sw), 0, sw), (ml_norm_g[0].reshape(1, mw), 0, mw)],
                              x0, odd_w_out[0].astype(BF16), mod1, norm2_g[1], router_w(1),
                              even=False, tile0=1, ntiles=nt - 1)
    out = _moe(h2b.reshape(bsz * S, d), gate1.reshape(bsz * S, 128), 1, *ew, resid=(x2.reshape(bsz * S, d), mod1))
    return out.reshape(bsz, S, d)
```

```python
import functools
import math

import jax
import jax.numpy as jnp
import numpy as np
from jax import lax
from jax.experimental import pallas as pl
from jax.experimental.pallas import tpu as pltpu

F32 = jnp.float32
BF16 = jnp.bfloat16
HI = lax.Precision.HIGHEST

EPS = 1e-6
NEG = -1e30
GRID_W = 64
CONV_W = 4
LRU_C = 8.0
TM = 256
CHUNK = 128
NA_RQ = 4
NA_RK = 12
N_EXPERTS = 16
EXPERT_LANE0 = 4
VMEM_LIMIT = 56 * 1024 * 1024


def _cp(sem, vmem=VMEM_LIMIT):
    return pltpu.CompilerParams(dimension_semantics=sem, vmem_limit_bytes=vmem)


def _sigmoid(x):
    return jax.nn.sigmoid(x)


def _silu(x):
    return x * jax.nn.sigmoid(x)


def _softplus(x):
    return jnp.maximum(x, 0.0) + jnp.log(1.0 + jnp.exp(-jnp.abs(x)))


def _gelu_tanh(x):
    return 0.5 * x * (1.0 + jnp.tanh(math.sqrt(2.0 / math.pi) * (x + 0.044715 * (x * x * x))))


def _rms(x, axis=-1):
    return x * lax.rsqrt(jnp.mean(x * x, axis=axis, keepdims=True) + EPS)


def _dot(a, b):
    return jnp.dot(a.astype(BF16), b.astype(BF16), preferred_element_type=F32)


def _dot_hi(a, b):
    return jnp.dot(a, b, precision=HI, preferred_element_type=F32)


def _split3(x):
    x1 = x.astype(BF16)
    r = x - x1.astype(F32)
    x2 = r.astype(BF16)
    x3 = (r - x2.astype(F32)).astype(BF16)
    return x1, x2, x3


def _dot_sel(sel, x, dims=(((1,), (0,)), ((), ()))):
    s = sel.astype(BF16)
    x1, x2, x3 = _split3(x)
    d = lambda xi: lax.dot_general(s, xi, dims, preferred_element_type=F32)
    return (d(x3) + d(x2)) + d(x1)


def _dot_sel_r(x, sel):
    s = sel.astype(BF16)
    x1, x2, x3 = _split3(x)
    d = lambda xi: jnp.dot(xi, s, preferred_element_type=F32)
    return (d(x3) + d(x2)) + d(x1)


_NT = (((1,), (1,)), ((), ()))


def _interleave(gens):
    results = [None] * len(gens)
    live = list(range(len(gens)))
    while live:
        for i in list(live):
            try:
                next(gens[i])
            except StopIteration as stop:
                results[i] = stop.value
                live.remove(i)
    return results


def _dot_nt(a, b):
    return lax.dot_general(a.astype(BF16), b.astype(BF16), (((1,), (1,)), ((), ())),
                           preferred_element_type=F32)


def _mod_kernel(c_ref, w_ref, b_ref, o_ref):
    c = c_ref[...]
    o_ref[0] = _dot_hi(_silu(c), w_ref[0]) + b_ref[0]


def _modulation(cc, mod_w, mod_b):
    depth, d, n = mod_w.shape
    tn = 1536
    return pl.pallas_call(
        _mod_kernel,
        grid=(depth, n // tn),
        in_specs=[pl.BlockSpec((8, d), lambda l, j: (0, 0)),
                  pl.BlockSpec((1, d, tn), lambda l, j: (l, 0, j)),
                  pl.BlockSpec((1, 1, tn), lambda l, j: (l, 0, j))],
        out_specs=pl.BlockSpec((1, 8, tn), lambda l, j: (l, 0, j)),
        out_shape=jax.ShapeDtypeStruct((depth, 8, n), F32),
        compiler_params=_cp(("arbitrary", "arbitrary")),
        name="adaln_mod",
    )(cc, mod_w, mod_b.reshape(depth, 1, n))


def _inproj_kernel(*refs, fuse_prev):
    if fuse_prev:
        x_ref, mo_ref, pmod_ref, mod_ref, g_ref, w_ref, xo_ref, p_ref = refs
        x = x_ref[0] + pmod_ref[0, 0][5:6, :] * mo_ref[0]
        xo_ref[0] = x
    else:
        c_ref, x_ref, mod_ref, g_ref, w_ref, p_ref = refs
        x = _pick_segment(c_ref, x_ref)
    mod = mod_ref[0, 0]
    h = _rms(x) * g_ref[...] * (1.0 + mod[1:2, :]) + mod[0:1, :]
    p_ref[0] = _dot(h, w_ref[...])


def _seg_map(b, i):
    return (b, jnp.minimum(i, 1), 0, 0)


def _segment_specs(d):
    return [pl.BlockSpec((1, TM, d), lambda b, i: (b, 0, 0)),
            pl.BlockSpec((1, TM, d), lambda b, i: (b, jnp.maximum(i - 1, 0), 0))]


def _pick_segment(c_ref, x_ref):
    return jnp.where(pl.program_id(1) == 0, c_ref[0], x_ref[0])


def _inproj(x, mod, g, w, prev=None):
    if prev is None:
        ctx, lat = x
        bsz, S, d = lat.shape
        L = TM + S
    else:
        bsz, L, d = x.shape
    n = w.shape[1]
    nt = L // TM
    tok = pl.BlockSpec((1, TM, d), lambda b, i: (b, i, 0))
    modspec = pl.BlockSpec((1, 1, 6, d), _seg_map)
    tail = [modspec, pl.BlockSpec((1, d), lambda b, i: (0, 0)), pl.BlockSpec((d, n), lambda b, i: (0, 0))]
    pspec = pl.BlockSpec((1, TM, n), lambda b, i: (b, i, 0))
    pshape = jax.ShapeDtypeStruct((bsz, L, n), F32)
    if prev is None:
        return pl.pallas_call(
            functools.partial(_inproj_kernel, fuse_prev=False),
            grid=(bsz, nt), in_specs=_segment_specs(d) + tail, out_specs=pspec, out_shape=pshape,
            compiler_params=_cp(("parallel", "arbitrary")), name="inproj",
        )(ctx, lat, mod, g.reshape(1, d), w)
    moe_out, pmod = prev
    return pl.pallas_call(
        functools.partial(_inproj_kernel, fuse_prev=True),
        grid=(bsz, nt), in_specs=[tok, tok, modspec] + tail,
        out_specs=[tok, pspec], out_shape=[jax.ShapeDtypeStruct(x.shape, F32), pshape],
        compiler_params=_cp(("parallel", "parallel")), name="inproj_res",
    )(x, moe_out, pmod, mod, g.reshape(1, d), w)


def _conv_tile(ref, i, nt, cw, cb, width=TM):
    L = nt * width
    t0 = pl.multiple_of(i * width, width)
    cur = ref[0, pl.ds(t0, width), :]
    prev = ref[0, pl.ds(pl.multiple_of(jnp.maximum(t0 - 8, 0), 8), 8), :]
    nxt = ref[0, pl.ds(pl.multiple_of(jnp.minimum(t0 + width, L - 8), 8), 8), :]
    return _conv_vals(cur, prev, nxt, i, nt, cw, cb, width, first_lat=TM // width)


def _conv_vals(cur, prev, nxt, i, nt, cw, cb, width, first_lat):
    prev = jnp.where((i != 0) & (i != first_lat), prev, 0.0)
    nxt = jnp.where((i != first_lat - 1) & (i != nt - 1), nxt, 0.0)
    cat = jnp.concatenate([prev, cur, nxt], axis=0)
    return (cw[0:1] * cat[6:6 + width] + cw[1:2] * cat[7:7 + width] + cw[2:3] * cur
            + cw[3:4] * cat[9:9 + width] + cb)


def _lru_kernel(ux_ref, cw_ref, cb_ref, gw_ref, gb_ref, lam_ref, o_ref, *, nt):
    cw = cw_ref[...]
    cb = cb_ref[...]
    lam = lam_ref[0]
    sp = _softplus(-lam)
    row = lax.broadcasted_iota(jnp.int32, (TM, 128), 0) & 7

    def gates(i, d):
        xl = _conv_tile(ux_ref, i, nt, cw, cb)
        g = _dot(xl, gw_ref[0, d]) + gb_ref[0, d]
        r = _sigmoid(g[:, :128])
        ig = _sigmoid(g[:, 128:])
        log_a = -LRU_C * r * sp[d:d + 1]
        a = jnp.exp(log_a)
        u = jnp.sqrt(1.0 - a * a) * (ig * xl)
        return a, u

    def scan_tile(i, d, carry, accumulate):
        a, u = gates(i, d)
        rev = d == 1
        for k in (1, 2, 4):
            sh = TM - k if rev else k
            ok = (row < 8 - k) if rev else (row >= k)
            ash = pltpu.roll(a, sh, 0)
            ush = pltpu.roll(u, sh, 0)
            u = jnp.where(ok, u + a * ush, u)
            a = jnp.where(ok, a * ash, a)
        t0 = i * TM
        groups = range(TM // 8)
        for s in (reversed(groups) if rev else groups):
            h = u[s * 8:(s + 1) * 8] + a[s * 8:(s + 1) * 8] * carry
            carry = h[0:1] if rev else h[7:8]
            idx = pl.ds(pl.multiple_of(t0 + s * 8, 8), 8)
            if accumulate:
                o_ref[0, idx, :] += h
            else:
                o_ref[0, idx, :] = h
        return carry

    zero = jnp.zeros((1, 128), F32)
    lax.fori_loop(0, nt, lambda i, c: scan_tile(i, 0, c, False), zero)
    lax.fori_loop(0, nt, lambda j, c: scan_tile(jnp.where(j == 0, 0, nt - j), 1, c, True), zero)


def _lru(proj, conv_w, conv_b, wa, ba, wx, bx, lam):
    bsz, L, _ = proj.shape
    nt = L // TM
    width = conv_w.shape[1]
    ng = width // 128

    def blockdiag(w):
        w = w.reshape(2, ng, 2, 64, 64)
        z = jnp.zeros_like(w[:, :, 0])
        top = jnp.concatenate([w[:, :, 0], z], axis=-1)
        bot = jnp.concatenate([z, w[:, :, 1]], axis=-1)
        return jnp.concatenate([top, bot], axis=-2)
    gw = jnp.concatenate([blockdiag(wa), blockdiag(wx)], axis=-1).transpose(1, 0, 2, 3).astype(BF16)
    gb = jnp.concatenate([ba.reshape(2, ng, 1, 128), bx.reshape(2, ng, 1, 128)], axis=-1).transpose(1, 0, 2, 3)
    lam_g = lam.reshape(2, ng, 128).transpose(1, 0, 2)
    return pl.pallas_call(
        functools.partial(_lru_kernel, nt=nt),
        grid=(bsz, ng),
        in_specs=[pl.BlockSpec((1, L, 128), lambda b, c: (b, 0, c)),
                  pl.BlockSpec((CONV_W, 128), lambda b, c: (0, c)),
                  pl.BlockSpec((1, 128), lambda b, c: (0, c)),
                  pl.BlockSpec((1, 2, 128, 256), lambda b, c: (c, 0, 0, 0)),
                  pl.BlockSpec((1, 2, 1, 256), lambda b, c: (c, 0, 0, 0)),
                  pl.BlockSpec((1, 2, 128), lambda b, c: (c, 0, 0))],
        out_specs=pl.BlockSpec((1, L, 128), lambda b, c: (b, 0, c)),
        out_shape=jax.ShapeDtypeStruct((bsz, L, width), F32),
        compiler_params=_cp(("parallel", "parallel")), name="rglru",
    )(proj, conv_w, conv_b.reshape(1, width), gw, gb, lam_g)


def _na_bias_table(rpb, rows):
    nh = rpb.shape[0]
    win_c = (rpb.shape[2] + 1) // 2
    qc = np.arange(GRID_W)[:, None]
    kc = np.arange(GRID_W)[None, :]
    cstart = np.clip(qc - win_c // 2, 0, GRID_W - win_c)
    col_ok = (kc >= cstart) & (kc < cstart + win_c)
    dcol = np.clip(kc - qc + (win_c - 1), 0, 2 * win_c - 2)
    oc = (np.arange(2 * win_c - 1)[:, None, None] == dcol[None]).astype(np.float32)
    ct = jnp.where(jnp.asarray(col_ok), jnp.einsum('hrc,cqk->hrqk', rpb.astype(F32), jnp.asarray(oc), precision=HI),
                   NEG)
    return pl.pallas_call(
        functools.partial(_na_bias_kernel, rows=rows),
        grid=(nh,),
        in_specs=[pl.BlockSpec((1,) + ct.shape[1:], lambda h: (h, 0, 0, 0))],
        out_specs=pl.BlockSpec((3, 1, TM, NA_RK * GRID_W), lambda h: (0, h, 0, 0)),
        out_shape=jax.ShapeDtypeStruct((3, nh, TM, NA_RK * GRID_W), F32),
        compiler_params=_cp(("parallel",)), name="nbr_bias",
    )(ct)


def _na_bias_kernel(ct_ref, o_ref, *, rows):
    win_r = (ct_ref.shape[1] + 1) // 2
    neg = jnp.full((GRID_W, GRID_W), NEG, F32)
    for p, (r0, w0) in enumerate(((0, 0), (2 * NA_RQ, NA_RQ), (rows - NA_RQ, rows - NA_RK))):
        for a in range(NA_RQ):
            r = r0 + a
            rstart = min(max(r - win_r // 2, 0), rows - win_r)
            tiles = [ct_ref[0, w0 + b - r + win_r - 1] if rstart <= w0 + b < rstart + win_r else neg
                     for b in range(NA_RK)]
            for bp in range(NA_RK // 2):
                o_ref[p, 0, a * GRID_W:(a + 1) * GRID_W, bp * 2 * GRID_W:(bp + 1) * 2 * GRID_W] = (
                    jnp.concatenate(tiles[2 * bp:2 * bp + 2], axis=1))


def _pair_rms(x, lo):
    x2 = x * x
    s0 = jnp.sum(jnp.where(lo, x2, 0.0), axis=-1, keepdims=True)
    s1 = jnp.sum(jnp.where(lo, 0.0, x2), axis=-1, keepdims=True)
    return x * lax.rsqrt(jnp.where(lo, s0, s1) * (2.0 / x.shape[-1]) + EPS)


def _na_kernel(q_ref, k_ref, v_ref, bias0_ref, bias1_ref, qg_ref, kg_ref, o_ref, kn_s, vb_s, *, rows, hd):
    j = pl.program_id(2)
    L = k_ref.shape[1]
    nb = L // TM
    nkeys = NA_RK * GRID_W
    lo = lax.broadcasted_iota(jnp.int32, (1, 2 * hd), 1) < hd

    @pl.when(j == 0)
    def _prep():
        def body(t, _):
            idx = pl.ds(pl.multiple_of(t * TM, TM), TM)
            kn_s[idx, :] = (_pair_rms(k_ref[0, idx, :], lo) * kg_ref[...]).astype(BF16)
            vb_s[idx, :] = v_ref[0, idx, :].astype(BF16)
            return 0
        lax.fori_loop(0, nb, body, 0)

    kctx = kn_s[0:TM, :]
    vctx = vb_s[0:TM, :]
    qscale = qg_ref[...] * hd ** -0.5

    def head(q, bias, kwin, vwin):
        s_c = _dot_nt(q, kctx)
        if kwin is not None:
            s_w = _dot_nt(q, kwin) + bias
        yield
        m = jnp.max(s_c, axis=-1, keepdims=True)
        if kwin is not None:
            m = jnp.maximum(m, jnp.max(s_w, axis=-1, keepdims=True))
            p_w = jnp.exp(s_w - m)
        p_c = jnp.exp(s_c - m)
        den = jnp.sum(p_c, axis=-1, keepdims=True)
        num = _dot(p_c, vctx)
        if kwin is not None:
            den = den + jnp.sum(p_w, axis=-1, keepdims=True)
            num = num + _dot(p_w, vwin)
        yield
        return num / den

    def tile_chains(slot, bias_ref, tile):
        qn = _pair_rms(q_ref[0, slot * TM:(slot + 1) * TM, :], lo) * qscale
        q_h = [jnp.where(lo, qn, 0.0).astype(BF16), jnp.where(lo, 0.0, qn).astype(BF16)]
        if tile is None:
            return [head(q_h[hh], None, None, None) for hh in range(2)]
        w0 = jnp.clip((tile - 1) * NA_RQ - NA_RQ, 0, rows - NA_RK)
        start = pl.multiple_of(TM + w0 * GRID_W, GRID_W)
        kwin = kn_s[pl.ds(start, nkeys), :]
        vwin = vb_s[pl.ds(start, nkeys), :]
        return [head(q_h[hh], bias_ref[0, hh], kwin, vwin) for hh in range(2)]

    def run(chains):
        outs = _interleave(chains)
        for slot in range(len(outs) // 2):
            o_ref[0, slot * TM:(slot + 1) * TM, :] = jnp.where(lo, outs[2 * slot], outs[2 * slot + 1])

    last = (nb - 1) // 2

    @pl.when(j == 0)
    def _():
        run(tile_chains(0, None, None) + tile_chains(1, bias1_ref, 1))

    @pl.when((j > 0) & (j < last))
    def _():
        run(tile_chains(0, bias0_ref, 2 * j) + tile_chains(1, bias1_ref, 2 * j + 1))

    @pl.when(j == last)
    def _():
        run(tile_chains(0, bias0_ref, 2 * j))


def _na(proj, q_g, k_g, rpb, col0):
    bsz, L, _ = proj.shape
    nh = rpb.shape[0]
    hd = q_g.shape[0]
    width = nh * hd
    rows = (L - TM) // GRID_W
    nb = L // TM
    bias = _na_bias_table(rpb, rows)
    qb, kb, vb = col0 // 128, (col0 + width) // 128, (col0 + 2 * width) // 128

    assert nb % 2 == 1

    def pat(t):
        return jnp.where(t <= 1, 0, jnp.where(t >= nb - 1, 2, 1))
    bias_spec = lambda slot: pl.BlockSpec((1, 2, TM, NA_RK * GRID_W), lambda b, h, j: (pat(2 * j + slot), h, 0, 0))
    return pl.pallas_call(
        functools.partial(_na_kernel, rows=rows, hd=hd),
        grid=(bsz, width // 128, (nb + 1) // 2),
        in_specs=[pl.BlockSpec((1, 2 * TM, 128), lambda b, h, j: (b, j, qb + h)),
                  pl.BlockSpec((1, L, 128), lambda b, h, j: (b, 0, kb + h)),
                  pl.BlockSpec((1, L, 128), lambda b, h, j: (b, 0, vb + h)),
                  bias_spec(0), bias_spec(1),
                  pl.BlockSpec((1, 2 * hd), lambda b, h, j: (0, 0)),
                  pl.BlockSpec((1, 2 * hd), lambda b, h, j: (0, 0))],
        out_specs=pl.BlockSpec((1, 2 * TM, 128), lambda b, h, j: (b, j, h)),
        out_shape=jax.ShapeDtypeStruct((bsz, L, width), F32),
        scratch_shapes=[pltpu.VMEM((L, 128), BF16), pltpu.VMEM((L, 128), BF16)],
        compiler_params=_cp(("parallel", "parallel", "arbitrary")), name="nbr_attn",
    )(proj, proj, proj, bias, bias, jnp.tile(q_g, 2).reshape(1, 2 * hd), jnp.tile(k_g, 2).reshape(1, 2 * hd))


def _route(lg):
    lane = lax.broadcasted_iota(jnp.int32, lg.shape, 1)
    lane_f = lane.astype(F32)
    is_g = lane < EXPERT_LANE0
    gl = jnp.where(is_g, lg, NEG)
    gmax = jnp.max(gl, axis=-1, keepdims=True)
    gsel = jnp.min(jnp.where(is_g & (gl == gmax), lane_f, 1e9), axis=-1, keepdims=True)
    g_w = 1.0 / jnp.sum(jnp.where(is_g, jnp.exp(gl - gmax), 0.0), axis=-1, keepdims=True)
    grp = ((lane - EXPERT_LANE0) >> 2).astype(F32)
    in_g = (lane >= EXPERT_LANE0) & (lane < EXPERT_LANE0 + N_EXPERTS) & (grp == gsel)
    el = jnp.where(in_g, lg, NEG)
    v1 = jnp.max(el, axis=-1, keepdims=True)
    i1 = jnp.min(jnp.where(in_g & (el == v1), lane_f, 1e9), axis=-1, keepdims=True)
    el2 = jnp.where(lane_f == i1, NEG, el)
    v2 = jnp.max(el2, axis=-1, keepdims=True)
    i2 = jnp.min(jnp.where(in_g & (lane_f != i1) & (el2 == v2), lane_f, 1e9), axis=-1, keepdims=True)
    t = jnp.exp(v2 - v1)
    w1 = g_w / (1.0 + t)
    w2 = g_w * t / (1.0 + t)
    return (jnp.where(lane_f == i1, w1, 0.0) + jnp.where(lane_f == i2, w2, 0.0)
            + jnp.where(lane_f == gsel, 1.0, 0.0))


def _outproj_kernel(*refs, even):
    if even:
        (r_ref, ug_ref, a_ref, c_ref, x_ref, w_ref, mod_ref, g2_ref, rw_ref, x1_ref, h2_ref, gate_ref) = refs
        is_ctx = pl.program_id(1) == 0
        x_rows = lambda rs: jnp.where(is_ctx, c_ref[0, rs, :], x_ref[0, rs, :])
    else:
        (ys_ref, z_ref, hm_ref, mo_ref, sg_ref, mg_ref, x_ref, w_ref, mod_ref, g2_ref, rw_ref,
         x1_ref, h2_ref, gate_ref) = refs
        x_rows = lambda rs: x_ref[0, rs, :]
    mod = mod_ref[0, 0]

    def part(rs):
        if even:
            y_in = jnp.concatenate([r_ref[0, rs, :] * _gelu_tanh(ug_ref[0, rs, :]), a_ref[0, rs, :]], axis=-1)
        else:
            ys = ys_ref[0, rs, :] * _silu(z_ref[0, rs, :])
            sg = sg_ref[...]
            mg = mg_ref[...]
            hm = hm_ref[0, rs, :]
            sig_o = _sigmoid(mo_ref[0, rs, :])
            gw = ys.shape[-1] // 2
            parts = [_rms(ys[:, g * gw:(g + 1) * gw]) * sg[:, g * gw:(g + 1) * gw] for g in range(2)]
            hw = 128
            parts += [_rms(hm[:, h * hw:(h + 1) * hw]) * mg[:, h * hw:(h + 1) * hw] * sig_o[:, h * hw:(h + 1) * hw]
                      for h in range(hm.shape[-1] // hw)]
            y_in = jnp.concatenate(parts, axis=-1)
        y = _dot(y_in, w_ref[...])
        yield
        x1 = x_rows(rs) + mod[2:3, :] * y
        x1_ref[0, rs, :] = x1
        h2 = _rms(x1) * g2_ref[...] * (1.0 + mod[4:5, :]) + mod[3:4, :]
        h2_ref[0, rs, :] = h2.astype(BF16)
        hi = h2.astype(BF16)
        lo = (h2 - hi.astype(F32)).astype(BF16)
        lg2 = jnp.dot(hi, rw_ref[...], preferred_element_type=F32)
        lg1 = jnp.dot(lo, rw_ref[:, 0:128], preferred_element_type=F32)
        yield
        gate_ref[0, rs, :] = _route(lg2[:, 0:128] + lg2[:, 128:256] + lg1)

    nparts = 2
    rows = TM // nparts
    _interleave([part(pl.ds(p * rows, rows)) for p in range(nparts)])


def _outproj(mix_inputs, x, w, mod, g2, rw, even, tile0, ntiles):
    xs = list(x) if even else [x]
    bsz, _, d = xs[-1].shape
    specs, args = [], []
    for arr, cb, wdt in mix_inputs:
        if arr.ndim == 3:
            specs.append(pl.BlockSpec((1, TM, wdt), lambda b, i, cb=cb: (b, i + tile0, cb)))
        else:
            specs.append(pl.BlockSpec((1, wdt), lambda b, i: (0, 0)))
        args.append(arr)
    specs += _segment_specs(d) if even else [pl.BlockSpec((1, TM, d), lambda b, i: (b, i + tile0, 0))]
    specs += [pl.BlockSpec(w.shape, lambda b, i: (0, 0)),
              pl.BlockSpec((1, 1, 6, d), lambda b, i: (b, jnp.minimum(i + tile0, 1), 0, 0)),
              pl.BlockSpec((1, d), lambda b, i: (0, 0)),
              pl.BlockSpec(rw.shape, lambda b, i: (0, 0))]
    args += xs + [w, mod, g2.reshape(1, d), rw]
    lo = ntiles * TM
    return pl.pallas_call(
        functools.partial(_outproj_kernel, even=even),
        grid=(bsz, ntiles), in_specs=specs,
        out_specs=[pl.BlockSpec((1, TM, d), lambda b, i: (b, i, 0)),
                   pl.BlockSpec((1, TM, d), lambda b, i: (b, i, 0)),
                   pl.BlockSpec((1, TM, 128), lambda b, i: (b, i, 0))],
        out_shape=[jax.ShapeDtypeStruct((bsz, lo, d), F32), jax.ShapeDtypeStruct((bsz, lo, d), BF16),
                   jax.ShapeDtypeStruct((bsz, lo, 128), F32)],
        compiler_params=_cp(("parallel", "parallel")), name="outproj_even" if even else "outproj_odd",
    )(*args)


MOE_CH = 128
GROUP_SIZE = 4
MOE_VMEM_LIMIT = 62 * 1024 * 1024
N_GROUPS = N_EXPERTS // GROUP_SIZE


def _moe_kernel(*refs, residual):
    if residual:
        x_ref, g_ref, w1_ref, w3_ref, w2_ref, x1_ref, mod_ref, o_ref, xs_s, ys_s, gs_s, pt_s, plan_s = refs
    else:
        x_ref, g_ref, w1_ref, w3_ref, w2_ref, o_ref, xs_s, ys_s, gs_s, pt_s, plan_s = refs
    grp = pl.program_id(1)
    tb = x_ref.shape[0]
    nch = xs_s.shape[0] // MOE_CH

    @pl.when(grp == 0)
    def _plan():
        g = g_ref[...]
        lane = lax.broadcasted_iota(jnp.int32, g.shape, 1)
        oh = jnp.where(lane < N_GROUPS, g, 0.0)
        earlier = (lax.broadcasted_iota(jnp.int32, (tb, tb), 0) > lax.broadcasted_iota(jnp.int32, (tb, tb), 1))
        rank = jnp.dot(jnp.where(earlier, 1.0, 0.0).astype(BF16), oh.astype(BF16),
                       preferred_element_type=F32)
        cnt = jnp.sum(oh, axis=0, keepdims=True)
        lane1 = lax.broadcasted_iota(jnp.int32, (1, 128), 1)
        off = jnp.int32(0)
        offv = jnp.zeros((1, 128), F32)
        for gi in range(N_GROUPS):
            n = jnp.sum(jnp.where(lane1 == gi, cnt, 0.0)).astype(jnp.int32)
            nchunks = (n + (MOE_CH - 1)) // MOE_CH
            plan_s[gi] = off // MOE_CH
            plan_s[N_GROUPS + gi] = nchunks
            offv = offv + jnp.where(lane1 == gi, off.astype(F32), 0.0)
            off = off + nchunks * MOE_CH
        pos_col = jnp.sum(oh * (rank + offv), axis=1, keepdims=True)
        posb = jnp.broadcast_to(pos_col, (tb, 128))
        pos_row = jnp.concatenate([posb[i * 128:(i + 1) * 128, :].T[0:1, :] for i in range(tb // 128)], axis=1)
        x = x_ref[...]
        g_hi = g.astype(BF16)
        g_lo = (g - g_hi.astype(F32)).astype(BF16)
        lane_c = lax.broadcasted_iota(jnp.int32, (tb, MOE_CH), 1).astype(F32)
        row_c = lax.broadcasted_iota(jnp.int32, (MOE_CH, tb), 0).astype(F32)
        for c in range(nch):
            sl = slice(c * MOE_CH, (c + 1) * MOE_CH)
            pt_s[:, sl] = jnp.where(pos_col == lane_c + float(c * MOE_CH), 1.0, 0.0).astype(BF16)
            p = jnp.where(row_c + float(c * MOE_CH) == pos_row, 1.0, 0.0).astype(BF16)
            xs_s[sl, :] = jnp.dot(p, x, preferred_element_type=F32).astype(BF16)
            gs_s[sl, :] = (jnp.dot(p, g_hi, preferred_element_type=F32)
                           + jnp.dot(p, g_lo, preferred_element_type=F32))
        ys_s[...] = jnp.zeros_like(ys_s)

    c0 = plan_s[grp]
    nchunks = plan_s[N_GROUPS + grp]

    def ffn(chunk0, nrows):
        rows = pl.ds(pl.multiple_of(chunk0 * MOE_CH, MOE_CH), nrows)
        xs = xs_s[rows, :]
        gs = gs_s[rows, :]
        lane_g = lax.broadcasted_iota(jnp.int32, (nrows, 128), 1)
        acc = None
        for k in range(GROUP_SIZE):
            a = jnp.dot(xs, w1_ref[0, k], preferred_element_type=F32)
            b = jnp.dot(xs, w3_ref[0, k], preferred_element_type=F32)
            y = _dot(_silu(a) * b, w2_ref[0, k])
            ge = jnp.sum(jnp.where(lane_g == grp * GROUP_SIZE + k + EXPERT_LANE0, gs, 0.0), axis=-1, keepdims=True)
            acc = ge * y if acc is None else acc + ge * y
        ys_s[rows, :] = acc.astype(BF16)

    def pair(p, _):
        ffn(c0 + 2 * p, 2 * MOE_CH)
        return 0
    lax.fori_loop(0, nchunks // 2, pair, 0)

    @pl.when(nchunks % 2 == 1)
    def _():
        ffn(c0 + nchunks - 1, MOE_CH)

    @pl.when(grp == N_GROUPS - 1)
    def _combine():
        out = jnp.dot(pt_s[...], ys_s[...], preferred_element_type=F32)
        if residual:
            out = x1_ref[...] + mod_ref[0, 0][5:6, :] * out
        o_ref[...] = out


def _moe(h2, gate, layer, w1, w3, w2, resid=None):
    t, d = h2.shape
    _, ne, _, ff = w1.shape
    tb = math.gcd(t, 1024)
    npad = tb + N_GROUPS * MOE_CH
    assert ne == N_EXPERTS
    extra_specs, extra_args = [], []
    if resid is not None:
        x1, mod = resid
        per_batch = t // mod.shape[0] // tb
        extra_specs = [pl.BlockSpec((tb, d), lambda i, e: (i, 0)),
                       pl.BlockSpec((1, 1, 6, d), lambda i, e: (i // per_batch, 1, 0, 0))]
        extra_args = [x1, mod]
    return pl.pallas_call(
        functools.partial(_moe_kernel, residual=resid is not None),
        grid=(t // tb, N_GROUPS),
        in_specs=[pl.BlockSpec((tb, d), lambda i, e: (i, 0)),
                  pl.BlockSpec((tb, 128), lambda i, e: (i, 0)),
                  pl.BlockSpec((1, GROUP_SIZE, d, ff), lambda i, e: (layer, e, 0, 0)),
                  pl.BlockSpec((1, GROUP_SIZE, d, ff), lambda i, e: (layer, e, 0, 0)),
                  pl.BlockSpec((1, GROUP_SIZE, ff, d), lambda i, e: (layer, e, 0, 0))] + extra_specs,
        out_specs=pl.BlockSpec((tb, d), lambda i, e: (i, 0)),
        out_shape=jax.ShapeDtypeStruct((t, d), F32),
        scratch_shapes=[pltpu.VMEM((npad, d), BF16), pltpu.VMEM((npad, d), BF16), pltpu.VMEM((npad, 128), F32),
                        pltpu.VMEM((tb, npad), BF16), pltpu.SMEM((2 * N_GROUPS,), jnp.int32)],
        compiler_params=_cp(("parallel", "arbitrary"), MOE_VMEM_LIMIT), name="moe",
    )(h2, gate, w1, w3, w2, *extra_args)


def _tri(q, rev):
    r = lax.broadcasted_iota(jnp.int32, (q, q), 0)
    c = lax.broadcasted_iota(jnp.int32, (q, q), 1)
    return (c >= r) if rev else (c <= r)


def _ssd_kernel(xs_ref, b_ref, c_ref, dt_ref, cwx_ref, cbx_ref, cwb_ref, cbb_ref, cwc_ref, cbc_ref,
                dtb_ref, alog_ref, dsk_ref, sel_ref, o_ref, xc_s, bc_s, cc_s, dt_s, bct_s, *, nc):
    q = CHUNK
    first_lat = TM // q

    dtb = dtb_ref[...]

    def prep(i, _):
        t0 = pl.multiple_of(i * q, q)
        idx = pl.ds(t0, q)
        for src, cw, cb, dst in ((xs_ref, cwx_ref, cbx_ref, xc_s), (b_ref, cwb_ref, cbb_ref, bc_s),
                                 (c_ref, cwc_ref, cbc_ref, cc_s)):
            L = nc * q
            cur = src[0, idx, :]
            prev = src[0, pl.ds(pl.multiple_of(jnp.maximum(t0 - 8, 0), 8), 8), :]
            nxt = src[0, pl.ds(pl.multiple_of(jnp.minimum(t0 + q, L - 8), 8), 8), :]
            dst[idx, :] = _silu(_conv_vals(cur, prev, nxt, i, nc, cw[...], cb[...], q, first_lat))
        dt_s[idx, :] = _softplus(dt_ref[0, idx, :] + dtb)
        o_ref[0, idx, :] = dsk_ref[...] * xc_s[idx, :]
        bct_s[i] = bc_s[idx, :].T.astype(BF16)
        return 0
    lax.fori_loop(0, nc, prep, 0)

    a_lane = jnp.broadcast_to(-jnp.exp(alog_ref[...]), (8, 128))
    a_e = [_dot_hi(a_lane, sel_ref[d, 0])[0:1, :] for d in (0, 1)]
    lane256 = lax.broadcasted_iota(jnp.int32, (q, 256), 1) // 64
    r8 = lax.broadcasted_iota(jnp.int32, (8, 256), 0)
    l8 = lax.broadcasted_iota(jnp.int32, (8, 256), 1)
    head_rows = (l8 == r8 * 64).astype(F32)

    def chain(d, j, h):
        rev = d == 1
        causal = _tri(q, rev)
        tri = causal.astype(F32)
        sel = sel_ref[d, 0]
        last = 0 if rev else q - 1
        if rev:
            ci = jnp.where(j < first_lat, first_lat - 1 - j, nc - 1 - (j - first_lat))
        else:
            ci = j
        idx = pl.ds(pl.multiple_of(ci * q, q), q)
        dt_e = _dot_sel_r(dt_s[idx, :], sel)
        xc = xc_s[idx, :]
        bc = bc_s[idx, :]
        cc = cc_s[idx, :]
        cb = _dot_nt(cc, bc)
        y_off = _dot(cc, h)
        yield
        cum_e = _dot_sel(tri, dt_e * a_e[d])
        dtx = xc * dt_e
        yield
        cum_rows = _dot_sel(head_rows, cum_e, _NT)
        tot_e = cum_e[last:last + 1, :]
        states = _dot(bct_s[ci], dtx * jnp.exp(tot_e - cum_e))
        yield
        y = jnp.exp(cum_e) * y_off
        for r in range(4):
            seg = cum_e[:, r * 64:r * 64 + 1] - cum_rows[r:r + 1, :]
            m = cb * jnp.exp(jnp.where(causal, seg, NEG))
            y = y + _dot(m, jnp.where(lane256 == r, dtx, 0.0))
            yield
        o_ref[0, idx, :] += y
        return jnp.exp(tot_e) * h + states

    def chunk(j, hs):
        return tuple(_interleave([chain(0, j, hs[0]), chain(1, j, hs[1])]))
    h0 = jnp.zeros((128, 256), F32)
    lax.fori_loop(0, nc, chunk, (h0, h0))


def _ssd(proj, conv_w, conv_b, dt_bias, a_log, d_skip, cols):
    bsz, L, _ = proj.shape
    nc = L // CHUNK
    cx, cbm, ccm, cdt = cols
    nh = d_skip.shape[0]
    dtb = jnp.zeros((1, 128), F32).at[0, :2 * nh].set(dt_bias.reshape(-1))
    alog = jnp.zeros((1, 128), F32).at[0, :2 * nh].set(a_log.reshape(-1))
    dsk = jnp.repeat(d_skip.astype(F32), 64).reshape(1, nh * 64)
    sel = np.zeros((2, 2, 128, 256), np.float32)
    for d in range(2):
        for g in range(2):
            for r in range(4):
                sel[d, g, d * nh + g * 4 + r, r * 64:(r + 1) * 64] = 1.0
    cw_specs = []
    for width, off in ((256, 0), (128, 512), (128, 768)):
        cw_specs += [pl.BlockSpec((CONV_W, width), lambda b, g, off=off, width=width: (0, off // width + g)),
                     pl.BlockSpec((1, width), lambda b, g, off=off, width=width: (0, off // width + g))]
    return pl.pallas_call(
        functools.partial(_ssd_kernel, nc=nc),
        grid=(bsz, 2),
        in_specs=[pl.BlockSpec((1, L, 256), lambda b, g: (b, 0, cx // 256 + g)),
                  pl.BlockSpec((1, L, 128), lambda b, g: (b, 0, cbm // 128 + g)),
                  pl.BlockSpec((1, L, 128), lambda b, g: (b, 0, ccm // 128 + g)),
                  pl.BlockSpec((1, L, 128), lambda b, g: (b, 0, cdt // 128))] + cw_specs + [
                  pl.BlockSpec((1, 128), lambda b, g: (0, 0)),
                  pl.BlockSpec((1, 128), lambda b, g: (0, 0)),
                  pl.BlockSpec((1, 256), lambda b, g: (0, g)),
                  pl.BlockSpec((2, 1, 128, 256), lambda b, g: (0, g, 0, 0))],
        out_specs=pl.BlockSpec((1, L, 256), lambda b, g: (b, 0, g)),
        out_shape=jax.ShapeDtypeStruct((bsz, L, nh * 64), F32),
        scratch_shapes=[pltpu.VMEM((L, 256), F32), pltpu.VMEM((L, 128), F32), pltpu.VMEM((L, 128), F32),
                        pltpu.VMEM((L, 128), F32), pltpu.VMEM((nc, 128, CHUNK), BF16)],
        compiler_params=_cp(("parallel", "parallel")), name="ssd",
    )(proj, proj, proj, proj, conv_w, conv_b.reshape(1, -1), conv_w, conv_b.reshape(1, -1),
      conv_w, conv_b.reshape(1, -1), dtb, alog, dsk, jnp.asarray(sel))


ML_HP = 2


def _ml_gate_lanes(hh, d):
    li = (hh * 2 + d) * 2
    return li, li + 1


def _ml_gate_columns(nh):
    cols = []
    for hp in range(nh // ML_HP):
        blk = [None] * (4 * ML_HP)
        for hh in range(ML_HP):
            for d in range(2):
                for t, lane in enumerate(_ml_gate_lanes(hh, d)):
                    blk[lane] = (d * 2 + t) * nh + hp * ML_HP + hh
        cols.append(blk)
    return cols


def _mlstm_kernel(q_ref, k_ref, v_ref, g_ref, cwq_ref, cbq_ref, cwk_ref, cbk_ref, gb_ref, o_ref,
                  qc_s, kc_s, kct_s, *, nc, dh):
    q = CHUNK
    first_lat = TM // q

    def prep(i, _):
        t0 = pl.multiple_of(i * q, q)
        idx = pl.ds(t0, q)
        L = nc * q
        for src, cw, cb, dst, mul in ((q_ref, cwq_ref, cbq_ref, qc_s, 1.0), (k_ref, cwk_ref, cbk_ref, kc_s, dh ** -0.5)):
            cur = src[0, idx, :]
            prev = src[0, pl.ds(pl.multiple_of(jnp.maximum(t0 - 8, 0), 8), 8), :]
            nxt = src[0, pl.ds(pl.multiple_of(jnp.minimum(t0 + q, L - 8), 8), 8), :]
            dst[idx, :] = _silu(_conv_vals(cur, prev, nxt, i, nc, cw[...], cb[...], q, first_lat)) * mul
        for hh in range(ML_HP):
            kct_s[i, hh * dh:(hh + 1) * dh, :] = kc_s[idx, hh * dh:(hh + 1) * dh].T.astype(BF16)
        return 0
    lax.fori_loop(0, nc, prep, 0)

    gb = gb_ref[0]
    lane = lax.broadcasted_iota(jnp.int32, (q, 128), 1)
    eye8 = (lax.broadcasted_iota(jnp.int32, (8, 128), 0) == lax.broadcasted_iota(jnp.int32, (8, 128), 1)).astype(F32)
    o_ref[...] = jnp.zeros_like(o_ref)
    heads = range(ML_HP)

    def direction(d, j, states):
        rev = d == 1
        causal = _tri(q, rev)
        last = 0 if rev else q - 1
        if rev:
            ci = jnp.where(j < first_lat, first_lat - 1 - j, nc - 1 - (j - first_lat))
        else:
            ci = j
        idx = pl.ds(pl.multiple_of(ci * q, q), q)
        g = g_ref[0, idx, :] + gb
        logf = jnp.minimum(g, 0.0) - jnp.log(1.0 + jnp.exp(-jnp.abs(g)))
        is_f = functools.reduce(jnp.logical_or, [lane == _ml_gate_lanes(hh, d)[1] for hh in heads])
        cum = _dot_sel(jnp.where(causal, 1.0, 0.0), jnp.where(is_f, logf, 0.0))
        qc = [qc_s[idx, hh * dh:(hh + 1) * dh] for hh in heads]
        kc = [kc_s[idx, hh * dh:(hh + 1) * dh] for hh in heads]
        vc = [v_ref[0, idx, hh * dh:(hh + 1) * dh] for hh in heads]
        kct = [kct_s[ci, hh * dh:(hh + 1) * dh, :] for hh in heads]
        qk = [_dot(qc[hh], kct[hh]) for hh in heads]
        inter_c = [_dot(qc[hh], states[hh][0]) for hh in heads]
        yield
        rows = _dot_sel(eye8, jnp.where(is_f, cum, g), _NT)
        yield
        w, w_s, g_in, m_t, keep, m_new, wv, upd = [], [], [], [], [], [], [], []
        for hh in heads:
            li, lf = _ml_gate_lanes(hh, d)
            m_prev = states[hh][2]
            i_col, i_row = g[:, li:li + 1], rows[li:li + 1, :]
            b_col, b_row = cum[:, lf:lf + 1], rows[lf:lf + 1, :]
            dlog = jnp.where(causal, b_col - b_row + i_row, NEG)
            inter = b_col + m_prev
            m_t.append(jnp.maximum(jnp.max(dlog, axis=-1, keepdims=True), inter))
            w.append(jnp.exp(dlog - m_t[hh]) * qk[hh])
            g_in.append(jnp.exp(inter - m_t[hh]))
            b_end = b_col[last:last + 1, :]
            g_s = b_end - b_col + i_col
            m_new.append(jnp.maximum(jnp.max(g_s, axis=0, keepdims=True), b_end + m_prev))
            w_s.append(jnp.exp(g_s - m_new[hh]))
            keep.append(jnp.exp(b_end + m_prev - m_new[hh]))
            wv.append(_dot(w[hh], vc[hh]))
            upd.append(_dot(kct[hh], w_s[hh] * vc[hh]))
        yield
        new_states = []
        for hh in heads:
            c_prev, n_prev, _ = states[hh]
            num = wv[hh] + g_in[hh] * inter_c[hh]
            den = (jnp.sum(w[hh], axis=-1, keepdims=True)
                   + g_in[hh] * jnp.sum(qc[hh] * n_prev, axis=-1, keepdims=True))
            o_ref[0, idx, hh * dh:(hh + 1) * dh] += num / jnp.maximum(jnp.abs(den), jnp.exp(-m_t[hh]))
            new_states.append((keep[hh] * c_prev + upd[hh],
                               keep[hh] * n_prev + jnp.sum(w_s[hh] * kc[hh], axis=0, keepdims=True), m_new[hh]))
        return tuple(new_states)

    def chunk(j, states):
        return tuple(_interleave([direction(d, j, states[d]) for d in (0, 1)]))
    init = (jnp.zeros((dh, dh), F32), jnp.zeros((1, dh), F32), jnp.full((1, 1), NEG, F32))
    lax.fori_loop(0, nc, chunk, tuple(tuple(init for _ in heads) for _ in (0, 1)))


def _mlstm(proj, conv_w, conv_b, gate_b, cols):
    bsz, L, _ = proj.shape
    nc = L // CHUNK
    cq, ck, cv, cg = cols
    nh = gate_b.shape[-1]
    dh = conv_w.shape[1] // (2 * nh)
    bw = ML_HP * dh
    gcols = np.asarray(_ml_gate_columns(nh))
    gbl = jnp.zeros((nh // ML_HP, 1, 128), F32).at[:, 0, :gcols.shape[1]].set(gate_b.reshape(-1)[gcols])
    return pl.pallas_call(
        functools.partial(_mlstm_kernel, nc=nc, dh=dh),
        grid=(bsz, nh // ML_HP),
        in_specs=[pl.BlockSpec((1, L, bw), lambda b, h: (b, 0, cq // bw + h)),
                  pl.BlockSpec((1, L, bw), lambda b, h: (b, 0, ck // bw + h)),
                  pl.BlockSpec((1, L, bw), lambda b, h: (b, 0, cv // bw + h)),
                  pl.BlockSpec((1, L, 128), lambda b, h: (b, 0, cg // 128 + h)),
                  pl.BlockSpec((CONV_W, bw), lambda b, h: (0, h)),
                  pl.BlockSpec((1, bw), lambda b, h: (0, h)),
                  pl.BlockSpec((CONV_W, bw), lambda b, h: (0, nh // ML_HP + h)),
                  pl.BlockSpec((1, bw), lambda b, h: (0, nh // ML_HP + h)),
                  pl.BlockSpec((1, 1, 128), lambda b, h: (h, 0, 0))],
        out_specs=pl.BlockSpec((1, L, bw), lambda b, h: (b, 0, h)),
        out_shape=jax.ShapeDtypeStruct((bsz, L, nh * dh), F32),
        scratch_shapes=[pltpu.VMEM((L, bw), F32), pltpu.VMEM((L, bw), F32), pltpu.VMEM((nc, bw, CHUNK), BF16)],
        compiler_params=_cp(("parallel", "parallel")), name="mlstm",
    )(proj, proj, proj, proj, conv_w, conv_b.reshape(1, -1), conv_w, conv_b.reshape(1, -1), gbl)


def kernel(x, c, ctx, c_ctx, mod_w, mod_b, norm1_g, norm2_g, even_w_in, even_w_out, lru_conv_w, lru_conv_b, lru_wa, lru_ba, lru_wx, lru_bx, lru_lam, na_q_g, na_k_g, na_rpb, odd_w_in, odd_w_out, ssd_conv_w, ssd_conv_b, ssd_dt_bias, ssd_a_log, ssd_d, ssd_norm_g, ml_conv_w, ml_conv_b, ml_gate_b, ml_norm_g, moe_router_g, moe_router_e, moe_w1, moe_w3, moe_w2):
    bsz, S, d = x.shape
    lc = ctx.shape[1]
    assert lc == TM and S % TM == 0 and mod_w.shape[0] == 2
    nt = (lc + S) // TM
    L = lc + S

    cc = jnp.zeros((8, d), F32).at[0].set(c_ctx).at[1:1 + bsz].set(c)
    mod_all = _modulation(cc, mod_w, mod_b)

    def mod_for(l):
        m = mod_all[l].reshape(8, 6, d)
        return jnp.stack([jnp.broadcast_to(m[0], (bsz, 6, d)), m[1:1 + bsz]], axis=1)

    def router_w(l):
        w = jnp.zeros((d, 128), F32).at[:, :EXPERT_LANE0].set(moe_router_g[l]) \
            .at[:, EXPERT_LANE0:EXPERT_LANE0 + N_EXPERTS].set(moe_router_e[l])
        hi = w.astype(BF16)
        return jnp.concatenate([hi, (w - hi.astype(F32)).astype(BF16)], axis=1)

    xx = (ctx, x)

    mod0 = mod_for(0)
    proj = _inproj(xx, mod0, norm1_g[0], even_w_in[0].astype(BF16))
    lw = lru_conv_w.shape[-1]
    r = _lru(proj, lru_conv_w[0], lru_conv_b[0], lru_wa[0], lru_ba[0], lru_wx[0], lru_bx[0], lru_lam[0])
    a = _na(proj, na_q_g[0], na_k_g[0], na_rpb[0], col0=2 * lw)
    x1, h2, gate = _outproj([(r, 0, lw), (proj, 1, lw), (a, 0, lw)], xx, even_w_out[0].astype(BF16), mod0,
                            norm2_g[0], router_w(0), even=True, tile0=0, ntiles=nt)
    ew = (moe_w1.astype(BF16), moe_w3.astype(BF16), moe_w2.astype(BF16))
    moe0 = _moe(h2.reshape(bsz * L, d), gate.reshape(bsz * L, 128), 0, *ew).reshape(bsz, L, d)

    mod1 = mod_for(1)
    sw = ssd_d.shape[-1] * 64
    xbc = ssd_conv_w.shape[-1]
    mw = ml_norm_g.shape[-1]
    w = odd_w_in[0]
    o = np.cumsum([0, sw, xbc, 2 * ssd_d.shape[-1], mw, mw, mw, mw])
    ndt = 2 * ssd_d.shape[-1]
    dt_blk = jnp.concatenate([w[:, o[2]:o[3]], jnp.zeros((d, 128 - ndt), F32)], axis=1)
    gate_blks = [jnp.concatenate([w[:, o[7] + np.asarray(blk)], jnp.zeros((d, 128 - len(blk)), F32)], axis=1)
                 for blk in _ml_gate_columns(ml_gate_b.shape[-1])]
    w_odd = jnp.concatenate([w[:, :o[2]], w[:, o[3]:o[7]], dt_blk] + gate_blks, axis=1).astype(BF16)
    cz, cxs = 0, sw
    cB, cC = cxs + sw, cxs + sw + (xbc - sw) // 2
    cq = sw + xbc
    ck, cv, co, csm = cq + mw, cq + 2 * mw, cq + 3 * mw, cq + 4 * mw
    x0, proj1 = _inproj(x1, mod1, norm1_g[1], w_odd, prev=(moe0, mod0))
    ys = _ssd(proj1, ssd_conv_w[0], ssd_conv_b[0], ssd_dt_bias[0], ssd_a_log[0], ssd_d[0], (cxs, cB, cC, csm))
    hm = _mlstm(proj1, ml_conv_w[0], ml_conv_b[0], ml_gate_b[0], (cq, ck, cv, csm + 128))
    x2, h2b, gate1 = _outproj([(ys, 0, sw), (proj1, cz // sw, sw), (hm, 0, mw), (proj1, co // mw, mw),
                               (ssd_norm_g[0].reshape(1, sw), 0, sw), (ml_norm_g[0].reshape(1, mw), 0, mw)],
                              x0, odd_w_out[0].astype(BF16), mod1, norm2_g[1], router_w(1),
                              even=False, tile0=1, ntiles=nt - 1)
    out = _moe(h2b.reshape(bsz * S, d), gate1.reshape(bsz * S, 128), 1, *ew, resid=(x2.reshape(bsz * S, d), mod1))
    return out.reshape(bsz, S, d)
```

```python
import functools
import math

import jax
import jax.numpy as jnp
import numpy as np
from jax import lax
from jax.experimental import pallas as pl
from jax.experimental.pallas import tpu as pltpu

F32 = jnp.float32
BF16 = jnp.bfloat16
HI = lax.Precision.HIGHEST

EPS = 1e-6
NEG = -1e30
GRID_W = 64
CONV_W = 4
LRU_C = 8.0
TM = 256
CHUNK = 256
NA_RQ = 4
NA_RK = 12
N_EXPERTS = 16
EXPERT_LANE0 = 4
VMEM_LIMIT = 56 * 1024 * 1024


def _cp(sem, vmem=VMEM_LIMIT):
    return pltpu.CompilerParams(dimension_semantics=sem, vmem_limit_bytes=vmem)


def _sigmoid(x):
    return jax.nn.sigmoid(x)


def _silu(x):
    return x * jax.nn.sigmoid(x)


def _softplus(x):
    return jnp.maximum(x, 0.0) + jnp.log(1.0 + jnp.exp(-jnp.abs(x)))


def _gelu_tanh(x):
    return 0.5 * x * (1.0 + jnp.tanh(math.sqrt(2.0 / math.pi) * (x + 0.044715 * (x * x * x))))


def _rms(x, axis=-1):
    return x * lax.rsqrt(jnp.mean(x * x, axis=axis, keepdims=True) + EPS)


def _dot(a, b):
    return jnp.dot(a.astype(BF16), b.astype(BF16), preferred_element_type=F32)


def _dot_hi(a, b):
    return jnp.dot(a, b, precision=HI, preferred_element_type=F32)


def _split3(x):
    x1 = x.astype(BF16)
    r = x - x1.astype(F32)
    x2 = r.astype(BF16)
    x3 = (r - x2.astype(F32)).astype(BF16)
    return x1, x2, x3


def _dot_sel(sel, x, dims=(((1,), (0,)), ((), ()))):
    s = sel.astype(BF16)
    x1, x2, x3 = _split3(x)
    d = lambda xi: lax.dot_general(s, xi, dims, preferred_element_type=F32)
    return (d(x3) + d(x2)) + d(x1)


def _dot_sel_r(x, sel):
    s = sel.astype(BF16)
    x1, x2, x3 = _split3(x)
    d = lambda xi: jnp.dot(xi, s, preferred_element_type=F32)
    return (d(x3) + d(x2)) + d(x1)


_NT = (((1,), (1,)), ((), ()))


def _interleave(gens):
    results = [None] * len(gens)
    live = list(range(len(gens)))
    while live:
        for i in list(live):
            try:
                next(gens[i])
            except StopIteration as stop:
                results[i] = stop.value
                live.remove(i)
    return results


def _dot_nt(a, b):
    return lax.dot_general(a.astype(BF16), b.astype(BF16), (((1,), (1,)), ((), ())),
                           preferred_element_type=F32)


def _mod_kernel(c_ref, w_ref, b_ref, o_ref):
    c = c_ref[...]
    o_ref[0] = _dot_hi(_silu(c), w_ref[0]) + b_ref[0]


def _modulation(cc, mod_w, mod_b):
    depth, d, n = mod_w.shape
    tn = 1536
    return pl.pallas_call(
        _mod_kernel,
        grid=(depth, n // tn),
        in_specs=[pl.BlockSpec((8, d), lambda l, j: (0, 0)),
                  pl.BlockSpec((1, d, tn), lambda l, j: (l, 0, j)),
                  pl.BlockSpec((1, 1, tn), lambda l, j: (l, 0, j))],
        out_specs=pl.BlockSpec((1, 8, tn), lambda l, j: (l, 0, j)),
        out_shape=jax.ShapeDtypeStruct((depth, 8, n), F32),
        compiler_params=_cp(("arbitrary", "arbitrary")),
        name="adaln_mod",
    )(cc, mod_w, mod_b.reshape(depth, 1, n))


def _inproj_kernel(*refs, fuse_prev, n_f32):
    refs = list(refs)
    s_ref = refs.pop() if n_f32 else None
    if fuse_prev:
        x_ref, mo_ref, pmod_ref, mod_ref, g_ref, w_ref, xo_ref, p_ref = refs
        x = x_ref[0] + pmod_ref[0, 0][5:6, :] * mo_ref[0]
        xo_ref[0] = x
    else:
        c_ref, x_ref, mod_ref, g_ref, w_ref, p_ref = refs
        x = _pick_segment(c_ref, x_ref)
    mod = mod_ref[0, 0]
    h = _rms(x) * g_ref[...] * (1.0 + mod[1:2, :]) + mod[0:1, :]
    p = _dot(h, w_ref[...])
    n_main = p.shape[-1] - n_f32
    p_ref[0] = p[:, :n_main].astype(BF16)
    if n_f32:
        s_ref[0] = p[:, n_main:]


def _seg_map(b, i):
    return (b, jnp.minimum(i, 1), 0, 0)


def _segment_specs(d):
    return [pl.BlockSpec((1, TM, d), lambda b, i: (b, 0, 0)),
            pl.BlockSpec((1, TM, d), lambda b, i: (b, jnp.maximum(i - 1, 0), 0))]


def _pick_segment(c_ref, x_ref):
    return jnp.where(pl.program_id(1) == 0, c_ref[0], x_ref[0])


def _inproj(x, mod, g, w, prev=None, n_f32=0):
    if prev is None:
        ctx, lat = x
        bsz, S, d = lat.shape
        L = TM + S
    else:
        bsz, L, d = x.shape
    n = w.shape[1]
    nt = L // TM
    tok = pl.BlockSpec((1, TM, d), lambda b, i: (b, i, 0))
    modspec = pl.BlockSpec((1, 1, 6, d), _seg_map)
    tail = [modspec, pl.BlockSpec((1, d), lambda b, i: (0, 0)), pl.BlockSpec((d, n), lambda b, i: (0, 0))]
    pspecs = [pl.BlockSpec((1, TM, n - n_f32), lambda b, i: (b, i, 0))]
    pshapes = [jax.ShapeDtypeStruct((bsz, L, n - n_f32), BF16)]
    if n_f32:
        pspecs.append(pl.BlockSpec((1, TM, n_f32), lambda b, i: (b, i, 0)))
        pshapes.append(jax.ShapeDtypeStruct((bsz, L, n_f32), F32))
    if prev is None:
        return pl.pallas_call(
            functools.partial(_inproj_kernel, fuse_prev=False, n_f32=n_f32),
            grid=(bsz, nt), in_specs=_segment_specs(d) + tail, out_specs=pspecs, out_shape=pshapes,
            compiler_params=_cp(("parallel", "arbitrary")), name="inproj",
        )(ctx, lat, mod, g.reshape(1, d), w)
    moe_out, pmod = prev
    return pl.pallas_call(
        functools.partial(_inproj_kernel, fuse_prev=True, n_f32=n_f32),
        grid=(bsz, nt), in_specs=[tok, tok, modspec] + tail,
        out_specs=[tok] + pspecs, out_shape=[jax.ShapeDtypeStruct(x.shape, F32)] + pshapes,
        compiler_params=_cp(("parallel", "parallel")), name="inproj_res",
    )(x, moe_out, pmod, mod, g.reshape(1, d), w)


def _conv_tile(ref, i, nt, cw, cb, width=TM):
    L = nt * width
    t0 = pl.multiple_of(i * width, width)
    cur = ref[0, pl.ds(t0, width), :].astype(F32)
    prev = ref[0, pl.ds(pl.multiple_of(jnp.maximum(t0 - 16, 0), 16), 16), :].astype(F32)[8:16]
    nxt = ref[0, pl.ds(pl.multiple_of(jnp.minimum(t0 + width, L - 16), 16), 16), :].astype(F32)[0:8]
    return _conv_vals(cur, prev, nxt, i, nt, cw, cb, width, first_lat=TM // width)


def _conv_vals(cur, prev, nxt, i, nt, cw, cb, width, first_lat):
    prev = jnp.where((i != 0) & (i != first_lat), prev, 0.0)
    nxt = jnp.where((i != first_lat - 1) & (i != nt - 1), nxt, 0.0)
    cat = jnp.concatenate([prev, cur, nxt], axis=0)
    return (cw[0:1] * cat[6:6 + width] + cw[1:2] * cat[7:7 + width] + cw[2:3] * cur
            + cw[3:4] * cat[9:9 + width] + cb)


def _lru_kernel(ux_ref, cw_ref, cb_ref, gw_ref, gb_ref, lam_ref, o_ref, *, nt):
    cw = cw_ref[...]
    cb = cb_ref[...]
    lam = lam_ref[0]
    sp = _softplus(-lam)
    row = lax.broadcasted_iota(jnp.int32, (TM, 128), 0) & 7

    def gates(i, d):
        xl = _conv_tile(ux_ref, i, nt, cw, cb)
        g = _dot(xl, gw_ref[0, d]) + gb_ref[0, d]
        r = _sigmoid(g[:, :128])
        ig = _sigmoid(g[:, 128:])
        log_a = -LRU_C * r * sp[d:d + 1]
        a = jnp.exp(log_a)
        u = jnp.sqrt(1.0 - a * a) * (ig * xl)
        return a, u

    def scan_tile(i, d, carry, accumulate):
        a, u = gates(i, d)
        rev = d == 1
        for k in (1, 2, 4):
            sh = TM - k if rev else k
            ok = (row < 8 - k) if rev else (row >= k)
            ash = pltpu.roll(a, sh, 0)
            ush = pltpu.roll(u, sh, 0)
            u = jnp.where(ok, u + a * ush, u)
            a = jnp.where(ok, a * ash, a)
        t0 = i * TM
        groups = range(TM // 8)
        for s in (reversed(groups) if rev else groups):
            h = u[s * 8:(s + 1) * 8] + a[s * 8:(s + 1) * 8] * carry
            carry = h[0:1] if rev else h[7:8]
            idx = pl.ds(pl.multiple_of(t0 + s * 8, 8), 8)
            if accumulate:
                o_ref[0, idx, :] += h
            else:
                o_ref[0, idx, :] = h
        return carry

    zero = jnp.zeros((1, 128), F32)
    lax.fori_loop(0, nt, lambda i, c: scan_tile(i, 0, c, False), zero)
    lax.fori_loop(0, nt, lambda j, c: scan_tile(jnp.where(j == 0, 0, nt - j), 1, c, True), zero)


def _lru(proj, conv_w, conv_b, wa, ba, wx, bx, lam):
    bsz, L, _ = proj.shape
    nt = L // TM
    width = conv_w.shape[1]
    ng = width // 128

    def blockdiag(w):
        w = w.reshape(2, ng, 2, 64, 64)
        z = jnp.zeros_like(w[:, :, 0])
        top = jnp.concatenate([w[:, :, 0], z], axis=-1)
        bot = jnp.concatenate([z, w[:, :, 1]], axis=-1)
        return jnp.concatenate([top, bot], axis=-2)
    gw = jnp.concatenate([blockdiag(wa), blockdiag(wx)], axis=-1).transpose(1, 0, 2, 3).astype(BF16)
    gb = jnp.concatenate([ba.reshape(2, ng, 1, 128), bx.reshape(2, ng, 1, 128)], axis=-1).transpose(1, 0, 2, 3)
    lam_g = lam.reshape(2, ng, 128).transpose(1, 0, 2)
    return pl.pallas_call(
        functools.partial(_lru_kernel, nt=nt),
        grid=(bsz, ng),
        in_specs=[pl.BlockSpec((1, L, 128), lambda b, c: (b, 0, c)),
                  pl.BlockSpec((CONV_W, 128), lambda b, c: (0, c)),
                  pl.BlockSpec((1, 128), lambda b, c: (0, c)),
                  pl.BlockSpec((1, 2, 128, 256), lambda b, c: (c, 0, 0, 0)),
                  pl.BlockSpec((1, 2, 1, 256), lambda b, c: (c, 0, 0, 0)),
                  pl.BlockSpec((1, 2, 128), lambda b, c: (c, 0, 0))],
        out_specs=pl.BlockSpec((1, L, 128), lambda b, c: (b, 0, c)),
        out_shape=jax.ShapeDtypeStruct((bsz, L, width), F32),
        compiler_params=_cp(("parallel", "parallel")), name="rglru",
    )(proj, conv_w, conv_b.reshape(1, width), gw, gb, lam_g)


def _na_bias_table(rpb, rows):
    nh = rpb.shape[0]
    win_c = (rpb.shape[2] + 1) // 2
    qc = np.arange(GRID_W)[:, None]
    kc = np.arange(GRID_W)[None, :]
    cstart = np.clip(qc - win_c // 2, 0, GRID_W - win_c)
    col_ok = (kc >= cstart) & (kc < cstart + win_c)
    dcol = np.clip(kc - qc + (win_c - 1), 0, 2 * win_c - 2)
    oc = (np.arange(2 * win_c - 1)[:, None, None] == dcol[None]).astype(np.float32)
    ct = jnp.where(jnp.asarray(col_ok), jnp.einsum('hrc,cqk->hrqk', rpb.astype(F32), jnp.asarray(oc), precision=HI),
                   NEG)
    return pl.pallas_call(
        functools.partial(_na_bias_kernel, rows=rows),
        grid=(nh,),
        in_specs=[pl.BlockSpec((1,) + ct.shape[1:], lambda h: (h, 0, 0, 0))],
        out_specs=pl.BlockSpec((3, 1, TM, NA_RK * GRID_W), lambda h: (0, h, 0, 0)),
        out_shape=jax.ShapeDtypeStruct((3, nh, TM, NA_RK * GRID_W), F32),
        compiler_params=_cp(("parallel",)), name="nbr_bias",
    )(ct)


def _na_bias_kernel(ct_ref, o_ref, *, rows):
    win_r = (ct_ref.shape[1] + 1) // 2
    neg = jnp.full((GRID_W, GRID_W), NEG, F32)
    for p, (r0, w0) in enumerate(((0, 0), (2 * NA_RQ, NA_RQ), (rows - NA_RQ, rows - NA_RK))):
        for a in range(NA_RQ):
            r = r0 + a
            rstart = min(max(r - win_r // 2, 0), rows - win_r)
            tiles = [ct_ref[0, w0 + b - r + win_r - 1] if rstart <= w0 + b < rstart + win_r else neg
                     for b in range(NA_RK)]
            for bp in range(NA_RK // 2):
                o_ref[p, 0, a * GRID_W:(a + 1) * GRID_W, bp * 2 * GRID_W:(bp + 1) * 2 * GRID_W] = (
                    jnp.concatenate(tiles[2 * bp:2 * bp + 2], axis=1))


def _pair_rms(x, lo):
    x2 = x * x
    s0 = jnp.sum(jnp.where(lo, x2, 0.0), axis=-1, keepdims=True)
    s1 = jnp.sum(jnp.where(lo, 0.0, x2), axis=-1, keepdims=True)
    return x * lax.rsqrt(jnp.where(lo, s0, s1) * (2.0 / x.shape[-1]) + EPS)


def _na_kernel(q_ref, k_ref, v_ref, bias0_ref, bias1_ref, qg_ref, kg_ref, o_ref, kn_s, vb_s, *, rows, hd):
    j = pl.program_id(2)
    L = k_ref.shape[1]
    nb = L // TM
    nkeys = NA_RK * GRID_W
    lo = lax.broadcasted_iota(jnp.int32, (1, 2 * hd), 1) < hd

    @pl.when(j == 0)
    def _prep():
        def body(t, _):
            idx = pl.ds(pl.multiple_of(t * TM, TM), TM)
            kn_s[idx, :] = (_pair_rms(k_ref[0, idx, :].astype(F32), lo) * kg_ref[...]).astype(BF16)
            vb_s[idx, :] = v_ref[0, idx, :].astype(BF16)
            return 0
        lax.fori_loop(0, nb, body, 0)

    kctx = kn_s[0:TM, :]
    vctx = vb_s[0:TM, :]
    qscale = qg_ref[...] * hd ** -0.5

    def head(q, bias, kwin, vwin):
        s_c = _dot_nt(q, kctx)
        if kwin is not None:
            s_w = _dot_nt(q, kwin) + bias
        yield
        m = jnp.max(s_c, axis=-1, keepdims=True)
        if kwin is not None:
            m = jnp.maximum(m, jnp.max(s_w, axis=-1, keepdims=True))
            p_w = jnp.exp(s_w - m)
        p_c = jnp.exp(s_c - m)
        den = jnp.sum(p_c, axis=-1, keepdims=True)
        num = _dot(p_c, vctx)
        if kwin is not None:
            den = den + jnp.sum(p_w, axis=-1, keepdims=True)
            num = num + _dot(p_w, vwin)
        yield
        return num / den

    def tile_chains(slot, bias_ref, tile):
        qn = _pair_rms(q_ref[0, slot * TM:(slot + 1) * TM, :].astype(F32), lo) * qscale
        q_h = [jnp.where(lo, qn, 0.0).astype(BF16), jnp.where(lo, 0.0, qn).astype(BF16)]
        if tile is None:
            return [head(q_h[hh], None, None, None) for hh in range(2)]
        w0 = jnp.clip((tile - 1) * NA_RQ - NA_RQ, 0, rows - NA_RK)
        start = pl.multiple_of(TM + w0 * GRID_W, GRID_W)
        kwin = kn_s[pl.ds(start, nkeys), :]
        vwin = vb_s[pl.ds(start, nkeys), :]
        return [head(q_h[hh], bias_ref[0, hh], kwin, vwin) for hh in range(2)]

    def run(chains):
        outs = _interleave(chains)
        for slot in range(len(outs) // 2):
            o_ref[0, slot * TM:(slot + 1) * TM, :] = jnp.where(lo, outs[2 * slot], outs[2 * slot + 1])

    last = (nb - 1) // 2

    @pl.when(j == 0)
    def _():
        run(tile_chains(0, None, None) + tile_chains(1, bias1_ref, 1))

    @pl.when((j > 0) & (j < last))
    def _():
        run(tile_chains(0, bias0_ref, 2 * j) + tile_chains(1, bias1_ref, 2 * j + 1))

    @pl.when(j == last)
    def _():
        run(tile_chains(0, bias0_ref, 2 * j))


def _na(proj, q_g, k_g, rpb, col0):
    bsz, L, _ = proj.shape
    nh = rpb.shape[0]
    hd = q_g.shape[0]
    width = nh * hd
    rows = (L - TM) // GRID_W
    nb = L // TM
    bias = _na_bias_table(rpb, rows)
    qb, kb, vb = col0 // 128, (col0 + width) // 128, (col0 + 2 * width) // 128

    assert nb % 2 == 1

    def pat(t):
        return jnp.where(t <= 1, 0, jnp.where(t >= nb - 1, 2, 1))
    bias_spec = lambda slot: pl.BlockSpec((1, 2, TM, NA_RK * GRID_W), lambda b, h, j: (pat(2 * j + slot), h, 0, 0))
    return pl.pallas_call(
        functools.partial(_na_kernel, rows=rows, hd=hd),
        grid=(bsz, width // 128, (nb + 1) // 2),
        in_specs=[pl.BlockSpec((1, 2 * TM, 128), lambda b, h, j: (b, j, qb + h)),
                  pl.BlockSpec((1, L, 128), lambda b, h, j: (b, 0, kb + h)),
                  pl.BlockSpec((1, L, 128), lambda b, h, j: (b, 0, vb + h)),
                  bias_spec(0), bias_spec(1),
                  pl.BlockSpec((1, 2 * hd), lambda b, h, j: (0, 0)),
                  pl.BlockSpec((1, 2 * hd), lambda b, h, j: (0, 0))],
        out_specs=pl.BlockSpec((1, 2 * TM, 128), lambda b, h, j: (b, j, h)),
        out_shape=jax.ShapeDtypeStruct((bsz, L, width), F32),
        scratch_shapes=[pltpu.VMEM((L, 128), BF16), pltpu.VMEM((L, 128), BF16)],
        compiler_params=_cp(("parallel", "parallel", "arbitrary")), name="nbr_attn",
    )(proj, proj, proj, bias, bias, jnp.tile(q_g, 2).reshape(1, 2 * hd), jnp.tile(k_g, 2).reshape(1, 2 * hd))


def _route(lg):
    lane = lax.broadcasted_iota(jnp.int32, lg.shape, 1)
    lane_f = lane.astype(F32)
    is_g = lane < EXPERT_LANE0
    gl = jnp.where(is_g, lg, NEG)
    gmax = jnp.max(gl, axis=-1, keepdims=True)
    gsel = jnp.min(jnp.where(is_g & (gl == gmax), lane_f, 1e9), axis=-1, keepdims=True)
    g_w = 1.0 / jnp.sum(jnp.where(is_g, jnp.exp(gl - gmax), 0.0), axis=-1, keepdims=True)
    grp = ((lane - EXPERT_LANE0) >> 2).astype(F32)
    in_g = (lane >= EXPERT_LANE0) & (lane < EXPERT_LANE0 + N_EXPERTS) & (grp == gsel)
    el = jnp.where(in_g, lg, NEG)
    v1 = jnp.max(el, axis=-1, keepdims=True)
    i1 = jnp.min(jnp.where(in_g & (el == v1), lane_f, 1e9), axis=-1, keepdims=True)
    el2 = jnp.where(lane_f == i1, NEG, el)
    v2 = jnp.max(el2, axis=-1, keepdims=True)
    i2 = jnp.min(jnp.where(in_g & (lane_f != i1) & (el2 == v2), lane_f, 1e9), axis=-1, keepdims=True)
    t = jnp.exp(v2 - v1)
    w1 = g_w / (1.0 + t)
    w2 = g_w * t / (1.0 + t)
    return (jnp.where(lane_f == i1, w1, 0.0) + jnp.where(lane_f == i2, w2, 0.0)
            + jnp.where(lane_f == gsel, 1.0, 0.0))


def _outproj_kernel(*refs, even):
    if even:
        (r_ref, ug_ref, a_ref, c_ref, x_ref, w_ref, mod_ref, g2_ref, rw_ref, x1_ref, h2_ref, gate_ref) = refs
        is_ctx = pl.program_id(1) == 0
        x_rows = lambda rs: jnp.where(is_ctx, c_ref[0, rs, :], x_ref[0, rs, :])
    else:
        (ys_ref, z_ref, hm_ref, mo_ref, sg_ref, mg_ref, x_ref, w_ref, mod_ref, g2_ref, rw_ref,
         x1_ref, h2_ref, gate_ref) = refs
        x_rows = lambda rs: x_ref[0, rs, :]
    mod = mod_ref[0, 0]

    def part(rs):
        if even:
            y_in = jnp.concatenate([r_ref[0, rs, :] * _gelu_tanh(ug_ref[0, rs, :].astype(F32)), a_ref[0, rs, :]],
                                   axis=-1)
        else:
            ys = ys_ref[0, rs, :] * _silu(z_ref[0, rs, :].astype(F32))
            sg = sg_ref[...]
            mg = mg_ref[...]
            hm = hm_ref[0, rs, :]
            sig_o = _sigmoid(mo_ref[0, rs, :].astype(F32))
            gw = ys.shape[-1] // 2
            parts = [_rms(ys[:, g * gw:(g + 1) * gw]) * sg[:, g * gw:(g + 1) * gw] for g in range(2)]
            hw = 128
            parts += [_rms(hm[:, h * hw:(h + 1) * hw]) * mg[:, h * hw:(h + 1) * hw] * sig_o[:, h * hw:(h + 1) * hw]
                      for h in range(hm.shape[-1] // hw)]
            y_in = jnp.concatenate(parts, axis=-1)
        y = _dot(y_in, w_ref[...])
        yield
        x1 = x_rows(rs) + mod[2:3, :] * y
        x1_ref[0, rs, :] = x1
        h2 = _rms(x1) * g2_ref[...] * (1.0 + mod[4:5, :]) + mod[3:4, :]
        h2_ref[0, rs, :] = h2.astype(BF16)
        hi = h2.astype(BF16)
        lo = (h2 - hi.astype(F32)).astype(BF16)
        lg2 = jnp.dot(hi, rw_ref[...], preferred_element_type=F32)
        lg1 = jnp.dot(lo, rw_ref[:, 0:128], preferred_element_type=F32)
        yield
        gate_ref[0, rs, :] = _route(lg2[:, 0:128] + lg2[:, 128:256] + lg1)

    nparts = 2
    rows = TM // nparts
    _interleave([part(pl.ds(p * rows, rows)) for p in range(nparts)])


def _outproj(mix_inputs, x, w, mod, g2, rw, even, tile0, ntiles):
    xs = list(x) if even else [x]
    bsz, _, d = xs[-1].shape
    specs, args = [], []
    for arr, cb, wdt in mix_inputs:
        if arr.ndim == 3:
            specs.append(pl.BlockSpec((1, TM, wdt), lambda b, i, cb=cb: (b, i + tile0, cb)))
        else:
            specs.append(pl.BlockSpec((1, wdt), lambda b, i: (0, 0)))
        args.append(arr)
    specs += _segment_specs(d) if even else [pl.BlockSpec((1, TM, d), lambda b, i: (b, i + tile0, 0))]
    specs += [pl.BlockSpec(w.shape, lambda b, i: (0, 0)),
              pl.BlockSpec((1, 1, 6, d), lambda b, i: (b, jnp.minimum(i + tile0, 1), 0, 0)),
              pl.BlockSpec((1, d), lambda b, i: (0, 0)),
              pl.BlockSpec(rw.shape, lambda b, i: (0, 0))]
    args += xs + [w, mod, g2.reshape(1, d), rw]
    lo = ntiles * TM
    return pl.pallas_call(
        functools.partial(_outproj_kernel, even=even),
        grid=(bsz, ntiles), in_specs=specs,
        out_specs=[pl.BlockSpec((1, TM, d), lambda b, i: (b, i, 0)),
                   pl.BlockSpec((1, TM, d), lambda b, i: (b, i, 0)),
                   pl.BlockSpec((1, TM, 128), lambda b, i: (b, i, 0))],
        out_shape=[jax.ShapeDtypeStruct((bsz, lo, d), F32), jax.ShapeDtypeStruct((bsz, lo, d), BF16),
                   jax.ShapeDtypeStruct((bsz, lo, 128), F32)],
        compiler_params=_cp(("parallel", "parallel")), name="outproj_even" if even else "outproj_odd",
    )(*args)


MOE_CH = 128
GROUP_SIZE = 4
MOE_VMEM_LIMIT = 62 * 1024 * 1024
N_GROUPS = N_EXPERTS // GROUP_SIZE


def _moe_kernel(*refs, residual):
    if residual:
        x_ref, g_ref, w1_ref, w3_ref, w2_ref, x1_ref, mod_ref, o_ref, xs_s, ys_s, gs_s, pt_s, plan_s = refs
    else:
        x_ref, g_ref, w1_ref, w3_ref, w2_ref, o_ref, xs_s, ys_s, gs_s, pt_s, plan_s = refs
    grp = pl.program_id(1)
    tb = x_ref.shape[0]
    nch = xs_s.shape[0] // MOE_CH

    @pl.when(grp == 0)
    def _plan():
        g = g_ref[...]
        lane = lax.broadcasted_iota(jnp.int32, g.shape, 1)
        oh = jnp.where(lane < N_GROUPS, g, 0.0)
        earlier = (lax.broadcasted_iota(jnp.int32, (tb, tb), 0) > lax.broadcasted_iota(jnp.int32, (tb, tb), 1))
        rank = jnp.dot(jnp.where(earlier, 1.0, 0.0).astype(BF16), oh.astype(BF16),
                       preferred_element_type=F32)
        cnt = jnp.sum(oh, axis=0, keepdims=True)
        lane1 = lax.broadcasted_iota(jnp.int32, (1, 128), 1)
        off = jnp.int32(0)
        offv = jnp.zeros((1, 128), F32)
        for gi in range(N_GROUPS):
            n = jnp.sum(jnp.where(lane1 == gi, cnt, 0.0)).astype(jnp.int32)
            nchunks = (n + (MOE_CH - 1)) // MOE_CH
            plan_s[gi] = off // MOE_CH
            plan_s[N_GROUPS + gi] = nchunks
            offv = offv + jnp.where(lane1 == gi, off.astype(F32), 0.0)
            off = off + nchunks * MOE_CH
        pos_col = jnp.sum(oh * (rank + offv), axis=1, keepdims=True)
        posb = jnp.broadcast_to(pos_col, (tb, 128))
        pos_row = jnp.concatenate([posb[i * 128:(i + 1) * 128, :].T[0:1, :] for i in range(tb // 128)], axis=1)
        x = x_ref[...]
        g_hi = g.astype(BF16)
        g_lo = (g - g_hi.astype(F32)).astype(BF16)
        lane_c = lax.broadcasted_iota(jnp.int32, (tb, MOE_CH), 1).astype(F32)
        row_c = lax.broadcasted_iota(jnp.int32, (MOE_CH, tb), 0).astype(F32)
        for c in range(nch):
            sl = slice(c * MOE_CH, (c + 1) * MOE_CH)
            pt_s[:, sl] = jnp.where(pos_col == lane_c + float(c * MOE_CH), 1.0, 0.0).astype(BF16)
            p = jnp.where(row_c + float(c * MOE_CH) == pos_row, 1.0, 0.0).astype(BF16)
            xs_s[sl, :] = jnp.dot(p, x, preferred_element_type=F32).astype(BF16)
            gs_s[sl, :] = (jnp.dot(p, g_hi, preferred_element_type=F32)
                           + jnp.dot(p, g_lo, preferred_element_type=F32))
        ys_s[...] = jnp.zeros_like(ys_s)

    c0 = plan_s[grp]
    nchunks = plan_s[N_GROUPS + grp]

    def ffn(chunk0, nrows):
        rows = pl.ds(pl.multiple_of(chunk0 * MOE_CH, MOE_CH), nrows)
        xs = xs_s[rows, :]
        gs = gs_s[rows, :]
        lane_g = lax.broadcasted_iota(jnp.int32, (nrows, 128), 1)
        acc = None
        for k in range(GROUP_SIZE):
            a = jnp.dot(xs, w1_ref[0, k], preferred_element_type=F32)
            b = jnp.dot(xs, w3_ref[0, k], preferred_element_type=F32)
            y = _dot(_silu(a) * b, w2_ref[0, k])
            ge = jnp.sum(jnp.where(lane_g == grp * GROUP_SIZE + k + EXPERT_LANE0, gs, 0.0), axis=-1, keepdims=True)
            acc = ge * y if acc is None else acc + ge * y
        ys_s[rows, :] = acc.astype(BF16)

    def pair(p, _):
        ffn(c0 + 2 * p, 2 * MOE_CH)
        return 0
    lax.fori_loop(0, nchunks // 2, pair, 0)

    @pl.when(nchunks % 2 == 1)
    def _():
        ffn(c0 + nchunks - 1, MOE_CH)

    @pl.when(grp == N_GROUPS - 1)
    def _combine():
        out = jnp.dot(pt_s[...], ys_s[...], preferred_element_type=F32)
        if residual:
            out = x1_ref[...] + mod_ref[0, 0][5:6, :] * out
        o_ref[...] = out


def _moe(h2, gate, layer, w1, w3, w2, resid=None):
    t, d = h2.shape
    _, ne, _, ff = w1.shape
    tb = math.gcd(t, 1024)
    npad = tb + N_GROUPS * MOE_CH
    assert ne == N_EXPERTS
    extra_specs, extra_args = [], []
    if resid is not None:
        x1, mod = resid
        per_batch = t // mod.shape[0] // tb
        extra_specs = [pl.BlockSpec((tb, d), lambda i, e: (i, 0)),
                       pl.BlockSpec((1, 1, 6, d), lambda i, e: (i // per_batch, 1, 0, 0))]
        extra_args = [x1, mod]
    return pl.pallas_call(
        functools.partial(_moe_kernel, residual=resid is not None),
        grid=(t // tb, N_GROUPS),
        in_specs=[pl.BlockSpec((tb, d), lambda i, e: (i, 0)),
                  pl.BlockSpec((tb, 128), lambda i, e: (i, 0)),
                  pl.BlockSpec((1, GROUP_SIZE, d, ff), lambda i, e: (layer, e, 0, 0)),
                  pl.BlockSpec((1, GROUP_SIZE, d, ff), lambda i, e: (layer, e, 0, 0)),
                  pl.BlockSpec((1, GROUP_SIZE, ff, d), lambda i, e: (layer, e, 0, 0))] + extra_specs,
        out_specs=pl.BlockSpec((tb, d), lambda i, e: (i, 0)),
        out_shape=jax.ShapeDtypeStruct((t, d), F32),
        scratch_shapes=[pltpu.VMEM((npad, d), BF16), pltpu.VMEM((npad, d), BF16), pltpu.VMEM((npad, 128), F32),
                        pltpu.VMEM((tb, npad), BF16), pltpu.SMEM((2 * N_GROUPS,), jnp.int32)],
        compiler_params=_cp(("parallel", "arbitrary"), MOE_VMEM_LIMIT), name="moe",
    )(h2, gate, w1, w3, w2, *extra_args)


def _tri(q, rev):
    r = lax.broadcasted_iota(jnp.int32, (q, q), 0)
    c = lax.broadcasted_iota(jnp.int32, (q, q), 1)
    return (c >= r) if rev else (c <= r)


def _ssd_kernel(xs_ref, b_ref, c_ref, dt_ref, cwx_ref, cbx_ref, cwb_ref, cbb_ref, cwc_ref, cbc_ref,
                dtb_ref, alog_ref, dsk_ref, sel_ref, o_ref, xc_s, bc_s, cc_s, dt_s, bct_s, *, nc):
    q = CHUNK
    first_lat = TM // q

    dtb = dtb_ref[...]

    def prep(i, _):
        t0 = pl.multiple_of(i * q, q)
        idx = pl.ds(t0, q)
        for src, cw, cb, dst in ((xs_ref, cwx_ref, cbx_ref, xc_s), (b_ref, cwb_ref, cbb_ref, bc_s),
                                 (c_ref, cwc_ref, cbc_ref, cc_s)):
            dst[idx, :] = _silu(_conv_tile(src, i, nc, cw[...], cb[...], q))
        dt_s[idx, :] = _softplus(dt_ref[0, idx, :] + dtb)
        o_ref[0, idx, :] = dsk_ref[...] * xc_s[idx, :]
        bct_s[i] = bc_s[idx, :].T.astype(BF16)
        return 0
    lax.fori_loop(0, nc, prep, 0)

    a_lane = jnp.broadcast_to(-jnp.exp(alog_ref[...]), (8, 128))
    a_e = [_dot_hi(a_lane, sel_ref[d, 0])[0:1, :] for d in (0, 1)]
    lane256 = lax.broadcasted_iota(jnp.int32, (q, 256), 1) // 64
    r8 = lax.broadcasted_iota(jnp.int32, (8, 256), 0)
    l8 = lax.broadcasted_iota(jnp.int32, (8, 256), 1)
    head_rows = (l8 == r8 * 64).astype(F32)

    def chain(d, j, h):
        rev = d == 1
        causal = _tri(q, rev)
        tri = causal.astype(F32)
        sel = sel_ref[d, 0]
        last = 0 if rev else q - 1
        if rev:
            ci = jnp.where(j < first_lat, first_lat - 1 - j, nc - 1 - (j - first_lat))
        else:
            ci = j
        idx = pl.ds(pl.multiple_of(ci * q, q), q)
        dt_e = _dot_sel_r(dt_s[idx, :], sel)
        xc = xc_s[idx, :]
        bc = bc_s[idx, :]
        cc = cc_s[idx, :]
        cb = _dot_nt(cc, bc)
        y_off = _dot(cc, h)
        yield
        cum_e = _dot_sel(tri, dt_e * a_e[d])
        dtx = xc * dt_e
        yield
        cum_rows = _dot_sel(head_rows, cum_e, _NT)
        tot_e = cum_e[last:last + 1, :]
        states = _dot(bct_s[ci], dtx * jnp.exp(tot_e - cum_e))
        yield
        y = jnp.exp(cum_e) * y_off
        for r in range(4):
            seg = cum_e[:, r * 64:r * 64 + 1] - cum_rows[r:r + 1, :]
            m = cb * jnp.exp(jnp.where(causal, seg, NEG))
            y = y + _dot(m, jnp.where(lane256 == r, dtx, 0.0))
            yield
        o_ref[0, idx, :] += y
        return jnp.exp(tot_e) * h + states

    def chunk(j, hs):
        return tuple(_interleave([chain(0, j, hs[0]), chain(1, j, hs[1])]))
    h0 = jnp.zeros((128, 256), F32)
    lax.fori_loop(0, nc, chunk, (h0, h0))


def _ssd(proj, lanes, conv_w, conv_b, dt_bias, a_log, d_skip, cols):
    bsz, L, _ = proj.shape
    nc = L // CHUNK
    cx, cbm, ccm, cdt = cols
    nh = d_skip.shape[0]
    dtb = jnp.zeros((1, 128), F32).at[0, :2 * nh].set(dt_bias.reshape(-1))
    alog = jnp.zeros((1, 128), F32).at[0, :2 * nh].set(a_log.reshape(-1))
    dsk = jnp.repeat(d_skip.astype(F32), 64).reshape(1, nh * 64)
    sel = np.zeros((2, 2, 128, 256), np.float32)
    for d in range(2):
        for g in range(2):
            for r in range(4):
                sel[d, g, d * nh + g * 4 + r, r * 64:(r + 1) * 64] = 1.0
    cw_specs = []
    for width, off in ((256, 0), (128, 512), (128, 768)):
        cw_specs += [pl.BlockSpec((CONV_W, width), lambda b, g, off=off, width=width: (0, off // width + g)),
                     pl.BlockSpec((1, width), lambda b, g, off=off, width=width: (0, off // width + g))]
    return pl.pallas_call(
        functools.partial(_ssd_kernel, nc=nc),
        grid=(bsz, 2),
        in_specs=[pl.BlockSpec((1, L, 256), lambda b, g: (b, 0, cx // 256 + g)),
                  pl.BlockSpec((1, L, 128), lambda b, g: (b, 0, cbm // 128 + g)),
                  pl.BlockSpec((1, L, 128), lambda b, g: (b, 0, ccm // 128 + g)),
                  pl.BlockSpec((1, L, 128), lambda b, g: (b, 0, cdt // 128))] + cw_specs + [
                  pl.BlockSpec((1, 128), lambda b, g: (0, 0)),
                  pl.BlockSpec((1, 128), lambda b, g: (0, 0)),
                  pl.BlockSpec((1, 256), lambda b, g: (0, g)),
                  pl.BlockSpec((2, 1, 128, 256), lambda b, g: (0, g, 0, 0))],
        out_specs=pl.BlockSpec((1, L, 256), lambda b, g: (b, 0, g)),
        out_shape=jax.ShapeDtypeStruct((bsz, L, nh * 64), F32),
        scratch_shapes=[pltpu.VMEM((L, 256), F32), pltpu.VMEM((L, 128), F32), pltpu.VMEM((L, 128), F32),
                        pltpu.VMEM((L, 128), F32), pltpu.VMEM((nc, 128, CHUNK), BF16)],
        compiler_params=_cp(("parallel", "parallel")), name="ssd",
    )(proj, proj, proj, lanes, conv_w, conv_b.reshape(1, -1), conv_w, conv_b.reshape(1, -1),
      conv_w, conv_b.reshape(1, -1), dtb, alog, dsk, jnp.asarray(sel))


ML_HP = 2


def _ml_gate_lanes(hh, d):
    li = (hh * 2 + d) * 2
    return li, li + 1


def _ml_gate_columns(nh):
    cols = []
    for hp in range(nh // ML_HP):
        blk = [None] * (4 * ML_HP)
        for hh in range(ML_HP):
            for d in range(2):
                for t, lane in enumerate(_ml_gate_lanes(hh, d)):
                    blk[lane] = (d * 2 + t) * nh + hp * ML_HP + hh
        cols.append(blk)
    return cols


def _mlstm_kernel(q_ref, k_ref, v_ref, g_ref, cwq_ref, cbq_ref, cwk_ref, cbk_ref, gb_ref, o_ref,
                  qc_s, kc_s, kct_s, *, nc, dh):
    q = CHUNK
    first_lat = TM // q

    def prep(i, _):
        t0 = pl.multiple_of(i * q, q)
        idx = pl.ds(t0, q)
        for src, cw, cb, dst, mul in ((q_ref, cwq_ref, cbq_ref, qc_s, 1.0), (k_ref, cwk_ref, cbk_ref, kc_s, dh ** -0.5)):
            dst[idx, :] = _silu(_conv_tile(src, i, nc, cw[...], cb[...], q)) * mul
        for hh in range(ML_HP):
            kct_s[i, hh * dh:(hh + 1) * dh, :] = kc_s[idx, hh * dh:(hh + 1) * dh].T.astype(BF16)
        return 0
    lax.fori_loop(0, nc, prep, 0)

    gb = gb_ref[0]
    lane = lax.broadcasted_iota(jnp.int32, (q, 128), 1)
    eye8 = (lax.broadcasted_iota(jnp.int32, (8, 128), 0) == lax.broadcasted_iota(jnp.int32, (8, 128), 1)).astype(F32)
    o_ref[...] = jnp.zeros_like(o_ref)
    heads = range(ML_HP)

    def direction(d, j, states):
        rev = d == 1
        causal = _tri(q, rev)
        last = 0 if rev else q - 1
        if rev:
            ci = jnp.where(j < first_lat, first_lat - 1 - j, nc - 1 - (j - first_lat))
        else:
            ci = j
        idx = pl.ds(pl.multiple_of(ci * q, q), q)
        g = g_ref[0, idx, :] + gb
        logf = jnp.minimum(g, 0.0) - jnp.log(1.0 + jnp.exp(-jnp.abs(g)))
        is_f = functools.reduce(jnp.logical_or, [lane == _ml_gate_lanes(hh, d)[1] for hh in heads])
        cum = _dot_sel(jnp.where(causal, 1.0, 0.0), jnp.where(is_f, logf, 0.0))
        qc = [qc_s[idx, hh * dh:(hh + 1) * dh] for hh in heads]
        kc = [kc_s[idx, hh * dh:(hh + 1) * dh] for hh in heads]
        vc = [v_ref[0, idx, hh * dh:(hh + 1) * dh] for hh in heads]
        kct = [kct_s[ci, hh * dh:(hh + 1) * dh, :] for hh in heads]
        qk = [_dot(qc[hh], kct[hh]) for hh in heads]
        inter_c = [_dot(qc[hh], states[hh][0]) for hh in heads]
        yield
        rows = _dot_sel(eye8, jnp.where(is_f, cum, g), _NT)
        yield
        w, w_s, g_in, m_t, keep, m_new, wv, upd = [], [], [], [], [], [], [], []
        for hh in heads:
            li, lf = _ml_gate_lanes(hh, d)
            m_prev = states[hh][2]
            i_col, i_row = g[:, li:li + 1], rows[li:li + 1, :]
            b_col, b_row = cum[:, lf:lf + 1], rows[lf:lf + 1, :]
            dlog = jnp.where(causal, b_col - b_row + i_row, NEG)
            inter = b_col + m_prev
            m_t.append(jnp.maximum(jnp.max(dlog, axis=-1, keepdims=True), inter))
            w.append(jnp.exp(dlog - m_t[hh]) * qk[hh])
            g_in.append(jnp.exp(inter - m_t[hh]))
            b_end = b_col[last:last + 1, :]
            g_s = b_end - b_col + i_col
            m_new.append(jnp.maximum(jnp.max(g_s, axis=0, keepdims=True), b_end + m_prev))
            w_s.append(jnp.exp(g_s - m_new[hh]))
            keep.append(jnp.exp(b_end + m_prev - m_new[hh]))
            wv.append(_dot(w[hh], vc[hh]))
            upd.append(_dot(kct[hh], w_s[hh] * vc[hh]))
        yield
        new_states = []
        for hh in heads:
            c_prev, n_prev, _ = states[hh]
            num = wv[hh] + g_in[hh] * inter_c[hh]
            den = (jnp.sum(w[hh], axis=-1, keepdims=True)
                   + g_in[hh] * jnp.sum(qc[hh] * n_prev, axis=-1, keepdims=True))
            o_ref[0, idx, hh * dh:(hh + 1) * dh] += num / jnp.maximum(jnp.abs(den), jnp.exp(-m_t[hh]))
            new_states.append((keep[hh] * c_prev + upd[hh],
                               keep[hh] * n_prev + jnp.sum(w_s[hh] * kc[hh], axis=0, keepdims=True), m_new[hh]))
        return tuple(new_states)

    def chunk(j, states):
        return tuple(_interleave([direction(d, j, states[d]) for d in (0, 1)]))
    init = (jnp.zeros((dh, dh), F32), jnp.zeros((1, dh), F32), jnp.full((1, 1), NEG, F32))
    lax.fori_loop(0, nc, chunk, tuple(tuple(init for _ in heads) for _ in (0, 1)))


def _mlstm(proj, lanes, conv_w, conv_b, gate_b, cols):
    bsz, L, _ = proj.shape
    nc = L // CHUNK
    cq, ck, cv, cg = cols
    nh = gate_b.shape[-1]
    dh = conv_w.shape[1] // (2 * nh)
    bw = ML_HP * dh
    gcols = np.asarray(_ml_gate_columns(nh))
    gbl = jnp.zeros((nh // ML_HP, 1, 128), F32).at[:, 0, :gcols.shape[1]].set(gate_b.reshape(-1)[gcols])
    return pl.pallas_call(
        functools.partial(_mlstm_kernel, nc=nc, dh=dh),
        grid=(bsz, nh // ML_HP),
        in_specs=[pl.BlockSpec((1, L, bw), lambda b, h: (b, 0, cq // bw + h)),
                  pl.BlockSpec((1, L, bw), lambda b, h: (b, 0, ck // bw + h)),
                  pl.BlockSpec((1, L, bw), lambda b, h: (b, 0, cv // bw + h)),
                  pl.BlockSpec((1, L, 128), lambda b, h: (b, 0, cg // 128 + h)),
                  pl.BlockSpec((CONV_W, bw), lambda b, h: (0, h)),
                  pl.BlockSpec((1, bw), lambda b, h: (0, h)),
                  pl.BlockSpec((CONV_W, bw), lambda b, h: (0, nh // ML_HP + h)),
                  pl.BlockSpec((1, bw), lambda b, h: (0, nh // ML_HP + h)),
                  pl.BlockSpec((1, 1, 128), lambda b, h: (h, 0, 0))],
        out_specs=pl.BlockSpec((1, L, bw), lambda b, h: (b, 0, h)),
        out_shape=jax.ShapeDtypeStruct((bsz, L, nh * dh), F32),
        scratch_shapes=[pltpu.VMEM((L, bw), F32), pltpu.VMEM((L, bw), F32), pltpu.VMEM((nc, bw, CHUNK), BF16)],
        compiler_params=_cp(("parallel", "parallel")), name="mlstm",
    )(proj, proj, proj, lanes, conv_w, conv_b.reshape(1, -1), conv_w, conv_b.reshape(1, -1), gbl)


def kernel(x, c, ctx, c_ctx, mod_w, mod_b, norm1_g, norm2_g, even_w_in, even_w_out, lru_conv_w, lru_conv_b, lru_wa, lru_ba, lru_wx, lru_bx, lru_lam, na_q_g, na_k_g, na_rpb, odd_w_in, odd_w_out, ssd_conv_w, ssd_conv_b, ssd_dt_bias, ssd_a_log, ssd_d, ssd_norm_g, ml_conv_w, ml_conv_b, ml_gate_b, ml_norm_g, moe_router_g, moe_router_e, moe_w1, moe_w3, moe_w2):
    bsz, S, d = x.shape
    lc = ctx.shape[1]
    assert lc == TM and S % TM == 0 and mod_w.shape[0] == 2
    nt = (lc + S) // TM
    L = lc + S

    cc = jnp.zeros((8, d), F32).at[0].set(c_ctx).at[1:1 + bsz].set(c)
    mod_all = _modulation(cc, mod_w, mod_b)

    def mod_for(l):
        m = mod_all[l].reshape(8, 6, d)
        return jnp.stack([jnp.broadcast_to(m[0], (bsz, 6, d)), m[1:1 + bsz]], axis=1)

    def router_w(l):
        w = jnp.zeros((d, 128), F32).at[:, :EXPERT_LANE0].set(moe_router_g[l]) \
            .at[:, EXPERT_LANE0:EXPERT_LANE0 + N_EXPERTS].set(moe_router_e[l])
        hi = w.astype(BF16)
        return jnp.concatenate([hi, (w - hi.astype(F32)).astype(BF16)], axis=1)

    xx = (ctx, x)

    mod0 = mod_for(0)
    (proj,) = _inproj(xx, mod0, norm1_g[0], even_w_in[0].astype(BF16))
    lw = lru_conv_w.shape[-1]
    r = _lru(proj, lru_conv_w[0], lru_conv_b[0], lru_wa[0], lru_ba[0], lru_wx[0], lru_bx[0], lru_lam[0])
    a = _na(proj, na_q_g[0], na_k_g[0], na_rpb[0], col0=2 * lw)
    x1, h2, gate = _outproj([(r, 0, lw), (proj, 1, lw), (a, 0, lw)], xx, even_w_out[0].astype(BF16), mod0,
                            norm2_g[0], router_w(0), even=True, tile0=0, ntiles=nt)
    ew = (moe_w1.astype(BF16), moe_w3.astype(BF16), moe_w2.astype(BF16))
    moe0 = _moe(h2.reshape(bsz * L, d), gate.reshape(bsz * L, 128), 0, *ew).reshape(bsz, L, d)

    mod1 = mod_for(1)
    sw = ssd_d.shape[-1] * 64
    xbc = ssd_conv_w.shape[-1]
    mw = ml_norm_g.shape[-1]
    w = odd_w_in[0]
    o = np.cumsum([0, sw, xbc, 2 * ssd_d.shape[-1], mw, mw, mw, mw])
    ndt = 2 * ssd_d.shape[-1]
    dt_blk = jnp.concatenate([w[:, o[2]:o[3]], jnp.zeros((d, 128 - ndt), F32)], axis=1)
    gate_blks = [jnp.concatenate([w[:, o[7] + np.asarray(blk)], jnp.zeros((d, 128 - len(blk)), F32)], axis=1)
                 for blk in _ml_gate_columns(ml_gate_b.shape[-1])]
    w_odd = jnp.concatenate([w[:, :o[2]], w[:, o[3]:o[7]], dt_blk] + gate_blks, axis=1).astype(BF16)
    cz, cxs = 0, sw
    cB, cC = cxs + sw, cxs + sw + (xbc - sw) // 2
    cq = sw + xbc
    ck, cv, co, csm = cq + mw, cq + 2 * mw, cq + 3 * mw, cq + 4 * mw
    x0, proj1, lanes1 = _inproj(x1, mod1, norm1_g[1], w_odd, prev=(moe0, mod0), n_f32=w_odd.shape[1] - csm)
    ys = _ssd(proj1, lanes1, ssd_conv_w[0], ssd_conv_b[0], ssd_dt_bias[0], ssd_a_log[0], ssd_d[0], (cxs, cB, cC, 0))
    hm = _mlstm(proj1, lanes1, ml_conv_w[0], ml_conv_b[0], ml_gate_b[0], (cq, ck, cv, 128))
    x2, h2b, gate1 = _outproj([(ys, 0, sw), (proj1, cz // sw, sw), (hm, 0, mw), (proj1, co // mw, mw),
                               (ssd_norm_g[0].reshape(1, sw), 0, sw), (ml_norm_g[0].reshape(1, mw), 0, mw)],
                              x0, odd_w_out[0].astype(BF16), mod1, norm2_g[1], router_w(1),
                              even=False, tile0=1, ntiles=nt - 1)
    out = _moe(h2b.reshape(bsz * S, d), gate1.reshape(bsz * S, 128), 1, *ew, resid=(x2.reshape(bsz * S, d), mod1))
    return out.reshape(bsz, S, d)
```

```python
import functools
import math

import jax
import jax.numpy as jnp
import numpy as np
from jax import lax
from jax.experimental import pallas as pl
from jax.experimental.pallas import tpu as pltpu

F32 = jnp.float32
BF16 = jnp.bfloat16
HI = lax.Precision.HIGHEST

EPS = 1e-6
NEG = -1e30
GRID_W = 64
CONV_W = 4
LRU_C = 8.0
TM = 256
CHUNK = 256
NA_RQ = 4
NA_RK = 12
N_EXPERTS = 16
EXPERT_LANE0 = 4
VMEM_LIMIT = 56 * 1024 * 1024


def _cp(sem, vmem=VMEM_LIMIT):
    return pltpu.CompilerParams(dimension_semantics=sem, vmem_limit_bytes=vmem)


def _sigmoid(x):
    return jax.nn.sigmoid(x)


def _silu(x):
    return x * jax.nn.sigmoid(x)


def _softplus(x):
    return jnp.maximum(x, 0.0) + jnp.log(1.0 + jnp.exp(-jnp.abs(x)))


def _gelu_tanh(x):
    return 0.5 * x * (1.0 + jnp.tanh(math.sqrt(2.0 / math.pi) * (x + 0.044715 * (x * x * x))))


def _rms(x, axis=-1):
    return x * lax.rsqrt(jnp.mean(x * x, axis=axis, keepdims=True) + EPS)


def _dot(a, b):
    return jnp.dot(a.astype(BF16), b.astype(BF16), preferred_element_type=F32)


def _dot_hi(a, b):
    return jnp.dot(a, b, precision=HI, preferred_element_type=F32)


def _split3(x):
    x1 = x.astype(BF16)
    r = x - x1.astype(F32)
    x2 = r.astype(BF16)
    x3 = (r - x2.astype(F32)).astype(BF16)
    return x1, x2, x3


def _dot_sel(sel, x, dims=(((1,), (0,)), ((), ()))):
    s = sel.astype(BF16)
    x1, x2, x3 = _split3(x)
    d = lambda xi: lax.dot_general(s, xi, dims, preferred_element_type=F32)
    return (d(x3) + d(x2)) + d(x1)


def _dot_sel_r(x, sel):
    s = sel.astype(BF16)
    x1, x2, x3 = _split3(x)
    d = lambda xi: jnp.dot(xi, s, preferred_element_type=F32)
    return (d(x3) + d(x2)) + d(x1)


_NT = (((1,), (1,)), ((), ()))


def _interleave(gens):
    results = [None] * len(gens)
    live = list(range(len(gens)))
    while live:
        for i in list(live):
            try:
                next(gens[i])
            except StopIteration as stop:
                results[i] = stop.value
                live.remove(i)
    return results


def _dot_nt(a, b):
    return lax.dot_general(a.astype(BF16), b.astype(BF16), (((1,), (1,)), ((), ())),
                           preferred_element_type=F32)


def _mod_kernel(c_ref, w_ref, b_ref, o_ref):
    c = c_ref[...]
    o_ref[0] = _dot_hi(_silu(c), w_ref[0]) + b_ref[0]


def _modulation(cc, mod_w, mod_b):
    depth, d, n = mod_w.shape
    tn = 1536
    return pl.pallas_call(
        _mod_kernel,
        grid=(depth, n // tn),
        in_specs=[pl.BlockSpec((8, d), lambda l, j: (0, 0)),
                  pl.BlockSpec((1, d, tn), lambda l, j: (l, 0, j)),
                  pl.BlockSpec((1, 1, tn), lambda l, j: (l, 0, j))],
        out_specs=pl.BlockSpec((1, 8, tn), lambda l, j: (l, 0, j)),
        out_shape=jax.ShapeDtypeStruct((depth, 8, n), F32),
        compiler_params=_cp(("arbitrary", "arbitrary")),
        name="adaln_mod",
    )(cc, mod_w, mod_b.reshape(depth, 1, n))


def _inproj_kernel(*refs, fuse_prev, n_f32):
    refs = list(refs)
    s_ref = refs.pop() if n_f32 else None
    if fuse_prev:
        x_ref, mo_ref, pmod_ref, mod_ref, g_ref, w_ref, xo_ref, p_ref = refs
        x = x_ref[0] + pmod_ref[0, 0][5:6, :] * mo_ref[0]
        xo_ref[0] = x
    else:
        c_ref, x_ref, mod_ref, g_ref, w_ref, p_ref = refs
        x = _pick_segment(c_ref, x_ref)
    mod = mod_ref[0, 0]
    h = _rms(x) * g_ref[...] * (1.0 + mod[1:2, :]) + mod[0:1, :]
    p = _dot(h, w_ref[...])
    n_main = p.shape[-1] - n_f32
    p_ref[0] = p[:, :n_main].astype(BF16)
    if n_f32:
        s_ref[0] = p[:, n_main:]


def _seg_map(b, i):
    return (b, jnp.minimum(i, 1), 0, 0)


def _segment_specs(d):
    return [pl.BlockSpec((1, TM, d), lambda b, i: (b, 0, 0)),
            pl.BlockSpec((1, TM, d), lambda b, i: (b, jnp.maximum(i - 1, 0), 0))]


def _pick_segment(c_ref, x_ref):
    return jnp.where(pl.program_id(1) == 0, c_ref[0], x_ref[0])


def _inproj(x, mod, g, w, prev=None, n_f32=0):
    if prev is None:
        ctx, lat = x
        bsz, S, d = lat.shape
        L = TM + S
    else:
        bsz, L, d = x.shape
    n = w.shape[1]
    nt = L // TM
    tok = pl.BlockSpec((1, TM, d), lambda b, i: (b, i, 0))
    modspec = pl.BlockSpec((1, 1, 6, d), _seg_map)
    tail = [modspec, pl.BlockSpec((1, d), lambda b, i: (0, 0)), pl.BlockSpec((d, n), lambda b, i: (0, 0))]
    pspecs = [pl.BlockSpec((1, TM, n - n_f32), lambda b, i: (b, i, 0))]
    pshapes = [jax.ShapeDtypeStruct((bsz, L, n - n_f32), BF16)]
    if n_f32:
        pspecs.append(pl.BlockSpec((1, TM, n_f32), lambda b, i: (b, i, 0)))
        pshapes.append(jax.ShapeDtypeStruct((bsz, L, n_f32), F32))
    if prev is None:
        return pl.pallas_call(
            functools.partial(_inproj_kernel, fuse_prev=False, n_f32=n_f32),
            grid=(bsz, nt), in_specs=_segment_specs(d) + tail, out_specs=pspecs, out_shape=pshapes,
            compiler_params=_cp(("parallel", "arbitrary")), name="inproj",
        )(ctx, lat, mod, g.reshape(1, d), w)
    moe_out, pmod = prev
    return pl.pallas_call(
        functools.partial(_inproj_kernel, fuse_prev=True, n_f32=n_f32),
        grid=(bsz, nt), in_specs=[tok, tok, modspec] + tail,
        out_specs=[tok] + pspecs, out_shape=[jax.ShapeDtypeStruct(x.shape, F32)] + pshapes,
        compiler_params=_cp(("parallel", "parallel")), name="inproj_res",
    )(x, moe_out, pmod, mod, g.reshape(1, d), w)


def _conv_tile(ref, i, nt, cw, cb, width=TM):
    L = nt * width
    t0 = pl.multiple_of(i * width, width)
    cur = ref[0, pl.ds(t0, width), :].astype(F32)
    prev = ref[0, pl.ds(pl.multiple_of(jnp.maximum(t0 - 16, 0), 16), 16), :].astype(F32)[8:16]
    nxt = ref[0, pl.ds(pl.multiple_of(jnp.minimum(t0 + width, L - 16), 16), 16), :].astype(F32)[0:8]
    return _conv_vals(cur, prev, nxt, i, nt, cw, cb, width, first_lat=TM // width)


def _conv_vals(cur, prev, nxt, i, nt, cw, cb, width, first_lat):
    prev = jnp.where((i != 0) & (i != first_lat), prev, 0.0)
    nxt = jnp.where((i != first_lat - 1) & (i != nt - 1), nxt, 0.0)
    cat = jnp.concatenate([prev, cur, nxt], axis=0)
    return (cw[0:1] * cat[6:6 + width] + cw[1:2] * cat[7:7 + width] + cw[2:3] * cur
            + cw[3:4] * cat[9:9 + width] + cb)


def _lru_kernel(ux_ref, cw_ref, cb_ref, gw_ref, gb_ref, lam_ref, o_ref, *, nt):
    cw = cw_ref[...]
    cb = cb_ref[...]
    lam = lam_ref[0]
    sp = _softplus(-lam)
    row = lax.broadcasted_iota(jnp.int32, (TM, 128), 0) & 7

    def gates(i, d):
        xl = _conv_tile(ux_ref, i, nt, cw, cb)
        g = _dot(xl, gw_ref[0, d]) + gb_ref[0, d]
        r = _sigmoid(g[:, :128])
        ig = _sigmoid(g[:, 128:])
        log_a = -LRU_C * r * sp[d:d + 1]
        a = jnp.exp(log_a)
        u = jnp.sqrt(1.0 - a * a) * (ig * xl)
        return a, u

    def scan_tile(i, d, carry, accumulate):
        a, u = gates(i, d)
        rev = d == 1
        for k in (1, 2, 4):
            sh = TM - k if rev else k
            ok = (row < 8 - k) if rev else (row >= k)
            ash = pltpu.roll(a, sh, 0)
            ush = pltpu.roll(u, sh, 0)
            u = jnp.where(ok, u + a * ush, u)
            a = jnp.where(ok, a * ash, a)
        t0 = i * TM
        groups = range(TM // 8)
        for s in (reversed(groups) if rev else groups):
            h = u[s * 8:(s + 1) * 8] + a[s * 8:(s + 1) * 8] * carry
            carry = h[0:1] if rev else h[7:8]
            idx = pl.ds(pl.multiple_of(t0 + s * 8, 8), 8)
            if accumulate:
                o_ref[0, idx, :] += h
            else:
                o_ref[0, idx, :] = h
        return carry

    zero = jnp.zeros((1, 128), F32)
    lax.fori_loop(0, nt, lambda i, c: scan_tile(i, 0, c, False), zero)
    lax.fori_loop(0, nt, lambda j, c: scan_tile(jnp.where(j == 0, 0, nt - j), 1, c, True), zero)


def _lru(proj, conv_w, conv_b, wa, ba, wx, bx, lam):
    bsz, L, _ = proj.shape
    nt = L // TM
    width = conv_w.shape[1]
    ng = width // 128

    def blockdiag(w):
        w = w.reshape(2, ng, 2, 64, 64)
        z = jnp.zeros_like(w[:, :, 0])
        top = jnp.concatenate([w[:, :, 0], z], axis=-1)
        bot = jnp.concatenate([z, w[:, :, 1]], axis=-1)
        return jnp.concatenate([top, bot], axis=-2)
    gw = jnp.concatenate([blockdiag(wa), blockdiag(wx)], axis=-1).transpose(1, 0, 2, 3).astype(BF16)
    gb = jnp.concatenate([ba.reshape(2, ng, 1, 128), bx.reshape(2, ng, 1, 128)], axis=-1).transpose(1, 0, 2, 3)
    lam_g = lam.reshape(2, ng, 128).transpose(1, 0, 2)
    return pl.pallas_call(
        functools.partial(_lru_kernel, nt=nt),
        grid=(bsz, ng),
        in_specs=[pl.BlockSpec((1, L, 128), lambda b, c: (b, 0, c)),
                  pl.BlockSpec((CONV_W, 128), lambda b, c: (0, c)),
                  pl.BlockSpec((1, 128), lambda b, c: (0, c)),
                  pl.BlockSpec((1, 2, 128, 256), lambda b, c: (c, 0, 0, 0)),
                  pl.BlockSpec((1, 2, 1, 256), lambda b, c: (c, 0, 0, 0)),
                  pl.BlockSpec((1, 2, 128), lambda b, c: (c, 0, 0))],
        out_specs=pl.BlockSpec((1, L, 128), lambda b, c: (b, 0, c)),
        out_shape=jax.ShapeDtypeStruct((bsz, L, width), F32),
        compiler_params=_cp(("parallel", "parallel")), name="rglru",
    )(proj, conv_w, conv_b.reshape(1, width), gw, gb, lam_g)


def _na_bias_table(rpb, rows):
    nh = rpb.shape[0]
    win_c = (rpb.shape[2] + 1) // 2
    qc = np.arange(GRID_W)[:, None]
    kc = np.arange(GRID_W)[None, :]
    cstart = np.clip(qc - win_c // 2, 0, GRID_W - win_c)
    col_ok = (kc >= cstart) & (kc < cstart + win_c)
    dcol = np.clip(kc - qc + (win_c - 1), 0, 2 * win_c - 2)
    oc = (np.arange(2 * win_c - 1)[:, None, None] == dcol[None]).astype(np.float32)
    ct = jnp.where(jnp.asarray(col_ok), jnp.einsum('hrc,cqk->hrqk', rpb.astype(F32), jnp.asarray(oc), precision=HI),
                   NEG)
    return pl.pallas_call(
        functools.partial(_na_bias_kernel, rows=rows),
        grid=(nh,),
        in_specs=[pl.BlockSpec((1,) + ct.shape[1:], lambda h: (h, 0, 0, 0))],
        out_specs=pl.BlockSpec((3, 1, TM, NA_RK * GRID_W), lambda h: (0, h, 0, 0)),
        out_shape=jax.ShapeDtypeStruct((3, nh, TM, NA_RK * GRID_W), F32),
        compiler_params=_cp(("parallel",)), name="nbr_bias",
    )(ct)


def _na_bias_kernel(ct_ref, o_ref, *, rows):
    win_r = (ct_ref.shape[1] + 1) // 2
    neg = jnp.full((GRID_W, GRID_W), NEG, F32)
    for p, (r0, w0) in enumerate(((0, 0), (2 * NA_RQ, NA_RQ), (rows - NA_RQ, rows - NA_RK))):
        for a in range(NA_RQ):
            r = r0 + a
            rstart = min(max(r - win_r // 2, 0), rows - win_r)
            tiles = [ct_ref[0, w0 + b - r + win_r - 1] if rstart <= w0 + b < rstart + win_r else neg
                     for b in range(NA_RK)]
            for bp in range(NA_RK // 2):
                o_ref[p, 0, a * GRID_W:(a + 1) * GRID_W, bp * 2 * GRID_W:(bp + 1) * 2 * GRID_W] = (
                    jnp.concatenate(tiles[2 * bp:2 * bp + 2], axis=1))


def _pair_rms(x, lo):
    x2 = x * x
    s0 = jnp.sum(jnp.where(lo, x2, 0.0), axis=-1, keepdims=True)
    s1 = jnp.sum(jnp.where(lo, 0.0, x2), axis=-1, keepdims=True)
    return x * lax.rsqrt(jnp.where(lo, s0, s1) * (2.0 / x.shape[-1]) + EPS)


def _na_kernel(q_ref, k_ref, v_ref, bias0_ref, bias1_ref, qg_ref, kg_ref, o_ref, kn_s, vb_s, *, rows, hd):
    j = pl.program_id(2)
    L = k_ref.shape[1]
    nb = L // TM
    nkeys = NA_RK * GRID_W
    lo = lax.broadcasted_iota(jnp.int32, (1, 2 * hd), 1) < hd

    @pl.when(j == 0)
    def _prep():
        def body(t, _):
            idx = pl.ds(pl.multiple_of(t * TM, TM), TM)
            kn_s[idx, :] = (_pair_rms(k_ref[0, idx, :].astype(F32), lo) * kg_ref[...]).astype(BF16)
            vb_s[idx, :] = v_ref[0, idx, :].astype(BF16)
            return 0
        lax.fori_loop(0, nb, body, 0)

    kctx = kn_s[0:TM, :]
    vctx = vb_s[0:TM, :]
    qscale = qg_ref[...] * hd ** -0.5

    def head(q, bias, kwin, vwin):
        s_c = _dot_nt(q, kctx)
        if kwin is not None:
            s_w = _dot_nt(q, kwin) + bias
        yield
        m = jnp.max(s_c, axis=-1, keepdims=True)
        if kwin is not None:
            m = jnp.maximum(m, jnp.max(s_w, axis=-1, keepdims=True))
            p_w = jnp.exp(s_w - m)
        p_c = jnp.exp(s_c - m)
        den = jnp.sum(p_c, axis=-1, keepdims=True)
        num = _dot(p_c, vctx)
        if kwin is not None:
            den = den + jnp.sum(p_w, axis=-1, keepdims=True)
            num = num + _dot(p_w, vwin)
        yield
        return num / den

    def tile_chains(slot, bias_ref, tile):
        qn = _pair_rms(q_ref[0, slot * TM:(slot + 1) * TM, :].astype(F32), lo) * qscale
        q_h = [jnp.where(lo, qn, 0.0).astype(BF16), jnp.where(lo, 0.0, qn).astype(BF16)]
        if tile is None:
            return [head(q_h[hh], None, None, None) for hh in range(2)]
        w0 = jnp.clip((tile - 1) * NA_RQ - NA_RQ, 0, rows - NA_RK)
        start = pl.multiple_of(TM + w0 * GRID_W, GRID_W)
        kwin = kn_s[pl.ds(start, nkeys), :]
        vwin = vb_s[pl.ds(start, nkeys), :]
        return [head(q_h[hh], bias_ref[0, hh], kwin, vwin) for hh in range(2)]

    def run(chains):
        outs = _interleave(chains)
        for slot in range(len(outs) // 2):
            o_ref[0, slot * TM:(slot + 1) * TM, :] = jnp.where(lo, outs[2 * slot], outs[2 * slot + 1])

    last = (nb - 1) // 2

    @pl.when(j == 0)
    def _():
        run(tile_chains(0, None, None) + tile_chains(1, bias1_ref, 1))

    @pl.when((j > 0) & (j < last))
    def _():
        run(tile_chains(0, bias0_ref, 2 * j) + tile_chains(1, bias1_ref, 2 * j + 1))

    @pl.when(j == last)
    def _():
        run(tile_chains(0, bias0_ref, 2 * j))


def _na(proj, q_g, k_g, rpb, col0):
    bsz, L, _ = proj.shape
    nh = rpb.shape[0]
    hd = q_g.shape[0]
    width = nh * hd
    rows = (L - TM) // GRID_W
    nb = L // TM
    bias = _na_bias_table(rpb, rows)
    qb, kb, vb = col0 // 128, (col0 + width) // 128, (col0 + 2 * width) // 128

    assert nb % 2 == 1

    def pat(t):
        return jnp.where(t <= 1, 0, jnp.where(t >= nb - 1, 2, 1))
    bias_spec = lambda slot: pl.BlockSpec((1, 2, TM, NA_RK * GRID_W), lambda b, h, j: (pat(2 * j + slot), h, 0, 0))
    return pl.pallas_call(
        functools.partial(_na_kernel, rows=rows, hd=hd),
        grid=(bsz, width // 128, (nb + 1) // 2),
        in_specs=[pl.BlockSpec((1, 2 * TM, 128), lambda b, h, j: (b, j, qb + h)),
                  pl.BlockSpec((1, L, 128), lambda b, h, j: (b, 0, kb + h)),
                  pl.BlockSpec((1, L, 128), lambda b, h, j: (b, 0, vb + h)),
                  bias_spec(0), bias_spec(1),
                  pl.BlockSpec((1, 2 * hd), lambda b, h, j: (0, 0)),
                  pl.BlockSpec((1, 2 * hd), lambda b, h, j: (0, 0))],
        out_specs=pl.BlockSpec((1, 2 * TM, 128), lambda b, h, j: (b, j, h)),
        out_shape=jax.ShapeDtypeStruct((bsz, L, width), F32),
        scratch_shapes=[pltpu.VMEM((L, 128), BF16), pltpu.VMEM((L, 128), BF16)],
        compiler_params=_cp(("parallel", "parallel", "arbitrary")), name="nbr_attn",
    )(proj, proj, proj, bias, bias, jnp.tile(q_g, 2).reshape(1, 2 * hd), jnp.tile(k_g, 2).reshape(1, 2 * hd))


def _route(lg):
    lane = lax.broadcasted_iota(jnp.int32, lg.shape, 1)
    lane_f = lane.astype(F32)
    is_g = lane < EXPERT_LANE0
    gl = jnp.where(is_g, lg, NEG)
    gmax = jnp.max(gl, axis=-1, keepdims=True)
    gsel = jnp.min(jnp.where(is_g & (gl == gmax), lane_f, 1e9), axis=-1, keepdims=True)
    g_w = 1.0 / jnp.sum(jnp.where(is_g, jnp.exp(gl - gmax), 0.0), axis=-1, keepdims=True)
    grp = ((lane - EXPERT_LANE0) >> 2).astype(F32)
    in_g = (lane >= EXPERT_LANE0) & (lane < EXPERT_LANE0 + N_EXPERTS) & (grp == gsel)
    el = jnp.where(in_g, lg, NEG)
    v1 = jnp.max(el, axis=-1, keepdims=True)
    i1 = jnp.min(jnp.where(in_g & (el == v1), lane_f, 1e9), axis=-1, keepdims=True)
    el2 = jnp.where(lane_f == i1, NEG, el)
    v2 = jnp.max(el2, axis=-1, keepdims=True)
    i2 = jnp.min(jnp.where(in_g & (lane_f != i1) & (el2 == v2), lane_f, 1e9), axis=-1, keepdims=True)
    t = jnp.exp(v2 - v1)
    w1 = g_w / (1.0 + t)
    w2 = g_w * t / (1.0 + t)
    return (jnp.where(lane_f == i1, w1, 0.0) + jnp.where(lane_f == i2, w2, 0.0)
            + jnp.where(lane_f == gsel, 1.0, 0.0))


def _outproj_kernel(*refs, even):
    if even:
        (r_ref, ug_ref, a_ref, c_ref, x_ref, w_ref, mod_ref, g2_ref, rw_ref, x1_ref, h2_ref, gate_ref) = refs
        is_ctx = pl.program_id(1) == 0
        x_rows = lambda rs: jnp.where(is_ctx, c_ref[0, rs, :], x_ref[0, rs, :])
    else:
        (ys_ref, z_ref, hm_ref, mo_ref, sg_ref, mg_ref, x_ref, w_ref, mod_ref, g2_ref, rw_ref,
         x1_ref, h2_ref, gate_ref) = refs
        x_rows = lambda rs: x_ref[0, rs, :]
    mod = mod_ref[0, 0]

    def part(rs):
        if even:
            y_in = jnp.concatenate([r_ref[0, rs, :] * _gelu_tanh(ug_ref[0, rs, :].astype(F32)), a_ref[0, rs, :]],
                                   axis=-1)
        else:
            ys = ys_ref[0, rs, :] * _silu(z_ref[0, rs, :].astype(F32))
            sg = sg_ref[...]
            mg = mg_ref[...]
            hm = hm_ref[0, rs, :]
            sig_o = _sigmoid(mo_ref[0, rs, :].astype(F32))
            gw = ys.shape[-1] // 2
            parts = [_rms(ys[:, g * gw:(g + 1) * gw]) * sg[:, g * gw:(g + 1) * gw] for g in range(2)]
            hw = 128
            parts += [_rms(hm[:, h * hw:(h + 1) * hw]) * mg[:, h * hw:(h + 1) * hw] * sig_o[:, h * hw:(h + 1) * hw]
                      for h in range(hm.shape[-1] // hw)]
            y_in = jnp.concatenate(parts, axis=-1)
        y = _dot(y_in, w_ref[...])
        yield
        x1 = x_rows(rs) + mod[2:3, :] * y
        x1_ref[0, rs, :] = x1
        h2 = _rms(x1) * g2_ref[...] * (1.0 + mod[4:5, :]) + mod[3:4, :]
        h2_ref[0, rs, :] = h2.astype(BF16)
        hi = h2.astype(BF16)
        lo = (h2 - hi.astype(F32)).astype(BF16)
        lg2 = jnp.dot(hi, rw_ref[...], preferred_element_type=F32)
        lg1 = jnp.dot(lo, rw_ref[:, 0:128], preferred_element_type=F32)
        yield
        gate_ref[0, rs, :] = _route(lg2[:, 0:128] + lg2[:, 128:256] + lg1)

    nparts = 2
    rows = TM // nparts
    _interleave([part(pl.ds(p * rows, rows)) for p in range(nparts)])


def _outproj(mix_inputs, x, w, mod, g2, rw, even, tile0, ntiles):
    xs = list(x) if even else [x]
    bsz, _, d = xs[-1].shape
    specs, args = [], []
    for arr, cb, wdt in mix_inputs:
        if arr.ndim == 3:
            specs.append(pl.BlockSpec((1, TM, wdt), lambda b, i, cb=cb: (b, i + tile0, cb)))
        else:
            specs.append(pl.BlockSpec((1, wdt), lambda b, i: (0, 0)))
        args.append(arr)
    specs += _segment_specs(d) if even else [pl.BlockSpec((1, TM, d), lambda b, i: (b, i + tile0, 0))]
    specs += [pl.BlockSpec(w.shape, lambda b, i: (0, 0)),
              pl.BlockSpec((1, 1, 6, d), lambda b, i: (b, jnp.minimum(i + tile0, 1), 0, 0)),
              pl.BlockSpec((1, d), lambda b, i: (0, 0)),
              pl.BlockSpec(rw.shape, lambda b, i: (0, 0))]
    args += xs + [w, mod, g2.reshape(1, d), rw]
    lo = ntiles * TM
    return pl.pallas_call(
        functools.partial(_outproj_kernel, even=even),
        grid=(bsz, ntiles), in_specs=specs,
        out_specs=[pl.BlockSpec((1, TM, d), lambda b, i: (b, i, 0)),
                   pl.BlockSpec((1, TM, d), lambda b, i: (b, i, 0)),
                   pl.BlockSpec((1, TM, 128), lambda b, i: (b, i, 0))],
        out_shape=[jax.ShapeDtypeStruct((bsz, lo, d), F32), jax.ShapeDtypeStruct((bsz, lo, d), BF16),
                   jax.ShapeDtypeStruct((bsz, lo, 128), F32)],
        compiler_params=_cp(("parallel", "parallel")), name="outproj_even" if even else "outproj_odd",
    )(*args)


MOE_CH = 128
GROUP_SIZE = 4
MOE_VMEM_LIMIT = 62 * 1024 * 1024
N_GROUPS = N_EXPERTS // GROUP_SIZE


def _moe_kernel(*refs, residual):
    if residual:
        x_ref, g_ref, w1_ref, w3_ref, w2_ref, x1_ref, mod_ref, o_ref, xs_s, ys_s, gs_s, pt_s, plan_s = refs
    else:
        x_ref, g_ref, w1_ref, w3_ref, w2_ref, o_ref, xs_s, ys_s, gs_s, pt_s, plan_s = refs
    grp = pl.program_id(1)
    tb = x_ref.shape[0]
    nch = xs_s.shape[0] // MOE_CH

    @pl.when(grp == 0)
    def _plan():
        g = g_ref[...]
        lane = lax.broadcasted_iota(jnp.int32, g.shape, 1)
        oh = jnp.where(lane < N_GROUPS, g, 0.0)
        earlier = (lax.broadcasted_iota(jnp.int32, (tb, tb), 0) > lax.broadcasted_iota(jnp.int32, (tb, tb), 1))
        rank = jnp.dot(jnp.where(earlier, 1.0, 0.0).astype(BF16), oh.astype(BF16),
                       preferred_element_type=F32)
        cnt = jnp.sum(oh, axis=0, keepdims=True)
        lane1 = lax.broadcasted_iota(jnp.int32, (1, 128), 1)
        off = jnp.int32(0)
        offv = jnp.zeros((1, 128), F32)
        for gi in range(N_GROUPS):
            n = jnp.sum(jnp.where(lane1 == gi, cnt, 0.0)).astype(jnp.int32)
            nchunks = (n + (MOE_CH - 1)) // MOE_CH
            plan_s[gi] = off // MOE_CH
            plan_s[N_GROUPS + gi] = nchunks
            offv = offv + jnp.where(lane1 == gi, off.astype(F32), 0.0)
            off = off + nchunks * MOE_CH
        pos_col = jnp.sum(oh * (rank + offv), axis=1, keepdims=True)
        posb = jnp.broadcast_to(pos_col, (tb, 128))
        pos_row = jnp.concatenate([posb[i * 128:(i + 1) * 128, :].T[0:1, :] for i in range(tb // 128)], axis=1)
        x = x_ref[...]
        g_hi = g.astype(BF16)
        g_lo = (g - g_hi.astype(F32)).astype(BF16)
        lane_c = lax.broadcasted_iota(jnp.int32, (tb, MOE_CH), 1).astype(F32)
        row_c = lax.broadcasted_iota(jnp.int32, (MOE_CH, tb), 0).astype(F32)
        used = off // MOE_CH

        def dispatch(c):
            sl = slice(c * MOE_CH, (c + 1) * MOE_CH)
            p = jnp.where(row_c + float(c * MOE_CH) == pos_row, 1.0, 0.0).astype(BF16)
            xs_s[sl, :] = jnp.dot(p, x, preferred_element_type=F32).astype(BF16)
            gs_s[sl, :] = (jnp.dot(p, g_hi, preferred_element_type=F32)
                           + jnp.dot(p, g_lo, preferred_element_type=F32))
        for c in range(nch):
            sl = slice(c * MOE_CH, (c + 1) * MOE_CH)
            pt_s[:, sl] = jnp.where(pos_col == lane_c + float(c * MOE_CH), 1.0, 0.0).astype(BF16)
            if c < tb // MOE_CH:
                dispatch(c)
            else:
                pl.when(c < used)(functools.partial(dispatch, c))
        ys_s[...] = jnp.zeros_like(ys_s)

    c0 = plan_s[grp]
    nchunks = plan_s[N_GROUPS + grp]

    def ffn(chunk0, nrows):
        rows = pl.ds(pl.multiple_of(chunk0 * MOE_CH, MOE_CH), nrows)
        xs = xs_s[rows, :]
        gs = gs_s[rows, :]
        lane_g = lax.broadcasted_iota(jnp.int32, (nrows, 128), 1)
        acc = None
        for k in range(GROUP_SIZE):
            a = jnp.dot(xs, w1_ref[0, k], preferred_element_type=F32)
            b = jnp.dot(xs, w3_ref[0, k], preferred_element_type=F32)
            y = _dot(_silu(a) * b, w2_ref[0, k])
            ge = jnp.sum(jnp.where(lane_g == grp * GROUP_SIZE + k + EXPERT_LANE0, gs, 0.0), axis=-1, keepdims=True)
            acc = ge * y if acc is None else acc + ge * y
        ys_s[rows, :] = acc.astype(BF16)

    wide = 4

    def full(p, _):
        ffn(c0 + wide * p, wide * MOE_CH)
        return 0
    lax.fori_loop(0, nchunks // wide, full, 0)
    rem = nchunks % wide
    for r in range(1, wide):
        pl.when(rem == r)(functools.partial(ffn, c0 + nchunks - r, r * MOE_CH))

    @pl.when(grp == N_GROUPS - 1)
    def _combine():
        out = jnp.dot(pt_s[...], ys_s[...], preferred_element_type=F32)
        if residual:
            out = x1_ref[...] + mod_ref[0, 0][5:6, :] * out
        o_ref[...] = out


def _moe(h2, gate, layer, w1, w3, w2, resid=None):
    t, d = h2.shape
    _, ne, _, ff = w1.shape
    tb = math.gcd(t, 1024)
    npad = tb + N_GROUPS * MOE_CH
    assert ne == N_EXPERTS
    extra_specs, extra_args = [], []
    if resid is not None:
        x1, mod = resid
        per_batch = t // mod.shape[0] // tb
        extra_specs = [pl.BlockSpec((tb, d), lambda i, e: (i, 0)),
                       pl.BlockSpec((1, 1, 6, d), lambda i, e: (i // per_batch, 1, 0, 0))]
        extra_args = [x1, mod]
    return pl.pallas_call(
        functools.partial(_moe_kernel, residual=resid is not None),
        grid=(t // tb, N_GROUPS),
        in_specs=[pl.BlockSpec((tb, d), lambda i, e: (i, 0)),
                  pl.BlockSpec((tb, 128), lambda i, e: (i, 0)),
                  pl.BlockSpec((1, GROUP_SIZE, d, ff), lambda i, e: (layer, e, 0, 0)),
                  pl.BlockSpec((1, GROUP_SIZE, d, ff), lambda i, e: (layer, e, 0, 0)),
                  pl.BlockSpec((1, GROUP_SIZE, ff, d), lambda i, e: (layer, e, 0, 0))] + extra_specs,
        out_specs=pl.BlockSpec((tb, d), lambda i, e: (i, 0)),
        out_shape=jax.ShapeDtypeStruct((t, d), F32),
        scratch_shapes=[pltpu.VMEM((npad, d), BF16), pltpu.VMEM((npad, d), BF16), pltpu.VMEM((npad, 128), F32),
                        pltpu.VMEM((tb, npad), BF16), pltpu.SMEM((2 * N_GROUPS,), jnp.int32)],
        compiler_params=_cp(("parallel", "arbitrary"), MOE_VMEM_LIMIT), name="moe",
    )(h2, gate, w1, w3, w2, *extra_args)


def _tri(q, rev):
    r = lax.broadcasted_iota(jnp.int32, (q, q), 0)
    c = lax.broadcasted_iota(jnp.int32, (q, q), 1)
    return (c >= r) if rev else (c <= r)


def _ssd_kernel(xs_ref, b_ref, c_ref, dt_ref, cwx_ref, cbx_ref, cwb_ref, cbb_ref, cwc_ref, cbc_ref,
                dtb_ref, alog_ref, dsk_ref, sel_ref, o_ref, xc_s, bc_s, cc_s, dt_s, bct_s, *, nc):
    q = CHUNK
    first_lat = TM // q

    dtb = dtb_ref[...]

    def prep(i, _):
        t0 = pl.multiple_of(i * q, q)
        idx = pl.ds(t0, q)
        for src, cw, cb, dst in ((xs_ref, cwx_ref, cbx_ref, xc_s), (b_ref, cwb_ref, cbb_ref, bc_s),
                                 (c_ref, cwc_ref, cbc_ref, cc_s)):
            dst[idx, :] = _silu(_conv_tile(src, i, nc, cw[...], cb[...], q))
        dt_s[idx, :] = _softplus(dt_ref[0, idx, :] + dtb)
        o_ref[0, idx, :] = dsk_ref[...] * xc_s[idx, :]
        bct_s[i] = bc_s[idx, :].T.astype(BF16)
        return 0
    lax.fori_loop(0, nc, prep, 0)

    a_lane = jnp.broadcast_to(-jnp.exp(alog_ref[...]), (8, 128))
    a_e = [_dot_hi(a_lane, sel_ref[d, 0])[0:1, :] for d in (0, 1)]
    lane256 = lax.broadcasted_iota(jnp.int32, (q, 256), 1) // 64
    r8 = lax.broadcasted_iota(jnp.int32, (8, 256), 0)
    l8 = lax.broadcasted_iota(jnp.int32, (8, 256), 1)
    head_rows = (l8 == r8 * 64).astype(F32)

    def chain(d, j, h):
        rev = d == 1
        causal = _tri(q, rev)
        tri = causal.astype(F32)
        sel = sel_ref[d, 0]
        last = 0 if rev else q - 1
        if rev:
            ci = jnp.where(j < first_lat, first_lat - 1 - j, nc - 1 - (j - first_lat))
        else:
            ci = j
        idx = pl.ds(pl.multiple_of(ci * q, q), q)
        dt_e = _dot_sel_r(dt_s[idx, :], sel)
        xc = xc_s[idx, :]
        bc = bc_s[idx, :]
        cc = cc_s[idx, :]
        cb = _dot_nt(cc, bc)
        y_off = _dot(cc, h)
        yield
        cum_e = _dot_sel(tri, dt_e * a_e[d])
        dtx = xc * dt_e
        yield
        cum_rows = _dot_sel(head_rows, cum_e, _NT)
        tot_e = cum_e[last:last + 1, :]
        states = _dot(bct_s[ci], dtx * jnp.exp(tot_e - cum_e))
        yield
        y = jnp.exp(cum_e) * y_off
        for r in range(4):
            seg = cum_e[:, r * 64:r * 64 + 1] - cum_rows[r:r + 1, :]
            m = cb * jnp.exp(jnp.where(causal, seg, NEG))
            y = y + _dot(m, jnp.where(lane256 == r, dtx, 0.0))
            yield
        o_ref[0, idx, :] += y
        return jnp.exp(tot_e) * h + states

    def chunk(j, hs):
        return tuple(_interleave([chain(0, j, hs[0]), chain(1, j, hs[1])]))
    h0 = jnp.zeros((128, 256), F32)
    lax.fori_loop(0, nc, chunk, (h0, h0))


def _ssd(proj, lanes, conv_w, conv_b, dt_bias, a_log, d_skip, cols):
    bsz, L, _ = proj.shape
    nc = L // CHUNK
    cx, cbm, ccm, cdt = cols
    nh = d_skip.shape[0]
    dtb = jnp.zeros((1, 128), F32).at[0, :2 * nh].set(dt_bias.reshape(-1))
    alog = jnp.zeros((1, 128), F32).at[0, :2 * nh].set(a_log.reshape(-1))
    dsk = jnp.repeat(d_skip.astype(F32), 64).reshape(1, nh * 64)
    sel = np.zeros((2, 2, 128, 256), np.float32)
    for d in range(2):
        for g in range(2):
            for r in range(4):
                sel[d, g, d * nh + g * 4 + r, r * 64:(r + 1) * 64] = 1.0
    cw_specs = []
    for width, off in ((256, 0), (128, 512), (128, 768)):
        cw_specs += [pl.BlockSpec((CONV_W, width), lambda b, g, off=off, width=width: (0, off // width + g)),
                     pl.BlockSpec((1, width), lambda b, g, off=off, width=width: (0, off // width + g))]
    return pl.pallas_call(
        functools.partial(_ssd_kernel, nc=nc),
        grid=(bsz, 2),
        in_specs=[pl.BlockSpec((1, L, 256), lambda b, g: (b, 0, cx // 256 + g)),
                  pl.BlockSpec((1, L, 128), lambda b, g: (b, 0, cbm // 128 + g)),
                  pl.BlockSpec((1, L, 128), lambda b, g: (b, 0, ccm // 128 + g)),
                  pl.BlockSpec((1, L, 128), lambda b, g: (b, 0, cdt // 128))] + cw_specs + [
                  pl.BlockSpec((1, 128), lambda b, g: (0, 0)),
                  pl.BlockSpec((1, 128), lambda b, g: (0, 0)),
                  pl.BlockSpec((1, 256), lambda b, g: (0, g)),
                  pl.BlockSpec((2, 1, 128, 256), lambda b, g: (0, g, 0, 0))],
        out_specs=pl.BlockSpec((1, L, 256), lambda b, g: (b, 0, g)),
        out_shape=jax.ShapeDtypeStruct((bsz, L, nh * 64), F32),
        scratch_shapes=[pltpu.VMEM((L, 256), F32), pltpu.VMEM((L, 128), F32), pltpu.VMEM((L, 128), F32),
                        pltpu.VMEM((L, 128), F32), pltpu.VMEM((nc, 128, CHUNK), BF16)],
        compiler_params=_cp(("parallel", "parallel")), name="ssd",
    )(proj, proj, proj, lanes, conv_w, conv_b.reshape(1, -1), conv_w, conv_b.reshape(1, -1),
      conv_w, conv_b.reshape(1, -1), dtb, alog, dsk, jnp.asarray(sel))


ML_HP = 2


def _ml_gate_lanes(hh, d):
    li = (hh * 2 + d) * 2
    return li, li + 1


def _ml_gate_columns(nh):
    cols = []
    for hp in range(nh // ML_HP):
        blk = [None] * (4 * ML_HP)
        for hh in range(ML_HP):
            for d in range(2):
                for t, lane in enumerate(_ml_gate_lanes(hh, d)):
                    blk[lane] = (d * 2 + t) * nh + hp * ML_HP + hh
        cols.append(blk)
    return cols


def _mlstm_kernel(q_ref, k_ref, v_ref, g_ref, cwq_ref, cbq_ref, cwk_ref, cbk_ref, gb_ref, o_ref,
                  qc_s, kc_s, kct_s, *, nc, dh):
    q = CHUNK
    first_lat = TM // q

    def prep(i, _):
        t0 = pl.multiple_of(i * q, q)
        idx = pl.ds(t0, q)
        for src, cw, cb, dst, mul in ((q_ref, cwq_ref, cbq_ref, qc_s, 1.0), (k_ref, cwk_ref, cbk_ref, kc_s, dh ** -0.5)):
            dst[idx, :] = _silu(_conv_tile(src, i, nc, cw[...], cb[...], q)) * mul
        for hh in range(ML_HP):
            kct_s[i, hh * dh:(hh + 1) * dh, :] = kc_s[idx, hh * dh:(hh + 1) * dh].T.astype(BF16)
        return 0
    lax.fori_loop(0, nc, prep, 0)

    gb = gb_ref[0]
    lane = lax.broadcasted_iota(jnp.int32, (q, 128), 1)
    eye8 = (lax.broadcasted_iota(jnp.int32, (8, 128), 0) == lax.broadcasted_iota(jnp.int32, (8, 128), 1)).astype(F32)
    o_ref[...] = jnp.zeros_like(o_ref)
    heads = range(ML_HP)

    def direction(d, j, states):
        rev = d == 1
        causal = _tri(q, rev)
        last = 0 if rev else q - 1
        if rev:
            ci = jnp.where(j < first_lat, first_lat - 1 - j, nc - 1 - (j - first_lat))
        else:
            ci = j
        idx = pl.ds(pl.multiple_of(ci * q, q), q)
        g = g_ref[0, idx, :] + gb
        logf = jnp.minimum(g, 0.0) - jnp.log(1.0 + jnp.exp(-jnp.abs(g)))
        is_f = functools.reduce(jnp.logical_or, [lane == _ml_gate_lanes(hh, d)[1] for hh in heads])
        cum = _dot_sel(jnp.where(causal, 1.0, 0.0), jnp.where(is_f, logf, 0.0))
        qc = [qc_s[idx, hh * dh:(hh + 1) * dh] for hh in heads]
        kc = [kc_s[idx, hh * dh:(hh + 1) * dh] for hh in heads]
        vc = [v_ref[0, idx, hh * dh:(hh + 1) * dh] for hh in heads]
        kct = [kct_s[ci, hh * dh:(hh + 1) * dh, :] for hh in heads]
        qk = [_dot(qc[hh], kct[hh]) for hh in heads]
        inter_c = [_dot(qc[hh], states[hh][0]) for hh in heads]
        yield
        rows = _dot_sel(eye8, jnp.where(is_f, cum, g), _NT)
        yield
        w, w_s, g_in, m_t, keep, m_new, wv, upd = [], [], [], [], [], [], [], []
        for hh in heads:
            li, lf = _ml_gate_lanes(hh, d)
            m_prev = states[hh][2]
            i_col, i_row = g[:, li:li + 1], rows[li:li + 1, :]
            b_col, b_row = cum[:, lf:lf + 1], rows[lf:lf + 1, :]
            dlog = jnp.where(causal, b_col - b_row + i_row, NEG)
            inter = b_col + m_prev
            m_t.append(jnp.maximum(jnp.max(dlog, axis=-1, keepdims=True), inter))
            w.append(jnp.exp(dlog - m_t[hh]) * qk[hh])
            g_in.append(jnp.exp(inter - m_t[hh]))
            b_end = b_col[last:last + 1, :]
            g_s = b_end - b_col + i_col
            m_new.append(jnp.maximum(jnp.max(g_s, axis=0, keepdims=True), b_end + m_prev))
            w_s.append(jnp.exp(g_s - m_new[hh]))
            keep.append(jnp.exp(b_end + m_prev - m_new[hh]))
            wv.append(_dot(w[hh], vc[hh]))
            upd.append(_dot(kct[hh], w_s[hh] * vc[hh]))
        yield
        new_states = []
        for hh in heads:
            c_prev, n_prev, _ = states[hh]
            num = wv[hh] + g_in[hh] * inter_c[hh]
            den = (jnp.sum(w[hh], axis=-1, keepdims=True)
                   + g_in[hh] * jnp.sum(qc[hh] * n_prev, axis=-1, keepdims=True))
            o_ref[0, idx, hh * dh:(hh + 1) * dh] += num / jnp.maximum(jnp.abs(den), jnp.exp(-m_t[hh]))
            new_states.append((keep[hh] * c_prev + upd[hh],
                               keep[hh] * n_prev + jnp.sum(w_s[hh] * kc[hh], axis=0, keepdims=True), m_new[hh]))
        return tuple(new_states)

    def chunk(j, states):
        return tuple(_interleave([direction(d, j, states[d]) for d in (0, 1)]))
    init = (jnp.zeros((dh, dh), F32), jnp.zeros((1, dh), F32), jnp.full((1, 1), NEG, F32))
    lax.fori_loop(0, nc, chunk, tuple(tuple(init for _ in heads) for _ in (0, 1)))


def _mlstm(proj, lanes, conv_w, conv_b, gate_b, cols):
    bsz, L, _ = proj.shape
    nc = L // CHUNK
    cq, ck, cv, cg = cols
    nh = gate_b.shape[-1]
    dh = conv_w.shape[1] // (2 * nh)
    bw = ML_HP * dh
    gcols = np.asarray(_ml_gate_columns(nh))
    gbl = jnp.zeros((nh // ML_HP, 1, 128), F32).at[:, 0, :gcols.shape[1]].set(gate_b.reshape(-1)[gcols])
    return pl.pallas_call(
        functools.partial(_mlstm_kernel, nc=nc, dh=dh),
        grid=(bsz, nh // ML_HP),
        in_specs=[pl.BlockSpec((1, L, bw), lambda b, h: (b, 0, cq // bw + h)),
                  pl.BlockSpec((1, L, bw), lambda b, h: (b, 0, ck // bw + h)),
                  pl.BlockSpec((1, L, bw), lambda b, h: (b, 0, cv // bw + h)),
                  pl.BlockSpec((1, L, 128), lambda b, h: (b, 0, cg // 128 + h)),
                  pl.BlockSpec((CONV_W, bw), lambda b, h: (0, h)),
                  pl.BlockSpec((1, bw), lambda b, h: (0, h)),
                  pl.BlockSpec((CONV_W, bw), lambda b, h: (0, nh // ML_HP + h)),
                  pl.BlockSpec((1, bw), lambda b, h: (0, nh // ML_HP + h)),
                  pl.BlockSpec((1, 1, 128), lambda b, h: (h, 0, 0))],
        out_specs=pl.BlockSpec((1, L, bw), lambda b, h: (b, 0, h)),
        out_shape=jax.ShapeDtypeStruct((bsz, L, nh * dh), F32),
        scratch_shapes=[pltpu.VMEM((L, bw), F32), pltpu.VMEM((L, bw), F32), pltpu.VMEM((nc, bw, CHUNK), BF16)],
        compiler_params=_cp(("parallel", "parallel")), name="mlstm",
    )(proj, proj, proj, lanes, conv_w, conv_b.reshape(1, -1), conv_w, conv_b.reshape(1, -1), gbl)


def kernel(x, c, ctx, c_ctx, mod_w, mod_b, norm1_g, norm2_g, even_w_in, even_w_out, lru_conv_w, lru_conv_b, lru_wa, lru_ba, lru_wx, lru_bx, lru_lam, na_q_g, na_k_g, na_rpb, odd_w_in, odd_w_out, ssd_conv_w, ssd_conv_b, ssd_dt_bias, ssd_a_log, ssd_d, ssd_norm_g, ml_conv_w, ml_conv_b, ml_gate_b, ml_norm_g, moe_router_g, moe_router_e, moe_w1, moe_w3, moe_w2):
    bsz, S, d = x.shape
    lc = ctx.shape[1]
    assert lc == TM and S % TM == 0 and mod_w.shape[0] == 2
    nt = (lc + S) // TM
    L = lc + S

    cc = jnp.zeros((8, d), F32).at[0].set(c_ctx).at[1:1 + bsz].set(c)
    mod_all = _modulation(cc, mod_w, mod_b)

    def mod_for(l):
        m = mod_all[l].reshape(8, 6, d)
        return jnp.stack([jnp.broadcast_to(m[0], (bsz, 6, d)), m[1:1 + bsz]], axis=1)

    def router_w(l):
        w = jnp.zeros((d, 128), F32).at[:, :EXPERT_LANE0].set(moe_router_g[l]) \
            .at[:, EXPERT_LANE0:EXPERT_LANE0 + N_EXPERTS].set(moe_router_e[l])
        hi = w.astype(BF16)
        return jnp.concatenate([hi, (w - hi.astype(F32)).astype(BF16)], axis=1)

    xx = (ctx, x)

    mod0 = mod_for(0)
    (proj,) = _inproj(xx, mod0, norm1_g[0], even_w_in[0].astype(BF16))
    lw = lru_conv_w.shape[-1]
    r = _lru(proj, lru_conv_w[0], lru_conv_b[0], lru_wa[0], lru_ba[0], lru_wx[0], lru_bx[0], lru_lam[0])
    a = _na(proj, na_q_g[0], na_k_g[0], na_rpb[0], col0=2 * lw)
    x1, h2, gate = _outproj([(r, 0, lw), (proj, 1, lw), (a, 0, lw)], xx, even_w_out[0].astype(BF16), mod0,
                            norm2_g[0], router_w(0), even=True, tile0=0, ntiles=nt)
    ew = (moe_w1.astype(BF16), moe_w3.astype(BF16), moe_w2.astype(BF16))
    moe0 = _moe(h2.reshape(bsz * L, d), gate.reshape(bsz * L, 128), 0, *ew).reshape(bsz, L, d)

    mod1 = mod_for(1)
    sw = ssd_d.shape[-1] * 64
    xbc = ssd_conv_w.shape[-1]
    mw = ml_norm_g.shape[-1]
    w = odd_w_in[0]
    o = np.cumsum([0, sw, xbc, 2 * ssd_d.shape[-1], mw, mw, mw, mw])
    ndt = 2 * ssd_d.shape[-1]
    dt_blk = jnp.concatenate([w[:, o[2]:o[3]], jnp.zeros((d, 128 - ndt), F32)], axis=1)
    gate_blks = [jnp.concatenate([w[:, o[7] + np.asarray(blk)], jnp.zeros((d, 128 - len(blk)), F32)], axis=1)
                 for blk in _ml_gate_columns(ml_gate_b.shape[-1])]
    w_odd = jnp.concatenate([w[:, :o[2]], w[:, o[3]:o[7]], dt_blk] + gate_blks, axis=1).astype(BF16)
    cz, cxs = 0, sw
    cB, cC = cxs + sw, cxs + sw + (xbc - sw) // 2
    cq = sw + xbc
    ck, cv, co, csm = cq + mw, cq + 2 * mw, cq + 3 * mw, cq + 4 * mw
    x0, proj1, lanes1 = _inproj(x1, mod1, norm1_g[1], w_odd, prev=(moe0, mod0), n_f32=w_odd.shape[1] - csm)
    ys = _ssd(proj1, lanes1, ssd_conv_w[0], ssd_conv_b[0], ssd_dt_bias[0], ssd_a_log[0], ssd_d[0], (cxs, cB, cC, 0))
    hm = _mlstm(proj1, lanes1, ml_conv_w[0], ml_conv_b[0], ml_gate_b[0], (cq, ck, cv, 128))
    x2, h2b, gate1 = _outproj([(ys, 0, sw), (proj1, cz // sw, sw), (hm, 0, mw), (proj1, co // mw, mw),
                               (ssd_norm_g[0].reshape(1, sw), 0, sw), (ml_norm_g[0].reshape(1, mw), 0, mw)],
                              x0, odd_w_out[0].astype(BF16), mod1, norm2_g[1], router_w(1),
                              even=False, tile0=1, ntiles=nt - 1)
    out = _moe(h2b.reshape(bsz * S, d), gate1.reshape(bsz * S, 128), 1, *ew, resid=(x2.reshape(bsz * S, d), mod1))
    return out.reshape(bsz, S, d)
```

```python
import functools
import math

import jax
import jax.numpy as jnp
import numpy as np
from jax import lax
from jax.experimental import pallas as pl
from jax.experimental.pallas import tpu as pltpu

F32 = jnp.float32
BF16 = jnp.bfloat16
HI = lax.Precision.HIGHEST

EPS = 1e-6
NEG = -1e30
GRID_W = 64
CONV_W = 4
LRU_C = 8.0
TM = 256
CHUNK = 256
NA_RQ = 4
NA_RK = 12
N_EXPERTS = 16
EXPERT_LANE0 = 4
VMEM_LIMIT = 56 * 1024 * 1024


def _cp(sem, vmem=VMEM_LIMIT):
    return pltpu.CompilerParams(dimension_semantics=sem, vmem_limit_bytes=vmem)


def _sigmoid(x):
    return jax.nn.sigmoid(x)


def _silu(x):
    return x * jax.nn.sigmoid(x)


def _softplus(x):
    return jnp.maximum(x, 0.0) + jnp.log(1.0 + jnp.exp(-jnp.abs(x)))


def _gelu_tanh(x):
    return 0.5 * x * (1.0 + jnp.tanh(math.sqrt(2.0 / math.pi) * (x + 0.044715 * (x * x * x))))


def _rms(x, axis=-1):
    return x * lax.rsqrt(jnp.mean(x * x, axis=axis, keepdims=True) + EPS)


def _dot(a, b):
    return jnp.dot(a.astype(BF16), b.astype(BF16), preferred_element_type=F32)


def _dot_hi(a, b):
    return jnp.dot(a, b, precision=HI, preferred_element_type=F32)


def _split3(x):
    x1 = x.astype(BF16)
    r = x - x1.astype(F32)
    x2 = r.astype(BF16)
    x3 = (r - x2.astype(F32)).astype(BF16)
    return x1, x2, x3


def _dot_sel(sel, x, dims=(((1,), (0,)), ((), ()))):
    s = sel.astype(BF16)
    x1, x2, x3 = _split3(x)
    d = lambda xi: lax.dot_general(s, xi, dims, preferred_element_type=F32)
    return (d(x3) + d(x2)) + d(x1)


def _dot_sel_r(x, sel):
    s = sel.astype(BF16)
    x1, x2, x3 = _split3(x)
    d = lambda xi: jnp.dot(xi, s, preferred_element_type=F32)
    return (d(x3) + d(x2)) + d(x1)


_NT = (((1,), (1,)), ((), ()))


def _interleave(gens):
    results = [None] * len(gens)
    live = list(range(len(gens)))
    while live:
        for i in list(live):
            try:
                next(gens[i])
            except StopIteration as stop:
                results[i] = stop.value
                live.remove(i)
    return results


def _dot_nt(a, b):
    return lax.dot_general(a.astype(BF16), b.astype(BF16), (((1,), (1,)), ((), ())),
                           preferred_element_type=F32)


def _mod_kernel(c_ref, w_ref, b_ref, o_ref):
    c = c_ref[...]
    o_ref[0] = _dot_hi(_silu(c), w_ref[0]) + b_ref[0]


def _modulation(cc, mod_w, mod_b):
    depth, d, n = mod_w.shape
    tn = 1536
    return pl.pallas_call(
        _mod_kernel,
        grid=(depth, n // tn),
        in_specs=[pl.BlockSpec((8, d), lambda l, j: (0, 0)),
                  pl.BlockSpec((1, d, tn), lambda l, j: (l, 0, j)),
                  pl.BlockSpec((1, 1, tn), lambda l, j: (l, 0, j))],
        out_specs=pl.BlockSpec((1, 8, tn), lambda l, j: (l, 0, j)),
        out_shape=jax.ShapeDtypeStruct((depth, 8, n), F32),
        compiler_params=_cp(("arbitrary", "arbitrary")),
        name="adaln_mod",
    )(cc, mod_w, mod_b.reshape(depth, 1, n))


def _inproj_kernel(*refs, fuse_prev, n_f32):
    refs = list(refs)
    s_ref = refs.pop() if n_f32 else None
    if fuse_prev:
        x_ref, mo_ref, pmod_ref, mod_ref, g_ref, w_ref, xo_ref, p_ref = refs
        x = x_ref[0] + pmod_ref[0, 0][5:6, :] * mo_ref[0]
        xo_ref[0] = x
    else:
        c_ref, x_ref, mod_ref, g_ref, w_ref, p_ref = refs
        x = _pick_segment(c_ref, x_ref)
    mod = mod_ref[0, 0]
    h = _rms(x) * g_ref[...] * (1.0 + mod[1:2, :]) + mod[0:1, :]
    p = _dot(h, w_ref[...])
    n_main = p.shape[-1] - n_f32
    p_ref[0] = p[:, :n_main].astype(BF16)
    if n_f32:
        s_ref[0] = p[:, n_main:]


def _seg_map(b, i):
    return (b, jnp.minimum(i, 1), 0, 0)


def _segment_specs(d):
    return [pl.BlockSpec((1, TM, d), lambda b, i: (b, 0, 0)),
            pl.BlockSpec((1, TM, d), lambda b, i: (b, jnp.maximum(i - 1, 0), 0))]


def _pick_segment(c_ref, x_ref):
    return jnp.where(pl.program_id(1) == 0, c_ref[0], x_ref[0])


def _inproj(x, mod, g, w, prev=None, n_f32=0):
    if prev is None:
        ctx, lat = x
        bsz, S, d = lat.shape
        L = TM + S
    else:
        bsz, L, d = x.shape
    n = w.shape[1]
    nt = L // TM
    tok = pl.BlockSpec((1, TM, d), lambda b, i: (b, i, 0))
    modspec = pl.BlockSpec((1, 1, 6, d), _seg_map)
    tail = [modspec, pl.BlockSpec((1, d), lambda b, i: (0, 0)), pl.BlockSpec((d, n), lambda b, i: (0, 0))]
    pspecs = [pl.BlockSpec((1, TM, n - n_f32), lambda b, i: (b, i, 0))]
    pshapes = [jax.ShapeDtypeStruct((bsz, L, n - n_f32), BF16)]
    if n_f32:
        pspecs.append(pl.BlockSpec((1, TM, n_f32), lambda b, i: (b, i, 0)))
        pshapes.append(jax.ShapeDtypeStruct((bsz, L, n_f32), F32))
    if prev is None:
        return pl.pallas_call(
            functools.partial(_inproj_kernel, fuse_prev=False, n_f32=n_f32),
            grid=(bsz, nt), in_specs=_segment_specs(d) + tail, out_specs=pspecs, out_shape=pshapes,
            compiler_params=_cp(("parallel", "arbitrary")), name="inproj",
        )(ctx, lat, mod, g.reshape(1, d), w)
    moe_out, pmod = prev
    return pl.pallas_call(
        functools.partial(_inproj_kernel, fuse_prev=True, n_f32=n_f32),
        grid=(bsz, nt), in_specs=[tok, tok, modspec] + tail,
        out_specs=[tok] + pspecs, out_shape=[jax.ShapeDtypeStruct(x.shape, F32)] + pshapes,
        compiler_params=_cp(("parallel", "parallel")), name="inproj_res",
    )(x, moe_out, pmod, mod, g.reshape(1, d), w)


def _conv_tile(ref, i, nt, cw, cb, width=TM):
    L = nt * width
    t0 = pl.multiple_of(i * width, width)
    cur = ref[0, pl.ds(t0, width), :].astype(F32)
    prev = ref[0, pl.ds(pl.multiple_of(jnp.maximum(t0 - 16, 0), 16), 16), :].astype(F32)[8:16]
    nxt = ref[0, pl.ds(pl.multiple_of(jnp.minimum(t0 + width, L - 16), 16), 16), :].astype(F32)[0:8]
    return _conv_vals(cur, prev, nxt, i, nt, cw, cb, width, first_lat=TM // width)


def _conv_vals(cur, prev, nxt, i, nt, cw, cb, width, first_lat):
    prev = jnp.where((i != 0) & (i != first_lat), prev, 0.0)
    nxt = jnp.where((i != first_lat - 1) & (i != nt - 1), nxt, 0.0)
    cat = jnp.concatenate([prev, cur, nxt], axis=0)
    return (cw[0:1] * cat[6:6 + width] + cw[1:2] * cat[7:7 + width] + cw[2:3] * cur
            + cw[3:4] * cat[9:9 + width] + cb)


def _lru_kernel(ux_ref, cw_ref, cb_ref, gw_ref, gb_ref, lam_ref, o_ref, *, nt):
    cw = cw_ref[...]
    cb = cb_ref[...]
    lam = lam_ref[0]
    sp = _softplus(-lam)
    row = lax.broadcasted_iota(jnp.int32, (TM, 128), 0) & 7

    def gates(i, d):
        xl = _conv_tile(ux_ref, i, nt, cw, cb)
        g = _dot(xl, gw_ref[0, d]) + gb_ref[0, d]
        r = _sigmoid(g[:, :128])
        ig = _sigmoid(g[:, 128:])
        log_a = -LRU_C * r * sp[d:d + 1]
        a = jnp.exp(log_a)
        u = jnp.sqrt(1.0 - a * a) * (ig * xl)
        return a, u

    def scan_tile(i, d, carry, accumulate):
        a, u = gates(i, d)
        rev = d == 1
        for k in (1, 2, 4):
            sh = 8 - k if rev else k
            ok = (row < 8 - k) if rev else (row >= k)
            ash = pltpu.roll(a.reshape(TM // 8, 8, 128), sh, 1).reshape(TM, 128)
            ush = pltpu.roll(u.reshape(TM // 8, 8, 128), sh, 1).reshape(TM, 128)
            u = jnp.where(ok, u + a * ush, u)
            a = jnp.where(ok, a * ash, a)
        t0 = i * TM
        groups = range(TM // 8)
        for s in (reversed(groups) if rev else groups):
            h = u[s * 8:(s + 1) * 8] + a[s * 8:(s + 1) * 8] * carry
            carry = h[0:1] if rev else h[7:8]
            idx = pl.ds(pl.multiple_of(t0 + s * 8, 8), 8)
            if accumulate:
                o_ref[0, idx, :] += h
            else:
                o_ref[0, idx, :] = h
        return carry

    zero = jnp.zeros((1, 128), F32)
    lax.fori_loop(0, nt, lambda i, c: scan_tile(i, 0, c, False), zero)
    lax.fori_loop(0, nt, lambda j, c: scan_tile(jnp.where(j == 0, 0, nt - j), 1, c, True), zero)


def _lru(proj, conv_w, conv_b, wa, ba, wx, bx, lam):
    bsz, L, _ = proj.shape
    nt = L // TM
    width = conv_w.shape[1]
    ng = width // 128

    def blockdiag(w):
        w = w.reshape(2, ng, 2, 64, 64)
        z = jnp.zeros_like(w[:, :, 0])
        top = jnp.concatenate([w[:, :, 0], z], axis=-1)
        bot = jnp.concatenate([z, w[:, :, 1]], axis=-1)
        return jnp.concatenate([top, bot], axis=-2)
    gw = jnp.concatenate([blockdiag(wa), blockdiag(wx)], axis=-1).transpose(1, 0, 2, 3).astype(BF16)
    gb = jnp.concatenate([ba.reshape(2, ng, 1, 128), bx.reshape(2, ng, 1, 128)], axis=-1).transpose(1, 0, 2, 3)
    lam_g = lam.reshape(2, ng, 128).transpose(1, 0, 2)
    return pl.pallas_call(
        functools.partial(_lru_kernel, nt=nt),
        grid=(bsz, ng),
        in_specs=[pl.BlockSpec((1, L, 128), lambda b, c: (b, 0, c)),
                  pl.BlockSpec((CONV_W, 128), lambda b, c: (0, c)),
                  pl.BlockSpec((1, 128), lambda b, c: (0, c)),
                  pl.BlockSpec((1, 2, 128, 256), lambda b, c: (c, 0, 0, 0)),
                  pl.BlockSpec((1, 2, 1, 256), lambda b, c: (c, 0, 0, 0)),
                  pl.BlockSpec((1, 2, 128), lambda b, c: (c, 0, 0))],
        out_specs=pl.BlockSpec((1, L, 128), lambda b, c: (b, 0, c)),
        out_shape=jax.ShapeDtypeStruct((bsz, L, width), F32),
        compiler_params=_cp(("parallel", "parallel")), name="rglru",
    )(proj, conv_w, conv_b.reshape(1, width), gw, gb, lam_g)


def _na_bias_table(rpb, rows):
    nh = rpb.shape[0]
    win_c = (rpb.shape[2] + 1) // 2
    qc = np.arange(GRID_W)[:, None]
    kc = np.arange(GRID_W)[None, :]
    cstart = np.clip(qc - win_c // 2, 0, GRID_W - win_c)
    col_ok = (kc >= cstart) & (kc < cstart + win_c)
    dcol = np.clip(kc - qc + (win_c - 1), 0, 2 * win_c - 2)
    oc = (np.arange(2 * win_c - 1)[:, None, None] == dcol[None]).astype(np.float32)
    ct = jnp.where(jnp.asarray(col_ok), jnp.einsum('hrc,cqk->hrqk', rpb.astype(F32), jnp.asarray(oc), precision=HI),
                   NEG)
    return pl.pallas_call(
        functools.partial(_na_bias_kernel, rows=rows),
        grid=(nh,),
        in_specs=[pl.BlockSpec((1,) + ct.shape[1:], lambda h: (h, 0, 0, 0))],
        out_specs=pl.BlockSpec((3, 1, TM, NA_RK * GRID_W), lambda h: (0, h, 0, 0)),
        out_shape=jax.ShapeDtypeStruct((3, nh, TM, NA_RK * GRID_W), F32),
        compiler_params=_cp(("parallel",)), name="nbr_bias",
    )(ct)


def _na_bias_kernel(ct_ref, o_ref, *, rows):
    win_r = (ct_ref.shape[1] + 1) // 2
    neg = jnp.full((GRID_W, GRID_W), NEG, F32)
    for p, (r0, w0) in enumerate(((0, 0), (2 * NA_RQ, NA_RQ), (rows - NA_RQ, rows - NA_RK))):
        for a in range(NA_RQ):
            r = r0 + a
            rstart = min(max(r - win_r // 2, 0), rows - win_r)
            tiles = [ct_ref[0, w0 + b - r + win_r - 1] if rstart <= w0 + b < rstart + win_r else neg
                     for b in range(NA_RK)]
            for bp in range(NA_RK // 2):
                o_ref[p, 0, a * GRID_W:(a + 1) * GRID_W, bp * 2 * GRID_W:(bp + 1) * 2 * GRID_W] = (
                    jnp.concatenate(tiles[2 * bp:2 * bp + 2], axis=1))


def _pair_rms(x, lo):
    x2 = x * x
    s0 = jnp.sum(jnp.where(lo, x2, 0.0), axis=-1, keepdims=True)
    s1 = jnp.sum(jnp.where(lo, 0.0, x2), axis=-1, keepdims=True)
    return x * lax.rsqrt(jnp.where(lo, s0, s1) * (2.0 / x.shape[-1]) + EPS)


def _na_kernel(q_ref, k_ref, v_ref, bias0_ref, bias1_ref, qg_ref, kg_ref, o_ref, kn_s, vb_s, *, rows, hd):
    j = pl.program_id(2)
    L = k_ref.shape[1]
    nb = L // TM
    nkeys = NA_RK * GRID_W
    lo = lax.broadcasted_iota(jnp.int32, (1, 2 * hd), 1) < hd

    @pl.when(j == 0)
    def _prep():
        def body(t, _):
            idx = pl.ds(pl.multiple_of(t * TM, TM), TM)
            kn_s[idx, :] = (_pair_rms(k_ref[0, idx, :].astype(F32), lo) * kg_ref[...]).astype(BF16)
            vb_s[idx, :] = v_ref[0, idx, :].astype(BF16)
            return 0
        lax.fori_loop(0, nb, body, 0)

    kctx = kn_s[0:TM, :]
    vctx = vb_s[0:TM, :]
    qscale = qg_ref[...] * hd ** -0.5

    def head(q, bias, kwin, vwin):
        s_c = _dot_nt(q, kctx)
        if kwin is not None:
            s_w = _dot_nt(q, kwin) + bias
        yield
        m = jnp.max(s_c, axis=-1, keepdims=True)
        if kwin is not None:
            m = jnp.maximum(m, jnp.max(s_w, axis=-1, keepdims=True))
            p_w = jnp.exp(s_w - m)
        p_c = jnp.exp(s_c - m)
        den = jnp.sum(p_c, axis=-1, keepdims=True)
        num = _dot(p_c, vctx)
        if kwin is not None:
            den = den + jnp.sum(p_w, axis=-1, keepdims=True)
            num = num + _dot(p_w, vwin)
        yield
        return num / den

    def tile_chains(slot, bias_ref, tile):
        qn = _pair_rms(q_ref[0, slot * TM:(slot + 1) * TM, :].astype(F32), lo) * qscale
        q_h = [jnp.where(lo, qn, 0.0).astype(BF16), jnp.where(lo, 0.0, qn).astype(BF16)]
        if tile is None:
            return [head(q_h[hh], None, None, None) for hh in range(2)]
        w0 = jnp.clip((tile - 1) * NA_RQ - NA_RQ, 0, rows - NA_RK)
        start = pl.multiple_of(TM + w0 * GRID_W, GRID_W)
        kwin = kn_s[pl.ds(start, nkeys), :]
        vwin = vb_s[pl.ds(start, nkeys), :]
        return [head(q_h[hh], bias_ref[0, hh], kwin, vwin) for hh in range(2)]

    def run(chains):
        outs = _interleave(chains)
        for slot in range(len(outs) // 2):
            o_ref[0, slot * TM:(slot + 1) * TM, :] = jnp.where(lo, outs[2 * slot], outs[2 * slot + 1]).astype(BF16)

    last = (nb - 1) // 2

    @pl.when(j == 0)
    def _():
        run(tile_chains(0, None, None) + tile_chains(1, bias1_ref, 1))

    @pl.when((j > 0) & (j < last))
    def _():
        run(tile_chains(0, bias0_ref, 2 * j) + tile_chains(1, bias1_ref, 2 * j + 1))

    @pl.when(j == last)
    def _():
        run(tile_chains(0, bias0_ref, 2 * j))


def _na(proj, q_g, k_g, rpb, col0):
    bsz, L, _ = proj.shape
    nh = rpb.shape[0]
    hd = q_g.shape[0]
    width = nh * hd
    rows = (L - TM) // GRID_W
    nb = L // TM
    bias = _na_bias_table(rpb, rows)
    qb, kb, vb = col0 // 128, (col0 + width) // 128, (col0 + 2 * width) // 128

    assert nb % 2 == 1

    def pat(t):
        return jnp.where(t <= 1, 0, jnp.where(t >= nb - 1, 2, 1))
    bias_spec = lambda slot: pl.BlockSpec((1, 2, TM, NA_RK * GRID_W), lambda b, h, j: (pat(2 * j + slot), h, 0, 0))
    return pl.pallas_call(
        functools.partial(_na_kernel, rows=rows, hd=hd),
        grid=(bsz, width // 128, (nb + 1) // 2),
        in_specs=[pl.BlockSpec((1, 2 * TM, 128), lambda b, h, j: (b, j, qb + h)),
                  pl.BlockSpec((1, L, 128), lambda b, h, j: (b, 0, kb + h)),
                  pl.BlockSpec((1, L, 128), lambda b, h, j: (b, 0, vb + h)),
                  bias_spec(0), bias_spec(1),
                  pl.BlockSpec((1, 2 * hd), lambda b, h, j: (0, 0)),
                  pl.BlockSpec((1, 2 * hd), lambda b, h, j: (0, 0))],
        out_specs=pl.BlockSpec((1, 2 * TM, 128), lambda b, h, j: (b, j, h)),
        out_shape=jax.ShapeDtypeStruct((bsz, L, width), BF16),
        scratch_shapes=[pltpu.VMEM((L, 128), BF16), pltpu.VMEM((L, 128), BF16)],
        compiler_params=_cp(("parallel", "parallel", "arbitrary")), name="nbr_attn",
    )(proj, proj, proj, bias, bias, jnp.tile(q_g, 2).reshape(1, 2 * hd), jnp.tile(k_g, 2).reshape(1, 2 * hd))


def _route(lg):
    lane = lax.broadcasted_iota(jnp.int32, lg.shape, 1)
    lane_f = lane.astype(F32)
    is_g = lane < EXPERT_LANE0
    gl = jnp.where(is_g, lg, NEG)
    gmax = jnp.max(gl, axis=-1, keepdims=True)
    gsel = jnp.min(jnp.where(is_g & (gl == gmax), lane_f, 1e9), axis=-1, keepdims=True)
    g_w = 1.0 / jnp.sum(jnp.where(is_g, jnp.exp(gl - gmax), 0.0), axis=-1, keepdims=True)
    grp = ((lane - EXPERT_LANE0) >> 2).astype(F32)
    in_g = (lane >= EXPERT_LANE0) & (lane < EXPERT_LANE0 + N_EXPERTS) & (grp == gsel)
    el = jnp.where(in_g, lg, NEG)
    v1 = jnp.max(el, axis=-1, keepdims=True)
    i1 = jnp.min(jnp.where(in_g & (el == v1), lane_f, 1e9), axis=-1, keepdims=True)
    el2 = jnp.where(lane_f == i1, NEG, el)
    v2 = jnp.max(el2, axis=-1, keepdims=True)
    i2 = jnp.min(jnp.where(in_g & (lane_f != i1) & (el2 == v2), lane_f, 1e9), axis=-1, keepdims=True)
    t = jnp.exp(v2 - v1)
    w1 = g_w / (1.0 + t)
    w2 = g_w * t / (1.0 + t)
    return (jnp.where(lane_f == i1, w1, 0.0) + jnp.where(lane_f == i2, w2, 0.0)
            + jnp.where(lane_f == gsel, 1.0, 0.0))


def _outproj_kernel(*refs, even):
    if even:
        (r_ref, ug_ref, a_ref, c_ref, x_ref, w_ref, mod_ref, g2_ref, rw_ref, x1_ref, h2_ref, gate_ref) = refs
        is_ctx = pl.program_id(1) == 0
        x_rows = lambda rs: jnp.where(is_ctx, c_ref[0, rs, :], x_ref[0, rs, :])
    else:
        (ys_ref, z_ref, hm_ref, mo_ref, sg_ref, mg_ref, x_ref, w_ref, mod_ref, g2_ref, rw_ref,
         x1_ref, h2_ref, gate_ref) = refs
        x_rows = lambda rs: x_ref[0, rs, :]
    mod = mod_ref[0, 0]

    def part(rs):
        if even:
            y_in = jnp.concatenate([(r_ref[0, rs, :] * _gelu_tanh(ug_ref[0, rs, :].astype(F32))).astype(BF16),
                                    a_ref[0, rs, :]], axis=-1)
        else:
            ys = ys_ref[0, rs, :] * _silu(z_ref[0, rs, :].astype(F32))
            sg = sg_ref[...]
            mg = mg_ref[...]
            hm = hm_ref[0, rs, :]
            sig_o = _sigmoid(mo_ref[0, rs, :].astype(F32))
            gw = ys.shape[-1] // 2
            parts = [_rms(ys[:, g * gw:(g + 1) * gw]) * sg[:, g * gw:(g + 1) * gw] for g in range(2)]
            hw = 128
            parts += [_rms(hm[:, h * hw:(h + 1) * hw]) * mg[:, h * hw:(h + 1) * hw] * sig_o[:, h * hw:(h + 1) * hw]
                      for h in range(hm.shape[-1] // hw)]
            y_in = jnp.concatenate(parts, axis=-1)
        y = _dot(y_in, w_ref[...])
        yield
        x1 = x_rows(rs) + mod[2:3, :] * y
        x1_ref[0, rs, :] = x1
        h2 = _rms(x1) * g2_ref[...] * (1.0 + mod[4:5, :]) + mod[3:4, :]
        h2_ref[0, rs, :] = h2.astype(BF16)
        hi = h2.astype(BF16)
        lo = (h2 - hi.astype(F32)).astype(BF16)
        lg2 = jnp.dot(hi, rw_ref[...], preferred_element_type=F32)
        lg1 = jnp.dot(lo, rw_ref[:, 0:128], preferred_element_type=F32)
        yield
        gate_ref[0, rs, :] = _route(lg2[:, 0:128] + lg2[:, 128:256] + lg1)

    nparts = 2
    rows = TM // nparts
    _interleave([part(pl.ds(p * rows, rows)) for p in range(nparts)])


def _outproj(mix_inputs, x, w, mod, g2, rw, even, tile0, ntiles):
    xs = list(x) if even else [x]
    bsz, _, d = xs[-1].shape
    specs, args = [], []
    for arr, cb, wdt in mix_inputs:
        if arr.ndim == 3:
            specs.append(pl.BlockSpec((1, TM, wdt), lambda b, i, cb=cb: (b, i + tile0, cb)))
        else:
            specs.append(pl.BlockSpec((1, wdt), lambda b, i: (0, 0)))
        args.append(arr)
    specs += _segment_specs(d) if even else [pl.BlockSpec((1, TM, d), lambda b, i: (b, i + tile0, 0))]
    specs += [pl.BlockSpec(w.shape, lambda b, i: (0, 0)),
              pl.BlockSpec((1, 1, 6, d), lambda b, i: (b, jnp.minimum(i + tile0, 1), 0, 0)),
              pl.BlockSpec((1, d), lambda b, i: (0, 0)),
              pl.BlockSpec(rw.shape, lambda b, i: (0, 0))]
    args += xs + [w, mod, g2.reshape(1, d), rw]
    lo = ntiles * TM
    return pl.pallas_call(
        functools.partial(_outproj_kernel, even=even),
        grid=(bsz, ntiles), in_specs=specs,
        out_specs=[pl.BlockSpec((1, TM, d), lambda b, i: (b, i, 0)),
                   pl.BlockSpec((1, TM, d), lambda b, i: (b, i, 0)),
                   pl.BlockSpec((1, TM, 128), lambda b, i: (b, i, 0))],
        out_shape=[jax.ShapeDtypeStruct((bsz, lo, d), F32), jax.ShapeDtypeStruct((bsz, lo, d), BF16),
                   jax.ShapeDtypeStruct((bsz, lo, 128), F32)],
        compiler_params=_cp(("parallel", "parallel")), name="outproj_even" if even else "outproj_odd",
    )(*args)


MOE_CH = 128
GROUP_SIZE = 4
MOE_VMEM_LIMIT = 62 * 1024 * 1024
N_GROUPS = N_EXPERTS // GROUP_SIZE


def _moe_kernel(*refs, residual):
    if residual:
        x_ref, g_ref, w1_ref, w3_ref, w2_ref, x1_ref, mod_ref, o_ref, xs_s, ys_s, gs_s, pt_s, plan_s = refs
    else:
        x_ref, g_ref, w1_ref, w3_ref, w2_ref, o_ref, xs_s, ys_s, gs_s, pt_s, plan_s = refs
    grp = pl.program_id(1)
    tb = x_ref.shape[0]
    nch = xs_s.shape[0] // MOE_CH

    @pl.when(grp == 0)
    def _plan():
        g = g_ref[...]
        lane = lax.broadcasted_iota(jnp.int32, g.shape, 1)
        oh = jnp.where(lane < N_GROUPS, g, 0.0)
        earlier = (lax.broadcasted_iota(jnp.int32, (tb, tb), 0) > lax.broadcasted_iota(jnp.int32, (tb, tb), 1))
        rank = jnp.dot(jnp.where(earlier, 1.0, 0.0).astype(BF16), oh.astype(BF16),
                       preferred_element_type=F32)
        cnt = jnp.sum(oh, axis=0, keepdims=True)
        lane1 = lax.broadcasted_iota(jnp.int32, (1, 128), 1)
        off = jnp.int32(0)
        offv = jnp.zeros((1, 128), F32)
        for gi in range(N_GROUPS):
            n = jnp.sum(jnp.where(lane1 == gi, cnt, 0.0)).astype(jnp.int32)
            nchunks = (n + (MOE_CH - 1)) // MOE_CH
            plan_s[gi] = off // MOE_CH
            plan_s[N_GROUPS + gi] = nchunks
            offv = offv + jnp.where(lane1 == gi, off.astype(F32), 0.0)
            off = off + nchunks * MOE_CH
        pos_col = jnp.sum(oh * (rank + offv), axis=1, keepdims=True)
        posb = jnp.broadcast_to(pos_col, (tb, 128))
        pos_row = jnp.concatenate([posb[i * 128:(i + 1) * 128, :].T[0:1, :] for i in range(tb // 128)], axis=1)
        x = x_ref[...]
        g_hi = g.astype(BF16)
        g_lo = (g - g_hi.astype(F32)).astype(BF16)
        lane_c = lax.broadcasted_iota(jnp.int32, (tb, MOE_CH), 1).astype(F32)
        row_c = lax.broadcasted_iota(jnp.int32, (MOE_CH, tb), 0).astype(F32)
        used = off // MOE_CH

        def dispatch(c):
            sl = slice(c * MOE_CH, (c + 1) * MOE_CH)
            p = jnp.where(row_c + float(c * MOE_CH) == pos_row, 1.0, 0.0).astype(BF16)
            xs_s[sl, :] = jnp.dot(p, x, preferred_element_type=F32).astype(BF16)
            gs_s[sl, :] = (jnp.dot(p, g_hi, preferred_element_type=F32)
                           + jnp.dot(p, g_lo, preferred_element_type=F32))
        for c in range(nch):
            sl = slice(c * MOE_CH, (c + 1) * MOE_CH)
            pt_s[:, sl] = jnp.where(pos_col == lane_c + float(c * MOE_CH), 1.0, 0.0).astype(BF16)
            if c < tb // MOE_CH:
                dispatch(c)
            else:
                pl.when(c < used)(functools.partial(dispatch, c))
        ys_s[...] = jnp.zeros_like(ys_s)

    c0 = plan_s[grp]
    nchunks = plan_s[N_GROUPS + grp]

    def ffn(chunk0, nrows):
        rows = pl.ds(pl.multiple_of(chunk0 * MOE_CH, MOE_CH), nrows)
        xs = xs_s[rows, :]
        gs = gs_s[rows, :]
        lane_g = lax.broadcasted_iota(jnp.int32, (nrows, 128), 1)
        acc = None
        for k in range(GROUP_SIZE):
            a = jnp.dot(xs, w1_ref[0, k], preferred_element_type=F32)
            b = jnp.dot(xs, w3_ref[0, k], preferred_element_type=F32)
            y = _dot(_silu(a) * b, w2_ref[0, k])
            ge = jnp.sum(jnp.where(lane_g == grp * GROUP_SIZE + k + EXPERT_LANE0, gs, 0.0), axis=-1, keepdims=True)
            acc = ge * y if acc is None else acc + ge * y
        ys_s[rows, :] = acc.astype(BF16)

    wide = 4

    def full(p, _):
        ffn(c0 + wide * p, wide * MOE_CH)
        return 0
    lax.fori_loop(0, nchunks // wide, full, 0)
    rem = nchunks % wide
    for r in range(1, wide):
        pl.when(rem == r)(functools.partial(ffn, c0 + nchunks - r, r * MOE_CH))

    @pl.when(grp == N_GROUPS - 1)
    def _combine():
        out = jnp.dot(pt_s[...], ys_s[...], preferred_element_type=F32)
        if residual:
            out = x1_ref[...] + mod_ref[0, 0][5:6, :] * out
        o_ref[...] = out


def _moe(h2, gate, layer, w1, w3, w2, resid=None):
    t, d = h2.shape
    _, ne, _, ff = w1.shape
    tb = math.gcd(t, 1024)
    npad = tb + N_GROUPS * MOE_CH
    assert ne == N_EXPERTS
    extra_specs, extra_args = [], []
    if resid is not None:
        x1, mod = resid
        per_batch = t // mod.shape[0] // tb
        extra_specs = [pl.BlockSpec((tb, d), lambda i, e: (i, 0)),
                       pl.BlockSpec((1, 1, 6, d), lambda i, e: (i // per_batch, 1, 0, 0))]
        extra_args = [x1, mod]
    return pl.pallas_call(
        functools.partial(_moe_kernel, residual=resid is not None),
        grid=(t // tb, N_GROUPS),
        in_specs=[pl.BlockSpec((tb, d), lambda i, e: (i, 0)),
                  pl.BlockSpec((tb, 128), lambda i, e: (i, 0)),
                  pl.BlockSpec((1, GROUP_SIZE, d, ff), lambda i, e: (layer, e, 0, 0)),
                  pl.BlockSpec((1, GROUP_SIZE, d, ff), lambda i, e: (layer, e, 0, 0)),
                  pl.BlockSpec((1, GROUP_SIZE, ff, d), lambda i, e: (layer, e, 0, 0))] + extra_specs,
        out_specs=pl.BlockSpec((tb, d), lambda i, e: (i, 0)),
        out_shape=jax.ShapeDtypeStruct((t, d), F32),
        scratch_shapes=[pltpu.VMEM((npad, d), BF16), pltpu.VMEM((npad, d), BF16), pltpu.VMEM((npad, 128), F32),
                        pltpu.VMEM((tb, npad), BF16), pltpu.SMEM((2 * N_GROUPS,), jnp.int32)],
        compiler_params=_cp(("parallel", "arbitrary"), MOE_VMEM_LIMIT), name="moe",
    )(h2, gate, w1, w3, w2, *extra_args)


def _tri(q, rev):
    r = lax.broadcasted_iota(jnp.int32, (q, q), 0)
    c = lax.broadcasted_iota(jnp.int32, (q, q), 1)
    return (c >= r) if rev else (c <= r)


def _ssd_kernel(xs_ref, b_ref, c_ref, dt_ref, cwx_ref, cbx_ref, cwb_ref, cbb_ref, cwc_ref, cbc_ref,
                dtb_ref, alog_ref, dsk_ref, sel_ref, o_ref, xc_s, bc_s, cc_s, dt_s, bct_s, *, nc):
    q = CHUNK
    first_lat = TM // q

    dtb = dtb_ref[...]

    def prep(i, _):
        t0 = pl.multiple_of(i * q, q)
        idx = pl.ds(t0, q)
        for src, cw, cb, dst in ((xs_ref, cwx_ref, cbx_ref, xc_s), (b_ref, cwb_ref, cbb_ref, bc_s),
                                 (c_ref, cwc_ref, cbc_ref, cc_s)):
            dst[idx, :] = _silu(_conv_tile(src, i, nc, cw[...], cb[...], q))
        dt_s[idx, :] = _softplus(dt_ref[0, idx, :] + dtb)
        o_ref[0, idx, :] = dsk_ref[...] * xc_s[idx, :]
        bct_s[i] = bc_s[idx, :].T.astype(BF16)
        return 0
    lax.fori_loop(0, nc, prep, 0)

    a_lane = jnp.broadcast_to(-jnp.exp(alog_ref[...]), (8, 128))
    a_e = [_dot_hi(a_lane, sel_ref[d, 0])[0:1, :] for d in (0, 1)]
    lane256 = lax.broadcasted_iota(jnp.int32, (q, 256), 1) // 64
    r8 = lax.broadcasted_iota(jnp.int32, (8, 256), 0)
    l8 = lax.broadcasted_iota(jnp.int32, (8, 256), 1)
    head_rows = (l8 == r8 * 64).astype(F32)

    def chain(d, j, h):
        rev = d == 1
        causal = _tri(q, rev)
        tri = causal.astype(F32)
        sel = sel_ref[d, 0]
        last = 0 if rev else q - 1
        if rev:
            ci = jnp.where(j < first_lat, first_lat - 1 - j, nc - 1 - (j - first_lat))
        else:
            ci = j
        idx = pl.ds(pl.multiple_of(ci * q, q), q)
        dt_e = _dot_sel_r(dt_s[idx, :], sel)
        xc = xc_s[idx, :]
        bc = bc_s[idx, :]
        cc = cc_s[idx, :]
        cb = _dot_nt(cc, bc)
        y_off = _dot(cc, h)
        yield
        cum_e = _dot_sel(tri, dt_e * a_e[d])
        dtx = xc * dt_e
        yield
        cum_rows = _dot_sel(head_rows, cum_e, _NT)
        tot_e = cum_e[last:last + 1, :]
        states = _dot(bct_s[ci], dtx * jnp.exp(tot_e - cum_e))
        yield
        y = jnp.exp(cum_e) * y_off
        for r in range(4):
            seg = cum_e[:, r * 64:r * 64 + 1] - cum_rows[r:r + 1, :]
            m = cb * jnp.exp(jnp.where(causal, seg, NEG))
            y = y + _dot(m, jnp.where(lane256 == r, dtx, 0.0))
            yield
        o_ref[0, idx, :] += y
        return jnp.exp(tot_e) * h + states

    def chunk(j, hs):
        return tuple(_interleave([chain(0, j, hs[0]), chain(1, j, hs[1])]))
    h0 = jnp.zeros((128, 256), F32)
    lax.fori_loop(0, nc, chunk, (h0, h0))


def _ssd(proj, lanes, conv_w, conv_b, dt_bias, a_log, d_skip, cols):
    bsz, L, _ = proj.shape
    nc = L // CHUNK
    cx, cbm, ccm, cdt = cols
    nh = d_skip.shape[0]
    dtb = jnp.zeros((1, 128), F32).at[0, :2 * nh].set(dt_bias.reshape(-1))
    alog = jnp.zeros((1, 128), F32).at[0, :2 * nh].set(a_log.reshape(-1))
    dsk = jnp.repeat(d_skip.astype(F32), 64).reshape(1, nh * 64)
    sel = np.zeros((2, 2, 128, 256), np.float32)
    for d in range(2):
        for g in range(2):
            for r in range(4):
                sel[d, g, d * nh + g * 4 + r, r * 64:(r + 1) * 64] = 1.0
    cw_specs = []
    for width, off in ((256, 0), (128, 512), (128, 768)):
        cw_specs += [pl.BlockSpec((CONV_W, width), lambda b, g, off=off, width=width: (0, off // width + g)),
                     pl.BlockSpec((1, width), lambda b, g, off=off, width=width: (0, off // width + g))]
    return pl.pallas_call(
        functools.partial(_ssd_kernel, nc=nc),
        grid=(bsz, 2),
        in_specs=[pl.BlockSpec((1, L, 256), lambda b, g: (b, 0, cx // 256 + g)),
                  pl.BlockSpec((1, L, 128), lambda b, g: (b, 0, cbm // 128 + g)),
                  pl.BlockSpec((1, L, 128), lambda b, g: (b, 0, ccm // 128 + g)),
                  pl.BlockSpec((1, L, 128), lambda b, g: (b, 0, cdt // 128))] + cw_specs + [
                  pl.BlockSpec((1, 128), lambda b, g: (0, 0)),
                  pl.BlockSpec((1, 128), lambda b, g: (0, 0)),
                  pl.BlockSpec((1, 256), lambda b, g: (0, g)),
                  pl.BlockSpec((2, 1, 128, 256), lambda b, g: (0, g, 0, 0))],
        out_specs=pl.BlockSpec((1, L, 256), lambda b, g: (b, 0, g)),
        out_shape=jax.ShapeDtypeStruct((bsz, L, nh * 64), F32),
        scratch_shapes=[pltpu.VMEM((L, 256), F32), pltpu.VMEM((L, 128), F32), pltpu.VMEM((L, 128), F32),
                        pltpu.VMEM((L, 128), F32), pltpu.VMEM((nc, 128, CHUNK), BF16)],
        compiler_params=_cp(("parallel", "parallel")), name="ssd",
    )(proj, proj, proj, lanes, conv_w, conv_b.reshape(1, -1), conv_w, conv_b.reshape(1, -1),
      conv_w, conv_b.reshape(1, -1), dtb, alog, dsk, jnp.asarray(sel))


ML_HP = 2


def _ml_gate_lanes(hh, d):
    li = (hh * 2 + d) * 2
    return li, li + 1


def _ml_gate_columns(nh):
    cols = []
    for hp in range(nh // ML_HP):
        blk = [None] * (4 * ML_HP)
        for hh in range(ML_HP):
            for d in range(2):
                for t, lane in enumerate(_ml_gate_lanes(hh, d)):
                    blk[lane] = (d * 2 + t) * nh + hp * ML_HP + hh
        cols.append(blk)
    return cols


def _mlstm_kernel(q_ref, k_ref, v_ref, g_ref, cwq_ref, cbq_ref, cwk_ref, cbk_ref, gb_ref, o_ref,
                  qc_s, kc_s, kct_s, *, nc, dh):
    q = CHUNK
    first_lat = TM // q

    def prep(i, _):
        t0 = pl.multiple_of(i * q, q)
        idx = pl.ds(t0, q)
        for src, cw, cb, dst, mul in ((q_ref, cwq_ref, cbq_ref, qc_s, 1.0), (k_ref, cwk_ref, cbk_ref, kc_s, dh ** -0.5)):
            dst[idx, :] = _silu(_conv_tile(src, i, nc, cw[...], cb[...], q)) * mul
        for hh in range(ML_HP):
            kct_s[i, hh * dh:(hh + 1) * dh, :] = kc_s[idx, hh * dh:(hh + 1) * dh].T.astype(BF16)
        return 0
    lax.fori_loop(0, nc, prep, 0)

    gb = gb_ref[0]
    lane = lax.broadcasted_iota(jnp.int32, (q, 128), 1)
    eye8 = (lax.broadcasted_iota(jnp.int32, (8, 128), 0) == lax.broadcasted_iota(jnp.int32, (8, 128), 1)).astype(F32)
    o_ref[...] = jnp.zeros_like(o_ref)
    heads = range(ML_HP)

    def direction(d, j, states):
        rev = d == 1
        causal = _tri(q, rev)
        last = 0 if rev else q - 1
        if rev:
            ci = jnp.where(j < first_lat, first_lat - 1 - j, nc - 1 - (j - first_lat))
        else:
            ci = j
        idx = pl.ds(pl.multiple_of(ci * q, q), q)
        g = g_ref[0, idx, :] + gb
        logf = jnp.minimum(g, 0.0) - jnp.log(1.0 + jnp.exp(-jnp.abs(g)))
        is_f = functools.reduce(jnp.logical_or, [lane == _ml_gate_lanes(hh, d)[1] for hh in heads])
        cum = _dot_sel(jnp.where(causal, 1.0, 0.0), jnp.where(is_f, logf, 0.0))
        qc = [qc_s[idx, hh * dh:(hh + 1) * dh] for hh in heads]
        kc = [kc_s[idx, hh * dh:(hh + 1) * dh] for hh in heads]
        vc = [v_ref[0, idx, hh * dh:(hh + 1) * dh] for hh in heads]
        kct = [kct_s[ci, hh * dh:(hh + 1) * dh, :] for hh in heads]
        qk = [_dot(qc[hh], kct[hh]) for hh in heads]
        inter_c = [_dot(qc[hh], states[hh][0]) for hh in heads]
        yield
        rows = _dot_sel(eye8, jnp.where(is_f, cum, g), _NT)
        yield
        w, w_s, g_in, m_t, keep, m_new, wv, upd = [], [], [], [], [], [], [], []
        for hh in heads:
            li, lf = _ml_gate_lanes(hh, d)
            m_prev = states[hh][2]
            i_col, i_row = g[:, li:li + 1], rows[li:li + 1, :]
            b_col, b_row = cum[:, lf:lf + 1], rows[lf:lf + 1, :]
            dlog = jnp.where(causal, b_col - b_row + i_row, NEG)
            inter = b_col + m_prev
            m_t.append(jnp.maximum(jnp.max(dlog, axis=-1, keepdims=True), inter))
            w.append(jnp.exp(dlog - m_t[hh]) * qk[hh])
            g_in.append(jnp.exp(inter - m_t[hh]))
            b_end = b_col[last:last + 1, :]
            g_s = b_end - b_col + i_col
            m_new.append(jnp.maximum(jnp.max(g_s, axis=0, keepdims=True), b_end + m_prev))
            w_s.append(jnp.exp(g_s - m_new[hh]))
            keep.append(jnp.exp(b_end + m_prev - m_new[hh]))
            wv.append(_dot(w[hh], vc[hh]))
            upd.append(_dot(kct[hh], w_s[hh] * vc[hh]))
        yield
        new_states = []
        for hh in heads:
            c_prev, n_prev, _ = states[hh]
            num = wv[hh] + g_in[hh] * inter_c[hh]
            den = (jnp.sum(w[hh], axis=-1, keepdims=True)
                   + g_in[hh] * jnp.sum(qc[hh] * n_prev, axis=-1, keepdims=True))
            o_ref[0, idx, hh * dh:(hh + 1) * dh] += num / jnp.maximum(jnp.abs(den), jnp.exp(-m_t[hh]))
            new_states.append((keep[hh] * c_prev + upd[hh],
                               keep[hh] * n_prev + jnp.sum(w_s[hh] * kc[hh], axis=0, keepdims=True), m_new[hh]))
        return tuple(new_states)

    def chunk(j, states):
        return tuple(_interleave([direction(d, j, states[d]) for d in (0, 1)]))
    init = (jnp.zeros((dh, dh), F32), jnp.zeros((1, dh), F32), jnp.full((1, 1), NEG, F32))
    lax.fori_loop(0, nc, chunk, tuple(tuple(init for _ in heads) for _ in (0, 1)))


def _mlstm(proj, lanes, conv_w, conv_b, gate_b, cols):
    bsz, L, _ = proj.shape
    nc = L // CHUNK
    cq, ck, cv, cg = cols
    nh = gate_b.shape[-1]
    dh = conv_w.shape[1] // (2 * nh)
    bw = ML_HP * dh
    gcols = np.asarray(_ml_gate_columns(nh))
    gbl = jnp.zeros((nh // ML_HP, 1, 128), F32).at[:, 0, :gcols.shape[1]].set(gate_b.reshape(-1)[gcols])
    return pl.pallas_call(
        functools.partial(_mlstm_kernel, nc=nc, dh=dh),
        grid=(bsz, nh // ML_HP),
        in_specs=[pl.BlockSpec((1, L, bw), lambda b, h: (b, 0, cq // bw + h)),
                  pl.BlockSpec((1, L, bw), lambda b, h: (b, 0, ck // bw + h)),
                  pl.BlockSpec((1, L, bw), lambda b, h: (b, 0, cv // bw + h)),
                  pl.BlockSpec((1, L, 128), lambda b, h: (b, 0, cg // 128 + h)),
                  pl.BlockSpec((CONV_W, bw), lambda b, h: (0, h)),
                  pl.BlockSpec((1, bw), lambda b, h: (0, h)),
                  pl.BlockSpec((CONV_W, bw), lambda b, h: (0, nh // ML_HP + h)),
                  pl.BlockSpec((1, bw), lambda b, h: (0, nh // ML_HP + h)),
                  pl.BlockSpec((1, 1, 128), lambda b, h: (h, 0, 0))],
        out_specs=pl.BlockSpec((1, L, bw), lambda b, h: (b, 0, h)),
        out_shape=jax.ShapeDtypeStruct((bsz, L, nh * dh), F32),
        scratch_shapes=[pltpu.VMEM((L, bw), F32), pltpu.VMEM((L, bw), F32), pltpu.VMEM((nc, bw, CHUNK), BF16)],
        compiler_params=_cp(("parallel", "parallel")), name="mlstm",
    )(proj, proj, proj, lanes, conv_w, conv_b.reshape(1, -1), conv_w, conv_b.reshape(1, -1), gbl)


def kernel(x, c, ctx, c_ctx, mod_w, mod_b, norm1_g, norm2_g, even_w_in, even_w_out, lru_conv_w, lru_conv_b, lru_wa, lru_ba, lru_wx, lru_bx, lru_lam, na_q_g, na_k_g, na_rpb, odd_w_in, odd_w_out, ssd_conv_w, ssd_conv_b, ssd_dt_bias, ssd_a_log, ssd_d, ssd_norm_g, ml_conv_w, ml_conv_b, ml_gate_b, ml_norm_g, moe_router_g, moe_router_e, moe_w1, moe_w3, moe_w2):
    bsz, S, d = x.shape
    lc = ctx.shape[1]
    assert lc == TM and S % TM == 0 and mod_w.shape[0] == 2
    nt = (lc + S) // TM
    L = lc + S

    cc = jnp.zeros((8, d), F32).at[0].set(c_ctx).at[1:1 + bsz].set(c)
    mod_all = _modulation(cc, mod_w, mod_b)

    def mod_for(l):
        m = mod_all[l].reshape(8, 6, d)
        return jnp.stack([jnp.broadcast_to(m[0], (bsz, 6, d)), m[1:1 + bsz]], axis=1)

    def router_w(l):
        w = jnp.zeros((d, 128), F32).at[:, :EXPERT_LANE0].set(moe_router_g[l]) \
            .at[:, EXPERT_LANE0:EXPERT_LANE0 + N_EXPERTS].set(moe_router_e[l])
        hi = w.astype(BF16)
        return jnp.concatenate([hi, (w - hi.astype(F32)).astype(BF16)], axis=1)

    xx = (ctx, x)

    mod0 = mod_for(0)
    (proj,) = _inproj(xx, mod0, norm1_g[0], even_w_in[0].astype(BF16))
    lw = lru_conv_w.shape[-1]
    r = _lru(proj, lru_conv_w[0], lru_conv_b[0], lru_wa[0], lru_ba[0], lru_wx[0], lru_bx[0], lru_lam[0])
    a = _na(proj, na_q_g[0], na_k_g[0], na_rpb[0], col0=2 * lw)
    x1, h2, gate = _outproj([(r, 0, lw), (proj, 1, lw), (a, 0, lw)], xx, even_w_out[0].astype(BF16), mod0,
                            norm2_g[0], router_w(0), even=True, tile0=0, ntiles=nt)
    ew = (moe_w1.astype(BF16), moe_w3.astype(BF16), moe_w2.astype(BF16))
    moe0 = _moe(h2.reshape(bsz * L, d), gate.reshape(bsz * L, 128), 0, *ew).reshape(bsz, L, d)

    mod1 = mod_for(1)
    sw = ssd_d.shape[-1] * 64
    xbc = ssd_conv_w.shape[-1]
    mw = ml_norm_g.shape[-1]
    w = odd_w_in[0]
    o = np.cumsum([0, sw, xbc, 2 * ssd_d.shape[-1], mw, mw, mw, mw])
    ndt = 2 * ssd_d.shape[-1]
    dt_blk = jnp.concatenate([w[:, o[2]:o[3]], jnp.zeros((d, 128 - ndt), F32)], axis=1)
    gate_blks = [jnp.concatenate([w[:, o[7] + np.asarray(blk)], jnp.zeros((d, 128 - len(blk)), F32)], axis=1)
                 for blk in _ml_gate_columns(ml_gate_b.shape[-1])]
    w_odd = jnp.concatenate([w[:, :o[2]], w[:, o[3]:o[7]], dt_blk] + gate_blks, axis=1).astype(BF16)
    cz, cxs = 0, sw
    cB, cC = cxs + sw, cxs + sw + (xbc - sw) // 2
    cq = sw + xbc
    ck, cv, co, csm = cq + mw, cq + 2 * mw, cq + 3 * mw, cq + 4 * mw
    x0, proj1, lanes1 = _inproj(x1, mod1, norm1_g[1], w_odd, prev=(moe0, mod0), n_f32=w_odd.shape[1] - csm)
    ys = _ssd(proj1, lanes1, ssd_conv_w[0], ssd_conv_b[0], ssd_dt_bias[0], ssd_a_log[0], ssd_d[0], (cxs, cB, cC, 0))
    hm = _mlstm(proj1, lanes1, ml_conv_w[0], ml_conv_b[0], ml_gate_b[0], (cq, ck, cv, 128))
    x2, h2b, gate1 = _outproj([(ys, 0, sw), (proj1, cz // sw, sw), (hm, 0, mw), (proj1, co // mw, mw),
                               (ssd_norm_g[0].reshape(1, sw), 0, sw), (ml_norm_g[0].reshape(1, mw), 0, mw)],
                              x0, odd_w_out[0].astype(BF16), mod1, norm2_g[1], router_w(1),
                              even=False, tile0=1, ntiles=nt - 1)
    out = _moe(h2b.reshape(bsz * S, d), gate1.reshape(bsz * S, 128), 1, *ew, resid=(x2.reshape(bsz * S, d), mod1))
    return out.reshape(bsz, S, d)
```

```python
import functools
import math

import jax
import jax.numpy as jnp
import numpy as np
from jax import lax
from jax.experimental import pallas as pl
from jax.experimental.pallas import tpu as pltpu

F32 = jnp.float32
BF16 = jnp.bfloat16
HI = lax.Precision.HIGHEST

EPS = 1e-6
NEG = -1e30
GRID_W = 64
CONV_W = 4
LRU_C = 8.0
TM = 256
CHUNK = 256
NA_RQ = 4
NA_RK = 12
N_EXPERTS = 16
EXPERT_LANE0 = 4
VMEM_LIMIT = 56 * 1024 * 1024


def _cp(sem, vmem=VMEM_LIMIT):
    return pltpu.CompilerParams(dimension_semantics=sem, vmem_limit_bytes=vmem)


def _sigmoid(x):
    return jax.nn.sigmoid(x)


def _silu(x):
    return x * jax.nn.sigmoid(x)


def _softplus(x):
    return jnp.maximum(x, 0.0) + jnp.log(1.0 + jnp.exp(-jnp.abs(x)))


def _gelu_tanh(x):
    return 0.5 * x * (1.0 + jnp.tanh(math.sqrt(2.0 / math.pi) * (x + 0.044715 * (x * x * x))))


def _rms(x, axis=-1):
    return x * lax.rsqrt(jnp.mean(x * x, axis=axis, keepdims=True) + EPS)


def _dot(a, b):
    return jnp.dot(a.astype(BF16), b.astype(BF16), preferred_element_type=F32)


def _dot_hi(a, b):
    return jnp.dot(a, b, precision=HI, preferred_element_type=F32)


def _split3(x):
    x1 = x.astype(BF16)
    r = x - x1.astype(F32)
    x2 = r.astype(BF16)
    x3 = (r - x2.astype(F32)).astype(BF16)
    return x1, x2, x3


def _dot_sel(sel, x, dims=(((1,), (0,)), ((), ()))):
    s = sel.astype(BF16)
    x1, x2, x3 = _split3(x)
    d = lambda xi: lax.dot_general(s, xi, dims, preferred_element_type=F32)
    return (d(x3) + d(x2)) + d(x1)


def _dot_sel_r(x, sel):
    s = sel.astype(BF16)
    x1, x2, x3 = _split3(x)
    d = lambda xi: jnp.dot(xi, s, preferred_element_type=F32)
    return (d(x3) + d(x2)) + d(x1)


_NT = (((1,), (1,)), ((), ()))


def _interleave(gens):
    results = [None] * len(gens)
    live = list(range(len(gens)))
    while live:
        for i in list(live):
            try:
                next(gens[i])
            except StopIteration as stop:
                results[i] = stop.value
                live.remove(i)
    return results


def _dot_nt(a, b):
    return lax.dot_general(a.astype(BF16), b.astype(BF16), (((1,), (1,)), ((), ())),
                           preferred_element_type=F32)


def _mod_kernel(c_ref, w_ref, b_ref, o_ref):
    c = c_ref[...]
    o_ref[0] = _dot_hi(_silu(c), w_ref[0]) + b_ref[0]


def _modulation(cc, mod_w, mod_b):
    depth, d, n = mod_w.shape
    tn = 1536
    return pl.pallas_call(
        _mod_kernel,
        grid=(depth, n // tn),
        in_specs=[pl.BlockSpec((8, d), lambda l, j: (0, 0)),
                  pl.BlockSpec((1, d, tn), lambda l, j: (l, 0, j)),
                  pl.BlockSpec((1, 1, tn), lambda l, j: (l, 0, j))],
        out_specs=pl.BlockSpec((1, 8, tn), lambda l, j: (l, 0, j)),
        out_shape=jax.ShapeDtypeStruct((depth, 8, n), F32),
        compiler_params=_cp(("arbitrary", "arbitrary")),
        name="adaln_mod",
    )(cc, mod_w, mod_b.reshape(depth, 1, n))


def _inproj_kernel(*refs, fuse_prev, n_f32):
    refs = list(refs)
    s_ref = refs.pop() if n_f32 else None
    if fuse_prev:
        x_ref, mo_ref, pmod_ref, mod_ref, g_ref, w_ref, xo_ref, p_ref = refs
        x = x_ref[0] + pmod_ref[0, 0][5:6, :] * mo_ref[0]
        xo_ref[0] = x
    else:
        c_ref, x_ref, mod_ref, g_ref, w_ref, p_ref = refs
        x = _pick_segment(c_ref, x_ref)
    mod = mod_ref[0, 0]
    h = _rms(x) * g_ref[...] * (1.0 + mod[1:2, :]) + mod[0:1, :]
    p = _dot(h, w_ref[...])
    n_main = p.shape[-1] - n_f32
    p_ref[0] = p[:, :n_main].astype(BF16)
    if n_f32:
        s_ref[0] = p[:, n_main:]


def _seg_map(b, i):
    return (b, jnp.minimum(i, 1), 0, 0)


def _segment_specs(d):
    return [pl.BlockSpec((1, TM, d), lambda b, i: (b, 0, 0)),
            pl.BlockSpec((1, TM, d), lambda b, i: (b, jnp.maximum(i - 1, 0), 0))]


def _pick_segment(c_ref, x_ref):
    return jnp.where(pl.program_id(1) == 0, c_ref[0], x_ref[0])


def _inproj(x, mod, g, w, prev=None, n_f32=0):
    if prev is None:
        ctx, lat = x
        bsz, S, d = lat.shape
        L = TM + S
    else:
        bsz, L, d = x.shape
    n = w.shape[1]
    nt = L // TM
    tok = pl.BlockSpec((1, TM, d), lambda b, i: (b, i, 0))
    modspec = pl.BlockSpec((1, 1, 6, d), _seg_map)
    tail = [modspec, pl.BlockSpec((1, d), lambda b, i: (0, 0)), pl.BlockSpec((d, n), lambda b, i: (0, 0))]
    pspecs = [pl.BlockSpec((1, TM, n - n_f32), lambda b, i: (b, i, 0))]
    pshapes = [jax.ShapeDtypeStruct((bsz, L, n - n_f32), BF16)]
    if n_f32:
        pspecs.append(pl.BlockSpec((1, TM, n_f32), lambda b, i: (b, i, 0)))
        pshapes.append(jax.ShapeDtypeStruct((bsz, L, n_f32), F32))
    if prev is None:
        return pl.pallas_call(
            functools.partial(_inproj_kernel, fuse_prev=False, n_f32=n_f32),
            grid=(bsz, nt), in_specs=_segment_specs(d) + tail, out_specs=pspecs, out_shape=pshapes,
            compiler_params=_cp(("parallel", "arbitrary")), name="inproj",
        )(ctx, lat, mod, g.reshape(1, d), w)
    moe_out, pmod = prev
    return pl.pallas_call(
        functools.partial(_inproj_kernel, fuse_prev=True, n_f32=n_f32),
        grid=(bsz, nt), in_specs=[tok, tok, modspec] + tail,
        out_specs=[tok] + pspecs, out_shape=[jax.ShapeDtypeStruct(x.shape, F32)] + pshapes,
        compiler_params=_cp(("parallel", "parallel")), name="inproj_res",
    )(x, moe_out, pmod, mod, g.reshape(1, d), w)


def _conv_tile(ref, i, nt, cw, cb, width=TM):
    L = nt * width
    t0 = pl.multiple_of(i * width, width)
    cur = ref[0, pl.ds(t0, width), :].astype(F32)
    prev = ref[0, pl.ds(pl.multiple_of(jnp.maximum(t0 - 16, 0), 16), 16), :].astype(F32)[8:16]
    nxt = ref[0, pl.ds(pl.multiple_of(jnp.minimum(t0 + width, L - 16), 16), 16), :].astype(F32)[0:8]
    return _conv_vals(cur, prev, nxt, i, nt, cw, cb, width, first_lat=TM // width)


def _conv_vals(cur, prev, nxt, i, nt, cw, cb, width, first_lat):
    prev = jnp.where((i != 0) & (i != first_lat), prev, 0.0)
    nxt = jnp.where((i != first_lat - 1) & (i != nt - 1), nxt, 0.0)
    cat = jnp.concatenate([prev, cur, nxt], axis=0)
    return (cw[0:1] * cat[6:6 + width] + cw[1:2] * cat[7:7 + width] + cw[2:3] * cur
            + cw[3:4] * cat[9:9 + width] + cb)


def _lru_kernel(ux_ref, cw_ref, cb_ref, gw_ref, gb_ref, lam_ref, o_ref, *, nt):
    cw = cw_ref[...]
    cb = cb_ref[...]
    lam = lam_ref[0]
    sp = _softplus(-lam)
    row = lax.broadcasted_iota(jnp.int32, (TM, 128), 0) & 7

    def gates(i, d):
        xl = _conv_tile(ux_ref, i, nt, cw, cb)
        g = _dot(xl, gw_ref[0, d]) + gb_ref[0, d]
        r = _sigmoid(g[:, :128])
        ig = _sigmoid(g[:, 128:])
        log_a = -LRU_C * r * sp[d:d + 1]
        a = jnp.exp(log_a)
        u = jnp.sqrt(1.0 - a * a) * (ig * xl)
        return a, u

    def scan_tile(i, d, carry, accumulate):
        a, u = gates(i, d)
        rev = d == 1
        for k in (1, 2, 4):
            sh = 8 - k if rev else k
            ok = (row < 8 - k) if rev else (row >= k)
            ash = pltpu.roll(a.reshape(TM // 8, 8, 128), sh, 1).reshape(TM, 128)
            ush = pltpu.roll(u.reshape(TM // 8, 8, 128), sh, 1).reshape(TM, 128)
            u = jnp.where(ok, u + a * ush, u)
            a = jnp.where(ok, a * ash, a)
        t0 = i * TM
        groups = range(TM // 8)
        for s in (reversed(groups) if rev else groups):
            h = u[s * 8:(s + 1) * 8] + a[s * 8:(s + 1) * 8] * carry
            carry = h[0:1] if rev else h[7:8]
            idx = pl.ds(pl.multiple_of(t0 + s * 8, 8), 8)
            if accumulate:
                o_ref[0, idx, :] += h
            else:
                o_ref[0, idx, :] = h
        return carry

    zero = jnp.zeros((1, 128), F32)
    lax.fori_loop(0, nt, lambda i, c: scan_tile(i, 0, c, False), zero)
    lax.fori_loop(0, nt, lambda j, c: scan_tile(jnp.where(j == 0, 0, nt - j), 1, c, True), zero)


def _lru(proj, conv_w, conv_b, wa, ba, wx, bx, lam):
    bsz, L, _ = proj.shape
    nt = L // TM
    width = conv_w.shape[1]
    ng = width // 128

    def blockdiag(w):
        w = w.reshape(2, ng, 2, 64, 64)
        z = jnp.zeros_like(w[:, :, 0])
        top = jnp.concatenate([w[:, :, 0], z], axis=-1)
        bot = jnp.concatenate([z, w[:, :, 1]], axis=-1)
        return jnp.concatenate([top, bot], axis=-2)
    gw = jnp.concatenate([blockdiag(wa), blockdiag(wx)], axis=-1).transpose(1, 0, 2, 3).astype(BF16)
    gb = jnp.concatenate([ba.reshape(2, ng, 1, 128), bx.reshape(2, ng, 1, 128)], axis=-1).transpose(1, 0, 2, 3)
    lam_g = lam.reshape(2, ng, 128).transpose(1, 0, 2)
    return pl.pallas_call(
        functools.partial(_lru_kernel, nt=nt),
        grid=(bsz, ng),
        in_specs=[pl.BlockSpec((1, L, 128), lambda b, c: (b, 0, c)),
                  pl.BlockSpec((CONV_W, 128), lambda b, c: (0, c)),
                  pl.BlockSpec((1, 128), lambda b, c: (0, c)),
                  pl.BlockSpec((1, 2, 128, 256), lambda b, c: (c, 0, 0, 0)),
                  pl.BlockSpec((1, 2, 1, 256), lambda b, c: (c, 0, 0, 0)),
                  pl.BlockSpec((1, 2, 128), lambda b, c: (c, 0, 0))],
        out_specs=pl.BlockSpec((1, L, 128), lambda b, c: (b, 0, c)),
        out_shape=jax.ShapeDtypeStruct((bsz, L, width), F32),
        compiler_params=_cp(("parallel", "parallel")), name="rglru",
    )(proj, conv_w, conv_b.reshape(1, width), gw, gb, lam_g)


def _na_bias_table(rpb, rows):
    nh = rpb.shape[0]
    win_c = (rpb.shape[2] + 1) // 2
    qc = np.arange(GRID_W)[:, None]
    kc = np.arange(GRID_W)[None, :]
    cstart = np.clip(qc - win_c // 2, 0, GRID_W - win_c)
    col_ok = (kc >= cstart) & (kc < cstart + win_c)
    dcol = np.clip(kc - qc + (win_c - 1), 0, 2 * win_c - 2)
    oc = (np.arange(2 * win_c - 1)[:, None, None] == dcol[None]).astype(np.float32)
    ct = jnp.where(jnp.asarray(col_ok), jnp.einsum('hrc,cqk->hrqk', rpb.astype(F32), jnp.asarray(oc), precision=HI),
                   NEG)
    return pl.pallas_call(
        functools.partial(_na_bias_kernel, rows=rows),
        grid=(nh,),
        in_specs=[pl.BlockSpec((1,) + ct.shape[1:], lambda h: (h, 0, 0, 0))],
        out_specs=pl.BlockSpec((3, 1, TM, NA_RK * GRID_W), lambda h: (0, h, 0, 0)),
        out_shape=jax.ShapeDtypeStruct((3, nh, TM, NA_RK * GRID_W), F32),
        compiler_params=_cp(("parallel",)), name="nbr_bias",
    )(ct)


def _na_bias_kernel(ct_ref, o_ref, *, rows):
    win_r = (ct_ref.shape[1] + 1) // 2
    neg = jnp.full((GRID_W, GRID_W), NEG, F32)
    for p, (r0, w0) in enumerate(((0, 0), (2 * NA_RQ, NA_RQ), (rows - NA_RQ, rows - NA_RK))):
        for a in range(NA_RQ):
            r = r0 + a
            rstart = min(max(r - win_r // 2, 0), rows - win_r)
            tiles = [ct_ref[0, w0 + b - r + win_r - 1] if rstart <= w0 + b < rstart + win_r else neg
                     for b in range(NA_RK)]
            for bp in range(NA_RK // 2):
                o_ref[p, 0, a * GRID_W:(a + 1) * GRID_W, bp * 2 * GRID_W:(bp + 1) * 2 * GRID_W] = (
                    jnp.concatenate(tiles[2 * bp:2 * bp + 2], axis=1))


def _pair_rms(x, lo):
    x2 = x * x
    s0 = jnp.sum(jnp.where(lo, x2, 0.0), axis=-1, keepdims=True)
    s1 = jnp.sum(jnp.where(lo, 0.0, x2), axis=-1, keepdims=True)
    return x * lax.rsqrt(jnp.where(lo, s0, s1) * (2.0 / x.shape[-1]) + EPS)


def _na_kernel(q_ref, k_ref, v_ref, bias0_ref, bias1_ref, qg_ref, kg_ref, o_ref, kn_s, vb_s, *, rows, hd):
    j = pl.program_id(2)
    L = k_ref.shape[1]
    nb = L // TM
    nkeys = NA_RK * GRID_W
    lo = lax.broadcasted_iota(jnp.int32, (1, 2 * hd), 1) < hd

    @pl.when(j == 0)
    def _prep():
        def body(t, _):
            idx = pl.ds(pl.multiple_of(t * TM, TM), TM)
            kn_s[idx, :] = (_pair_rms(k_ref[0, idx, :].astype(F32), lo) * kg_ref[...]).astype(BF16)
            vb_s[idx, :] = v_ref[0, idx, :].astype(BF16)
            return 0
        lax.fori_loop(0, nb, body, 0)

    kctx = kn_s[0:TM, :]
    vctx = vb_s[0:TM, :]
    qscale = qg_ref[...] * hd ** -0.5

    def head(q, bias, kwin, vwin):
        s_c = _dot_nt(q, kctx)
        if kwin is not None:
            s_w = _dot_nt(q, kwin) + bias
        yield
        m = jnp.max(s_c, axis=-1, keepdims=True)
        if kwin is not None:
            m = jnp.maximum(m, jnp.max(s_w, axis=-1, keepdims=True))
            p_w = jnp.exp(s_w - m)
        p_c = jnp.exp(s_c - m)
        den = jnp.sum(p_c, axis=-1, keepdims=True)
        num = _dot(p_c, vctx)
        if kwin is not None:
            den = den + jnp.sum(p_w, axis=-1, keepdims=True)
            num = num + _dot(p_w, vwin)
        yield
        return num / den

    def tile_chains(slot, bias_ref, tile):
        qn = _pair_rms(q_ref[0, slot * TM:(slot + 1) * TM, :].astype(F32), lo) * qscale
        q_h = [jnp.where(lo, qn, 0.0).astype(BF16), jnp.where(lo, 0.0, qn).astype(BF16)]
        if tile is None:
            return [head(q_h[hh], None, None, None) for hh in range(2)]
        w0 = jnp.clip((tile - 1) * NA_RQ - NA_RQ, 0, rows - NA_RK)
        start = pl.multiple_of(TM + w0 * GRID_W, GRID_W)
        kwin = kn_s[pl.ds(start, nkeys), :]
        vwin = vb_s[pl.ds(start, nkeys), :]
        return [head(q_h[hh], bias_ref[0, hh], kwin, vwin) for hh in range(2)]

    def run(chains):
        outs = _interleave(chains)
        for slot in range(len(outs) // 2):
            o_ref[0, slot * TM:(slot + 1) * TM, :] = jnp.where(lo, outs[2 * slot], outs[2 * slot + 1]).astype(BF16)

    last = (nb - 1) // 2

    @pl.when(j == 0)
    def _():
        run(tile_chains(0, None, None) + tile_chains(1, bias1_ref, 1))

    @pl.when((j > 0) & (j < last))
    def _():
        run(tile_chains(0, bias0_ref, 2 * j) + tile_chains(1, bias1_ref, 2 * j + 1))

    @pl.when(j == last)
    def _():
        run(tile_chains(0, bias0_ref, 2 * j))


def _na(proj, q_g, k_g, rpb, col0):
    bsz, L, _ = proj.shape
    nh = rpb.shape[0]
    hd = q_g.shape[0]
    width = nh * hd
    rows = (L - TM) // GRID_W
    nb = L // TM
    bias = _na_bias_table(rpb, rows)
    qb, kb, vb = col0 // 128, (col0 + width) // 128, (col0 + 2 * width) // 128

    assert nb % 2 == 1

    def pat(t):
        return jnp.where(t <= 1, 0, jnp.where(t >= nb - 1, 2, 1))
    bias_spec = lambda slot: pl.BlockSpec((1, 2, TM, NA_RK * GRID_W), lambda b, h, j: (pat(2 * j + slot), h, 0, 0))
    return pl.pallas_call(
        functools.partial(_na_kernel, rows=rows, hd=hd),
        grid=(bsz, width // 128, (nb + 1) // 2),
        in_specs=[pl.BlockSpec((1, 2 * TM, 128), lambda b, h, j: (b, j, qb + h)),
                  pl.BlockSpec((1, L, 128), lambda b, h, j: (b, 0, kb + h)),
                  pl.BlockSpec((1, L, 128), lambda b, h, j: (b, 0, vb + h)),
                  bias_spec(0), bias_spec(1),
                  pl.BlockSpec((1, 2 * hd), lambda b, h, j: (0, 0)),
                  pl.BlockSpec((1, 2 * hd), lambda b, h, j: (0, 0))],
        out_specs=pl.BlockSpec((1, 2 * TM, 128), lambda b, h, j: (b, j, h)),
        out_shape=jax.ShapeDtypeStruct((bsz, L, width), BF16),
        scratch_shapes=[pltpu.VMEM((L, 128), BF16), pltpu.VMEM((L, 128), BF16)],
        compiler_params=_cp(("parallel", "parallel", "arbitrary")), name="nbr_attn",
    )(proj, proj, proj, bias, bias, jnp.tile(q_g, 2).reshape(1, 2 * hd), jnp.tile(k_g, 2).reshape(1, 2 * hd))


def _route(lg):
    lane = lax.broadcasted_iota(jnp.int32, lg.shape, 1)
    lane_f = lane.astype(F32)
    is_g = lane < EXPERT_LANE0
    gl = jnp.where(is_g, lg, NEG)
    gmax = jnp.max(gl, axis=-1, keepdims=True)
    gsel = jnp.min(jnp.where(is_g & (gl == gmax), lane_f, 1e9), axis=-1, keepdims=True)
    g_w = 1.0 / jnp.sum(jnp.where(is_g, jnp.exp(gl - gmax), 0.0), axis=-1, keepdims=True)
    grp = ((lane - EXPERT_LANE0) >> 2).astype(F32)
    in_g = (lane >= EXPERT_LANE0) & (lane < EXPERT_LANE0 + N_EXPERTS) & (grp == gsel)
    el = jnp.where(in_g, lg, NEG)
    v1 = jnp.max(el, axis=-1, keepdims=True)
    i1 = jnp.min(jnp.where(in_g & (el == v1), lane_f, 1e9), axis=-1, keepdims=True)
    el2 = jnp.where(lane_f == i1, NEG, el)
    v2 = jnp.max(el2, axis=-1, keepdims=True)
    i2 = jnp.min(jnp.where(in_g & (lane_f != i1) & (el2 == v2), lane_f, 1e9), axis=-1, keepdims=True)
    t = jnp.exp(v2 - v1)
    w1 = g_w / (1.0 + t)
    w2 = g_w * t / (1.0 + t)
    return (jnp.where(lane_f == i1, w1, 0.0) + jnp.where(lane_f == i2, w2, 0.0)
            + jnp.where(lane_f == gsel, 1.0, 0.0))


def _outproj_kernel(*refs, even):
    if even:
        (r_ref, ug_ref, a_ref, c_ref, x_ref, w_ref, mod_ref, g2_ref, rw_ref, x1_ref, h2_ref, gate_ref) = refs
        is_ctx = pl.program_id(1) == 0
        x_rows = lambda rs: jnp.where(is_ctx, c_ref[0, rs, :], x_ref[0, rs, :])
    else:
        (ys_ref, z_ref, hm_ref, mo_ref, sg_ref, mg_ref, x_ref, w_ref, mod_ref, g2_ref, rw_ref,
         x1_ref, h2_ref, gate_ref) = refs
        x_rows = lambda rs: x_ref[0, rs, :]
    mod = mod_ref[0, 0]

    def part(rs):
        if even:
            y_in = jnp.concatenate([(r_ref[0, rs, :] * _gelu_tanh(ug_ref[0, rs, :].astype(F32))).astype(BF16),
                                    a_ref[0, rs, :]], axis=-1)
        else:
            ys = ys_ref[0, rs, :] * _silu(z_ref[0, rs, :].astype(F32))
            sg = sg_ref[...]
            mg = mg_ref[...]
            hm = hm_ref[0, rs, :]
            sig_o = _sigmoid(mo_ref[0, rs, :].astype(F32))
            gw = ys.shape[-1] // 2
            parts = [_rms(ys[:, g * gw:(g + 1) * gw]) * sg[:, g * gw:(g + 1) * gw] for g in range(2)]
            hw = 128
            parts += [_rms(hm[:, h * hw:(h + 1) * hw]) * mg[:, h * hw:(h + 1) * hw] * sig_o[:, h * hw:(h + 1) * hw]
                      for h in range(hm.shape[-1] // hw)]
            y_in = jnp.concatenate(parts, axis=-1)
        y = _dot(y_in, w_ref[...])
        yield
        x1 = x_rows(rs) + mod[2:3, :] * y
        x1_ref[0, rs, :] = x1
        h2 = _rms(x1) * g2_ref[...] * (1.0 + mod[4:5, :]) + mod[3:4, :]
        h2_ref[0, rs, :] = h2.astype(BF16)
        hi = h2.astype(BF16)
        lo = (h2 - hi.astype(F32)).astype(BF16)
        lg2 = jnp.dot(hi, rw_ref[...], preferred_element_type=F32)
        lg1 = jnp.dot(lo, rw_ref[:, 0:128], preferred_element_type=F32)
        yield
        gate_ref[0, rs, :] = _route(lg2[:, 0:128] + lg2[:, 128:256] + lg1)

    nparts = 2
    rows = TM // nparts
    _interleave([part(pl.ds(p * rows, rows)) for p in range(nparts)])


def _outproj(mix_inputs, x, w, mod, g2, rw, even, tile0, ntiles):
    xs = list(x) if even else [x]
    bsz, _, d = xs[-1].shape
    specs, args = [], []
    for arr, cb, wdt in mix_inputs:
        if arr.ndim == 3:
            specs.append(pl.BlockSpec((1, TM, wdt), lambda b, i, cb=cb: (b, i + tile0, cb)))
        else:
            specs.append(pl.BlockSpec((1, wdt), lambda b, i: (0, 0)))
        args.append(arr)
    specs += _segment_specs(d) if even else [pl.BlockSpec((1, TM, d), lambda b, i: (b, i + tile0, 0))]
    specs += [pl.BlockSpec(w.shape, lambda b, i: (0, 0)),
              pl.BlockSpec((1, 1, 6, d), lambda b, i: (b, jnp.minimum(i + tile0, 1), 0, 0)),
              pl.BlockSpec((1, d), lambda b, i: (0, 0)),
              pl.BlockSpec(rw.shape, lambda b, i: (0, 0))]
    args += xs + [w, mod, g2.reshape(1, d), rw]
    lo = ntiles * TM
    return pl.pallas_call(
        functools.partial(_outproj_kernel, even=even),
        grid=(bsz, ntiles), in_specs=specs,
        out_specs=[pl.BlockSpec((1, TM, d), lambda b, i: (b, i, 0)),
                   pl.BlockSpec((1, TM, d), lambda b, i: (b, i, 0)),
                   pl.BlockSpec((1, TM, 128), lambda b, i: (b, i, 0))],
        out_shape=[jax.ShapeDtypeStruct((bsz, lo, d), F32), jax.ShapeDtypeStruct((bsz, lo, d), BF16),
                   jax.ShapeDtypeStruct((bsz, lo, 128), F32)],
        compiler_params=_cp(("parallel", "parallel")), name="outproj_even" if even else "outproj_odd",
    )(*args)


MOE_CH = 128
GROUP_SIZE = 4
MOE_VMEM_LIMIT = 62 * 1024 * 1024
N_GROUPS = N_EXPERTS // GROUP_SIZE


def _moe_kernel(*refs, residual):
    if residual:
        x_ref, g_ref, w1_ref, w3_ref, w2_ref, x1_ref, mod_ref, o_ref, xs_s, ys_s, gs_s, pt_s, plan_s = refs
    else:
        x_ref, g_ref, w1_ref, w3_ref, w2_ref, o_ref, xs_s, ys_s, gs_s, pt_s, plan_s = refs
    grp = pl.program_id(1)
    tb = x_ref.shape[0]
    nch = xs_s.shape[0] // MOE_CH

    @pl.when(grp == 0)
    def _plan():
        g = g_ref[...]
        lane = lax.broadcasted_iota(jnp.int32, g.shape, 1)
        oh = jnp.where(lane < N_GROUPS, g, 0.0)
        earlier = (lax.broadcasted_iota(jnp.int32, (tb, tb), 0) > lax.broadcasted_iota(jnp.int32, (tb, tb), 1))
        rank = jnp.dot(jnp.where(earlier, 1.0, 0.0).astype(BF16), oh.astype(BF16),
                       preferred_element_type=F32)
        cnt = jnp.sum(oh, axis=0, keepdims=True)
        lane1 = lax.broadcasted_iota(jnp.int32, (1, 128), 1)
        off = jnp.int32(0)
        offv = jnp.zeros((1, 128), F32)
        for gi in range(N_GROUPS):
            n = jnp.sum(jnp.where(lane1 == gi, cnt, 0.0)).astype(jnp.int32)
            nchunks = (n + (MOE_CH - 1)) // MOE_CH
            plan_s[gi] = off // MOE_CH
            plan_s[N_GROUPS + gi] = nchunks
            offv = offv + jnp.where(lane1 == gi, off.astype(F32), 0.0)
            off = off + nchunks * MOE_CH
        pos_col = jnp.sum(oh * (rank + offv), axis=1, keepdims=True)
        posb = jnp.broadcast_to(pos_col, (tb, 128))
        pos_row = jnp.concatenate([posb[i * 128:(i + 1) * 128, :].T[0:1, :] for i in range(tb // 128)], axis=1)
        x = x_ref[...]
        g_hi = g.astype(BF16)
        g_lo = (g - g_hi.astype(F32)).astype(BF16)
        lane_c = lax.broadcasted_iota(jnp.int32, (tb, MOE_CH), 1).astype(F32)
        row_c = lax.broadcasted_iota(jnp.int32, (MOE_CH, tb), 0).astype(F32)
        used = off // MOE_CH

        def dispatch(c):
            sl = slice(c * MOE_CH, (c + 1) * MOE_CH)
            p = jnp.where(row_c + float(c * MOE_CH) == pos_row, 1.0, 0.0).astype(BF16)
            xs_s[sl, :] = jnp.dot(p, x, preferred_element_type=F32).astype(BF16)
            gs_s[sl, :] = (jnp.dot(p, g_hi, preferred_element_type=F32)
                           + jnp.dot(p, g_lo, preferred_element_type=F32))
        for c in range(nch):
            sl = slice(c * MOE_CH, (c + 1) * MOE_CH)
            pt_s[:, sl] = jnp.where(pos_col == lane_c + float(c * MOE_CH), 1.0, 0.0).astype(BF16)
            if c < tb // MOE_CH:
                dispatch(c)
            else:
                pl.when(c < used)(functools.partial(dispatch, c))
        ys_s[...] = jnp.zeros_like(ys_s)

    c0 = plan_s[grp]
    nchunks = plan_s[N_GROUPS + grp]

    def ffn(chunk0, nrows):
        rows = pl.ds(pl.multiple_of(chunk0 * MOE_CH, MOE_CH), nrows)
        xs = xs_s[rows, :]
        gs = gs_s[rows, :]
        lane_g = lax.broadcasted_iota(jnp.int32, (nrows, 128), 1)
        acc = None
        for k in range(GROUP_SIZE):
            a = jnp.dot(xs, w1_ref[0, k], preferred_element_type=F32)
            b = jnp.dot(xs, w3_ref[0, k], preferred_element_type=F32)
            y = _dot(_silu(a) * b, w2_ref[0, k])
            ge = jnp.sum(jnp.where(lane_g == grp * GROUP_SIZE + k + EXPERT_LANE0, gs, 0.0), axis=-1, keepdims=True)
            acc = ge * y if acc is None else acc + ge * y
        ys_s[rows, :] = acc.astype(BF16)

    wide = 4

    def full(p, _):
        ffn(c0 + wide * p, wide * MOE_CH)
        return 0
    lax.fori_loop(0, nchunks // wide, full, 0)
    rem = nchunks % wide
    for r in range(1, wide):
        pl.when(rem == r)(functools.partial(ffn, c0 + nchunks - r, r * MOE_CH))

    @pl.when(grp == N_GROUPS - 1)
    def _combine():
        out = jnp.dot(pt_s[...], ys_s[...], preferred_element_type=F32)
        if residual:
            out = x1_ref[...] + mod_ref[0, 0][5:6, :] * out
        o_ref[...] = out


def _moe(h2, gate, layer, w1, w3, w2, resid=None):
    t, d = h2.shape
    _, ne, _, ff = w1.shape
    tb = math.gcd(t, 1024)
    npad = tb + N_GROUPS * MOE_CH
    assert ne == N_EXPERTS
    extra_specs, extra_args = [], []
    if resid is not None:
        x1, mod = resid
        per_batch = t // mod.shape[0] // tb
        extra_specs = [pl.BlockSpec((tb, d), lambda i, e: (i, 0)),
                       pl.BlockSpec((1, 1, 6, d), lambda i, e: (i // per_batch, 1, 0, 0))]
        extra_args = [x1, mod]
    return pl.pallas_call(
        functools.partial(_moe_kernel, residual=resid is not None),
        grid=(t // tb, N_GROUPS),
        in_specs=[pl.BlockSpec((tb, d), lambda i, e: (i, 0)),
                  pl.BlockSpec((tb, 128), lambda i, e: (i, 0)),
                  pl.BlockSpec((1, GROUP_SIZE, d, ff), lambda i, e: (layer, e, 0, 0)),
                  pl.BlockSpec((1, GROUP_SIZE, d, ff), lambda i, e: (layer, e, 0, 0)),
                  pl.BlockSpec((1, GROUP_SIZE, ff, d), lambda i, e: (layer, e, 0, 0))] + extra_specs,
        out_specs=pl.BlockSpec((tb, d), lambda i, e: (i, 0)),
        out_shape=jax.ShapeDtypeStruct((t, d), F32),
        scratch_shapes=[pltpu.VMEM((npad, d), BF16), pltpu.VMEM((npad, d), BF16), pltpu.VMEM((npad, 128), F32),
                        pltpu.VMEM((tb, npad), BF16), pltpu.SMEM((2 * N_GROUPS,), jnp.int32)],
        compiler_params=_cp(("parallel", "arbitrary"), MOE_VMEM_LIMIT), name="moe",
    )(h2, gate, w1, w3, w2, *extra_args)


def _tri(q, rev):
    r = lax.broadcasted_iota(jnp.int32, (q, q), 0)
    c = lax.broadcasted_iota(jnp.int32, (q, q), 1)
    return (c >= r) if rev else (c <= r)


def _ssd_kernel(xs_ref, b_ref, c_ref, dt_ref, cwx_ref, cbx_ref, cwb_ref, cbb_ref, cwc_ref, cbc_ref,
                dtb_ref, alog_ref, dsk_ref, sel_ref, o_ref, xc_s, bc_s, cc_s, dt_s, bct_s, *, nc):
    q = CHUNK
    first_lat = TM // q

    dtb = dtb_ref[...]

    def prep(i, _):
        t0 = pl.multiple_of(i * q, q)
        idx = pl.ds(t0, q)
        for src, cw, cb, dst in ((xs_ref, cwx_ref, cbx_ref, xc_s), (b_ref, cwb_ref, cbb_ref, bc_s),
                                 (c_ref, cwc_ref, cbc_ref, cc_s)):
            dst[idx, :] = _silu(_conv_tile(src, i, nc, cw[...], cb[...], q))
        dt_s[idx, :] = _softplus(dt_ref[0, idx, :] + dtb)
        o_ref[0, idx, :] = dsk_ref[...] * xc_s[idx, :]
        bct_s[i] = bc_s[idx, :].T.astype(BF16)
        return 0
    lax.fori_loop(0, nc, prep, 0)

    a_lane = jnp.broadcast_to(-jnp.exp(alog_ref[...]), (8, 128))
    a_e = [_dot_hi(a_lane, sel_ref[d, 0])[0:1, :] for d in (0, 1)]
    lane256 = lax.broadcasted_iota(jnp.int32, (q, 256), 1) // 64
    r8 = lax.broadcasted_iota(jnp.int32, (8, 256), 0)
    l8 = lax.broadcasted_iota(jnp.int32, (8, 256), 1)
    head_rows = (l8 == r8 * 64).astype(F32)

    def chain(d, j, h):
        rev = d == 1
        causal = _tri(q, rev)
        tri = causal.astype(F32)
        sel = sel_ref[d, 0]
        last = 0 if rev else q - 1
        if rev:
            ci = jnp.where(j < first_lat, first_lat - 1 - j, nc - 1 - (j - first_lat))
        else:
            ci = j
        idx = pl.ds(pl.multiple_of(ci * q, q), q)
        dt_e = _dot_sel_r(dt_s[idx, :], sel)
        xc = xc_s[idx, :]
        bc = bc_s[idx, :]
        cc = cc_s[idx, :]
        cb = _dot_nt(cc, bc)
        y_off = _dot(cc, h)
        yield
        cum_e = _dot_sel(tri, dt_e * a_e[d])
        dtx = xc * dt_e
        yield
        cum_rows = _dot_sel(head_rows, cum_e, _NT)
        tot_e = cum_e[last:last + 1, :]
        states = _dot(bct_s[ci], dtx * jnp.exp(tot_e - cum_e))
        yield
        y = jnp.exp(cum_e) * y_off
        for r in range(4):
            seg = cum_e[:, r * 64:r * 64 + 1] - cum_rows[r:r + 1, :]
            m = cb * jnp.exp(jnp.where(causal, seg, NEG))
            y = y + _dot(m, jnp.where(lane256 == r, dtx, 0.0))
            yield
        o_ref[0, idx, :] += y
        return jnp.exp(tot_e) * h + states

    def chunk(j, hs):
        return tuple(_interleave([chain(0, j, hs[0]), chain(1, j, hs[1])]))
    h0 = jnp.zeros((128, 256), F32)
    lax.fori_loop(0, nc, chunk, (h0, h0))


def _ssd(proj, lanes, conv_w, conv_b, dt_bias, a_log, d_skip, cols):
    bsz, L, _ = proj.shape
    nc = L // CHUNK
    cx, cbm, ccm, cdt = cols
    nh = d_skip.shape[0]
    dtb = jnp.zeros((1, 128), F32).at[0, :2 * nh].set(dt_bias.reshape(-1))
    alog = jnp.zeros((1, 128), F32).at[0, :2 * nh].set(a_log.reshape(-1))
    dsk = jnp.repeat(d_skip.astype(F32), 64).reshape(1, nh * 64)
    sel = np.zeros((2, 2, 128, 256), np.float32)
    for d in range(2):
        for g in range(2):
            for r in range(4):
                sel[d, g, d * nh + g * 4 + r, r * 64:(r + 1) * 64] = 1.0
    cw_specs = []
    for width, off in ((256, 0), (128, 512), (128, 768)):
        cw_specs += [pl.BlockSpec((CONV_W, width), lambda b, g, off=off, width=width: (0, off // width + g)),
                     pl.BlockSpec((1, width), lambda b, g, off=off, width=width: (0, off // width + g))]
    return pl.pallas_call(
        functools.partial(_ssd_kernel, nc=nc),
        grid=(bsz, 2),
        in_specs=[pl.BlockSpec((1, L, 256), lambda b, g: (b, 0, cx // 256 + g)),
                  pl.BlockSpec((1, L, 128), lambda b, g: (b, 0, cbm // 128 + g)),
                  pl.BlockSpec((1, L, 128), lambda b, g: (b, 0, ccm // 128 + g)),
                  pl.BlockSpec((1, L, 128), lambda b, g: (b, 0, cdt // 128))] + cw_specs + [
                  pl.BlockSpec((1, 128), lambda b, g: (0, 0)),
                  pl.BlockSpec((1, 128), lambda b, g: (0, 0)),
                  pl.BlockSpec((1, 256), lambda b, g: (0, g)),
                  pl.BlockSpec((2, 1, 128, 256), lambda b, g: (0, g, 0, 0))],
        out_specs=pl.BlockSpec((1, L, 256), lambda b, g: (b, 0, g)),
        out_shape=jax.ShapeDtypeStruct((bsz, L, nh * 64), F32),
        scratch_shapes=[pltpu.VMEM((L, 256), F32), pltpu.VMEM((L, 128), F32), pltpu.VMEM((L, 128), F32),
                        pltpu.VMEM((L, 128), F32), pltpu.VMEM((nc, 128, CHUNK), BF16)],
        compiler_params=_cp(("parallel", "parallel")), name="ssd",
    )(proj, proj, proj, lanes, conv_w, conv_b.reshape(1, -1), conv_w, conv_b.reshape(1, -1),
      conv_w, conv_b.reshape(1, -1), dtb, alog, dsk, jnp.asarray(sel))


ML_HP = 2


def _ml_gate_lanes(hh, d):
    li = (hh * 2 + d) * 2
    return li, li + 1


def _ml_gate_columns(nh):
    cols = []
    for hp in range(nh // ML_HP):
        blk = [None] * (4 * ML_HP)
        for hh in range(ML_HP):
            for d in range(2):
                for t, lane in enumerate(_ml_gate_lanes(hh, d)):
                    blk[lane] = (d * 2 + t) * nh + hp * ML_HP + hh
        cols.append(blk)
    return cols


def _mlstm_kernel(q_ref, k_ref, v_ref, g_ref, cwq_ref, cbq_ref, cwk_ref, cbk_ref, gb_ref, o_ref,
                  qc_s, kc_s, kct_s, *, nc, dh):
    q = CHUNK
    first_lat = TM // q

    def prep(i, _):
        t0 = pl.multiple_of(i * q, q)
        idx = pl.ds(t0, q)
        for src, cw, cb, dst, mul in ((q_ref, cwq_ref, cbq_ref, qc_s, 1.0), (k_ref, cwk_ref, cbk_ref, kc_s, dh ** -0.5)):
            dst[idx, :] = _silu(_conv_tile(src, i, nc, cw[...], cb[...], q)) * mul
        for hh in range(ML_HP):
            kct_s[i, hh * dh:(hh + 1) * dh, :] = kc_s[idx, hh * dh:(hh + 1) * dh].T.astype(BF16)
        return 0
    lax.fori_loop(0, nc, prep, 0)

    gb = gb_ref[0]
    lane = lax.broadcasted_iota(jnp.int32, (q, 128), 1)
    eye8 = (lax.broadcasted_iota(jnp.int32, (8, 128), 0) == lax.broadcasted_iota(jnp.int32, (8, 128), 1)).astype(F32)
    o_ref[...] = jnp.zeros_like(o_ref)
    heads = range(ML_HP)
    assert q % dh == 0
    wide = lambda col: jnp.concatenate([col] * (q // dh), axis=1)

    def direction(d, j, states):
        rev = d == 1
        causal = _tri(q, rev)
        last = 0 if rev else q - 1
        if rev:
            ci = jnp.where(j < first_lat, first_lat - 1 - j, nc - 1 - (j - first_lat))
        else:
            ci = j
        idx = pl.ds(pl.multiple_of(ci * q, q), q)
        g = g_ref[0, idx, :] + gb
        logf = jnp.minimum(g, 0.0) - jnp.log(1.0 + jnp.exp(-jnp.abs(g)))
        is_f = functools.reduce(jnp.logical_or, [lane == _ml_gate_lanes(hh, d)[1] for hh in heads])
        cum = _dot_sel(jnp.where(causal, 1.0, 0.0), jnp.where(is_f, logf, 0.0))
        qc = [qc_s[idx, hh * dh:(hh + 1) * dh] for hh in heads]
        kc = [kc_s[idx, hh * dh:(hh + 1) * dh] for hh in heads]
        vc = [v_ref[0, idx, hh * dh:(hh + 1) * dh] for hh in heads]
        kct = [kct_s[ci, hh * dh:(hh + 1) * dh, :] for hh in heads]
        qk = [_dot(qc[hh], kct[hh]) for hh in heads]
        inter_c = [_dot(qc[hh], states[hh][0]) for hh in heads]
        yield
        rows = _dot_sel(eye8, jnp.where(is_f, cum, g), _NT)
        yield
        w, w_s, g_in, m_t, keep, m_new, wv, upd = [], [], [], [], [], [], [], []
        for hh in heads:
            li, lf = _ml_gate_lanes(hh, d)
            m_prev = states[hh][2]
            i_col = jnp.broadcast_to(g[:, li:li + 1], (q, dh))
            b_col = jnp.broadcast_to(cum[:, lf:lf + 1], (q, dh))
            i_row, b_row = rows[li:li + 1, :], rows[lf:lf + 1, :]
            dlog = jnp.where(causal, wide(b_col) - b_row + i_row, NEG)
            inter = b_col + m_prev
            m_t.append(jnp.maximum(jnp.max(dlog, axis=-1, keepdims=True), inter))
            w.append(jnp.exp(dlog - wide(m_t[hh])) * qk[hh])
            g_in.append(jnp.exp(inter - m_t[hh]))
            b_end = b_col[last:last + 1, :]
            g_s = b_end - b_col + i_col
            m_new.append(jnp.maximum(jnp.max(g_s, axis=0, keepdims=True), b_end + m_prev))
            w_s.append(jnp.exp(g_s - m_new[hh]))
            keep.append(jnp.exp(b_end + m_prev - m_new[hh]))
            wv.append(_dot(w[hh], vc[hh]))
            upd.append(_dot(kct[hh], w_s[hh] * vc[hh]))
        yield
        new_states = []
        for hh in heads:
            c_prev, n_prev, _ = states[hh]
            num = wv[hh] + g_in[hh] * inter_c[hh]
            den = (jnp.sum(w[hh], axis=-1, keepdims=True)
                   + g_in[hh] * jnp.sum(qc[hh] * n_prev, axis=-1, keepdims=True))
            o_ref[0, idx, hh * dh:(hh + 1) * dh] += num / jnp.maximum(jnp.abs(den), jnp.exp(-m_t[hh]))
            new_states.append((keep[hh] * c_prev + upd[hh],
                               keep[hh] * n_prev + jnp.sum(w_s[hh] * kc[hh], axis=0, keepdims=True), m_new[hh]))
        return tuple(new_states)

    def chunk(j, states):
        return tuple(_interleave([direction(d, j, states[d]) for d in (0, 1)]))
    init = (jnp.zeros((dh, dh), F32), jnp.zeros((1, dh), F32), jnp.full((1, dh), NEG, F32))
    lax.fori_loop(0, nc, chunk, tuple(tuple(init for _ in heads) for _ in (0, 1)))


def _mlstm(proj, lanes, conv_w, conv_b, gate_b, cols):
    bsz, L, _ = proj.shape
    nc = L // CHUNK
    cq, ck, cv, cg = cols
    nh = gate_b.shape[-1]
    dh = conv_w.shape[1] // (2 * nh)
    bw = ML_HP * dh
    gcols = np.asarray(_ml_gate_columns(nh))
    gbl = jnp.zeros((nh // ML_HP, 1, 128), F32).at[:, 0, :gcols.shape[1]].set(gate_b.reshape(-1)[gcols])
    return pl.pallas_call(
        functools.partial(_mlstm_kernel, nc=nc, dh=dh),
        grid=(bsz, nh // ML_HP),
        in_specs=[pl.BlockSpec((1, L, bw), lambda b, h: (b, 0, cq // bw + h)),
                  pl.BlockSpec((1, L, bw), lambda b, h: (b, 0, ck // bw + h)),
                  pl.BlockSpec((1, L, bw), lambda b, h: (b, 0, cv // bw + h)),
                  pl.BlockSpec((1, L, 128), lambda b, h: (b, 0, cg // 128 + h)),
                  pl.BlockSpec((CONV_W, bw), lambda b, h: (0, h)),
                  pl.BlockSpec((1, bw), lambda b, h: (0, h)),
                  pl.BlockSpec((CONV_W, bw), lambda b, h: (0, nh // ML_HP + h)),
                  pl.BlockSpec((1, bw), lambda b, h: (0, nh // ML_HP + h)),
                  pl.BlockSpec((1, 1, 128), lambda b, h: (h, 0, 0))],
        out_specs=pl.BlockSpec((1, L, bw), lambda b, h: (b, 0, h)),
        out_shape=jax.ShapeDtypeStruct((bsz, L, nh * dh), F32),
        scratch_shapes=[pltpu.VMEM((L, bw), F32), pltpu.VMEM((L, bw), F32), pltpu.VMEM((nc, bw, CHUNK), BF16)],
        compiler_params=_cp(("parallel", "parallel")), name="mlstm",
    )(proj, proj, proj, lanes, conv_w, conv_b.reshape(1, -1), conv_w, conv_b.reshape(1, -1), gbl)


def kernel(x, c, ctx, c_ctx, mod_w, mod_b, norm1_g, norm2_g, even_w_in, even_w_out, lru_conv_w, lru_conv_b, lru_wa, lru_ba, lru_wx, lru_bx, lru_lam, na_q_g, na_k_g, na_rpb, odd_w_in, odd_w_out, ssd_conv_w, ssd_conv_b, ssd_dt_bias, ssd_a_log, ssd_d, ssd_norm_g, ml_conv_w, ml_conv_b, ml_gate_b, ml_norm_g, moe_router_g, moe_router_e, moe_w1, moe_w3, moe_w2):
    bsz, S, d = x.shape
    lc = ctx.shape[1]
    assert lc == TM and S % TM == 0 and mod_w.shape[0] == 2
    nt = (lc + S) // TM
    L = lc + S

    cc = jnp.zeros((8, d), F32).at[0].set(c_ctx).at[1:1 + bsz].set(c)
    mod_all = _modulation(cc, mod_w, mod_b)

    def mod_for(l):
        m = mod_all[l].reshape(8, 6, d)
        return jnp.stack([jnp.broadcast_to(m[0], (bsz, 6, d)), m[1:1 + bsz]], axis=1)

    def router_w(l):
        w = jnp.zeros((d, 128), F32).at[:, :EXPERT_LANE0].set(moe_router_g[l]) \
            .at[:, EXPERT_LANE0:EXPERT_LANE0 + N_EXPERTS].set(moe_router_e[l])
        hi = w.astype(BF16)
        return jnp.concatenate([hi, (w - hi.astype(F32)).astype(BF16)], axis=1)

    xx = (ctx, x)

    mod0 = mod_for(0)
    (proj,) = _inproj(xx, mod0, norm1_g[0], even_w_in[0].astype(BF16))
    lw = lru_conv_w.shape[-1]
    r = _lru(proj, lru_conv_w[0], lru_conv_b[0], lru_wa[0], lru_ba[0], lru_wx[0], lru_bx[0], lru_lam[0])
    a = _na(proj, na_q_g[0], na_k_g[0], na_rpb[0], col0=2 * lw)
    x1, h2, gate = _outproj([(r, 0, lw), (proj, 1, lw), (a, 0, lw)], xx, even_w_out[0].astype(BF16), mod0,
                            norm2_g[0], router_w(0), even=True, tile0=0, ntiles=nt)
    ew = (moe_w1.astype(BF16), moe_w3.astype(BF16), moe_w2.astype(BF16))
    moe0 = _moe(h2.reshape(bsz * L, d), gate.reshape(bsz * L, 128), 0, *ew).reshape(bsz, L, d)

    mod1 = mod_for(1)
    sw = ssd_d.shape[-1] * 64
    xbc = ssd_conv_w.shape[-1]
    mw = ml_norm_g.shape[-1]
    w = odd_w_in[0]
    o = np.cumsum([0, sw, xbc, 2 * ssd_d.shape[-1], mw, mw, mw, mw])
    ndt = 2 * ssd_d.shape[-1]
    dt_blk = jnp.concatenate([w[:, o[2]:o[3]], jnp.zeros((d, 128 - ndt), F32)], axis=1)
    gate_blks = [jnp.concatenate([w[:, o[7] + np.asarray(blk)], jnp.zeros((d, 128 - len(blk)), F32)], axis=1)
                 for blk in _ml_gate_columns(ml_gate_b.shape[-1])]
    w_odd = jnp.concatenate([w[:, :o[2]], w[:, o[3]:o[7]], dt_blk] + gate_blks, axis=1).astype(BF16)
    cz, cxs = 0, sw
    cB, cC = cxs + sw, cxs + sw + (xbc - sw) // 2
    cq = sw + xbc
    ck, cv, co, csm = cq + mw, cq + 2 * mw, cq + 3 * mw, cq + 4 * mw
    x0, proj1, lanes1 = _inproj(x1, mod1, norm1_g[1], w_odd, prev=(moe0, mod0), n_f32=w_odd.shape[1] - csm)
    ys = _ssd(proj1, lanes1, ssd_conv_w[0], ssd_conv_b[0], ssd_dt_bias[0], ssd_a_log[0], ssd_d[0], (cxs, cB, cC, 0))
    hm = _mlstm(proj1, lanes1, ml_conv_w[0], ml_conv_b[0], ml_gate_b[0], (cq, ck, cv, 128))
    x2, h2b, gate1 = _outproj([(ys, 0, sw), (proj1, cz // sw, sw), (hm, 0, mw), (proj1, co // mw, mw),
                               (ssd_norm_g[0].reshape(1, sw), 0, sw), (ml_norm_g[0].reshape(1, mw), 0, mw)],
                              x0, odd_w_out[0].astype(BF16), mod1, norm2_g[1], router_w(1),
                              even=False, tile0=1, ntiles=nt - 1)
    out = _moe(h2b.reshape(bsz * S, d), gate1.reshape(bsz * S, 128), 1, *ew, resid=(x2.reshape(bsz * S, d), mod1))
    return out.reshape(bsz, S, d)
```

```python
import functools
import math

import jax
import jax.numpy as jnp
import numpy as np
from jax import lax
from jax.experimental import pallas as pl
from jax.experimental.pallas import tpu as pltpu

F32 = jnp.float32
BF16 = jnp.bfloat16
HI = lax.Precision.HIGHEST

EPS = 1e-6
NEG = -1e30
GRID_W = 64
CONV_W = 4
LRU_C = 8.0
TM = 256
CHUNK = 256
NA_RQ = 4
NA_RK = 12
N_EXPERTS = 16
EXPERT_LANE0 = 4
VMEM_LIMIT = 56 * 1024 * 1024


def _cp(sem, vmem=VMEM_LIMIT):
    return pltpu.CompilerParams(dimension_semantics=sem, vmem_limit_bytes=vmem)


def _sigmoid(x):
    return jax.nn.sigmoid(x)


def _silu(x):
    return x * jax.nn.sigmoid(x)


def _softplus(x):
    return jnp.maximum(x, 0.0) + jnp.log(1.0 + jnp.exp(-jnp.abs(x)))


def _gelu_tanh(x):
    return 0.5 * x * (1.0 + jnp.tanh(math.sqrt(2.0 / math.pi) * (x + 0.044715 * (x * x * x))))


def _rms(x, axis=-1):
    return x * lax.rsqrt(jnp.mean(x * x, axis=axis, keepdims=True) + EPS)


def _dot(a, b):
    return jnp.dot(a.astype(BF16), b.astype(BF16), preferred_element_type=F32)


def _dot_hi(a, b):
    return jnp.dot(a, b, precision=HI, preferred_element_type=F32)


def _split3(x):
    x1 = x.astype(BF16)
    r = x - x1.astype(F32)
    x2 = r.astype(BF16)
    x3 = (r - x2.astype(F32)).astype(BF16)
    return x1, x2, x3


def _dot_sel(sel, x, dims=(((1,), (0,)), ((), ()))):
    s = sel.astype(BF16)
    x1, x2, x3 = _split3(x)
    d = lambda xi: lax.dot_general(s, xi, dims, preferred_element_type=F32)
    return (d(x3) + d(x2)) + d(x1)


def _dot_sel_r(x, sel):
    s = sel.astype(BF16)
    x1, x2, x3 = _split3(x)
    d = lambda xi: jnp.dot(xi, s, preferred_element_type=F32)
    return (d(x3) + d(x2)) + d(x1)


_NT = (((1,), (1,)), ((), ()))


def _interleave(gens):
    results = [None] * len(gens)
    live = list(range(len(gens)))
    while live:
        for i in list(live):
            try:
                next(gens[i])
            except StopIteration as stop:
                results[i] = stop.value
                live.remove(i)
    return results


def _dot_nt(a, b):
    return lax.dot_general(a.astype(BF16), b.astype(BF16), (((1,), (1,)), ((), ())),
                           preferred_element_type=F32)


def _mod_kernel(c_ref, w_ref, b_ref, o_ref):
    c = c_ref[...]
    o_ref[0] = _dot_hi(_silu(c), w_ref[0]) + b_ref[0]


def _modulation(cc, mod_w, mod_b):
    depth, d, n = mod_w.shape
    tn = 1536
    return pl.pallas_call(
        _mod_kernel,
        grid=(depth, n // tn),
        in_specs=[pl.BlockSpec((8, d), lambda l, j: (0, 0)),
                  pl.BlockSpec((1, d, tn), lambda l, j: (l, 0, j)),
                  pl.BlockSpec((1, 1, tn), lambda l, j: (l, 0, j))],
        out_specs=pl.BlockSpec((1, 8, tn), lambda l, j: (l, 0, j)),
        out_shape=jax.ShapeDtypeStruct((depth, 8, n), F32),
        compiler_params=_cp(("arbitrary", "arbitrary")),
        name="adaln_mod",
    )(cc, mod_w, mod_b.reshape(depth, 1, n))


def _inproj_kernel(*refs, fuse_prev, n_f32):
    refs = list(refs)
    s_ref = refs.pop() if n_f32 else None
    if fuse_prev:
        x_ref, mo_ref, pmod_ref, mod_ref, g_ref, w_ref, xo_ref, p_ref = refs
        x = x_ref[0] + pmod_ref[0, 0][5:6, :] * mo_ref[0]
        xo_ref[0] = x
    else:
        c_ref, x_ref, mod_ref, g_ref, w_ref, p_ref = refs
        x = _pick_segment(c_ref, x_ref)
    mod = mod_ref[0, 0]
    h = _rms(x) * g_ref[...] * (1.0 + mod[1:2, :]) + mod[0:1, :]
    p = _dot(h, w_ref[...])
    n_main = p.shape[-1] - n_f32
    p_ref[0] = p[:, :n_main].astype(BF16)
    if n_f32:
        s_ref[0] = p[:, n_main:]


def _seg_map(b, i):
    return (b, jnp.minimum(i, 1), 0, 0)


def _segment_specs(d):
    return [pl.BlockSpec((1, TM, d), lambda b, i: (b, 0, 0)),
            pl.BlockSpec((1, TM, d), lambda b, i: (b, jnp.maximum(i - 1, 0), 0))]


def _pick_segment(c_ref, x_ref):
    return jnp.where(pl.program_id(1) == 0, c_ref[0], x_ref[0])


def _inproj(x, mod, g, w, prev=None, n_f32=0):
    if prev is None:
        ctx, lat = x
        bsz, S, d = lat.shape
        L = TM + S
    else:
        bsz, L, d = x.shape
    n = w.shape[1]
    nt = L // TM
    tok = pl.BlockSpec((1, TM, d), lambda b, i: (b, i, 0))
    modspec = pl.BlockSpec((1, 1, 6, d), _seg_map)
    tail = [modspec, pl.BlockSpec((1, d), lambda b, i: (0, 0)), pl.BlockSpec((d, n), lambda b, i: (0, 0))]
    pspecs = [pl.BlockSpec((1, TM, n - n_f32), lambda b, i: (b, i, 0))]
    pshapes = [jax.ShapeDtypeStruct((bsz, L, n - n_f32), BF16)]
    if n_f32:
        pspecs.append(pl.BlockSpec((1, TM, n_f32), lambda b, i: (b, i, 0)))
        pshapes.append(jax.ShapeDtypeStruct((bsz, L, n_f32), F32))
    if prev is None:
        return pl.pallas_call(
            functools.partial(_inproj_kernel, fuse_prev=False, n_f32=n_f32),
            grid=(bsz, nt), in_specs=_segment_specs(d) + tail, out_specs=pspecs, out_shape=pshapes,
            compiler_params=_cp(("parallel", "arbitrary")), name="inproj",
        )(ctx, lat, mod, g.reshape(1, d), w)
    moe_out, pmod = prev
    return pl.pallas_call(
        functools.partial(_inproj_kernel, fuse_prev=True, n_f32=n_f32),
        grid=(bsz, nt), in_specs=[tok, tok, modspec] + tail,
        out_specs=[tok] + pspecs, out_shape=[jax.ShapeDtypeStruct(x.shape, F32)] + pshapes,
        compiler_params=_cp(("parallel", "parallel")), name="inproj_res",
    )(x, moe_out, pmod, mod, g.reshape(1, d), w)


def _conv_tile(ref, i, nt, cw, cb, width=TM):
    L = nt * width
    t0 = pl.multiple_of(i * width, width)
    cur = ref[0, pl.ds(t0, width), :].astype(F32)
    prev = ref[0, pl.ds(pl.multiple_of(jnp.maximum(t0 - 16, 0), 16), 16), :].astype(F32)[8:16]
    nxt = ref[0, pl.ds(pl.multiple_of(jnp.minimum(t0 + width, L - 16), 16), 16), :].astype(F32)[0:8]
    return _conv_vals(cur, prev, nxt, i, nt, cw, cb, width, first_lat=TM // width)


def _conv_vals(cur, prev, nxt, i, nt, cw, cb, width, first_lat):
    prev = jnp.where((i != 0) & (i != first_lat), prev, 0.0)
    nxt = jnp.where((i != first_lat - 1) & (i != nt - 1), nxt, 0.0)
    cat = jnp.concatenate([prev, cur, nxt], axis=0)
    return (cw[0:1] * cat[6:6 + width] + cw[1:2] * cat[7:7 + width] + cw[2:3] * cur
            + cw[3:4] * cat[9:9 + width] + cb)


def _lru_kernel(ux_ref, cw_ref, cb_ref, gw_ref, gb_ref, lam_ref, o_ref, *, nt):
    cw = cw_ref[...]
    cb = cb_ref[...]
    lam = lam_ref[0]
    sp = _softplus(-lam)
    row = lax.broadcasted_iota(jnp.int32, (TM, 128), 0) & 7

    def gates(i, d):
        xl = _conv_tile(ux_ref, i, nt, cw, cb)
        g = _dot(xl, gw_ref[0, d]) + gb_ref[0, d]
        r = _sigmoid(g[:, :128])
        ig = _sigmoid(g[:, 128:])
        log_a = -LRU_C * r * sp[d:d + 1]
        a = jnp.exp(log_a)
        u = jnp.sqrt(1.0 - a * a) * (ig * xl)
        return a, u

    def scan_tile(i, d, carry, accumulate):
        a, u = gates(i, d)
        rev = d == 1
        for k in (1, 2, 4):
            sh = 8 - k if rev else k
            ok = (row < 8 - k) if rev else (row >= k)
            ash = pltpu.roll(a.reshape(TM // 8, 8, 128), sh, 1).reshape(TM, 128)
            ush = pltpu.roll(u.reshape(TM // 8, 8, 128), sh, 1).reshape(TM, 128)
            u = jnp.where(ok, u + a * ush, u)
            a = jnp.where(ok, a * ash, a)
        t0 = i * TM
        groups = range(TM // 8)
        for s in (reversed(groups) if rev else groups):
            h = u[s * 8:(s + 1) * 8] + a[s * 8:(s + 1) * 8] * carry
            carry = h[0:1] if rev else h[7:8]
            idx = pl.ds(pl.multiple_of(t0 + s * 8, 8), 8)
            if accumulate:
                o_ref[0, idx, :] += h
            else:
                o_ref[0, idx, :] = h
        return carry

    zero = jnp.zeros((1, 128), F32)
    lax.fori_loop(0, nt, lambda i, c: scan_tile(i, 0, c, False), zero)
    lax.fori_loop(0, nt, lambda j, c: scan_tile(jnp.where(j == 0, 0, nt - j), 1, c, True), zero)


def _lru(proj, conv_w, conv_b, wa, ba, wx, bx, lam):
    bsz, L, _ = proj.shape
    nt = L // TM
    width = conv_w.shape[1]
    ng = width // 128

    def blockdiag(w):
        w = w.reshape(2, ng, 2, 64, 64)
        z = jnp.zeros_like(w[:, :, 0])
        top = jnp.concatenate([w[:, :, 0], z], axis=-1)
        bot = jnp.concatenate([z, w[:, :, 1]], axis=-1)
        return jnp.concatenate([top, bot], axis=-2)
    gw = jnp.concatenate([blockdiag(wa), blockdiag(wx)], axis=-1).transpose(1, 0, 2, 3).astype(BF16)
    gb = jnp.concatenate([ba.reshape(2, ng, 1, 128), bx.reshape(2, ng, 1, 128)], axis=-1).transpose(1, 0, 2, 3)
    lam_g = lam.reshape(2, ng, 128).transpose(1, 0, 2)
    return pl.pallas_call(
        functools.partial(_lru_kernel, nt=nt),
        grid=(bsz, ng),
        in_specs=[pl.BlockSpec((1, L, 128), lambda b, c: (b, 0, c)),
                  pl.BlockSpec((CONV_W, 128), lambda b, c: (0, c)),
                  pl.BlockSpec((1, 128), lambda b, c: (0, c)),
                  pl.BlockSpec((1, 2, 128, 256), lambda b, c: (c, 0, 0, 0)),
                  pl.BlockSpec((1, 2, 1, 256), lambda b, c: (c, 0, 0, 0)),
                  pl.BlockSpec((1, 2, 128), lambda b, c: (c, 0, 0))],
        out_specs=pl.BlockSpec((1, L, 128), lambda b, c: (b, 0, c)),
        out_shape=jax.ShapeDtypeStruct((bsz, L, width), F32),
        compiler_params=_cp(("parallel", "parallel")), name="rglru",
    )(proj, conv_w, conv_b.reshape(1, width), gw, gb, lam_g)


def _na_bias_table(rpb, rows):
    nh = rpb.shape[0]
    win_c = (rpb.shape[2] + 1) // 2
    qc = np.arange(GRID_W)[:, None]
    kc = np.arange(GRID_W)[None, :]
    cstart = np.clip(qc - win_c // 2, 0, GRID_W - win_c)
    col_ok = (kc >= cstart) & (kc < cstart + win_c)
    dcol = np.clip(kc - qc + (win_c - 1), 0, 2 * win_c - 2)
    oc = (np.arange(2 * win_c - 1)[:, None, None] == dcol[None]).astype(np.float32)
    ct = jnp.where(jnp.asarray(col_ok), jnp.einsum('hrc,cqk->hrqk', rpb.astype(F32), jnp.asarray(oc), precision=HI),
                   NEG)
    return pl.pallas_call(
        functools.partial(_na_bias_kernel, rows=rows),
        grid=(nh,),
        in_specs=[pl.BlockSpec((1,) + ct.shape[1:], lambda h: (h, 0, 0, 0))],
        out_specs=pl.BlockSpec((3, 1, TM, NA_RK * GRID_W), lambda h: (0, h, 0, 0)),
        out_shape=jax.ShapeDtypeStruct((3, nh, TM, NA_RK * GRID_W), F32),
        compiler_params=_cp(("parallel",)), name="nbr_bias",
    )(ct)


def _na_bias_kernel(ct_ref, o_ref, *, rows):
    win_r = (ct_ref.shape[1] + 1) // 2
    neg = jnp.full((GRID_W, GRID_W), NEG, F32)
    for p, (r0, w0) in enumerate(((0, 0), (2 * NA_RQ, NA_RQ), (rows - NA_RQ, rows - NA_RK))):
        for a in range(NA_RQ):
            r = r0 + a
            rstart = min(max(r - win_r // 2, 0), rows - win_r)
            tiles = [ct_ref[0, w0 + b - r + win_r - 1] if rstart <= w0 + b < rstart + win_r else neg
                     for b in range(NA_RK)]
            for bp in range(NA_RK // 2):
                o_ref[p, 0, a * GRID_W:(a + 1) * GRID_W, bp * 2 * GRID_W:(bp + 1) * 2 * GRID_W] = (
                    jnp.concatenate(tiles[2 * bp:2 * bp + 2], axis=1))


def _pair_rms(x, lo):
    x2 = x * x
    s0 = jnp.sum(jnp.where(lo, x2, 0.0), axis=-1, keepdims=True)
    s1 = jnp.sum(jnp.where(lo, 0.0, x2), axis=-1, keepdims=True)
    return x * lax.rsqrt(jnp.where(lo, s0, s1) * (2.0 / x.shape[-1]) + EPS)


def _na_kernel(q_ref, k_ref, v_ref, bias0_ref, bias1_ref, qg_ref, kg_ref, o_ref, kn_s, vb_s, *, rows, hd):
    j = pl.program_id(2)
    L = k_ref.shape[1]
    nb = L // TM
    nkeys = NA_RK * GRID_W
    lo = lax.broadcasted_iota(jnp.int32, (1, 2 * hd), 1) < hd

    @pl.when(j == 0)
    def _prep():
        def body(t, _):
            idx = pl.ds(pl.multiple_of(t * TM, TM), TM)
            kn_s[idx, :] = (_pair_rms(k_ref[0, idx, :].astype(F32), lo) * kg_ref[...]).astype(BF16)
            vb_s[idx, :] = v_ref[0, idx, :].astype(BF16)
            return 0
        lax.fori_loop(0, nb, body, 0)

    kctx = kn_s[0:TM, :]
    vctx = vb_s[0:TM, :]
    qscale = qg_ref[...] * hd ** -0.5

    def head(q, bias, kwin, vwin):
        s_c = _dot_nt(q, kctx)
        if kwin is not None:
            s_w = _dot_nt(q, kwin) + bias
        yield
        m = jnp.max(s_c, axis=-1, keepdims=True)
        if kwin is not None:
            m = jnp.maximum(m, jnp.max(s_w, axis=-1, keepdims=True))
            p_w = jnp.exp(s_w - m)
        p_c = jnp.exp(s_c - m)
        den = jnp.sum(p_c, axis=-1, keepdims=True)
        num = _dot(p_c, vctx)
        if kwin is not None:
            den = den + jnp.sum(p_w, axis=-1, keepdims=True)
            num = num + _dot(p_w, vwin)
        yield
        return num / den

    def tile_chains(slot, bias_ref, tile):
        qn = _pair_rms(q_ref[0, slot * TM:(slot + 1) * TM, :].astype(F32), lo) * qscale
        q_h = [jnp.where(lo, qn, 0.0).astype(BF16), jnp.where(lo, 0.0, qn).astype(BF16)]
        if tile is None:
            return [head(q_h[hh], None, None, None) for hh in range(2)]
        w0 = jnp.clip((tile - 1) * NA_RQ - NA_RQ, 0, rows - NA_RK)
        start = pl.multiple_of(TM + w0 * GRID_W, GRID_W)
        kwin = kn_s[pl.ds(start, nkeys), :]
        vwin = vb_s[pl.ds(start, nkeys), :]
        return [head(q_h[hh], bias_ref[0, hh], kwin, vwin) for hh in range(2)]

    def run(chains):
        outs = _interleave(chains)
        for slot in range(len(outs) // 2):
            o_ref[0, slot * TM:(slot + 1) * TM, :] = jnp.where(lo, outs[2 * slot], outs[2 * slot + 1]).astype(BF16)

    last = (nb - 1) // 2

    @pl.when(j == 0)
    def _():
        run(tile_chains(0, None, None) + tile_chains(1, bias1_ref, 1))

    @pl.when((j > 0) & (j < last))
    def _():
        run(tile_chains(0, bias0_ref, 2 * j) + tile_chains(1, bias1_ref, 2 * j + 1))

    @pl.when(j == last)
    def _():
        run(tile_chains(0, bias0_ref, 2 * j))


def _na(proj, q_g, k_g, rpb, col0):
    bsz, L, _ = proj.shape
    nh = rpb.shape[0]
    hd = q_g.shape[0]
    width = nh * hd
    rows = (L - TM) // GRID_W
    nb = L // TM
    bias = _na_bias_table(rpb, rows)
    qb, kb, vb = col0 // 128, (col0 + width) // 128, (col0 + 2 * width) // 128

    assert nb % 2 == 1

    def pat(t):
        return jnp.where(t <= 1, 0, jnp.where(t >= nb - 1, 2, 1))
    bias_spec = lambda slot: pl.BlockSpec((1, 2, TM, NA_RK * GRID_W), lambda b, h, j: (pat(2 * j + slot), h, 0, 0))
    return pl.pallas_call(
        functools.partial(_na_kernel, rows=rows, hd=hd),
        grid=(bsz, width // 128, (nb + 1) // 2),
        in_specs=[pl.BlockSpec((1, 2 * TM, 128), lambda b, h, j: (b, j, qb + h)),
                  pl.BlockSpec((1, L, 128), lambda b, h, j: (b, 0, kb + h)),
                  pl.BlockSpec((1, L, 128), lambda b, h, j: (b, 0, vb + h)),
                  bias_spec(0), bias_spec(1),
                  pl.BlockSpec((1, 2 * hd), lambda b, h, j: (0, 0)),
                  pl.BlockSpec((1, 2 * hd), lambda b, h, j: (0, 0))],
        out_specs=pl.BlockSpec((1, 2 * TM, 128), lambda b, h, j: (b, j, h)),
        out_shape=jax.ShapeDtypeStruct((bsz, L, width), BF16),
        scratch_shapes=[pltpu.VMEM((L, 128), BF16), pltpu.VMEM((L, 128), BF16)],
        compiler_params=_cp(("parallel", "parallel", "arbitrary")), name="nbr_attn",
    )(proj, proj, proj, bias, bias, jnp.tile(q_g, 2).reshape(1, 2 * hd), jnp.tile(k_g, 2).reshape(1, 2 * hd))


def _route(lg):
    lane = lax.broadcasted_iota(jnp.int32, lg.shape, 1)
    lane_f = lane.astype(F32)
    is_g = lane < EXPERT_LANE0
    gl = jnp.where(is_g, lg, NEG)
    gmax = jnp.max(gl, axis=-1, keepdims=True)
    gsel = jnp.min(jnp.where(is_g & (gl == gmax), lane_f, 1e9), axis=-1, keepdims=True)
    g_w = 1.0 / jnp.sum(jnp.where(is_g, jnp.exp(gl - gmax), 0.0), axis=-1, keepdims=True)
    grp = ((lane - EXPERT_LANE0) >> 2).astype(F32)
    in_g = (lane >= EXPERT_LANE0) & (lane < EXPERT_LANE0 + N_EXPERTS) & (grp == gsel)
    el = jnp.where(in_g, lg, NEG)
    v1 = jnp.max(el, axis=-1, keepdims=True)
    i1 = jnp.min(jnp.where(in_g & (el == v1), lane_f, 1e9), axis=-1, keepdims=True)
    el2 = jnp.where(lane_f == i1, NEG, el)
    v2 = jnp.max(el2, axis=-1, keepdims=True)
    i2 = jnp.min(jnp.where(in_g & (lane_f != i1) & (el2 == v2), lane_f, 1e9), axis=-1, keepdims=True)
    t = jnp.exp(v2 - v1)
    w1 = g_w / (1.0 + t)
    w2 = g_w * t / (1.0 + t)
    return (jnp.where(lane_f == i1, w1, 0.0) + jnp.where(lane_f == i2, w2, 0.0)
            + jnp.where(lane_f == gsel, 1.0, 0.0))


def _outproj_kernel(*refs, even):
    if even:
        (r_ref, ug_ref, a_ref, c_ref, x_ref, w_ref, mod_ref, g2_ref, rw_ref, x1_ref, h2_ref, gate_ref) = refs
        is_ctx = pl.program_id(1) == 0
        x_rows = lambda rs: jnp.where(is_ctx, c_ref[0, rs, :], x_ref[0, rs, :])
    else:
        (ys_ref, z_ref, hm_ref, mo_ref, sg_ref, mg_ref, x_ref, w_ref, mod_ref, g2_ref, rw_ref,
         x1_ref, h2_ref, gate_ref) = refs
        x_rows = lambda rs: x_ref[0, rs, :]
    mod = mod_ref[0, 0]

    def part(rs):
        if even:
            y_in = jnp.concatenate([(r_ref[0, rs, :] * _gelu_tanh(ug_ref[0, rs, :].astype(F32))).astype(BF16),
                                    a_ref[0, rs, :]], axis=-1)
        else:
            ys = ys_ref[0, rs, :] * _silu(z_ref[0, rs, :].astype(F32))
            sg = sg_ref[...]
            mg = mg_ref[...]
            hm = hm_ref[0, rs, :]
            sig_o = _sigmoid(mo_ref[0, rs, :].astype(F32))
            gw = ys.shape[-1] // 2
            parts = [_rms(ys[:, g * gw:(g + 1) * gw]) * sg[:, g * gw:(g + 1) * gw] for g in range(2)]
            hw = 128
            parts += [_rms(hm[:, h * hw:(h + 1) * hw]) * mg[:, h * hw:(h + 1) * hw] * sig_o[:, h * hw:(h + 1) * hw]
                      for h in range(hm.shape[-1] // hw)]
            y_in = jnp.concatenate(parts, axis=-1)
        y = _dot(y_in, w_ref[...])
        yield
        x1 = x_rows(rs) + mod[2:3, :] * y
        x1_ref[0, rs, :] = x1
        h2 = _rms(x1) * g2_ref[...] * (1.0 + mod[4:5, :]) + mod[3:4, :]
        h2_ref[0, rs, :] = h2.astype(BF16)
        hi = h2.astype(BF16)
        lo = (h2 - hi.astype(F32)).astype(BF16)
        lg2 = jnp.dot(hi, rw_ref[...], preferred_element_type=F32)
        lg1 = jnp.dot(lo, rw_ref[:, 0:128], preferred_element_type=F32)
        yield
        gate_ref[0, rs, :] = _route(lg2[:, 0:128] + lg2[:, 128:256] + lg1)

    nparts = 2
    rows = TM // nparts
    _interleave([part(pl.ds(p * rows, rows)) for p in range(nparts)])


def _outproj(mix_inputs, x, w, mod, g2, rw, even, tile0, ntiles):
    xs = list(x) if even else [x]
    bsz, _, d = xs[-1].shape
    specs, args = [], []
    for arr, cb, wdt in mix_inputs:
        if arr.ndim == 3:
            specs.append(pl.BlockSpec((1, TM, wdt), lambda b, i, cb=cb: (b, i + tile0, cb)))
        else:
            specs.append(pl.BlockSpec((1, wdt), lambda b, i: (0, 0)))
        args.append(arr)
    specs += _segment_specs(d) if even else [pl.BlockSpec((1, TM, d), lambda b, i: (b, i + tile0, 0))]
    specs += [pl.BlockSpec(w.shape, lambda b, i: (0, 0)),
              pl.BlockSpec((1, 1, 6, d), lambda b, i: (b, jnp.minimum(i + tile0, 1), 0, 0)),
              pl.BlockSpec((1, d), lambda b, i: (0, 0)),
              pl.BlockSpec(rw.shape, lambda b, i: (0, 0))]
    args += xs + [w, mod, g2.reshape(1, d), rw]
    lo = ntiles * TM
    return pl.pallas_call(
        functools.partial(_outproj_kernel, even=even),
        grid=(bsz, ntiles), in_specs=specs,
        out_specs=[pl.BlockSpec((1, TM, d), lambda b, i: (b, i, 0)),
                   pl.BlockSpec((1, TM, d), lambda b, i: (b, i, 0)),
                   pl.BlockSpec((1, TM, 128), lambda b, i: (b, i, 0))],
        out_shape=[jax.ShapeDtypeStruct((bsz, lo, d), F32), jax.ShapeDtypeStruct((bsz, lo, d), BF16),
                   jax.ShapeDtypeStruct((bsz, lo, 128), F32)],
        compiler_params=_cp(("parallel", "parallel")), name="outproj_even" if even else "outproj_odd",
    )(*args)


MOE_CH = 128
GROUP_SIZE = 4
MOE_VMEM_LIMIT = 62 * 1024 * 1024
N_GROUPS = N_EXPERTS // GROUP_SIZE


def _moe_kernel(*refs, residual):
    if residual:
        x_ref, g_ref, w1_ref, w3_ref, w2_ref, x1_ref, mod_ref, o_ref, xs_s, ys_s, gs_s, pt_s, plan_s = refs
    else:
        x_ref, g_ref, w1_ref, w3_ref, w2_ref, o_ref, xs_s, ys_s, gs_s, pt_s, plan_s = refs
    grp = pl.program_id(1)
    tb = x_ref.shape[0]
    nch = xs_s.shape[0] // MOE_CH

    @pl.when(grp == 0)
    def _plan():
        g = g_ref[...]
        lane = lax.broadcasted_iota(jnp.int32, g.shape, 1)
        oh = jnp.where(lane < N_GROUPS, g, 0.0)
        earlier = (lax.broadcasted_iota(jnp.int32, (tb, tb), 0) > lax.broadcasted_iota(jnp.int32, (tb, tb), 1))
        rank = jnp.dot(jnp.where(earlier, 1.0, 0.0).astype(BF16), oh.astype(BF16),
                       preferred_element_type=F32)
        cnt = jnp.sum(oh, axis=0, keepdims=True)
        lane1 = lax.broadcasted_iota(jnp.int32, (1, 128), 1)
        off = jnp.int32(0)
        offv = jnp.zeros((1, 128), F32)
        for gi in range(N_GROUPS):
            n = jnp.sum(jnp.where(lane1 == gi, cnt, 0.0)).astype(jnp.int32)
            nchunks = (n + (MOE_CH - 1)) // MOE_CH
            plan_s[gi] = off // MOE_CH
            plan_s[N_GROUPS + gi] = nchunks
            offv = offv + jnp.where(lane1 == gi, off.astype(F32), 0.0)
            off = off + nchunks * MOE_CH
        pos_col = jnp.sum(oh * (rank + offv), axis=1, keepdims=True)
        posb = jnp.broadcast_to(pos_col, (tb, 128))
        pos_row = jnp.concatenate([posb[i * 128:(i + 1) * 128, :].T[0:1, :] for i in range(tb // 128)], axis=1)
        x = x_ref[...]
        g_hi = g.astype(BF16)
        g_lo = (g - g_hi.astype(F32)).astype(BF16)
        lane_c = lax.broadcasted_iota(jnp.int32, (tb, MOE_CH), 1).astype(F32)
        row_c = lax.broadcasted_iota(jnp.int32, (MOE_CH, tb), 0).astype(F32)
        used = off // MOE_CH

        def dispatch(c):
            sl = slice(c * MOE_CH, (c + 1) * MOE_CH)
            p = jnp.where(row_c + float(c * MOE_CH) == pos_row, 1.0, 0.0).astype(BF16)
            xs_s[sl, :] = jnp.dot(p, x, preferred_element_type=F32).astype(BF16)
            gs_s[sl, :] = (jnp.dot(p, g_hi, preferred_element_type=F32)
                           + jnp.dot(p, g_lo, preferred_element_type=F32))
        for c in range(nch):
            sl = slice(c * MOE_CH, (c + 1) * MOE_CH)
            pt_s[:, sl] = jnp.where(pos_col == lane_c + float(c * MOE_CH), 1.0, 0.0).astype(BF16)
            if c < tb // MOE_CH:
                dispatch(c)
            else:
                pl.when(c < used)(functools.partial(dispatch, c))
        ys_s[...] = jnp.zeros_like(ys_s)

    c0 = plan_s[grp]
    nchunks = plan_s[N_GROUPS + grp]

    def ffn(chunk0, nrows):
        rows = pl.ds(pl.multiple_of(chunk0 * MOE_CH, MOE_CH), nrows)
        xs = xs_s[rows, :]
        gs = gs_s[rows, :]
        lane_g = lax.broadcasted_iota(jnp.int32, (nrows, 128), 1)
        acc = None
        for k in range(GROUP_SIZE):
            a = jnp.dot(xs, w1_ref[0, k], preferred_element_type=F32)
            b = jnp.dot(xs, w3_ref[0, k], preferred_element_type=F32)
            y = _dot(_silu(a) * b, w2_ref[0, k])
            ge = jnp.sum(jnp.where(lane_g == grp * GROUP_SIZE + k + EXPERT_LANE0, gs, 0.0), axis=-1, keepdims=True)
            acc = ge * y if acc is None else acc + ge * y
        ys_s[rows, :] = acc.astype(BF16)

    wide = 4

    def full(p, _):
        ffn(c0 + wide * p, wide * MOE_CH)
        return 0
    lax.fori_loop(0, nchunks // wide, full, 0)
    rem = nchunks % wide
    for r in range(1, wide):
        pl.when(rem == r)(functools.partial(ffn, c0 + nchunks - r, r * MOE_CH))

    @pl.when(grp == N_GROUPS - 1)
    def _combine():
        out = jnp.dot(pt_s[...], ys_s[...], preferred_element_type=F32)
        if residual:
            out = x1_ref[...] + mod_ref[0, 0][5:6, :] * out
        o_ref[...] = out


def _moe(h2, gate, layer, w1, w3, w2, resid=None):
    t, d = h2.shape
    _, ne, _, ff = w1.shape
    tb = math.gcd(t, 1024)
    npad = tb + N_GROUPS * MOE_CH
    assert ne == N_EXPERTS
    extra_specs, extra_args = [], []
    if resid is not None:
        x1, mod = resid
        per_batch = t // mod.shape[0] // tb
        extra_specs = [pl.BlockSpec((tb, d), lambda i, e: (i, 0)),
                       pl.BlockSpec((1, 1, 6, d), lambda i, e: (i // per_batch, 1, 0, 0))]
        extra_args = [x1, mod]
    return pl.pallas_call(
        functools.partial(_moe_kernel, residual=resid is not None),
        grid=(t // tb, N_GROUPS),
        in_specs=[pl.BlockSpec((tb, d), lambda i, e: (i, 0)),
                  pl.BlockSpec((tb, 128), lambda i, e: (i, 0)),
                  pl.BlockSpec((1, GROUP_SIZE, d, ff), lambda i, e: (layer, e, 0, 0)),
                  pl.BlockSpec((1, GROUP_SIZE, d, ff), lambda i, e: (layer, e, 0, 0)),
                  pl.BlockSpec((1, GROUP_SIZE, ff, d), lambda i, e: (layer, e, 0, 0))] + extra_specs,
        out_specs=pl.BlockSpec((tb, d), lambda i, e: (i, 0)),
        out_shape=jax.ShapeDtypeStruct((t, d), F32),
        scratch_shapes=[pltpu.VMEM((npad, d), BF16), pltpu.VMEM((npad, d), BF16), pltpu.VMEM((npad, 128), F32),
                        pltpu.VMEM((tb, npad), BF16), pltpu.SMEM((2 * N_GROUPS,), jnp.int32)],
        compiler_params=_cp(("parallel", "arbitrary"), MOE_VMEM_LIMIT), name="moe",
    )(h2, gate, w1, w3, w2, *extra_args)


def _tri(q, rev):
    r = lax.broadcasted_iota(jnp.int32, (q, q), 0)
    c = lax.broadcasted_iota(jnp.int32, (q, q), 1)
    return (c >= r) if rev else (c <= r)


def _ssd_dt_columns(nh):
    return [[d * nh + 4 * g + r for d in range(2) for r in range(4)] for g in range(nh // 4)]


def _ssd_kernel(xs_ref, b_ref, c_ref, dt_ref, cwx_ref, cbx_ref, cwb_ref, cbb_ref, cwc_ref, cbc_ref,
                dtb_ref, alog_ref, dsk_ref, o_ref, xc_s, bc_s, cc_s, dt_s, bct_s, *, nc):
    q = CHUNK
    first_lat = TM // q

    dtb = dtb_ref[0]

    def prep(i, _):
        t0 = pl.multiple_of(i * q, q)
        idx = pl.ds(t0, q)
        for src, cw, cb, dst in ((xs_ref, cwx_ref, cbx_ref, xc_s), (b_ref, cwb_ref, cbb_ref, bc_s),
                                 (c_ref, cwc_ref, cbc_ref, cc_s)):
            dst[idx, :] = _silu(_conv_tile(src, i, nc, cw[...], cb[...], q))
        dt_s[idx, :] = _softplus(dt_ref[0, idx, :] + dtb)
        o_ref[0, idx, :] = dsk_ref[...] * xc_s[idx, :]
        bct_s[i] = bc_s[idx, :].T.astype(BF16)
        return 0
    lax.fori_loop(0, nc, prep, 0)

    low_half = lax.broadcasted_iota(jnp.int32, (1, 128), 1) < 64

    def per_head(t, d):
        col = [jnp.broadcast_to(t[:, 4 * d + r:4 * d + r + 1], (t.shape[0], 128)) for r in range(4)]
        return jnp.concatenate([jnp.where(low_half, col[0], col[1]), jnp.where(low_half, col[2], col[3])], axis=1)

    a_lane = -jnp.exp(alog_ref[0])
    a_e = [per_head(a_lane, d) for d in (0, 1)]
    lane256 = lax.broadcasted_iota(jnp.int32, (q, 256), 1) // 64
    r8 = lax.broadcasted_iota(jnp.int32, (8, 256), 0)
    l8 = lax.broadcasted_iota(jnp.int32, (8, 256), 1)
    head_rows = (l8 == r8 * 64).astype(F32)

    def chain(d, j, state_in, state_out):
        rev = d == 1
        causal = _tri(q, rev)
        tri = causal.astype(F32)
        last = 0 if rev else q - 1
        if rev:
            ci = jnp.where(j < first_lat, first_lat - 1 - j, nc - 1 - (j - first_lat))
        else:
            ci = j
        idx = pl.ds(pl.multiple_of(ci * q, q), q)
        dt_e = per_head(dt_s[idx, :], d)
        xc = xc_s[idx, :]
        bc = bc_s[idx, :]
        cc = cc_s[idx, :]
        cb = _dot_nt(cc, bc)
        yield
        cum_e = _dot_sel(tri, dt_e * a_e[d])
        dtx = xc * dt_e
        yield
        cum_rows = _dot_sel(head_rows, cum_e, _NT)
        tot_e = cum_e[last:last + 1, :]
        states = _dot(bct_s[ci], dtx * jnp.exp(tot_e - cum_e))
        yield
        y = None
        for r in range(4):
            seg = cum_e[:, r * 64:r * 64 + 1] - cum_rows[r:r + 1, :]
            m = cb * jnp.exp(jnp.where(causal, seg, NEG))
            yr = _dot(m, jnp.where(lane256 == r, dtx, 0.0))
            y = yr if y is None else y + yr
            yield
        h = state_in[0]
        o_ref[0, idx, :] += y + jnp.exp(cum_e) * _dot(cc, h)
        state_out[0] = jnp.exp(tot_e) * h + states

    def step(positions, hs):
        cells = [[[hs[d]]] + [[None] for _ in positions] for d in (0, 1)]
        _interleave([chain(d, p, cells[d][k], cells[d][k + 1]) for k, p in enumerate(positions) for d in (0, 1)])
        return cells[0][-1][0], cells[1][-1][0]

    h0 = jnp.zeros((128, 256), F32)
    hs = lax.fori_loop(0, nc // 2, lambda j, hs: step((2 * j, 2 * j + 1), hs), (h0, h0))
    if nc % 2:
        step((nc - 1,), hs)


def _ssd(proj, lanes, conv_w, conv_b, dt_bias, a_log, d_skip, cols):
    bsz, L, _ = proj.shape
    nc = L // CHUNK
    cx, cbm, ccm, cdt = cols
    nh = d_skip.shape[0]
    gcols = np.asarray(_ssd_dt_columns(nh))
    dtb = jnp.zeros((gcols.shape[0], 1, 128), F32).at[:, 0, :gcols.shape[1]].set(dt_bias.reshape(-1)[gcols])
    alog = jnp.zeros((gcols.shape[0], 1, 128), F32).at[:, 0, :gcols.shape[1]].set(a_log.reshape(-1)[gcols])
    dsk = jnp.repeat(d_skip.astype(F32), 64).reshape(1, nh * 64)
    cw_specs = []
    for width, off in ((256, 0), (128, 512), (128, 768)):
        cw_specs += [pl.BlockSpec((CONV_W, width), lambda b, g, off=off, width=width: (0, off // width + g)),
                     pl.BlockSpec((1, width), lambda b, g, off=off, width=width: (0, off // width + g))]
    return pl.pallas_call(
        functools.partial(_ssd_kernel, nc=nc),
        grid=(bsz, 2),
        in_specs=[pl.BlockSpec((1, L, 256), lambda b, g: (b, 0, cx // 256 + g)),
                  pl.BlockSpec((1, L, 128), lambda b, g: (b, 0, cbm // 128 + g)),
                  pl.BlockSpec((1, L, 128), lambda b, g: (b, 0, ccm // 128 + g)),
                  pl.BlockSpec((1, L, 128), lambda b, g: (b, 0, cdt // 128 + g))] + cw_specs + [
                  pl.BlockSpec((1, 1, 128), lambda b, g: (g, 0, 0)),
                  pl.BlockSpec((1, 1, 128), lambda b, g: (g, 0, 0)),
                  pl.BlockSpec((1, 256), lambda b, g: (0, g))],
        out_specs=pl.BlockSpec((1, L, 256), lambda b, g: (b, 0, g)),
        out_shape=jax.ShapeDtypeStruct((bsz, L, nh * 64), F32),
        scratch_shapes=[pltpu.VMEM((L, 256), F32), pltpu.VMEM((L, 128), F32), pltpu.VMEM((L, 128), F32),
                        pltpu.VMEM((L, 128), F32), pltpu.VMEM((nc, 128, CHUNK), BF16)],
        compiler_params=_cp(("parallel", "parallel")), name="ssd",
    )(proj, proj, proj, lanes, conv_w, conv_b.reshape(1, -1), conv_w, conv_b.reshape(1, -1),
      conv_w, conv_b.reshape(1, -1), dtb, alog, dsk)


ML_HP = 2


def _ml_gate_lanes(hh, d):
    li = (hh * 2 + d) * 2
    return li, li + 1


def _ml_gate_columns(nh):
    cols = []
    for hp in range(nh // ML_HP):
        blk = [None] * (4 * ML_HP)
        for hh in range(ML_HP):
            for d in range(2):
                for t, lane in enumerate(_ml_gate_lanes(hh, d)):
                    blk[lane] = (d * 2 + t) * nh + hp * ML_HP + hh
        cols.append(blk)
    return cols


def _mlstm_kernel(q_ref, k_ref, v_ref, g_ref, cwq_ref, cbq_ref, cwk_ref, cbk_ref, gb_ref, o_ref,
                  qc_s, kc_s, kct_s, *, nc, dh):
    q = CHUNK
    first_lat = TM // q

    def prep(i, _):
        t0 = pl.multiple_of(i * q, q)
        idx = pl.ds(t0, q)
        for src, cw, cb, dst, mul in ((q_ref, cwq_ref, cbq_ref, qc_s, 1.0), (k_ref, cwk_ref, cbk_ref, kc_s, dh ** -0.5)):
            dst[idx, :] = _silu(_conv_tile(src, i, nc, cw[...], cb[...], q)) * mul
        for hh in range(ML_HP):
            kct_s[i, hh * dh:(hh + 1) * dh, :] = kc_s[idx, hh * dh:(hh + 1) * dh].T.astype(BF16)
        return 0
    lax.fori_loop(0, nc, prep, 0)

    gb = gb_ref[0]
    lane = lax.broadcasted_iota(jnp.int32, (q, 128), 1)
    eye8 = (lax.broadcasted_iota(jnp.int32, (8, 128), 0) == lax.broadcasted_iota(jnp.int32, (8, 128), 1)).astype(F32)
    o_ref[...] = jnp.zeros_like(o_ref)
    heads = range(ML_HP)
    assert q % dh == 0
    wide = lambda col: jnp.concatenate([col] * (q // dh), axis=1)

    def direction(d, j, states):
        rev = d == 1
        causal = _tri(q, rev)
        last = 0 if rev else q - 1
        if rev:
            ci = jnp.where(j < first_lat, first_lat - 1 - j, nc - 1 - (j - first_lat))
        else:
            ci = j
        idx = pl.ds(pl.multiple_of(ci * q, q), q)
        g = g_ref[0, idx, :] + gb
        logf = jnp.minimum(g, 0.0) - jnp.log(1.0 + jnp.exp(-jnp.abs(g)))
        is_f = functools.reduce(jnp.logical_or, [lane == _ml_gate_lanes(hh, d)[1] for hh in heads])
        cum = _dot_sel(jnp.where(causal, 1.0, 0.0), jnp.where(is_f, logf, 0.0))
        qc = [qc_s[idx, hh * dh:(hh + 1) * dh] for hh in heads]
        kc = [kc_s[idx, hh * dh:(hh + 1) * dh] for hh in heads]
        vc = [v_ref[0, idx, hh * dh:(hh + 1) * dh] for hh in heads]
        kct = [kct_s[ci, hh * dh:(hh + 1) * dh, :] for hh in heads]
        qk = [_dot(qc[hh], kct[hh]) for hh in heads]
        inter_c = [_dot(qc[hh], states[hh][0]) for hh in heads]
        yield
        rows = _dot_sel(eye8, jnp.where(is_f, cum, g), _NT)
        yield
        w, w_s, g_in, m_t, keep, m_new, wv, upd = [], [], [], [], [], [], [], []
        for hh in heads:
            li, lf = _ml_gate_lanes(hh, d)
            m_prev = states[hh][2]
            i_col = jnp.broadcast_to(g[:, li:li + 1], (q, dh))
            b_col = jnp.broadcast_to(cum[:, lf:lf + 1], (q, dh))
            i_row, b_row = rows[li:li + 1, :], rows[lf:lf + 1, :]
            dlog = jnp.where(causal, wide(b_col) - b_row + i_row, NEG)
            inter = b_col + m_prev
            m_t.append(jnp.maximum(jnp.max(dlog, axis=-1, keepdims=True), inter))
            w.append(jnp.exp(dlog - wide(m_t[hh])) * qk[hh])
            g_in.append(jnp.exp(inter - m_t[hh]))
            b_end = b_col[last:last + 1, :]
            g_s = b_end - b_col + i_col
            m_new.append(jnp.maximum(jnp.max(g_s, axis=0, keepdims=True), b_end + m_prev))
            w_s.append(jnp.exp(g_s - m_new[hh]))
            keep.append(jnp.exp(b_end + m_prev - m_new[hh]))
            wv.append(_dot(w[hh], vc[hh]))
            upd.append(_dot(kct[hh], w_s[hh] * vc[hh]))
        yield
        new_states = []
        for hh in heads:
            c_prev, n_prev, _ = states[hh]
            num = wv[hh] + g_in[hh] * inter_c[hh]
            den = (jnp.sum(w[hh], axis=-1, keepdims=True)
                   + g_in[hh] * jnp.sum(qc[hh] * n_prev, axis=-1, keepdims=True))
            o_ref[0, idx, hh * dh:(hh + 1) * dh] += num / jnp.maximum(jnp.abs(den), jnp.exp(-m_t[hh]))
            new_states.append((keep[hh] * c_prev + upd[hh],
                               keep[hh] * n_prev + jnp.sum(w_s[hh] * kc[hh], axis=0, keepdims=True), m_new[hh]))
        return tuple(new_states)

    def chunk(j, states):
        return tuple(_interleave([direction(d, j, states[d]) for d in (0, 1)]))
    init = (jnp.zeros((dh, dh), F32), jnp.zeros((1, dh), F32), jnp.full((1, dh), NEG, F32))
    lax.fori_loop(0, nc, chunk, tuple(tuple(init for _ in heads) for _ in (0, 1)))


def _mlstm(proj, lanes, conv_w, conv_b, gate_b, cols):
    bsz, L, _ = proj.shape
    nc = L // CHUNK
    cq, ck, cv, cg = cols
    nh = gate_b.shape[-1]
    dh = conv_w.shape[1] // (2 * nh)
    bw = ML_HP * dh
    gcols = np.asarray(_ml_gate_columns(nh))
    gbl = jnp.zeros((nh // ML_HP, 1, 128), F32).at[:, 0, :gcols.shape[1]].set(gate_b.reshape(-1)[gcols])
    return pl.pallas_call(
        functools.partial(_mlstm_kernel, nc=nc, dh=dh),
        grid=(bsz, nh // ML_HP),
        in_specs=[pl.BlockSpec((1, L, bw), lambda b, h: (b, 0, cq // bw + h)),
                  pl.BlockSpec((1, L, bw), lambda b, h: (b, 0, ck // bw + h)),
                  pl.BlockSpec((1, L, bw), lambda b, h: (b, 0, cv // bw + h)),
                  pl.BlockSpec((1, L, 128), lambda b, h: (b, 0, cg // 128 + h)),
                  pl.BlockSpec((CONV_W, bw), lambda b, h: (0, h)),
                  pl.BlockSpec((1, bw), lambda b, h: (0, h)),
                  pl.BlockSpec((CONV_W, bw), lambda b, h: (0, nh // ML_HP + h)),
                  pl.BlockSpec((1, bw), lambda b, h: (0, nh // ML_HP + h)),
                  pl.BlockSpec((1, 1, 128), lambda b, h: (h, 0, 0))],
        out_specs=pl.BlockSpec((1, L, bw), lambda b, h: (b, 0, h)),
        out_shape=jax.ShapeDtypeStruct((bsz, L, nh * dh), F32),
        scratch_shapes=[pltpu.VMEM((L, bw), F32), pltpu.VMEM((L, bw), F32), pltpu.VMEM((nc, bw, CHUNK), BF16)],
        compiler_params=_cp(("parallel", "parallel")), name="mlstm",
    )(proj, proj, proj, lanes, conv_w, conv_b.reshape(1, -1), conv_w, conv_b.reshape(1, -1), gbl)


def kernel(x, c, ctx, c_ctx, mod_w, mod_b, norm1_g, norm2_g, even_w_in, even_w_out, lru_conv_w, lru_conv_b, lru_wa, lru_ba, lru_wx, lru_bx, lru_lam, na_q_g, na_k_g, na_rpb, odd_w_in, odd_w_out, ssd_conv_w, ssd_conv_b, ssd_dt_bias, ssd_a_log, ssd_d, ssd_norm_g, ml_conv_w, ml_conv_b, ml_gate_b, ml_norm_g, moe_router_g, moe_router_e, moe_w1, moe_w3, moe_w2):
    bsz, S, d = x.shape
    lc = ctx.shape[1]
    assert lc == TM and S % TM == 0 and mod_w.shape[0] == 2
    nt = (lc + S) // TM
    L = lc + S

    cc = jnp.zeros((8, d), F32).at[0].set(c_ctx).at[1:1 + bsz].set(c)
    mod_all = _modulation(cc, mod_w, mod_b)

    def mod_for(l):
        m = mod_all[l].reshape(8, 6, d)
        return jnp.stack([jnp.broadcast_to(m[0], (bsz, 6, d)), m[1:1 + bsz]], axis=1)

    def router_w(l):
        w = jnp.zeros((d, 128), F32).at[:, :EXPERT_LANE0].set(moe_router_g[l]) \
            .at[:, EXPERT_LANE0:EXPERT_LANE0 + N_EXPERTS].set(moe_router_e[l])
        hi = w.astype(BF16)
        return jnp.concatenate([hi, (w - hi.astype(F32)).astype(BF16)], axis=1)

    xx = (ctx, x)

    mod0 = mod_for(0)
    (proj,) = _inproj(xx, mod0, norm1_g[0], even_w_in[0].astype(BF16))
    lw = lru_conv_w.shape[-1]
    r = _lru(proj, lru_conv_w[0], lru_conv_b[0], lru_wa[0], lru_ba[0], lru_wx[0], lru_bx[0], lru_lam[0])
    a = _na(proj, na_q_g[0], na_k_g[0], na_rpb[0], col0=2 * lw)
    x1, h2, gate = _outproj([(r, 0, lw), (proj, 1, lw), (a, 0, lw)], xx, even_w_out[0].astype(BF16), mod0,
                            norm2_g[0], router_w(0), even=True, tile0=0, ntiles=nt)
    ew = (moe_w1.astype(BF16), moe_w3.astype(BF16), moe_w2.astype(BF16))
    moe0 = _moe(h2.reshape(bsz * L, d), gate.reshape(bsz * L, 128), 0, *ew).reshape(bsz, L, d)

    mod1 = mod_for(1)
    sw = ssd_d.shape[-1] * 64
    xbc = ssd_conv_w.shape[-1]
    mw = ml_norm_g.shape[-1]
    w = odd_w_in[0]
    o = np.cumsum([0, sw, xbc, 2 * ssd_d.shape[-1], mw, mw, mw, mw])
    def lane_blocks(col0, groups):
        return [jnp.concatenate([w[:, col0 + np.asarray(blk)], jnp.zeros((d, 128 - len(blk)), F32)], axis=1)
                for blk in groups]
    dt_blks = lane_blocks(o[2], _ssd_dt_columns(ssd_d.shape[-1]))
    gate_blks = lane_blocks(o[7], _ml_gate_columns(ml_gate_b.shape[-1]))
    w_odd = jnp.concatenate([w[:, :o[2]], w[:, o[3]:o[7]]] + dt_blks + gate_blks, axis=1).astype(BF16)
    cz, cxs = 0, sw
    cB, cC = cxs + sw, cxs + sw + (xbc - sw) // 2
    cq = sw + xbc
    ck, cv, co, csm = cq + mw, cq + 2 * mw, cq + 3 * mw, cq + 4 * mw
    x0, proj1, lanes1 = _inproj(x1, mod1, norm1_g[1], w_odd, prev=(moe0, mod0), n_f32=w_odd.shape[1] - csm)
    ys = _ssd(proj1, lanes1, ssd_conv_w[0], ssd_conv_b[0], ssd_dt_bias[0], ssd_a_log[0], ssd_d[0], (cxs, cB, cC, 0))
    hm = _mlstm(proj1, lanes1, ml_conv_w[0], ml_conv_b[0], ml_gate_b[0], (cq, ck, cv, 128 * len(dt_blks)))
    x2, h2b, gate1 = _outproj([(ys, 0, sw), (proj1, cz // sw, sw), (hm, 0, mw), (proj1, co // mw, mw),
                               (ssd_norm_g[0].reshape(1, sw), 0, sw), (ml_norm_g[0].reshape(1, mw), 0, mw)],
                              x0, odd_w_out[0].astype(BF16), mod1, norm2_g[1], router_w(1),
                              even=False, tile0=1, ntiles=nt - 1)
    out = _moe(h2b.reshape(bsz * S, d), gate1.reshape(bsz * S, 128), 1, *ew, resid=(x2.reshape(bsz * S, d), mod1))
    return out.reshape(bsz, S, d)
```

```python
import functools
import math

import jax
import jax.numpy as jnp
import numpy as np
from jax import lax
from jax.experimental import pallas as pl
from jax.experimental.pallas import tpu as pltpu

F32 = jnp.float32
BF16 = jnp.bfloat16
HI = lax.Precision.HIGHEST

EPS = 1e-6
NEG = -1e30
GRID_W = 64
CONV_W = 4
LRU_C = 8.0
TM = 256
CHUNK = 256
NA_RQ = 4
NA_RK = 12
N_EXPERTS = 16
EXPERT_LANE0 = 4
VMEM_LIMIT = 56 * 1024 * 1024


def _cp(sem, vmem=VMEM_LIMIT):
    return pltpu.CompilerParams(dimension_semantics=sem, vmem_limit_bytes=vmem)


def _sigmoid(x):
    return jax.nn.sigmoid(x)


def _silu(x):
    return x * jax.nn.sigmoid(x)


def _softplus(x):
    return jnp.maximum(x, 0.0) + jnp.log(1.0 + jnp.exp(-jnp.abs(x)))


def _gelu_tanh(x):
    return 0.5 * x * (1.0 + jnp.tanh(math.sqrt(2.0 / math.pi) * (x + 0.044715 * (x * x * x))))


def _rms(x, axis=-1):
    return x * lax.rsqrt(jnp.mean(x * x, axis=axis, keepdims=True) + EPS)


def _dot(a, b):
    return jnp.dot(a.astype(BF16), b.astype(BF16), preferred_element_type=F32)


def _dot_hi(a, b):
    return jnp.dot(a, b, precision=HI, preferred_element_type=F32)


def _split3(x):
    x1 = x.astype(BF16)
    r = x - x1.astype(F32)
    x2 = r.astype(BF16)
    x3 = (r - x2.astype(F32)).astype(BF16)
    return x1, x2, x3


def _dot_sel(sel, x, dims=(((1,), (0,)), ((), ()))):
    s = sel.astype(BF16)
    x1, x2, x3 = _split3(x)
    d = lambda xi: lax.dot_general(s, xi, dims, preferred_element_type=F32)
    return (d(x3) + d(x2)) + d(x1)


def _dot_sel_r(x, sel):
    s = sel.astype(BF16)
    x1, x2, x3 = _split3(x)
    d = lambda xi: jnp.dot(xi, s, preferred_element_type=F32)
    return (d(x3) + d(x2)) + d(x1)


_NT = (((1,), (1,)), ((), ()))


def _interleave(gens):
    results = [None] * len(gens)
    live = list(range(len(gens)))
    while live:
        for i in list(live):
            try:
                next(gens[i])
            except StopIteration as stop:
                results[i] = stop.value
                live.remove(i)
    return results


def _dot_nt(a, b):
    return lax.dot_general(a.astype(BF16), b.astype(BF16), (((1,), (1,)), ((), ())),
                           preferred_element_type=F32)


def _mod_kernel(c_ref, w_ref, b_ref, o_ref):
    c = c_ref[...]
    o_ref[0] = _dot_hi(_silu(c), w_ref[0]) + b_ref[0]


def _modulation(cc, mod_w, mod_b):
    depth, d, n = mod_w.shape
    tn = 1536
    return pl.pallas_call(
        _mod_kernel,
        grid=(depth, n // tn),
        in_specs=[pl.BlockSpec((8, d), lambda l, j: (0, 0)),
                  pl.BlockSpec((1, d, tn), lambda l, j: (l, 0, j)),
                  pl.BlockSpec((1, 1, tn), lambda l, j: (l, 0, j))],
        out_specs=pl.BlockSpec((1, 8, tn), lambda l, j: (l, 0, j)),
        out_shape=jax.ShapeDtypeStruct((depth, 8, n), F32),
        compiler_params=_cp(("arbitrary", "arbitrary")),
        name="adaln_mod",
    )(cc, mod_w, mod_b.reshape(depth, 1, n))


def _inproj_kernel(*refs, fuse_prev, n_f32):
    refs = list(refs)
    s_ref = refs.pop() if n_f32 else None
    if fuse_prev:
        x_ref, mo_ref, pmod_ref, mod_ref, g_ref, w_ref, xo_ref, p_ref = refs
        x = x_ref[0] + pmod_ref[0, 0][5:6, :] * mo_ref[0]
        xo_ref[0] = x
    else:
        c_ref, x_ref, mod_ref, g_ref, w_ref, p_ref = refs
        x = _pick_segment(c_ref, x_ref)
    mod = mod_ref[0, 0]
    h = _rms(x) * g_ref[...] * (1.0 + mod[1:2, :]) + mod[0:1, :]
    p = _dot(h, w_ref[...])
    n_main = p.shape[-1] - n_f32
    p_ref[0] = p[:, :n_main].astype(BF16)
    if n_f32:
        s_ref[0] = p[:, n_main:]


def _seg_map(b, i):
    return (b, jnp.minimum(i, 1), 0, 0)


def _segment_specs(d):
    return [pl.BlockSpec((1, TM, d), lambda b, i: (b, 0, 0)),
            pl.BlockSpec((1, TM, d), lambda b, i: (b, jnp.maximum(i - 1, 0), 0))]


def _pick_segment(c_ref, x_ref):
    return jnp.where(pl.program_id(1) == 0, c_ref[0], x_ref[0])


def _inproj(x, mod, g, w, prev=None, n_f32=0):
    if prev is None:
        ctx, lat = x
        bsz, S, d = lat.shape
        L = TM + S
    else:
        bsz, L, d = x.shape
    n = w.shape[1]
    nt = L // TM
    tok = pl.BlockSpec((1, TM, d), lambda b, i: (b, i, 0))
    modspec = pl.BlockSpec((1, 1, 6, d), _seg_map)
    tail = [modspec, pl.BlockSpec((1, d), lambda b, i: (0, 0)), pl.BlockSpec((d, n), lambda b, i: (0, 0))]
    pspecs = [pl.BlockSpec((1, TM, n - n_f32), lambda b, i: (b, i, 0))]
    pshapes = [jax.ShapeDtypeStruct((bsz, L, n - n_f32), BF16)]
    if n_f32:
        pspecs.append(pl.BlockSpec((1, TM, n_f32), lambda b, i: (b, i, 0)))
        pshapes.append(jax.ShapeDtypeStruct((bsz, L, n_f32), F32))
    if prev is None:
        return pl.pallas_call(
            functools.partial(_inproj_kernel, fuse_prev=False, n_f32=n_f32),
            grid=(bsz, nt), in_specs=_segment_specs(d) + tail, out_specs=pspecs, out_shape=pshapes,
            compiler_params=_cp(("parallel", "arbitrary")), name="inproj",
        )(ctx, lat, mod, g.reshape(1, d), w)
    moe_out, pmod = prev
    return pl.pallas_call(
        functools.partial(_inproj_kernel, fuse_prev=True, n_f32=n_f32),
        grid=(bsz, nt), in_specs=[tok, tok, modspec] + tail,
        out_specs=[tok] + pspecs, out_shape=[jax.ShapeDtypeStruct(x.shape, F32)] + pshapes,
        compiler_params=_cp(("parallel", "parallel")), name="inproj_res",
    )(x, moe_out, pmod, mod, g.reshape(1, d), w)


def _conv_tile(ref, i, nt, cw, cb, width=TM):
    L = nt * width
    t0 = pl.multiple_of(i * width, width)
    cur = ref[0, pl.ds(t0, width), :].astype(F32)
    prev = ref[0, pl.ds(pl.multiple_of(jnp.maximum(t0 - 16, 0), 16), 16), :].astype(F32)[8:16]
    nxt = ref[0, pl.ds(pl.multiple_of(jnp.minimum(t0 + width, L - 16), 16), 16), :].astype(F32)[0:8]
    return _conv_vals(cur, prev, nxt, i, nt, cw, cb, width, first_lat=TM // width)


def _conv_vals(cur, prev, nxt, i, nt, cw, cb, width, first_lat):
    prev = jnp.where((i != 0) & (i != first_lat), prev, 0.0)
    nxt = jnp.where((i != first_lat - 1) & (i != nt - 1), nxt, 0.0)
    cat = jnp.concatenate([prev, cur, nxt], axis=0)
    return (cw[0:1] * cat[6:6 + width] + cw[1:2] * cat[7:7 + width] + cw[2:3] * cur
            + cw[3:4] * cat[9:9 + width] + cb)


def _lru_kernel(ux_ref, cw_ref, cb_ref, gw_ref, gb_ref, lam_ref, o_ref, *, nt):
    cw = cw_ref[...]
    cb = cb_ref[...]
    lam = lam_ref[0]
    sp = _softplus(-lam)
    row = lax.broadcasted_iota(jnp.int32, (TM, 128), 0) & 7

    def gates(i, d):
        xl = _conv_tile(ux_ref, i, nt, cw, cb)
        g = _dot(xl, gw_ref[0, d]) + gb_ref[0, d]
        r = _sigmoid(g[:, :128])
        ig = _sigmoid(g[:, 128:])
        log_a = -LRU_C * r * sp[d:d + 1]
        a = jnp.exp(log_a)
        u = jnp.sqrt(1.0 - a * a) * (ig * xl)
        return a, u

    def scan_tile(i, d, carry, accumulate):
        a, u = gates(i, d)
        rev = d == 1
        for k in (1, 2, 4):
            sh = 8 - k if rev else k
            ok = (row < 8 - k) if rev else (row >= k)
            ash = pltpu.roll(a.reshape(TM // 8, 8, 128), sh, 1).reshape(TM, 128)
            ush = pltpu.roll(u.reshape(TM // 8, 8, 128), sh, 1).reshape(TM, 128)
            u = jnp.where(ok, u + a * ush, u)
            a = jnp.where(ok, a * ash, a)
        t0 = i * TM
        groups = range(TM // 8)
        for s in (reversed(groups) if rev else groups):
            h = u[s * 8:(s + 1) * 8] + a[s * 8:(s + 1) * 8] * carry
            carry = h[0:1] if rev else h[7:8]
            idx = pl.ds(pl.multiple_of(t0 + s * 8, 8), 8)
            if accumulate:
                o_ref[0, idx, :] += h
            else:
                o_ref[0, idx, :] = h
        return carry

    zero = jnp.zeros((1, 128), F32)
    lax.fori_loop(0, nt, lambda i, c: scan_tile(i, 0, c, False), zero)
    lax.fori_loop(0, nt, lambda j, c: scan_tile(jnp.where(j == 0, 0, nt - j), 1, c, True), zero)


def _lru(proj, conv_w, conv_b, wa, ba, wx, bx, lam):
    bsz, L, _ = proj.shape
    nt = L // TM
    width = conv_w.shape[1]
    ng = width // 128

    def blockdiag(w):
        w = w.reshape(2, ng, 2, 64, 64)
        z = jnp.zeros_like(w[:, :, 0])
        top = jnp.concatenate([w[:, :, 0], z], axis=-1)
        bot = jnp.concatenate([z, w[:, :, 1]], axis=-1)
        return jnp.concatenate([top, bot], axis=-2)
    gw = jnp.concatenate([blockdiag(wa), blockdiag(wx)], axis=-1).transpose(1, 0, 2, 3).astype(BF16)
    gb = jnp.concatenate([ba.reshape(2, ng, 1, 128), bx.reshape(2, ng, 1, 128)], axis=-1).transpose(1, 0, 2, 3)
    lam_g = lam.reshape(2, ng, 128).transpose(1, 0, 2)
    return pl.pallas_call(
        functools.partial(_lru_kernel, nt=nt),
        grid=(bsz, ng),
        in_specs=[pl.BlockSpec((1, L, 128), lambda b, c: (b, 0, c)),
                  pl.BlockSpec((CONV_W, 128), lambda b, c: (0, c)),
                  pl.BlockSpec((1, 128), lambda b, c: (0, c)),
                  pl.BlockSpec((1, 2, 128, 256), lambda b, c: (c, 0, 0, 0)),
                  pl.BlockSpec((1, 2, 1, 256), lambda b, c: (c, 0, 0, 0)),
                  pl.BlockSpec((1, 2, 128), lambda b, c: (c, 0, 0))],
        out_specs=pl.BlockSpec((1, L, 128), lambda b, c: (b, 0, c)),
        out_shape=jax.ShapeDtypeStruct((bsz, L, width), F32),
        compiler_params=_cp(("parallel", "parallel")), name="rglru",
    )(proj, conv_w, conv_b.reshape(1, width), gw, gb, lam_g)


def _na_bias_table(rpb, rows):
    nh = rpb.shape[0]
    win_c = (rpb.shape[2] + 1) // 2
    qc = np.arange(GRID_W)[:, None]
    kc = np.arange(GRID_W)[None, :]
    cstart = np.clip(qc - win_c // 2, 0, GRID_W - win_c)
    col_ok = (kc >= cstart) & (kc < cstart + win_c)
    dcol = np.clip(kc - qc + (win_c - 1), 0, 2 * win_c - 2)
    oc = (np.arange(2 * win_c - 1)[:, None, None] == dcol[None]).astype(np.float32)
    ct = jnp.where(jnp.asarray(col_ok), jnp.einsum('hrc,cqk->hrqk', rpb.astype(F32), jnp.asarray(oc), precision=HI),
                   NEG)
    return pl.pallas_call(
        functools.partial(_na_bias_kernel, rows=rows),
        grid=(nh,),
        in_specs=[pl.BlockSpec((1,) + ct.shape[1:], lambda h: (h, 0, 0, 0))],
        out_specs=pl.BlockSpec((3, 1, TM, NA_RK * GRID_W), lambda h: (0, h, 0, 0)),
        out_shape=jax.ShapeDtypeStruct((3, nh, TM, NA_RK * GRID_W), F32),
        compiler_params=_cp(("parallel",)), name="nbr_bias",
    )(ct)


def _na_bias_kernel(ct_ref, o_ref, *, rows):
    win_r = (ct_ref.shape[1] + 1) // 2
    neg = jnp.full((GRID_W, GRID_W), NEG, F32)
    for p, (r0, w0) in enumerate(((0, 0), (2 * NA_RQ, NA_RQ), (rows - NA_RQ, rows - NA_RK))):
        for a in range(NA_RQ):
            r = r0 + a
            rstart = min(max(r - win_r // 2, 0), rows - win_r)
            tiles = [ct_ref[0, w0 + b - r + win_r - 1] if rstart <= w0 + b < rstart + win_r else neg
                     for b in range(NA_RK)]
            for bp in range(NA_RK // 2):
                o_ref[p, 0, a * GRID_W:(a + 1) * GRID_W, bp * 2 * GRID_W:(bp + 1) * 2 * GRID_W] = (
                    jnp.concatenate(tiles[2 * bp:2 * bp + 2], axis=1))


def _pair_rms(x, lo):
    x2 = x * x
    s0 = jnp.sum(jnp.where(lo, x2, 0.0), axis=-1, keepdims=True)
    s1 = jnp.sum(jnp.where(lo, 0.0, x2), axis=-1, keepdims=True)
    return x * lax.rsqrt(jnp.where(lo, s0, s1) * (2.0 / x.shape[-1]) + EPS)


def _na_kernel(q_ref, k_ref, v_ref, bias_ref, qg_ref, kg_ref, o_ref, kn_s, vb_s, *, rows, hd):
    j = pl.program_id(2)
    L = k_ref.shape[1]
    nb = L // TM
    nkeys = NA_RK * GRID_W
    lo = lax.broadcasted_iota(jnp.int32, (1, 2 * hd), 1) < hd

    @pl.when(j == 0)
    def _prep():
        def body(t, _):
            idx = pl.ds(pl.multiple_of(t * TM, TM), TM)
            kn_s[idx, :] = (_pair_rms(k_ref[0, idx, :].astype(F32), lo) * kg_ref[...]).astype(BF16)
            vb_s[idx, :] = v_ref[0, idx, :].astype(BF16)
            return 0
        lax.fori_loop(0, nb, body, 0)

    kctx = kn_s[0:TM, :]
    vctx = vb_s[0:TM, :]
    qscale = qg_ref[...] * hd ** -0.5

    def head(q, bias, kwin, vwin):
        s_c = _dot_nt(q, kctx)
        if kwin is not None:
            s_w = _dot_nt(q, kwin) + bias
        yield
        m = jnp.max(s_c, axis=-1, keepdims=True)
        if kwin is not None:
            m = jnp.maximum(m, jnp.max(s_w, axis=-1, keepdims=True))
            p_w = jnp.exp(s_w - m)
        p_c = jnp.exp(s_c - m)
        den = jnp.sum(p_c, axis=-1, keepdims=True)
        num = _dot(p_c, vctx)
        if kwin is not None:
            den = den + jnp.sum(p_w, axis=-1, keepdims=True)
            num = num + _dot(p_w, vwin)
        yield
        return num / den

    def tile_chains(slot, tile):
        qn = _pair_rms(q_ref[0, slot * TM:(slot + 1) * TM, :].astype(F32), lo) * qscale
        q_h = [jnp.where(lo, qn, 0.0).astype(BF16), jnp.where(lo, 0.0, qn).astype(BF16)]
        if tile is None:
            return [head(q_h[hh], None, None, None) for hh in range(2)]
        w0 = jnp.clip((tile - 1) * NA_RQ - NA_RQ, 0, rows - NA_RK)
        start = pl.multiple_of(TM + w0 * GRID_W, GRID_W)
        kwin = kn_s[pl.ds(start, nkeys), :]
        vwin = vb_s[pl.ds(start, nkeys), :]
        pat = jnp.where(tile <= 1, 0, jnp.where(tile >= nb - 1, 2, 1))
        return [head(q_h[hh], bias_ref[pat, hh], kwin, vwin) for hh in range(2)]

    def run(chains):
        outs = _interleave(chains)
        for slot in range(len(outs) // 2):
            o_ref[0, slot * TM:(slot + 1) * TM, :] = jnp.where(lo, outs[2 * slot], outs[2 * slot + 1]).astype(BF16)

    last = (nb - 1) // 2

    @pl.when(j == 0)
    def _():
        run(tile_chains(0, None) + tile_chains(1, 1))

    @pl.when((j > 0) & (j < last))
    def _():
        run(tile_chains(0, 2 * j) + tile_chains(1, 2 * j + 1))

    @pl.when(j == last)
    def _():
        run(tile_chains(0, 2 * j))


def _na(proj, q_g, k_g, rpb, col0):
    bsz, L, _ = proj.shape
    nh = rpb.shape[0]
    hd = q_g.shape[0]
    width = nh * hd
    rows = (L - TM) // GRID_W
    nb = L // TM
    bias = _na_bias_table(rpb, rows)
    qb, kb, vb = col0 // 128, (col0 + width) // 128, (col0 + 2 * width) // 128

    assert nb % 2 == 1

    return pl.pallas_call(
        functools.partial(_na_kernel, rows=rows, hd=hd),
        grid=(width // 128, bsz, (nb + 1) // 2),
        in_specs=[pl.BlockSpec((1, 2 * TM, 128), lambda h, b, j: (b, j, qb + h)),
                  pl.BlockSpec((1, L, 128), lambda h, b, j: (b, 0, kb + h)),
                  pl.BlockSpec((1, L, 128), lambda h, b, j: (b, 0, vb + h)),
                  pl.BlockSpec((3, 2, TM, NA_RK * GRID_W), lambda h, b, j: (0, h, 0, 0)),
                  pl.BlockSpec((1, 2 * hd), lambda h, b, j: (0, 0)),
                  pl.BlockSpec((1, 2 * hd), lambda h, b, j: (0, 0))],
        out_specs=pl.BlockSpec((1, 2 * TM, 128), lambda h, b, j: (b, j, h)),
        out_shape=jax.ShapeDtypeStruct((bsz, L, width), BF16),
        scratch_shapes=[pltpu.VMEM((L, 128), BF16), pltpu.VMEM((L, 128), BF16)],
        compiler_params=_cp(("parallel", "parallel", "arbitrary")), name="nbr_attn",
    )(proj, proj, proj, bias, jnp.tile(q_g, 2).reshape(1, 2 * hd), jnp.tile(k_g, 2).reshape(1, 2 * hd))


def _route(lg):
    lane = lax.broadcasted_iota(jnp.int32, lg.shape, 1)
    lane_f = lane.astype(F32)
    is_g = lane < EXPERT_LANE0
    gl = jnp.where(is_g, lg, NEG)
    gmax = jnp.max(gl, axis=-1, keepdims=True)
    gsel = jnp.min(jnp.where(is_g & (gl == gmax), lane_f, 1e9), axis=-1, keepdims=True)
    g_w = 1.0 / jnp.sum(jnp.where(is_g, jnp.exp(gl - gmax), 0.0), axis=-1, keepdims=True)
    grp = ((lane - EXPERT_LANE0) >> 2).astype(F32)
    in_g = (lane >= EXPERT_LANE0) & (lane < EXPERT_LANE0 + N_EXPERTS) & (grp == gsel)
    el = jnp.where(in_g, lg, NEG)
    v1 = jnp.max(el, axis=-1, keepdims=True)
    i1 = jnp.min(jnp.where(in_g & (el == v1), lane_f, 1e9), axis=-1, keepdims=True)
    el2 = jnp.where(lane_f == i1, NEG, el)
    v2 = jnp.max(el2, axis=-1, keepdims=True)
    i2 = jnp.min(jnp.where(in_g & (lane_f != i1) & (el2 == v2), lane_f, 1e9), axis=-1, keepdims=True)
    t = jnp.exp(v2 - v1)
    w1 = g_w / (1.0 + t)
    w2 = g_w * t / (1.0 + t)
    return (jnp.where(lane_f == i1, w1, 0.0) + jnp.where(lane_f == i2, w2, 0.0)
            + jnp.where(lane_f == gsel, 1.0, 0.0))


def _outproj_kernel(*refs, even):
    if even:
        (r_ref, ug_ref, a_ref, c_ref, x_ref, w_ref, mod_ref, g2_ref, rw_ref, x1_ref, h2_ref, gate_ref) = refs
        is_ctx = pl.program_id(1) == 0
        x_rows = lambda rs: jnp.where(is_ctx, c_ref[0, rs, :], x_ref[0, rs, :])
    else:
        (ys_ref, z_ref, hm_ref, mo_ref, sg_ref, mg_ref, x_ref, w_ref, mod_ref, g2_ref, rw_ref,
         x1_ref, h2_ref, gate_ref) = refs
        x_rows = lambda rs: x_ref[0, rs, :]
    mod = mod_ref[0, 0]

    def part(rs):
        if even:
            y_in = jnp.concatenate([(r_ref[0, rs, :] * _gelu_tanh(ug_ref[0, rs, :].astype(F32))).astype(BF16),
                                    a_ref[0, rs, :]], axis=-1)
        else:
            ys = ys_ref[0, rs, :] * _silu(z_ref[0, rs, :].astype(F32))
            sg = sg_ref[...]
            mg = mg_ref[...]
            hm = hm_ref[0, rs, :]
            sig_o = _sigmoid(mo_ref[0, rs, :].astype(F32))
            gw = ys.shape[-1] // 2
            parts = [_rms(ys[:, g * gw:(g + 1) * gw]) * sg[:, g * gw:(g + 1) * gw] for g in range(2)]
            hw = 128
            parts += [_rms(hm[:, h * hw:(h + 1) * hw]) * mg[:, h * hw:(h + 1) * hw] * sig_o[:, h * hw:(h + 1) * hw]
                      for h in range(hm.shape[-1] // hw)]
            y_in = jnp.concatenate(parts, axis=-1)
        y = _dot(y_in, w_ref[...])
        yield
        x1 = x_rows(rs) + mod[2:3, :] * y
        x1_ref[0, rs, :] = x1
        h2 = _rms(x1) * g2_ref[...] * (1.0 + mod[4:5, :]) + mod[3:4, :]
        h2_ref[0, rs, :] = h2.astype(BF16)
        hi = h2.astype(BF16)
        lo = (h2 - hi.astype(F32)).astype(BF16)
        lg2 = jnp.dot(hi, rw_ref[...], preferred_element_type=F32)
        lg1 = jnp.dot(lo, rw_ref[:, 0:128], preferred_element_type=F32)
        yield
        gate_ref[0, rs, :] = _route(lg2[:, 0:128] + lg2[:, 128:256] + lg1)

    nparts = 2
    rows = TM // nparts
    _interleave([part(pl.ds(p * rows, rows)) for p in range(nparts)])


def _outproj(mix_inputs, x, w, mod, g2, rw, even, tile0, ntiles):
    xs = list(x) if even else [x]
    bsz, _, d = xs[-1].shape
    specs, args = [], []
    for arr, cb, wdt in mix_inputs:
        if arr.ndim == 3:
            specs.append(pl.BlockSpec((1, TM, wdt), lambda b, i, cb=cb: (b, i + tile0, cb)))
        else:
            specs.append(pl.BlockSpec((1, wdt), lambda b, i: (0, 0)))
        args.append(arr)
    specs += _segment_specs(d) if even else [pl.BlockSpec((1, TM, d), lambda b, i: (b, i + tile0, 0))]
    specs += [pl.BlockSpec(w.shape, lambda b, i: (0, 0)),
              pl.BlockSpec((1, 1, 6, d), lambda b, i: (b, jnp.minimum(i + tile0, 1), 0, 0)),
              pl.BlockSpec((1, d), lambda b, i: (0, 0)),
              pl.BlockSpec(rw.shape, lambda b, i: (0, 0))]
    args += xs + [w, mod, g2.reshape(1, d), rw]
    lo = ntiles * TM
    return pl.pallas_call(
        functools.partial(_outproj_kernel, even=even),
        grid=(bsz, ntiles), in_specs=specs,
        out_specs=[pl.BlockSpec((1, TM, d), lambda b, i: (b, i, 0)),
                   pl.BlockSpec((1, TM, d), lambda b, i: (b, i, 0)),
                   pl.BlockSpec((1, TM, 128), lambda b, i: (b, i, 0))],
        out_shape=[jax.ShapeDtypeStruct((bsz, lo, d), F32), jax.ShapeDtypeStruct((bsz, lo, d), BF16),
                   jax.ShapeDtypeStruct((bsz, lo, 128), F32)],
        compiler_params=_cp(("parallel", "parallel")), name="outproj_even" if even else "outproj_odd",
    )(*args)


MOE_CH = 128
GROUP_SIZE = 4
MOE_VMEM_LIMIT = 62 * 1024 * 1024
N_GROUPS = N_EXPERTS // GROUP_SIZE


def _moe_kernel(*refs, residual):
    if residual:
        x_ref, g_ref, w1_ref, w3_ref, w2_ref, x1_ref, mod_ref, o_ref, xs_s, ys_s, gs_s, pt_s, plan_s = refs
    else:
        x_ref, g_ref, w1_ref, w3_ref, w2_ref, o_ref, xs_s, ys_s, gs_s, pt_s, plan_s = refs
    grp = pl.program_id(1)
    tb = x_ref.shape[0]
    nch = xs_s.shape[0] // MOE_CH

    @pl.when(grp == 0)
    def _plan():
        g = g_ref[...]
        lane = lax.broadcasted_iota(jnp.int32, g.shape, 1)
        oh = jnp.where(lane < N_GROUPS, g, 0.0)
        earlier = (lax.broadcasted_iota(jnp.int32, (tb, tb), 0) > lax.broadcasted_iota(jnp.int32, (tb, tb), 1))
        rank = jnp.dot(jnp.where(earlier, 1.0, 0.0).astype(BF16), oh.astype(BF16),
                       preferred_element_type=F32)
        cnt = jnp.sum(oh, axis=0, keepdims=True)
        lane1 = lax.broadcasted_iota(jnp.int32, (1, 128), 1)
        off = jnp.int32(0)
        offv = jnp.zeros((1, 128), F32)
        for gi in range(N_GROUPS):
            n = jnp.sum(jnp.where(lane1 == gi, cnt, 0.0)).astype(jnp.int32)
            nchunks = (n + (MOE_CH - 1)) // MOE_CH
            plan_s[gi] = off // MOE_CH
            plan_s[N_GROUPS + gi] = nchunks
            offv = offv + jnp.where(lane1 == gi, off.astype(F32), 0.0)
            off = off + nchunks * MOE_CH
        pos_col = jnp.sum(oh * (rank + offv), axis=1, keepdims=True)
        posb = jnp.broadcast_to(pos_col, (tb, 128))
        pos_row = jnp.concatenate([posb[i * 128:(i + 1) * 128, :].T[0:1, :] for i in range(tb // 128)], axis=1)
        x = x_ref[...]
        g_hi = g.astype(BF16)
        g_lo = (g - g_hi.astype(F32)).astype(BF16)
        lane_c = lax.broadcasted_iota(jnp.int32, (tb, MOE_CH), 1).astype(F32)
        row_c = lax.broadcasted_iota(jnp.int32, (MOE_CH, tb), 0).astype(F32)
        used = off // MOE_CH

        def dispatch(c):
            sl = slice(c * MOE_CH, (c + 1) * MOE_CH)
            p = jnp.where(row_c + float(c * MOE_CH) == pos_row, 1.0, 0.0).astype(BF16)
            xs_s[sl, :] = jnp.dot(p, x, preferred_element_type=F32).astype(BF16)
            gs_s[sl, :] = (jnp.dot(p, g_hi, preferred_element_type=F32)
                           + jnp.dot(p, g_lo, preferred_element_type=F32))
        for c in range(nch):
            sl = slice(c * MOE_CH, (c + 1) * MOE_CH)
            pt_s[:, sl] = jnp.where(pos_col == lane_c + float(c * MOE_CH), 1.0, 0.0).astype(BF16)
            if c < tb // MOE_CH:
                dispatch(c)
            else:
                pl.when(c < used)(functools.partial(dispatch, c))
        ys_s[...] = jnp.zeros_like(ys_s)

    c0 = plan_s[grp]
    nchunks = plan_s[N_GROUPS + grp]

    def ffn(chunk0, nrows):
        rows = pl.ds(pl.multiple_of(chunk0 * MOE_CH, MOE_CH), nrows)
        xs = xs_s[rows, :]
        gs = gs_s[rows, :]
        lane_g = lax.broadcasted_iota(jnp.int32, (nrows, 128), 1)
        acc = None
        for k in range(GROUP_SIZE):
            a = jnp.dot(xs, w1_ref[0, k], preferred_element_type=F32)
            b = jnp.dot(xs, w3_ref[0, k], preferred_element_type=F32)
            y = _dot(_silu(a) * b, w2_ref[0, k])
            ge = jnp.sum(jnp.where(lane_g == grp * GROUP_SIZE + k + EXPERT_LANE0, gs, 0.0), axis=-1, keepdims=True)
            acc = ge * y if acc is None else acc + ge * y
        ys_s[rows, :] = acc.astype(BF16)

    wide = 4

    def full(p, _):
        ffn(c0 + wide * p, wide * MOE_CH)
        return 0
    lax.fori_loop(0, nchunks // wide, full, 0)
    rem = nchunks % wide
    for r in range(1, wide):
        pl.when(rem == r)(functools.partial(ffn, c0 + nchunks - r, r * MOE_CH))

    @pl.when(grp == N_GROUPS - 1)
    def _combine():
        out = jnp.dot(pt_s[...], ys_s[...], preferred_element_type=F32)
        if residual:
            out = x1_ref[...] + mod_ref[0, 0][5:6, :] * out
        o_ref[...] = out


def _moe(h2, gate, layer, w1, w3, w2, resid=None):
    t, d = h2.shape
    _, ne, _, ff = w1.shape
    tb = math.gcd(t, 1024)
    npad = tb + N_GROUPS * MOE_CH
    assert ne == N_EXPERTS
    extra_specs, extra_args = [], []
    if resid is not None:
        x1, mod = resid
        per_batch = t // mod.shape[0] // tb
        extra_specs = [pl.BlockSpec((tb, d), lambda i, e: (i, 0)),
                       pl.BlockSpec((1, 1, 6, d), lambda i, e: (i // per_batch, 1, 0, 0))]
        extra_args = [x1, mod]
    return pl.pallas_call(
        functools.partial(_moe_kernel, residual=resid is not None),
        grid=(t // tb, N_GROUPS),
        in_specs=[pl.BlockSpec((tb, d), lambda i, e: (i, 0)),
                  pl.BlockSpec((tb, 128), lambda i, e: (i, 0)),
                  pl.BlockSpec((1, GROUP_SIZE, d, ff), lambda i, e: (layer, e, 0, 0)),
                  pl.BlockSpec((1, GROUP_SIZE, d, ff), lambda i, e: (layer, e, 0, 0)),
                  pl.BlockSpec((1, GROUP_SIZE, ff, d), lambda i, e: (layer, e, 0, 0))] + extra_specs,
        out_specs=pl.BlockSpec((tb, d), lambda i, e: (i, 0)),
        out_shape=jax.ShapeDtypeStruct((t, d), F32),
        scratch_shapes=[pltpu.VMEM((npad, d), BF16), pltpu.VMEM((npad, d), BF16), pltpu.VMEM((npad, 128), F32),
                        pltpu.VMEM((tb, npad), BF16), pltpu.SMEM((2 * N_GROUPS,), jnp.int32)],
        compiler_params=_cp(("parallel", "arbitrary"), MOE_VMEM_LIMIT), name="moe",
    )(h2, gate, w1, w3, w2, *extra_args)


def _tri(q, rev):
    r = lax.broadcasted_iota(jnp.int32, (q, q), 0)
    c = lax.broadcasted_iota(jnp.int32, (q, q), 1)
    return (c >= r) if rev else (c <= r)


def _ssd_dt_columns(nh):
    return [[d * nh + 4 * g + r for d in range(2) for r in range(4)] for g in range(nh // 4)]


def _ssd_kernel(xs_ref, b_ref, c_ref, dt_ref, cwx_ref, cbx_ref, cwb_ref, cbb_ref, cwc_ref, cbc_ref,
                dtb_ref, alog_ref, dsk_ref, o_ref, xc_s, bc_s, cc_s, dt_s, bct_s, *, nc):
    q = CHUNK
    first_lat = TM // q

    dtb = dtb_ref[0]

    def prep(i, _):
        t0 = pl.multiple_of(i * q, q)
        idx = pl.ds(t0, q)
        for src, cw, cb, dst in ((xs_ref, cwx_ref, cbx_ref, xc_s), (b_ref, cwb_ref, cbb_ref, bc_s),
                                 (c_ref, cwc_ref, cbc_ref, cc_s)):
            dst[idx, :] = _silu(_conv_tile(src, i, nc, cw[...], cb[...], q))
        dt_s[idx, :] = _softplus(dt_ref[0, idx, :] + dtb)
        o_ref[0, idx, :] = dsk_ref[...] * xc_s[idx, :]
        bct_s[i] = bc_s[idx, :].T.astype(BF16)
        return 0
    lax.fori_loop(0, nc, prep, 0)

    low_half = lax.broadcasted_iota(jnp.int32, (1, 128), 1) < 64

    def per_head(t, d):
        col = [jnp.broadcast_to(t[:, 4 * d + r:4 * d + r + 1], (t.shape[0], 128)) for r in range(4)]
        return jnp.concatenate([jnp.where(low_half, col[0], col[1]), jnp.where(low_half, col[2], col[3])], axis=1)

    a_lane = -jnp.exp(alog_ref[0])
    a_e = [per_head(a_lane, d) for d in (0, 1)]
    lane256 = lax.broadcasted_iota(jnp.int32, (q, 256), 1) // 64
    r8 = lax.broadcasted_iota(jnp.int32, (8, 256), 0)
    l8 = lax.broadcasted_iota(jnp.int32, (8, 256), 1)
    head_rows = (l8 == r8 * 64).astype(F32)

    def chain(d, j, state_in, state_out):
        rev = d == 1
        causal = _tri(q, rev)
        tri = causal.astype(F32)
        last = 0 if rev else q - 1
        if rev:
            ci = jnp.where(j < first_lat, first_lat - 1 - j, nc - 1 - (j - first_lat))
        else:
            ci = j
        idx = pl.ds(pl.multiple_of(ci * q, q), q)
        dt_e = per_head(dt_s[idx, :], d)
        xc = xc_s[idx, :]
        bc = bc_s[idx, :]
        cc = cc_s[idx, :]
        cb = _dot_nt(cc, bc)
        yield
        cum_e = _dot_sel(tri, dt_e * a_e[d])
        dtx = xc * dt_e
        yield
        cum_rows = _dot_sel(head_rows, cum_e, _NT)
        tot_e = cum_e[last:last + 1, :]
        states = _dot(bct_s[ci], dtx * jnp.exp(tot_e - cum_e))
        yield
        y = None
        for r in range(4):
            seg = cum_e[:, r * 64:r * 64 + 1] - cum_rows[r:r + 1, :]
            m = cb * jnp.exp(jnp.where(causal, seg, NEG))
            yr = _dot(m, jnp.where(lane256 == r, dtx, 0.0))
            y = yr if y is None else y + yr
            yield
        h = state_in[0]
        o_ref[0, idx, :] += y + jnp.exp(cum_e) * _dot(cc, h)
        state_out[0] = jnp.exp(tot_e) * h + states

    def step(positions, hs):
        cells = [[[hs[d]]] + [[None] for _ in positions] for d in (0, 1)]
        _interleave([chain(d, p, cells[d][k], cells[d][k + 1]) for k, p in enumerate(positions) for d in (0, 1)])
        return cells[0][-1][0], cells[1][-1][0]

    h0 = jnp.zeros((128, 256), F32)
    hs = lax.fori_loop(0, nc // 2, lambda j, hs: step((2 * j, 2 * j + 1), hs), (h0, h0))
    if nc % 2:
        step((nc - 1,), hs)


def _ssd(proj, lanes, conv_w, conv_b, dt_bias, a_log, d_skip, cols):
    bsz, L, _ = proj.shape
    nc = L // CHUNK
    cx, cbm, ccm, cdt = cols
    nh = d_skip.shape[0]
    gcols = np.asarray(_ssd_dt_columns(nh))
    dtb = jnp.zeros((gcols.shape[0], 1, 128), F32).at[:, 0, :gcols.shape[1]].set(dt_bias.reshape(-1)[gcols])
    alog = jnp.zeros((gcols.shape[0], 1, 128), F32).at[:, 0, :gcols.shape[1]].set(a_log.reshape(-1)[gcols])
    dsk = jnp.repeat(d_skip.astype(F32), 64).reshape(1, nh * 64)
    cw_specs = []
    for width, off in ((256, 0), (128, 512), (128, 768)):
        cw_specs += [pl.BlockSpec((CONV_W, width), lambda b, g, off=off, width=width: (0, off // width + g)),
                     pl.BlockSpec((1, width), lambda b, g, off=off, width=width: (0, off // width + g))]
    return pl.pallas_call(
        functools.partial(_ssd_kernel, nc=nc),
        grid=(bsz, 2),
        in_specs=[pl.BlockSpec((1, L, 256), lambda b, g: (b, 0, cx // 256 + g)),
                  pl.BlockSpec((1, L, 128), lambda b, g: (b, 0, cbm // 128 + g)),
                  pl.BlockSpec((1, L, 128), lambda b, g: (b, 0, ccm // 128 + g)),
                  pl.BlockSpec((1, L, 128), lambda b, g: (b, 0, cdt // 128 + g))] + cw_specs + [
                  pl.BlockSpec((1, 1, 128), lambda b, g: (g, 0, 0)),
                  pl.BlockSpec((1, 1, 128), lambda b, g: (g, 0, 0)),
                  pl.BlockSpec((1, 256), lambda b, g: (0, g))],
        out_specs=pl.BlockSpec((1, L, 256), lambda b, g: (b, 0, g)),
        out_shape=jax.ShapeDtypeStruct((bsz, L, nh * 64), F32),
        scratch_shapes=[pltpu.VMEM((L, 256), F32), pltpu.VMEM((L, 128), F32), pltpu.VMEM((L, 128), F32),
                        pltpu.VMEM((L, 128), F32), pltpu.VMEM((nc, 128, CHUNK), BF16)],
        compiler_params=_cp(("parallel", "parallel")), name="ssd",
    )(proj, proj, proj, lanes, conv_w, conv_b.reshape(1, -1), conv_w, conv_b.reshape(1, -1),
      conv_w, conv_b.reshape(1, -1), dtb, alog, dsk)


ML_HP = 2


def _ml_gate_lanes(hh, d):
    li = (hh * 2 + d) * 2
    return li, li + 1


def _ml_gate_columns(nh):
    cols = []
    for hp in range(nh // ML_HP):
        blk = [None] * (4 * ML_HP)
        for hh in range(ML_HP):
            for d in range(2):
                for t, lane in enumerate(_ml_gate_lanes(hh, d)):
                    blk[lane] = (d * 2 + t) * nh + hp * ML_HP + hh
        cols.append(blk)
    return cols


def _mlstm_kernel(q_ref, k_ref, v_ref, g_ref, cwq_ref, cbq_ref, cwk_ref, cbk_ref, gb_ref, o_ref,
                  qc_s, kc_s, kct_s, *, nc, dh):
    q = CHUNK
    first_lat = TM // q

    def prep(i, _):
        t0 = pl.multiple_of(i * q, q)
        idx = pl.ds(t0, q)
        for src, cw, cb, dst, mul in ((q_ref, cwq_ref, cbq_ref, qc_s, 1.0), (k_ref, cwk_ref, cbk_ref, kc_s, dh ** -0.5)):
            dst[idx, :] = _silu(_conv_tile(src, i, nc, cw[...], cb[...], q)) * mul
        for hh in range(ML_HP):
            kct_s[i, hh * dh:(hh + 1) * dh, :] = kc_s[idx, hh * dh:(hh + 1) * dh].T.astype(BF16)
        return 0
    lax.fori_loop(0, nc, prep, 0)

    gb = gb_ref[0]
    lane = lax.broadcasted_iota(jnp.int32, (q, 128), 1)
    eye8 = (lax.broadcasted_iota(jnp.int32, (8, 128), 0) == lax.broadcasted_iota(jnp.int32, (8, 128), 1)).astype(F32)
    o_ref[...] = jnp.zeros_like(o_ref)
    heads = range(ML_HP)
    assert q % dh == 0
    wide = lambda col: jnp.concatenate([col] * (q // dh), axis=1)

    def direction(d, j, states):
        rev = d == 1
        causal = _tri(q, rev)
        last = 0 if rev else q - 1
        if rev:
            ci = jnp.where(j < first_lat, first_lat - 1 - j, nc - 1 - (j - first_lat))
        else:
            ci = j
        idx = pl.ds(pl.multiple_of(ci * q, q), q)
        g = g_ref[0, idx, :] + gb
        logf = jnp.minimum(g, 0.0) - jnp.log(1.0 + jnp.exp(-jnp.abs(g)))
        is_f = functools.reduce(jnp.logical_or, [lane == _ml_gate_lanes(hh, d)[1] for hh in heads])
        cum = _dot_sel(jnp.where(causal, 1.0, 0.0), jnp.where(is_f, logf, 0.0))
        qc = [qc_s[idx, hh * dh:(hh + 1) * dh] for hh in heads]
        kc = [kc_s[idx, hh * dh:(hh + 1) * dh] for hh in heads]
        vc = [v_ref[0, idx, hh * dh:(hh + 1) * dh] for hh in heads]
        kct = [kct_s[ci, hh * dh:(hh + 1) * dh, :] for hh in heads]
        qk = [_dot(qc[hh], kct[hh]) for hh in heads]
        inter_c = [_dot(qc[hh], states[hh][0]) for hh in heads]
        yield
        rows = _dot_sel(eye8, jnp.where(is_f, cum, g), _NT)
        yield
        w, w_s, g_in, m_t, keep, m_new, wv, upd = [], [], [], [], [], [], [], []
        for hh in heads:
            li, lf = _ml_gate_lanes(hh, d)
            m_prev = states[hh][2]
            i_col = jnp.broadcast_to(g[:, li:li + 1], (q, dh))
            b_col = jnp.broadcast_to(cum[:, lf:lf + 1], (q, dh))
            i_row, b_row = rows[li:li + 1, :], rows[lf:lf + 1, :]
            dlog = jnp.where(causal, wide(b_col) - b_row + i_row, NEG)
            inter = b_col + m_prev
            m_t.append(jnp.maximum(jnp.max(dlog, axis=-1, keepdims=True), inter))
            w.append(jnp.exp(dlog - wide(m_t[hh])) * qk[hh])
            g_in.append(jnp.exp(inter - m_t[hh]))
            b_end = b_col[last:last + 1, :]
            g_s = b_end - b_col + i_col
            m_new.append(jnp.maximum(jnp.max(g_s, axis=0, keepdims=True), b_end + m_prev))
            w_s.append(jnp.exp(g_s - m_new[hh]))
            keep.append(jnp.exp(b_end + m_prev - m_new[hh]))
            wv.append(_dot(w[hh], vc[hh]))
            upd.append(_dot(kct[hh], w_s[hh] * vc[hh]))
        yield
        new_states = []
        for hh in heads:
            c_prev, n_prev, _ = states[hh]
            num = wv[hh] + g_in[hh] * inter_c[hh]
            den = (jnp.sum(w[hh], axis=-1, keepdims=True)
                   + g_in[hh] * jnp.sum(qc[hh] * n_prev, axis=-1, keepdims=True))
            o_ref[0, idx, hh * dh:(hh + 1) * dh] += num / jnp.maximum(jnp.abs(den), jnp.exp(-m_t[hh]))
            new_states.append((keep[hh] * c_prev + upd[hh],
                               keep[hh] * n_prev + jnp.sum(w_s[hh] * kc[hh], axis=0, keepdims=True), m_new[hh]))
        return tuple(new_states)

    def chunk(j, states):
        return tuple(_interleave([direction(d, j, states[d]) for d in (0, 1)]))
    init = (jnp.zeros((dh, dh), F32), jnp.zeros((1, dh), F32), jnp.full((1, dh), NEG, F32))
    lax.fori_loop(0, nc, chunk, tuple(tuple(init for _ in heads) for _ in (0, 1)))


def _mlstm(proj, lanes, conv_w, conv_b, gate_b, cols):
    bsz, L, _ = proj.shape
    nc = L // CHUNK
    cq, ck, cv, cg = cols
    nh = gate_b.shape[-1]
    dh = conv_w.shape[1] // (2 * nh)
    bw = ML_HP * dh
    gcols = np.asarray(_ml_gate_columns(nh))
    gbl = jnp.zeros((nh // ML_HP, 1, 128), F32).at[:, 0, :gcols.shape[1]].set(gate_b.reshape(-1)[gcols])
    return pl.pallas_call(
        functools.partial(_mlstm_kernel, nc=nc, dh=dh),
        grid=(bsz, nh // ML_HP),
        in_specs=[pl.BlockSpec((1, L, bw), lambda b, h: (b, 0, cq // bw + h)),
                  pl.BlockSpec((1, L, bw), lambda b, h: (b, 0, ck // bw + h)),
                  pl.BlockSpec((1, L, bw), lambda b, h: (b, 0, cv // bw + h)),
                  pl.BlockSpec((1, L, 128), lambda b, h: (b, 0, cg // 128 + h)),
                  pl.BlockSpec((CONV_W, bw), lambda b, h: (0, h)),
                  pl.BlockSpec((1, bw), lambda b, h: (0, h)),
                  pl.BlockSpec((CONV_W, bw), lambda b, h: (0, nh // ML_HP + h)),
                  pl.BlockSpec((1, bw), lambda b, h: (0, nh // ML_HP + h)),
                  pl.BlockSpec((1, 1, 128), lambda b, h: (h, 0, 0))],
        out_specs=pl.BlockSpec((1, L, bw), lambda b, h: (b, 0, h)),
        out_shape=jax.ShapeDtypeStruct((bsz, L, nh * dh), F32),
        scratch_shapes=[pltpu.VMEM((L, bw), F32), pltpu.VMEM((L, bw), F32), pltpu.VMEM((nc, bw, CHUNK), BF16)],
        compiler_params=_cp(("parallel", "parallel")), name="mlstm",
    )(proj, proj, proj, lanes, conv_w, conv_b.reshape(1, -1), conv_w, conv_b.reshape(1, -1), gbl)


def kernel(x, c, ctx, c_ctx, mod_w, mod_b, norm1_g, norm2_g, even_w_in, even_w_out, lru_conv_w, lru_conv_b, lru_wa, lru_ba, lru_wx, lru_bx, lru_lam, na_q_g, na_k_g, na_rpb, odd_w_in, odd_w_out, ssd_conv_w, ssd_conv_b, ssd_dt_bias, ssd_a_log, ssd_d, ssd_norm_g, ml_conv_w, ml_conv_b, ml_gate_b, ml_norm_g, moe_router_g, moe_router_e, moe_w1, moe_w3, moe_w2):
    bsz, S, d = x.shape
    lc = ctx.shape[1]
    assert lc == TM and S % TM == 0 and mod_w.shape[0] == 2
    nt = (lc + S) // TM
    L = lc + S

    cc = jnp.zeros((8, d), F32).at[0].set(c_ctx).at[1:1 + bsz].set(c)
    mod_all = _modulation(cc, mod_w, mod_b)

    def mod_for(l):
        m = mod_all[l].reshape(8, 6, d)
        return jnp.stack([jnp.broadcast_to(m[0], (bsz, 6, d)), m[1:1 + bsz]], axis=1)

    def router_w(l):
        w = jnp.zeros((d, 128), F32).at[:, :EXPERT_LANE0].set(moe_router_g[l]) \
            .at[:, EXPERT_LANE0:EXPERT_LANE0 + N_EXPERTS].set(moe_router_e[l])
        hi = w.astype(BF16)
        return jnp.concatenate([hi, (w - hi.astype(F32)).astype(BF16)], axis=1)

    xx = (ctx, x)

    mod0 = mod_for(0)
    (proj,) = _inproj(xx, mod0, norm1_g[0], even_w_in[0].astype(BF16))
    lw = lru_conv_w.shape[-1]
    r = _lru(proj, lru_conv_w[0], lru_conv_b[0], lru_wa[0], lru_ba[0], lru_wx[0], lru_bx[0], lru_lam[0])
    a = _na(proj, na_q_g[0], na_k_g[0], na_rpb[0], col0=2 * lw)
    x1, h2, gate = _outproj([(r, 0, lw), (proj, 1, lw), (a, 0, lw)], xx, even_w_out[0].astype(BF16), mod0,
                            norm2_g[0], router_w(0), even=True, tile0=0, ntiles=nt)
    ew = (moe_w1.astype(BF16), moe_w3.astype(BF16), moe_w2.astype(BF16))
    moe0 = _moe(h2.reshape(bsz * L, d), gate.reshape(bsz * L, 128), 0, *ew).reshape(bsz, L, d)

    mod1 = mod_for(1)
    sw = ssd_d.shape[-1] * 64
    xbc = ssd_conv_w.shape[-1]
    mw = ml_norm_g.shape[-1]
    w = odd_w_in[0]
    o = np.cumsum([0, sw, xbc, 2 * ssd_d.shape[-1], mw, mw, mw, mw])
    def lane_blocks(col0, groups):
        return [jnp.concatenate([w[:, col0 + np.asarray(blk)], jnp.zeros((d, 128 - len(blk)), F32)], axis=1)
                for blk in groups]
    dt_blks = lane_blocks(o[2], _ssd_dt_columns(ssd_d.shape[-1]))
    gate_blks = lane_blocks(o[7], _ml_gate_columns(ml_gate_b.shape[-1]))
    w_odd = jnp.concatenate([w[:, :o[2]], w[:, o[3]:o[7]]] + dt_blks + gate_blks, axis=1).astype(BF16)
    cz, cxs = 0, sw
    cB, cC = cxs + sw, cxs + sw + (xbc - sw) // 2
    cq = sw + xbc
    ck, cv, co, csm = cq + mw, cq + 2 * mw, cq + 3 * mw, cq + 4 * mw
    x0, proj1, lanes1 = _inproj(x1, mod1, norm1_g[1], w_odd, prev=(moe0, mod0), n_f32=w_odd.shape[1] - csm)
    ys = _ssd(proj1, lanes1, ssd_conv_w[0], ssd_conv_b[0], ssd_dt_bias[0], ssd_a_log[0], ssd_d[0], (cxs, cB, cC, 0))
    hm = _mlstm(proj1, lanes1, ml_conv_w[0], ml_conv_b[0], ml_gate_b[0], (cq, ck, cv, 128 * len(dt_blks)))
    x2, h2b, gate1 = _outproj([(ys, 0, sw), (proj1, cz // sw, sw), (hm, 0, mw), (proj1, co // mw, mw),
                               (ssd_norm_g[0].reshape(1, sw), 0, sw), (ml_norm_g[0].reshape(1, mw), 0, mw)],
                              x0, odd_w_out[0].astype(BF16), mod1, norm2_g[1], router_w(1),
                              even=False, tile0=1, ntiles=nt - 1)
    out = _moe(h2b.reshape(bsz * S, d), gate1.reshape(bsz * S, 128), 1, *ew, resid=(x2.reshape(bsz * S, d), mod1))
    return out.reshape(bsz, S, d)
```

```python
import functools
import math

import jax
import jax.numpy as jnp
import numpy as np
from jax import lax
from jax.experimental import pallas as pl
from jax.experimental.pallas import tpu as pltpu

F32 = jnp.float32
BF16 = jnp.bfloat16
HI = lax.Precision.HIGHEST

EPS = 1e-6
NEG = -1e30
GRID_W = 64
CONV_W = 4
LRU_C = 8.0
TM = 256
CHUNK = 256
NA_RQ = 4
NA_RK = 12
N_EXPERTS = 16
EXPERT_LANE0 = 4
VMEM_LIMIT = 56 * 1024 * 1024


def _cp(sem, vmem=VMEM_LIMIT):
    return pltpu.CompilerParams(dimension_semantics=sem, vmem_limit_bytes=vmem)


def _sigmoid(x):
    return jax.nn.sigmoid(x)


def _silu(x):
    return x * jax.nn.sigmoid(x)


def _softplus(x):
    return jnp.maximum(x, 0.0) + jnp.log(1.0 + jnp.exp(-jnp.abs(x)))


def _gelu_tanh(x):
    return 0.5 * x * (1.0 + jnp.tanh(math.sqrt(2.0 / math.pi) * (x + 0.044715 * (x * x * x))))


def _rms(x, axis=-1):
    return x * lax.rsqrt(jnp.mean(x * x, axis=axis, keepdims=True) + EPS)


def _dot(a, b):
    return jnp.dot(a.astype(BF16), b.astype(BF16), preferred_element_type=F32)


def _dot_hi(a, b):
    return jnp.dot(a, b, precision=HI, preferred_element_type=F32)


def _split3(x):
    x1 = x.astype(BF16)
    r = x - x1.astype(F32)
    x2 = r.astype(BF16)
    x3 = (r - x2.astype(F32)).astype(BF16)
    return x1, x2, x3


def _dot_sel(sel, x, dims=(((1,), (0,)), ((), ()))):
    s = sel.astype(BF16)
    x1, x2, x3 = _split3(x)
    d = lambda xi: lax.dot_general(s, xi, dims, preferred_element_type=F32)
    return (d(x3) + d(x2)) + d(x1)


def _dot_sel_r(x, sel):
    s = sel.astype(BF16)
    x1, x2, x3 = _split3(x)
    d = lambda xi: jnp.dot(xi, s, preferred_element_type=F32)
    return (d(x3) + d(x2)) + d(x1)


_NT = (((1,), (1,)), ((), ()))


def _interleave(gens):
    results = [None] * len(gens)
    live = list(range(len(gens)))
    while live:
        for i in list(live):
            try:
                next(gens[i])
            except StopIteration as stop:
                results[i] = stop.value
                live.remove(i)
    return results


def _dot_nt(a, b):
    return lax.dot_general(a.astype(BF16), b.astype(BF16), (((1,), (1,)), ((), ())),
                           preferred_element_type=F32)


def _mod_kernel(c_ref, w_ref, b_ref, o_ref):
    c = c_ref[...]
    o_ref[0] = _dot_hi(_silu(c), w_ref[0]) + b_ref[0]


def _modulation(cc, mod_w, mod_b):
    depth, d, n = mod_w.shape
    tn = 1536
    return pl.pallas_call(
        _mod_kernel,
        grid=(depth, n // tn),
        in_specs=[pl.BlockSpec((8, d), lambda l, j: (0, 0)),
                  pl.BlockSpec((1, d, tn), lambda l, j: (l, 0, j)),
                  pl.BlockSpec((1, 1, tn), lambda l, j: (l, 0, j))],
        out_specs=pl.BlockSpec((1, 8, tn), lambda l, j: (l, 0, j)),
        out_shape=jax.ShapeDtypeStruct((depth, 8, n), F32),
        compiler_params=_cp(("arbitrary", "arbitrary")),
        name="adaln_mod",
    )(cc, mod_w, mod_b.reshape(depth, 1, n))


def _inproj_kernel(*refs, fuse_prev, n_f32):
    refs = list(refs)
    s_ref = refs.pop() if n_f32 else None
    j = pl.program_id(1)
    if fuse_prev:
        x_ref, mo_ref, pmod_ref, mod_ref, g_ref, w_ref, xo_ref, p_ref = refs
    else:
        c_ref, xa_ref, xb_ref, mod_ref, g_ref, w_ref, p_ref = refs
    hs = []
    for s in range(2):
        rs = slice(s * TM, (s + 1) * TM)
        seg = 1 if s else jnp.minimum(j, 1)
        mod = mod_ref[0, seg]
        if fuse_prev:
            x = x_ref[0, rs, :] + pmod_ref[0, seg][5:6, :] * mo_ref[0, rs, :]
            xo_ref[0, rs, :] = x
        elif s == 0:
            x = jnp.where(j == 0, c_ref[0], xa_ref[0])
        else:
            x = xb_ref[0]
        hs.append(_rms(x) * g_ref[...] * (1.0 + mod[1:2, :]) + mod[0:1, :])
    p = _dot(jnp.concatenate(hs, axis=0), w_ref[...])
    n_main = p.shape[-1] - n_f32
    p_ref[0] = p[:, :n_main].astype(BF16)
    if n_f32:
        s_ref[0] = p[:, n_main:]


def _segment_specs(d):
    return [pl.BlockSpec((1, TM, d), lambda b, i: (b, 0, 0)),
            pl.BlockSpec((1, TM, d), lambda b, i: (b, jnp.maximum(i - 1, 0), 0))]


def _pick_segment(c_ref, x_ref):
    return jnp.where(pl.program_id(1) == 0, c_ref[0], x_ref[0])


def _inproj(x, mod, g, w, prev=None, n_f32=0):
    if prev is None:
        ctx, lat = x
        bsz, S, d = lat.shape
        L = TM + S
    else:
        bsz, L, d = x.shape
    n = w.shape[1]
    nt = L // TM
    steps = (nt + 1) // 2
    tok2 = pl.BlockSpec((1, 2 * TM, d), lambda b, j: (b, j, 0))
    modspec = pl.BlockSpec((1, 2, 6, d), lambda b, j: (b, 0, 0, 0))
    tail = [modspec, pl.BlockSpec((1, d), lambda b, j: (0, 0)), pl.BlockSpec((d, n), lambda b, j: (0, 0))]
    pspecs = [pl.BlockSpec((1, 2 * TM, n - n_f32), lambda b, j: (b, j, 0))]
    pshapes = [jax.ShapeDtypeStruct((bsz, L, n - n_f32), BF16)]
    if n_f32:
        pspecs.append(pl.BlockSpec((1, 2 * TM, n_f32), lambda b, j: (b, j, 0)))
        pshapes.append(jax.ShapeDtypeStruct((bsz, L, n_f32), F32))
    if prev is None:
        last_lat = S // TM - 1
        segs = [pl.BlockSpec((1, TM, d), lambda b, j: (b, 0, 0)),
                pl.BlockSpec((1, TM, d), lambda b, j: (b, jnp.maximum(2 * j - 1, 0), 0)),
                pl.BlockSpec((1, TM, d), lambda b, j: (b, jnp.minimum(2 * j, last_lat), 0))]
        return pl.pallas_call(
            functools.partial(_inproj_kernel, fuse_prev=False, n_f32=n_f32),
            grid=(bsz, steps), in_specs=segs + tail, out_specs=pspecs, out_shape=pshapes,
            compiler_params=_cp(("parallel", "arbitrary")), name="inproj",
        )(ctx, lat, lat, mod, g.reshape(1, d), w)
    moe_out, pmod = prev
    return pl.pallas_call(
        functools.partial(_inproj_kernel, fuse_prev=True, n_f32=n_f32),
        grid=(bsz, steps), in_specs=[tok2, tok2, modspec] + tail,
        out_specs=[tok2] + pspecs, out_shape=[jax.ShapeDtypeStruct(x.shape, F32)] + pshapes,
        compiler_params=_cp(("parallel", "parallel")), name="inproj_res",
    )(x, moe_out, pmod, mod, g.reshape(1, d), w)


def _conv_tile(ref, i, nt, cw, cb, width=TM):
    L = nt * width
    t0 = pl.multiple_of(i * width, width)
    cur = ref[0, pl.ds(t0, width), :].astype(F32)
    prev = ref[0, pl.ds(pl.multiple_of(jnp.maximum(t0 - 16, 0), 16), 16), :].astype(F32)[8:16]
    nxt = ref[0, pl.ds(pl.multiple_of(jnp.minimum(t0 + width, L - 16), 16), 16), :].astype(F32)[0:8]
    return _conv_vals(cur, prev, nxt, i, nt, cw, cb, width, first_lat=TM // width)


def _conv_vals(cur, prev, nxt, i, nt, cw, cb, width, first_lat):
    prev = jnp.where((i != 0) & (i != first_lat), prev, 0.0)
    nxt = jnp.where((i != first_lat - 1) & (i != nt - 1), nxt, 0.0)
    cat = jnp.concatenate([prev, cur, nxt], axis=0)
    return (cw[0:1] * cat[6:6 + width] + cw[1:2] * cat[7:7 + width] + cw[2:3] * cur
            + cw[3:4] * cat[9:9 + width] + cb)


def _lru_kernel(ux_ref, cw_ref, cb_ref, gw_ref, gb_ref, lam_ref, o_ref, *, nt):
    cw = cw_ref[...]
    cb = cb_ref[...]
    lam = lam_ref[0]
    sp = _softplus(-lam)
    row = lax.broadcasted_iota(jnp.int32, (TM, 128), 0) & 7

    def gates(i, d):
        xl = _conv_tile(ux_ref, i, nt, cw, cb)
        g = _dot(xl, gw_ref[0, d]) + gb_ref[0, d]
        r = _sigmoid(g[:, :128])
        ig = _sigmoid(g[:, 128:])
        log_a = -LRU_C * r * sp[d:d + 1]
        a = jnp.exp(log_a)
        u = jnp.sqrt(1.0 - a * a) * (ig * xl)
        return a, u

    def scan_tile(i, d, carry, accumulate):
        a, u = gates(i, d)
        rev = d == 1
        for k in (1, 2, 4):
            sh = 8 - k if rev else k
            ok = (row < 8 - k) if rev else (row >= k)
            ash = pltpu.roll(a.reshape(TM // 8, 8, 128), sh, 1).reshape(TM, 128)
            ush = pltpu.roll(u.reshape(TM // 8, 8, 128), sh, 1).reshape(TM, 128)
            u = jnp.where(ok, u + a * ush, u)
            a = jnp.where(ok, a * ash, a)
        t0 = i * TM
        groups = range(TM // 8)
        for s in (reversed(groups) if rev else groups):
            h = u[s * 8:(s + 1) * 8] + a[s * 8:(s + 1) * 8] * carry
            carry = h[0:1] if rev else h[7:8]
            idx = pl.ds(pl.multiple_of(t0 + s * 8, 8), 8)
            if accumulate:
                o_ref[0, idx, :] += h
            else:
                o_ref[0, idx, :] = h
        return carry

    zero = jnp.zeros((1, 128), F32)
    lax.fori_loop(0, nt, lambda i, c: scan_tile(i, 0, c, False), zero)
    lax.fori_loop(0, nt, lambda j, c: scan_tile(jnp.where(j == 0, 0, nt - j), 1, c, True), zero)


def _lru(proj, conv_w, conv_b, wa, ba, wx, bx, lam):
    bsz, L, _ = proj.shape
    nt = L // TM
    width = conv_w.shape[1]
    ng = width // 128

    def blockdiag(w):
        w = w.reshape(2, ng, 2, 64, 64)
        z = jnp.zeros_like(w[:, :, 0])
        top = jnp.concatenate([w[:, :, 0], z], axis=-1)
        bot = jnp.concatenate([z, w[:, :, 1]], axis=-1)
        return jnp.concatenate([top, bot], axis=-2)
    gw = jnp.concatenate([blockdiag(wa), blockdiag(wx)], axis=-1).transpose(1, 0, 2, 3).astype(BF16)
    gb = jnp.concatenate([ba.reshape(2, ng, 1, 128), bx.reshape(2, ng, 1, 128)], axis=-1).transpose(1, 0, 2, 3)
    lam_g = lam.reshape(2, ng, 128).transpose(1, 0, 2)
    return pl.pallas_call(
        functools.partial(_lru_kernel, nt=nt),
        grid=(bsz, ng),
        in_specs=[pl.BlockSpec((1, L, 128), lambda b, c: (b, 0, c)),
                  pl.BlockSpec((CONV_W, 128), lambda b, c: (0, c)),
                  pl.BlockSpec((1, 128), lambda b, c: (0, c)),
                  pl.BlockSpec((1, 2, 128, 256), lambda b, c: (c, 0, 0, 0)),
                  pl.BlockSpec((1, 2, 1, 256), lambda b, c: (c, 0, 0, 0)),
                  pl.BlockSpec((1, 2, 128), lambda b, c: (c, 0, 0))],
        out_specs=pl.BlockSpec((1, L, 128), lambda b, c: (b, 0, c)),
        out_shape=jax.ShapeDtypeStruct((bsz, L, width), F32),
        compiler_params=_cp(("parallel", "parallel")), name="rglru",
    )(proj, conv_w, conv_b.reshape(1, width), gw, gb, lam_g)


def _na_bias_table(rpb, rows):
    nh = rpb.shape[0]
    win_c = (rpb.shape[2] + 1) // 2
    qc = np.arange(GRID_W)[:, None]
    kc = np.arange(GRID_W)[None, :]
    cstart = np.clip(qc - win_c // 2, 0, GRID_W - win_c)
    col_ok = (kc >= cstart) & (kc < cstart + win_c)
    dcol = np.clip(kc - qc + (win_c - 1), 0, 2 * win_c - 2)
    oc = (np.arange(2 * win_c - 1)[:, None, None] == dcol[None]).astype(np.float32)
    ct = jnp.where(jnp.asarray(col_ok), jnp.einsum('hrc,cqk->hrqk', rpb.astype(F32), jnp.asarray(oc), precision=HI),
                   NEG)
    return pl.pallas_call(
        functools.partial(_na_bias_kernel, rows=rows),
        grid=(nh,),
        in_specs=[pl.BlockSpec((1,) + ct.shape[1:], lambda h: (h, 0, 0, 0))],
        out_specs=pl.BlockSpec((3, 1, TM, NA_RK * GRID_W), lambda h: (0, h, 0, 0)),
        out_shape=jax.ShapeDtypeStruct((3, nh, TM, NA_RK * GRID_W), F32),
        compiler_params=_cp(("parallel",)), name="nbr_bias",
    )(ct)


def _na_bias_kernel(ct_ref, o_ref, *, rows):
    win_r = (ct_ref.shape[1] + 1) // 2
    neg = jnp.full((GRID_W, GRID_W), NEG, F32)
    for p, (r0, w0) in enumerate(((0, 0), (2 * NA_RQ, NA_RQ), (rows - NA_RQ, rows - NA_RK))):
        for a in range(NA_RQ):
            r = r0 + a
            rstart = min(max(r - win_r // 2, 0), rows - win_r)
            tiles = [ct_ref[0, w0 + b - r + win_r - 1] if rstart <= w0 + b < rstart + win_r else neg
                     for b in range(NA_RK)]
            for bp in range(NA_RK // 2):
                o_ref[p, 0, a * GRID_W:(a + 1) * GRID_W, bp * 2 * GRID_W:(bp + 1) * 2 * GRID_W] = (
                    jnp.concatenate(tiles[2 * bp:2 * bp + 2], axis=1))


def _pair_rms(x, lo):
    x2 = x * x
    s0 = jnp.sum(jnp.where(lo, x2, 0.0), axis=-1, keepdims=True)
    s1 = jnp.sum(jnp.where(lo, 0.0, x2), axis=-1, keepdims=True)
    return x * lax.rsqrt(jnp.where(lo, s0, s1) * (2.0 / x.shape[-1]) + EPS)


def _na_kernel(q_ref, k_ref, v_ref, bias_ref, qg_ref, kg_ref, o_ref, kn_s, vb_s, *, rows, hd):
    j = pl.program_id(2)
    L = k_ref.shape[1]
    nb = L // TM
    nkeys = NA_RK * GRID_W
    lo = lax.broadcasted_iota(jnp.int32, (1, 2 * hd), 1) < hd

    @pl.when(j == 0)
    def _prep():
        def body(t, _):
            idx = pl.ds(pl.multiple_of(t * TM, TM), TM)
            kn_s[idx, :] = (_pair_rms(k_ref[0, idx, :].astype(F32), lo) * kg_ref[...]).astype(BF16)
            vb_s[idx, :] = v_ref[0, idx, :].astype(BF16)
            return 0
        lax.fori_loop(0, nb, body, 0)

    kctx = kn_s[0:TM, :]
    vctx = vb_s[0:TM, :]
    qscale = qg_ref[...] * hd ** -0.5

    def head(q, bias, kwin, vwin):
        s_c = _dot_nt(q, kctx)
        if kwin is not None:
            s_w = _dot_nt(q, kwin) + bias
        yield
        m = jnp.max(s_c, axis=-1, keepdims=True)
        if kwin is not None:
            m = jnp.maximum(m, jnp.max(s_w, axis=-1, keepdims=True))
            p_w = jnp.exp(s_w - m)
        p_c = jnp.exp(s_c - m)
        den = jnp.sum(p_c, axis=-1, keepdims=True)
        num = _dot(p_c, vctx)
        if kwin is not None:
            den = den + jnp.sum(p_w, axis=-1, keepdims=True)
            num = num + _dot(p_w, vwin)
        yield
        return num / den

    def tile_chains(slot, tile):
        qn = _pair_rms(q_ref[0, slot * TM:(slot + 1) * TM, :].astype(F32), lo) * qscale
        q_h = [jnp.where(lo, qn, 0.0).astype(BF16), jnp.where(lo, 0.0, qn).astype(BF16)]
        if tile is None:
            return [head(q_h[hh], None, None, None) for hh in range(2)]
        w0 = jnp.clip((tile - 1) * NA_RQ - NA_RQ, 0, rows - NA_RK)
        start = pl.multiple_of(TM + w0 * GRID_W, GRID_W)
        kwin = kn_s[pl.ds(start, nkeys), :]
        vwin = vb_s[pl.ds(start, nkeys), :]
        pat = jnp.where(tile <= 1, 0, jnp.where(tile >= nb - 1, 2, 1))
        return [head(q_h[hh], bias_ref[pat, hh], kwin, vwin) for hh in range(2)]

    def run(chains):
        outs = _interleave(chains)
        for slot in range(len(outs) // 2):
            o_ref[0, slot * TM:(slot + 1) * TM, :] = jnp.where(lo, outs[2 * slot], outs[2 * slot + 1]).astype(BF16)

    last = (nb - 1) // 2

    @pl.when(j == 0)
    def _():
        run(tile_chains(0, None) + tile_chains(1, 1))

    @pl.when((j > 0) & (j < last))
    def _():
        run(tile_chains(0, 2 * j) + tile_chains(1, 2 * j + 1))

    @pl.when(j == last)
    def _():
        run(tile_chains(0, 2 * j))


def _na(proj, q_g, k_g, rpb, col0):
    bsz, L, _ = proj.shape
    nh = rpb.shape[0]
    hd = q_g.shape[0]
    width = nh * hd
    rows = (L - TM) // GRID_W
    nb = L // TM
    bias = _na_bias_table(rpb, rows)
    qb, kb, vb = col0 // 128, (col0 + width) // 128, (col0 + 2 * width) // 128

    assert nb % 2 == 1

    return pl.pallas_call(
        functools.partial(_na_kernel, rows=rows, hd=hd),
        grid=(width // 128, bsz, (nb + 1) // 2),
        in_specs=[pl.BlockSpec((1, 2 * TM, 128), lambda h, b, j: (b, j, qb + h)),
                  pl.BlockSpec((1, L, 128), lambda h, b, j: (b, 0, kb + h)),
                  pl.BlockSpec((1, L, 128), lambda h, b, j: (b, 0, vb + h)),
                  pl.BlockSpec((3, 2, TM, NA_RK * GRID_W), lambda h, b, j: (0, h, 0, 0)),
                  pl.BlockSpec((1, 2 * hd), lambda h, b, j: (0, 0)),
                  pl.BlockSpec((1, 2 * hd), lambda h, b, j: (0, 0))],
        out_specs=pl.BlockSpec((1, 2 * TM, 128), lambda h, b, j: (b, j, h)),
        out_shape=jax.ShapeDtypeStruct((bsz, L, width), BF16),
        scratch_shapes=[pltpu.VMEM((L, 128), BF16), pltpu.VMEM((L, 128), BF16)],
        compiler_params=_cp(("parallel", "parallel", "arbitrary")), name="nbr_attn",
    )(proj, proj, proj, bias, jnp.tile(q_g, 2).reshape(1, 2 * hd), jnp.tile(k_g, 2).reshape(1, 2 * hd))


def _route(lg):
    lane = lax.broadcasted_iota(jnp.int32, lg.shape, 1)
    lane_f = lane.astype(F32)
    is_g = lane < EXPERT_LANE0
    gl = jnp.where(is_g, lg, NEG)
    gmax = jnp.max(gl, axis=-1, keepdims=True)
    gsel = jnp.min(jnp.where(is_g & (gl == gmax), lane_f, 1e9), axis=-1, keepdims=True)
    g_w = 1.0 / jnp.sum(jnp.where(is_g, jnp.exp(gl - gmax), 0.0), axis=-1, keepdims=True)
    grp = ((lane - EXPERT_LANE0) >> 2).astype(F32)
    in_g = (lane >= EXPERT_LANE0) & (lane < EXPERT_LANE0 + N_EXPERTS) & (grp == gsel)
    el = jnp.where(in_g, lg, NEG)
    v1 = jnp.max(el, axis=-1, keepdims=True)
    i1 = jnp.min(jnp.where(in_g & (el == v1), lane_f, 1e9), axis=-1, keepdims=True)
    el2 = jnp.where(lane_f == i1, NEG, el)
    v2 = jnp.max(el2, axis=-1, keepdims=True)
    i2 = jnp.min(jnp.where(in_g & (lane_f != i1) & (el2 == v2), lane_f, 1e9), axis=-1, keepdims=True)
    t = jnp.exp(v2 - v1)
    w1 = g_w / (1.0 + t)
    w2 = g_w * t / (1.0 + t)
    return (jnp.where(lane_f == i1, w1, 0.0) + jnp.where(lane_f == i2, w2, 0.0)
            + jnp.where(lane_f == gsel, 1.0, 0.0))


def _outproj_kernel(*refs, even):
    if even:
        (r_ref, ug_ref, a_ref, c_ref, x_ref, w_ref, mod_ref, g2_ref, rw_ref, x1_ref, h2_ref, gate_ref) = refs
        is_ctx = pl.program_id(1) == 0
        x_rows = lambda rs: jnp.where(is_ctx, c_ref[0, rs, :], x_ref[0, rs, :])
    else:
        (ys_ref, z_ref, hm_ref, mo_ref, sg_ref, mg_ref, x_ref, w_ref, mod_ref, g2_ref, rw_ref,
         x1_ref, h2_ref, gate_ref) = refs
        x_rows = lambda rs: x_ref[0, rs, :]
    mod = mod_ref[0, 0]

    def part(rs):
        if even:
            y_in = jnp.concatenate([(r_ref[0, rs, :] * _gelu_tanh(ug_ref[0, rs, :].astype(F32))).astype(BF16),
                                    a_ref[0, rs, :]], axis=-1)
        else:
            ys = ys_ref[0, rs, :] * _silu(z_ref[0, rs, :].astype(F32))
            sg = sg_ref[...]
            mg = mg_ref[...]
            hm = hm_ref[0, rs, :]
            sig_o = _sigmoid(mo_ref[0, rs, :].astype(F32))
            gw = ys.shape[-1] // 2
            parts = [_rms(ys[:, g * gw:(g + 1) * gw]) * sg[:, g * gw:(g + 1) * gw] for g in range(2)]
            hw = 128
            parts += [_rms(hm[:, h * hw:(h + 1) * hw]) * mg[:, h * hw:(h + 1) * hw] * sig_o[:, h * hw:(h + 1) * hw]
                      for h in range(hm.shape[-1] // hw)]
            y_in = jnp.concatenate(parts, axis=-1)
        y = _dot(y_in, w_ref[...])
        yield
        x1 = x_rows(rs) + mod[2:3, :] * y
        x1_ref[0, rs, :] = x1
        h2 = _rms(x1) * g2_ref[...] * (1.0 + mod[4:5, :]) + mod[3:4, :]
        h2_ref[0, rs, :] = h2.astype(BF16)
        hi = h2.astype(BF16)
        lo = (h2 - hi.astype(F32)).astype(BF16)
        lg2 = jnp.dot(hi, rw_ref[...], preferred_element_type=F32)
        lg1 = jnp.dot(lo, rw_ref[:, 0:128], preferred_element_type=F32)
        yield
        gate_ref[0, rs, :] = _route(lg2[:, 0:128] + lg2[:, 128:256] + lg1)

    nparts = 2
    rows = TM // nparts
    _interleave([part(pl.ds(p * rows, rows)) for p in range(nparts)])


def _outproj(mix_inputs, x, w, mod, g2, rw, even, tile0, ntiles):
    xs = list(x) if even else [x]
    bsz, _, d = xs[-1].shape
    specs, args = [], []
    for arr, cb, wdt in mix_inputs:
        if arr.ndim == 3:
            specs.append(pl.BlockSpec((1, TM, wdt), lambda b, i, cb=cb: (b, i + tile0, cb)))
        else:
            specs.append(pl.BlockSpec((1, wdt), lambda b, i: (0, 0)))
        args.append(arr)
    specs += _segment_specs(d) if even else [pl.BlockSpec((1, TM, d), lambda b, i: (b, i + tile0, 0))]
    specs += [pl.BlockSpec(w.shape, lambda b, i: (0, 0)),
              pl.BlockSpec((1, 1, 6, d), lambda b, i: (b, jnp.minimum(i + tile0, 1), 0, 0)),
              pl.BlockSpec((1, d), lambda b, i: (0, 0)),
              pl.BlockSpec(rw.shape, lambda b, i: (0, 0))]
    args += xs + [w, mod, g2.reshape(1, d), rw]
    lo = ntiles * TM
    return pl.pallas_call(
        functools.partial(_outproj_kernel, even=even),
        grid=(bsz, ntiles), in_specs=specs,
        out_specs=[pl.BlockSpec((1, TM, d), lambda b, i: (b, i, 0)),
                   pl.BlockSpec((1, TM, d), lambda b, i: (b, i, 0)),
                   pl.BlockSpec((1, TM, 128), lambda b, i: (b, i, 0))],
        out_shape=[jax.ShapeDtypeStruct((bsz, lo, d), F32), jax.ShapeDtypeStruct((bsz, lo, d), BF16),
                   jax.ShapeDtypeStruct((bsz, lo, 128), F32)],
        compiler_params=_cp(("parallel", "parallel")), name="outproj_even" if even else "outproj_odd",
    )(*args)


MOE_CH = 128
GROUP_SIZE = 4
MOE_VMEM_LIMIT = 62 * 1024 * 1024
N_GROUPS = N_EXPERTS // GROUP_SIZE


def _moe_kernel(*refs, residual):
    if residual:
        x_ref, g_ref, w1_ref, w3_ref, w2_ref, x1_ref, mod_ref, o_ref, xs_s, ys_s, gs_s, pt_s, plan_s = refs
    else:
        x_ref, g_ref, w1_ref, w3_ref, w2_ref, o_ref, xs_s, ys_s, gs_s, pt_s, plan_s = refs
    grp = pl.program_id(1)
    tb = x_ref.shape[0]
    nch = xs_s.shape[0] // MOE_CH

    @pl.when(grp == 0)
    def _plan():
        g = g_ref[...]
        lane = lax.broadcasted_iota(jnp.int32, g.shape, 1)
        oh = jnp.where(lane < N_GROUPS, g, 0.0)
        earlier = (lax.broadcasted_iota(jnp.int32, (tb, tb), 0) > lax.broadcasted_iota(jnp.int32, (tb, tb), 1))
        rank = jnp.dot(jnp.where(earlier, 1.0, 0.0).astype(BF16), oh.astype(BF16),
                       preferred_element_type=F32)
        cnt = jnp.sum(oh, axis=0, keepdims=True)
        lane1 = lax.broadcasted_iota(jnp.int32, (1, 128), 1)
        off = jnp.int32(0)
        offv = jnp.zeros((1, 128), F32)
        for gi in range(N_GROUPS):
            n = jnp.sum(jnp.where(lane1 == gi, cnt, 0.0)).astype(jnp.int32)
            nchunks = (n + (MOE_CH - 1)) // MOE_CH
            plan_s[gi] = off // MOE_CH
            plan_s[N_GROUPS + gi] = nchunks
            offv = offv + jnp.where(lane1 == gi, off.astype(F32), 0.0)
            off = off + nchunks * MOE_CH
        pos_col = jnp.sum(oh * (rank + offv), axis=1, keepdims=True)
        posb = jnp.broadcast_to(pos_col, (tb, 128))
        pos_row = jnp.concatenate([posb[i * 128:(i + 1) * 128, :].T[0:1, :] for i in range(tb // 128)], axis=1)
        x = x_ref[...]
        g_hi = g.astype(BF16)
        g_lo = (g - g_hi.astype(F32)).astype(BF16)
        lane_c = lax.broadcasted_iota(jnp.int32, (tb, MOE_CH), 1).astype(F32)
        row_c = lax.broadcasted_iota(jnp.int32, (MOE_CH, tb), 0).astype(F32)
        used = off // MOE_CH

        def dispatch(c):
            sl = slice(c * MOE_CH, (c + 1) * MOE_CH)
            p = jnp.where(row_c + float(c * MOE_CH) == pos_row, 1.0, 0.0).astype(BF16)
            xs_s[sl, :] = jnp.dot(p, x, preferred_element_type=F32).astype(BF16)
            gs_s[sl, :] = (jnp.dot(p, g_hi, preferred_element_type=F32)
                           + jnp.dot(p, g_lo, preferred_element_type=F32))
        for c in range(nch):
            sl = slice(c * MOE_CH, (c + 1) * MOE_CH)
            pt_s[:, sl] = jnp.where(pos_col == lane_c + float(c * MOE_CH), 1.0, 0.0).astype(BF16)
            if c < tb // MOE_CH:
                dispatch(c)
            else:
                pl.when(c < used)(functools.partial(dispatch, c))
        ys_s[...] = jnp.zeros_like(ys_s)

    c0 = plan_s[grp]
    nchunks = plan_s[N_GROUPS + grp]

    def ffn(chunk0, nrows):
        rows = pl.ds(pl.multiple_of(chunk0 * MOE_CH, MOE_CH), nrows)
        xs = xs_s[rows, :]
        gs = gs_s[rows, :]
        lane_g = lax.broadcasted_iota(jnp.int32, (nrows, 128), 1)
        acc = None
        for k in range(GROUP_SIZE):
            a = jnp.dot(xs, w1_ref[0, k], preferred_element_type=F32)
            b = jnp.dot(xs, w3_ref[0, k], preferred_element_type=F32)
            y = _dot(_silu(a) * b, w2_ref[0, k])
            ge = jnp.sum(jnp.where(lane_g == grp * GROUP_SIZE + k + EXPERT_LANE0, gs, 0.0), axis=-1, keepdims=True)
            acc = ge * y if acc is None else acc + ge * y
        ys_s[rows, :] = acc.astype(BF16)

    wide = 4

    def full(p, _):
        ffn(c0 + wide * p, wide * MOE_CH)
        return 0
    lax.fori_loop(0, nchunks // wide, full, 0)
    rem = nchunks % wide
    for r in range(1, wide):
        pl.when(rem == r)(functools.partial(ffn, c0 + nchunks - r, r * MOE_CH))

    @pl.when(grp == N_GROUPS - 1)
    def _combine():
        out = jnp.dot(pt_s[...], ys_s[...], preferred_element_type=F32)
        if residual:
            out = x1_ref[...] + mod_ref[0, 0][5:6, :] * out
        o_ref[...] = out


def _moe(h2, gate, layer, w1, w3, w2, resid=None):
    t, d = h2.shape
    _, ne, _, ff = w1.shape
    tb = math.gcd(t, 1024)
    npad = tb + N_GROUPS * MOE_CH
    assert ne == N_EXPERTS
    extra_specs, extra_args = [], []
    if resid is not None:
        x1, mod = resid
        per_batch = t // mod.shape[0] // tb
        extra_specs = [pl.BlockSpec((tb, d), lambda i, e: (i, 0)),
                       pl.BlockSpec((1, 1, 6, d), lambda i, e: (i // per_batch, 1, 0, 0))]
        extra_args = [x1, mod]
    return pl.pallas_call(
        functools.partial(_moe_kernel, residual=resid is not None),
        grid=(t // tb, N_GROUPS),
        in_specs=[pl.BlockSpec((tb, d), lambda i, e: (i, 0)),
                  pl.BlockSpec((tb, 128), lambda i, e: (i, 0)),
                  pl.BlockSpec((1, GROUP_SIZE, d, ff), lambda i, e: (layer, e, 0, 0)),
                  pl.BlockSpec((1, GROUP_SIZE, d, ff), lambda i, e: (layer, e, 0, 0)),
                  pl.BlockSpec((1, GROUP_SIZE, ff, d), lambda i, e: (layer, e, 0, 0))] + extra_specs,
        out_specs=pl.BlockSpec((tb, d), lambda i, e: (i, 0)),
        out_shape=jax.ShapeDtypeStruct((t, d), F32),
        scratch_shapes=[pltpu.VMEM((npad, d), BF16), pltpu.VMEM((npad, d), BF16), pltpu.VMEM((npad, 128), F32),
                        pltpu.VMEM((tb, npad), BF16), pltpu.SMEM((2 * N_GROUPS,), jnp.int32)],
        compiler_params=_cp(("parallel", "arbitrary"), MOE_VMEM_LIMIT), name="moe",
    )(h2, gate, w1, w3, w2, *extra_args)


def _tri(q, rev):
    r = lax.broadcasted_iota(jnp.int32, (q, q), 0)
    c = lax.broadcasted_iota(jnp.int32, (q, q), 1)
    return (c >= r) if rev else (c <= r)


def _ssd_dt_columns(nh):
    return [[d * nh + 4 * g + r for d in range(2) for r in range(4)] for g in range(nh // 4)]


def _ssd_kernel(xs_ref, b_ref, c_ref, dt_ref, cwx_ref, cbx_ref, cwb_ref, cbb_ref, cwc_ref, cbc_ref,
                dtb_ref, alog_ref, dsk_ref, o_ref, xc_s, bc_s, cc_s, dt_s, bct_s, *, nc):
    q = CHUNK
    first_lat = TM // q

    dtb = dtb_ref[0]

    def prep(i, _):
        t0 = pl.multiple_of(i * q, q)
        idx = pl.ds(t0, q)
        for src, cw, cb, dst in ((xs_ref, cwx_ref, cbx_ref, xc_s), (b_ref, cwb_ref, cbb_ref, bc_s),
                                 (c_ref, cwc_ref, cbc_ref, cc_s)):
            dst[idx, :] = _silu(_conv_tile(src, i, nc, cw[...], cb[...], q))
        dt_s[idx, :] = _softplus(dt_ref[0, idx, :] + dtb)
        o_ref[0, idx, :] = dsk_ref[...] * xc_s[idx, :]
        bct_s[i] = bc_s[idx, :].T.astype(BF16)
        return 0
    lax.fori_loop(0, nc, prep, 0)

    low_half = lax.broadcasted_iota(jnp.int32, (1, 128), 1) < 64

    def per_head(t, d):
        col = [jnp.broadcast_to(t[:, 4 * d + r:4 * d + r + 1], (t.shape[0], 128)) for r in range(4)]
        return jnp.concatenate([jnp.where(low_half, col[0], col[1]), jnp.where(low_half, col[2], col[3])], axis=1)

    a_lane = -jnp.exp(alog_ref[0])
    a_e = [per_head(a_lane, d) for d in (0, 1)]
    lane256 = lax.broadcasted_iota(jnp.int32, (q, 256), 1) // 64
    r8 = lax.broadcasted_iota(jnp.int32, (8, 256), 0)
    l8 = lax.broadcasted_iota(jnp.int32, (8, 256), 1)
    head_rows = (l8 == r8 * 64).astype(F32)

    def chain(d, j, state_in, state_out):
        rev = d == 1
        causal = _tri(q, rev)
        tri = causal.astype(F32)
        last = 0 if rev else q - 1
        if rev:
            ci = jnp.where(j < first_lat, first_lat - 1 - j, nc - 1 - (j - first_lat))
        else:
            ci = j
        idx = pl.ds(pl.multiple_of(ci * q, q), q)
        dt_e = per_head(dt_s[idx, :], d)
        xc = xc_s[idx, :]
        bc = bc_s[idx, :]
        cc = cc_s[idx, :]
        cb = _dot_nt(cc, bc)
        yield
        cum_e = _dot_sel(tri, dt_e * a_e[d])
        dtx = xc * dt_e
        yield
        cum_rows = _dot_sel(head_rows, cum_e, _NT)
        tot_e = cum_e[last:last + 1, :]
        states = _dot(bct_s[ci], dtx * jnp.exp(tot_e - cum_e))
        yield
        y = None
        for r in range(4):
            seg = cum_e[:, r * 64:r * 64 + 1] - cum_rows[r:r + 1, :]
            m = cb * jnp.exp(jnp.where(causal, seg, NEG))
            yr = _dot(m, jnp.where(lane256 == r, dtx, 0.0))
            y = yr if y is None else y + yr
            yield
        h = state_in[0]
        o_ref[0, idx, :] += y + jnp.exp(cum_e) * _dot(cc, h)
        state_out[0] = jnp.exp(tot_e) * h + states

    def step(positions, hs):
        cells = [[[hs[d]]] + [[None] for _ in positions] for d in (0, 1)]
        _interleave([chain(d, p, cells[d][k], cells[d][k + 1]) for k, p in enumerate(positions) for d in (0, 1)])
        return cells[0][-1][0], cells[1][-1][0]

    h0 = jnp.zeros((128, 256), F32)
    hs = lax.fori_loop(0, nc // 2, lambda j, hs: step((2 * j, 2 * j + 1), hs), (h0, h0))
    if nc % 2:
        step((nc - 1,), hs)


def _ssd(proj, lanes, conv_w, conv_b, dt_bias, a_log, d_skip, cols):
    bsz, L, _ = proj.shape
    nc = L // CHUNK
    cx, cbm, ccm, cdt = cols
    nh = d_skip.shape[0]
    gcols = np.asarray(_ssd_dt_columns(nh))
    dtb = jnp.zeros((gcols.shape[0], 1, 128), F32).at[:, 0, :gcols.shape[1]].set(dt_bias.reshape(-1)[gcols])
    alog = jnp.zeros((gcols.shape[0], 1, 128), F32).at[:, 0, :gcols.shape[1]].set(a_log.reshape(-1)[gcols])
    dsk = jnp.repeat(d_skip.astype(F32), 64).reshape(1, nh * 64)
    cw_specs = []
    for width, off in ((256, 0), (128, 512), (128, 768)):
        cw_specs += [pl.BlockSpec((CONV_W, width), lambda b, g, off=off, width=width: (0, off // width + g)),
                     pl.BlockSpec((1, width), lambda b, g, off=off, width=width: (0, off // width + g))]
    return pl.pallas_call(
        functools.partial(_ssd_kernel, nc=nc),
        grid=(bsz, 2),
        in_specs=[pl.BlockSpec((1, L, 256), lambda b, g: (b, 0, cx // 256 + g)),
                  pl.BlockSpec((1, L, 128), lambda b, g: (b, 0, cbm // 128 + g)),
                  pl.BlockSpec((1, L, 128), lambda b, g: (b, 0, ccm // 128 + g)),
                  pl.BlockSpec((1, L, 128), lambda b, g: (b, 0, cdt // 128 + g))] + cw_specs + [
                  pl.BlockSpec((1, 1, 128), lambda b, g: (g, 0, 0)),
                  pl.BlockSpec((1, 1, 128), lambda b, g: (g, 0, 0)),
                  pl.BlockSpec((1, 256), lambda b, g: (0, g))],
        out_specs=pl.BlockSpec((1, L, 256), lambda b, g: (b, 0, g)),
        out_shape=jax.ShapeDtypeStruct((bsz, L, nh * 64), F32),
        scratch_shapes=[pltpu.VMEM((L, 256), F32), pltpu.VMEM((L, 128), F32), pltpu.VMEM((L, 128), F32),
                        pltpu.VMEM((L, 128), F32), pltpu.VMEM((nc, 128, CHUNK), BF16)],
        compiler_params=_cp(("parallel", "parallel")), name="ssd",
    )(proj, proj, proj, lanes, conv_w, conv_b.reshape(1, -1), conv_w, conv_b.reshape(1, -1),
      conv_w, conv_b.reshape(1, -1), dtb, alog, dsk)


ML_HP = 2


def _ml_gate_lanes(hh, d):
    li = (hh * 2 + d) * 2
    return li, li + 1


def _ml_gate_columns(nh):
    cols = []
    for hp in range(nh // ML_HP):
        blk = [None] * (4 * ML_HP)
        for hh in range(ML_HP):
            for d in range(2):
                for t, lane in enumerate(_ml_gate_lanes(hh, d)):
                    blk[lane] = (d * 2 + t) * nh + hp * ML_HP + hh
        cols.append(blk)
    return cols


def _mlstm_kernel(q_ref, k_ref, v_ref, g_ref, cwq_ref, cbq_ref, cwk_ref, cbk_ref, gb_ref, o_ref,
                  qc_s, kc_s, kct_s, *, nc, dh):
    q = CHUNK
    first_lat = TM // q

    def prep(i, _):
        t0 = pl.multiple_of(i * q, q)
        idx = pl.ds(t0, q)
        for src, cw, cb, dst, mul in ((q_ref, cwq_ref, cbq_ref, qc_s, 1.0), (k_ref, cwk_ref, cbk_ref, kc_s, dh ** -0.5)):
            dst[idx, :] = _silu(_conv_tile(src, i, nc, cw[...], cb[...], q)) * mul
        for hh in range(ML_HP):
            kct_s[i, hh * dh:(hh + 1) * dh, :] = kc_s[idx, hh * dh:(hh + 1) * dh].T.astype(BF16)
        return 0
    lax.fori_loop(0, nc, prep, 0)

    gb = gb_ref[0]
    lane = lax.broadcasted_iota(jnp.int32, (q, 128), 1)
    eye8 = (lax.broadcasted_iota(jnp.int32, (8, 128), 0) == lax.broadcasted_iota(jnp.int32, (8, 128), 1)).astype(F32)
    o_ref[...] = jnp.zeros_like(o_ref)
    heads = range(ML_HP)
    assert q % dh == 0
    wide = lambda col: jnp.concatenate([col] * (q // dh), axis=1)

    def direction(d, j, states):
        rev = d == 1
        causal = _tri(q, rev)
        last = 0 if rev else q - 1
        if rev:
            ci = jnp.where(j < first_lat, first_lat - 1 - j, nc - 1 - (j - first_lat))
        else:
            ci = j
        idx = pl.ds(pl.multiple_of(ci * q, q), q)
        g = g_ref[0, idx, :] + gb
        logf = jnp.minimum(g, 0.0) - jnp.log(1.0 + jnp.exp(-jnp.abs(g)))
        is_f = functools.reduce(jnp.logical_or, [lane == _ml_gate_lanes(hh, d)[1] for hh in heads])
        cum = _dot_sel(jnp.where(causal, 1.0, 0.0), jnp.where(is_f, logf, 0.0))
        qc = [qc_s[idx, hh * dh:(hh + 1) * dh] for hh in heads]
        kc = [kc_s[idx, hh * dh:(hh + 1) * dh] for hh in heads]
        vc = [v_ref[0, idx, hh * dh:(hh + 1) * dh] for hh in heads]
        kct = [kct_s[ci, hh * dh:(hh + 1) * dh, :] for hh in heads]
        qk = [_dot(qc[hh], kct[hh]) for hh in heads]
        inter_c = [_dot(qc[hh], states[hh][0]) for hh in heads]
        yield
        rows = _dot_sel(eye8, jnp.where(is_f, cum, g), _NT)
        yield
        w, w_s, g_in, m_t, keep, m_new, wv, upd = [], [], [], [], [], [], [], []
        for hh in heads:
            li, lf = _ml_gate_lanes(hh, d)
            m_prev = states[hh][2]
            i_col = jnp.broadcast_to(g[:, li:li + 1], (q, dh))
            b_col = jnp.broadcast_to(cum[:, lf:lf + 1], (q, dh))
            i_row, b_row = rows[li:li + 1, :], rows[lf:lf + 1, :]
            dlog = jnp.where(causal, wide(b_col) - b_row + i_row, NEG)
            inter = b_col + m_prev
            m_t.append(jnp.maximum(jnp.max(dlog, axis=-1, keepdims=True), inter))
            w.append(jnp.exp(dlog - wide(m_t[hh])) * qk[hh])
            g_in.append(jnp.exp(inter - m_t[hh]))
            b_end = b_col[last:last + 1, :]
            g_s = b_end - b_col + i_col
            m_new.append(jnp.maximum(jnp.max(g_s, axis=0, keepdims=True), b_end + m_prev))
            w_s.append(jnp.exp(g_s - m_new[hh]))
            keep.append(jnp.exp(b_end + m_prev - m_new[hh]))
            wv.append(_dot(w[hh], vc[hh]))
            upd.append(_dot(kct[hh], w_s[hh] * vc[hh]))
        yield
        new_states = []
        for hh in heads:
            c_prev, n_prev, _ = states[hh]
            num = wv[hh] + g_in[hh] * inter_c[hh]
            den = (jnp.sum(w[hh], axis=-1, keepdims=True)
                   + g_in[hh] * jnp.sum(qc[hh] * n_prev, axis=-1, keepdims=True))
            o_ref[0, idx, hh * dh:(hh + 1) * dh] += num / jnp.maximum(jnp.abs(den), jnp.exp(-m_t[hh]))
            new_states.append((keep[hh] * c_prev + upd[hh],
                               keep[hh] * n_prev + jnp.sum(w_s[hh] * kc[hh], axis=0, keepdims=True), m_new[hh]))
        return tuple(new_states)

    def chunk(j, states):
        return tuple(_interleave([direction(d, j, states[d]) for d in (0, 1)]))
    init = (jnp.zeros((dh, dh), F32), jnp.zeros((1, dh), F32), jnp.full((1, dh), NEG, F32))
    lax.fori_loop(0, nc, chunk, tuple(tuple(init for _ in heads) for _ in (0, 1)))


def _mlstm(proj, lanes, conv_w, conv_b, gate_b, cols):
    bsz, L, _ = proj.shape
    nc = L // CHUNK
    cq, ck, cv, cg = cols
    nh = gate_b.shape[-1]
    dh = conv_w.shape[1] // (2 * nh)
    bw = ML_HP * dh
    gcols = np.asarray(_ml_gate_columns(nh))
    gbl = jnp.zeros((nh // ML_HP, 1, 128), F32).at[:, 0, :gcols.shape[1]].set(gate_b.reshape(-1)[gcols])
    return pl.pallas_call(
        functools.partial(_mlstm_kernel, nc=nc, dh=dh),
        grid=(bsz, nh // ML_HP),
        in_specs=[pl.BlockSpec((1, L, bw), lambda b, h: (b, 0, cq // bw + h)),
                  pl.BlockSpec((1, L, bw), lambda b, h: (b, 0, ck // bw + h)),
                  pl.BlockSpec((1, L, bw), lambda b, h: (b, 0, cv // bw + h)),
                  pl.BlockSpec((1, L, 128), lambda b, h: (b, 0, cg // 128 + h)),
                  pl.BlockSpec((CONV_W, bw), lambda b, h: (0, h)),
                  pl.BlockSpec((1, bw), lambda b, h: (0, h)),
                  pl.BlockSpec((CONV_W, bw), lambda b, h: (0, nh // ML_HP + h)),
                  pl.BlockSpec((1, bw), lambda b, h: (0, nh // ML_HP + h)),
                  pl.BlockSpec((1, 1, 128), lambda b, h: (h, 0, 0))],
        out_specs=pl.BlockSpec((1, L, bw), lambda b, h: (b, 0, h)),
        out_shape=jax.ShapeDtypeStruct((bsz, L, nh * dh), F32),
        scratch_shapes=[pltpu.VMEM((L, bw), F32), pltpu.VMEM((L, bw), F32), pltpu.VMEM((nc, bw, CHUNK), BF16)],
        compiler_params=_cp(("parallel", "parallel")), name="mlstm",
    )(proj, proj, proj, lanes, conv_w, conv_b.reshape(1, -1), conv_w, conv_b.reshape(1, -1), gbl)


def kernel(x, c, ctx, c_ctx, mod_w, mod_b, norm1_g, norm2_g, even_w_in, even_w_out, lru_conv_w, lru_conv_b, lru_wa, lru_ba, lru_wx, lru_bx, lru_lam, na_q_g, na_k_g, na_rpb, odd_w_in, odd_w_out, ssd_conv_w, ssd_conv_b, ssd_dt_bias, ssd_a_log, ssd_d, ssd_norm_g, ml_conv_w, ml_conv_b, ml_gate_b, ml_norm_g, moe_router_g, moe_router_e, moe_w1, moe_w3, moe_w2):
    bsz, S, d = x.shape
    lc = ctx.shape[1]
    assert lc == TM and S % TM == 0 and mod_w.shape[0] == 2
    nt = (lc + S) // TM
    L = lc + S

    cc = jnp.zeros((8, d), F32).at[0].set(c_ctx).at[1:1 + bsz].set(c)
    mod_all = _modulation(cc, mod_w, mod_b)

    def mod_for(l):
        m = mod_all[l].reshape(8, 6, d)
        return jnp.stack([jnp.broadcast_to(m[0], (bsz, 6, d)), m[1:1 + bsz]], axis=1)

    def router_w(l):
        w = jnp.zeros((d, 128), F32).at[:, :EXPERT_LANE0].set(moe_router_g[l]) \
            .at[:, EXPERT_LANE0:EXPERT_LANE0 + N_EXPERTS].set(moe_router_e[l])
        hi = w.astype(BF16)
        return jnp.concatenate([hi, (w - hi.astype(F32)).astype(BF16)], axis=1)

    xx = (ctx, x)

    mod0 = mod_for(0)
    (proj,) = _inproj(xx, mod0, norm1_g[0], even_w_in[0].astype(BF16))
    lw = lru_conv_w.shape[-1]
    r = _lru(proj, lru_conv_w[0], lru_conv_b[0], lru_wa[0], lru_ba[0], lru_wx[0], lru_bx[0], lru_lam[0])
    a = _na(proj, na_q_g[0], na_k_g[0], na_rpb[0], col0=2 * lw)
    x1, h2, gate = _outproj([(r, 0, lw), (proj, 1, lw), (a, 0, lw)], xx, even_w_out[0].astype(BF16), mod0,
                            norm2_g[0], router_w(0), even=True, tile0=0, ntiles=nt)
    ew = (moe_w1.astype(BF16), moe_w3.astype(BF16), moe_w2.astype(BF16))
    moe0 = _moe(h2.reshape(bsz * L, d), gate.reshape(bsz * L, 128), 0, *ew).reshape(bsz, L, d)

    mod1 = mod_for(1)
    sw = ssd_d.shape[-1] * 64
    xbc = ssd_conv_w.shape[-1]
    mw = ml_norm_g.shape[-1]
    w = odd_w_in[0]
    o = np.cumsum([0, sw, xbc, 2 * ssd_d.shape[-1], mw, mw, mw, mw])
    def lane_blocks(col0, groups):
        return [jnp.concatenate([w[:, col0 + np.asarray(blk)], jnp.zeros((d, 128 - len(blk)), F32)], axis=1)
                for blk in groups]
    dt_blks = lane_blocks(o[2], _ssd_dt_columns(ssd_d.shape[-1]))
    gate_blks = lane_blocks(o[7], _ml_gate_columns(ml_gate_b.shape[-1]))
    w_odd = jnp.concatenate([w[:, :o[2]], w[:, o[3]:o[7]]] + dt_blks + gate_blks, axis=1).astype(BF16)
    cz, cxs = 0, sw
    cB, cC = cxs + sw, cxs + sw + (xbc - sw) // 2
    cq = sw + xbc
    ck, cv, co, csm = cq + mw, cq + 2 * mw, cq + 3 * mw, cq + 4 * mw
    x0, proj1, lanes1 = _inproj(x1, mod1, norm1_g[1], w_odd, prev=(moe0, mod0), n_f32=w_odd.shape[1] - csm)
    ys = _ssd(proj1, lanes1, ssd_conv_w[0], ssd_conv_b[0], ssd_dt_bias[0], ssd_a_log[0], ssd_d[0], (cxs, cB, cC, 0))
    hm = _mlstm(proj1, lanes1, ml_conv_w[0], ml_conv_b[0], ml_gate_b[0], (cq, ck, cv, 128 * len(dt_blks)))
    x2, h2b, gate1 = _outproj([(ys, 0, sw), (proj1, cz // sw, sw), (hm, 0, mw), (proj1, co // mw, mw),
                               (ssd_norm_g[0].reshape(1, sw), 0, sw), (ml_norm_g[0].reshape(1, mw), 0, mw)],
                              x0, odd_w_out[0].astype(BF16), mod1, norm2_g[1], router_w(1),
                              even=False, tile0=1, ntiles=nt - 1)
    out = _moe(h2b.reshape(bsz * S, d), gate1.reshape(bsz * S, 128), 1, *ew, resid=(x2.reshape(bsz * S, d), mod1))
    return out.reshape(bsz, S, d)
```

```python
import functools
import math

import jax
import jax.numpy as jnp
import numpy as np
from jax import lax
from jax.experimental import pallas as pl
from jax.experimental.pallas import tpu as pltpu

F32 = jnp.float32
BF16 = jnp.bfloat16
HI = lax.Precision.HIGHEST

EPS = 1e-6
NEG = -1e30
GRID_W = 64
CONV_W = 4
LRU_C = 8.0
TM = 256
CHUNK = 256
NA_RQ = 4
NA_RK = 12
N_EXPERTS = 16
EXPERT_LANE0 = 4
VMEM_LIMIT = 56 * 1024 * 1024


def _cp(sem, vmem=VMEM_LIMIT):
    return pltpu.CompilerParams(dimension_semantics=sem, vmem_limit_bytes=vmem)


def _sigmoid(x):
    return jax.nn.sigmoid(x)


def _silu(x):
    return x * jax.nn.sigmoid(x)


def _softplus(x):
    return jnp.maximum(x, 0.0) + jnp.log(1.0 + jnp.exp(-jnp.abs(x)))


def _gelu_tanh(x):
    return 0.5 * x * (1.0 + jnp.tanh(math.sqrt(2.0 / math.pi) * (x + 0.044715 * (x * x * x))))


def _rms(x, axis=-1):
    return x * lax.rsqrt(jnp.mean(x * x, axis=axis, keepdims=True) + EPS)


def _dot(a, b):
    return jnp.dot(a.astype(BF16), b.astype(BF16), preferred_element_type=F32)


def _dot_hi(a, b):
    return jnp.dot(a, b, precision=HI, preferred_element_type=F32)


def _split3(x):
    x1 = x.astype(BF16)
    r = x - x1.astype(F32)
    x2 = r.astype(BF16)
    x3 = (r - x2.astype(F32)).astype(BF16)
    return x1, x2, x3


def _dot_sel(sel, x, dims=(((1,), (0,)), ((), ()))):
    s = sel.astype(BF16)
    x1, x2, x3 = _split3(x)
    d = lambda xi: lax.dot_general(s, xi, dims, preferred_element_type=F32)
    return (d(x3) + d(x2)) + d(x1)


def _dot_sel_r(x, sel):
    s = sel.astype(BF16)
    x1, x2, x3 = _split3(x)
    d = lambda xi: jnp.dot(xi, s, preferred_element_type=F32)
    return (d(x3) + d(x2)) + d(x1)


_NT = (((1,), (1,)), ((), ()))


def _interleave(gens):
    results = [None] * len(gens)
    live = list(range(len(gens)))
    while live:
        for i in list(live):
            try:
                next(gens[i])
            except StopIteration as stop:
                results[i] = stop.value
                live.remove(i)
    return results


def _dot_nt(a, b):
    return lax.dot_general(a.astype(BF16), b.astype(BF16), (((1,), (1,)), ((), ())),
                           preferred_element_type=F32)


def _mod_kernel(c_ref, w_ref, b_ref, o_ref):
    c = c_ref[...]
    o_ref[0] = _dot_hi(_silu(c), w_ref[0]) + b_ref[0]


def _modulation(cc, mod_w, mod_b):
    depth, d, n = mod_w.shape
    tn = 1536
    return pl.pallas_call(
        _mod_kernel,
        grid=(depth, n // tn),
        in_specs=[pl.BlockSpec((8, d), lambda l, j: (0, 0)),
                  pl.BlockSpec((1, d, tn), lambda l, j: (l, 0, j)),
                  pl.BlockSpec((1, 1, tn), lambda l, j: (l, 0, j))],
        out_specs=pl.BlockSpec((1, 8, tn), lambda l, j: (l, 0, j)),
        out_shape=jax.ShapeDtypeStruct((depth, 8, n), F32),
        compiler_params=_cp(("arbitrary", "arbitrary")),
        name="adaln_mod",
    )(cc, mod_w, mod_b.reshape(depth, 1, n))


def _inproj_kernel(*refs, fuse_prev, n_f32):
    refs = list(refs)
    s_ref = refs.pop() if n_f32 else None
    j = pl.program_id(1)
    if fuse_prev:
        x_ref, mo_ref, pmod_ref, mod_ref, g_ref, w_ref, xo_ref, p_ref = refs
    else:
        c_ref, xa_ref, xb_ref, mod_ref, g_ref, w_ref, p_ref = refs
    hs = []
    for s in range(2):
        rs = slice(s * TM, (s + 1) * TM)
        seg = 1 if s else jnp.minimum(j, 1)
        mod = mod_ref[0, seg]
        if fuse_prev:
            x = x_ref[0, rs, :] + pmod_ref[0, seg][5:6, :] * mo_ref[0, rs, :]
            xo_ref[0, rs, :] = x
        elif s == 0:
            x = jnp.where(j == 0, c_ref[0], xa_ref[0])
        else:
            x = xb_ref[0]
        hs.append(_rms(x) * g_ref[...] * (1.0 + mod[1:2, :]) + mod[0:1, :])
    p = _dot(jnp.concatenate(hs, axis=0), w_ref[...])
    n_main = p.shape[-1] - n_f32
    p_ref[0] = p[:, :n_main].astype(BF16)
    if n_f32:
        s_ref[0] = p[:, n_main:]


def _segment_specs(d):
    return [pl.BlockSpec((1, TM, d), lambda b, i: (b, 0, 0)),
            pl.BlockSpec((1, TM, d), lambda b, i: (b, jnp.maximum(i - 1, 0), 0))]


def _pick_segment(c_ref, x_ref):
    return jnp.where(pl.program_id(1) == 0, c_ref[0], x_ref[0])


def _inproj(x, mod, g, w, prev=None, n_f32=0):
    if prev is None:
        ctx, lat = x
        bsz, S, d = lat.shape
        L = TM + S
    else:
        bsz, L, d = x.shape
    n = w.shape[1]
    nt = L // TM
    steps = (nt + 1) // 2
    tok2 = pl.BlockSpec((1, 2 * TM, d), lambda b, j: (b, j, 0))
    modspec = pl.BlockSpec((1, 2, 6, d), lambda b, j: (b, 0, 0, 0))
    tail = [modspec, pl.BlockSpec((1, d), lambda b, j: (0, 0)), pl.BlockSpec((d, n), lambda b, j: (0, 0))]
    pspecs = [pl.BlockSpec((1, 2 * TM, n - n_f32), lambda b, j: (b, j, 0))]
    pshapes = [jax.ShapeDtypeStruct((bsz, L, n - n_f32), BF16)]
    if n_f32:
        pspecs.append(pl.BlockSpec((1, 2 * TM, n_f32), lambda b, j: (b, j, 0)))
        pshapes.append(jax.ShapeDtypeStruct((bsz, L, n_f32), F32))
    if prev is None:
        last_lat = S // TM - 1
        segs = [pl.BlockSpec((1, TM, d), lambda b, j: (b, 0, 0)),
                pl.BlockSpec((1, TM, d), lambda b, j: (b, jnp.maximum(2 * j - 1, 0), 0)),
                pl.BlockSpec((1, TM, d), lambda b, j: (b, jnp.minimum(2 * j, last_lat), 0))]
        return pl.pallas_call(
            functools.partial(_inproj_kernel, fuse_prev=False, n_f32=n_f32),
            grid=(bsz, steps), in_specs=segs + tail, out_specs=pspecs, out_shape=pshapes,
            compiler_params=_cp(("parallel", "arbitrary")), name="inproj",
        )(ctx, lat, lat, mod, g.reshape(1, d), w)
    moe_out, pmod = prev
    return pl.pallas_call(
        functools.partial(_inproj_kernel, fuse_prev=True, n_f32=n_f32),
        grid=(bsz, steps), in_specs=[tok2, tok2, modspec] + tail,
        out_specs=[tok2] + pspecs, out_shape=[jax.ShapeDtypeStruct(x.shape, F32)] + pshapes,
        compiler_params=_cp(("parallel", "parallel")), name="inproj_res",
    )(x, moe_out, pmod, mod, g.reshape(1, d), w)


def _conv_tile(ref, i, nt, cw, cb, width=TM):
    L = nt * width
    t0 = pl.multiple_of(i * width, width)
    cur = ref[0, pl.ds(t0, width), :].astype(F32)
    prev = ref[0, pl.ds(pl.multiple_of(jnp.maximum(t0 - 16, 0), 16), 16), :].astype(F32)[8:16]
    nxt = ref[0, pl.ds(pl.multiple_of(jnp.minimum(t0 + width, L - 16), 16), 16), :].astype(F32)[0:8]
    return _conv_vals(cur, prev, nxt, i, nt, cw, cb, width, first_lat=TM // width)


def _conv_vals(cur, prev, nxt, i, nt, cw, cb, width, first_lat):
    prev = jnp.where((i != 0) & (i != first_lat), prev, 0.0)
    nxt = jnp.where((i != first_lat - 1) & (i != nt - 1), nxt, 0.0)
    cat = jnp.concatenate([prev, cur, nxt], axis=0)
    return (cw[0:1] * cat[6:6 + width] + cw[1:2] * cat[7:7 + width] + cw[2:3] * cur
            + cw[3:4] * cat[9:9 + width] + cb)


def _lru_kernel(ux_ref, cw_ref, cb_ref, gw_ref, gb_ref, lam_ref, o_ref, ob_s, *, nt):
    cw = cw_ref[...]
    cb = cb_ref[...]
    lam = lam_ref[0]
    sp = _softplus(-lam)
    row = lax.broadcasted_iota(jnp.int32, (TM, 128), 0) & 7

    def gates(i, d):
        xl = _conv_tile(ux_ref, i, nt, cw, cb)
        g = _dot(xl, gw_ref[0, d]) + gb_ref[0, d]
        r = _sigmoid(g[:, :128])
        ig = _sigmoid(g[:, 128:])
        log_a = -LRU_C * r * sp[d:d + 1]
        a = jnp.exp(log_a)
        u = jnp.sqrt(1.0 - a * a) * (ig * xl)
        return a, u

    def scan_tile(i, d, carry):
        a, u = gates(i, d)
        rev = d == 1
        yield
        for k in (1, 2, 4):
            sh = 8 - k if rev else k
            ok = (row < 8 - k) if rev else (row >= k)
            ash = pltpu.roll(a.reshape(TM // 8, 8, 128), sh, 1).reshape(TM, 128)
            ush = pltpu.roll(u.reshape(TM // 8, 8, 128), sh, 1).reshape(TM, 128)
            u = jnp.where(ok, u + a * ush, u)
            a = jnp.where(ok, a * ash, a)
            yield
        t0 = i * TM
        groups = range(TM // 8)
        for n, s in enumerate(reversed(groups) if rev else groups):
            h = u[s * 8:(s + 1) * 8] + a[s * 8:(s + 1) * 8] * carry
            carry = h[0:1] if rev else h[7:8]
            idx = pl.ds(pl.multiple_of(t0 + s * 8, 8), 8)
            if rev:
                ob_s[idx, :] = h
            else:
                o_ref[0, idx, :] = h
            if n % 4 == 3:
                yield
        return carry

    def both(j, carries):
        jb = jnp.where(j == 0, 0, nt - j)
        return tuple(_interleave([scan_tile(j, 0, carries[0]), scan_tile(jb, 1, carries[1])]))
    zero = jnp.zeros((1, 128), F32)
    lax.fori_loop(0, nt, both, (zero, zero))

    def combine(i, _):
        idx = pl.ds(pl.multiple_of(i * TM, TM), TM)
        o_ref[0, idx, :] += ob_s[idx, :]
        return 0
    lax.fori_loop(0, nt, combine, 0)


def _lru(proj, conv_w, conv_b, wa, ba, wx, bx, lam):
    bsz, L, _ = proj.shape
    nt = L // TM
    width = conv_w.shape[1]
    ng = width // 128

    def blockdiag(w):
        w = w.reshape(2, ng, 2, 64, 64)
        z = jnp.zeros_like(w[:, :, 0])
        top = jnp.concatenate([w[:, :, 0], z], axis=-1)
        bot = jnp.concatenate([z, w[:, :, 1]], axis=-1)
        return jnp.concatenate([top, bot], axis=-2)
    gw = jnp.concatenate([blockdiag(wa), blockdiag(wx)], axis=-1).transpose(1, 0, 2, 3).astype(BF16)
    gb = jnp.concatenate([ba.reshape(2, ng, 1, 128), bx.reshape(2, ng, 1, 128)], axis=-1).transpose(1, 0, 2, 3)
    lam_g = lam.reshape(2, ng, 128).transpose(1, 0, 2)
    return pl.pallas_call(
        functools.partial(_lru_kernel, nt=nt),
        grid=(bsz, ng),
        in_specs=[pl.BlockSpec((1, L, 128), lambda b, c: (b, 0, c)),
                  pl.BlockSpec((CONV_W, 128), lambda b, c: (0, c)),
                  pl.BlockSpec((1, 128), lambda b, c: (0, c)),
                  pl.BlockSpec((1, 2, 128, 256), lambda b, c: (c, 0, 0, 0)),
                  pl.BlockSpec((1, 2, 1, 256), lambda b, c: (c, 0, 0, 0)),
                  pl.BlockSpec((1, 2, 128), lambda b, c: (c, 0, 0))],
        out_specs=pl.BlockSpec((1, L, 128), lambda b, c: (b, 0, c)),
        out_shape=jax.ShapeDtypeStruct((bsz, L, width), F32),
        scratch_shapes=[pltpu.VMEM((L, 128), F32)],
        compiler_params=_cp(("parallel", "parallel")), name="rglru",
    )(proj, conv_w, conv_b.reshape(1, width), gw, gb, lam_g)


def _na_bias_table(rpb, rows):
    nh = rpb.shape[0]
    win_c = (rpb.shape[2] + 1) // 2
    qc = np.arange(GRID_W)[:, None]
    kc = np.arange(GRID_W)[None, :]
    cstart = np.clip(qc - win_c // 2, 0, GRID_W - win_c)
    col_ok = (kc >= cstart) & (kc < cstart + win_c)
    dcol = np.clip(kc - qc + (win_c - 1), 0, 2 * win_c - 2)
    oc = (np.arange(2 * win_c - 1)[:, None, None] == dcol[None]).astype(np.float32)
    ct = jnp.where(jnp.asarray(col_ok), jnp.einsum('hrc,cqk->hrqk', rpb.astype(F32), jnp.asarray(oc), precision=HI),
                   NEG)
    return pl.pallas_call(
        functools.partial(_na_bias_kernel, rows=rows),
        grid=(nh,),
        in_specs=[pl.BlockSpec((1,) + ct.shape[1:], lambda h: (h, 0, 0, 0))],
        out_specs=pl.BlockSpec((3, 1, TM, NA_RK * GRID_W), lambda h: (0, h, 0, 0)),
        out_shape=jax.ShapeDtypeStruct((3, nh, TM, NA_RK * GRID_W), F32),
        compiler_params=_cp(("parallel",)), name="nbr_bias",
    )(ct)


def _na_bias_kernel(ct_ref, o_ref, *, rows):
    win_r = (ct_ref.shape[1] + 1) // 2
    neg = jnp.full((GRID_W, GRID_W), NEG, F32)
    for p, (r0, w0) in enumerate(((0, 0), (2 * NA_RQ, NA_RQ), (rows - NA_RQ, rows - NA_RK))):
        for a in range(NA_RQ):
            r = r0 + a
            rstart = min(max(r - win_r // 2, 0), rows - win_r)
            tiles = [ct_ref[0, w0 + b - r + win_r - 1] if rstart <= w0 + b < rstart + win_r else neg
                     for b in range(NA_RK)]
            for bp in range(NA_RK // 2):
                o_ref[p, 0, a * GRID_W:(a + 1) * GRID_W, bp * 2 * GRID_W:(bp + 1) * 2 * GRID_W] = (
                    jnp.concatenate(tiles[2 * bp:2 * bp + 2], axis=1))


def _pair_rms(x, lo):
    x2 = x * x
    s0 = jnp.sum(jnp.where(lo, x2, 0.0), axis=-1, keepdims=True)
    s1 = jnp.sum(jnp.where(lo, 0.0, x2), axis=-1, keepdims=True)
    return x * lax.rsqrt(jnp.where(lo, s0, s1) * (2.0 / x.shape[-1]) + EPS)


def _na_kernel(q_ref, k_ref, v_ref, bias_ref, qg_ref, kg_ref, o_ref, kn_s, vb_s, *, rows, hd):
    j = pl.program_id(2)
    L = k_ref.shape[1]
    nb = L // TM
    nkeys = NA_RK * GRID_W
    lo = lax.broadcasted_iota(jnp.int32, (1, 2 * hd), 1) < hd

    @pl.when(j == 0)
    def _prep():
        def body(t, _):
            idx = pl.ds(pl.multiple_of(t * TM, TM), TM)
            kn_s[idx, :] = (_pair_rms(k_ref[0, idx, :].astype(F32), lo) * kg_ref[...]).astype(BF16)
            vb_s[idx, :] = v_ref[0, idx, :].astype(BF16)
            return 0
        lax.fori_loop(0, nb, body, 0)

    kctx = kn_s[0:TM, :]
    vctx = vb_s[0:TM, :]
    qscale = qg_ref[...] * hd ** -0.5

    def head(q, bias, kwin, vwin):
        s_c = _dot_nt(q, kctx)
        if kwin is not None:
            s_w = _dot_nt(q, kwin) + bias
        yield
        m = jnp.max(s_c, axis=-1, keepdims=True)
        if kwin is not None:
            m = jnp.maximum(m, jnp.max(s_w, axis=-1, keepdims=True))
            p_w = jnp.exp(s_w - m)
        p_c = jnp.exp(s_c - m)
        den = jnp.sum(p_c, axis=-1, keepdims=True)
        num = _dot(p_c, vctx)
        if kwin is not None:
            den = den + jnp.sum(p_w, axis=-1, keepdims=True)
            num = num + _dot(p_w, vwin)
        yield
        return num / den

    def tile_chains(slot, tile):
        qn = _pair_rms(q_ref[0, slot * TM:(slot + 1) * TM, :].astype(F32), lo) * qscale
        q_h = [jnp.where(lo, qn, 0.0).astype(BF16), jnp.where(lo, 0.0, qn).astype(BF16)]
        if tile is None:
            return [head(q_h[hh], None, None, None) for hh in range(2)]
        w0 = jnp.clip((tile - 1) * NA_RQ - NA_RQ, 0, rows - NA_RK)
        start = pl.multiple_of(TM + w0 * GRID_W, GRID_W)
        kwin = kn_s[pl.ds(start, nkeys), :]
        vwin = vb_s[pl.ds(start, nkeys), :]
        pat = jnp.where(tile <= 1, 0, jnp.where(tile >= nb - 1, 2, 1))
        return [head(q_h[hh], bias_ref[pat, hh], kwin, vwin) for hh in range(2)]

    def run(chains):
        outs = _interleave(chains)
        for slot in range(len(outs) // 2):
            o_ref[0, slot * TM:(slot + 1) * TM, :] = jnp.where(lo, outs[2 * slot], outs[2 * slot + 1]).astype(BF16)

    last = (nb - 1) // 2

    @pl.when(j == 0)
    def _():
        run(tile_chains(0, None) + tile_chains(1, 1))

    @pl.when((j > 0) & (j < last))
    def _():
        run(tile_chains(0, 2 * j) + tile_chains(1, 2 * j + 1))

    @pl.when(j == last)
    def _():
        run(tile_chains(0, 2 * j))


def _na(proj, q_g, k_g, rpb, col0):
    bsz, L, _ = proj.shape
    nh = rpb.shape[0]
    hd = q_g.shape[0]
    width = nh * hd
    rows = (L - TM) // GRID_W
    nb = L // TM
    bias = _na_bias_table(rpb, rows)
    qb, kb, vb = col0 // 128, (col0 + width) // 128, (col0 + 2 * width) // 128

    assert nb % 2 == 1

    return pl.pallas_call(
        functools.partial(_na_kernel, rows=rows, hd=hd),
        grid=(width // 128, bsz, (nb + 1) // 2),
        in_specs=[pl.BlockSpec((1, 2 * TM, 128), lambda h, b, j: (b, j, qb + h)),
                  pl.BlockSpec((1, L, 128), lambda h, b, j: (b, 0, kb + h)),
                  pl.BlockSpec((1, L, 128), lambda h, b, j: (b, 0, vb + h)),
                  pl.BlockSpec((3, 2, TM, NA_RK * GRID_W), lambda h, b, j: (0, h, 0, 0)),
                  pl.BlockSpec((1, 2 * hd), lambda h, b, j: (0, 0)),
                  pl.BlockSpec((1, 2 * hd), lambda h, b, j: (0, 0))],
        out_specs=pl.BlockSpec((1, 2 * TM, 128), lambda h, b, j: (b, j, h)),
        out_shape=jax.ShapeDtypeStruct((bsz, L, width), BF16),
        scratch_shapes=[pltpu.VMEM((L, 128), BF16), pltpu.VMEM((L, 128), BF16)],
        compiler_params=_cp(("parallel", "parallel", "arbitrary")), name="nbr_attn",
    )(proj, proj, proj, bias, jnp.tile(q_g, 2).reshape(1, 2 * hd), jnp.tile(k_g, 2).reshape(1, 2 * hd))


def _route(lg):
    lane = lax.broadcasted_iota(jnp.int32, lg.shape, 1)
    lane_f = lane.astype(F32)
    is_g = lane < EXPERT_LANE0
    gl = jnp.where(is_g, lg, NEG)
    gmax = jnp.max(gl, axis=-1, keepdims=True)
    gsel = jnp.min(jnp.where(is_g & (gl == gmax), lane_f, 1e9), axis=-1, keepdims=True)
    g_w = 1.0 / jnp.sum(jnp.where(is_g, jnp.exp(gl - gmax), 0.0), axis=-1, keepdims=True)
    grp = ((lane - EXPERT_LANE0) >> 2).astype(F32)
    in_g = (lane >= EXPERT_LANE0) & (lane < EXPERT_LANE0 + N_EXPERTS) & (grp == gsel)
    el = jnp.where(in_g, lg, NEG)
    v1 = jnp.max(el, axis=-1, keepdims=True)
    i1 = jnp.min(jnp.where(in_g & (el == v1), lane_f, 1e9), axis=-1, keepdims=True)
    el2 = jnp.where(lane_f == i1, NEG, el)
    v2 = jnp.max(el2, axis=-1, keepdims=True)
    i2 = jnp.min(jnp.where(in_g & (lane_f != i1) & (el2 == v2), lane_f, 1e9), axis=-1, keepdims=True)
    t = jnp.exp(v2 - v1)
    w1 = g_w / (1.0 + t)
    w2 = g_w * t / (1.0 + t)
    return (jnp.where(lane_f == i1, w1, 0.0) + jnp.where(lane_f == i2, w2, 0.0)
            + jnp.where(lane_f == gsel, 1.0, 0.0))


def _outproj_kernel(*refs, even):
    if even:
        (r_ref, ug_ref, a_ref, c_ref, x_ref, w_ref, mod_ref, g2_ref, rw_ref, x1_ref, h2_ref, gate_ref) = refs
        is_ctx = pl.program_id(1) == 0
        x_rows = lambda rs: jnp.where(is_ctx, c_ref[0, rs, :], x_ref[0, rs, :])
    else:
        (ys_ref, z_ref, hm_ref, mo_ref, sg_ref, mg_ref, x_ref, w_ref, mod_ref, g2_ref, rw_ref,
         x1_ref, h2_ref, gate_ref) = refs
        x_rows = lambda rs: x_ref[0, rs, :]
    mod = mod_ref[0, 0]

    def part(rs):
        if even:
            y_in = jnp.concatenate([(r_ref[0, rs, :] * _gelu_tanh(ug_ref[0, rs, :].astype(F32))).astype(BF16),
                                    a_ref[0, rs, :]], axis=-1)
        else:
            ys = ys_ref[0, rs, :] * _silu(z_ref[0, rs, :].astype(F32))
            sg = sg_ref[...]
            mg = mg_ref[...]
            hm = hm_ref[0, rs, :]
            sig_o = _sigmoid(mo_ref[0, rs, :].astype(F32))
            gw = ys.shape[-1] // 2
            parts = [_rms(ys[:, g * gw:(g + 1) * gw]) * sg[:, g * gw:(g + 1) * gw] for g in range(2)]
            hw = 128
            parts += [_rms(hm[:, h * hw:(h + 1) * hw]) * mg[:, h * hw:(h + 1) * hw] * sig_o[:, h * hw:(h + 1) * hw]
                      for h in range(hm.shape[-1] // hw)]
            y_in = jnp.concatenate(parts, axis=-1)
        y = _dot(y_in, w_ref[...])
        yield
        x1 = x_rows(rs) + mod[2:3, :] * y
        x1_ref[0, rs, :] = x1
        h2 = _rms(x1) * g2_ref[...] * (1.0 + mod[4:5, :]) + mod[3:4, :]
        h2_ref[0, rs, :] = h2.astype(BF16)
        hi = h2.astype(BF16)
        lo = (h2 - hi.astype(F32)).astype(BF16)
        lg2 = jnp.dot(hi, rw_ref[...], preferred_element_type=F32)
        lg1 = jnp.dot(lo, rw_ref[:, 0:128], preferred_element_type=F32)
        yield
        gate_ref[0, rs, :] = _route(lg2[:, 0:128] + lg2[:, 128:256] + lg1)

    nparts = 2
    rows = TM // nparts
    _interleave([part(pl.ds(p * rows, rows)) for p in range(nparts)])


def _outproj(mix_inputs, x, w, mod, g2, rw, even, tile0, ntiles):
    xs = list(x) if even else [x]
    bsz, _, d = xs[-1].shape
    specs, args = [], []
    for arr, cb, wdt in mix_inputs:
        if arr.ndim == 3:
            specs.append(pl.BlockSpec((1, TM, wdt), lambda b, i, cb=cb: (b, i + tile0, cb)))
        else:
            specs.append(pl.BlockSpec((1, wdt), lambda b, i: (0, 0)))
        args.append(arr)
    specs += _segment_specs(d) if even else [pl.BlockSpec((1, TM, d), lambda b, i: (b, i + tile0, 0))]
    specs += [pl.BlockSpec(w.shape, lambda b, i: (0, 0)),
              pl.BlockSpec((1, 1, 6, d), lambda b, i: (b, jnp.minimum(i + tile0, 1), 0, 0)),
              pl.BlockSpec((1, d), lambda b, i: (0, 0)),
              pl.BlockSpec(rw.shape, lambda b, i: (0, 0))]
    args += xs + [w, mod, g2.reshape(1, d), rw]
    lo = ntiles * TM
    return pl.pallas_call(
        functools.partial(_outproj_kernel, even=even),
        grid=(bsz, ntiles), in_specs=specs,
        out_specs=[pl.BlockSpec((1, TM, d), lambda b, i: (b, i, 0)),
                   pl.BlockSpec((1, TM, d), lambda b, i: (b, i, 0)),
                   pl.BlockSpec((1, TM, 128), lambda b, i: (b, i, 0))],
        out_shape=[jax.ShapeDtypeStruct((bsz, lo, d), F32), jax.ShapeDtypeStruct((bsz, lo, d), BF16),
                   jax.ShapeDtypeStruct((bsz, lo, 128), F32)],
        compiler_params=_cp(("parallel", "parallel")), name="outproj_even" if even else "outproj_odd",
    )(*args)


MOE_CH = 128
GROUP_SIZE = 4
MOE_VMEM_LIMIT = 62 * 1024 * 1024
N_GROUPS = N_EXPERTS // GROUP_SIZE


def _moe_kernel(*refs, residual):
    if residual:
        x_ref, g_ref, w1_ref, w3_ref, w2_ref, x1_ref, mod_ref, o_ref, xs_s, ys_s, gs_s, pt_s, plan_s = refs
    else:
        x_ref, g_ref, w1_ref, w3_ref, w2_ref, o_ref, xs_s, ys_s, gs_s, pt_s, plan_s = refs
    grp = pl.program_id(1)
    tb = x_ref.shape[0]
    nch = xs_s.shape[0] // MOE_CH

    @pl.when(grp == 0)
    def _plan():
        g = g_ref[...]
        lane = lax.broadcasted_iota(jnp.int32, g.shape, 1)
        oh = jnp.where(lane < N_GROUPS, g, 0.0)
        earlier = (lax.broadcasted_iota(jnp.int32, (tb, tb), 0) > lax.broadcasted_iota(jnp.int32, (tb, tb), 1))
        rank = jnp.dot(jnp.where(earlier, 1.0, 0.0).astype(BF16), oh.astype(BF16),
                       preferred_element_type=F32)
        cnt = jnp.sum(oh, axis=0, keepdims=True)
        lane1 = lax.broadcasted_iota(jnp.int32, (1, 128), 1)
        off = jnp.int32(0)
        offv = jnp.zeros((1, 128), F32)
        for gi in range(N_GROUPS):
            n = jnp.sum(jnp.where(lane1 == gi, cnt, 0.0)).astype(jnp.int32)
            nchunks = (n + (MOE_CH - 1)) // MOE_CH
            plan_s[gi] = off // MOE_CH
            plan_s[N_GROUPS + gi] = nchunks
            offv = offv + jnp.where(lane1 == gi, off.astype(F32), 0.0)
            off = off + nchunks * MOE_CH
        pos_col = jnp.sum(oh * (rank + offv), axis=1, keepdims=True)
        posb = jnp.broadcast_to(pos_col, (tb, 128))
        pos_row = jnp.concatenate([posb[i * 128:(i + 1) * 128, :].T[0:1, :] for i in range(tb // 128)], axis=1)
        x = x_ref[...]
        g_hi = g.astype(BF16)
        g_lo = (g - g_hi.astype(F32)).astype(BF16)
        lane_c = lax.broadcasted_iota(jnp.int32, (tb, MOE_CH), 1).astype(F32)
        row_c = lax.broadcasted_iota(jnp.int32, (MOE_CH, tb), 0).astype(F32)
        used = off // MOE_CH

        def dispatch(c):
            sl = slice(c * MOE_CH, (c + 1) * MOE_CH)
            p = jnp.where(row_c + float(c * MOE_CH) == pos_row, 1.0, 0.0).astype(BF16)
            xs_s[sl, :] = jnp.dot(p, x, preferred_element_type=F32).astype(BF16)
            gs_s[sl, :] = (jnp.dot(p, g_hi, preferred_element_type=F32)
                           + jnp.dot(p, g_lo, preferred_element_type=F32))
        for c in range(nch):
            sl = slice(c * MOE_CH, (c + 1) * MOE_CH)
            pt_s[:, sl] = jnp.where(pos_col == lane_c + float(c * MOE_CH), 1.0, 0.0).astype(BF16)
            if c < tb // MOE_CH:
                dispatch(c)
            else:
                pl.when(c < used)(functools.partial(dispatch, c))
        ys_s[...] = jnp.zeros_like(ys_s)

    c0 = plan_s[grp]
    nchunks = plan_s[N_GROUPS + grp]

    def ffn(chunk0, nrows):
        rows = pl.ds(pl.multiple_of(chunk0 * MOE_CH, MOE_CH), nrows)
        xs = xs_s[rows, :]
        gs = gs_s[rows, :]
        lane_g = lax.broadcasted_iota(jnp.int32, (nrows, 128), 1)
        acc = None
        for k in range(GROUP_SIZE):
            a = jnp.dot(xs, w1_ref[0, k], preferred_element_type=F32)
            b = jnp.dot(xs, w3_ref[0, k], preferred_element_type=F32)
            y = _dot(_silu(a) * b, w2_ref[0, k])
            ge = jnp.sum(jnp.where(lane_g == grp * GROUP_SIZE + k + EXPERT_LANE0, gs, 0.0), axis=-1, keepdims=True)
            acc = ge * y if acc is None else acc + ge * y
        ys_s[rows, :] = acc.astype(BF16)

    wide = 4

    def full(p, _):
        ffn(c0 + wide * p, wide * MOE_CH)
        return 0
    lax.fori_loop(0, nchunks // wide, full, 0)
    rem = nchunks % wide
    for r in range(1, wide):
        pl.when(rem == r)(functools.partial(ffn, c0 + nchunks - r, r * MOE_CH))

    @pl.when(grp == N_GROUPS - 1)
    def _combine():
        out = jnp.dot(pt_s[...], ys_s[...], preferred_element_type=F32)
        if residual:
            out = x1_ref[...] + mod_ref[0, 0][5:6, :] * out
        o_ref[...] = out


def _moe(h2, gate, layer, w1, w3, w2, resid=None):
    t, d = h2.shape
    _, ne, _, ff = w1.shape
    tb = math.gcd(t, 1024)
    npad = tb + N_GROUPS * MOE_CH
    assert ne == N_EXPERTS
    extra_specs, extra_args = [], []
    if resid is not None:
        x1, mod = resid
        per_batch = t // mod.shape[0] // tb
        extra_specs = [pl.BlockSpec((tb, d), lambda i, e: (i, 0)),
                       pl.BlockSpec((1, 1, 6, d), lambda i, e: (i // per_batch, 1, 0, 0))]
        extra_args = [x1, mod]
    return pl.pallas_call(
        functools.partial(_moe_kernel, residual=resid is not None),
        grid=(t // tb, N_GROUPS),
        in_specs=[pl.BlockSpec((tb, d), lambda i, e: (i, 0)),
                  pl.BlockSpec((tb, 128), lambda i, e: (i, 0)),
                  pl.BlockSpec((1, GROUP_SIZE, d, ff), lambda i, e: (layer, e, 0, 0)),
                  pl.BlockSpec((1, GROUP_SIZE, d, ff), lambda i, e: (layer, e, 0, 0)),
                  pl.BlockSpec((1, GROUP_SIZE, ff, d), lambda i, e: (layer, e, 0, 0))] + extra_specs,
        out_specs=pl.BlockSpec((tb, d), lambda i, e: (i, 0)),
        out_shape=jax.ShapeDtypeStruct((t, d), F32),
        scratch_shapes=[pltpu.VMEM((npad, d), BF16), pltpu.VMEM((npad, d), BF16), pltpu.VMEM((npad, 128), F32),
                        pltpu.VMEM((tb, npad), BF16), pltpu.SMEM((2 * N_GROUPS,), jnp.int32)],
        compiler_params=_cp(("parallel", "arbitrary"), MOE_VMEM_LIMIT), name="moe",
    )(h2, gate, w1, w3, w2, *extra_args)


def _tri(q, rev):
    r = lax.broadcasted_iota(jnp.int32, (q, q), 0)
    c = lax.broadcasted_iota(jnp.int32, (q, q), 1)
    return (c >= r) if rev else (c <= r)


def _ssd_dt_columns(nh):
    return [[d * nh + 4 * g + r for d in range(2) for r in range(4)] for g in range(nh // 4)]


def _ssd_kernel(xs_ref, b_ref, c_ref, dt_ref, cwx_ref, cbx_ref, cwb_ref, cbb_ref, cwc_ref, cbc_ref,
                dtb_ref, alog_ref, dsk_ref, o_ref, xc_s, bc_s, cc_s, dt_s, bct_s, *, nc):
    q = CHUNK
    first_lat = TM // q

    dtb = dtb_ref[0]

    def prep(i, _):
        t0 = pl.multiple_of(i * q, q)
        idx = pl.ds(t0, q)
        for src, cw, cb, dst in ((xs_ref, cwx_ref, cbx_ref, xc_s), (b_ref, cwb_ref, cbb_ref, bc_s),
                                 (c_ref, cwc_ref, cbc_ref, cc_s)):
            dst[idx, :] = _silu(_conv_tile(src, i, nc, cw[...], cb[...], q))
        dt_s[idx, :] = _softplus(dt_ref[0, idx, :] + dtb)
        o_ref[0, idx, :] = dsk_ref[...] * xc_s[idx, :]
        bct_s[i] = bc_s[idx, :].T.astype(BF16)
        return 0
    lax.fori_loop(0, nc, prep, 0)

    low_half = lax.broadcasted_iota(jnp.int32, (1, 128), 1) < 64

    def per_head(t, d):
        col = [jnp.broadcast_to(t[:, 4 * d + r:4 * d + r + 1], (t.shape[0], 128)) for r in range(4)]
        return jnp.concatenate([jnp.where(low_half, col[0], col[1]), jnp.where(low_half, col[2], col[3])], axis=1)

    a_lane = -jnp.exp(alog_ref[0])
    a_e = [per_head(a_lane, d) for d in (0, 1)]
    lane256 = lax.broadcasted_iota(jnp.int32, (q, 256), 1) // 64
    r8 = lax.broadcasted_iota(jnp.int32, (8, 256), 0)
    l8 = lax.broadcasted_iota(jnp.int32, (8, 256), 1)
    head_rows = (l8 == r8 * 64).astype(F32)

    def chain(d, j, state_in, state_out):
        rev = d == 1
        causal = _tri(q, rev)
        tri = causal.astype(F32)
        last = 0 if rev else q - 1
        if rev:
            ci = jnp.where(j < first_lat, first_lat - 1 - j, nc - 1 - (j - first_lat))
        else:
            ci = j
        idx = pl.ds(pl.multiple_of(ci * q, q), q)
        dt_e = per_head(dt_s[idx, :], d)
        xc = xc_s[idx, :]
        bc = bc_s[idx, :]
        cc = cc_s[idx, :]
        cb = _dot_nt(cc, bc)
        yield
        cum_e = _dot_sel(tri, dt_e * a_e[d])
        dtx = xc * dt_e
        yield
        cum_rows = _dot_sel(head_rows, cum_e, _NT)
        tot_e = cum_e[last:last + 1, :]
        states = _dot(bct_s[ci], dtx * jnp.exp(tot_e - cum_e))
        yield
        y = None
        for r in range(4):
            seg = cum_e[:, r * 64:r * 64 + 1] - cum_rows[r:r + 1, :]
            m = cb * jnp.exp(jnp.where(causal, seg, NEG))
            yr = _dot(m, jnp.where(lane256 == r, dtx, 0.0))
            y = yr if y is None else y + yr
            yield
        h = state_in[0]
        o_ref[0, idx, :] += y + jnp.exp(cum_e) * _dot(cc, h)
        state_out[0] = jnp.exp(tot_e) * h + states

    def step(positions, hs):
        cells = [[[hs[d]]] + [[None] for _ in positions] for d in (0, 1)]
        _interleave([chain(d, p, cells[d][k], cells[d][k + 1]) for k, p in enumerate(positions) for d in (0, 1)])
        return cells[0][-1][0], cells[1][-1][0]

    h0 = jnp.zeros((128, 256), F32)
    hs = lax.fori_loop(0, nc // 2, lambda j, hs: step((2 * j, 2 * j + 1), hs), (h0, h0))
    if nc % 2:
        step((nc - 1,), hs)


def _ssd(proj, lanes, conv_w, conv_b, dt_bias, a_log, d_skip, cols):
    bsz, L, _ = proj.shape
    nc = L // CHUNK
    cx, cbm, ccm, cdt = cols
    nh = d_skip.shape[0]
    gcols = np.asarray(_ssd_dt_columns(nh))
    dtb = jnp.zeros((gcols.shape[0], 1, 128), F32).at[:, 0, :gcols.shape[1]].set(dt_bias.reshape(-1)[gcols])
    alog = jnp.zeros((gcols.shape[0], 1, 128), F32).at[:, 0, :gcols.shape[1]].set(a_log.reshape(-1)[gcols])
    dsk = jnp.repeat(d_skip.astype(F32), 64).reshape(1, nh * 64)
    cw_specs = []
    for width, off in ((256, 0), (128, 512), (128, 768)):
        cw_specs += [pl.BlockSpec((CONV_W, width), lambda b, g, off=off, width=width: (0, off // width + g)),
                     pl.BlockSpec((1, width), lambda b, g, off=off, width=width: (0, off // width + g))]
    return pl.pallas_call(
        functools.partial(_ssd_kernel, nc=nc),
        grid=(bsz, 2),
        in_specs=[pl.BlockSpec((1, L, 256), lambda b, g: (b, 0, cx // 256 + g)),
                  pl.BlockSpec((1, L, 128), lambda b, g: (b, 0, cbm // 128 + g)),
                  pl.BlockSpec((1, L, 128), lambda b, g: (b, 0, ccm // 128 + g)),
                  pl.BlockSpec((1, L, 128), lambda b, g: (b, 0, cdt // 128 + g))] + cw_specs + [
                  pl.BlockSpec((1, 1, 128), lambda b, g: (g, 0, 0)),
                  pl.BlockSpec((1, 1, 128), lambda b, g: (g, 0, 0)),
                  pl.BlockSpec((1, 256), lambda b, g: (0, g))],
        out_specs=pl.BlockSpec((1, L, 256), lambda b, g: (b, 0, g)),
        out_shape=jax.ShapeDtypeStruct((bsz, L, nh * 64), F32),
        scratch_shapes=[pltpu.VMEM((L, 256), F32), pltpu.VMEM((L, 128), F32), pltpu.VMEM((L, 128), F32),
                        pltpu.VMEM((L, 128), F32), pltpu.VMEM((nc, 128, CHUNK), BF16)],
        compiler_params=_cp(("parallel", "parallel")), name="ssd",
    )(proj, proj, proj, lanes, conv_w, conv_b.reshape(1, -1), conv_w, conv_b.reshape(1, -1),
      conv_w, conv_b.reshape(1, -1), dtb, alog, dsk)


ML_HP = 2


def _ml_gate_lanes(hh, d):
    li = (hh * 2 + d) * 2
    return li, li + 1


def _ml_gate_columns(nh):
    cols = []
    for hp in range(nh // ML_HP):
        blk = [None] * (4 * ML_HP)
        for hh in range(ML_HP):
            for d in range(2):
                for t, lane in enumerate(_ml_gate_lanes(hh, d)):
                    blk[lane] = (d * 2 + t) * nh + hp * ML_HP + hh
        cols.append(blk)
    return cols


def _mlstm_kernel(q_ref, k_ref, v_ref, g_ref, cwq_ref, cbq_ref, cwk_ref, cbk_ref, gb_ref, o_ref,
                  qc_s, kc_s, kct_s, *, nc, dh):
    q = CHUNK
    first_lat = TM // q

    def prep(i, _):
        t0 = pl.multiple_of(i * q, q)
        idx = pl.ds(t0, q)
        for src, cw, cb, dst, mul in ((q_ref, cwq_ref, cbq_ref, qc_s, 1.0), (k_ref, cwk_ref, cbk_ref, kc_s, dh ** -0.5)):
            dst[idx, :] = _silu(_conv_tile(src, i, nc, cw[...], cb[...], q)) * mul
        for hh in range(ML_HP):
            kct_s[i, hh * dh:(hh + 1) * dh, :] = kc_s[idx, hh * dh:(hh + 1) * dh].T.astype(BF16)
        return 0
    lax.fori_loop(0, nc, prep, 0)

    gb = gb_ref[0]
    lane = lax.broadcasted_iota(jnp.int32, (q, 128), 1)
    eye8 = (lax.broadcasted_iota(jnp.int32, (8, 128), 0) == lax.broadcasted_iota(jnp.int32, (8, 128), 1)).astype(F32)
    o_ref[...] = jnp.zeros_like(o_ref)
    heads = range(ML_HP)
    assert q % dh == 0
    wide = lambda col: jnp.concatenate([col] * (q // dh), axis=1)

    def direction(d, j, states):
        rev = d == 1
        causal = _tri(q, rev)
        last = 0 if rev else q - 1
        if rev:
            ci = jnp.where(j < first_lat, first_lat - 1 - j, nc - 1 - (j - first_lat))
        else:
            ci = j
        idx = pl.ds(pl.multiple_of(ci * q, q), q)
        g = g_ref[0, idx, :] + gb
        logf = jnp.minimum(g, 0.0) - jnp.log(1.0 + jnp.exp(-jnp.abs(g)))
        is_f = functools.reduce(jnp.logical_or, [lane == _ml_gate_lanes(hh, d)[1] for hh in heads])
        cum = _dot_sel(jnp.where(causal, 1.0, 0.0), jnp.where(is_f, logf, 0.0))
        qc = [qc_s[idx, hh * dh:(hh + 1) * dh] for hh in heads]
        kc = [kc_s[idx, hh * dh:(hh + 1) * dh] for hh in heads]
        vc = [v_ref[0, idx, hh * dh:(hh + 1) * dh] for hh in heads]
        kct = [kct_s[ci, hh * dh:(hh + 1) * dh, :] for hh in heads]
        qk = [_dot(qc[hh], kct[hh]) for hh in heads]
        inter_c = [_dot(qc[hh], states[hh][0]) for hh in heads]
        yield
        rows = _dot_sel(eye8, jnp.where(is_f, cum, g), _NT)
        yield
        w, w_s, g_in, m_t, keep, m_new, wv, upd = [], [], [], [], [], [], [], []
        for hh in heads:
            li, lf = _ml_gate_lanes(hh, d)
            m_prev = states[hh][2]
            i_col = jnp.broadcast_to(g[:, li:li + 1], (q, dh))
            b_col = jnp.broadcast_to(cum[:, lf:lf + 1], (q, dh))
            i_row, b_row = rows[li:li + 1, :], rows[lf:lf + 1, :]
            dlog = jnp.where(causal, wide(b_col) - b_row + i_row, NEG)
            inter = b_col + m_prev
            m_t.append(jnp.maximum(jnp.max(dlog, axis=-1, keepdims=True), inter))
            w.append(jnp.exp(dlog - wide(m_t[hh])) * qk[hh])
            g_in.append(jnp.exp(inter - m_t[hh]))
            b_end = b_col[last:last + 1, :]
            g_s = b_end - b_col + i_col
            m_new.append(jnp.maximum(jnp.max(g_s, axis=0, keepdims=True), b_end + m_prev))
            w_s.append(jnp.exp(g_s - m_new[hh]))
            keep.append(jnp.exp(b_end + m_prev - m_new[hh]))
            wv.append(_dot(w[hh], vc[hh]))
            upd.append(_dot(kct[hh], w_s[hh] * vc[hh]))
        yield
        new_states = []
        for hh in heads:
            c_prev, n_prev, _ = states[hh]
            num = wv[hh] + g_in[hh] * inter_c[hh]
            den = (jnp.sum(w[hh], axis=-1, keepdims=True)
                   + g_in[hh] * jnp.sum(qc[hh] * n_prev, axis=-1, keepdims=True))
            o_ref[0, idx, hh * dh:(hh + 1) * dh] += num / jnp.maximum(jnp.abs(den), jnp.exp(-m_t[hh]))
            new_states.append((keep[hh] * c_prev + upd[hh],
                               keep[hh] * n_prev + jnp.sum(w_s[hh] * kc[hh], axis=0, keepdims=True), m_new[hh]))
        return tuple(new_states)

    def chunk(j, states):
        return tuple(_interleave([direction(d, j, states[d]) for d in (0, 1)]))
    init = (jnp.zeros((dh, dh), F32), jnp.zeros((1, dh), F32), jnp.full((1, dh), NEG, F32))
    lax.fori_loop(0, nc, chunk, tuple(tuple(init for _ in heads) for _ in (0, 1)))


def _mlstm(proj, lanes, conv_w, conv_b, gate_b, cols):
    bsz, L, _ = proj.shape
    nc = L // CHUNK
    cq, ck, cv, cg = cols
    nh = gate_b.shape[-1]
    dh = conv_w.shape[1] // (2 * nh)
    bw = ML_HP * dh
    gcols = np.asarray(_ml_gate_columns(nh))
    gbl = jnp.zeros((nh // ML_HP, 1, 128), F32).at[:, 0, :gcols.shape[1]].set(gate_b.reshape(-1)[gcols])
    return pl.pallas_call(
        functools.partial(_mlstm_kernel, nc=nc, dh=dh),
        grid=(bsz, nh // ML_HP),
        in_specs=[pl.BlockSpec((1, L, bw), lambda b, h: (b, 0, cq // bw + h)),
                  pl.BlockSpec((1, L, bw), lambda b, h: (b, 0, ck // bw + h)),
                  pl.BlockSpec((1, L, bw), lambda b, h: (b, 0, cv // bw + h)),
                  pl.BlockSpec((1, L, 128), lambda b, h: (b, 0, cg // 128 + h)),
                  pl.BlockSpec((CONV_W, bw), lambda b, h: (0, h)),
                  pl.BlockSpec((1, bw), lambda b, h: (0, h)),
                  pl.BlockSpec((CONV_W, bw), lambda b, h: (0, nh // ML_HP + h)),
                  pl.BlockSpec((1, bw), lambda b, h: (0, nh // ML_HP + h)),
                  pl.BlockSpec((1, 1, 128), lambda b, h: (h, 0, 0))],
        out_specs=pl.BlockSpec((1, L, bw), lambda b, h: (b, 0, h)),
        out_shape=jax.ShapeDtypeStruct((bsz, L, nh * dh), F32),
        scratch_shapes=[pltpu.VMEM((L, bw), F32), pltpu.VMEM((L, bw), F32), pltpu.VMEM((nc, bw, CHUNK), BF16)],
        compiler_params=_cp(("parallel", "parallel")), name="mlstm",
    )(proj, proj, proj, lanes, conv_w, conv_b.reshape(1, -1), conv_w, conv_b.reshape(1, -1), gbl)


def kernel(x, c, ctx, c_ctx, mod_w, mod_b, norm1_g, norm2_g, even_w_in, even_w_out, lru_conv_w, lru_conv_b, lru_wa, lru_ba, lru_wx, lru_bx, lru_lam, na_q_g, na_k_g, na_rpb, odd_w_in, odd_w_out, ssd_conv_w, ssd_conv_b, ssd_dt_bias, ssd_a_log, ssd_d, ssd_norm_g, ml_conv_w, ml_conv_b, ml_gate_b, ml_norm_g, moe_router_g, moe_router_e, moe_w1, moe_w3, moe_w2):
    bsz, S, d = x.shape
    lc = ctx.shape[1]
    assert lc == TM and S % TM == 0 and mod_w.shape[0] == 2
    nt = (lc + S) // TM
    L = lc + S

    cc = jnp.zeros((8, d), F32).at[0].set(c_ctx).at[1:1 + bsz].set(c)
    mod_all = _modulation(cc, mod_w, mod_b)

    def mod_for(l):
        m = mod_all[l].reshape(8, 6, d)
        return jnp.stack([jnp.broadcast_to(m[0], (bsz, 6, d)), m[1:1 + bsz]], axis=1)

    def router_w(l):
        w = jnp.zeros((d, 128), F32).at[:, :EXPERT_LANE0].set(moe_router_g[l]) \
            .at[:, EXPERT_LANE0:EXPERT_LANE0 + N_EXPERTS].set(moe_router_e[l])
        hi = w.astype(BF16)
        return jnp.concatenate([hi, (w - hi.astype(F32)).astype(BF16)], axis=1)

    xx = (ctx, x)

    mod0 = mod_for(0)
    (proj,) = _inproj(xx, mod0, norm1_g[0], even_w_in[0].astype(BF16))
    lw = lru_conv_w.shape[-1]
    r = _lru(proj, lru_conv_w[0], lru_conv_b[0], lru_wa[0], lru_ba[0], lru_wx[0], lru_bx[0], lru_lam[0])
    a = _na(proj, na_q_g[0], na_k_g[0], na_rpb[0], col0=2 * lw)
    x1, h2, gate = _outproj([(r, 0, lw), (proj, 1, lw), (a, 0, lw)], xx, even_w_out[0].astype(BF16), mod0,
                            norm2_g[0], router_w(0), even=True, tile0=0, ntiles=nt)
    ew = (moe_w1.astype(BF16), moe_w3.astype(BF16), moe_w2.astype(BF16))
    moe0 = _moe(h2.reshape(bsz * L, d), gate.reshape(bsz * L, 128), 0, *ew).reshape(bsz, L, d)

    mod1 = mod_for(1)
    sw = ssd_d.shape[-1] * 64
    xbc = ssd_conv_w.shape[-1]
    mw = ml_norm_g.shape[-1]
    w = odd_w_in[0]
    o = np.cumsum([0, sw, xbc, 2 * ssd_d.shape[-1], mw, mw, mw, mw])
    def lane_blocks(col0, groups):
        return [jnp.concatenate([w[:, col0 + np.asarray(blk)], jnp.zeros((d, 128 - len(blk)), F32)], axis=1)
                for blk in groups]
    dt_blks = lane_blocks(o[2], _ssd_dt_columns(ssd_d.shape[-1]))
    gate_blks = lane_blocks(o[7], _ml_gate_columns(ml_gate_b.shape[-1]))
    w_odd = jnp.concatenate([w[:, :o[2]], w[:, o[3]:o[7]]] + dt_blks + gate_blks, axis=1).astype(BF16)
    cz, cxs = 0, sw
    cB, cC = cxs + sw, cxs + sw + (xbc - sw) // 2
    cq = sw + xbc
    ck, cv, co, csm = cq + mw, cq + 2 * mw, cq + 3 * mw, cq + 4 * mw
    x0, proj1, lanes1 = _inproj(x1, mod1, norm1_g[1], w_odd, prev=(moe0, mod0), n_f32=w_odd.shape[1] - csm)
    ys = _ssd(proj1, lanes1, ssd_conv_w[0], ssd_conv_b[0], ssd_dt_bias[0], ssd_a_log[0], ssd_d[0], (cxs, cB, cC, 0))
    hm = _mlstm(proj1, lanes1, ml_conv_w[0], ml_conv_b[0], ml_gate_b[0], (cq, ck, cv, 128 * len(dt_blks)))
    x2, h2b, gate1 = _outproj([(ys, 0, sw), (proj1, cz // sw, sw), (hm, 0, mw), (proj1, co // mw, mw),
                               (ssd_norm_g[0].reshape(1, sw), 0, sw), (ml_norm_g[0].reshape(1, mw), 0, mw)],
                              x0, odd_w_out[0].astype(BF16), mod1, norm2_g[1], router_w(1),
                              even=False, tile0=1, ntiles=nt - 1)
    out = _moe(h2b.reshape(bsz * S, d), gate1.reshape(bsz * S, 128), 1, *ew, resid=(x2.reshape(bsz * S, d), mod1))
    return out.reshape(bsz, S, d)
```
